```python
import math
import jax
import jax.numpy as jnp
from jax import lax
import numpy as np

D_MODEL = 1024
BATCH = 2
SEQ = 16384
DEPTH = 2

CTX_LEN = 256
GRID_W = 64

N_BRANCH = 4
BRANCH_W = D_MODEL // N_BRANCH

MLA_HEADS = 4
MLA_NOPE = 64
MLA_ROPE = 32
MLA_V = 64
MLA_Q_LORA = 256
MLA_KV_LORA = 128
ML_HEADS = 4
ML_DH = 64
ML_CONV = 3
ML_CHUNK = 64
GLA_HEADS = 4
GLA_DK = 32
GLA_DV = 64
GLA_RANK = 16
GLA_TAU = 16.0
GLA_CHUNK = 64
DIFF_HEADS = 4
DIFF_DQK = 32
DIFF_DV = 64
ROPE_DIM = 32
ROPE_BASE = 10000.0
Q_BLOCK = 128
N_EXPERTS = 16
EC_CAPACITY = 2
EXPERT_FF = 1408

NEG_BIG = -1e30
EPS = 1e-6

MLA_COLS = (MLA_Q_LORA, MLA_KV_LORA, MLA_ROPE)
ML_COLS = (2 * ML_HEADS * ML_DH, ML_HEADS * ML_DH, ML_HEADS * ML_DH, 2 * ML_HEADS, 2 * ML_HEADS)
GLA_COLS = (GLA_HEADS * GLA_DK, GLA_HEADS * GLA_DK, GLA_HEADS * GLA_DV, GLA_HEADS * GLA_DV, 2 * GLA_RANK)
DIFF_COLS = (DIFF_HEADS * 2 * DIFF_DQK, DIFF_HEADS * 2 * DIFF_DQK, DIFF_HEADS * DIFF_DV)
MIXER_COLS = (sum(MLA_COLS), sum(ML_COLS), sum(GLA_COLS), sum(DIFF_COLS))
N_IN = sum(MIXER_COLS)

kernel_name = 'hybrid_mla_mlstm_gla_diffattn_ecmoe_block'


def _split(z, sizes):
    return jnp.split(z, np.cumsum(sizes)[:-1].tolist(), axis=-1)


def _chunk(a, size):
    return a.reshape((a.shape[0], a.shape[1] // size, size) + a.shape[2:])


def _flip(a, direction):
    return a if (a is None or direction == 0) else jnp.flip(a, axis=1)


def rms_norm(x, g):
    xf = x.astype(jnp.float32)
    y = xf * lax.rsqrt(jnp.mean(xf * xf, axis=-1, keepdims=True) + EPS)
    return (y * g.astype(jnp.float32)).astype(x.dtype)


def head_rms(x, g, n_heads):
    shp = x.shape
    xh = x.reshape(shp[:-1] + (n_heads, shp[-1] // n_heads))
    return rms_norm(xh, g.reshape(n_heads, -1)).reshape(shp)


def modulate(x, g, shift, scale):
    return rms_norm(x, g) * (1 + scale[:, None]) + shift[:, None]


def axial_rope_tables(rows):
    nf = ROPE_DIM // 4
    row = jnp.repeat(jnp.arange(rows), GRID_W)
    col = jnp.tile(jnp.arange(GRID_W), rows)
    inv_freq = ROPE_BASE ** (-jnp.arange(nf, dtype=jnp.float32) / nf)
    ang = jnp.stack([row, col], axis=-1).astype(jnp.float32)[..., None] * inv_freq
    return jnp.cos(ang), jnp.sin(ang)


def apply_rope(x, rope):
    cos, sin = rope
    nf = ROPE_DIM // 4
    shp = x.shape
    xr = x.astype(jnp.float32).reshape(shp[:-1] + (2, 2, nf))
    x1, x2 = xr[..., 0, :], xr[..., 1, :]
    bshape = cos.shape[:1] + (1,) * (x.ndim - 3) + cos.shape[1:]
    cb, sb = cos.reshape(bshape), sin.reshape(bshape)
    out = jnp.stack([x1 * cb - x2 * sb, x2 * cb + x1 * sb], axis=-2)
    return out.reshape(shp).astype(x.dtype)


def sweep_query_blocks(q, block_fn):
    b, t = q.shape[0], q.shape[1]
    qb = q.reshape((b, t // Q_BLOCK, Q_BLOCK) + q.shape[2:]).swapaxes(0, 1)
    out = lax.map(block_fn, qb).swapaxes(0, 1)
    return out.reshape((b, t) + out.shape[3:])


def mla_attention(q, k, v):
    scale = (MLA_NOPE + MLA_ROPE) ** -0.5

    def blk(qb):
        s = jnp.einsum('bqhd,bkhd->bhqk', qb, k).astype(jnp.float32) * scale
        p = jax.nn.softmax(s, axis=-1).astype(v.dtype)
        return jnp.einsum('bhqk,bkhd->bqhd', p, v)

    return sweep_query_blocks(q, blk)


def diff_attention(q, k, v, lam):
    scale = DIFF_DQK ** -0.5

    def blk(qb):
        s = jnp.einsum('bqhmd,bkhmd->bhmqk', qb, k).astype(jnp.float32) * scale
        p = jax.nn.softmax(s, axis=-1)
        a = (p[:, :, 0] - lam * p[:, :, 1]).astype(v.dtype)
        return jnp.einsum('bhqk,bkhd->bqhd', a, v)

    return sweep_query_blocks(q, blk)


def short_conv(x, w, b):
    pad = ML_CONV // 2
    t = x.shape[1]
    xp = jnp.pad(x, ((0, 0), (pad, pad), (0, 0)))
    return sum(xp[:, j:j + t] * w[j] for j in range(ML_CONV)) + b


def mlstm_dir(q, k, v, ig, lf, init, want_out):
    kc, vc, igc, lfc = (_chunk(a, ML_CHUNK) for a in (k, v, ig, lf))
    bcum = jnp.cumsum(lfc, axis=2)
    b_end = bcum[:, :, -1]
    w_log = b_end[:, :, None] - bcum + igc
    m_loc = jnp.max(w_log, axis=2)
    w = jnp.exp(w_log - m_loc[:, :, None])
    c_loc = jnp.einsum('bclh,bclhv,bclhk->bchvk', w, vc, kc)
    n_loc = jnp.einsum('bclh,bclhk->bchk', w, kc)

    def step(carry, inp):
        c_st, n_st, m_st = carry
        be, ml, cl, nl = inp
        m_new = jnp.maximum(be + m_st, ml)
        a = jnp.exp(be + m_st - m_new)
        g = jnp.exp(ml - m_new)
        c_new = a[..., None, None] * c_st + g[..., None, None] * cl
        n_new = a[..., None] * n_st + g[..., None] * nl
        return (c_new, n_new, m_new), (c_st, n_st, m_st)

    xs = tuple(jnp.moveaxis(a, 1, 0) for a in (b_end, m_loc, c_loc, n_loc))
    final, starts = lax.scan(step, init, xs)
    if not want_out:
        return None, final
    c0, n0, m0 = (jnp.moveaxis(a, 0, 1) for a in starts)
    qc = _chunk(q, ML_CHUNK)
    causal = jnp.tril(jnp.ones((ML_CHUNK, ML_CHUNK), dtype=bool))
    bt = jnp.swapaxes(bcum, 2, 3)
    it = jnp.swapaxes(igc, 2, 3)
    d_log = jnp.where(causal, bt[..., :, None] - bt[..., None, :] + it[..., None, :], -jnp.inf)
    inter = bt + m0[..., None]
    m_t = jnp.maximum(inter, jnp.max(d_log, axis=-1))
    s = jnp.exp(d_log - m_t[..., None]) * jnp.einsum('bclhd,bcshd->bchls', qc, kc)
    a_int = jnp.exp(inter - m_t)
    num = jnp.einsum('bchls,bcshv->bclhv', s, vc) + jnp.einsum('bchl,bclhk,bchvk->bclhv', a_int, qc, c0)
    den = jnp.sum(s, axis=-1) + a_int * jnp.einsum('bclhk,bchk->bchl', qc, n0)
    h = num / jnp.swapaxes(jnp.maximum(jnp.abs(den), jnp.exp(-m_t)), 2, 3)[..., None]
    return h.reshape(v.shape), final


def gla_dir(q, k, v, lg, init, want_out):
    kc, vc, gc = (_chunk(a, GLA_CHUNK) for a in (k, v, lg))
    gcum = jnp.cumsum(gc, axis=2)
    g_end = gcum[:, :, -1]
    s_loc = jnp.einsum('bclhk,bclhv->bchkv', kc * jnp.exp(g_end[:, :, None] - gcum), vc)

    def step(s_st, inp):
        ge, sl = inp
        return jnp.exp(ge)[..., None] * s_st + sl, s_st

    final, starts = lax.scan(step, init, (jnp.moveaxis(g_end, 1, 0), jnp.moveaxis(s_loc, 1, 0)))
    if not want_out:
        return None, final
    s0 = jnp.moveaxis(starts, 0, 1)
    causal = jnp.tril(jnp.ones((GLA_CHUNK, GLA_CHUNK), dtype=bool))
    q_dec = _chunk(q, GLA_CHUNK) * jnp.exp(gcum)
    att = jnp.einsum('bclhk,bcshk->bchls', q_dec, kc * jnp.exp(-gcum))
    att = jnp.where(causal, att, 0.0)
    o = jnp.einsum('bchls,bcshv->bclhv', att, vc) + jnp.einsum('bclhk,bchkv->bclhv', q_dec, s0)
    return o.reshape(v.shape), final


def run_bidirectional(dir_fn, ctx_inputs, lat_inputs, init, need_ctx):
    out_c, out_l = None, None
    for d in range(2):
        h_c, fin_c = dir_fn(*[_flip(a, d) for a in ctx_inputs[d]], init, need_ctx)
        h_l, _ = dir_fn(*[_flip(a, d) for a in lat_inputs[d]], fin_c, True)
        h_l = _flip(h_l, d)
        out_l = h_l if out_l is None else out_l + h_l
        if need_ctx:
            h_c = _flip(h_c, d)
            out_c = h_c if out_c is None else out_c + h_c
    return out_c, out_l


def mla_branch(zc, zl, rope, g_q_lat, w_uq, g_kv_lat, w_ukv, need_ctx):
    def queries(z, rot):
        q_lat = _split(z, MLA_COLS)[0]
        q = (rms_norm(q_lat, g_q_lat) @ w_uq).reshape(z.shape[:2] + (MLA_HEADS, MLA_NOPE + MLA_ROPE))
        q_nope, q_rope = q[..., :MLA_NOPE], q[..., MLA_NOPE:]
        if rot is not None:
            q_rope = apply_rope(q_rope, rot)
        return jnp.concatenate([q_nope, q_rope], axis=-1)

    def keys_values(z, rot):
        _, kv_lat, k_rope = _split(z, MLA_COLS)
        kv = (rms_norm(kv_lat, g_kv_lat) @ w_ukv).reshape(z.shape[:2] + (MLA_HEADS, MLA_NOPE + MLA_V))
        k_nope, v = kv[..., :MLA_NOPE], kv[..., MLA_NOPE:]
        k_rope = k_rope[:, :, None, :]
        if rot is not None:
            k_rope = apply_rope(k_rope, rot)
        k = jnp.concatenate([k_nope, jnp.broadcast_to(k_rope, k_nope.shape[:-1] + (MLA_ROPE,))], axis=-1)
        return k, v

    def flat(y):
        return y.reshape(y.shape[:2] + (MLA_HEADS * MLA_V,))

    k_c, v_c = keys_values(zc, None)
    k_l, v_l = keys_values(zl, rope)
    y_l = mla_attention(queries(zl, rope), jnp.concatenate([k_c, k_l], axis=1), jnp.concatenate([v_c, v_l], axis=1))
    y_c = flat(mla_attention(queries(zc, None), k_c, v_c)) if need_ctx else None
    return y_c, flat(y_l)


def mlstm_branch(zc, zl, w_conv, b_conv, b_igate, b_fgate, g_out, need_ctx):
    def prep(z):
        bsz, t = z.shape[:2]
        hd = (bsz, t, ML_HEADS, ML_DH)
        qk, v, o, ig, fg = _split(z, ML_COLS)
        qk = jax.nn.silu(short_conv(qk, w_conv, b_conv)).astype(jnp.float32)
        q = qk[..., :ML_HEADS * ML_DH].reshape(hd)
        k = qk[..., ML_HEADS * ML_DH:].reshape(hd) * ML_DH ** -0.5
        v = v.astype(jnp.float32).reshape(hd)
        ig = ig.astype(jnp.float32).reshape(bsz, t, 2, ML_HEADS) + b_igate
        lf = jax.nn.log_sigmoid(fg.astype(jnp.float32).reshape(bsz, t, 2, ML_HEADS) + b_fgate)
        return q, k, v, o, ig, lf

    q_c, k_c, v_c, o_c, ig_c, lf_c = prep(zc)
    q_l, k_l, v_l, o_l, ig_l, lf_l = prep(zl)
    bsz = zl.shape[0]
    init = (jnp.zeros((bsz, ML_HEADS, ML_DH, ML_DH), jnp.float32),
            jnp.zeros((bsz, ML_HEADS, ML_DH), jnp.float32),
            jnp.full((bsz, ML_HEADS), NEG_BIG, jnp.float32))
    ctx_in = [(q_c if need_ctx else None, k_c, v_c, ig_c[:, :, d], lf_c[:, :, d]) for d in range(2)]
    lat_in = [(q_l, k_l, v_l, ig_l[:, :, d], lf_l[:, :, d]) for d in range(2)]
    h_c, h_l = run_bidirectional(mlstm_dir, ctx_in, lat_in, init, need_ctx)

    def finish(h, o):
        return jax.nn.sigmoid(o) * head_rms(h.reshape(o.shape).astype(o.dtype), g_out, ML_HEADS)

    return (finish(h_c, o_c) if need_ctx else None), finish(h_l, o_l)


def gla_branch(zc, zl, w_alpha2, b_alpha, g_out, need_ctx):
    def prep(z):
        bsz, t = z.shape[:2]
        q, k, v, r, a = _split(z, GLA_COLS)
        q = q.astype(jnp.float32).reshape(bsz, t, GLA_HEADS, GLA_DK) * GLA_DK ** -0.5
        k = k.astype(jnp.float32).reshape(bsz, t, GLA_HEADS, GLA_DK)
        v = v.astype(jnp.float32).reshape(bsz, t, GLA_HEADS, GLA_DV)
        zg = jnp.einsum('btnr,nrk->btnk', a.reshape(bsz, t, 2, GLA_RANK), w_alpha2) + b_alpha
        lg = (jax.nn.log_sigmoid(zg.astype(jnp.float32)) / GLA_TAU).reshape(bsz, t, 2, GLA_HEADS, GLA_DK)
        return q, k, v, r, lg

    q_c, k_c, v_c, r_c, lg_c = prep(zc)
    q_l, k_l, v_l, r_l, lg_l = prep(zl)
    init = jnp.zeros((zl.shape[0], GLA_HEADS, GLA_DK, GLA_DV), jnp.float32)
    ctx_in = [(q_c if need_ctx else None, k_c, v_c, lg_c[:, :, d]) for d in range(2)]
    lat_in = [(q_l, k_l, v_l, lg_l[:, :, d]) for d in range(2)]
    o_c, o_l = run_bidirectional(gla_dir, ctx_in, lat_in, init, need_ctx)

    def finish(o, r):
        return jax.nn.silu(r) * head_rms(o.reshape(r.shape).astype(r.dtype), g_out, GLA_HEADS)

    return (finish(o_c, r_c) if need_ctx else None), finish(o_l, r_l)


def diff_branch(zc, zl, rope, diff_lambda, g_out, lam_init, need_ctx):
    lv = diff_lambda.astype(jnp.float32)
    lam = jnp.exp(jnp.sum(lv[0] * lv[1])) - jnp.exp(jnp.sum(lv[2] * lv[3])) + lam_init

    def qkv(z, rot):
        bsz, t = z.shape[:2]
        q, k, v = _split(z, DIFF_COLS)
        qk_shape = (bsz, t, DIFF_HEADS, 2, DIFF_DQK)
        q, k = q.reshape(qk_shape), k.reshape(qk_shape)
        if rot is not None:
            q, k = apply_rope(q, rot), apply_rope(k, rot)
        return q, k, v.reshape(bsz, t, DIFF_HEADS, DIFF_DV)

    def finish(o):
        o = o.reshape(o.shape[:2] + (DIFF_HEADS * DIFF_DV,))
        return head_rms(o, g_out, DIFF_HEADS) * (1.0 - lam_init)

    q_c, k_c, v_c = qkv(zc, None)
    q_l, k_l, v_l = qkv(zl, rope)
    y_l = finish(diff_attention(q_l, jnp.concatenate([k_c, k_l], axis=1), jnp.concatenate([v_c, v_l], axis=1), lam))
    y_c = finish(diff_attention(q_c, k_c, v_c, lam)) if need_ctx else None
    return y_c, y_l


def merge_branches(h, ys, w_branch, w_gate, b_gate, w_out):
    y = jnp.stack(ys, axis=2)
    p = jnp.einsum('btnc,ncd->btnd', y, w_branch)
    g = jax.nn.sigmoid(h @ w_gate + b_gate).reshape(p.shape)
    return jnp.sum(g * p, axis=2) @ w_out


def ec_moe(h, w_router, w_e_gate, w_e_up, w_e_down):
    b, t, d = h.shape
    cap = EC_CAPACITY * t // N_EXPERTS
    aff = jax.nn.softmax((h @ w_router).astype(jnp.float32), axis=-1)
    vals, idx = lax.top_k(jnp.swapaxes(aff, 1, 2), cap)
    xs = jax.vmap(lambda hb, ib: hb[ib])(h, idx)
    hid = jax.nn.silu(jnp.einsum('becd,edf->becf', xs, w_e_gate)) * jnp.einsum('becd,edf->becf', xs, w_e_up)
    ys = jnp.einsum('becf,efd->becd', hid, w_e_down) * vals[..., None].astype(h.dtype)
    return jax.vmap(lambda yb, ib: jnp.zeros((t, d), yb.dtype).at[ib.reshape(-1)].add(yb.reshape(-1, d)))(ys, idx)


def hybrid_layer(x_c, x_l, silu_c, silu_cc, rope, layer_idx, need_ctx,
                 w_ada, b_ada, g_mix_pre, g_mix_post, g_ffn_pre, g_ffn_post, w_in,
                 g_q_lat, w_uq, g_kv_lat, w_ukv,
                 w_conv, b_conv, b_igate, b_fgate, g_mlstm_out,
                 w_alpha2, b_alpha, g_gla_out,
                 diff_lambda, g_diff_out,
                 w_branch, w_gate, b_gate, w_out,
                 w_router, w_e_gate, w_e_up, w_e_down):
    lam_init = 0.8 - 0.6 * math.exp(-0.3 * layer_idx)
    mod_l = jnp.split(silu_c @ w_ada + b_ada, 6, axis=-1)
    mod_c = jnp.split(silu_cc @ w_ada + b_ada, 6, axis=-1)

    h_l = modulate(x_l, g_mix_pre, mod_l[0], mod_l[1])
    h_c = modulate(x_c, g_mix_pre, mod_c[0], mod_c[1])
    z_l = _split(h_l @ w_in, MIXER_COLS)
    z_c = _split(h_c @ w_in, MIXER_COLS)
    a_c, a_l = mla_branch(z_c[0], z_l[0], rope, g_q_lat, w_uq, g_kv_lat, w_ukv, need_ctx)
    b_c, b_l = mlstm_branch(z_c[1], z_l[1], w_conv, b_conv, b_igate, b_fgate, g_mlstm_out, need_ctx)
    c_c, c_l = gla_branch(z_c[2], z_l[2], w_alpha2, b_alpha, g_gla_out, need_ctx)
    d_c, d_l = diff_branch(z_c[3], z_l[3], rope, diff_lambda, g_diff_out, lam_init, need_ctx)
    y_l = merge_branches(h_l, (a_l, b_l, c_l, d_l), w_branch, w_gate, b_gate, w_out)
    x_l = x_l + mod_l[2][:, None] * rms_norm(y_l, g_mix_post)

    f_l = ec_moe(modulate(x_l, g_ffn_pre, mod_l[3], mod_l[4]), w_router, w_e_gate, w_e_up, w_e_down)
    x_l = x_l + mod_l[5][:, None] * rms_norm(f_l, g_ffn_post)

    if need_ctx:
        y_c = merge_branches(h_c, (a_c, b_c, c_c, d_c), w_branch, w_gate, b_gate, w_out)
        x_c = x_c + mod_c[2][:, None] * rms_norm(y_c, g_mix_post)
        f_c = ec_moe(modulate(x_c, g_ffn_pre, mod_c[3], mod_c[4]), w_router, w_e_gate, w_e_up, w_e_down)
        x_c = x_c + mod_c[5][:, None] * rms_norm(f_c, g_ffn_post)
    return x_c, x_l


def setup_inputs(seed: int = 0) -> dict:
    key = jax.random.key(seed)
    ks = iter(jax.random.split(key, 40))
    L, D = DEPTH, D_MODEL

    def nrm(shape, scale):
        return jax.random.normal(next(ks), shape, jnp.float32) * scale

    def gain(shape):
        return 1.0 + nrm(shape, 0.02)

    return {
        'x': nrm((BATCH, SEQ, D), 1.0),
        'c': nrm((BATCH, D), 1.0),
        'ctx': nrm((BATCH, CTX_LEN, D), 1.0),
        'c_ctx': nrm((D,), 1.0),
        'w_ada': nrm((L, D, 6 * D), 0.5 * D ** -0.5),
        'b_ada': nrm((L, 6 * D), 0.01),
        'g_mix_pre': gain((L, D)),
        'g_mix_post': gain((L, D)),
        'g_ffn_pre': gain((L, D)),
        'g_ffn_post': gain((L, D)),
        'w_in': nrm((L, D, N_IN), D ** -0.5),
        'g_q_lat': gain((L, MLA_Q_LORA)),
        'w_uq': nrm((L, MLA_Q_LORA, MLA_HEADS * (MLA_NOPE + MLA_ROPE)), MLA_Q_LORA ** -0.5),
        'g_kv_lat': gain((L, MLA_KV_LORA)),
        'w_ukv': nrm((L, MLA_KV_LORA, MLA_HEADS * (MLA_NOPE + MLA_V)), MLA_KV_LORA ** -0.5),
        'w_conv': nrm((L, ML_CONV, 2 * ML_HEADS * ML_DH), ML_CONV ** -0.5),
        'b_conv': nrm((L, 2 * ML_HEADS * ML_DH), 0.01),
        'b_igate': nrm((L, 2, ML_HEADS), 0.1),
        'b_fgate': jnp.linspace(3.0, 6.0, ML_HEADS)[None, None] + nrm((L, 2, ML_HEADS), 0.1),
        'g_mlstm_out': gain((L, ML_HEADS * ML_DH)),
        'w_alpha2': nrm((L, 2, GLA_RANK, GLA_HEADS * GLA_DK), GLA_RANK ** -0.5),
        'b_alpha': nrm((L, 2, GLA_HEADS * GLA_DK), 0.01),
        'g_gla_out': gain((L, GLA_HEADS * GLA_DV)),
        'diff_lambda': nrm((L, 4, DIFF_DQK), 0.1),
        'g_diff_out': gain((L, DIFF_HEADS * DIFF_DV)),
        'w_branch': nrm((L, N_BRANCH, BRANCH_W, D), BRANCH_W ** -0.5),
        'w_gate': nrm((L, D, N_BRANCH * D), D ** -0.5),
        'b_gate': nrm((L, N_BRANCH * D), 0.01),
        'w_out': nrm((L, D, D), D ** -0.5),
        'w_router': nrm((L, D, N_EXPERTS), D ** -0.5),
        'w_e_gate': nrm((L, N_EXPERTS, D, EXPERT_FF), D ** -0.5),
        'w_e_up': nrm((L, N_EXPERTS, D, EXPERT_FF), D ** -0.5),
        'w_e_down': nrm((L, N_EXPERTS, EXPERT_FF, D), EXPERT_FF ** -0.5),
    }


def reference(x, c, ctx, c_ctx, w_ada, b_ada, g_mix_pre, g_mix_post, g_ffn_pre, g_ffn_post, w_in,
              g_q_lat, w_uq, g_kv_lat, w_ukv, w_conv, b_conv, b_igate, b_fgate, g_mlstm_out,
              w_alpha2, b_alpha, g_gla_out, diff_lambda, g_diff_out,
              w_branch, w_gate, b_gate, w_out, w_router, w_e_gate, w_e_up, w_e_down):
    rows = x.shape[1] // GRID_W
    rope = axial_rope_tables(rows)
    silu_c = jax.nn.silu(c)
    silu_cc = jax.nn.silu(c_ctx)[None]
    x_c, x_l = ctx, x
    for i in range(DEPTH):
        x_c, x_l = hybrid_layer(
            x_c, x_l, silu_c, silu_cc, rope, i, i < DEPTH - 1,
            w_ada[i], b_ada[i], g_mix_pre[i], g_mix_post[i], g_ffn_pre[i], g_ffn_post[i], w_in[i],
            g_q_lat[i], w_uq[i], g_kv_lat[i], w_ukv[i],
            w_conv[i], b_conv[i], b_igate[i], b_fgate[i], g_mlstm_out[i],
            w_alpha2[i], b_alpha[i], g_gla_out[i],
            diff_lambda[i], g_diff_out[i],
            w_branch[i], w_gate[i], b_gate[i], w_out[i],
            w_router[i], w_e_gate[i], w_e_up[i], w_e_down[i])
    return x_l
```

```python
import functools
import math

import numpy as np
import jax
import jax.numpy as jnp
from jax import lax
from jax.experimental import pallas as pl
from jax.experimental.pallas import tpu as pltpu

F32 = jnp.float32
BF16 = jnp.bfloat16
I32 = jnp.int32
HIGHEST = lax.Precision.HIGHEST

D = 1024
DEPTH = 2
GRID_W = 64
N_HEADS = 4
MLA_NOPE, MLA_ROPE, MLA_V = 64, 32, 64
MLA_Q_LORA, MLA_KV_LORA = 256, 128
ML_DH = 64
GLA_DK, GLA_DV, GLA_RANK, GLA_TAU = 32, 64, 16, 16.0
DIFF_DQK, DIFF_DV = 32, 64
ROPE_DIM, ROPE_BASE = 32, 10000.0
N_EXPERTS, EC_CAPACITY, EXPERT_FF = 16, 2, 1408
NEG = -1e30
EPS = 1e-6
LOG2E = 1.4426950408889634

LANE = 128
HEAD_SLAB = 128
TOK_BLK = 256
ML_CHUNK = 128
GLA_CHUNK = 64
GATHER_WIN = TOK_BLK + 16
VMEM_LIMIT = 56 * 1024 * 1024

ZQ, ZKV, ZKRA, ZKRB, ZMLQK, ZMLV, ZMLO, ZGATE, ZGA = 0, 256, 384, 512, 640, 1152, 1408, 1664, 1792
ZGQ, ZGK, ZGV, ZGR, ZDQ, ZDQS, ZDK, ZDKS, ZDV, NZ = 1920, 2048, 2176, 2432, 2688, 2944, 3200, 3456, 3712, 4224


def _swap32(c):
    return (c // 32) * 32 + ((c % 32) ^ 8)


def _win_index():
    idx = -np.ones((NZ,), np.int64)
    idx[ZQ:ZQ + 256] = np.arange(0, 256)
    idx[ZKV:ZKV + 128] = np.arange(256, 384)
    r = np.arange(32)
    idx[ZKRA + 64:ZKRA + 96] = 384 + r
    idx[ZKRB + 64:ZKRB + 96] = 384 + (r ^ 8)
    ml = 416
    idx[ZMLQK:ZMLQK + 512] = ml + np.arange(512)
    idx[ZMLV:ZMLV + 256] = ml + 512 + np.arange(256)
    idx[ZMLO:ZMLO + 256] = ml + 768 + np.arange(256)
    idx[ZGATE:ZGATE + 16] = ml + 1024 + np.arange(16)
    gl = 1456
    idx[ZGQ:ZGQ + 128] = gl + np.arange(128)
    idx[ZGK:ZGK + 128] = gl + 128 + np.arange(128)
    idx[ZGV:ZGV + 256] = gl + 256 + np.arange(256)
    idx[ZGR:ZGR + 256] = gl + 512 + np.arange(256)
    idx[ZGA:ZGA + 32] = gl + 768 + np.arange(32)
    df = 2256
    c = np.arange(256)
    idx[ZDQ:ZDQ + 256] = df + c
    idx[ZDQS:ZDQS + 256] = df + _swap32(c)
    idx[ZDK:ZDK + 256] = df + 256 + c
    idx[ZDKS:ZDKS + 256] = df + 256 + _swap32(c)
    for h in range(N_HEADS):
        idx[ZDV + 128 * h:ZDV + 128 * h + 64] = df + 512 + 64 * h + np.arange(64)
    return idx


_WIN_IDX = _win_index()


def _gather_cols(w, idx):
    safe = np.maximum(idx, 0)
    return jnp.where(jnp.asarray(idx >= 0)[None, :], w[:, safe], 0.0)


def _pad_heads_rows(w, width):
    n = w.shape[1]
    w4 = w.reshape(N_HEADS, width, n)
    return jnp.pad(w4, ((0, 0), (0, HEAD_SLAB - width), (0, 0))).reshape(N_HEADS * HEAD_SLAB, n)


def _cparams(sem):
    return pltpu.CompilerParams(dimension_semantics=sem, vmem_limit_bytes=VMEM_LIMIT)


def _rms(x):
    return x * lax.rsqrt(jnp.mean(x * x, axis=-1, keepdims=True) + EPS)


def _sigmoid(x):
    return 1.0 / (1.0 + jnp.exp(-x))


def _log_sigmoid(x):
    return jnp.minimum(x, 0.0) - jnp.log1p(jnp.exp(-jnp.abs(x)))


def _dot(a, b, precision=None):
    return jnp.dot(a, b, preferred_element_type=F32, precision=precision)


def _dot_nt(a, b, precision=None):
    return lax.dot_general(a, b, (((1,), (1,)), ((), ())), preferred_element_type=F32, precision=precision)


def _dot_tn(a, b, precision=None):
    return lax.dot_general(a, b, (((0,), (0,)), ((), ())), preferred_element_type=F32, precision=precision)


def _const_spec(shape):
    nd = len(shape)
    return pl.BlockSpec(shape, lambda *_: (0,) * nd)


def _ada_kernel(c_ref, w_ref, b_ref, o_ref):
    cv = c_ref[...]
    s = (cv * _sigmoid(cv)).astype(BF16)
    o_ref[...] = _dot(s, w_ref[...].astype(BF16)) + b_ref[...]


def _ada(c8, w_ada, b_ada):
    n, tn = 6 * D, 1024
    return pl.pallas_call(
        _ada_kernel, name="ada", grid=(n // tn,),
        in_specs=[pl.BlockSpec((8, D), lambda j: (0, 0)), pl.BlockSpec((D, tn), lambda j: (0, j)),
                  pl.BlockSpec((1, tn), lambda j: (0, j))],
        out_specs=pl.BlockSpec((8, tn), lambda j: (0, j)),
        out_shape=jax.ShapeDtypeStruct((8, n), F32), compiler_params=_cparams(("arbitrary",)),
    )(c8, w_ada, b_ada.reshape(1, n))


_PROJ_OUT = (
    ("mq", 512, BF16), ("mk", 512, BF16), ("mv", 512, BF16),
    ("lqk", 512, F32), ("lv", 256, BF16), ("lo", 256, BF16), ("gc", 128, F32),
    ("gq", 128, BF16), ("gk", 128, BF16), ("gv", 256, BF16), ("gr", 256, BF16), ("glg", 256, F32),
    ("dq", 1024, BF16), ("dk", 512, BF16), ("dv", 512, BF16),
)


def _proj_kernel(x_ref, mod_ref, g_ref, w_ref, tab_ref, gq_ref, wq_ref, wqs_ref, gkv_ref, wk_ref, wv_ref,
                 wgt_ref, gbr_ref, gbc_ref, wal_ref, bal_ref,
                 mq_ref, mk_ref, mv_ref, lqk_ref, lv_ref, lo_ref, gc_ref, gq_o, gk_o, gv_o, gr_o, glg_o,
                 dq_ref, dk_ref, dv_ref, grow_ref):
    x = x_ref[0]
    tm = x.shape[0]
    mod = mod_ref[0]
    h = _rms(x) * g_ref[...] * (1.0 + mod[1:2]) + mod[0:1]
    hb = h.astype(BF16)
    z = _dot(hb, w_ref[...])
    tab = tab_ref[...]
    ct, st, cd, sd = tab[:, 0:128], tab[:, 128:256], tab[:, 256:384], tab[:, 384:512]
    lane = lax.broadcasted_iota(I32, (tm, LANE), 1)

    qn = (_rms(z[:, ZQ:ZQ + 256]) * gq_ref[...]).astype(BF16)
    qa = _dot(qn, wq_ref[...])
    qb = _dot(qn, wqs_ref[...])
    qscale = (MLA_NOPE + MLA_ROPE) ** -0.5 * LOG2E
    for hh in range(N_HEADS):
        sl = slice(HEAD_SLAB * hh, HEAD_SLAB * (hh + 1))
        mq_ref[0, :, sl] = ((qa[:, sl] * ct + qb[:, sl] * st) * qscale).astype(BF16)
    kvn = (_rms(z[:, ZKV:ZKV + 128]) * gkv_ref[...]).astype(BF16)
    kk = _dot(kvn, wk_ref[...])
    vv = _dot(kvn, wv_ref[...])
    kr = z[:, ZKRA:ZKRA + 128] * ct + z[:, ZKRB:ZKRB + 128] * st
    for hh in range(N_HEADS):
        sl = slice(HEAD_SLAB * hh, HEAD_SLAB * (hh + 1))
        mk_ref[0, :, sl] = (kk[:, sl] + kr).astype(BF16)
        mv_ref[0, :, sl] = jnp.where(lane == MLA_V, 1.0, vv[:, sl]).astype(BF16)

    lqk_ref[0] = z[:, ZMLQK:ZMLQK + 512]
    lv_ref[0] = z[:, ZMLV:ZMLV + 256].astype(BF16)
    lo_ref[0] = z[:, ZMLO:ZMLO + 256].astype(BF16)
    gcol = z[:, ZGATE:ZGATE + 128] + gbr_ref[...]
    gc_ref[0] = jnp.where(lane < 8, gcol, jnp.where(lane < 16, _log_sigmoid(gcol), 0.0))
    zr = _dot_nt(wgt_ref[...], hb) + gbc_ref[...]
    rowi = lax.broadcasted_iota(I32, zr.shape, 0)
    grow_ref[0] = jnp.where(rowi < 8, zr, _log_sigmoid(zr))

    gq_o[0] = (z[:, ZGQ:ZGQ + 128] * GLA_DK ** -0.5).astype(BF16)
    gk_o[0] = z[:, ZGK:ZGK + 128].astype(BF16)
    gv_o[0] = z[:, ZGV:ZGV + 256].astype(BF16)
    gr_o[0] = z[:, ZGR:ZGR + 256].astype(BF16)
    zg = _dot(z[:, ZGA:ZGA + 128].astype(BF16), wal_ref[...]) + bal_ref[...]
    glg_o[0] = _log_sigmoid(zg) * (1.0 / GLA_TAU)

    dscale = DIFF_DQK ** -0.5 * LOG2E
    for g in range(2):
        gs = slice(128 * g, 128 * (g + 1))
        qg = (z[:, ZDQ:ZDQ + 256][:, gs] * cd + z[:, ZDQS:ZDQS + 256][:, gs] * sd) * dscale
        kg = z[:, ZDK:ZDK + 256][:, gs] * cd + z[:, ZDKS:ZDKS + 256][:, gs] * sd
        for hl in range(2):
            hh = 2 * g + hl
            for m in range(2):
                lo = 64 * hl + 32 * m
                s0 = (2 * hh + m) * HEAD_SLAB
                dq_ref[0, :, s0:s0 + HEAD_SLAB] = jnp.where((lane >= lo) & (lane < lo + 32), qg, 0.0).astype(BF16)
            dk_ref[0, :, HEAD_SLAB * hh:HEAD_SLAB * (hh + 1)] = jnp.where(
                (lane >= 64 * hl) & (lane < 64 * hl + 64), kg, 0.0).astype(BF16)
    for hh in range(N_HEADS):
        sl = slice(ZDV + HEAD_SLAB * hh, ZDV + HEAD_SLAB * (hh + 1))
        dv_ref[0, :, HEAD_SLAB * hh:HEAD_SLAB * (hh + 1)] = jnp.where(lane == DIFF_DV, 1.0, z[:, sl]).astype(BF16)


def _proj(x, mod, tab, lw, tm):
    b, t, _ = x.shape
    consts = [lw["g_mix_pre"], lw["w_ext"], None, lw["g_q_lat"], lw["wq"], lw["wqs"], lw["g_kv_lat"], lw["wk"], lw["wv"],
              lw["wgt"], lw["gate_bias_row"], lw["gate_bias_col"], lw["walpha"], lw["balpha"]]
    in_specs = [pl.BlockSpec((1, tm, D), lambda bi, i: (bi, i, 0)), pl.BlockSpec((1, 8, D), lambda bi, i: (bi, 0, 0))]
    args = [x, mod]
    for cst in consts:
        if cst is None:
            in_specs.append(pl.BlockSpec((tm, 512), lambda bi, i: (i, 0)))
            args.append(tab)
        else:
            in_specs.append(_const_spec(cst.shape))
            args.append(cst)
    out_specs = [pl.BlockSpec((1, tm, w), lambda bi, i: (bi, i, 0)) for _, w, _ in _PROJ_OUT]
    out_shape = [jax.ShapeDtypeStruct((b, t, w), dt) for _, w, dt in _PROJ_OUT]
    out_specs.append(pl.BlockSpec((1, 16, tm), lambda bi, i: (bi, 0, i)))
    out_shape.append(jax.ShapeDtypeStruct((b, 16, t), F32))
    outs = pl.pallas_call(
        _proj_kernel, name="proj", grid=(b, t // tm), in_specs=in_specs, out_specs=out_specs, out_shape=out_shape,
        compiler_params=_cparams(("parallel", "arbitrary")),
    )(*args)
    res = {name: o for (name, _, _), o in zip(_PROJ_OUT, outs[:-1])}
    res["grow"] = outs[-1]
    return res


def _flash_kernel(*refs, nmap, has_lat, tk, finish, post):
    if has_lat:
        q_ref, kc_ref, vc_ref, kl_ref, vl_ref, dl_ref, g_ref, o_ref, m_ref, acc_ref = refs
    else:
        q_ref, kc_ref, vc_ref, dl_ref, g_ref, o_ref, m_ref, acc_ref = refs
    qb = q_ref[0]
    tq = qb.shape[0]
    q = qb if nmap == 1 else jnp.concatenate([qb[:, :HEAD_SLAB], qb[:, HEAD_SLAB:]], axis=0)
    m_ref[...] = jnp.full(m_ref.shape, NEG, F32)
    acc_ref[...] = jnp.zeros(acc_ref.shape, F32)

    def step(k, v):
        s = _dot_nt(q, k)
        m_old = m_ref[...]
        m_new = jnp.maximum(m_old, jnp.max(s, axis=-1, keepdims=True))
        p = jnp.exp2(s - m_new[:, 0:1]).astype(BF16)
        acc_ref[...] = jnp.exp2(m_old - m_new) * acc_ref[...] + _dot(p, v)
        m_ref[...] = m_new

    step(kc_ref[0], vc_ref[0])
    if has_lat:
        def body(j, carry):
            off = pl.multiple_of(j * tk, tk)
            step(kl_ref[0, pl.ds(off, tk), :], vl_ref[0, pl.ds(off, tk), :])
            return carry
        lax.fori_loop(0, kl_ref.shape[1] // tk, body, 0)

    acc = acc_ref[...]
    o = acc / acc[:, MLA_V:MLA_V + 1]
    if nmap == 2:
        lv = dl_ref[...]
        lam = (jnp.exp(jnp.sum(lv[0:1] * lv[1:2], axis=-1, keepdims=True))
               - jnp.exp(jnp.sum(lv[2:3] * lv[3:4], axis=-1, keepdims=True)) + (1.0 - post))
        o = o[:tq] - lam * o[tq:]
    lane = lax.broadcasted_iota(I32, (tq, HEAD_SLAB), 1)
    o = jnp.where(lane < MLA_V, o, 0.0)
    if finish:
        ms = jnp.sum(o * o, axis=-1, keepdims=True) * (1.0 / DIFF_DV)
        o = o * lax.rsqrt(ms + EPS) * g_ref[0] * post
    o_ref[0] = o.astype(BF16)


def _flash(q, kc, vc, kl, vl, dlam, g_out, *, nmap, finish, post, tq, tk):
    b, t, _ = q.shape
    has_lat = kl is not None
    tc = kc.shape[1]
    qw = nmap * HEAD_SLAB
    in_specs = [pl.BlockSpec((1, tq, qw), lambda bi, h, i: (bi, i, h)),
                pl.BlockSpec((1, tc, HEAD_SLAB), lambda bi, h, i: (bi, 0, h)),
                pl.BlockSpec((1, tc, HEAD_SLAB), lambda bi, h, i: (bi, 0, h))]
    args = [q, kc, vc]
    if has_lat:
        tl = kl.shape[1]
        in_specs += [pl.BlockSpec((1, tl, HEAD_SLAB), lambda bi, h, i: (bi, 0, h)),
                     pl.BlockSpec((1, tl, HEAD_SLAB), lambda bi, h, i: (bi, 0, h))]
        args += [kl, vl]
    in_specs += [_const_spec(dlam.shape), pl.BlockSpec((1, 1, HEAD_SLAB), lambda bi, h, i: (h, 0, 0))]
    args += [dlam, g_out]
    rows = nmap * tq
    return pl.pallas_call(
        functools.partial(_flash_kernel, nmap=nmap, has_lat=has_lat, tk=tk, finish=finish, post=post),
        name="flash_diff" if nmap == 2 else "flash_mla",
        grid=(b, N_HEADS, t // tq), in_specs=in_specs,
        out_specs=pl.BlockSpec((1, tq, HEAD_SLAB), lambda bi, h, i: (bi, i, h)),
        out_shape=jax.ShapeDtypeStruct((b, t, N_HEADS * HEAD_SLAB), BF16),
        scratch_shapes=[pltpu.VMEM((rows, HEAD_SLAB), F32), pltpu.VMEM((rows, HEAD_SLAB), F32)],
        compiler_params=_cparams(("parallel", "parallel", "arbitrary")),
    )(*args)


def _head_of(shape, axis, width):
    return (lax.broadcasted_iota(I32, shape, axis) % (N_HEADS * width)) // width


def _mlstm_dir(d, first, last, x, xprev, xnext, v, gcol, grow, wc, bcv, c_ref, m_ref):
    L = x.shape[0]
    row = lax.broadcasted_iota(I32, x.shape, 0)
    pr = jnp.where(first, 0.0, xprev[7:8, :])
    nx = jnp.where(last, 0.0, xnext[0:1, :])
    xm = jnp.where(row == 0, pr, pltpu.roll(x, 1, 0))
    xp = jnp.where(row == L - 1, nx, pltpu.roll(x, L - 1, 0))
    y = xm * wc[0:1] + x * wc[1:2] + xp * wc[2:3] + bcv
    qk = y * _sigmoid(y)
    q = qk[:, :256]
    k = qk[:, 256:] * ML_DH ** -0.5

    li = lax.broadcasted_iota(I32, (L, L), 0)
    si = lax.broadcasted_iota(I32, (L, L), 1)
    tin = (si <= li) if d == 0 else (si >= li)
    tinf = tin.astype(F32)
    bcol = _dot(tinf, gcol, HIGHEST)
    brow = _dot_nt(grow, tinf, HIGHEST)
    m0e = m_ref[0, d]
    cb = c_ref[0, d]
    hm256 = _head_of((L, 256), 1, ML_DH)
    hm512 = _head_of((L, 512), 1, ML_DH)
    e_idx = L - 1 if d == 0 else 0

    d_blk, inter_blk = [], []
    for hh in range(N_HEADS):
        c = 4 * d + hh
        bc = bcol[:, 8 + c:9 + c]
        d_blk.append(jnp.where(tin, bc - brow[8 + c:9 + c, :] + grow[c:c + 1, :], NEG))
        inter_blk.append(bc + m0e[0:1, 64 * hh:64 * hh + 1])
    d_st = jnp.concatenate(d_blk, axis=0)
    inter_st = jnp.concatenate(inter_blk, axis=0)
    mt = jnp.maximum(inter_st, jnp.max(d_st, axis=-1, keepdims=True))
    q_st = jnp.concatenate([jnp.where(hm256 == hh, q, 0.0) for hh in range(N_HEADS)], axis=0).astype(BF16)
    s_st = (jnp.exp(d_st - mt) * _dot_nt(q_st, k.astype(BF16))).astype(BF16)
    vext = jnp.concatenate([v, jnp.ones((L, 256), BF16)], axis=1)
    r = _dot(s_st, vext)
    aint = jnp.exp(inter_st - mt)
    p = _dot(q.astype(BF16), cb.astype(BF16))
    tot = jnp.zeros((L, 512), F32)
    mte = jnp.zeros((L, 256), F32)
    for hh in range(N_HEADS):
        rs = slice(hh * L, (hh + 1) * L)
        tot = jnp.where(hm512 == hh, r[rs] + aint[rs] * p, tot)
        mte = jnp.where(hm256 == hh, mt[rs], mte)
    hout = tot[:, :256] / jnp.maximum(jnp.abs(tot[:, 256:]), jnp.exp(-mte))

    wexp = jnp.zeros((L, 256), F32)
    arow = jnp.zeros((1, 512), F32)
    grw = jnp.zeros((1, 512), F32)
    mnew = jnp.zeros((1, 256), F32)
    hr512 = _head_of((1, 512), 1, ML_DH)
    hr256 = _head_of((1, 256), 1, ML_DH)
    for hh in range(N_HEADS):
        c = 4 * d + hh
        bc = bcol[:, 8 + c:9 + c]
        be = bc[e_idx:e_idx + 1, :]
        wl = be - bc + gcol[:, c:c + 1]
        mloc = jnp.max(wl, axis=0, keepdims=True)
        m0h = m0e[0:1, 64 * hh:64 * hh + 1]
        mn = jnp.maximum(be + m0h, mloc)
        wexp = jnp.where(hm256 == hh, jnp.exp(wl - mloc), wexp)
        arow = jnp.where(hr512 == hh, jnp.exp(be + m0h - mn), arow)
        grw = jnp.where(hr512 == hh, jnp.exp(mloc - mn), grw)
        mnew = jnp.where(hr256 == hh, mn, mnew)
    cl = _dot_tn((k * wexp).astype(BF16), vext)
    bd = lax.broadcasted_iota(I32, (256, 512), 0) // ML_DH == _head_of((256, 512), 1, ML_DH)
    c_ref[0, d] = arow * cb + jnp.where(bd, grw * cl, 0.0)
    m_ref[0, d] = jnp.broadcast_to(mnew, (8, 256))
    return hout


def _mlstm_kernel(xf, xfp, xfn, xb, xbp, xbn, vf, vb, gcf, gcb, grf, grb, wc_ref, bc_ref, c0_ref, m0_ref,
                  hf_ref, hb_ref, c_ref, m_ref):
    i = pl.program_id(1)
    n = pl.num_programs(1)

    @pl.when(i == 0)
    def _():
        c_ref[...] = c0_ref[...]
        m_ref[...] = m0_ref[...]

    wc = wc_ref[...]
    bcv = bc_ref[...]
    hf_ref[0] = _mlstm_dir(0, i == 0, i == n - 1, xf[0], xfp[0], xfn[0], vf[0], gcf[0], grf[0], wc, bcv, c_ref, m_ref)
    hb_ref[0] = _mlstm_dir(1, i == n - 1, i == 0, xb[0], xbp[0], xbn[0], vb[0], gcb[0], grb[0], wc, bcv, c_ref, m_ref)


def _mlstm(pr, w_conv, b_conv, c0, m0):
    x, v, gc, gr = pr["lqk"], pr["lv"], pr["gc"], pr["grow"]
    b, t, _ = x.shape
    L = ML_CHUNK
    n = t // L
    r8 = L // 8
    last8 = t // 8 - 1

    def fw(bi, i):
        return (bi, i, 0)

    def bw(bi, i):
        return (bi, n - 1 - i, 0)

    def halo(ix, shift):
        def f(bi, i):
            blk = ix(bi, i)[1]
            return (bi, jnp.clip(blk * r8 + shift, 0, last8), 0)
        return f

    main = lambda w, ix: pl.BlockSpec((1, L, w), ix)
    in_specs = [main(512, fw), pl.BlockSpec((1, 8, 512), halo(fw, -1)), pl.BlockSpec((1, 8, 512), halo(fw, r8)),
                main(512, bw), pl.BlockSpec((1, 8, 512), halo(bw, -1)), pl.BlockSpec((1, 8, 512), halo(bw, r8)),
                main(256, fw), main(256, bw), main(128, fw), main(128, bw),
                pl.BlockSpec((1, 16, L), lambda bi, i: (bi, 0, i)), pl.BlockSpec((1, 16, L), lambda bi, i: (bi, 0, n - 1 - i)),
                _const_spec(w_conv.shape), _const_spec(b_conv.shape),
                pl.BlockSpec((1, 2, 256, 512), lambda bi, i: (bi, 0, 0, 0)), pl.BlockSpec((1, 2, 8, 256), lambda bi, i: (bi, 0, 0, 0))]
    out_specs = [main(256, fw), main(256, bw),
                 pl.BlockSpec((1, 2, 256, 512), lambda bi, i: (bi, 0, 0, 0)), pl.BlockSpec((1, 2, 8, 256), lambda bi, i: (bi, 0, 0, 0))]
    out_shape = [jax.ShapeDtypeStruct((b, t, 256), F32), jax.ShapeDtypeStruct((b, t, 256), F32),
                 jax.ShapeDtypeStruct(c0.shape, F32), jax.ShapeDtypeStruct(m0.shape, F32)]
    return pl.pallas_call(
        _mlstm_kernel, name="mlstm", grid=(b, n), in_specs=in_specs, out_specs=out_specs, out_shape=out_shape,
        compiler_params=_cparams(("parallel", "arbitrary")),
    )(x, x, x, x, x, x, v, v, gc, gc, gr, gr, w_conv, b_conv, c0, m0)


def _gla_dir(d, q, k, v, lg, s_ref):
    L = q.shape[0]
    li = lax.broadcasted_iota(I32, (L, L), 0)
    si = lax.broadcasted_iota(I32, (L, L), 1)
    tin = (si <= li) if d == 0 else (si >= li)
    tinf = tin.astype(F32)
    lgd = lg[:, 128 * d:128 * (d + 1)]
    gcum = _dot(tinf, lgd, HIGHEST)
    e_idx = L - 1 if d == 0 else 0
    gend = gcum[e_idx:e_idx + 1, :]
    qf, kf = q.astype(F32), k.astype(F32)
    q_dec = qf * jnp.exp(gcum)
    k_dec = (kf * jnp.exp(-gcum)).astype(BF16)
    k_end = (kf * jnp.exp(gend - gcum)).astype(BF16)
    hm128 = _head_of((L, 128), 1, GLA_DK)
    hm256 = _head_of((L, 256), 1, GLA_DV)
    q_st = jnp.concatenate([jnp.where(hm128 == hh, q_dec, 0.0) for hh in range(N_HEADS)], axis=0).astype(BF16)
    att = _dot_nt(q_st, k_dec)
    tin4 = jnp.concatenate([tin] * N_HEADS, axis=0)
    o_st = _dot(jnp.where(tin4, att, 0.0).astype(BF16), v)
    sb = s_ref[0, d]
    o = _dot(q_dec.astype(BF16), sb.astype(BF16))
    for hh in range(N_HEADS):
        o = o + jnp.where(hm256 == hh, o_st[hh * L:(hh + 1) * L], 0.0)
    gend_col = _dot_tn(lgd, jnp.ones((L, 256), F32), HIGHEST)
    bd = lax.broadcasted_iota(I32, (128, 256), 0) // GLA_DK == _head_of((128, 256), 1, GLA_DV)
    s_ref[0, d] = jnp.exp(gend_col) * sb + jnp.where(bd, _dot_tn(k_end, v), 0.0)
    return o


def _gla_kernel(qf, kf, vf, lf, qb, kb, vb, lb, s0_ref, of_ref, ob_ref, s_ref):
    i = pl.program_id(1)

    @pl.when(i == 0)
    def _():
        s_ref[...] = s0_ref[...]

    of_ref[0] = _gla_dir(0, qf[0], kf[0], vf[0], lf[0], s_ref)
    ob_ref[0] = _gla_dir(1, qb[0], kb[0], vb[0], lb[0], s_ref)


def _gla(pr, s0):
    q, k, v, lg = pr["gq"], pr["gk"], pr["gv"], pr["glg"]
    b, t, _ = q.shape
    L = GLA_CHUNK
    n = t // L
    fw = lambda bi, i: (bi, i, 0)
    bw = lambda bi, i: (bi, n - 1 - i, 0)
    blk = lambda w, ix: pl.BlockSpec((1, L, w), ix)
    st_spec = pl.BlockSpec((1, 2, 128, 256), lambda bi, i: (bi, 0, 0, 0))
    return pl.pallas_call(
        _gla_kernel, name="gla", grid=(b, n),
        in_specs=[blk(128, fw), blk(128, fw), blk(256, fw), blk(256, fw),
                  blk(128, bw), blk(128, bw), blk(256, bw), blk(256, bw), st_spec],
        out_specs=[blk(256, fw), blk(256, bw), st_spec],
        out_shape=[jax.ShapeDtypeStruct((b, t, 256), F32), jax.ShapeDtypeStruct((b, t, 256), F32),
                   jax.ShapeDtypeStruct(s0.shape, F32)],
        compiler_params=_cparams(("parallel", "arbitrary")),
    )(q, k, v, lg, q, k, v, lg, s0)


def _head_rms_expanded(x, width):
    n = x.shape[1]
    bd = (lax.broadcasted_iota(I32, (n, n), 0) // width == lax.broadcasted_iota(I32, (n, n), 1) // width).astype(F32)
    return _dot(x * x, bd, HIGHEST) * (1.0 / width)


def _merge_kernel(x_ref, mod_ref, ya_ref, hf_ref, hb_ref, lo_ref, gf_ref, gb_ref, gr_ref, yd_ref,
                  gpre_ref, wg_ref, bg_ref, wbr_ref, wo_ref, gpost_ref, gffn_ref, wr_ref, wrt_ref, gml_ref, ggla_ref,
                  xm_ref, h2_ref, aff_ref, afft_ref):
    x = x_ref[0]
    tm = x.shape[0]
    mod = mod_ref[0]
    hb = (_rms(x) * gpre_ref[...] * (1.0 + mod[1:2]) + mod[0:1]).astype(BF16)

    hs = hf_ref[0] + hb_ref[0]
    y_ml = _sigmoid(lo_ref[0].astype(F32)) * (hs * lax.rsqrt(_head_rms_expanded(hs, ML_DH) + EPS) * gml_ref[...])
    gs = gf_ref[0] + gb_ref[0]
    rr = gr_ref[0].astype(F32)
    y_gla = rr * _sigmoid(rr) * (gs * lax.rsqrt(_head_rms_expanded(gs, GLA_DV) + EPS) * ggla_ref[...])

    branches = ((ya_ref[0], 0, 512), (y_ml.astype(BF16), 512, 256), (y_gla.astype(BF16), 768, 256), (yd_ref[0], 1024, 512))
    mix = jnp.zeros((tm, D), F32)
    for nb, (yb, r0, rw) in enumerate(branches):
        gate = _sigmoid(_dot(hb, wg_ref[:, nb * D:(nb + 1) * D]) + bg_ref[:, nb * D:(nb + 1) * D])
        mix = mix + gate * _dot(yb, wbr_ref[r0:r0 + rw, :])
    y = _dot(mix.astype(BF16), wo_ref[...])
    xm = x + mod[2:3] * (_rms(y) * gpost_ref[...])
    xm_ref[0] = xm

    h2 = (_rms(xm) * gffn_ref[...] * (1.0 + mod[4:5]) + mod[3:4]).astype(BF16)
    h2_ref[0] = h2
    lane = lax.broadcasted_iota(I32, (tm, LANE), 1)
    lg = jnp.where(lane < N_EXPERTS, _dot(h2, wr_ref[...]), NEG)
    e = jnp.exp(lg - jnp.max(lg, axis=-1, keepdims=True))
    aff_ref[0] = (e / jnp.sum(e, axis=-1, keepdims=True))[:, :N_EXPERTS]
    lt = _dot_nt(wrt_ref[...], h2)
    et = jnp.exp(lt - jnp.max(lt, axis=0, keepdims=True))
    afft_ref[0] = et / jnp.sum(et, axis=0, keepdims=True)


def _merge(x, mod, ya, ml, lo, gl, gr, yd, lw, tm):
    b, t, _ = x.shape
    tok = lambda w: pl.BlockSpec((1, tm, w), lambda bi, i: (bi, i, 0))
    consts = [lw["g_mix_pre"], lw["w_gate"], lw["b_gate"], lw["wbr"], lw["w_out"], lw["g_mix_post"], lw["g_ffn_pre"],
              lw["w_router"], lw["w_router_t"], lw["g_mlstm_out"], lw["g_gla_out"]]
    in_specs = [tok(D), pl.BlockSpec((1, 8, D), lambda bi, i: (bi, 0, 0)), tok(512), tok(256), tok(256), tok(256),
                tok(256), tok(256), tok(256), tok(512)] + [_const_spec(c.shape) for c in consts]
    out_specs = [tok(D), tok(D), tok(N_EXPERTS), pl.BlockSpec((1, N_EXPERTS, tm), lambda bi, i: (bi, 0, i))]
    out_shape = [jax.ShapeDtypeStruct((b, t, D), F32), jax.ShapeDtypeStruct((b, t, D), BF16),
                 jax.ShapeDtypeStruct((b, t, N_EXPERTS), F32), jax.ShapeDtypeStruct((b, N_EXPERTS, t), F32)]
    return pl.pallas_call(
        _merge_kernel, name="merge", grid=(b, t // tm), in_specs=in_specs, out_specs=out_specs, out_shape=out_shape,
        compiler_params=_cparams(("parallel", "arbitrary")),
    )(x, mod, ya, ml[0], ml[1], lo, gl[0], gl[1], gr, yd, *consts)


def _topk_kernel(a_ref, pos_ref, s0_ref, *, cap):
    nblk = a_ref.shape[1]
    bits = pltpu.bitcast(a_ref[0], I32)

    def bisect(i, thr):
        cand = thr | (1 << (30 - i))
        cnt = jnp.sum((bits >= cand).astype(I32), axis=(0, 2), keepdims=True)
        return jnp.where(cnt >= cap, cand, thr)

    thr3 = lax.fori_loop(0, 31, bisect, jnp.zeros((1, N_EXPERTS, 1), I32))
    need3 = cap - jnp.sum((bits > thr3).astype(I32), axis=(0, 2), keepdims=True)
    thr, need = thr3[0], need3[0].astype(F32)
    upper = (lax.broadcasted_iota(I32, (TOK_BLK, TOK_BLK), 0) <= lax.broadcasted_iota(I32, (TOK_BLK, TOK_BLK), 1)).astype(BF16)

    def blk(j, carry):
        c_eq, c_sel = carry
        bj = pltpu.bitcast(a_ref[0, j], I32)
        gt, eq = bj > thr, bj == thr
        cum_eq = _dot(eq.astype(BF16), upper) + c_eq
        sel = gt | (eq & (cum_eq <= need))
        cum_sel = _dot(sel.astype(BF16), upper) + c_sel
        pos_ref[0, j] = jnp.where(sel, cum_sel - 1.0, -1.0).astype(I32)
        s0_ref[0, j] = jnp.broadcast_to(c_sel, (N_EXPERTS, LANE)).astype(I32)
        return cum_eq[:, TOK_BLK - 1:TOK_BLK], cum_sel[:, TOK_BLK - 1:TOK_BLK]

    zero = jnp.zeros((N_EXPERTS, 1), F32)
    lax.fori_loop(0, nblk, blk, (zero, zero))


def _topk(aff_t, cap):
    b, _, t = aff_t.shape
    nblk = t // TOK_BLK
    a4 = aff_t.reshape(b, N_EXPERTS, nblk, TOK_BLK).transpose(0, 2, 1, 3)
    spec = lambda w: pl.BlockSpec((1, nblk, N_EXPERTS, w), lambda bi: (bi, 0, 0, 0))
    return pl.pallas_call(
        functools.partial(_topk_kernel, cap=cap), name="topk", grid=(b,),
        in_specs=[spec(TOK_BLK)], out_specs=[spec(TOK_BLK), spec(LANE)],
        out_shape=[jax.ShapeDtypeStruct((b, nblk, N_EXPERTS, TOK_BLK), I32),
                   jax.ShapeDtypeStruct((b, nblk, N_EXPERTS, LANE), I32)],
        compiler_params=_cparams(("parallel",)),
    )(a4)


def _moe_kernel(s0_ref, pos_ref, h_ref, wg_ref, wu_ref, wd_ref, ys_ref, xs_ref, *, nblk, capp):
    bi, e, tb = pl.program_id(0), pl.program_id(1), pl.program_id(2)

    @pl.when(tb == 0)
    def _():
        xs_ref[...] = jnp.zeros(xs_ref.shape, BF16)

    s0 = s0_ref[(bi * N_EXPERTS + e) * nblk + tb]
    a0 = pl.multiple_of((s0 // 16) * 16, 16)
    prow = pos_ref[0, 0, pl.ds(e, 1), :]
    slot = lax.broadcasted_iota(I32, (GATHER_WIN, TOK_BLK), 0) + a0
    onehot = (slot == prow).astype(BF16)
    rows = _dot(onehot, h_ref[0]).astype(BF16)
    xs_ref[pl.ds(a0, GATHER_WIN), :] = xs_ref[pl.ds(a0, GATHER_WIN), :] + rows

    @pl.when(tb == nblk - 1)
    def _():
        def chunk(c, carry):
            off = pl.multiple_of(c * TOK_BLK, TOK_BLK)
            xc = xs_ref[pl.ds(off, TOK_BLK), :]
            hg = _dot(xc, wg_ref[0])
            hid = (hg * _sigmoid(hg) * _dot(xc, wu_ref[0])).astype(BF16)
            ys_ref[0, 0, pl.ds(off, TOK_BLK), :] = _dot(hid, wd_ref[0]).astype(BF16)
            return carry
        lax.fori_loop(0, capp // TOK_BLK, chunk, 0)


def _moe(s0_flat, pos4, h2, lw, capp):
    b, t, _ = h2.shape
    nblk = t // TOK_BLK
    grid_spec = pltpu.PrefetchScalarGridSpec(
        num_scalar_prefetch=1, grid=(b, N_EXPERTS, nblk),
        in_specs=[pl.BlockSpec((1, 1, N_EXPERTS, TOK_BLK), lambda bi, e, tb, s: (bi, tb, 0, 0)),
                  pl.BlockSpec((1, TOK_BLK, D), lambda bi, e, tb, s: (bi, tb, 0)),
                  pl.BlockSpec((1, D, EXPERT_FF), lambda bi, e, tb, s: (e, 0, 0)),
                  pl.BlockSpec((1, D, EXPERT_FF), lambda bi, e, tb, s: (e, 0, 0)),
                  pl.BlockSpec((1, EXPERT_FF, D), lambda bi, e, tb, s: (e, 0, 0))],
        out_specs=pl.BlockSpec((1, 1, capp, D), lambda bi, e, tb, s: (bi, e, 0, 0)),
        scratch_shapes=[pltpu.VMEM((capp + GATHER_WIN, D), BF16)])
    return pl.pallas_call(
        functools.partial(_moe_kernel, nblk=nblk, capp=capp), name="moe", grid_spec=grid_spec,
        out_shape=jax.ShapeDtypeStruct((b, N_EXPERTS, capp, D), BF16),
        compiler_params=_cparams(("arbitrary", "arbitrary", "arbitrary")),
    )(s0_flat, pos4, h2, lw["w_e_gate"], lw["w_e_up"], lw["w_e_down"])


def _combine_kernel(s0_ref, *refs, nblk, nb):
    ys_refs = refs[:2 * N_EXPERTS]
    pos_ref, aff_ref, xm_ref, mod_ref, g_ref, o_ref = refs[2 * N_EXPERTS:]
    bi, tb = pl.program_id(0), pl.program_id(1)
    pos = pos_ref[0]
    aff = aff_ref[0]
    lane = lax.broadcasted_iota(I32, (TOK_BLK, TOK_BLK), 1)
    acc = jnp.zeros((TOK_BLK, D), F32)
    for e in range(N_EXPERTS):
        s0 = s0_ref[(bi * N_EXPERTS + e) * nblk + tb]
        blk0 = jnp.minimum(s0 // TOK_BLK, nb - 1)
        rel = pos[:, e:e + 1] - blk0 * TOK_BLK
        y = (_dot((lane == rel).astype(BF16), ys_refs[2 * e][0, 0])
             + _dot((lane + TOK_BLK == rel).astype(BF16), ys_refs[2 * e + 1][0, 0]))
        acc = acc + aff[:, e:e + 1] * y
    mod = mod_ref[0]
    o_ref[0] = xm_ref[0] + mod[5:6] * (_rms(acc) * g_ref[...])


def _combine(s0_flat, ys, pos_t, aff, xm, mod, g_post):
    b, t, _ = xm.shape
    nblk = t // TOK_BLK
    nb = ys.shape[2] // TOK_BLK

    def ys_spec(e, k):
        def ix(bi, tb, s):
            blk0 = jnp.minimum(s[(bi * N_EXPERTS + e) * nblk + tb] // TOK_BLK, nb - 1)
            return (bi, e, jnp.minimum(blk0 + k, nb - 1), 0)
        return pl.BlockSpec((1, 1, TOK_BLK, D), ix)

    tok = lambda w: pl.BlockSpec((1, TOK_BLK, w), lambda bi, tb, s: (bi, tb, 0))
    in_specs = [ys_spec(e, k) for e in range(N_EXPERTS) for k in range(2)]
    in_specs += [tok(N_EXPERTS), tok(N_EXPERTS), tok(D), pl.BlockSpec((1, 8, D), lambda bi, tb, s: (bi, 0, 0)),
                 pl.BlockSpec((1, D), lambda bi, tb, s: (0, 0))]
    grid_spec = pltpu.PrefetchScalarGridSpec(num_scalar_prefetch=1, grid=(b, nblk), in_specs=in_specs, out_specs=tok(D))
    return pl.pallas_call(
        functools.partial(_combine_kernel, nblk=nblk, nb=nb), name="combine", grid_spec=grid_spec,
        out_shape=jax.ShapeDtypeStruct((b, t, D), F32),
        compiler_params=_cparams(("arbitrary", "arbitrary")),
    )(s0_flat, *([ys] * (2 * N_EXPERTS)), pos_t, aff, xm, mod, g_post)


def _rope_table(t):
    nf = ROPE_DIM // 4
    pos = jnp.arange(t)
    inv = ROPE_BASE ** (-jnp.arange(nf, dtype=F32) / nf)
    ang = jnp.stack([pos // GRID_W, pos % GRID_W], axis=-1).astype(F32)[..., None] * inv
    cos, sin = jnp.cos(ang), jnp.sin(ang)
    c32 = jnp.stack([cos, cos], axis=2).reshape(t, ROPE_DIM)
    s32 = jnp.stack([-sin, sin], axis=2).reshape(t, ROPE_DIM)
    one, zero = jnp.ones((t, 64), F32), jnp.zeros((t, 32), F32)
    ct = jnp.concatenate([one, c32, zero], axis=1)
    st = jnp.concatenate([0.0 * one, s32, zero], axis=1)
    return jnp.concatenate([ct, st, jnp.tile(c32, (1, 4)), jnp.tile(s32, (1, 4))], axis=1)


def _identity_table(t):
    one, zero = jnp.ones((t, 128), F32), jnp.zeros((t, 128), F32)
    ct = jnp.concatenate([jnp.ones((t, 96), F32), jnp.zeros((t, 32), F32)], axis=1)
    return jnp.concatenate([ct, zero, one, zero], axis=1)


def _layer_weights(i, p):
    lw = {}
    row = lambda a: a.reshape(1, -1)
    for name in ("g_mix_pre", "g_mix_post", "g_ffn_pre", "g_ffn_post", "g_q_lat", "g_kv_lat", "g_mlstm_out", "g_gla_out"):
        lw[name] = row(p[name][i])
    lw["w_ext"] = _gather_cols(p["w_in"][i], _WIN_IDX).astype(BF16)
    lw["wgt"] = p["w_in"][i][:, 416 + 1024:416 + 1040].T.astype(BF16)
    gb = jnp.concatenate([p["b_igate"][i].reshape(-1), p["b_fgate"][i].reshape(-1)])
    lw["gate_bias_row"] = jnp.pad(gb, (0, LANE - 16)).reshape(1, LANE)
    lw["gate_bias_col"] = gb.reshape(16, 1)
    wuq = p["w_uq"][i]
    qi = -np.ones((512,), np.int64)
    qsi = -np.ones((512,), np.int64)
    for h in range(N_HEADS):
        qi[128 * h:128 * h + 96] = 96 * h + np.arange(96)
        qsi[128 * h + 64:128 * h + 96] = 96 * h + 64 + (np.arange(32) ^ 8)
    lw["wq"] = _gather_cols(wuq, qi).astype(BF16)
    lw["wqs"] = _gather_cols(wuq, qsi).astype(BF16)
    ki = -np.ones((512,), np.int64)
    vi = -np.ones((512,), np.int64)
    for h in range(N_HEADS):
        ki[128 * h:128 * h + 64] = 128 * h + np.arange(64)
        vi[128 * h:128 * h + 64] = 128 * h + 64 + np.arange(64)
    lw["wk"] = _gather_cols(p["w_ukv"][i], ki).astype(BF16)
    lw["wv"] = _gather_cols(p["w_ukv"][i], vi).astype(BF16)
    wa = p["w_alpha2"][i]
    wal = jnp.zeros((LANE, 256), F32).at[0:16, 0:128].set(wa[0]).at[16:32, 128:256].set(wa[1])
    lw["walpha"] = wal.astype(BF16)
    lw["balpha"] = p["b_alpha"][i].reshape(1, 256)
    lw["w_conv"] = p["w_conv"][i]
    lw["b_conv"] = row(p["b_conv"][i])
    lw["dlam"] = p["diff_lambda"][i]
    lw["g_diff"] = jnp.pad(p["g_diff_out"][i].reshape(N_HEADS, 1, DIFF_DV), ((0, 0), (0, 0), (0, HEAD_SLAB - DIFF_DV)))
    wb = p["w_branch"][i]
    lw["wbr"] = jnp.concatenate([_pad_heads_rows(wb[0], 64), wb[1], wb[2], _pad_heads_rows(wb[3], 64)], axis=0).astype(BF16)
    lw["w_gate"] = p["w_gate"][i].astype(BF16)
    lw["b_gate"] = row(p["b_gate"][i])
    lw["w_out"] = p["w_out"][i].astype(BF16)
    lw["w_router"] = jnp.pad(p["w_router"][i], ((0, 0), (0, LANE - N_EXPERTS))).astype(BF16)
    lw["w_router_t"] = p["w_router"][i].T.astype(BF16)
    lw["w_e_gate"] = p["w_e_gate"][i].astype(BF16)
    lw["w_e_up"] = p["w_e_up"][i].astype(BF16)
    lw["w_e_down"] = p["w_e_down"][i].astype(BF16)
    return lw


def _ffn(xm, h2, aff, aff_t, mod, lw):
    b, t, _ = xm.shape
    nblk = t // TOK_BLK
    cap = EC_CAPACITY * t // N_EXPERTS
    capp = -(-cap // TOK_BLK) * TOK_BLK
    pos4, s04 = _topk(aff_t, cap)
    s0_flat = s04[..., 0].transpose(0, 2, 1).reshape(-1)
    ys = _moe(s0_flat, pos4, h2, lw, capp)
    pos_t = pos4.transpose(0, 1, 3, 2).reshape(b, t, N_EXPERTS)
    return _combine(s0_flat, ys, pos_t, aff, xm, mod, lw["g_ffn_post"])


def _hybrid_layer(i, x_c, x_l, c8, need_ctx, p):
    lw = _layer_weights(i, p)
    b, t, _ = x_l.shape
    tc = x_c.shape[1]
    lam_init = 0.8 - 0.6 * math.exp(-0.3 * i)
    mod8 = _ada(c8, p["w_ada"][i], p["b_ada"][i])
    pad = lambda m: jnp.pad(m.reshape(b, 6, D), ((0, 0), (0, 2), (0, 0)))
    mod_l = pad(mod8[:b])
    mod_c = pad(jnp.broadcast_to(mod8[b:b + 1], (b, 6 * D)))

    pc = _proj(x_c, mod_c, _identity_table(tc), lw, tm=tc)
    pt = _proj(x_l, mod_l, _rope_table(t), lw, tm=256)

    zc = jnp.zeros((b, 2, 256, 512), F32)
    zm = jnp.full((b, 2, 8, 256), NEG, F32)
    zs = jnp.zeros((b, 2, 128, 256), F32)
    hf_c, hb_c, c_fin, m_fin = _mlstm(pc, lw["w_conv"], lw["b_conv"], zc, zm)
    hf_l, hb_l, _, _ = _mlstm(pt, lw["w_conv"], lw["b_conv"], c_fin, m_fin)
    gf_c, gb_c, s_fin = _gla(pc, zs)
    gf_l, gb_l, _ = _gla(pt, s_fin)

    one_g = jnp.ones((N_HEADS, 1, HEAD_SLAB), F32)
    fl = functools.partial(_flash, tk=512)
    ya_l = fl(pt["mq"], pc["mk"], pc["mv"], pt["mk"], pt["mv"], lw["dlam"], one_g, nmap=1, finish=False, post=1.0, tq=256)
    yd_l = fl(pt["dq"], pc["dk"], pc["dv"], pt["dk"], pt["dv"], lw["dlam"], lw["g_diff"], nmap=2, finish=True,
              post=1.0 - lam_init, tq=256)
    xm, h2, aff, aff_t = _merge(x_l, mod_l, ya_l, (hf_l, hb_l), pt["lo"], (gf_l, gb_l), pt["gr"], yd_l, lw, tm=256)
    x_l = _ffn(xm, h2, aff, aff_t, mod_l, lw)

    if need_ctx:
        ya_c = fl(pc["mq"], pc["mk"], pc["mv"], None, None, lw["dlam"], one_g, nmap=1, finish=False, post=1.0, tq=tc)
        yd_c = fl(pc["dq"], pc["dk"], pc["dv"], None, None, lw["dlam"], lw["g_diff"], nmap=2, finish=True,
                  post=1.0 - lam_init, tq=tc)
        xm, h2, aff, aff_t = _merge(x_c, mod_c, ya_c, (hf_c, hb_c), pc["lo"], (gf_c, gb_c), pc["gr"], yd_c, lw, tm=tc)
        x_c = _ffn(xm, h2, aff, aff_t, mod_c, lw)
    return x_c, x_l


def kernel(x, c, ctx, c_ctx, w_ada, b_ada, g_mix_pre, g_mix_post, g_ffn_pre, g_ffn_post, w_in, g_q_lat, w_uq, g_kv_lat, w_ukv, w_conv, b_conv, b_igate, b_fgate, g_mlstm_out, w_alpha2, b_alpha, g_gla_out, diff_lambda, g_diff_out, w_branch, w_gate, b_gate, w_out, w_router, w_e_gate, w_e_up, w_e_down):
    p = dict(w_ada=w_ada, b_ada=b_ada, g_mix_pre=g_mix_pre, g_mix_post=g_mix_post, g_ffn_pre=g_ffn_pre,
             g_ffn_post=g_ffn_post, w_in=w_in, g_q_lat=g_q_lat, w_uq=w_uq, g_kv_lat=g_kv_lat, w_ukv=w_ukv,
             w_conv=w_conv, b_conv=b_conv, b_igate=b_igate, b_fgate=b_fgate, g_mlstm_out=g_mlstm_out,
             w_alpha2=w_alpha2, b_alpha=b_alpha, g_gla_out=g_gla_out, diff_lambda=diff_lambda, g_diff_out=g_diff_out,
             w_branch=w_branch, w_gate=w_gate, b_gate=b_gate, w_out=w_out, w_router=w_router,
             w_e_gate=w_e_gate, w_e_up=w_e_up, w_e_down=w_e_down)
    b = x.shape[0]
    c8 = jnp.concatenate([c, c_ctx[None], jnp.zeros((8 - b - 1, D), F32)], axis=0)
    x_c, x_l = ctx, x
    for i in range(DEPTH):
        x_c, x_l = _hybrid_layer(i, x_c, x_l, c8, i < DEPTH - 1, p)
    return x_l
```

```python
import functools
import math

import numpy as np
import jax
import jax.numpy as jnp
from jax import lax
from jax.experimental import pallas as pl
from jax.experimental.pallas import tpu as pltpu

F32 = jnp.float32
BF16 = jnp.bfloat16
I32 = jnp.int32
HIGHEST = lax.Precision.HIGHEST

D = 1024
DEPTH = 2
GRID_W = 64
N_HEADS = 4
MLA_NOPE, MLA_ROPE, MLA_V = 64, 32, 64
MLA_Q_LORA, MLA_KV_LORA = 256, 128
ML_DH = 64
GLA_DK, GLA_DV, GLA_RANK, GLA_TAU = 32, 64, 16, 16.0
DIFF_DQK, DIFF_DV = 32, 64
ROPE_DIM, ROPE_BASE = 32, 10000.0
N_EXPERTS, EC_CAPACITY, EXPERT_FF = 16, 2, 1408
NEG = -1e30
EPS = 1e-6
LOG2E = 1.4426950408889634

LANE = 128
HEAD_SLAB = 128
TOK_BLK = 256
ML_CHUNK = 128
GLA_CHUNK = 64
GATHER_WIN = TOK_BLK + 16
VMEM_LIMIT = 56 * 1024 * 1024

ZQ, ZKV, ZKRA, ZKRB, ZMLQK, ZMLV, ZMLO, ZGATE, ZGA = 0, 256, 384, 512, 640, 1152, 1408, 1664, 1792
ZGQ, ZGK, ZGV, ZGR, ZDQ, ZDQS, ZDK, ZDKS, NZ = 1920, 2048, 2176, 2432, 2688, 2944, 3200, 3456, 3712
KV_CHUNK = 256
VT_ROWS = 80


def _swap32(c):
    return (c // 32) * 32 + ((c % 32) ^ 8)


def _win_index():
    idx = -np.ones((NZ,), np.int64)
    idx[ZQ:ZQ + 256] = np.arange(0, 256)
    idx[ZKV:ZKV + 128] = np.arange(256, 384)
    r = np.arange(32)
    idx[ZKRA + 64:ZKRA + 96] = 384 + r
    idx[ZKRB + 64:ZKRB + 96] = 384 + (r ^ 8)
    ml = 416
    idx[ZMLQK:ZMLQK + 512] = ml + np.arange(512)
    idx[ZMLV:ZMLV + 256] = ml + 512 + np.arange(256)
    idx[ZMLO:ZMLO + 256] = ml + 768 + np.arange(256)
    idx[ZGATE:ZGATE + 16] = ml + 1024 + np.arange(16)
    gl = 1456
    idx[ZGQ:ZGQ + 128] = gl + np.arange(128)
    idx[ZGK:ZGK + 128] = gl + 128 + np.arange(128)
    idx[ZGV:ZGV + 256] = gl + 256 + np.arange(256)
    idx[ZGR:ZGR + 256] = gl + 512 + np.arange(256)
    idx[ZGA:ZGA + 32] = gl + 768 + np.arange(32)
    df = 2256
    c = np.arange(256)
    idx[ZDQ:ZDQ + 256] = df + c
    idx[ZDQS:ZDQS + 256] = df + _swap32(c)
    idx[ZDK:ZDK + 256] = df + 256 + c
    idx[ZDKS:ZDKS + 256] = df + 256 + _swap32(c)
    return idx


def _vt_rows(w_cols):
    n = w_cols.shape[0]
    w4 = w_cols.T.reshape(N_HEADS, 64, n)
    return jnp.pad(w4, ((0, 0), (0, VT_ROWS - 64), (0, 0))).reshape(N_HEADS * VT_ROWS, n)


_WIN_IDX = _win_index()


def _gather_cols(w, idx):
    safe = np.maximum(idx, 0)
    return jnp.where(jnp.asarray(idx >= 0)[None, :], w[:, safe], 0.0)


def _pad_heads_rows(w, width):
    n = w.shape[1]
    w4 = w.reshape(N_HEADS, width, n)
    return jnp.pad(w4, ((0, 0), (0, HEAD_SLAB - width), (0, 0))).reshape(N_HEADS * HEAD_SLAB, n)


def _cparams(sem):
    return pltpu.CompilerParams(dimension_semantics=sem, vmem_limit_bytes=VMEM_LIMIT)


def _rms(x):
    return x * lax.rsqrt(jnp.mean(x * x, axis=-1, keepdims=True) + EPS)


def _sigmoid(x):
    return 1.0 / (1.0 + jnp.exp(-x))


def _log_sigmoid(x):
    return jnp.minimum(x, 0.0) - jnp.log1p(jnp.exp(-jnp.abs(x)))


def _dot(a, b, precision=None):
    return jnp.dot(a, b, preferred_element_type=F32, precision=precision)


def _dot_nt(a, b, precision=None):
    return lax.dot_general(a, b, (((1,), (1,)), ((), ())), preferred_element_type=F32, precision=precision)


def _dot_tn(a, b, precision=None):
    return lax.dot_general(a, b, (((0,), (0,)), ((), ())), preferred_element_type=F32, precision=precision)


def _const_spec(shape):
    nd = len(shape)
    return pl.BlockSpec(shape, lambda *_: (0,) * nd)


def _ada_kernel(c_ref, w_ref, b_ref, o_ref):
    cv = c_ref[...]
    s = (cv * _sigmoid(cv)).astype(BF16)
    o_ref[...] = _dot(s, w_ref[...].astype(BF16)) + b_ref[...]


def _ada(c8, w_ada, b_ada):
    n, tn = 6 * D, 1024
    return pl.pallas_call(
        _ada_kernel, name="ada", grid=(n // tn,),
        in_specs=[pl.BlockSpec((8, D), lambda j: (0, 0)), pl.BlockSpec((D, tn), lambda j: (0, j)),
                  pl.BlockSpec((1, tn), lambda j: (0, j))],
        out_specs=pl.BlockSpec((8, tn), lambda j: (0, j)),
        out_shape=jax.ShapeDtypeStruct((8, n), F32), compiler_params=_cparams(("arbitrary",)),
    )(c8, w_ada, b_ada.reshape(1, n))


_PROJ_OUT = (
    ("mq", 512, BF16), ("mk", 512, BF16),
    ("lqk", 512, F32), ("lv", 256, BF16), ("lo", 256, BF16), ("gc", 128, F32),
    ("gq", 128, BF16), ("gk", 128, BF16), ("gv", 256, BF16), ("gr", 256, BF16), ("glg", 256, F32),
    ("dq", 1024, BF16), ("dk", 512, BF16),
)


def _proj_kernel(x_ref, mod_ref, g_ref, w_ref, tab_ref, gq_ref, wq_ref, wqs_ref, gkv_ref, wk_ref, wvt_ref,
                 wgt_ref, gbr_ref, gbc_ref, wal_ref, bal_ref, wdvt_ref,
                 mq_ref, mk_ref, lqk_ref, lv_ref, lo_ref, gc_ref, gq_o, gk_o, gv_o, gr_o, glg_o,
                 dq_ref, dk_ref, grow_ref, mvt_ref, dvt_ref):
    x = x_ref[0]
    tm = x.shape[0]
    mod = mod_ref[0]
    h = _rms(x) * g_ref[...] * (1.0 + mod[1:2]) + mod[0:1]
    hb = h.astype(BF16)
    z = _dot(hb, w_ref[...])
    tab = tab_ref[...]
    ct, st, cd, sd = tab[:, 0:128], tab[:, 128:256], tab[:, 256:384], tab[:, 384:512]
    lane = lax.broadcasted_iota(I32, (tm, LANE), 1)

    qn = (_rms(z[:, ZQ:ZQ + 256]) * gq_ref[...]).astype(BF16)
    qa = _dot(qn, wq_ref[...])
    qb = _dot(qn, wqs_ref[...])
    qscale = (MLA_NOPE + MLA_ROPE) ** -0.5 * LOG2E
    for hh in range(N_HEADS):
        sl = slice(HEAD_SLAB * hh, HEAD_SLAB * (hh + 1))
        mq_ref[0, :, sl] = ((qa[:, sl] * ct + qb[:, sl] * st) * qscale).astype(BF16)
    kvn = (_rms(z[:, ZKV:ZKV + 128]) * gkv_ref[...]).astype(BF16)
    kk = _dot(kvn, wk_ref[...])
    kr = z[:, ZKRA:ZKRA + 128] * ct + z[:, ZKRB:ZKRB + 128] * st
    for hh in range(N_HEADS):
        sl = slice(HEAD_SLAB * hh, HEAD_SLAB * (hh + 1))
        mk_ref[0, :, sl] = (kk[:, sl] + kr).astype(BF16)
    ones_row = lax.broadcasted_iota(I32, (N_HEADS * VT_ROWS, tm), 0) % VT_ROWS == MLA_V
    mvt_ref[0, 0] = jnp.where(ones_row, 1.0, _dot_nt(wvt_ref[...], kvn)).astype(BF16)
    dvt_ref[0, 0] = jnp.where(ones_row, 1.0, _dot_nt(wdvt_ref[...], hb)).astype(BF16)

    lqk_ref[0] = z[:, ZMLQK:ZMLQK + 512]
    lv_ref[0] = z[:, ZMLV:ZMLV + 256].astype(BF16)
    lo_ref[0] = z[:, ZMLO:ZMLO + 256].astype(BF16)
    gcol = z[:, ZGATE:ZGATE + 128] + gbr_ref[...]
    gc_ref[0] = jnp.where(lane < 8, gcol, jnp.where(lane < 16, _log_sigmoid(gcol), 0.0))
    zr = _dot_nt(wgt_ref[...], hb) + gbc_ref[...]
    rowi = lax.broadcasted_iota(I32, zr.shape, 0)
    grow_ref[0] = jnp.where(rowi < 8, zr, _log_sigmoid(zr))

    gq_o[0] = (z[:, ZGQ:ZGQ + 128] * GLA_DK ** -0.5).astype(BF16)
    gk_o[0] = z[:, ZGK:ZGK + 128].astype(BF16)
    gv_o[0] = z[:, ZGV:ZGV + 256].astype(BF16)
    gr_o[0] = z[:, ZGR:ZGR + 256].astype(BF16)
    zg = _dot(z[:, ZGA:ZGA + 128].astype(BF16), wal_ref[...]) + bal_ref[...]
    glg_o[0] = _log_sigmoid(zg) * (1.0 / GLA_TAU)

    dscale = DIFF_DQK ** -0.5 * LOG2E
    for g in range(2):
        gs = slice(128 * g, 128 * (g + 1))
        qg = (z[:, ZDQ:ZDQ + 256][:, gs] * cd + z[:, ZDQS:ZDQS + 256][:, gs] * sd) * dscale
        kg = z[:, ZDK:ZDK + 256][:, gs] * cd + z[:, ZDKS:ZDKS + 256][:, gs] * sd
        for hl in range(2):
            hh = 2 * g + hl
            for m in range(2):
                lo = 64 * hl + 32 * m
                s0 = (2 * hh + m) * HEAD_SLAB
                dq_ref[0, :, s0:s0 + HEAD_SLAB] = jnp.where((lane >= lo) & (lane < lo + 32), qg, 0.0).astype(BF16)
            dk_ref[0, :, HEAD_SLAB * hh:HEAD_SLAB * (hh + 1)] = jnp.where(
                (lane >= 64 * hl) & (lane < 64 * hl + 64), kg, 0.0).astype(BF16)


def _proj(x, mod, tab, lw):
    b, t, _ = x.shape
    tm = KV_CHUNK
    consts = [lw["g_mix_pre"], lw["w_ext"], None, lw["g_q_lat"], lw["wq"], lw["wqs"], lw["g_kv_lat"], lw["wk"], lw["wvt"],
              lw["wgt"], lw["gate_bias_row"], lw["gate_bias_col"], lw["walpha"], lw["balpha"], lw["wdvt"]]
    in_specs = [pl.BlockSpec((1, tm, D), lambda bi, i: (bi, i, 0)), pl.BlockSpec((1, 8, D), lambda bi, i: (bi, 0, 0))]
    args = [x, mod]
    for cst in consts:
        if cst is None:
            in_specs.append(pl.BlockSpec((tm, 512), lambda bi, i: (i, 0)))
            args.append(tab)
        else:
            in_specs.append(_const_spec(cst.shape))
            args.append(cst)
    out_specs = [pl.BlockSpec((1, tm, w), lambda bi, i: (bi, i, 0)) for _, w, _ in _PROJ_OUT]
    out_shape = [jax.ShapeDtypeStruct((b, t, w), dt) for _, w, dt in _PROJ_OUT]
    out_specs.append(pl.BlockSpec((1, 16, tm), lambda bi, i: (bi, 0, i)))
    out_shape.append(jax.ShapeDtypeStruct((b, 16, t), F32))
    for _ in range(2):
        out_specs.append(pl.BlockSpec((1, 1, N_HEADS * VT_ROWS, tm), lambda bi, i: (bi, i, 0, 0)))
        out_shape.append(jax.ShapeDtypeStruct((b, t // tm, N_HEADS * VT_ROWS, tm), BF16))
    outs = pl.pallas_call(
        _proj_kernel, name="proj", grid=(b, t // tm), in_specs=in_specs, out_specs=out_specs, out_shape=out_shape,
        compiler_params=_cparams(("parallel", "arbitrary")),
    )(*args)
    res = {name: o for (name, _, _), o in zip(_PROJ_OUT, outs[:-3])}
    res["grow"], res["mvt"], res["dvt"] = outs[-3:]
    return res


def _flash_kernel(*refs, nmap, has_lat, tk, finish, post):
    if has_lat:
        q_ref, kc_ref, vc_ref, kl_ref, vl_ref, dl_ref, g_ref, o_ref, s_ref = refs
    else:
        q_ref, kc_ref, vc_ref, dl_ref, g_ref, o_ref = refs
    qb = q_ref[0]
    tq = qb.shape[0]
    q = qb if nmap == 1 else jnp.concatenate([qb[:, :HEAD_SLAB], qb[:, HEAD_SLAB:]], axis=0)
    rows = nmap * tq
    sub = tk // KV_CHUNK

    def update(s, vts, m, acc):
        m_new = jnp.maximum(m, jnp.max(s, axis=0, keepdims=True))
        p = jnp.exp2(s - m_new).astype(BF16)
        pv = _dot(vts[0], p[0:KV_CHUNK])
        for c in range(1, len(vts)):
            pv = pv + _dot(vts[c], p[c * KV_CHUNK:(c + 1) * KV_CHUNK])
        return m_new, jnp.exp2(m - m_new) * acc + pv

    m = jnp.full((1, rows), NEG, F32)
    acc = jnp.zeros((VT_ROWS, rows), F32)
    m, acc = update(_dot_nt(kc_ref[0], q), [vc_ref[0, 0]], m, acc)
    if has_lat:
        n = kl_ref.shape[1] // tk

        def scores(j):
            off = pl.multiple_of(j * tk, tk)
            return _dot_nt(kl_ref[0, pl.ds(off, tk), :], q)

        def values(j):
            return [vl_ref[0, j * sub + c] for c in range(sub)]

        s_ref[0] = scores(0)

        def body(jj, carry):
            m, acc = carry
            j = 2 * jj
            s_ref[1] = scores(j + 1)
            m, acc = update(s_ref[0], values(j), m, acc)
            s_ref[0] = scores(jnp.minimum(j + 2, n - 1))
            m, acc = update(s_ref[1], values(j + 1), m, acc)
            return m, acc

        m, acc = lax.fori_loop(0, n // 2, body, (m, acc))

    o = acc[0:MLA_V, :] / acc[MLA_V:MLA_V + 1, :]
    if nmap == 2:
        lv = dl_ref[...]
        lam = (jnp.exp(jnp.sum(lv[0:1] * lv[1:2], axis=-1, keepdims=True))
               - jnp.exp(jnp.sum(lv[2:3] * lv[3:4], axis=-1, keepdims=True)) + (1.0 - post))
        o = o[:, :tq] - lam * o[:, tq:]
    if finish:
        ms = jnp.mean(o * o, axis=0, keepdims=True)
        o = o * lax.rsqrt(ms + EPS) * jnp.concatenate([g_ref[0]] * (tq // LANE), axis=1) * post
    o_pad = jnp.concatenate([o, jnp.zeros((HEAD_SLAB - MLA_V, tq), F32)], axis=0)
    o_ref[0] = o_pad.T.astype(BF16)


def _flash(q, kc, vct, kl, vlt, dlam, g_out, *, nmap, finish, post, tq, tk):
    b, t, _ = q.shape
    has_lat = kl is not None
    assert kc.shape[1] == KV_CHUNK and tk % KV_CHUNK == 0
    qw = nmap * HEAD_SLAB
    kspec = lambda n: pl.BlockSpec((1, n, HEAD_SLAB), lambda bi, h, i: (bi, 0, h))
    vspec = lambda n: pl.BlockSpec((1, n // KV_CHUNK, VT_ROWS, KV_CHUNK), lambda bi, h, i: (bi, 0, h, 0))
    in_specs = [pl.BlockSpec((1, tq, qw), lambda bi, h, i: (bi, i, h)), kspec(KV_CHUNK), vspec(KV_CHUNK)]
    args = [q, kc, vct]
    scratch = []
    if has_lat:
        tl = kl.shape[1]
        assert tl % (2 * tk) == 0
        in_specs += [kspec(tl), vspec(tl)]
        args += [kl, vlt]
        scratch = [pltpu.VMEM((2, tk, nmap * tq), F32)]
    in_specs += [_const_spec(dlam.shape), pl.BlockSpec((1, MLA_V, LANE), lambda bi, h, i: (h, 0, 0))]
    args += [dlam, g_out]
    return pl.pallas_call(
        functools.partial(_flash_kernel, nmap=nmap, has_lat=has_lat, tk=tk, finish=finish, post=post),
        name="flash_diff" if nmap == 2 else "flash_mla",
        grid=(b, N_HEADS, t // tq), in_specs=in_specs,
        out_specs=pl.BlockSpec((1, tq, HEAD_SLAB), lambda bi, h, i: (bi, i, h)),
        out_shape=jax.ShapeDtypeStruct((b, t, N_HEADS * HEAD_SLAB), BF16),
        scratch_shapes=scratch,
        compiler_params=_cparams(("parallel", "parallel", "arbitrary")),
    )(*args)


def _head_of(shape, axis, width):
    return (lax.broadcasted_iota(I32, shape, axis) % (N_HEADS * width)) // width


def _mlstm_dir(d, first, last, x, xprev, xnext, v, gcol, grow, wc, bcv, c_ref, m_ref):
    L = x.shape[0]
    row = lax.broadcasted_iota(I32, x.shape, 0)
    pr = jnp.where(first, 0.0, xprev[7:8, :])
    nx = jnp.where(last, 0.0, xnext[0:1, :])
    xm = jnp.where(row == 0, pr, pltpu.roll(x, 1, 0))
    xp = jnp.where(row == L - 1, nx, pltpu.roll(x, L - 1, 0))
    y = xm * wc[0:1] + x * wc[1:2] + xp * wc[2:3] + bcv
    qk = y * _sigmoid(y)
    q = qk[:, :256]
    k = qk[:, 256:] * ML_DH ** -0.5

    li = lax.broadcasted_iota(I32, (L, L), 0)
    si = lax.broadcasted_iota(I32, (L, L), 1)
    tin = (si <= li) if d == 0 else (si >= li)
    tinf = tin.astype(F32)
    bcol = _dot(tinf, gcol, HIGHEST)
    brow = _dot_nt(grow, tinf, HIGHEST)
    m0e = m_ref[0, d]
    cb = c_ref[0, d]
    hm256 = _head_of((L, 256), 1, ML_DH)
    hm512 = _head_of((L, 512), 1, ML_DH)
    e_idx = L - 1 if d == 0 else 0

    d_blk, inter_blk = [], []
    for hh in range(N_HEADS):
        c = 4 * d + hh
        bc = bcol[:, 8 + c:9 + c]
        d_blk.append(jnp.where(tin, bc - brow[8 + c:9 + c, :] + grow[c:c + 1, :], NEG))
        inter_blk.append(bc + m0e[0:1, 64 * hh:64 * hh + 1])
    d_st = jnp.concatenate(d_blk, axis=0)
    inter_st = jnp.concatenate(inter_blk, axis=0)
    mt = jnp.maximum(inter_st, jnp.max(d_st, axis=-1, keepdims=True))
    q_st = jnp.concatenate([jnp.where(hm256 == hh, q, 0.0) for hh in range(N_HEADS)], axis=0).astype(BF16)
    s_st = (jnp.exp(d_st - mt) * _dot_nt(q_st, k.astype(BF16))).astype(BF16)
    vext = jnp.concatenate([v, jnp.ones((L, 256), BF16)], axis=1)
    r = _dot(s_st, vext)
    aint = jnp.exp(inter_st - mt)
    p = _dot(q.astype(BF16), cb.astype(BF16))
    tot = jnp.zeros((L, 512), F32)
    mte = jnp.zeros((L, 256), F32)
    for hh in range(N_HEADS):
        rs = slice(hh * L, (hh + 1) * L)
        tot = jnp.where(hm512 == hh, r[rs] + aint[rs] * p, tot)
        mte = jnp.where(hm256 == hh, mt[rs], mte)
    hout = tot[:, :256] / jnp.maximum(jnp.abs(tot[:, 256:]), jnp.exp(-mte))

    wexp = jnp.zeros((L, 256), F32)
    arow = jnp.zeros((1, 512), F32)
    grw = jnp.zeros((1, 512), F32)
    mnew = jnp.zeros((1, 256), F32)
    hr512 = _head_of((1, 512), 1, ML_DH)
    hr256 = _head_of((1, 256), 1, ML_DH)
    for hh in range(N_HEADS):
        c = 4 * d + hh
        bc = bcol[:, 8 + c:9 + c]
        be = bc[e_idx:e_idx + 1, :]
        wl = be - bc + gcol[:, c:c + 1]
        mloc = jnp.max(wl, axis=0, keepdims=True)
        m0h = m0e[0:1, 64 * hh:64 * hh + 1]
        mn = jnp.maximum(be + m0h, mloc)
        wexp = jnp.where(hm256 == hh, jnp.exp(wl - mloc), wexp)
        arow = jnp.where(hr512 == hh, jnp.exp(be + m0h - mn), arow)
        grw = jnp.where(hr512 == hh, jnp.exp(mloc - mn), grw)
        mnew = jnp.where(hr256 == hh, mn, mnew)
    cl = _dot_tn((k * wexp).astype(BF16), vext)
    bd = lax.broadcasted_iota(I32, (256, 512), 0) // ML_DH == _head_of((256, 512), 1, ML_DH)
    c_ref[0, d] = arow * cb + jnp.where(bd, grw * cl, 0.0)
    m_ref[0, d] = jnp.broadcast_to(mnew, (8, 256))
    return hout


def _mlstm_kernel(xf, xfp, xfn, xb, xbp, xbn, vf, vb, gcf, gcb, grf, grb, wc_ref, bc_ref, c0_ref, m0_ref,
                  hf_ref, hb_ref, c_ref, m_ref):
    i = pl.program_id(1)
    n = pl.num_programs(1)

    @pl.when(i == 0)
    def _():
        c_ref[...] = c0_ref[...]
        m_ref[...] = m0_ref[...]

    wc = wc_ref[...]
    bcv = bc_ref[...]
    hf_ref[0] = _mlstm_dir(0, i == 0, i == n - 1, xf[0], xfp[0], xfn[0], vf[0], gcf[0], grf[0], wc, bcv, c_ref, m_ref)
    hb_ref[0] = _mlstm_dir(1, i == n - 1, i == 0, xb[0], xbp[0], xbn[0], vb[0], gcb[0], grb[0], wc, bcv, c_ref, m_ref)


def _mlstm(pr, w_conv, b_conv, c0, m0):
    x, v, gc, gr = pr["lqk"], pr["lv"], pr["gc"], pr["grow"]
    b, t, _ = x.shape
    L = ML_CHUNK
    n = t // L
    r8 = L // 8
    last8 = t // 8 - 1

    def fw(bi, i):
        return (bi, i, 0)

    def bw(bi, i):
        return (bi, n - 1 - i, 0)

    def halo(ix, shift):
        def f(bi, i):
            blk = ix(bi, i)[1]
            return (bi, jnp.clip(blk * r8 + shift, 0, last8), 0)
        return f

    main = lambda w, ix: pl.BlockSpec((1, L, w), ix)
    in_specs = [main(512, fw), pl.BlockSpec((1, 8, 512), halo(fw, -1)), pl.BlockSpec((1, 8, 512), halo(fw, r8)),
                main(512, bw), pl.BlockSpec((1, 8, 512), halo(bw, -1)), pl.BlockSpec((1, 8, 512), halo(bw, r8)),
                main(256, fw), main(256, bw), main(128, fw), main(128, bw),
                pl.BlockSpec((1, 16, L), lambda bi, i: (bi, 0, i)), pl.BlockSpec((1, 16, L), lambda bi, i: (bi, 0, n - 1 - i)),
                _const_spec(w_conv.shape), _const_spec(b_conv.shape),
                pl.BlockSpec((1, 2, 256, 512), lambda bi, i: (bi, 0, 0, 0)), pl.BlockSpec((1, 2, 8, 256), lambda bi, i: (bi, 0, 0, 0))]
    out_specs = [main(256, fw), main(256, bw),
                 pl.BlockSpec((1, 2, 256, 512), lambda bi, i: (bi, 0, 0, 0)), pl.BlockSpec((1, 2, 8, 256), lambda bi, i: (bi, 0, 0, 0))]
    out_shape = [jax.ShapeDtypeStruct((b, t, 256), F32), jax.ShapeDtypeStruct((b, t, 256), F32),
                 jax.ShapeDtypeStruct(c0.shape, F32), jax.ShapeDtypeStruct(m0.shape, F32)]
    return pl.pallas_call(
        _mlstm_kernel, name="mlstm", grid=(b, n), in_specs=in_specs, out_specs=out_specs, out_shape=out_shape,
        compiler_params=_cparams(("parallel", "arbitrary")),
    )(x, x, x, x, x, x, v, v, gc, gc, gr, gr, w_conv, b_conv, c0, m0)


def _gla_dir(d, q, k, v, lg, s_ref):
    L = q.shape[0]
    li = lax.broadcasted_iota(I32, (L, L), 0)
    si = lax.broadcasted_iota(I32, (L, L), 1)
    tin = (si <= li) if d == 0 else (si >= li)
    tinf = tin.astype(F32)
    lgd = lg[:, 128 * d:128 * (d + 1)]
    gcum = _dot(tinf, lgd, HIGHEST)
    e_idx = L - 1 if d == 0 else 0
    gend = gcum[e_idx:e_idx + 1, :]
    qf, kf = q.astype(F32), k.astype(F32)
    q_dec = qf * jnp.exp(gcum)
    k_dec = (kf * jnp.exp(-gcum)).astype(BF16)
    k_end = (kf * jnp.exp(gend - gcum)).astype(BF16)
    hm128 = _head_of((L, 128), 1, GLA_DK)
    hm256 = _head_of((L, 256), 1, GLA_DV)
    q_st = jnp.concatenate([jnp.where(hm128 == hh, q_dec, 0.0) for hh in range(N_HEADS)], axis=0).astype(BF16)
    att = _dot_nt(q_st, k_dec)
    tin4 = jnp.concatenate([tin] * N_HEADS, axis=0)
    o_st = _dot(jnp.where(tin4, att, 0.0).astype(BF16), v)
    sb = s_ref[0, d]
    o = _dot(q_dec.astype(BF16), sb.astype(BF16))
    for hh in range(N_HEADS):
        o = o + jnp.where(hm256 == hh, o_st[hh * L:(hh + 1) * L], 0.0)
    gend_col = _dot_tn(lgd, jnp.ones((L, 256), F32), HIGHEST)
    bd = lax.broadcasted_iota(I32, (128, 256), 0) // GLA_DK == _head_of((128, 256), 1, GLA_DV)
    s_ref[0, d] = jnp.exp(gend_col) * sb + jnp.where(bd, _dot_tn(k_end, v), 0.0)
    return o


def _gla_kernel(qf, kf, vf, lf, qb, kb, vb, lb, s0_ref, of_ref, ob_ref, s_ref):
    i = pl.program_id(1)

    @pl.when(i == 0)
    def _():
        s_ref[...] = s0_ref[...]

    of_ref[0] = _gla_dir(0, qf[0], kf[0], vf[0], lf[0], s_ref)
    ob_ref[0] = _gla_dir(1, qb[0], kb[0], vb[0], lb[0], s_ref)


def _gla(pr, s0):
    q, k, v, lg = pr["gq"], pr["gk"], pr["gv"], pr["glg"]
    b, t, _ = q.shape
    L = GLA_CHUNK
    n = t // L
    fw = lambda bi, i: (bi, i, 0)
    bw = lambda bi, i: (bi, n - 1 - i, 0)
    blk = lambda w, ix: pl.BlockSpec((1, L, w), ix)
    st_spec = pl.BlockSpec((1, 2, 128, 256), lambda bi, i: (bi, 0, 0, 0))
    return pl.pallas_call(
        _gla_kernel, name="gla", grid=(b, n),
        in_specs=[blk(128, fw), blk(128, fw), blk(256, fw), blk(256, fw),
                  blk(128, bw), blk(128, bw), blk(256, bw), blk(256, bw), st_spec],
        out_specs=[blk(256, fw), blk(256, bw), st_spec],
        out_shape=[jax.ShapeDtypeStruct((b, t, 256), F32), jax.ShapeDtypeStruct((b, t, 256), F32),
                   jax.ShapeDtypeStruct(s0.shape, F32)],
        compiler_params=_cparams(("parallel", "arbitrary")),
    )(q, k, v, lg, q, k, v, lg, s0)


def _head_rms_expanded(x, width):
    n = x.shape[1]
    bd = (lax.broadcasted_iota(I32, (n, n), 0) // width == lax.broadcasted_iota(I32, (n, n), 1) // width).astype(F32)
    return _dot(x * x, bd, HIGHEST) * (1.0 / width)


def _merge_kernel(x_ref, mod_ref, ya_ref, hf_ref, hb_ref, lo_ref, gf_ref, gb_ref, gr_ref, yd_ref,
                  gpre_ref, wg_ref, bg_ref, wbr_ref, wo_ref, gpost_ref, gffn_ref, wr_ref, wrt_ref, gml_ref, ggla_ref,
                  xm_ref, h2_ref, aff_ref, afft_ref):
    x = x_ref[0]
    tm = x.shape[0]
    mod = mod_ref[0]
    hb = (_rms(x) * gpre_ref[...] * (1.0 + mod[1:2]) + mod[0:1]).astype(BF16)

    hs = hf_ref[0] + hb_ref[0]
    y_ml = _sigmoid(lo_ref[0].astype(F32)) * (hs * lax.rsqrt(_head_rms_expanded(hs, ML_DH) + EPS) * gml_ref[...])
    gs = gf_ref[0] + gb_ref[0]
    rr = gr_ref[0].astype(F32)
    y_gla = rr * _sigmoid(rr) * (gs * lax.rsqrt(_head_rms_expanded(gs, GLA_DV) + EPS) * ggla_ref[...])

    branches = ((ya_ref[0], 0, 512), (y_ml.astype(BF16), 512, 256), (y_gla.astype(BF16), 768, 256), (yd_ref[0], 1024, 512))
    mix = jnp.zeros((tm, D), F32)
    for nb, (yb, r0, rw) in enumerate(branches):
        gate = _sigmoid(_dot(hb, wg_ref[:, nb * D:(nb + 1) * D]) + bg_ref[:, nb * D:(nb + 1) * D])
        mix = mix + gate * _dot(yb, wbr_ref[r0:r0 + rw, :])
    y = _dot(mix.astype(BF16), wo_ref[...])
    xm = x + mod[2:3] * (_rms(y) * gpost_ref[...])
    xm_ref[0] = xm

    h2 = (_rms(xm) * gffn_ref[...] * (1.0 + mod[4:5]) + mod[3:4]).astype(BF16)
    h2_ref[0] = h2
    lane = lax.broadcasted_iota(I32, (tm, LANE), 1)
    lg = jnp.where(lane < N_EXPERTS, _dot(h2, wr_ref[...]), NEG)
    e = jnp.exp(lg - jnp.max(lg, axis=-1, keepdims=True))
    aff_ref[0] = (e / jnp.sum(e, axis=-1, keepdims=True))[:, :N_EXPERTS]
    lt = _dot_nt(wrt_ref[...], h2)
    et = jnp.exp(lt - jnp.max(lt, axis=0, keepdims=True))
    afft_ref[0] = et / jnp.sum(et, axis=0, keepdims=True)


def _merge(x, mod, ya, ml, lo, gl, gr, yd, lw, tm):
    b, t, _ = x.shape
    tok = lambda w: pl.BlockSpec((1, tm, w), lambda bi, i: (bi, i, 0))
    consts = [lw["g_mix_pre"], lw["w_gate"], lw["b_gate"], lw["wbr"], lw["w_out"], lw["g_mix_post"], lw["g_ffn_pre"],
              lw["w_router"], lw["w_router_t"], lw["g_mlstm_out"], lw["g_gla_out"]]
    in_specs = [tok(D), pl.BlockSpec((1, 8, D), lambda bi, i: (bi, 0, 0)), tok(512), tok(256), tok(256), tok(256),
                tok(256), tok(256), tok(256), tok(512)] + [_const_spec(c.shape) for c in consts]
    out_specs = [tok(D), tok(D), tok(N_EXPERTS), pl.BlockSpec((1, N_EXPERTS, tm), lambda bi, i: (bi, 0, i))]
    out_shape = [jax.ShapeDtypeStruct((b, t, D), F32), jax.ShapeDtypeStruct((b, t, D), BF16),
                 jax.ShapeDtypeStruct((b, t, N_EXPERTS), F32), jax.ShapeDtypeStruct((b, N_EXPERTS, t), F32)]
    return pl.pallas_call(
        _merge_kernel, name="merge", grid=(b, t // tm), in_specs=in_specs, out_specs=out_specs, out_shape=out_shape,
        compiler_params=_cparams(("parallel", "arbitrary")),
    )(x, mod, ya, ml[0], ml[1], lo, gl[0], gl[1], gr, yd, *consts)


def _topk_kernel(a_ref, pos_ref, s0_ref, *, cap):
    nblk = a_ref.shape[1]
    bits = pltpu.bitcast(a_ref[0], I32)

    def bisect(i, thr):
        cand = thr | (1 << (30 - i))
        cnt = jnp.sum((bits >= cand).astype(I32), axis=(0, 2), keepdims=True)
        return jnp.where(cnt >= cap, cand, thr)

    thr3 = lax.fori_loop(0, 31, bisect, jnp.zeros((1, N_EXPERTS, 1), I32))
    need3 = cap - jnp.sum((bits > thr3).astype(I32), axis=(0, 2), keepdims=True)
    thr, need = thr3[0], need3[0].astype(F32)
    upper = (lax.broadcasted_iota(I32, (TOK_BLK, TOK_BLK), 0) <= lax.broadcasted_iota(I32, (TOK_BLK, TOK_BLK), 1)).astype(BF16)

    def blk(j, carry):
        c_eq, c_sel = carry
        bj = pltpu.bitcast(a_ref[0, j], I32)
        gt, eq = bj > thr, bj == thr
        cum_eq = _dot(eq.astype(BF16), upper) + c_eq
        sel = gt | (eq & (cum_eq <= need))
        cum_sel = _dot(sel.astype(BF16), upper) + c_sel
        pos_ref[0, j] = jnp.where(sel, cum_sel - 1.0, -1.0).astype(I32)
        s0_ref[0, j] = jnp.broadcast_to(c_sel, (N_EXPERTS, LANE)).astype(I32)
        return cum_eq[:, TOK_BLK - 1:TOK_BLK], cum_sel[:, TOK_BLK - 1:TOK_BLK]

    zero = jnp.zeros((N_EXPERTS, 1), F32)
    lax.fori_loop(0, nblk, blk, (zero, zero))


def _topk(aff_t, cap):
    b, _, t = aff_t.shape
    nblk = t // TOK_BLK
    a4 = aff_t.reshape(b, N_EXPERTS, nblk, TOK_BLK).transpose(0, 2, 1, 3)
    spec = lambda w: pl.BlockSpec((1, nblk, N_EXPERTS, w), lambda bi: (bi, 0, 0, 0))
    return pl.pallas_call(
        functools.partial(_topk_kernel, cap=cap), name="topk", grid=(b,),
        in_specs=[spec(TOK_BLK)], out_specs=[spec(TOK_BLK), spec(LANE)],
        out_shape=[jax.ShapeDtypeStruct((b, nblk, N_EXPERTS, TOK_BLK), I32),
                   jax.ShapeDtypeStruct((b, nblk, N_EXPERTS, LANE), I32)],
        compiler_params=_cparams(("parallel",)),
    )(a4)


def _moe_kernel(s0_ref, pos_ref, h_ref, wg_ref, wu_ref, wd_ref, ys_ref, xs_ref, *, nblk, capp):
    bi, e, tb = pl.program_id(0), pl.program_id(1), pl.program_id(2)

    @pl.when(tb == 0)
    def _():
        xs_ref[...] = jnp.zeros(xs_ref.shape, BF16)

    s0 = s0_ref[(bi * N_EXPERTS + e) * nblk + tb]
    a0 = pl.multiple_of((s0 // 16) * 16, 16)
    prow = pos_ref[0, 0, pl.ds(e, 1), :]
    slot = lax.broadcasted_iota(I32, (GATHER_WIN, TOK_BLK), 0) + a0
    onehot = (slot == prow).astype(BF16)
    rows = _dot(onehot, h_ref[0]).astype(BF16)
    xs_ref[pl.ds(a0, GATHER_WIN), :] = xs_ref[pl.ds(a0, GATHER_WIN), :] + rows

    @pl.when(tb == nblk - 1)
    def _():
        def chunk(c, carry):
            off = pl.multiple_of(c * TOK_BLK, TOK_BLK)
            xc = xs_ref[pl.ds(off, TOK_BLK), :]
            hg = _dot(xc, wg_ref[0])
            hid = (hg * _sigmoid(hg) * _dot(xc, wu_ref[0])).astype(BF16)
            ys_ref[0, 0, pl.ds(off, TOK_BLK), :] = _dot(hid, wd_ref[0]).astype(BF16)
            return carry
        lax.fori_loop(0, capp // TOK_BLK, chunk, 0)


def _moe(s0_flat, pos4, h2, lw, capp):
    b, t, _ = h2.shape
    nblk = t // TOK_BLK
    grid_spec = pltpu.PrefetchScalarGridSpec(
        num_scalar_prefetch=1, grid=(b, N_EXPERTS, nblk),
        in_specs=[pl.BlockSpec((1, 1, N_EXPERTS, TOK_BLK), lambda bi, e, tb, s: (bi, tb, 0, 0)),
                  pl.BlockSpec((1, TOK_BLK, D), lambda bi, e, tb, s: (bi, tb, 0)),
                  pl.BlockSpec((1, D, EXPERT_FF), lambda bi, e, tb, s: (e, 0, 0)),
                  pl.BlockSpec((1, D, EXPERT_FF), lambda bi, e, tb, s: (e, 0, 0)),
                  pl.BlockSpec((1, EXPERT_FF, D), lambda bi, e, tb, s: (e, 0, 0))],
        out_specs=pl.BlockSpec((1, 1, capp, D), lambda bi, e, tb, s: (bi, e, 0, 0)),
        scratch_shapes=[pltpu.VMEM((capp + GATHER_WIN, D), BF16)])
    return pl.pallas_call(
        functools.partial(_moe_kernel, nblk=nblk, capp=capp), name="moe", grid_spec=grid_spec,
        out_shape=jax.ShapeDtypeStruct((b, N_EXPERTS, capp, D), BF16),
        compiler_params=_cparams(("arbitrary", "arbitrary", "arbitrary")),
    )(s0_flat, pos4, h2, lw["w_e_gate"], lw["w_e_up"], lw["w_e_down"])


def _combine_kernel(s0_ref, *refs, nblk, nb):
    ys_refs = refs[:2 * N_EXPERTS]
    pos_ref, aff_ref, xm_ref, mod_ref, g_ref, o_ref = refs[2 * N_EXPERTS:]
    bi, tb = pl.program_id(0), pl.program_id(1)
    pos = pos_ref[0]
    aff = aff_ref[0]
    lane = lax.broadcasted_iota(I32, (TOK_BLK, TOK_BLK), 1)
    acc = jnp.zeros((TOK_BLK, D), F32)
    for e in range(N_EXPERTS):
        s0 = s0_ref[(bi * N_EXPERTS + e) * nblk + tb]
        blk0 = jnp.minimum(s0 // TOK_BLK, nb - 1)
        rel = pos[:, e:e + 1] - blk0 * TOK_BLK
        y = (_dot((lane == rel).astype(BF16), ys_refs[2 * e][0, 0])
             + _dot((lane + TOK_BLK == rel).astype(BF16), ys_refs[2 * e + 1][0, 0]))
        acc = acc + aff[:, e:e + 1] * y
    mod = mod_ref[0]
    o_ref[0] = xm_ref[0] + mod[5:6] * (_rms(acc) * g_ref[...])


def _combine(s0_flat, ys, pos_t, aff, xm, mod, g_post):
    b, t, _ = xm.shape
    nblk = t // TOK_BLK
    nb = ys.shape[2] // TOK_BLK

    def ys_spec(e, k):
        def ix(bi, tb, s):
            blk0 = jnp.minimum(s[(bi * N_EXPERTS + e) * nblk + tb] // TOK_BLK, nb - 1)
            return (bi, e, jnp.minimum(blk0 + k, nb - 1), 0)
        return pl.BlockSpec((1, 1, TOK_BLK, D), ix)

    tok = lambda w: pl.BlockSpec((1, TOK_BLK, w), lambda bi, tb, s: (bi, tb, 0))
    in_specs = [ys_spec(e, k) for e in range(N_EXPERTS) for k in range(2)]
    in_specs += [tok(N_EXPERTS), tok(N_EXPERTS), tok(D), pl.BlockSpec((1, 8, D), lambda bi, tb, s: (bi, 0, 0)),
                 pl.BlockSpec((1, D), lambda bi, tb, s: (0, 0))]
    grid_spec = pltpu.PrefetchScalarGridSpec(num_scalar_prefetch=1, grid=(b, nblk), in_specs=in_specs, out_specs=tok(D))
    return pl.pallas_call(
        functools.partial(_combine_kernel, nblk=nblk, nb=nb), name="combine", grid_spec=grid_spec,
        out_shape=jax.ShapeDtypeStruct((b, t, D), F32),
        compiler_params=_cparams(("arbitrary", "arbitrary")),
    )(s0_flat, *([ys] * (2 * N_EXPERTS)), pos_t, aff, xm, mod, g_post)


def _rope_table(t):
    nf = ROPE_DIM // 4
    pos = jnp.arange(t)
    inv = ROPE_BASE ** (-jnp.arange(nf, dtype=F32) / nf)
    ang = jnp.stack([pos // GRID_W, pos % GRID_W], axis=-1).astype(F32)[..., None] * inv
    cos, sin = jnp.cos(ang), jnp.sin(ang)
    c32 = jnp.stack([cos, cos], axis=2).reshape(t, ROPE_DIM)
    s32 = jnp.stack([-sin, sin], axis=2).reshape(t, ROPE_DIM)
    one, zero = jnp.ones((t, 64), F32), jnp.zeros((t, 32), F32)
    ct = jnp.concatenate([one, c32, zero], axis=1)
    st = jnp.concatenate([0.0 * one, s32, zero], axis=1)
    return jnp.concatenate([ct, st, jnp.tile(c32, (1, 4)), jnp.tile(s32, (1, 4))], axis=1)


def _identity_table(t):
    one, zero = jnp.ones((t, 128), F32), jnp.zeros((t, 128), F32)
    ct = jnp.concatenate([jnp.ones((t, 96), F32), jnp.zeros((t, 32), F32)], axis=1)
    return jnp.concatenate([ct, zero, one, zero], axis=1)


def _layer_weights(i, p):
    lw = {}
    row = lambda a: a.reshape(1, -1)
    for name in ("g_mix_pre", "g_mix_post", "g_ffn_pre", "g_ffn_post", "g_q_lat", "g_kv_lat", "g_mlstm_out", "g_gla_out"):
        lw[name] = row(p[name][i])
    lw["w_ext"] = _gather_cols(p["w_in"][i], _WIN_IDX).astype(BF16)
    lw["wgt"] = p["w_in"][i][:, 416 + 1024:416 + 1040].T.astype(BF16)
    gb = jnp.concatenate([p["b_igate"][i].reshape(-1), p["b_fgate"][i].reshape(-1)])
    lw["gate_bias_row"] = jnp.pad(gb, (0, LANE - 16)).reshape(1, LANE)
    lw["gate_bias_col"] = gb.reshape(16, 1)
    wuq = p["w_uq"][i]
    qi = -np.ones((512,), np.int64)
    qsi = -np.ones((512,), np.int64)
    for h in range(N_HEADS):
        qi[128 * h:128 * h + 96] = 96 * h + np.arange(96)
        qsi[128 * h + 64:128 * h + 96] = 96 * h + 64 + (np.arange(32) ^ 8)
    lw["wq"] = _gather_cols(wuq, qi).astype(BF16)
    lw["wqs"] = _gather_cols(wuq, qsi).astype(BF16)
    ki = -np.ones((512,), np.int64)
    for h in range(N_HEADS):
        ki[128 * h:128 * h + 64] = 128 * h + np.arange(64)
    lw["wk"] = _gather_cols(p["w_ukv"][i], ki).astype(BF16)
    lw["wvt"] = _vt_rows(p["w_ukv"][i].reshape(MLA_KV_LORA, N_HEADS, 128)[:, :, 64:].reshape(MLA_KV_LORA, 256)).astype(BF16)
    lw["wdvt"] = _vt_rows(p["w_in"][i][:, 2768:3024]).astype(BF16)
    wa = p["w_alpha2"][i]
    wal = jnp.zeros((LANE, 256), F32).at[0:16, 0:128].set(wa[0]).at[16:32, 128:256].set(wa[1])
    lw["walpha"] = wal.astype(BF16)
    lw["balpha"] = p["b_alpha"][i].reshape(1, 256)
    lw["w_conv"] = p["w_conv"][i]
    lw["b_conv"] = row(p["b_conv"][i])
    lw["dlam"] = p["diff_lambda"][i]
    lw["g_diff"] = jnp.broadcast_to(p["g_diff_out"][i].reshape(N_HEADS, DIFF_DV, 1), (N_HEADS, DIFF_DV, LANE))
    wb = p["w_branch"][i]
    lw["wbr"] = jnp.concatenate([_pad_heads_rows(wb[0], 64), wb[1], wb[2], _pad_heads_rows(wb[3], 64)], axis=0).astype(BF16)
    lw["w_gate"] = p["w_gate"][i].astype(BF16)
    lw["b_gate"] = row(p["b_gate"][i])
    lw["w_out"] = p["w_out"][i].astype(BF16)
    lw["w_router"] = jnp.pad(p["w_router"][i], ((0, 0), (0, LANE - N_EXPERTS))).astype(BF16)
    lw["w_router_t"] = p["w_router"][i].T.astype(BF16)
    lw["w_e_gate"] = p["w_e_gate"][i].astype(BF16)
    lw["w_e_up"] = p["w_e_up"][i].astype(BF16)
    lw["w_e_down"] = p["w_e_down"][i].astype(BF16)
    return lw


def _ffn(xm, h2, aff, aff_t, mod, lw):
    b, t, _ = xm.shape
    nblk = t // TOK_BLK
    cap = EC_CAPACITY * t // N_EXPERTS
    capp = -(-cap // TOK_BLK) * TOK_BLK
    pos4, s04 = _topk(aff_t, cap)
    s0_flat = s04[..., 0].transpose(0, 2, 1).reshape(-1)
    ys = _moe(s0_flat, pos4, h2, lw, capp)
    pos_t = pos4.transpose(0, 1, 3, 2).reshape(b, t, N_EXPERTS)
    return _combine(s0_flat, ys, pos_t, aff, xm, mod, lw["g_ffn_post"])


def _hybrid_layer(i, x_c, x_l, c8, need_ctx, p):
    lw = _layer_weights(i, p)
    b, t, _ = x_l.shape
    tc = x_c.shape[1]
    lam_init = 0.8 - 0.6 * math.exp(-0.3 * i)
    mod8 = _ada(c8, p["w_ada"][i], p["b_ada"][i])
    pad = lambda m: jnp.pad(m.reshape(b, 6, D), ((0, 0), (0, 2), (0, 0)))
    mod_l = pad(mod8[:b])
    mod_c = pad(jnp.broadcast_to(mod8[b:b + 1], (b, 6 * D)))

    pc = _proj(x_c, mod_c, _identity_table(tc), lw)
    pt = _proj(x_l, mod_l, _rope_table(t), lw)

    zc = jnp.zeros((b, 2, 256, 512), F32)
    zm = jnp.full((b, 2, 8, 256), NEG, F32)
    zs = jnp.zeros((b, 2, 128, 256), F32)
    hf_c, hb_c, c_fin, m_fin = _mlstm(pc, lw["w_conv"], lw["b_conv"], zc, zm)
    hf_l, hb_l, _, _ = _mlstm(pt, lw["w_conv"], lw["b_conv"], c_fin, m_fin)
    gf_c, gb_c, s_fin = _gla(pc, zs)
    gf_l, gb_l, _ = _gla(pt, s_fin)

    one_g = jnp.ones((N_HEADS, MLA_V, LANE), F32)
    fl = functools.partial(_flash, tk=512)
    ya_l = fl(pt["mq"], pc["mk"], pc["mvt"], pt["mk"], pt["mvt"], lw["dlam"], one_g, nmap=1, finish=False, post=1.0, tq=512)
    yd_l = fl(pt["dq"], pc["dk"], pc["dvt"], pt["dk"], pt["dvt"], lw["dlam"], lw["g_diff"], nmap=2, finish=True,
              post=1.0 - lam_init, tq=256)
    xm, h2, aff, aff_t = _merge(x_l, mod_l, ya_l, (hf_l, hb_l), pt["lo"], (gf_l, gb_l), pt["gr"], yd_l, lw, tm=256)
    x_l = _ffn(xm, h2, aff, aff_t, mod_l, lw)

    if need_ctx:
        ya_c = fl(pc["mq"], pc["mk"], pc["mvt"], None, None, lw["dlam"], one_g, nmap=1, finish=False, post=1.0, tq=tc)
        yd_c = fl(pc["dq"], pc["dk"], pc["dvt"], None, None, lw["dlam"], lw["g_diff"], nmap=2, finish=True,
                  post=1.0 - lam_init, tq=tc)
        xm, h2, aff, aff_t = _merge(x_c, mod_c, ya_c, (hf_c, hb_c), pc["lo"], (gf_c, gb_c), pc["gr"], yd_c, lw, tm=tc)
        x_c = _ffn(xm, h2, aff, aff_t, mod_c, lw)
    return x_c, x_l


def kernel(x, c, ctx, c_ctx, w_ada, b_ada, g_mix_pre, g_mix_post, g_ffn_pre, g_ffn_post, w_in, g_q_lat, w_uq, g_kv_lat, w_ukv, w_conv, b_conv, b_igate, b_fgate, g_mlstm_out, w_alpha2, b_alpha, g_gla_out, diff_lambda, g_diff_out, w_branch, w_gate, b_gate, w_out, w_router, w_e_gate, w_e_up, w_e_down):
    p = dict(w_ada=w_ada, b_ada=b_ada, g_mix_pre=g_mix_pre, g_mix_post=g_mix_post, g_ffn_pre=g_ffn_pre,
             g_ffn_post=g_ffn_post, w_in=w_in, g_q_lat=g_q_lat, w_uq=w_uq, g_kv_lat=g_kv_lat, w_ukv=w_ukv,
             w_conv=w_conv, b_conv=b_conv, b_igate=b_igate, b_fgate=b_fgate, g_mlstm_out=g_mlstm_out,
             w_alpha2=w_alpha2, b_alpha=b_alpha, g_gla_out=g_gla_out, diff_lambda=diff_lambda, g_diff_out=g_diff_out,
             w_branch=w_branch, w_gate=w_gate, b_gate=b_gate, w_out=w_out, w_router=w_router,
             w_e_gate=w_e_gate, w_e_up=w_e_up, w_e_down=w_e_down)
    b = x.shape[0]
    c8 = jnp.concatenate([c, c_ctx[None], jnp.zeros((8 - b - 1, D), F32)], axis=0)
    x_c, x_l = ctx, x
    for i in range(DEPTH):
        x_c, x_l = _hybrid_layer(i, x_c, x_l, c8, i < DEPTH - 1, p)
    return x_l
```

```python
import functools
import math

import numpy as np
import jax
import jax.numpy as jnp
from jax import lax
from jax.experimental import pallas as pl
from jax.experimental.pallas import tpu as pltpu

F32 = jnp.float32
BF16 = jnp.bfloat16
I32 = jnp.int32
HIGHEST = lax.Precision.HIGHEST

D = 1024
DEPTH = 2
GRID_W = 64
N_HEADS = 4
MLA_NOPE, MLA_ROPE, MLA_V = 64, 32, 64
MLA_Q_LORA, MLA_KV_LORA = 256, 128
ML_DH = 64
GLA_DK, GLA_DV, GLA_RANK, GLA_TAU = 32, 64, 16, 16.0
DIFF_DQK, DIFF_DV = 32, 64
ROPE_DIM, ROPE_BASE = 32, 10000.0
N_EXPERTS, EC_CAPACITY, EXPERT_FF = 16, 2, 1408
NEG = -1e30
EPS = 1e-6
LOG2E = 1.4426950408889634

LANE = 128
HEAD_SLAB = 128
TOK_BLK = 256
ML_CHUNK = 128
GLA_CHUNK = 64
GLA_BLOCK = 256
GATHER_WIN = TOK_BLK + 16
GATHER_WIN_SMALL = 64
MOE_TOK = 1024
VMEM_LIMIT = 56 * 1024 * 1024

ZQ, ZKV, ZKRA, ZKRB, ZMLQK, ZMLV, ZMLO, ZGATE, ZGA = 0, 256, 384, 512, 640, 1152, 1408, 1664, 1792
ZGQ, ZGK, ZGV, ZGR, ZDQ, ZDQS, ZDK, ZDKS, NZ = 1920, 2048, 2176, 2432, 2688, 2944, 3200, 3456, 3712
KV_CHUNK = 256
VT_ROWS = 80
FLASH_ROWS = 512
FLASH_UNROLL = 8


def _swap32(c):
    return (c // 32) * 32 + ((c % 32) ^ 8)


def _win_index():
    idx = -np.ones((NZ,), np.int64)
    idx[ZQ:ZQ + 256] = np.arange(0, 256)
    idx[ZKV:ZKV + 128] = np.arange(256, 384)
    r = np.arange(32)
    idx[ZKRA + 64:ZKRA + 96] = 384 + r
    idx[ZKRB + 64:ZKRB + 96] = 384 + (r ^ 8)
    ml = 416
    idx[ZMLQK:ZMLQK + 512] = ml + np.arange(512)
    idx[ZMLV:ZMLV + 256] = ml + 512 + np.arange(256)
    idx[ZMLO:ZMLO + 256] = ml + 768 + np.arange(256)
    idx[ZGATE:ZGATE + 16] = ml + 1024 + np.arange(16)
    gl = 1456
    idx[ZGQ:ZGQ + 128] = gl + np.arange(128)
    idx[ZGK:ZGK + 128] = gl + 128 + np.arange(128)
    idx[ZGV:ZGV + 256] = gl + 256 + np.arange(256)
    idx[ZGR:ZGR + 256] = gl + 512 + np.arange(256)
    idx[ZGA:ZGA + 32] = gl + 768 + np.arange(32)
    df = 2256
    c = np.arange(256)
    idx[ZDQ:ZDQ + 256] = df + c
    idx[ZDQS:ZDQS + 256] = df + _swap32(c)
    idx[ZDK:ZDK + 256] = df + 256 + c
    idx[ZDKS:ZDKS + 256] = df + 256 + _swap32(c)
    return idx


def _vt_rows(w_cols):
    n = w_cols.shape[0]
    w4 = w_cols.T.reshape(N_HEADS, 64, n)
    return jnp.pad(w4, ((0, 0), (0, VT_ROWS - 64), (0, 0))).reshape(N_HEADS * VT_ROWS, n)


_WIN_IDX = _win_index()


def _gather_cols(w, idx):
    safe = np.maximum(idx, 0)
    return jnp.where(jnp.asarray(idx >= 0)[None, :], w[:, safe], 0.0)


def _pad_heads_rows(w, width):
    n = w.shape[1]
    w4 = w.reshape(N_HEADS, width, n)
    return jnp.pad(w4, ((0, 0), (0, HEAD_SLAB - width), (0, 0))).reshape(N_HEADS * HEAD_SLAB, n)


def _cparams(sem):
    return pltpu.CompilerParams(dimension_semantics=sem, vmem_limit_bytes=VMEM_LIMIT)


def _rms(x):
    return x * lax.rsqrt(jnp.mean(x * x, axis=-1, keepdims=True) + EPS)


def _sigmoid(x):
    return 1.0 / (1.0 + jnp.exp(-x))


def _log_sigmoid(x):
    return jnp.minimum(x, 0.0) - jnp.log1p(jnp.exp(-jnp.abs(x)))


def _dot(a, b, precision=None):
    return jnp.dot(a, b, preferred_element_type=F32, precision=precision)


def _dot_nt(a, b, precision=None):
    return lax.dot_general(a, b, (((1,), (1,)), ((), ())), preferred_element_type=F32, precision=precision)


def _dot_tn(a, b, precision=None):
    return lax.dot_general(a, b, (((0,), (0,)), ((), ())), preferred_element_type=F32, precision=precision)


def _const_spec(shape):
    nd = len(shape)
    return pl.BlockSpec(shape, lambda *_: (0,) * nd)


def _ada_kernel(c_ref, w_ref, b_ref, o_ref):
    cv = c_ref[...]
    s = (cv * _sigmoid(cv)).astype(BF16)
    o_ref[...] = _dot(s, w_ref[...].astype(BF16)) + b_ref[...]


def _ada(c8, w_ada, b_ada):
    n, tn = 6 * D, 1024
    return pl.pallas_call(
        _ada_kernel, name="ada", grid=(n // tn,),
        in_specs=[pl.BlockSpec((8, D), lambda j: (0, 0)), pl.BlockSpec((D, tn), lambda j: (0, j)),
                  pl.BlockSpec((1, tn), lambda j: (0, j))],
        out_specs=pl.BlockSpec((8, tn), lambda j: (0, j)),
        out_shape=jax.ShapeDtypeStruct((8, n), F32), compiler_params=_cparams(("arbitrary",)),
    )(c8, w_ada, b_ada.reshape(1, n))


_PROJ_OUT = (
    ("mq", 512, BF16), ("mk", 512, BF16),
    ("lqk", 512, F32), ("lv", 256, BF16), ("lo", 256, BF16), ("gc", 128, F32),
    ("gq", 128, BF16), ("gk", 128, BF16), ("gv", 256, BF16), ("gr", 256, BF16), ("glg", 256, F32),
    ("dq", 1024, BF16), ("dk", 512, BF16),
)


def _proj_kernel(x_ref, mod_ref, g_ref, w_ref, tab_ref, gq_ref, wq_ref, wqs_ref, gkv_ref, wk_ref, wvt_ref,
                 wgt_ref, gbr_ref, gbc_ref, wal_ref, bal_ref, wdvt_ref,
                 mq_ref, mk_ref, lqk_ref, lv_ref, lo_ref, gc_ref, gq_o, gk_o, gv_o, gr_o, glg_o,
                 dq_ref, dk_ref, grow_ref, mvt_ref, dvt_ref):
    x = x_ref[0]
    tm = x.shape[0]
    mod = mod_ref[0]
    h = _rms(x) * g_ref[...] * (1.0 + mod[1:2]) + mod[0:1]
    hb = h.astype(BF16)
    z = _dot(hb, w_ref[...])
    tab = tab_ref[...]
    ct, st, cd, sd = tab[:, 0:128], tab[:, 128:256], tab[:, 256:384], tab[:, 384:512]
    lane = lax.broadcasted_iota(I32, (tm, LANE), 1)

    qn = (_rms(z[:, ZQ:ZQ + 256]) * gq_ref[...]).astype(BF16)
    qa = _dot(qn, wq_ref[...])
    qb = _dot(qn, wqs_ref[...])
    qscale = (MLA_NOPE + MLA_ROPE) ** -0.5 * LOG2E
    for hh in range(N_HEADS):
        sl = slice(HEAD_SLAB * hh, HEAD_SLAB * (hh + 1))
        mq_ref[0, :, sl] = ((qa[:, sl] * ct + qb[:, sl] * st) * qscale).astype(BF16)
    kvn = (_rms(z[:, ZKV:ZKV + 128]) * gkv_ref[...]).astype(BF16)
    kk = _dot(kvn, wk_ref[...])
    kr = z[:, ZKRA:ZKRA + 128] * ct + z[:, ZKRB:ZKRB + 128] * st
    for hh in range(N_HEADS):
        sl = slice(HEAD_SLAB * hh, HEAD_SLAB * (hh + 1))
        mk_ref[0, :, sl] = (kk[:, sl] + kr).astype(BF16)
    ones_row = lax.broadcasted_iota(I32, (N_HEADS * VT_ROWS, tm), 0) % VT_ROWS == MLA_V
    mvt_ref[0, 0] = jnp.where(ones_row, 1.0, _dot_nt(wvt_ref[...], kvn)).astype(BF16)
    dvt_ref[0, 0] = jnp.where(ones_row, 1.0, _dot_nt(wdvt_ref[...], hb)).astype(BF16)

    lqk_ref[0] = z[:, ZMLQK:ZMLQK + 512]
    lv_ref[0] = z[:, ZMLV:ZMLV + 256].astype(BF16)
    lo_ref[0] = z[:, ZMLO:ZMLO + 256].astype(BF16)
    gcol = z[:, ZGATE:ZGATE + 128] + gbr_ref[...]
    gc_ref[0] = jnp.where(lane < 8, gcol, jnp.where(lane < 16, _log_sigmoid(gcol), 0.0))
    zr = _dot_nt(wgt_ref[...], hb) + gbc_ref[...]
    rowi = lax.broadcasted_iota(I32, zr.shape, 0)
    grow_ref[0] = jnp.where(rowi < 8, zr, _log_sigmoid(zr))

    gq_o[0] = (z[:, ZGQ:ZGQ + 128] * GLA_DK ** -0.5).astype(BF16)
    gk_o[0] = z[:, ZGK:ZGK + 128].astype(BF16)
    gv_o[0] = z[:, ZGV:ZGV + 256].astype(BF16)
    gr_o[0] = z[:, ZGR:ZGR + 256].astype(BF16)
    zg = _dot(z[:, ZGA:ZGA + 128].astype(BF16), wal_ref[...]) + bal_ref[...]
    glg_o[0] = _log_sigmoid(zg) * (1.0 / GLA_TAU)

    dscale = DIFF_DQK ** -0.5 * LOG2E
    for g in range(2):
        gs = slice(128 * g, 128 * (g + 1))
        qg = (z[:, ZDQ:ZDQ + 256][:, gs] * cd + z[:, ZDQS:ZDQS + 256][:, gs] * sd) * dscale
        kg = z[:, ZDK:ZDK + 256][:, gs] * cd + z[:, ZDKS:ZDKS + 256][:, gs] * sd
        for hl in range(2):
            hh = 2 * g + hl
            for m in range(2):
                lo = 64 * hl + 32 * m
                s0 = (2 * hh + m) * HEAD_SLAB
                dq_ref[0, :, s0:s0 + HEAD_SLAB] = jnp.where((lane >= lo) & (lane < lo + 32), qg, 0.0).astype(BF16)
            dk_ref[0, :, HEAD_SLAB * hh:HEAD_SLAB * (hh + 1)] = jnp.where(
                (lane >= 64 * hl) & (lane < 64 * hl + 64), kg, 0.0).astype(BF16)


def _proj(x, mod, tab, lw):
    b, t, _ = x.shape
    tm = KV_CHUNK
    consts = [lw["g_mix_pre"], lw["w_ext"], None, lw["g_q_lat"], lw["wq"], lw["wqs"], lw["g_kv_lat"], lw["wk"], lw["wvt"],
              lw["wgt"], lw["gate_bias_row"], lw["gate_bias_col"], lw["walpha"], lw["balpha"], lw["wdvt"]]
    in_specs = [pl.BlockSpec((1, tm, D), lambda bi, i: (bi, i, 0)), pl.BlockSpec((1, 8, D), lambda bi, i: (bi, 0, 0))]
    args = [x, mod]
    for cst in consts:
        if cst is None:
            in_specs.append(pl.BlockSpec((tm, 512), lambda bi, i: (i, 0)))
            args.append(tab)
        else:
            in_specs.append(_const_spec(cst.shape))
            args.append(cst)
    out_specs = [pl.BlockSpec((1, tm, w), lambda bi, i: (bi, i, 0)) for _, w, _ in _PROJ_OUT]
    out_shape = [jax.ShapeDtypeStruct((b, t, w), dt) for _, w, dt in _PROJ_OUT]
    out_specs.append(pl.BlockSpec((1, 16, tm), lambda bi, i: (bi, 0, i)))
    out_shape.append(jax.ShapeDtypeStruct((b, 16, t), F32))
    for _ in range(2):
        out_specs.append(pl.BlockSpec((1, 1, N_HEADS * VT_ROWS, tm), lambda bi, i: (bi, i, 0, 0)))
        out_shape.append(jax.ShapeDtypeStruct((b, t // tm, N_HEADS * VT_ROWS, tm), BF16))
    outs = pl.pallas_call(
        _proj_kernel, name="proj", grid=(b, t // tm), in_specs=in_specs, out_specs=out_specs, out_shape=out_shape,
        compiler_params=_cparams(("parallel", "arbitrary")),
    )(*args)
    res = {name: o for (name, _, _), o in zip(_PROJ_OUT, outs[:-3])}
    res["grow"], res["mvt"], res["dvt"] = outs[-3:]
    return res


def _flash_kernel(*refs, nmap, has_lat, tk, finish, post):
    if has_lat:
        q_ref, kc_ref, vc_ref, kl_ref, vl_ref, dl_ref, g_ref, o_ref, s_ref = refs
    else:
        q_ref, kc_ref, vc_ref, dl_ref, g_ref, o_ref = refs
    qb = q_ref[0]
    tq = qb.shape[0]
    q = qb if nmap == 1 else jnp.concatenate([qb[:, :HEAD_SLAB], qb[:, HEAD_SLAB:]], axis=0)
    rows = nmap * tq
    sub = tk // KV_CHUNK

    def softmax(s, m):
        m_new = jnp.maximum(m, jnp.max(s, axis=0, keepdims=True))
        return m_new, jnp.exp2(m - m_new), jnp.exp2(s - m_new).astype(BF16)

    def pv(p, vts):
        out = _dot(vts[0], p[0:KV_CHUNK])
        for c in range(1, len(vts)):
            out = out + _dot(vts[c], p[c * KV_CHUNK:(c + 1) * KV_CHUNK])
        return out

    m, _, p = softmax(_dot_nt(kc_ref[0], q), jnp.full((1, rows), NEG, F32))
    acc = pv(p, [vc_ref[0, 0]])
    if has_lat:
        n = kl_ref.shape[1] // tk

        unroll = min(FLASH_UNROLL, n)

        def scores(j):
            if isinstance(j, int):
                return _dot_nt(kl_ref[0, j * tk:(j + 1) * tk, :], q)
            off = pl.multiple_of(j * tk, tk)
            return _dot_nt(kl_ref[0, pl.ds(off, tk), :], q)

        def values(j):
            return [vl_ref[0, j * sub + c] for c in range(sub)]

        s_ref[0] = scores(0)

        def body(jj, carry):
            m, acc = carry
            j = unroll * jj
            for u in range(unroll):
                if not isinstance(j, int):
                    s_ref[(u + 1) % 2] = scores(jnp.minimum(j + u + 1, n - 1))
                elif j + u + 1 < n:
                    s_ref[(u + 1) % 2] = scores(j + u + 1)
                m, alpha, p = softmax(s_ref[u % 2], m)
                acc = alpha * acc + pv(p, values(j + u))
            return m, acc

        if unroll == n:
            m, acc = body(0, (m, acc))
        else:
            m, acc = lax.fori_loop(0, n // unroll, body, (m, acc))

    o = acc[0:MLA_V, :] / acc[MLA_V:MLA_V + 1, :]
    if nmap == 2:
        lv = dl_ref[...]
        lam = (jnp.exp(jnp.sum(lv[0:1] * lv[1:2], axis=-1, keepdims=True))
               - jnp.exp(jnp.sum(lv[2:3] * lv[3:4], axis=-1, keepdims=True)) + (1.0 - post))
        o = o[:, :tq] - lam * o[:, tq:]
    if finish:
        ms = jnp.mean(o * o, axis=0, keepdims=True)
        o = o * lax.rsqrt(ms + EPS) * jnp.concatenate([g_ref[0]] * (tq // LANE), axis=1) * post
    o_pad = jnp.concatenate([o, jnp.zeros((HEAD_SLAB - MLA_V, tq), F32)], axis=0)
    o_ref[0] = o_pad.T.astype(BF16)


def _flash(q, kc, vct, kl, vlt, dlam, g_out, *, nmap, finish, post, tq, tk):
    b, t, _ = q.shape
    has_lat = kl is not None
    assert kc.shape[1] == KV_CHUNK and tk % KV_CHUNK == 0
    qw = nmap * HEAD_SLAB
    kspec = lambda n: pl.BlockSpec((1, n, HEAD_SLAB), lambda bi, h, i: (bi, 0, h))
    vspec = lambda n: pl.BlockSpec((1, n // KV_CHUNK, VT_ROWS, KV_CHUNK), lambda bi, h, i: (bi, 0, h, 0))
    in_specs = [pl.BlockSpec((1, tq, qw), lambda bi, h, i: (bi, i, h)), kspec(KV_CHUNK), vspec(KV_CHUNK)]
    args = [q, kc, vct]
    scratch = []
    if has_lat:
        tl = kl.shape[1]
        assert (tl // tk) % min(FLASH_UNROLL, tl // tk) == 0
        in_specs += [kspec(tl), vspec(tl)]
        args += [kl, vlt]
        scratch = [pltpu.VMEM((2, tk, nmap * tq), F32)]
    in_specs += [_const_spec(dlam.shape), pl.BlockSpec((1, MLA_V, LANE), lambda bi, h, i: (h, 0, 0))]
    args += [dlam, g_out]
    return pl.pallas_call(
        functools.partial(_flash_kernel, nmap=nmap, has_lat=has_lat, tk=tk, finish=finish, post=post),
        name="flash_diff" if nmap == 2 else "flash_mla",
        grid=(b, N_HEADS, t // tq), in_specs=in_specs,
        out_specs=pl.BlockSpec((1, tq, HEAD_SLAB), lambda bi, h, i: (bi, i, h)),
        out_shape=jax.ShapeDtypeStruct((b, t, N_HEADS * HEAD_SLAB), BF16),
        scratch_shapes=scratch,
        compiler_params=_cparams(("parallel", "parallel", "arbitrary")),
    )(*args)


def _head_of(shape, axis, width):
    return (lax.broadcasted_iota(I32, shape, axis) % (N_HEADS * width)) // width


def _mlstm_dir(d, first, last, x, xprev, xnext, v, gcol, grow, wc, bcv, c_ref, m_ref):
    L = x.shape[0]
    row = lax.broadcasted_iota(I32, x.shape, 0)
    pr = jnp.where(first, 0.0, xprev[7:8, :])
    nx = jnp.where(last, 0.0, xnext[0:1, :])
    xm = jnp.where(row == 0, pr, pltpu.roll(x, 1, 0))
    xp = jnp.where(row == L - 1, nx, pltpu.roll(x, L - 1, 0))
    y = xm * wc[0:1] + x * wc[1:2] + xp * wc[2:3] + bcv
    qk = y * _sigmoid(y)
    q = qk[:, :256]
    k = qk[:, 256:] * ML_DH ** -0.5

    li = lax.broadcasted_iota(I32, (L, L), 0)
    si = lax.broadcasted_iota(I32, (L, L), 1)
    tin = (si <= li) if d == 0 else (si >= li)
    tinf = tin.astype(F32)
    bcol = _dot(tinf, gcol, HIGHEST)
    brow = _dot_nt(grow, tinf, HIGHEST)
    m0e = m_ref[0, d]
    cb = c_ref[0, d]
    hm256 = _head_of((L, 256), 1, ML_DH)
    hm512 = _head_of((L, 512), 1, ML_DH)
    e_idx = L - 1 if d == 0 else 0

    d_blk, inter_blk = [], []
    for hh in range(N_HEADS):
        c = 4 * d + hh
        bc = bcol[:, 8 + c:9 + c]
        d_blk.append(jnp.where(tin, bc - brow[8 + c:9 + c, :] + grow[c:c + 1, :], NEG))
        inter_blk.append(bc + m0e[0:1, 64 * hh:64 * hh + 1])
    d_st = jnp.concatenate(d_blk, axis=0)
    inter_st = jnp.concatenate(inter_blk, axis=0)
    mt = jnp.maximum(inter_st, jnp.max(d_st, axis=-1, keepdims=True))
    q_st = jnp.concatenate([jnp.where(hm256 == hh, q, 0.0) for hh in range(N_HEADS)], axis=0).astype(BF16)
    s_st = (jnp.exp(d_st - mt) * _dot_nt(q_st, k.astype(BF16))).astype(BF16)
    vext = jnp.concatenate([v, jnp.ones((L, 256), BF16)], axis=1)
    r = _dot(s_st, vext)
    aint = jnp.exp(inter_st - mt)
    p = _dot(q.astype(BF16), cb.astype(BF16))
    tot = jnp.zeros((L, 512), F32)
    mte = jnp.zeros((L, 256), F32)
    for hh in range(N_HEADS):
        rs = slice(hh * L, (hh + 1) * L)
        tot = jnp.where(hm512 == hh, r[rs] + aint[rs] * p, tot)
        mte = jnp.where(hm256 == hh, mt[rs], mte)
    hout = tot[:, :256] / jnp.maximum(jnp.abs(tot[:, 256:]), jnp.exp(-mte))

    wexp = jnp.zeros((L, 256), F32)
    arow = jnp.zeros((1, 512), F32)
    grw = jnp.zeros((1, 512), F32)
    mnew = jnp.zeros((1, 256), F32)
    hr512 = _head_of((1, 512), 1, ML_DH)
    hr256 = _head_of((1, 256), 1, ML_DH)
    for hh in range(N_HEADS):
        c = 4 * d + hh
        bc = bcol[:, 8 + c:9 + c]
        be = bc[e_idx:e_idx + 1, :]
        wl = be - bc + gcol[:, c:c + 1]
        mloc = jnp.max(wl, axis=0, keepdims=True)
        m0h = m0e[0:1, 64 * hh:64 * hh + 1]
        mn = jnp.maximum(be + m0h, mloc)
        wexp = jnp.where(hm256 == hh, jnp.exp(wl - mloc), wexp)
        arow = jnp.where(hr512 == hh, jnp.exp(be + m0h - mn), arow)
        grw = jnp.where(hr512 == hh, jnp.exp(mloc - mn), grw)
        mnew = jnp.where(hr256 == hh, mn, mnew)
    cl = _dot_tn((k * wexp).astype(BF16), vext)
    bd = lax.broadcasted_iota(I32, (256, 512), 0) // ML_DH == _head_of((256, 512), 1, ML_DH)
    c_ref[0, d] = arow * cb + jnp.where(bd, grw * cl, 0.0)
    m_ref[0, d] = jnp.broadcast_to(mnew, (8, 256))
    return hout


def _mlstm_kernel(xf, xfp, xfn, xb, xbp, xbn, vf, vb, gcf, gcb, grf, grb, wc_ref, bc_ref, c0_ref, m0_ref,
                  hf_ref, hb_ref, c_ref, m_ref):
    i = pl.program_id(1)
    n = pl.num_programs(1)

    @pl.when(i == 0)
    def _():
        c_ref[...] = c0_ref[...]
        m_ref[...] = m0_ref[...]

    wc = wc_ref[...]
    bcv = bc_ref[...]
    hf_ref[0] = _mlstm_dir(0, i == 0, i == n - 1, xf[0], xfp[0], xfn[0], vf[0], gcf[0], grf[0], wc, bcv, c_ref, m_ref)
    hb_ref[0] = _mlstm_dir(1, i == n - 1, i == 0, xb[0], xbp[0], xbn[0], vb[0], gcb[0], grb[0], wc, bcv, c_ref, m_ref)


def _mlstm(pr, w_conv, b_conv, c0, m0):
    x, v, gc, gr = pr["lqk"], pr["lv"], pr["gc"], pr["grow"]
    b, t, _ = x.shape
    L = ML_CHUNK
    n = t // L
    r8 = L // 8
    last8 = t // 8 - 1

    def fw(bi, i):
        return (bi, i, 0)

    def bw(bi, i):
        return (bi, n - 1 - i, 0)

    def halo(ix, shift):
        def f(bi, i):
            blk = ix(bi, i)[1]
            return (bi, jnp.clip(blk * r8 + shift, 0, last8), 0)
        return f

    main = lambda w, ix: pl.BlockSpec((1, L, w), ix)
    in_specs = [main(512, fw), pl.BlockSpec((1, 8, 512), halo(fw, -1)), pl.BlockSpec((1, 8, 512), halo(fw, r8)),
                main(512, bw), pl.BlockSpec((1, 8, 512), halo(bw, -1)), pl.BlockSpec((1, 8, 512), halo(bw, r8)),
                main(256, fw), main(256, bw), main(128, fw), main(128, bw),
                pl.BlockSpec((1, 16, L), lambda bi, i: (bi, 0, i)), pl.BlockSpec((1, 16, L), lambda bi, i: (bi, 0, n - 1 - i)),
                _const_spec(w_conv.shape), _const_spec(b_conv.shape),
                pl.BlockSpec((1, 2, 256, 512), lambda bi, i: (bi, 0, 0, 0)), pl.BlockSpec((1, 2, 8, 256), lambda bi, i: (bi, 0, 0, 0))]
    out_specs = [main(256, fw), main(256, bw),
                 pl.BlockSpec((1, 2, 256, 512), lambda bi, i: (bi, 0, 0, 0)), pl.BlockSpec((1, 2, 8, 256), lambda bi, i: (bi, 0, 0, 0))]
    out_shape = [jax.ShapeDtypeStruct((b, t, 256), F32), jax.ShapeDtypeStruct((b, t, 256), F32),
                 jax.ShapeDtypeStruct(c0.shape, F32), jax.ShapeDtypeStruct(m0.shape, F32)]
    return pl.pallas_call(
        _mlstm_kernel, name="mlstm", grid=(b, n), in_specs=in_specs, out_specs=out_specs, out_shape=out_shape,
        compiler_params=_cparams(("parallel", "arbitrary")),
    )(x, x, x, x, x, x, v, v, gc, gc, gr, gr, w_conv, b_conv, c0, m0)


def _gla_chunk(d, q, k, v, lg, sb):
    L = q.shape[0]
    li = lax.broadcasted_iota(I32, (L, L), 0)
    si = lax.broadcasted_iota(I32, (L, L), 1)
    tin = (si <= li) if d == 0 else (si >= li)
    tinf = tin.astype(F32)
    lgd = lg[:, 128 * d:128 * (d + 1)]
    gcum = _dot(tinf, lgd, HIGHEST)
    e_idx = L - 1 if d == 0 else 0
    gend = gcum[e_idx:e_idx + 1, :]
    qf, kf = q.astype(F32), k.astype(F32)
    q_dec = qf * jnp.exp(gcum)
    k_dec = (kf * jnp.exp(-gcum)).astype(BF16)
    k_end = (kf * jnp.exp(gend - gcum)).astype(BF16)
    hm128 = _head_of((L, 128), 1, GLA_DK)
    hm256 = _head_of((L, 256), 1, GLA_DV)
    q_st = jnp.concatenate([jnp.where(hm128 == hh, q_dec, 0.0) for hh in range(N_HEADS)], axis=0).astype(BF16)
    att = _dot_nt(q_st, k_dec)
    tin4 = jnp.concatenate([tin] * N_HEADS, axis=0)
    o_st = _dot(jnp.where(tin4, att, 0.0).astype(BF16), v)
    o = _dot(q_dec.astype(BF16), sb.astype(BF16))
    for hh in range(N_HEADS):
        o = o + jnp.where(hm256 == hh, o_st[hh * L:(hh + 1) * L], 0.0)
    gend_col = _dot_tn(lgd, jnp.ones((L, 256), F32), HIGHEST)
    bd = lax.broadcasted_iota(I32, (128, 256), 0) // GLA_DK == _head_of((128, 256), 1, GLA_DV)
    return o, jnp.exp(gend_col) * sb + jnp.where(bd, _dot_tn(k_end, v), 0.0)


def _gla_kernel(qf, kf, vf, lf, qb, kb, vb, lb, s0_ref, of_ref, ob_ref, s_ref):
    i = pl.program_id(1)

    @pl.when(i == 0)
    def _():
        s_ref[...] = s0_ref[...]

    L = GLA_CHUNK
    nsub = qf.shape[1] // L
    for d, (q, k, v, lg, o_ref) in enumerate(((qf, kf, vf, lf, of_ref), (qb, kb, vb, lb, ob_ref))):
        sb = s_ref[0, d]
        for c in (range(nsub) if d == 0 else reversed(range(nsub))):
            sl = slice(c * L, (c + 1) * L)
            o_ref[0, sl, :], sb = _gla_chunk(d, q[0, sl, :], k[0, sl, :], v[0, sl, :], lg[0, sl, :], sb)
        s_ref[0, d] = sb


def _gla(pr, s0):
    q, k, v, lg = pr["gq"], pr["gk"], pr["gv"], pr["glg"]
    b, t, _ = q.shape
    L = min(GLA_BLOCK, t)
    n = t // L
    fw = lambda bi, i: (bi, i, 0)
    bw = lambda bi, i: (bi, n - 1 - i, 0)
    blk = lambda w, ix: pl.BlockSpec((1, L, w), ix)
    st_spec = pl.BlockSpec((1, 2, 128, 256), lambda bi, i: (bi, 0, 0, 0))
    return pl.pallas_call(
        _gla_kernel, name="gla", grid=(b, n),
        in_specs=[blk(128, fw), blk(128, fw), blk(256, fw), blk(256, fw),
                  blk(128, bw), blk(128, bw), blk(256, bw), blk(256, bw), st_spec],
        out_specs=[blk(256, fw), blk(256, bw), st_spec],
        out_shape=[jax.ShapeDtypeStruct((b, t, 256), F32), jax.ShapeDtypeStruct((b, t, 256), F32),
                   jax.ShapeDtypeStruct(s0.shape, F32)],
        compiler_params=_cparams(("parallel", "arbitrary")),
    )(q, k, v, lg, q, k, v, lg, s0)


def _head_rms_expanded(x, width):
    n = x.shape[1]
    bd = (lax.broadcasted_iota(I32, (n, n), 0) // width == lax.broadcasted_iota(I32, (n, n), 1) // width).astype(F32)
    return _dot(x * x, bd, HIGHEST) * (1.0 / width)


def _merge_kernel(x_ref, mod_ref, ya_ref, hf_ref, hb_ref, lo_ref, gf_ref, gb_ref, gr_ref, yd_ref,
                  gpre_ref, wg_ref, bg_ref, wbr_ref, wo_ref, gpost_ref, gffn_ref, wr_ref, wrt_ref, gml_ref, ggla_ref,
                  xm_ref, h2_ref, aff_ref, afft_ref):
    x = x_ref[0]
    tm = x.shape[0]
    mod = mod_ref[0]
    hb = (_rms(x) * gpre_ref[...] * (1.0 + mod[1:2]) + mod[0:1]).astype(BF16)

    hs = hf_ref[0] + hb_ref[0]
    y_ml = _sigmoid(lo_ref[0].astype(F32)) * (hs * lax.rsqrt(_head_rms_expanded(hs, ML_DH) + EPS) * gml_ref[...])
    gs = gf_ref[0] + gb_ref[0]
    rr = gr_ref[0].astype(F32)
    y_gla = rr * _sigmoid(rr) * (gs * lax.rsqrt(_head_rms_expanded(gs, GLA_DV) + EPS) * ggla_ref[...])

    branches = ((ya_ref[0], 0, 512), (y_ml.astype(BF16), 512, 256), (y_gla.astype(BF16), 768, 256), (yd_ref[0], 1024, 512))
    mix = jnp.zeros((tm, D), F32)
    for nb, (yb, r0, rw) in enumerate(branches):
        gate = _sigmoid(_dot(hb, wg_ref[:, nb * D:(nb + 1) * D]) + bg_ref[:, nb * D:(nb + 1) * D])
        mix = mix + gate * _dot(yb, wbr_ref[r0:r0 + rw, :])
    y = _dot(mix.astype(BF16), wo_ref[...])
    xm = x + mod[2:3] * (_rms(y) * gpost_ref[...])
    xm_ref[0] = xm

    h2 = (_rms(xm) * gffn_ref[...] * (1.0 + mod[4:5]) + mod[3:4]).astype(BF16)
    h2_ref[0] = h2
    lane = lax.broadcasted_iota(I32, (tm, LANE), 1)
    lg = jnp.where(lane < N_EXPERTS, _dot(h2, wr_ref[...]), NEG)
    e = jnp.exp(lg - jnp.max(lg, axis=-1, keepdims=True))
    aff_ref[0] = (e / jnp.sum(e, axis=-1, keepdims=True))[:, :N_EXPERTS]
    lt = _dot_nt(wrt_ref[...], h2)
    et = jnp.exp(lt - jnp.max(lt, axis=0, keepdims=True))
    afft_ref[0] = et / jnp.sum(et, axis=0, keepdims=True)


def _merge(x, mod, ya, ml, lo, gl, gr, yd, lw, tm):
    b, t, _ = x.shape
    tok = lambda w: pl.BlockSpec((1, tm, w), lambda bi, i: (bi, i, 0))
    consts = [lw["g_mix_pre"], lw["w_gate"], lw["b_gate"], lw["wbr"], lw["w_out"], lw["g_mix_post"], lw["g_ffn_pre"],
              lw["w_router"], lw["w_router_t"], lw["g_mlstm_out"], lw["g_gla_out"]]
    in_specs = [tok(D), pl.BlockSpec((1, 8, D), lambda bi, i: (bi, 0, 0)), tok(512), tok(256), tok(256), tok(256),
                tok(256), tok(256), tok(256), tok(512)] + [_const_spec(c.shape) for c in consts]
    out_specs = [tok(D), tok(D), tok(N_EXPERTS), pl.BlockSpec((1, N_EXPERTS, tm), lambda bi, i: (bi, 0, i))]
    out_shape = [jax.ShapeDtypeStruct((b, t, D), F32), jax.ShapeDtypeStruct((b, t, D), BF16),
                 jax.ShapeDtypeStruct((b, t, N_EXPERTS), F32), jax.ShapeDtypeStruct((b, N_EXPERTS, t), F32)]
    return pl.pallas_call(
        _merge_kernel, name="merge", grid=(b, t // tm), in_specs=in_specs, out_specs=out_specs, out_shape=out_shape,
        compiler_params=_cparams(("parallel", "arbitrary")),
    )(x, mod, ya, ml[0], ml[1], lo, gl[0], gl[1], gr, yd, *consts)


def _topk_kernel(a_ref, pos_ref, s0_ref, *, cap):
    nblk = a_ref.shape[1]
    bits = pltpu.bitcast(a_ref[0], I32)

    def bisect(i, thr):
        cand = thr | (1 << (30 - i))
        cnt = jnp.sum((bits >= cand).astype(I32), axis=(0, 2), keepdims=True)
        return jnp.where(cnt >= cap, cand, thr)

    thr3 = lax.fori_loop(0, 31, bisect, jnp.zeros((1, N_EXPERTS, 1), I32))
    need3 = cap - jnp.sum((bits > thr3).astype(I32), axis=(0, 2), keepdims=True)
    thr, need = thr3[0], need3[0].astype(F32)
    upper = (lax.broadcasted_iota(I32, (TOK_BLK, TOK_BLK), 0) <= lax.broadcasted_iota(I32, (TOK_BLK, TOK_BLK), 1)).astype(BF16)

    def blk(j, carry):
        c_eq, c_sel = carry
        bj = pltpu.bitcast(a_ref[0, j], I32)
        gt, eq = bj > thr, bj == thr
        cum_eq = _dot(eq.astype(BF16), upper) + c_eq
        sel = gt | (eq & (cum_eq <= need))
        cum_sel = _dot(sel.astype(BF16), upper) + c_sel
        pos_ref[0, j] = jnp.where(sel, cum_sel - 1.0, -1.0).astype(I32)
        s0_ref[0, j] = jnp.broadcast_to(c_sel, (N_EXPERTS, LANE)).astype(I32)
        return cum_eq[:, TOK_BLK - 1:TOK_BLK], cum_sel[:, TOK_BLK - 1:TOK_BLK]

    zero = jnp.zeros((N_EXPERTS, 1), F32)
    lax.fori_loop(0, nblk, blk, (zero, zero))


def _topk(aff_t, cap):
    b, _, t = aff_t.shape
    nblk = t // TOK_BLK
    a4 = aff_t.reshape(b, N_EXPERTS, nblk, TOK_BLK).transpose(0, 2, 1, 3)
    spec = lambda w: pl.BlockSpec((1, nblk, N_EXPERTS, w), lambda bi: (bi, 0, 0, 0))
    return pl.pallas_call(
        functools.partial(_topk_kernel, cap=cap), name="topk", grid=(b,),
        in_specs=[spec(TOK_BLK)], out_specs=[spec(TOK_BLK), spec(LANE)],
        out_shape=[jax.ShapeDtypeStruct((b, nblk, N_EXPERTS, TOK_BLK), I32),
                   jax.ShapeDtypeStruct((b, nblk, N_EXPERTS, LANE), I32)],
        compiler_params=_cparams(("parallel",)),
    )(a4)


def _moe_kernel(s0_ref, pos_ref, h_ref, wg_ref, wu_ref, wd_ref, ys_ref, xs_ref, *, nblk, nsub, capp):
    bi, e, tb = pl.program_id(0), pl.program_id(1), pl.program_id(2)

    @pl.when(tb == 0)
    def _():
        xs_ref[...] = jnp.zeros(xs_ref.shape, BF16)

    base = (bi * N_EXPERTS + e) * (nblk + 1) + tb * nsub
    for sb in range(nsub):
        s0 = s0_ref[base + sb]
        s1 = s0_ref[base + sb + 1]
        a0 = pl.multiple_of((s0 // 16) * 16, 16)

        def gather(win, sb=sb, a0=a0):
            prow = pos_ref[0, sb, pl.ds(e, 1), :]
            slot = lax.broadcasted_iota(I32, (win, TOK_BLK), 0) + a0
            rows = _dot((slot == prow).astype(BF16), h_ref[0, sb * TOK_BLK:(sb + 1) * TOK_BLK, :])
            xs_ref[pl.ds(a0, win), :] = xs_ref[pl.ds(a0, win), :] + rows.astype(BF16)

        pl.when((s1 > s0) & (s1 - a0 <= GATHER_WIN_SMALL))(functools.partial(gather, GATHER_WIN_SMALL))
        pl.when(s1 - a0 > GATHER_WIN_SMALL)(functools.partial(gather, GATHER_WIN))

    @pl.when(tb == pl.num_programs(2) - 1)
    def _():
        def chunk(c, carry):
            off = pl.multiple_of(c * TOK_BLK, TOK_BLK)
            xc = xs_ref[pl.ds(off, TOK_BLK), :]
            hg = _dot(xc, wg_ref[0])
            hid = (hg * _sigmoid(hg) * _dot(xc, wu_ref[0])).astype(BF16)
            ys_ref[0, 0, pl.ds(off, TOK_BLK), :] = _dot(hid, wd_ref[0]).astype(BF16)
            return carry
        lax.fori_loop(0, capp // TOK_BLK, chunk, 0)


def _moe(s0_flat, pos4, h2, lw, capp):
    b, t, _ = h2.shape
    nblk = t // TOK_BLK
    tok = min(MOE_TOK, t)
    nsub = tok // TOK_BLK
    grid_spec = pltpu.PrefetchScalarGridSpec(
        num_scalar_prefetch=1, grid=(b, N_EXPERTS, t // tok),
        in_specs=[pl.BlockSpec((1, nsub, N_EXPERTS, TOK_BLK), lambda bi, e, tb, s: (bi, tb, 0, 0)),
                  pl.BlockSpec((1, tok, D), lambda bi, e, tb, s: (bi, tb, 0)),
                  pl.BlockSpec((1, D, EXPERT_FF), lambda bi, e, tb, s: (e, 0, 0)),
                  pl.BlockSpec((1, D, EXPERT_FF), lambda bi, e, tb, s: (e, 0, 0)),
                  pl.BlockSpec((1, EXPERT_FF, D), lambda bi, e, tb, s: (e, 0, 0))],
        out_specs=pl.BlockSpec((1, 1, capp, D), lambda bi, e, tb, s: (bi, e, 0, 0)),
        scratch_shapes=[pltpu.VMEM((capp + GATHER_WIN, D), BF16)])
    return pl.pallas_call(
        functools.partial(_moe_kernel, nblk=nblk, nsub=nsub, capp=capp), name="moe", grid_spec=grid_spec,
        out_shape=jax.ShapeDtypeStruct((b, N_EXPERTS, capp, D), BF16),
        compiler_params=_cparams(("arbitrary", "arbitrary", "arbitrary")),
    )(s0_flat, pos4, h2, lw["w_e_gate"], lw["w_e_up"], lw["w_e_down"])


def _combine_kernel(s0_ref, *refs, nblk, nb):
    ys_refs = refs[:2 * N_EXPERTS]
    pos_ref, aff_ref, xm_ref, mod_ref, g_ref, o_ref = refs[2 * N_EXPERTS:]
    bi, tb = pl.program_id(0), pl.program_id(1)
    pos = pos_ref[0]
    aff = aff_ref[0]
    lane = lax.broadcasted_iota(I32, (TOK_BLK, TOK_BLK), 1)
    acc = jnp.zeros((TOK_BLK, D), F32)
    for e in range(N_EXPERTS):
        s0 = s0_ref[(bi * N_EXPERTS + e) * (nblk + 1) + tb]
        blk0 = jnp.minimum(s0 // TOK_BLK, nb - 1)
        rel = pos[:, e:e + 1] - blk0 * TOK_BLK
        y = (_dot((lane == rel).astype(BF16), ys_refs[2 * e][0, 0])
             + _dot((lane + TOK_BLK == rel).astype(BF16), ys_refs[2 * e + 1][0, 0]))
        acc = acc + aff[:, e:e + 1] * y
    mod = mod_ref[0]
    o_ref[0] = xm_ref[0] + mod[5:6] * (_rms(acc) * g_ref[...])


def _combine(s0_flat, ys, pos_t, aff, xm, mod, g_post):
    b, t, _ = xm.shape
    nblk = t // TOK_BLK
    nb = ys.shape[2] // TOK_BLK

    def ys_spec(e, k):
        def ix(bi, tb, s):
            blk0 = jnp.minimum(s[(bi * N_EXPERTS + e) * (nblk + 1) + tb] // TOK_BLK, nb - 1)
            return (bi, e, jnp.minimum(blk0 + k, nb - 1), 0)
        return pl.BlockSpec((1, 1, TOK_BLK, D), ix)

    tok = lambda w: pl.BlockSpec((1, TOK_BLK, w), lambda bi, tb, s: (bi, tb, 0))
    in_specs = [ys_spec(e, k) for e in range(N_EXPERTS) for k in range(2)]
    in_specs += [tok(N_EXPERTS), tok(N_EXPERTS), tok(D), pl.BlockSpec((1, 8, D), lambda bi, tb, s: (bi, 0, 0)),
                 pl.BlockSpec((1, D), lambda bi, tb, s: (0, 0))]
    grid_spec = pltpu.PrefetchScalarGridSpec(num_scalar_prefetch=1, grid=(b, nblk), in_specs=in_specs, out_specs=tok(D))
    return pl.pallas_call(
        functools.partial(_combine_kernel, nblk=nblk, nb=nb), name="combine", grid_spec=grid_spec,
        out_shape=jax.ShapeDtypeStruct((b, t, D), F32),
        compiler_params=_cparams(("arbitrary", "arbitrary")),
    )(s0_flat, *([ys] * (2 * N_EXPERTS)), pos_t, aff, xm, mod, g_post)


def _rope_table(t):
    nf = ROPE_DIM // 4
    pos = jnp.arange(t)
    inv = ROPE_BASE ** (-jnp.arange(nf, dtype=F32) / nf)
    ang = jnp.stack([pos // GRID_W, pos % GRID_W], axis=-1).astype(F32)[..., None] * inv
    cos, sin = jnp.cos(ang), jnp.sin(ang)
    c32 = jnp.stack([cos, cos], axis=2).reshape(t, ROPE_DIM)
    s32 = jnp.stack([-sin, sin], axis=2).reshape(t, ROPE_DIM)
    one, zero = jnp.ones((t, 64), F32), jnp.zeros((t, 32), F32)
    ct = jnp.concatenate([one, c32, zero], axis=1)
    st = jnp.concatenate([0.0 * one, s32, zero], axis=1)
    return jnp.concatenate([ct, st, jnp.tile(c32, (1, 4)), jnp.tile(s32, (1, 4))], axis=1)


def _identity_table(t):
    one, zero = jnp.ones((t, 128), F32), jnp.zeros((t, 128), F32)
    ct = jnp.concatenate([jnp.ones((t, 96), F32), jnp.zeros((t, 32), F32)], axis=1)
    return jnp.concatenate([ct, zero, one, zero], axis=1)


def _layer_weights(i, p):
    lw = {}
    row = lambda a: a.reshape(1, -1)
    for name in ("g_mix_pre", "g_mix_post", "g_ffn_pre", "g_ffn_post", "g_q_lat", "g_kv_lat", "g_mlstm_out", "g_gla_out"):
        lw[name] = row(p[name][i])
    lw["w_ext"] = _gather_cols(p["w_in"][i], _WIN_IDX).astype(BF16)
    lw["wgt"] = p["w_in"][i][:, 416 + 1024:416 + 1040].T.astype(BF16)
    gb = jnp.concatenate([p["b_igate"][i].reshape(-1), p["b_fgate"][i].reshape(-1)])
    lw["gate_bias_row"] = jnp.pad(gb, (0, LANE - 16)).reshape(1, LANE)
    lw["gate_bias_col"] = gb.reshape(16, 1)
    wuq = p["w_uq"][i]
    qi = -np.ones((512,), np.int64)
    qsi = -np.ones((512,), np.int64)
    for h in range(N_HEADS):
        qi[128 * h:128 * h + 96] = 96 * h + np.arange(96)
        qsi[128 * h + 64:128 * h + 96] = 96 * h + 64 + (np.arange(32) ^ 8)
    lw["wq"] = _gather_cols(wuq, qi).astype(BF16)
    lw["wqs"] = _gather_cols(wuq, qsi).astype(BF16)
    ki = -np.ones((512,), np.int64)
    for h in range(N_HEADS):
        ki[128 * h:128 * h + 64] = 128 * h + np.arange(64)
    lw["wk"] = _gather_cols(p["w_ukv"][i], ki).astype(BF16)
    lw["wvt"] = _vt_rows(p["w_ukv"][i].reshape(MLA_KV_LORA, N_HEADS, 128)[:, :, 64:].reshape(MLA_KV_LORA, 256)).astype(BF16)
    lw["wdvt"] = _vt_rows(p["w_in"][i][:, 2768:3024]).astype(BF16)
    wa = p["w_alpha2"][i]
    wal = jnp.zeros((LANE, 256), F32).at[0:16, 0:128].set(wa[0]).at[16:32, 128:256].set(wa[1])
    lw["walpha"] = wal.astype(BF16)
    lw["balpha"] = p["b_alpha"][i].reshape(1, 256)
    lw["w_conv"] = p["w_conv"][i]
    lw["b_conv"] = row(p["b_conv"][i])
    lw["dlam"] = p["diff_lambda"][i]
    lw["g_diff"] = jnp.broadcast_to(p["g_diff_out"][i].reshape(N_HEADS, DIFF_DV, 1), (N_HEADS, DIFF_DV, LANE))
    wb = p["w_branch"][i]
    lw["wbr"] = jnp.concatenate([_pad_heads_rows(wb[0], 64), wb[1], wb[2], _pad_heads_rows(wb[3], 64)], axis=0).astype(BF16)
    lw["w_gate"] = p["w_gate"][i].astype(BF16)
    lw["b_gate"] = row(p["b_gate"][i])
    lw["w_out"] = p["w_out"][i].astype(BF16)
    lw["w_router"] = jnp.pad(p["w_router"][i], ((0, 0), (0, LANE - N_EXPERTS))).astype(BF16)
    lw["w_router_t"] = p["w_router"][i].T.astype(BF16)
    lw["w_e_gate"] = p["w_e_gate"][i].astype(BF16)
    lw["w_e_up"] = p["w_e_up"][i].astype(BF16)
    lw["w_e_down"] = p["w_e_down"][i].astype(BF16)
    return lw


def _ffn(xm, h2, aff, aff_t, mod, lw):
    b, t, _ = xm.shape
    nblk = t // TOK_BLK
    cap = EC_CAPACITY * t // N_EXPERTS
    capp = -(-cap // TOK_BLK) * TOK_BLK
    pos4, s04 = _topk(aff_t, cap)
    s0_be = jnp.concatenate([s04[..., 0].transpose(0, 2, 1), jnp.full((b, N_EXPERTS, 1), cap, I32)], axis=-1)
    s0_flat = s0_be.reshape(-1)
    ys = _moe(s0_flat, pos4, h2, lw, capp)
    pos_t = pos4.transpose(0, 1, 3, 2).reshape(b, t, N_EXPERTS)
    return _combine(s0_flat, ys, pos_t, aff, xm, mod, lw["g_ffn_post"])


def _hybrid_layer(i, x_c, x_l, c8, need_ctx, p):
    lw = _layer_weights(i, p)
    b, t, _ = x_l.shape
    tc = x_c.shape[1]
    lam_init = 0.8 - 0.6 * math.exp(-0.3 * i)
    mod8 = _ada(c8, p["w_ada"][i], p["b_ada"][i])
    pad = lambda m: jnp.pad(m.reshape(b, 6, D), ((0, 0), (0, 2), (0, 0)))
    mod_l = pad(mod8[:b])
    mod_c = pad(jnp.broadcast_to(mod8[b:b + 1], (b, 6 * D)))

    pc = _proj(x_c, mod_c, _identity_table(tc), lw)
    pt = _proj(x_l, mod_l, _rope_table(t), lw)

    zc = jnp.zeros((b, 2, 256, 512), F32)
    zm = jnp.full((b, 2, 8, 256), NEG, F32)
    zs = jnp.zeros((b, 2, 128, 256), F32)
    hf_c, hb_c, c_fin, m_fin = _mlstm(pc, lw["w_conv"], lw["b_conv"], zc, zm)
    hf_l, hb_l, _, _ = _mlstm(pt, lw["w_conv"], lw["b_conv"], c_fin, m_fin)
    gf_c, gb_c, s_fin = _gla(pc, zs)
    gf_l, gb_l, _ = _gla(pt, s_fin)

    one_g = jnp.ones((N_HEADS, MLA_V, LANE), F32)
    fl = functools.partial(_flash, tk=512)
    ya_l = fl(pt["mq"], pc["mk"], pc["mvt"], pt["mk"], pt["mvt"], lw["dlam"], one_g, nmap=1, finish=False, post=1.0,
              tq=min(FLASH_ROWS, t))
    yd_l = fl(pt["dq"], pc["dk"], pc["dvt"], pt["dk"], pt["dvt"], lw["dlam"], lw["g_diff"], nmap=2, finish=True,
              post=1.0 - lam_init, tq=min(FLASH_ROWS // 2, t))
    xm, h2, aff, aff_t = _merge(x_l, mod_l, ya_l, (hf_l, hb_l), pt["lo"], (gf_l, gb_l), pt["gr"], yd_l, lw, tm=256)
    x_l = _ffn(xm, h2, aff, aff_t, mod_l, lw)

    if need_ctx:
        ya_c = fl(pc["mq"], pc["mk"], pc["mvt"], None, None, lw["dlam"], one_g, nmap=1, finish=False, post=1.0, tq=tc)
        yd_c = fl(pc["dq"], pc["dk"], pc["dvt"], None, None, lw["dlam"], lw["g_diff"], nmap=2, finish=True,
                  post=1.0 - lam_init, tq=tc)
        xm, h2, aff, aff_t = _merge(x_c, mod_c, ya_c, (hf_c, hb_c), pc["lo"], (gf_c, gb_c), pc["gr"], yd_c, lw, tm=tc)
        x_c = _ffn(xm, h2, aff, aff_t, mod_c, lw)
    return x_c, x_l


def kernel(x, c, ctx, c_ctx, w_ada, b_ada, g_mix_pre, g_mix_post, g_ffn_pre, g_ffn_post, w_in, g_q_lat, w_uq, g_kv_lat, w_ukv, w_conv, b_conv, b_igate, b_fgate, g_mlstm_out, w_alpha2, b_alpha, g_gla_out, diff_lambda, g_diff_out, w_branch, w_gate, b_gate, w_out, w_router, w_e_gate, w_e_up, w_e_down):
    p = dict(w_ada=w_ada, b_ada=b_ada, g_mix_pre=g_mix_pre, g_mix_post=g_mix_post, g_ffn_pre=g_ffn_pre,
             g_ffn_post=g_ffn_post, w_in=w_in, g_q_lat=g_q_lat, w_uq=w_uq, g_kv_lat=g_kv_lat, w_ukv=w_ukv,
             w_conv=w_conv, b_conv=b_conv, b_igate=b_igate, b_fgate=b_fgate, g_mlstm_out=g_mlstm_out,
             w_alpha2=w_alpha2, b_alpha=b_alpha, g_gla_out=g_gla_out, diff_lambda=diff_lambda, g_diff_out=g_diff_out,
             w_branch=w_branch, w_gate=w_gate, b_gate=b_gate, w_out=w_out, w_router=w_router,
             w_e_gate=w_e_gate, w_e_up=w_e_up, w_e_down=w_e_down)
    b = x.shape[0]
    c8 = jnp.concatenate([c, c_ctx[None], jnp.zeros((8 - b - 1, D), F32)], axis=0)
    x_c, x_l = ctx, x
    for i in range(DEPTH):
        x_c, x_l = _hybrid_layer(i, x_c, x_l, c8, i < DEPTH - 1, p)
    return x_l
```

```python
import functools
import math

import numpy as np
import jax
import jax.numpy as jnp
from jax import lax
from jax.experimental import pallas as pl
from jax.experimental.pallas import tpu as pltpu

F32 = jnp.float32
BF16 = jnp.bfloat16
I32 = jnp.int32
HIGHEST = lax.Precision.HIGHEST

D = 1024
DEPTH = 2
GRID_W = 64
N_HEADS = 4
MLA_NOPE, MLA_ROPE, MLA_V = 64, 32, 64
MLA_Q_LORA, MLA_KV_LORA = 256, 128
ML_DH = 64
GLA_DK, GLA_DV, GLA_RANK, GLA_TAU = 32, 64, 16, 16.0
DIFF_DQK, DIFF_DV = 32, 64
ROPE_DIM, ROPE_BASE = 32, 10000.0
N_EXPERTS, EC_CAPACITY, EXPERT_FF = 16, 2, 1408
NEG = -1e30
EPS = 1e-6
LOG2E = 1.4426950408889634

LANE = 128
HEAD_SLAB = 128
TOK_BLK = 256
ML_CHUNK = 128
GLA_CHUNK = 64
GLA_BLOCK = 256
GATHER_WIN = TOK_BLK + 16
GATHER_WIN_SMALL = 64
MOE_TOK = 1024
SLOT_BLK = 128
VMEM_LIMIT = 56 * 1024 * 1024

ZQ, ZKV, ZKRA, ZKRB, ZMLQK, ZMLV, ZMLO, ZGATE, ZGA = 0, 256, 384, 512, 640, 1152, 1408, 1664, 1792
ZGQ, ZGK, ZGV, ZGR, ZDQ, ZDQS, ZDK, ZDKS, NZ = 1920, 2048, 2176, 2432, 2688, 2944, 3200, 3456, 3712
KV_CHUNK = 256
VT_ROWS = 80
FLASH_ROWS = 512
FLASH_UNROLL = 8


def _swap32(c):
    return (c // 32) * 32 + ((c % 32) ^ 8)


def _win_index():
    idx = -np.ones((NZ,), np.int64)
    idx[ZQ:ZQ + 256] = np.arange(0, 256)
    idx[ZKV:ZKV + 128] = np.arange(256, 384)
    r = np.arange(32)
    idx[ZKRA + 64:ZKRA + 96] = 384 + r
    idx[ZKRB + 64:ZKRB + 96] = 384 + (r ^ 8)
    ml = 416
    idx[ZMLQK:ZMLQK + 512] = ml + np.arange(512)
    idx[ZMLV:ZMLV + 256] = ml + 512 + np.arange(256)
    idx[ZMLO:ZMLO + 256] = ml + 768 + np.arange(256)
    idx[ZGATE:ZGATE + 16] = ml + 1024 + np.arange(16)
    gl = 1456
    idx[ZGQ:ZGQ + 128] = gl + np.arange(128)
    idx[ZGK:ZGK + 128] = gl + 128 + np.arange(128)
    idx[ZGV:ZGV + 256] = gl + 256 + np.arange(256)
    idx[ZGR:ZGR + 256] = gl + 512 + np.arange(256)
    idx[ZGA:ZGA + 32] = gl + 768 + np.arange(32)
    df = 2256
    c = np.arange(256)
    idx[ZDQ:ZDQ + 256] = df + c
    idx[ZDQS:ZDQS + 256] = df + _swap32(c)
    idx[ZDK:ZDK + 256] = df + 256 + c
    idx[ZDKS:ZDKS + 256] = df + 256 + _swap32(c)
    return idx


def _vt_rows(w_cols):
    n = w_cols.shape[0]
    w4 = w_cols.T.reshape(N_HEADS, 64, n)
    return jnp.pad(w4, ((0, 0), (0, VT_ROWS - 64), (0, 0))).reshape(N_HEADS * VT_ROWS, n)


_WIN_IDX = _win_index()


def _gather_cols(w, idx):
    safe = np.maximum(idx, 0)
    return jnp.where(jnp.asarray(idx >= 0)[None, :], w[:, safe], 0.0)


def _pad_heads_rows(w, width):
    n = w.shape[1]
    w4 = w.reshape(N_HEADS, width, n)
    return jnp.pad(w4, ((0, 0), (0, HEAD_SLAB - width), (0, 0))).reshape(N_HEADS * HEAD_SLAB, n)


def _cparams(sem):
    return pltpu.CompilerParams(dimension_semantics=sem, vmem_limit_bytes=VMEM_LIMIT)


def _rms(x):
    return x * lax.rsqrt(jnp.mean(x * x, axis=-1, keepdims=True) + EPS)


def _sigmoid(x):
    return 1.0 / (1.0 + jnp.exp(-x))


def _log_sigmoid(x):
    return jnp.minimum(x, 0.0) - jnp.log1p(jnp.exp(-jnp.abs(x)))


def _dot(a, b, precision=None):
    return jnp.dot(a, b, preferred_element_type=F32, precision=precision)


def _dot_nt(a, b, precision=None):
    return lax.dot_general(a, b, (((1,), (1,)), ((), ())), preferred_element_type=F32, precision=precision)


def _dot_tn(a, b, precision=None):
    return lax.dot_general(a, b, (((0,), (0,)), ((), ())), preferred_element_type=F32, precision=precision)


def _const_spec(shape):
    nd = len(shape)
    return pl.BlockSpec(shape, lambda *_: (0,) * nd)


def _ada_kernel(c_ref, w_ref, b_ref, o_ref):
    cv = c_ref[...]
    s = (cv * _sigmoid(cv)).astype(BF16)
    o_ref[...] = _dot(s, w_ref[...].astype(BF16)) + b_ref[...]


def _ada(c8, w_ada, b_ada):
    n, tn = 6 * D, 1024
    return pl.pallas_call(
        _ada_kernel, name="ada", grid=(n // tn,),
        in_specs=[pl.BlockSpec((8, D), lambda j: (0, 0)), pl.BlockSpec((D, tn), lambda j: (0, j)),
                  pl.BlockSpec((1, tn), lambda j: (0, j))],
        out_specs=pl.BlockSpec((8, tn), lambda j: (0, j)),
        out_shape=jax.ShapeDtypeStruct((8, n), F32), compiler_params=_cparams(("arbitrary",)),
    )(c8, w_ada, b_ada.reshape(1, n))


_PROJ_OUT = (
    ("mq", 512, BF16), ("mk", 512, BF16),
    ("lqk", 512, F32), ("lv", 256, BF16), ("lo", 256, BF16), ("gc", 128, F32),
    ("gq", 128, BF16), ("gk", 128, BF16), ("gv", 256, BF16), ("gr", 256, BF16), ("glg", 256, F32),
    ("dq", 1024, BF16), ("dk", 512, BF16),
)


def _proj_kernel(x_ref, mod_ref, g_ref, w_ref, tab_ref, gq_ref, wq_ref, wqs_ref, gkv_ref, wk_ref, wvt_ref,
                 wgt_ref, gbr_ref, gbc_ref, wal_ref, bal_ref, wdvt_ref,
                 mq_ref, mk_ref, lqk_ref, lv_ref, lo_ref, gc_ref, gq_o, gk_o, gv_o, gr_o, glg_o,
                 dq_ref, dk_ref, grow_ref, mvt_ref, dvt_ref):
    x = x_ref[0]
    tm = x.shape[0]
    mod = mod_ref[0]
    h = _rms(x) * g_ref[...] * (1.0 + mod[1:2]) + mod[0:1]
    hb = h.astype(BF16)
    z = _dot(hb, w_ref[...])
    tab = tab_ref[...]
    ct, st, cd, sd = tab[:, 0:128], tab[:, 128:256], tab[:, 256:384], tab[:, 384:512]
    lane = lax.broadcasted_iota(I32, (tm, LANE), 1)

    qn = (_rms(z[:, ZQ:ZQ + 256]) * gq_ref[...]).astype(BF16)
    qa = _dot(qn, wq_ref[...])
    qb = _dot(qn, wqs_ref[...])
    qscale = (MLA_NOPE + MLA_ROPE) ** -0.5 * LOG2E
    for hh in range(N_HEADS):
        sl = slice(HEAD_SLAB * hh, HEAD_SLAB * (hh + 1))
        mq_ref[0, :, sl] = ((qa[:, sl] * ct + qb[:, sl] * st) * qscale).astype(BF16)
    kvn = (_rms(z[:, ZKV:ZKV + 128]) * gkv_ref[...]).astype(BF16)
    kk = _dot(kvn, wk_ref[...])
    kr = z[:, ZKRA:ZKRA + 128] * ct + z[:, ZKRB:ZKRB + 128] * st
    for hh in range(N_HEADS):
        sl = slice(HEAD_SLAB * hh, HEAD_SLAB * (hh + 1))
        mk_ref[0, :, sl] = (kk[:, sl] + kr).astype(BF16)
    ones_row = lax.broadcasted_iota(I32, (N_HEADS * VT_ROWS, tm), 0) % VT_ROWS == MLA_V
    mvt_ref[0, 0] = jnp.where(ones_row, 1.0, _dot_nt(wvt_ref[...], kvn)).astype(BF16)
    dvt_ref[0, 0] = jnp.where(ones_row, 1.0, _dot_nt(wdvt_ref[...], hb)).astype(BF16)

    lqk_ref[0] = z[:, ZMLQK:ZMLQK + 512]
    lv_ref[0] = z[:, ZMLV:ZMLV + 256].astype(BF16)
    lo_ref[0] = z[:, ZMLO:ZMLO + 256].astype(BF16)
    gcol = z[:, ZGATE:ZGATE + 128] + gbr_ref[...]
    gc_ref[0] = jnp.where(lane < 8, gcol, jnp.where(lane < 16, _log_sigmoid(gcol), 0.0))
    zr = _dot_nt(wgt_ref[...], hb) + gbc_ref[...]
    rowi = lax.broadcasted_iota(I32, zr.shape, 0)
    grow_ref[0] = jnp.where(rowi < 8, zr, _log_sigmoid(zr))

    gq_o[0] = (z[:, ZGQ:ZGQ + 128] * GLA_DK ** -0.5).astype(BF16)
    gk_o[0] = z[:, ZGK:ZGK + 128].astype(BF16)
    gv_o[0] = z[:, ZGV:ZGV + 256].astype(BF16)
    gr_o[0] = z[:, ZGR:ZGR + 256].astype(BF16)
    zg = _dot(z[:, ZGA:ZGA + 128].astype(BF16), wal_ref[...]) + bal_ref[...]
    glg_o[0] = _log_sigmoid(zg) * (1.0 / GLA_TAU)

    dscale = DIFF_DQK ** -0.5 * LOG2E
    for g in range(2):
        gs = slice(128 * g, 128 * (g + 1))
        qg = (z[:, ZDQ:ZDQ + 256][:, gs] * cd + z[:, ZDQS:ZDQS + 256][:, gs] * sd) * dscale
        kg = z[:, ZDK:ZDK + 256][:, gs] * cd + z[:, ZDKS:ZDKS + 256][:, gs] * sd
        for hl in range(2):
            hh = 2 * g + hl
            for m in range(2):
                lo = 64 * hl + 32 * m
                s0 = (2 * hh + m) * HEAD_SLAB
                dq_ref[0, :, s0:s0 + HEAD_SLAB] = jnp.where((lane >= lo) & (lane < lo + 32), qg, 0.0).astype(BF16)
            dk_ref[0, :, HEAD_SLAB * hh:HEAD_SLAB * (hh + 1)] = jnp.where(
                (lane >= 64 * hl) & (lane < 64 * hl + 64), kg, 0.0).astype(BF16)


def _proj(x, mod, tab, lw):
    b, t, _ = x.shape
    tm = KV_CHUNK
    consts = [lw["g_mix_pre"], lw["w_ext"], None, lw["g_q_lat"], lw["wq"], lw["wqs"], lw["g_kv_lat"], lw["wk"], lw["wvt"],
              lw["wgt"], lw["gate_bias_row"], lw["gate_bias_col"], lw["walpha"], lw["balpha"], lw["wdvt"]]
    in_specs = [pl.BlockSpec((1, tm, D), lambda bi, i: (bi, i, 0)), pl.BlockSpec((1, 8, D), lambda bi, i: (bi, 0, 0))]
    args = [x, mod]
    for cst in consts:
        if cst is None:
            in_specs.append(pl.BlockSpec((tm, 512), lambda bi, i: (i, 0)))
            args.append(tab)
        else:
            in_specs.append(_const_spec(cst.shape))
            args.append(cst)
    out_specs = [pl.BlockSpec((1, tm, w), lambda bi, i: (bi, i, 0)) for _, w, _ in _PROJ_OUT]
    out_shape = [jax.ShapeDtypeStruct((b, t, w), dt) for _, w, dt in _PROJ_OUT]
    out_specs.append(pl.BlockSpec((1, 16, tm), lambda bi, i: (bi, 0, i)))
    out_shape.append(jax.ShapeDtypeStruct((b, 16, t), F32))
    for _ in range(2):
        out_specs.append(pl.BlockSpec((1, 1, N_HEADS * VT_ROWS, tm), lambda bi, i: (bi, i, 0, 0)))
        out_shape.append(jax.ShapeDtypeStruct((b, t // tm, N_HEADS * VT_ROWS, tm), BF16))
    outs = pl.pallas_call(
        _proj_kernel, name="proj", grid=(b, t // tm), in_specs=in_specs, out_specs=out_specs, out_shape=out_shape,
        compiler_params=_cparams(("parallel", "arbitrary")),
    )(*args)
    res = {name: o for (name, _, _), o in zip(_PROJ_OUT, outs[:-3])}
    res["grow"], res["mvt"], res["dvt"] = outs[-3:]
    return res


def _flash_kernel(*refs, nmap, has_lat, tk, finish, post):
    if has_lat:
        q_ref, kc_ref, vc_ref, kl_ref, vl_ref, dl_ref, g_ref, o_ref, s_ref = refs
    else:
        q_ref, kc_ref, vc_ref, dl_ref, g_ref, o_ref = refs
    qb = q_ref[0]
    tq = qb.shape[0]
    q = qb if nmap == 1 else jnp.concatenate([qb[:, :HEAD_SLAB], qb[:, HEAD_SLAB:]], axis=0)
    rows = nmap * tq
    sub = tk // KV_CHUNK

    def softmax(s, smax, m):
        m_new = jnp.maximum(m, smax)
        return m_new, jnp.exp2(m - m_new), jnp.exp2(s - m_new).astype(BF16)

    def pv(p, vts):
        out = _dot(vts[0], p[0:KV_CHUNK])
        for c in range(1, len(vts)):
            out = out + _dot(vts[c], p[c * KV_CHUNK:(c + 1) * KV_CHUNK])
        return out

    s_ctx = _dot_nt(kc_ref[0], q)
    m, _, p = softmax(s_ctx, jnp.max(s_ctx, axis=0, keepdims=True), jnp.full((1, rows), NEG, F32))
    acc = pv(p, [vc_ref[0, 0]])
    if has_lat:
        n = kl_ref.shape[1] // tk

        unroll = min(FLASH_UNROLL, n)

        def scores(j):
            if isinstance(j, int):
                return _dot_nt(kl_ref[0, j * tk:(j + 1) * tk, :], q)
            off = pl.multiple_of(j * tk, tk)
            return _dot_nt(kl_ref[0, pl.ds(off, tk), :], q)

        def values(j):
            return [vl_ref[0, j * sub + c] for c in range(sub)]

        def produce(slot, j):
            s = scores(j)
            s_ref[slot] = s
            return jnp.max(s, axis=0, keepdims=True)

        smax0 = produce(0, 0)

        def body(jj, carry):
            m, acc, smax = carry
            j = unroll * jj
            for u in range(unroll):
                smax_next = smax
                if not isinstance(j, int):
                    smax_next = produce((u + 1) % 2, jnp.minimum(j + u + 1, n - 1))
                elif j + u + 1 < n:
                    smax_next = produce((u + 1) % 2, j + u + 1)
                m, alpha, p = softmax(s_ref[u % 2], smax, m)
                acc = alpha * acc + pv(p, values(j + u))
                smax = smax_next
            return m, acc, smax

        if unroll == n:
            m, acc, _ = body(0, (m, acc, smax0))
        else:
            m, acc, _ = lax.fori_loop(0, n // unroll, body, (m, acc, smax0))

    o = acc[0:MLA_V, :] / acc[MLA_V:MLA_V + 1, :]
    if nmap == 2:
        lv = dl_ref[...]
        lam = (jnp.exp(jnp.sum(lv[0:1] * lv[1:2], axis=-1, keepdims=True))
               - jnp.exp(jnp.sum(lv[2:3] * lv[3:4], axis=-1, keepdims=True)) + (1.0 - post))
        o = o[:, :tq] - lam * o[:, tq:]
    if finish:
        ms = jnp.mean(o * o, axis=0, keepdims=True)
        o = o * lax.rsqrt(ms + EPS) * jnp.concatenate([g_ref[0]] * (tq // LANE), axis=1) * post
    o_pad = jnp.concatenate([o, jnp.zeros((HEAD_SLAB - MLA_V, tq), F32)], axis=0)
    o_ref[0] = o_pad.T.astype(BF16)


def _flash(q, kc, vct, kl, vlt, dlam, g_out, *, nmap, finish, post, tq, tk):
    b, t, _ = q.shape
    has_lat = kl is not None
    assert kc.shape[1] == KV_CHUNK and tk % KV_CHUNK == 0
    qw = nmap * HEAD_SLAB
    kspec = lambda n: pl.BlockSpec((1, n, HEAD_SLAB), lambda bi, h, i: (bi, 0, h))
    vspec = lambda n: pl.BlockSpec((1, n // KV_CHUNK, VT_ROWS, KV_CHUNK), lambda bi, h, i: (bi, 0, h, 0))
    in_specs = [pl.BlockSpec((1, tq, qw), lambda bi, h, i: (bi, i, h)), kspec(KV_CHUNK), vspec(KV_CHUNK)]
    args = [q, kc, vct]
    scratch = []
    if has_lat:
        tl = kl.shape[1]
        assert (tl // tk) % min(FLASH_UNROLL, tl // tk) == 0
        in_specs += [kspec(tl), vspec(tl)]
        args += [kl, vlt]
        scratch = [pltpu.VMEM((2, tk, nmap * tq), F32)]
    in_specs += [_const_spec(dlam.shape), pl.BlockSpec((1, MLA_V, LANE), lambda bi, h, i: (h, 0, 0))]
    args += [dlam, g_out]
    return pl.pallas_call(
        functools.partial(_flash_kernel, nmap=nmap, has_lat=has_lat, tk=tk, finish=finish, post=post),
        name="flash_diff" if nmap == 2 else "flash_mla",
        grid=(b, N_HEADS, t // tq), in_specs=in_specs,
        out_specs=pl.BlockSpec((1, tq, HEAD_SLAB), lambda bi, h, i: (bi, i, h)),
        out_shape=jax.ShapeDtypeStruct((b, t, N_HEADS * HEAD_SLAB), BF16),
        scratch_shapes=scratch,
        compiler_params=_cparams(("parallel", "parallel", "arbitrary")),
    )(*args)


def _head_of(shape, axis, width):
    return (lax.broadcasted_iota(I32, shape, axis) % (N_HEADS * width)) // width


def _mlstm_dir(d, first, last, x, xprev, xnext, v, gcol, grow, wc, bcv, c_ref, m_ref):
    L = x.shape[0]
    row = lax.broadcasted_iota(I32, x.shape, 0)
    pr = jnp.where(first, 0.0, xprev[7:8, :])
    nx = jnp.where(last, 0.0, xnext[0:1, :])
    xm = jnp.where(row == 0, pr, pltpu.roll(x, 1, 0))
    xp = jnp.where(row == L - 1, nx, pltpu.roll(x, L - 1, 0))
    y = xm * wc[0:1] + x * wc[1:2] + xp * wc[2:3] + bcv
    qk = y * _sigmoid(y)
    q = qk[:, :256]
    k = qk[:, 256:] * ML_DH ** -0.5

    li = lax.broadcasted_iota(I32, (L, L), 0)
    si = lax.broadcasted_iota(I32, (L, L), 1)
    tin = (si <= li) if d == 0 else (si >= li)
    tinf = tin.astype(F32)
    bcol = _dot(tinf, gcol, HIGHEST)
    brow = _dot_nt(grow, tinf, HIGHEST)
    m0e = m_ref[0, d]
    cb = c_ref[0, d]
    hm256 = _head_of((L, 256), 1, ML_DH)
    hm512 = _head_of((L, 512), 1, ML_DH)
    e_idx = L - 1 if d == 0 else 0

    d_blk, inter_blk = [], []
    for hh in range(N_HEADS):
        c = 4 * d + hh
        bc = bcol[:, 8 + c:9 + c]
        d_blk.append(jnp.where(tin, bc - brow[8 + c:9 + c, :] + grow[c:c + 1, :], NEG))
        inter_blk.append(bc + m0e[0:1, 64 * hh:64 * hh + 1])
    d_st = jnp.concatenate(d_blk, axis=0)
    inter_st = jnp.concatenate(inter_blk, axis=0)
    mt = jnp.maximum(inter_st, jnp.max(d_st, axis=-1, keepdims=True))
    q_st = jnp.concatenate([jnp.where(hm256 == hh, q, 0.0) for hh in range(N_HEADS)], axis=0).astype(BF16)
    s_st = (jnp.exp(d_st - mt) * _dot_nt(q_st, k.astype(BF16))).astype(BF16)
    vext = jnp.concatenate([v, jnp.ones((L, 256), BF16)], axis=1)
    r = _dot(s_st, vext)
    aint = jnp.exp(inter_st - mt)
    p = _dot(q.astype(BF16), cb.astype(BF16))
    tot = jnp.zeros((L, 512), F32)
    mte = jnp.zeros((L, 256), F32)
    for hh in range(N_HEADS):
        rs = slice(hh * L, (hh + 1) * L)
        tot = jnp.where(hm512 == hh, r[rs] + aint[rs] * p, tot)
        mte = jnp.where(hm256 == hh, mt[rs], mte)
    hout = tot[:, :256] / jnp.maximum(jnp.abs(tot[:, 256:]), jnp.exp(-mte))

    wexp = jnp.zeros((L, 256), F32)
    arow = jnp.zeros((1, 512), F32)
    grw = jnp.zeros((1, 512), F32)
    mnew = jnp.zeros((1, 256), F32)
    hr512 = _head_of((1, 512), 1, ML_DH)
    hr256 = _head_of((1, 256), 1, ML_DH)
    for hh in range(N_HEADS):
        c = 4 * d + hh
        bc = bcol[:, 8 + c:9 + c]
        be = bc[e_idx:e_idx + 1, :]
        wl = be - bc + gcol[:, c:c + 1]
        mloc = jnp.max(wl, axis=0, keepdims=True)
        m0h = m0e[0:1, 64 * hh:64 * hh + 1]
        mn = jnp.maximum(be + m0h, mloc)
        wexp = jnp.where(hm256 == hh, jnp.exp(wl - mloc), wexp)
        arow = jnp.where(hr512 == hh, jnp.exp(be + m0h - mn), arow)
        grw = jnp.where(hr512 == hh, jnp.exp(mloc - mn), grw)
        mnew = jnp.where(hr256 == hh, mn, mnew)
    cl = _dot_tn((k * wexp).astype(BF16), vext)
    bd = lax.broadcasted_iota(I32, (256, 512), 0) // ML_DH == _head_of((256, 512), 1, ML_DH)
    c_ref[0, d] = arow * cb + jnp.where(bd, grw * cl, 0.0)
    m_ref[0, d] = jnp.broadcast_to(mnew, (8, 256))
    return hout


def _mlstm_kernel(xf, xfp, xfn, xb, xbp, xbn, vf, vb, gcf, gcb, grf, grb, wc_ref, bc_ref, c0_ref, m0_ref,
                  hf_ref, hb_ref, c_ref, m_ref):
    i = pl.program_id(1)
    n = pl.num_programs(1)

    @pl.when(i == 0)
    def _():
        c_ref[...] = c0_ref[...]
        m_ref[...] = m0_ref[...]

    wc = wc_ref[...]
    bcv = bc_ref[...]
    hf_ref[0] = _mlstm_dir(0, i == 0, i == n - 1, xf[0], xfp[0], xfn[0], vf[0], gcf[0], grf[0], wc, bcv, c_ref, m_ref)
    hb_ref[0] = _mlstm_dir(1, i == n - 1, i == 0, xb[0], xbp[0], xbn[0], vb[0], gcb[0], grb[0], wc, bcv, c_ref, m_ref)


def _mlstm(pr, w_conv, b_conv, c0, m0):
    x, v, gc, gr = pr["lqk"], pr["lv"], pr["gc"], pr["grow"]
    b, t, _ = x.shape
    L = ML_CHUNK
    n = t // L
    r8 = L // 8
    last8 = t // 8 - 1

    def fw(bi, i):
        return (bi, i, 0)

    def bw(bi, i):
        return (bi, n - 1 - i, 0)

    def halo(ix, shift):
        def f(bi, i):
            blk = ix(bi, i)[1]
            return (bi, jnp.clip(blk * r8 + shift, 0, last8), 0)
        return f

    main = lambda w, ix: pl.BlockSpec((1, L, w), ix)
    in_specs = [main(512, fw), pl.BlockSpec((1, 8, 512), halo(fw, -1)), pl.BlockSpec((1, 8, 512), halo(fw, r8)),
                main(512, bw), pl.BlockSpec((1, 8, 512), halo(bw, -1)), pl.BlockSpec((1, 8, 512), halo(bw, r8)),
                main(256, fw), main(256, bw), main(128, fw), main(128, bw),
                pl.BlockSpec((1, 16, L), lambda bi, i: (bi, 0, i)), pl.BlockSpec((1, 16, L), lambda bi, i: (bi, 0, n - 1 - i)),
                _const_spec(w_conv.shape), _const_spec(b_conv.shape),
                pl.BlockSpec((1, 2, 256, 512), lambda bi, i: (bi, 0, 0, 0)), pl.BlockSpec((1, 2, 8, 256), lambda bi, i: (bi, 0, 0, 0))]
    out_specs = [main(256, fw), main(256, bw),
                 pl.BlockSpec((1, 2, 256, 512), lambda bi, i: (bi, 0, 0, 0)), pl.BlockSpec((1, 2, 8, 256), lambda bi, i: (bi, 0, 0, 0))]
    out_shape = [jax.ShapeDtypeStruct((b, t, 256), F32), jax.ShapeDtypeStruct((b, t, 256), F32),
                 jax.ShapeDtypeStruct(c0.shape, F32), jax.ShapeDtypeStruct(m0.shape, F32)]
    return pl.pallas_call(
        _mlstm_kernel, name="mlstm", grid=(b, n), in_specs=in_specs, out_specs=out_specs, out_shape=out_shape,
        compiler_params=_cparams(("parallel", "arbitrary")),
    )(x, x, x, x, x, x, v, v, gc, gc, gr, gr, w_conv, b_conv, c0, m0)


def _gla_chunk(d, q, k, v, lg, sb):
    L = q.shape[0]
    li = lax.broadcasted_iota(I32, (L, L), 0)
    si = lax.broadcasted_iota(I32, (L, L), 1)
    tin = (si <= li) if d == 0 else (si >= li)
    tinf = tin.astype(F32)
    lgd = lg[:, 128 * d:128 * (d + 1)]
    gcum = _dot(tinf, lgd, HIGHEST)
    e_idx = L - 1 if d == 0 else 0
    gend = gcum[e_idx:e_idx + 1, :]
    qf, kf = q.astype(F32), k.astype(F32)
    q_dec = qf * jnp.exp(gcum)
    k_dec = (kf * jnp.exp(-gcum)).astype(BF16)
    k_end = (kf * jnp.exp(gend - gcum)).astype(BF16)
    hm128 = _head_of((L, 128), 1, GLA_DK)
    hm256 = _head_of((L, 256), 1, GLA_DV)
    q_st = jnp.concatenate([jnp.where(hm128 == hh, q_dec, 0.0) for hh in range(N_HEADS)], axis=0).astype(BF16)
    att = _dot_nt(q_st, k_dec)
    tin4 = jnp.concatenate([tin] * N_HEADS, axis=0)
    o_st = _dot(jnp.where(tin4, att, 0.0).astype(BF16), v)
    o = _dot(q_dec.astype(BF16), sb.astype(BF16))
    for hh in range(N_HEADS):
        o = o + jnp.where(hm256 == hh, o_st[hh * L:(hh + 1) * L], 0.0)
    gend_col = _dot_tn(lgd, jnp.ones((L, 256), F32), HIGHEST)
    bd = lax.broadcasted_iota(I32, (128, 256), 0) // GLA_DK == _head_of((128, 256), 1, GLA_DV)
    return o, jnp.exp(gend_col) * sb + jnp.where(bd, _dot_tn(k_end, v), 0.0)


def _gla_kernel(qf, kf, vf, lf, qb, kb, vb, lb, s0_ref, of_ref, ob_ref, s_ref):
    i = pl.program_id(1)

    @pl.when(i == 0)
    def _():
        s_ref[...] = s0_ref[...]

    L = GLA_CHUNK
    nsub = qf.shape[1] // L
    for d, (q, k, v, lg, o_ref) in enumerate(((qf, kf, vf, lf, of_ref), (qb, kb, vb, lb, ob_ref))):
        sb = s_ref[0, d]
        for c in (range(nsub) if d == 0 else reversed(range(nsub))):
            sl = slice(c * L, (c + 1) * L)
            o_ref[0, sl, :], sb = _gla_chunk(d, q[0, sl, :], k[0, sl, :], v[0, sl, :], lg[0, sl, :], sb)
        s_ref[0, d] = sb


def _gla(pr, s0):
    q, k, v, lg = pr["gq"], pr["gk"], pr["gv"], pr["glg"]
    b, t, _ = q.shape
    L = min(GLA_BLOCK, t)
    n = t // L
    fw = lambda bi, i: (bi, i, 0)
    bw = lambda bi, i: (bi, n - 1 - i, 0)
    blk = lambda w, ix: pl.BlockSpec((1, L, w), ix)
    st_spec = pl.BlockSpec((1, 2, 128, 256), lambda bi, i: (bi, 0, 0, 0))
    return pl.pallas_call(
        _gla_kernel, name="gla", grid=(b, n),
        in_specs=[blk(128, fw), blk(128, fw), blk(256, fw), blk(256, fw),
                  blk(128, bw), blk(128, bw), blk(256, bw), blk(256, bw), st_spec],
        out_specs=[blk(256, fw), blk(256, bw), st_spec],
        out_shape=[jax.ShapeDtypeStruct((b, t, 256), F32), jax.ShapeDtypeStruct((b, t, 256), F32),
                   jax.ShapeDtypeStruct(s0.shape, F32)],
        compiler_params=_cparams(("parallel", "arbitrary")),
    )(q, k, v, lg, q, k, v, lg, s0)


def _head_rms_expanded(x, width):
    n = x.shape[1]
    bd = (lax.broadcasted_iota(I32, (n, n), 0) // width == lax.broadcasted_iota(I32, (n, n), 1) // width).astype(F32)
    return _dot(x * x, bd, HIGHEST) * (1.0 / width)


def _merge_kernel(x_ref, mod_ref, ya_ref, hf_ref, hb_ref, lo_ref, gf_ref, gb_ref, gr_ref, yd_ref,
                  gpre_ref, wg_ref, bg_ref, wbr_ref, wo_ref, gpost_ref, gffn_ref, wr_ref, wrt_ref, gml_ref, ggla_ref,
                  xm_ref, h2_ref, aff_ref, afft_ref):
    x = x_ref[0]
    tm = x.shape[0]
    mod = mod_ref[0]
    hb = (_rms(x) * gpre_ref[...] * (1.0 + mod[1:2]) + mod[0:1]).astype(BF16)

    hs = hf_ref[0] + hb_ref[0]
    y_ml = _sigmoid(lo_ref[0].astype(F32)) * (hs * lax.rsqrt(_head_rms_expanded(hs, ML_DH) + EPS) * gml_ref[...])
    gs = gf_ref[0] + gb_ref[0]
    rr = gr_ref[0].astype(F32)
    y_gla = rr * _sigmoid(rr) * (gs * lax.rsqrt(_head_rms_expanded(gs, GLA_DV) + EPS) * ggla_ref[...])

    branches = ((ya_ref[0], 0, 512), (y_ml.astype(BF16), 512, 256), (y_gla.astype(BF16), 768, 256), (yd_ref[0], 1024, 512))
    mix = jnp.zeros((tm, D), F32)
    for nb, (yb, r0, rw) in enumerate(branches):
        gate = _sigmoid(_dot(hb, wg_ref[:, nb * D:(nb + 1) * D]) + bg_ref[:, nb * D:(nb + 1) * D])
        mix = mix + gate * _dot(yb, wbr_ref[r0:r0 + rw, :])
    y = _dot(mix.astype(BF16), wo_ref[...])
    xm = x + mod[2:3] * (_rms(y) * gpost_ref[...])
    xm_ref[0] = xm

    h2 = (_rms(xm) * gffn_ref[...] * (1.0 + mod[4:5]) + mod[3:4]).astype(BF16)
    h2_ref[0] = h2
    lane = lax.broadcasted_iota(I32, (tm, LANE), 1)
    lg = jnp.where(lane < N_EXPERTS, _dot(h2, wr_ref[...]), NEG)
    e = jnp.exp(lg - jnp.max(lg, axis=-1, keepdims=True))
    aff_ref[0] = (e / jnp.sum(e, axis=-1, keepdims=True))[:, :N_EXPERTS]
    lt = _dot_nt(wrt_ref[...], h2)
    et = jnp.exp(lt - jnp.max(lt, axis=0, keepdims=True))
    afft_ref[0] = et / jnp.sum(et, axis=0, keepdims=True)


def _merge(x, mod, ya, ml, lo, gl, gr, yd, lw, tm):
    b, t, _ = x.shape
    tok = lambda w: pl.BlockSpec((1, tm, w), lambda bi, i: (bi, i, 0))
    consts = [lw["g_mix_pre"], lw["w_gate"], lw["b_gate"], lw["wbr"], lw["w_out"], lw["g_mix_post"], lw["g_ffn_pre"],
              lw["w_router"], lw["w_router_t"], lw["g_mlstm_out"], lw["g_gla_out"]]
    in_specs = [tok(D), pl.BlockSpec((1, 8, D), lambda bi, i: (bi, 0, 0)), tok(512), tok(256), tok(256), tok(256),
                tok(256), tok(256), tok(256), tok(512)] + [_const_spec(c.shape) for c in consts]
    out_specs = [tok(D), tok(D), tok(N_EXPERTS), pl.BlockSpec((1, N_EXPERTS, tm), lambda bi, i: (bi, 0, i))]
    out_shape = [jax.ShapeDtypeStruct((b, t, D), F32), jax.ShapeDtypeStruct((b, t, D), BF16),
                 jax.ShapeDtypeStruct((b, t, N_EXPERTS), F32), jax.ShapeDtypeStruct((b, N_EXPERTS, t), F32)]
    return pl.pallas_call(
        _merge_kernel, name="merge", grid=(b, t // tm), in_specs=in_specs, out_specs=out_specs, out_shape=out_shape,
        compiler_params=_cparams(("parallel", "arbitrary")),
    )(x, mod, ya, ml[0], ml[1], lo, gl[0], gl[1], gr, yd, *consts)


def _topk_kernel(a_ref, pos_ref, s0_ref, *, cap):
    nblk = a_ref.shape[1]
    bits = pltpu.bitcast(a_ref[0], I32)

    def bisect(i, thr):
        cand = thr | (1 << (30 - i))
        cnt = jnp.sum((bits >= cand).astype(I32), axis=(0, 2), keepdims=True)
        return jnp.where(cnt >= cap, cand, thr)

    thr3 = lax.fori_loop(0, 31, bisect, jnp.zeros((1, N_EXPERTS, 1), I32))
    need3 = cap - jnp.sum((bits > thr3).astype(I32), axis=(0, 2), keepdims=True)
    thr, need = thr3[0], need3[0].astype(F32)
    upper = (lax.broadcasted_iota(I32, (TOK_BLK, TOK_BLK), 0) <= lax.broadcasted_iota(I32, (TOK_BLK, TOK_BLK), 1)).astype(BF16)

    def blk(j, carry):
        c_eq, c_sel = carry
        bj = pltpu.bitcast(a_ref[0, j], I32)
        gt, eq = bj > thr, bj == thr
        cum_eq = _dot(eq.astype(BF16), upper) + c_eq
        sel = gt | (eq & (cum_eq <= need))
        cum_sel = _dot(sel.astype(BF16), upper) + c_sel
        pos_ref[0, j] = jnp.where(sel, cum_sel - 1.0, -1.0).astype(I32)
        s0_ref[0, j] = jnp.broadcast_to(c_sel, (N_EXPERTS, LANE)).astype(I32)
        return cum_eq[:, TOK_BLK - 1:TOK_BLK], cum_sel[:, TOK_BLK - 1:TOK_BLK]

    zero = jnp.zeros((N_EXPERTS, 1), F32)
    lax.fori_loop(0, nblk, blk, (zero, zero))


def _topk(aff_t, cap):
    b, _, t = aff_t.shape
    nblk = t // TOK_BLK
    a4 = aff_t.reshape(b, N_EXPERTS, nblk, TOK_BLK).transpose(0, 2, 1, 3)
    spec = lambda w: pl.BlockSpec((1, nblk, N_EXPERTS, w), lambda bi: (bi, 0, 0, 0))
    return pl.pallas_call(
        functools.partial(_topk_kernel, cap=cap), name="topk", grid=(b,),
        in_specs=[spec(TOK_BLK)], out_specs=[spec(TOK_BLK), spec(LANE)],
        out_shape=[jax.ShapeDtypeStruct((b, nblk, N_EXPERTS, TOK_BLK), I32),
                   jax.ShapeDtypeStruct((b, nblk, N_EXPERTS, LANE), I32)],
        compiler_params=_cparams(("parallel",)),
    )(a4)


def _moe_kernel(s0_ref, pos_ref, h_ref, wg_ref, wu_ref, wd_ref, ys_ref, xs_ref, *, nblk, nsub, capp):
    bi, e, tb = pl.program_id(0), pl.program_id(1), pl.program_id(2)

    @pl.when(tb == 0)
    def _():
        xs_ref[...] = jnp.zeros(xs_ref.shape, BF16)

    base = (bi * N_EXPERTS + e) * (nblk + 1) + tb * nsub
    for sb in range(nsub):
        s0 = s0_ref[base + sb]
        s1 = s0_ref[base + sb + 1]
        a0 = pl.multiple_of((s0 // 16) * 16, 16)

        def gather(win, sb=sb, a0=a0):
            prow = pos_ref[0, sb, pl.ds(e, 1), :]
            slot = lax.broadcasted_iota(I32, (win, TOK_BLK), 0) + a0
            rows = _dot((slot == prow).astype(BF16), h_ref[0, sb * TOK_BLK:(sb + 1) * TOK_BLK, :])
            xs_ref[pl.ds(a0, win), :] = xs_ref[pl.ds(a0, win), :] + rows.astype(BF16)

        pl.when((s1 > s0) & (s1 - a0 <= GATHER_WIN_SMALL))(functools.partial(gather, GATHER_WIN_SMALL))
        pl.when(s1 - a0 > GATHER_WIN_SMALL)(functools.partial(gather, GATHER_WIN))

    @pl.when(tb == pl.num_programs(2) - 1)
    def _():
        def chunk(c, carry):
            off = pl.multiple_of(c * TOK_BLK, TOK_BLK)
            xc = xs_ref[pl.ds(off, TOK_BLK), :]
            hg = _dot(xc, wg_ref[0])
            hid = (hg * _sigmoid(hg) * _dot(xc, wu_ref[0])).astype(BF16)
            ys_ref[0, 0, pl.ds(off, TOK_BLK), :] = _dot(hid, wd_ref[0]).astype(BF16)
            return carry
        lax.fori_loop(0, capp // TOK_BLK, chunk, 0)


def _moe(s0_flat, pos4, h2, lw, capp):
    b, t, _ = h2.shape
    nblk = t // TOK_BLK
    tok = min(MOE_TOK, t)
    nsub = tok // TOK_BLK
    grid_spec = pltpu.PrefetchScalarGridSpec(
        num_scalar_prefetch=1, grid=(b, N_EXPERTS, t // tok),
        in_specs=[pl.BlockSpec((1, nsub, N_EXPERTS, TOK_BLK), lambda bi, e, tb, s: (bi, tb, 0, 0)),
                  pl.BlockSpec((1, tok, D), lambda bi, e, tb, s: (bi, tb, 0)),
                  pl.BlockSpec((1, D, EXPERT_FF), lambda bi, e, tb, s: (e, 0, 0)),
                  pl.BlockSpec((1, D, EXPERT_FF), lambda bi, e, tb, s: (e, 0, 0)),
                  pl.BlockSpec((1, EXPERT_FF, D), lambda bi, e, tb, s: (e, 0, 0))],
        out_specs=pl.BlockSpec((1, 1, capp, D), lambda bi, e, tb, s: (bi, e, 0, 0)),
        scratch_shapes=[pltpu.VMEM((capp + GATHER_WIN, D), BF16)])
    return pl.pallas_call(
        functools.partial(_moe_kernel, nblk=nblk, nsub=nsub, capp=capp), name="moe", grid_spec=grid_spec,
        out_shape=jax.ShapeDtypeStruct((b, N_EXPERTS, capp, D), BF16),
        compiler_params=_cparams(("arbitrary", "arbitrary", "arbitrary")),
    )(s0_flat, pos4, h2, lw["w_e_gate"], lw["w_e_up"], lw["w_e_down"])


def _slot_window(s, idx, nb):
    blk0 = jnp.minimum(s[idx] // SLOT_BLK, nb - 1)
    return blk0, s[idx + 1] <= (blk0 + 2) * SLOT_BLK


def _combine_kernel(s0_ref, *refs, nblk, nb):
    ys_refs = refs[:3 * N_EXPERTS]
    pos_ref, aff_ref, xm_ref, mod_ref, g_ref, o_ref, acc_ref = refs[3 * N_EXPERTS:]
    bi, tb = pl.program_id(0), pl.program_id(1)
    acc_ref[...] = jnp.zeros(acc_ref.shape, F32)
    for e in range(N_EXPERTS):
        idx = (bi * N_EXPERTS + e) * (nblk + 1) + tb
        blk0, fits = _slot_window(s0_ref, idx, nb)
        nonempty = s0_ref[idx + 1] > s0_ref[idx]

        def scatter(nfetch, e=e, blk0=blk0):
            rel = pos_ref[0][:, e:e + 1] - blk0 * SLOT_BLK
            lane = lax.broadcasted_iota(I32, (TOK_BLK, nfetch * SLOT_BLK), 1)
            ysw = jnp.concatenate([ys_refs[3 * e + k][0, 0] for k in range(nfetch)], axis=0)
            acc_ref[...] += aff_ref[0][:, e:e + 1] * _dot((lane == rel).astype(BF16), ysw)

        pl.when(fits & nonempty)(functools.partial(scatter, 2))
        pl.when(jnp.logical_not(fits))(functools.partial(scatter, 3))
    mod = mod_ref[0]
    o_ref[0] = xm_ref[0] + mod[5:6] * (_rms(acc_ref[...]) * g_ref[...])


def _combine(s0_flat, ys, pos_t, aff, xm, mod, g_post):
    b, t, _ = xm.shape
    nblk = t // TOK_BLK
    nb = ys.shape[2] // SLOT_BLK

    def ys_spec(e, k):
        def ix(bi, tb, s):
            blk0, fits = _slot_window(s, (bi * N_EXPERTS + e) * (nblk + 1) + tb, nb)
            blk = jnp.minimum(blk0 + k, nb - 1)
            return (bi, e, jnp.where(fits, 0, blk) if k == 2 else blk, 0)
        return pl.BlockSpec((1, 1, SLOT_BLK, D), ix)

    tok = lambda w: pl.BlockSpec((1, TOK_BLK, w), lambda bi, tb, s: (bi, tb, 0))
    in_specs = [ys_spec(e, k) for e in range(N_EXPERTS) for k in range(3)]
    in_specs += [tok(N_EXPERTS), tok(N_EXPERTS), tok(D), pl.BlockSpec((1, 8, D), lambda bi, tb, s: (bi, 0, 0)),
                 pl.BlockSpec((1, D), lambda bi, tb, s: (0, 0))]
    grid_spec = pltpu.PrefetchScalarGridSpec(num_scalar_prefetch=1, grid=(b, nblk), in_specs=in_specs, out_specs=tok(D),
                                             scratch_shapes=[pltpu.VMEM((TOK_BLK, D), F32)])
    return pl.pallas_call(
        functools.partial(_combine_kernel, nblk=nblk, nb=nb), name="combine", grid_spec=grid_spec,
        out_shape=jax.ShapeDtypeStruct((b, t, D), F32),
        compiler_params=_cparams(("arbitrary", "arbitrary")),
    )(s0_flat, *([ys] * (3 * N_EXPERTS)), pos_t, aff, xm, mod, g_post)


def _rope_table(t):
    nf = ROPE_DIM // 4
    pos = jnp.arange(t)
    inv = ROPE_BASE ** (-jnp.arange(nf, dtype=F32) / nf)
    ang = jnp.stack([pos // GRID_W, pos % GRID_W], axis=-1).astype(F32)[..., None] * inv
    cos, sin = jnp.cos(ang), jnp.sin(ang)
    c32 = jnp.stack([cos, cos], axis=2).reshape(t, ROPE_DIM)
    s32 = jnp.stack([-sin, sin], axis=2).reshape(t, ROPE_DIM)
    one, zero = jnp.ones((t, 64), F32), jnp.zeros((t, 32), F32)
    ct = jnp.concatenate([one, c32, zero], axis=1)
    st = jnp.concatenate([0.0 * one, s32, zero], axis=1)
    return jnp.concatenate([ct, st, jnp.tile(c32, (1, 4)), jnp.tile(s32, (1, 4))], axis=1)


def _identity_table(t):
    one, zero = jnp.ones((t, 128), F32), jnp.zeros((t, 128), F32)
    ct = jnp.concatenate([jnp.ones((t, 96), F32), jnp.zeros((t, 32), F32)], axis=1)
    return jnp.concatenate([ct, zero, one, zero], axis=1)


def _layer_weights(i, p):
    lw = {}
    row = lambda a: a.reshape(1, -1)
    for name in ("g_mix_pre", "g_mix_post", "g_ffn_pre", "g_ffn_post", "g_q_lat", "g_kv_lat", "g_mlstm_out", "g_gla_out"):
        lw[name] = row(p[name][i])
    lw["w_ext"] = _gather_cols(p["w_in"][i], _WIN_IDX).astype(BF16)
    lw["wgt"] = p["w_in"][i][:, 416 + 1024:416 + 1040].T.astype(BF16)
    gb = jnp.concatenate([p["b_igate"][i].reshape(-1), p["b_fgate"][i].reshape(-1)])
    lw["gate_bias_row"] = jnp.pad(gb, (0, LANE - 16)).reshape(1, LANE)
    lw["gate_bias_col"] = gb.reshape(16, 1)
    wuq = p["w_uq"][i]
    qi = -np.ones((512,), np.int64)
    qsi = -np.ones((512,), np.int64)
    for h in range(N_HEADS):
        qi[128 * h:128 * h + 96] = 96 * h + np.arange(96)
        qsi[128 * h + 64:128 * h + 96] = 96 * h + 64 + (np.arange(32) ^ 8)
    lw["wq"] = _gather_cols(wuq, qi).astype(BF16)
    lw["wqs"] = _gather_cols(wuq, qsi).astype(BF16)
    ki = -np.ones((512,), np.int64)
    for h in range(N_HEADS):
        ki[128 * h:128 * h + 64] = 128 * h + np.arange(64)
    lw["wk"] = _gather_cols(p["w_ukv"][i], ki).astype(BF16)
    lw["wvt"] = _vt_rows(p["w_ukv"][i].reshape(MLA_KV_LORA, N_HEADS, 128)[:, :, 64:].reshape(MLA_KV_LORA, 256)).astype(BF16)
    lw["wdvt"] = _vt_rows(p["w_in"][i][:, 2768:3024]).astype(BF16)
    wa = p["w_alpha2"][i]
    wal = jnp.zeros((LANE, 256), F32).at[0:16, 0:128].set(wa[0]).at[16:32, 128:256].set(wa[1])
    lw["walpha"] = wal.astype(BF16)
    lw["balpha"] = p["b_alpha"][i].reshape(1, 256)
    lw["w_conv"] = p["w_conv"][i]
    lw["b_conv"] = row(p["b_conv"][i])
    lw["dlam"] = p["diff_lambda"][i]
    lw["g_diff"] = jnp.broadcast_to(p["g_diff_out"][i].reshape(N_HEADS, DIFF_DV, 1), (N_HEADS, DIFF_DV, LANE))
    wb = p["w_branch"][i]
    lw["wbr"] = jnp.concatenate([_pad_heads_rows(wb[0], 64), wb[1], wb[2], _pad_heads_rows(wb[3], 64)], axis=0).astype(BF16)
    lw["w_gate"] = p["w_gate"][i].astype(BF16)
    lw["b_gate"] = row(p["b_gate"][i])
    lw["w_out"] = p["w_out"][i].astype(BF16)
    lw["w_router"] = jnp.pad(p["w_router"][i], ((0, 0), (0, LANE - N_EXPERTS))).astype(BF16)
    lw["w_router_t"] = p["w_router"][i].T.astype(BF16)
    lw["w_e_gate"] = p["w_e_gate"][i].astype(BF16)
    lw["w_e_up"] = p["w_e_up"][i].astype(BF16)
    lw["w_e_down"] = p["w_e_down"][i].astype(BF16)
    return lw


def _ffn(xm, h2, aff, aff_t, mod, lw):
    b, t, _ = xm.shape
    nblk = t // TOK_BLK
    cap = EC_CAPACITY * t // N_EXPERTS
    capp = -(-cap // TOK_BLK) * TOK_BLK
    pos4, s04 = _topk(aff_t, cap)
    s0_be = jnp.concatenate([s04[..., 0].transpose(0, 2, 1), jnp.full((b, N_EXPERTS, 1), cap, I32)], axis=-1)
    s0_flat = s0_be.reshape(-1)
    ys = _moe(s0_flat, pos4, h2, lw, capp)
    pos_t = pos4.transpose(0, 1, 3, 2).reshape(b, t, N_EXPERTS)
    return _combine(s0_flat, ys, pos_t, aff, xm, mod, lw["g_ffn_post"])


def _hybrid_layer(i, x_c, x_l, c8, need_ctx, p):
    lw = _layer_weights(i, p)
    b, t, _ = x_l.shape
    tc = x_c.shape[1]
    lam_init = 0.8 - 0.6 * math.exp(-0.3 * i)
    mod8 = _ada(c8, p["w_ada"][i], p["b_ada"][i])
    pad = lambda m: jnp.pad(m.reshape(b, 6, D), ((0, 0), (0, 2), (0, 0)))
    mod_l = pad(mod8[:b])
    mod_c = pad(jnp.broadcast_to(mod8[b:b + 1], (b, 6 * D)))

    pc = _proj(x_c, mod_c, _identity_table(tc), lw)
    pt = _proj(x_l, mod_l, _rope_table(t), lw)

    zc = jnp.zeros((b, 2, 256, 512), F32)
    zm = jnp.full((b, 2, 8, 256), NEG, F32)
    zs = jnp.zeros((b, 2, 128, 256), F32)
    hf_c, hb_c, c_fin, m_fin = _mlstm(pc, lw["w_conv"], lw["b_conv"], zc, zm)
    hf_l, hb_l, _, _ = _mlstm(pt, lw["w_conv"], lw["b_conv"], c_fin, m_fin)
    gf_c, gb_c, s_fin = _gla(pc, zs)
    gf_l, gb_l, _ = _gla(pt, s_fin)

    one_g = jnp.ones((N_HEADS, MLA_V, LANE), F32)
    fl = functools.partial(_flash, tk=512)
    ya_l = fl(pt["mq"], pc["mk"], pc["mvt"], pt["mk"], pt["mvt"], lw["dlam"], one_g, nmap=1, finish=False, post=1.0,
              tq=min(FLASH_ROWS, t))
    yd_l = fl(pt["dq"], pc["dk"], pc["dvt"], pt["dk"], pt["dvt"], lw["dlam"], lw["g_diff"], nmap=2, finish=True,
              post=1.0 - lam_init, tq=min(FLASH_ROWS // 2, t))
    xm, h2, aff, aff_t = _merge(x_l, mod_l, ya_l, (hf_l, hb_l), pt["lo"], (gf_l, gb_l), pt["gr"], yd_l, lw, tm=256)
    x_l = _ffn(xm, h2, aff, aff_t, mod_l, lw)

    if need_ctx:
        ya_c = fl(pc["mq"], pc["mk"], pc["mvt"], None, None, lw["dlam"], one_g, nmap=1, finish=False, post=1.0, tq=tc)
        yd_c = fl(pc["dq"], pc["dk"], pc["dvt"], None, None, lw["dlam"], lw["g_diff"], nmap=2, finish=True,
                  post=1.0 - lam_init, tq=tc)
        xm, h2, aff, aff_t = _merge(x_c, mod_c, ya_c, (hf_c, hb_c), pc["lo"], (gf_c, gb_c), pc["gr"], yd_c, lw, tm=tc)
        x_c = _ffn(xm, h2, aff, aff_t, mod_c, lw)
    return x_c, x_l


def kernel(x, c, ctx, c_ctx, w_ada, b_ada, g_mix_pre, g_mix_post, g_ffn_pre, g_ffn_post, w_in, g_q_lat, w_uq, g_kv_lat, w_ukv, w_conv, b_conv, b_igate, b_fgate, g_mlstm_out, w_alpha2, b_alpha, g_gla_out, diff_lambda, g_diff_out, w_branch, w_gate, b_gate, w_out, w_router, w_e_gate, w_e_up, w_e_down):
    p = dict(w_ada=w_ada, b_ada=b_ada, g_mix_pre=g_mix_pre, g_mix_post=g_mix_post, g_ffn_pre=g_ffn_pre,
             g_ffn_post=g_ffn_post, w_in=w_in, g_q_lat=g_q_lat, w_uq=w_uq, g_kv_lat=g_kv_lat, w_ukv=w_ukv,
             w_conv=w_conv, b_conv=b_conv, b_igate=b_igate, b_fgate=b_fgate, g_mlstm_out=g_mlstm_out,
             w_alpha2=w_alpha2, b_alpha=b_alpha, g_gla_out=g_gla_out, diff_lambda=diff_lambda, g_diff_out=g_diff_out,
             w_branch=w_branch, w_gate=w_gate, b_gate=b_gate, w_out=w_out, w_router=w_router,
             w_e_gate=w_e_gate, w_e_up=w_e_up, w_e_down=w_e_down)
    b = x.shape[0]
    c8 = jnp.concatenate([c, c_ctx[None], jnp.zeros((8 - b - 1, D), F32)], axis=0)
    x_c, x_l = ctx, x
    for i in range(DEPTH):
        x_c, x_l = _hybrid_layer(i, x_c, x_l, c8, i < DEPTH - 1, p)
    return x_l
```

```python
import functools
import math

import numpy as np
import jax
import jax.numpy as jnp
from jax import lax
from jax.experimental import pallas as pl
from jax.experimental.pallas import tpu as pltpu

F32 = jnp.float32
BF16 = jnp.bfloat16
I32 = jnp.int32
HIGHEST = lax.Precision.HIGHEST

D = 1024
DEPTH = 2
GRID_W = 64
N_HEADS = 4
MLA_NOPE, MLA_ROPE, MLA_V = 64, 32, 64
MLA_Q_LORA, MLA_KV_LORA = 256, 128
ML_DH = 64
GLA_DK, GLA_DV, GLA_RANK, GLA_TAU = 32, 64, 16, 16.0
DIFF_DQK, DIFF_DV = 32, 64
ROPE_DIM, ROPE_BASE = 32, 10000.0
N_EXPERTS, EC_CAPACITY, EXPERT_FF = 16, 2, 1408
NEG = -1e30
EPS = 1e-6
LOG2E = 1.4426950408889634

LANE = 128
HEAD_SLAB = 128
TOK_BLK = 256
ML_CHUNK = 128
GLA_CHUNK = 64
GLA_BLOCK = 256
GATHER_WIN = TOK_BLK + 16
GATHER_WIN_SMALL = 64
MOE_TOK = 1024
SLOT_BLK = 128
VMEM_LIMIT = 56 * 1024 * 1024

ZQ, ZKV, ZKRA, ZKRB, ZMLQK, ZMLV, ZMLO, ZGATE, ZGA = 0, 256, 384, 512, 640, 1152, 1408, 1664, 1792
ZGQ, ZGK, ZGV, ZGR, ZDQ, ZDQS, ZDK, ZDKS, NZ = 1920, 2048, 2176, 2432, 2688, 2944, 3200, 3456, 3712
KV_CHUNK = 256
VT_ROWS = 80
FLASH_ROWS = 512
FLASH_KEYS = 512
FLASH_UNROLL = 16


def _swap32(c):
    return (c // 32) * 32 + ((c % 32) ^ 8)


def _win_index():
    idx = -np.ones((NZ,), np.int64)
    idx[ZQ:ZQ + 256] = np.arange(0, 256)
    idx[ZKV:ZKV + 128] = np.arange(256, 384)
    r = np.arange(32)
    idx[ZKRA + 64:ZKRA + 96] = 384 + r
    idx[ZKRB + 64:ZKRB + 96] = 384 + (r ^ 8)
    ml = 416
    idx[ZMLQK:ZMLQK + 512] = ml + np.arange(512)
    idx[ZMLV:ZMLV + 256] = ml + 512 + np.arange(256)
    idx[ZMLO:ZMLO + 256] = ml + 768 + np.arange(256)
    idx[ZGATE:ZGATE + 16] = ml + 1024 + np.arange(16)
    gl = 1456
    idx[ZGQ:ZGQ + 128] = gl + np.arange(128)
    idx[ZGK:ZGK + 128] = gl + 128 + np.arange(128)
    idx[ZGV:ZGV + 256] = gl + 256 + np.arange(256)
    idx[ZGR:ZGR + 256] = gl + 512 + np.arange(256)
    idx[ZGA:ZGA + 32] = gl + 768 + np.arange(32)
    df = 2256
    c = np.arange(256)
    idx[ZDQ:ZDQ + 256] = df + c
    idx[ZDQS:ZDQS + 256] = df + _swap32(c)
    idx[ZDK:ZDK + 256] = df + 256 + c
    idx[ZDKS:ZDKS + 256] = df + 256 + _swap32(c)
    return idx


def _vt_rows(w_cols):
    n = w_cols.shape[0]
    w4 = w_cols.T.reshape(N_HEADS, 64, n)
    return jnp.pad(w4, ((0, 0), (0, VT_ROWS - 64), (0, 0))).reshape(N_HEADS * VT_ROWS, n)


_WIN_IDX = _win_index()


def _gather_cols(w, idx):
    safe = np.maximum(idx, 0)
    return jnp.where(jnp.asarray(idx >= 0)[None, :], w[:, safe], 0.0)


def _pad_heads_rows(w, width):
    n = w.shape[1]
    w4 = w.reshape(N_HEADS, width, n)
    return jnp.pad(w4, ((0, 0), (0, HEAD_SLAB - width), (0, 0))).reshape(N_HEADS * HEAD_SLAB, n)


def _cparams(sem):
    return pltpu.CompilerParams(dimension_semantics=sem, vmem_limit_bytes=VMEM_LIMIT)


def _rms(x):
    return x * lax.rsqrt(jnp.mean(x * x, axis=-1, keepdims=True) + EPS)


def _sigmoid(x):
    return 1.0 / (1.0 + jnp.exp(-x))


def _log_sigmoid(x):
    return jnp.minimum(x, 0.0) - jnp.log1p(jnp.exp(-jnp.abs(x)))


def _dot(a, b, precision=None):
    return jnp.dot(a, b, preferred_element_type=F32, precision=precision)


def _dot_nt(a, b, precision=None):
    return lax.dot_general(a, b, (((1,), (1,)), ((), ())), preferred_element_type=F32, precision=precision)


def _dot_tn(a, b, precision=None):
    return lax.dot_general(a, b, (((0,), (0,)), ((), ())), preferred_element_type=F32, precision=precision)


def _const_spec(shape):
    nd = len(shape)
    return pl.BlockSpec(shape, lambda *_: (0,) * nd)


def _ada_kernel(c_ref, w_ref, b_ref, o_ref):
    cv = c_ref[...]
    s = (cv * _sigmoid(cv)).astype(BF16)
    o_ref[...] = _dot(s, w_ref[...].astype(BF16)) + b_ref[...]


def _ada(c8, w_ada, b_ada):
    n, tn = 6 * D, 1024
    return pl.pallas_call(
        _ada_kernel, name="ada", grid=(n // tn,),
        in_specs=[pl.BlockSpec((8, D), lambda j: (0, 0)), pl.BlockSpec((D, tn), lambda j: (0, j)),
                  pl.BlockSpec((1, tn), lambda j: (0, j))],
        out_specs=pl.BlockSpec((8, tn), lambda j: (0, j)),
        out_shape=jax.ShapeDtypeStruct((8, n), F32), compiler_params=_cparams(("arbitrary",)),
    )(c8, w_ada, b_ada.reshape(1, n))


_PROJ_OUT = (
    ("mq", 512, BF16), ("mk", 512, BF16),
    ("lqk", 512, F32), ("lv", 256, BF16), ("lo", 256, BF16), ("gc", 128, F32),
    ("gq", 128, BF16), ("gk", 128, BF16), ("gv", 256, BF16), ("gr", 256, BF16), ("glg", 256, F32),
    ("dq", 1024, BF16), ("dk", 512, BF16),
)


def _proj_kernel(x_ref, mod_ref, g_ref, w_ref, tab_ref, gq_ref, wq_ref, wqs_ref, gkv_ref, wk_ref, wvt_ref,
                 wgt_ref, gbr_ref, gbc_ref, wal_ref, bal_ref, wdvt_ref,
                 mq_ref, mk_ref, lqk_ref, lv_ref, lo_ref, gc_ref, gq_o, gk_o, gv_o, gr_o, glg_o,
                 dq_ref, dk_ref, grow_ref, mvt_ref, dvt_ref):
    x = x_ref[0]
    tm = x.shape[0]
    mod = mod_ref[0]
    h = _rms(x) * g_ref[...] * (1.0 + mod[1:2]) + mod[0:1]
    hb = h.astype(BF16)
    z = _dot(hb, w_ref[...])
    tab = tab_ref[...]
    ct, st, cd, sd = tab[:, 0:128], tab[:, 128:256], tab[:, 256:384], tab[:, 384:512]
    lane = lax.broadcasted_iota(I32, (tm, LANE), 1)

    qn = (_rms(z[:, ZQ:ZQ + 256]) * gq_ref[...]).astype(BF16)
    qa = _dot(qn, wq_ref[...])
    qb = _dot(qn, wqs_ref[...])
    qscale = (MLA_NOPE + MLA_ROPE) ** -0.5 * LOG2E
    for hh in range(N_HEADS):
        sl = slice(HEAD_SLAB * hh, HEAD_SLAB * (hh + 1))
        mq_ref[0, :, sl] = ((qa[:, sl] * ct + qb[:, sl] * st) * qscale).astype(BF16)
    kvn = (_rms(z[:, ZKV:ZKV + 128]) * gkv_ref[...]).astype(BF16)
    kk = _dot(kvn, wk_ref[...])
    kr = z[:, ZKRA:ZKRA + 128] * ct + z[:, ZKRB:ZKRB + 128] * st
    for hh in range(N_HEADS):
        sl = slice(HEAD_SLAB * hh, HEAD_SLAB * (hh + 1))
        mk_ref[0, :, sl] = (kk[:, sl] + kr).astype(BF16)
    ones_row = lax.broadcasted_iota(I32, (N_HEADS * VT_ROWS, tm), 0) % VT_ROWS == MLA_V
    mvt_ref[0, 0] = jnp.where(ones_row, 1.0, _dot_nt(wvt_ref[...], kvn)).astype(BF16)
    dvt_ref[0, 0] = jnp.where(ones_row, 1.0, _dot_nt(wdvt_ref[...], hb)).astype(BF16)

    lqk_ref[0] = z[:, ZMLQK:ZMLQK + 512]
    lv_ref[0] = z[:, ZMLV:ZMLV + 256].astype(BF16)
    lo_ref[0] = z[:, ZMLO:ZMLO + 256].astype(BF16)
    gcol = z[:, ZGATE:ZGATE + 128] + gbr_ref[...]
    gc_ref[0] = jnp.where(lane < 8, gcol, jnp.where(lane < 16, _log_sigmoid(gcol), 0.0))
    zr = _dot_nt(wgt_ref[...], hb) + gbc_ref[...]
    rowi = lax.broadcasted_iota(I32, zr.shape, 0)
    grow_ref[0] = jnp.where(rowi < 8, zr, _log_sigmoid(zr))

    gq_o[0] = (z[:, ZGQ:ZGQ + 128] * GLA_DK ** -0.5).astype(BF16)
    gk_o[0] = z[:, ZGK:ZGK + 128].astype(BF16)
    gv_o[0] = z[:, ZGV:ZGV + 256].astype(BF16)
    gr_o[0] = z[:, ZGR:ZGR + 256].astype(BF16)
    zg = _dot(z[:, ZGA:ZGA + 128].astype(BF16), wal_ref[...]) + bal_ref[...]
    glg_o[0] = _log_sigmoid(zg) * (1.0 / GLA_TAU)

    dscale = DIFF_DQK ** -0.5 * LOG2E
    for g in range(2):
        gs = slice(128 * g, 128 * (g + 1))
        qg = (z[:, ZDQ:ZDQ + 256][:, gs] * cd + z[:, ZDQS:ZDQS + 256][:, gs] * sd) * dscale
        kg = z[:, ZDK:ZDK + 256][:, gs] * cd + z[:, ZDKS:ZDKS + 256][:, gs] * sd
        for hl in range(2):
            hh = 2 * g + hl
            for m in range(2):
                lo = 64 * hl + 32 * m
                s0 = (2 * hh + m) * HEAD_SLAB
                dq_ref[0, :, s0:s0 + HEAD_SLAB] = jnp.where((lane >= lo) & (lane < lo + 32), qg, 0.0).astype(BF16)
            dk_ref[0, :, HEAD_SLAB * hh:HEAD_SLAB * (hh + 1)] = jnp.where(
                (lane >= 64 * hl) & (lane < 64 * hl + 64), kg, 0.0).astype(BF16)


def _proj(x, mod, tab, lw):
    b, t, _ = x.shape
    tm = KV_CHUNK
    consts = [lw["g_mix_pre"], lw["w_ext"], None, lw["g_q_lat"], lw["wq"], lw["wqs"], lw["g_kv_lat"], lw["wk"], lw["wvt"],
              lw["wgt"], lw["gate_bias_row"], lw["gate_bias_col"], lw["walpha"], lw["balpha"], lw["wdvt"]]
    in_specs = [pl.BlockSpec((1, tm, D), lambda bi, i: (bi, i, 0)), pl.BlockSpec((1, 8, D), lambda bi, i: (bi, 0, 0))]
    args = [x, mod]
    for cst in consts:
        if cst is None:
            in_specs.append(pl.BlockSpec((tm, 512), lambda bi, i: (i, 0)))
            args.append(tab)
        else:
            in_specs.append(_const_spec(cst.shape))
            args.append(cst)
    out_specs = [pl.BlockSpec((1, tm, w), lambda bi, i: (bi, i, 0)) for _, w, _ in _PROJ_OUT]
    out_shape = [jax.ShapeDtypeStruct((b, t, w), dt) for _, w, dt in _PROJ_OUT]
    out_specs.append(pl.BlockSpec((1, 16, tm), lambda bi, i: (bi, 0, i)))
    out_shape.append(jax.ShapeDtypeStruct((b, 16, t), F32))
    for _ in range(2):
        out_specs.append(pl.BlockSpec((1, 1, N_HEADS * VT_ROWS, tm), lambda bi, i: (bi, i, 0, 0)))
        out_shape.append(jax.ShapeDtypeStruct((b, t // tm, N_HEADS * VT_ROWS, tm), BF16))
    outs = pl.pallas_call(
        _proj_kernel, name="proj", grid=(b, t // tm), in_specs=in_specs, out_specs=out_specs, out_shape=out_shape,
        compiler_params=_cparams(("parallel", "arbitrary")),
    )(*args)
    res = {name: o for (name, _, _), o in zip(_PROJ_OUT, outs[:-3])}
    res["grow"], res["mvt"], res["dvt"] = outs[-3:]
    return res


def _flash_kernel(*refs, nmap, has_lat, tk, finish, post):
    if has_lat:
        q_ref, kc_ref, vc_ref, kl_ref, vl_ref, dl_ref, g_ref, o_ref, s_ref = refs
    else:
        q_ref, kc_ref, vc_ref, dl_ref, g_ref, o_ref = refs
    qb = q_ref[0]
    tq = qb.shape[0]
    qt = jnp.concatenate([qb[:, HEAD_SLAB * mm:HEAD_SLAB * (mm + 1)].astype(F32).T for mm in range(nmap)],
                         axis=1).astype(BF16)
    rows = nmap * tq
    sub = tk // KV_CHUNK

    def softmax(s, smax, m):
        m_new = jnp.maximum(m, smax)
        return m_new, jnp.exp2(m - m_new), jnp.exp2(s - m_new).astype(BF16)

    def pv(p, vts):
        out = _dot(vts[0], p[0:KV_CHUNK])
        for c in range(1, len(vts)):
            out = out + _dot(vts[c], p[c * KV_CHUNK:(c + 1) * KV_CHUNK])
        return out

    s_ctx = _dot(kc_ref[0], qt)
    m, _, p = softmax(s_ctx, jnp.max(s_ctx, axis=0, keepdims=True), jnp.full((1, rows), NEG, F32))
    acc = pv(p, [vc_ref[0, 0]])
    if has_lat:
        n = kl_ref.shape[1] // tk

        unroll = min(FLASH_UNROLL, n)

        def scores(j):
            if isinstance(j, int):
                return _dot(kl_ref[0, j * tk:(j + 1) * tk, :], qt)
            off = pl.multiple_of(j * tk, tk)
            return _dot(kl_ref[0, pl.ds(off, tk), :], qt)

        def values(j):
            return [vl_ref[0, j * sub + c] for c in range(sub)]

        def produce(slot, j):
            s = scores(j)
            s_ref[slot] = s
            return jnp.max(s, axis=0, keepdims=True)

        smax0 = produce(0, 0)

        def body(jj, carry):
            m, acc, smax = carry
            j = unroll * jj
            for u in range(unroll):
                smax_next = smax
                if not isinstance(j, int):
                    smax_next = produce((u + 1) % 2, jnp.minimum(j + u + 1, n - 1))
                elif j + u + 1 < n:
                    smax_next = produce((u + 1) % 2, j + u + 1)
                m, alpha, p = softmax(s_ref[u % 2], smax, m)
                acc = alpha * acc + pv(p, values(j + u))
                smax = smax_next
            return m, acc, smax

        if unroll == n:
            m, acc, _ = body(0, (m, acc, smax0))
        else:
            m, acc, _ = lax.fori_loop(0, n // unroll, body, (m, acc, smax0))

    o = acc[0:MLA_V, :] / acc[MLA_V:MLA_V + 1, :]
    if nmap == 2:
        lv = dl_ref[...]
        lam = (jnp.exp(jnp.sum(lv[0:1] * lv[1:2], axis=-1, keepdims=True))
               - jnp.exp(jnp.sum(lv[2:3] * lv[3:4], axis=-1, keepdims=True)) + (1.0 - post))
        o = o[:, :tq] - lam * o[:, tq:]
    if finish:
        ms = jnp.mean(o * o, axis=0, keepdims=True)
        o = o * lax.rsqrt(ms + EPS) * jnp.concatenate([g_ref[0]] * (tq // LANE), axis=1) * post
    o_pad = jnp.concatenate([o, jnp.zeros((HEAD_SLAB - MLA_V, tq), F32)], axis=0)
    o_ref[0] = o_pad.T.astype(BF16)


def _flash(q, kc, vct, kl, vlt, dlam, g_out, *, nmap, finish, post, tq, tk):
    b, t, _ = q.shape
    has_lat = kl is not None
    assert kc.shape[1] == KV_CHUNK and tk % KV_CHUNK == 0
    qw = nmap * HEAD_SLAB
    kspec = lambda n: pl.BlockSpec((1, n, HEAD_SLAB), lambda bi, h, i: (bi, 0, h))
    vspec = lambda n: pl.BlockSpec((1, n // KV_CHUNK, VT_ROWS, KV_CHUNK), lambda bi, h, i: (bi, 0, h, 0))
    in_specs = [pl.BlockSpec((1, tq, qw), lambda bi, h, i: (bi, i, h)), kspec(KV_CHUNK), vspec(KV_CHUNK)]
    args = [q, kc, vct]
    scratch = []
    if has_lat:
        tl = kl.shape[1]
        assert (tl // tk) % min(FLASH_UNROLL, tl // tk) == 0
        in_specs += [kspec(tl), vspec(tl)]
        args += [kl, vlt]
        scratch = [pltpu.VMEM((2, tk, nmap * tq), F32)]
    in_specs += [_const_spec(dlam.shape), pl.BlockSpec((1, MLA_V, LANE), lambda bi, h, i: (h, 0, 0))]
    args += [dlam, g_out]
    return pl.pallas_call(
        functools.partial(_flash_kernel, nmap=nmap, has_lat=has_lat, tk=tk, finish=finish, post=post),
        name="flash_diff" if nmap == 2 else "flash_mla",
        grid=(b, N_HEADS, t // tq), in_specs=in_specs,
        out_specs=pl.BlockSpec((1, tq, HEAD_SLAB), lambda bi, h, i: (bi, i, h)),
        out_shape=jax.ShapeDtypeStruct((b, t, N_HEADS * HEAD_SLAB), BF16),
        scratch_shapes=scratch,
        compiler_params=_cparams(("parallel", "parallel", "arbitrary")),
    )(*args)


def _head_of(shape, axis, width):
    return (lax.broadcasted_iota(I32, shape, axis) % (N_HEADS * width)) // width


def _mlstm_dir(d, first, last, x, xprev, xnext, v, gcol, grow, wc, bcv, c_ref, m_ref):
    L = x.shape[0]
    row = lax.broadcasted_iota(I32, x.shape, 0)
    pr = jnp.where(first, 0.0, xprev[7:8, :])
    nx = jnp.where(last, 0.0, xnext[0:1, :])
    xm = jnp.where(row == 0, pr, pltpu.roll(x, 1, 0))
    xp = jnp.where(row == L - 1, nx, pltpu.roll(x, L - 1, 0))
    y = xm * wc[0:1] + x * wc[1:2] + xp * wc[2:3] + bcv
    qk = y * _sigmoid(y)
    q = qk[:, :256]
    k = qk[:, 256:] * ML_DH ** -0.5

    li = lax.broadcasted_iota(I32, (L, L), 0)
    si = lax.broadcasted_iota(I32, (L, L), 1)
    tin = (si <= li) if d == 0 else (si >= li)
    tinf = tin.astype(F32)
    bcol = _dot(tinf, gcol, HIGHEST)
    brow = _dot_nt(grow, tinf, HIGHEST)
    m0e = m_ref[0, d]
    cb = c_ref[0, d]
    hm256 = _head_of((L, 256), 1, ML_DH)
    hm512 = _head_of((L, 512), 1, ML_DH)
    e_idx = L - 1 if d == 0 else 0

    d_blk, inter_blk = [], []
    for hh in range(N_HEADS):
        c = 4 * d + hh
        bc = bcol[:, 8 + c:9 + c]
        d_blk.append(jnp.where(tin, bc - brow[8 + c:9 + c, :] + grow[c:c + 1, :], NEG))
        inter_blk.append(bc + m0e[0:1, 64 * hh:64 * hh + 1])
    d_st = jnp.concatenate(d_blk, axis=0)
    inter_st = jnp.concatenate(inter_blk, axis=0)
    mt = jnp.maximum(inter_st, jnp.max(d_st, axis=-1, keepdims=True))
    q_st = jnp.concatenate([jnp.where(hm256 == hh, q, 0.0) for hh in range(N_HEADS)], axis=0).astype(BF16)
    s_st = (jnp.exp(d_st - mt) * _dot_nt(q_st, k.astype(BF16))).astype(BF16)
    vext = jnp.concatenate([v, jnp.ones((L, 256), BF16)], axis=1)
    r = _dot(s_st, vext)
    aint = jnp.exp(inter_st - mt)
    p = _dot(q.astype(BF16), cb.astype(BF16))
    tot = jnp.zeros((L, 512), F32)
    mte = jnp.zeros((L, 256), F32)
    for hh in range(N_HEADS):
        rs = slice(hh * L, (hh + 1) * L)
        tot = jnp.where(hm512 == hh, r[rs] + aint[rs] * p, tot)
        mte = jnp.where(hm256 == hh, mt[rs], mte)
    hout = tot[:, :256] / jnp.maximum(jnp.abs(tot[:, 256:]), jnp.exp(-mte))

    wexp = jnp.zeros((L, 256), F32)
    arow = jnp.zeros((1, 512), F32)
    grw = jnp.zeros((1, 512), F32)
    mnew = jnp.zeros((1, 256), F32)
    hr512 = _head_of((1, 512), 1, ML_DH)
    hr256 = _head_of((1, 256), 1, ML_DH)
    for hh in range(N_HEADS):
        c = 4 * d + hh
        bc = bcol[:, 8 + c:9 + c]
        be = bc[e_idx:e_idx + 1, :]
        wl = be - bc + gcol[:, c:c + 1]
        mloc = jnp.max(wl, axis=0, keepdims=True)
        m0h = m0e[0:1, 64 * hh:64 * hh + 1]
        mn = jnp.maximum(be + m0h, mloc)
        wexp = jnp.where(hm256 == hh, jnp.exp(wl - mloc), wexp)
        arow = jnp.where(hr512 == hh, jnp.exp(be + m0h - mn), arow)
        grw = jnp.where(hr512 == hh, jnp.exp(mloc - mn), grw)
        mnew = jnp.where(hr256 == hh, mn, mnew)
    cl = _dot_tn((k * wexp).astype(BF16), vext)
    bd = lax.broadcasted_iota(I32, (256, 512), 0) // ML_DH == _head_of((256, 512), 1, ML_DH)
    c_ref[0, d] = arow * cb + jnp.where(bd, grw * cl, 0.0)
    m_ref[0, d] = jnp.broadcast_to(mnew, (8, 256))
    return hout


def _mlstm_kernel(xf, xfp, xfn, xb, xbp, xbn, vf, vb, gcf, gcb, grf, grb, wc_ref, bc_ref, c0_ref, m0_ref,
                  hf_ref, hb_ref, c_ref, m_ref):
    i = pl.program_id(1)
    n = pl.num_programs(1)

    @pl.when(i == 0)
    def _():
        c_ref[...] = c0_ref[...]
        m_ref[...] = m0_ref[...]

    wc = wc_ref[...]
    bcv = bc_ref[...]
    hf_ref[0] = _mlstm_dir(0, i == 0, i == n - 1, xf[0], xfp[0], xfn[0], vf[0], gcf[0], grf[0], wc, bcv, c_ref, m_ref)
    hb_ref[0] = _mlstm_dir(1, i == n - 1, i == 0, xb[0], xbp[0], xbn[0], vb[0], gcb[0], grb[0], wc, bcv, c_ref, m_ref)


def _mlstm(pr, w_conv, b_conv, c0, m0):
    x, v, gc, gr = pr["lqk"], pr["lv"], pr["gc"], pr["grow"]
    b, t, _ = x.shape
    L = ML_CHUNK
    n = t // L
    r8 = L // 8
    last8 = t // 8 - 1

    def fw(bi, i):
        return (bi, i, 0)

    def bw(bi, i):
        return (bi, n - 1 - i, 0)

    def halo(ix, shift):
        def f(bi, i):
            blk = ix(bi, i)[1]
            return (bi, jnp.clip(blk * r8 + shift, 0, last8), 0)
        return f

    main = lambda w, ix: pl.BlockSpec((1, L, w), ix)
    in_specs = [main(512, fw), pl.BlockSpec((1, 8, 512), halo(fw, -1)), pl.BlockSpec((1, 8, 512), halo(fw, r8)),
                main(512, bw), pl.BlockSpec((1, 8, 512), halo(bw, -1)), pl.BlockSpec((1, 8, 512), halo(bw, r8)),
                main(256, fw), main(256, bw), main(128, fw), main(128, bw),
                pl.BlockSpec((1, 16, L), lambda bi, i: (bi, 0, i)), pl.BlockSpec((1, 16, L), lambda bi, i: (bi, 0, n - 1 - i)),
                _const_spec(w_conv.shape), _const_spec(b_conv.shape),
                pl.BlockSpec((1, 2, 256, 512), lambda bi, i: (bi, 0, 0, 0)), pl.BlockSpec((1, 2, 8, 256), lambda bi, i: (bi, 0, 0, 0))]
    out_specs = [main(256, fw), main(256, bw),
                 pl.BlockSpec((1, 2, 256, 512), lambda bi, i: (bi, 0, 0, 0)), pl.BlockSpec((1, 2, 8, 256), lambda bi, i: (bi, 0, 0, 0))]
    out_shape = [jax.ShapeDtypeStruct((b, t, 256), F32), jax.ShapeDtypeStruct((b, t, 256), F32),
                 jax.ShapeDtypeStruct(c0.shape, F32), jax.ShapeDtypeStruct(m0.shape, F32)]
    return pl.pallas_call(
        _mlstm_kernel, name="mlstm", grid=(b, n), in_specs=in_specs, out_specs=out_specs, out_shape=out_shape,
        compiler_params=_cparams(("parallel", "arbitrary")),
    )(x, x, x, x, x, x, v, v, gc, gc, gr, gr, w_conv, b_conv, c0, m0)


def _gla_chunk(d, q, k, v, lg, sb):
    L = q.shape[0]
    li = lax.broadcasted_iota(I32, (L, L), 0)
    si = lax.broadcasted_iota(I32, (L, L), 1)
    tin = (si <= li) if d == 0 else (si >= li)
    tinf = tin.astype(F32)
    lgd = lg[:, 128 * d:128 * (d + 1)]
    gcum = _dot(tinf, lgd, HIGHEST)
    e_idx = L - 1 if d == 0 else 0
    gend = gcum[e_idx:e_idx + 1, :]
    qf, kf = q.astype(F32), k.astype(F32)
    q_dec = qf * jnp.exp(gcum)
    k_dec = (kf * jnp.exp(-gcum)).astype(BF16)
    k_end = (kf * jnp.exp(gend - gcum)).astype(BF16)
    hm128 = _head_of((L, 128), 1, GLA_DK)
    hm256 = _head_of((L, 256), 1, GLA_DV)
    q_st = jnp.concatenate([jnp.where(hm128 == hh, q_dec, 0.0) for hh in range(N_HEADS)], axis=0).astype(BF16)
    att = _dot_nt(q_st, k_dec)
    tin4 = jnp.concatenate([tin] * N_HEADS, axis=0)
    o_st = _dot(jnp.where(tin4, att, 0.0).astype(BF16), v)
    o = _dot(q_dec.astype(BF16), sb.astype(BF16))
    for hh in range(N_HEADS):
        o = o + jnp.where(hm256 == hh, o_st[hh * L:(hh + 1) * L], 0.0)
    gend_col = _dot_tn(lgd, jnp.ones((L, 256), F32), HIGHEST)
    bd = lax.broadcasted_iota(I32, (128, 256), 0) // GLA_DK == _head_of((128, 256), 1, GLA_DV)
    return o, jnp.exp(gend_col) * sb + jnp.where(bd, _dot_tn(k_end, v), 0.0)


def _gla_kernel(qf, kf, vf, lf, qb, kb, vb, lb, s0_ref, of_ref, ob_ref, s_ref):
    i = pl.program_id(1)

    @pl.when(i == 0)
    def _():
        s_ref[...] = s0_ref[...]

    L = GLA_CHUNK
    nsub = qf.shape[1] // L
    for d, (q, k, v, lg, o_ref) in enumerate(((qf, kf, vf, lf, of_ref), (qb, kb, vb, lb, ob_ref))):
        sb = s_ref[0, d]
        for c in (range(nsub) if d == 0 else reversed(range(nsub))):
            sl = slice(c * L, (c + 1) * L)
            o_ref[0, sl, :], sb = _gla_chunk(d, q[0, sl, :], k[0, sl, :], v[0, sl, :], lg[0, sl, :], sb)
        s_ref[0, d] = sb


def _gla(pr, s0):
    q, k, v, lg = pr["gq"], pr["gk"], pr["gv"], pr["glg"]
    b, t, _ = q.shape
    L = min(GLA_BLOCK, t)
    n = t // L
    fw = lambda bi, i: (bi, i, 0)
    bw = lambda bi, i: (bi, n - 1 - i, 0)
    blk = lambda w, ix: pl.BlockSpec((1, L, w), ix)
    st_spec = pl.BlockSpec((1, 2, 128, 256), lambda bi, i: (bi, 0, 0, 0))
    return pl.pallas_call(
        _gla_kernel, name="gla", grid=(b, n),
        in_specs=[blk(128, fw), blk(128, fw), blk(256, fw), blk(256, fw),
                  blk(128, bw), blk(128, bw), blk(256, bw), blk(256, bw), st_spec],
        out_specs=[blk(256, fw), blk(256, bw), st_spec],
        out_shape=[jax.ShapeDtypeStruct((b, t, 256), F32), jax.ShapeDtypeStruct((b, t, 256), F32),
                   jax.ShapeDtypeStruct(s0.shape, F32)],
        compiler_params=_cparams(("parallel", "arbitrary")),
    )(q, k, v, lg, q, k, v, lg, s0)


def _head_rms_expanded(x, width):
    n = x.shape[1]
    bd = (lax.broadcasted_iota(I32, (n, n), 0) // width == lax.broadcasted_iota(I32, (n, n), 1) // width).astype(F32)
    return _dot(x * x, bd, HIGHEST) * (1.0 / width)


def _merge_kernel(x_ref, mod_ref, ya_ref, hf_ref, hb_ref, lo_ref, gf_ref, gb_ref, gr_ref, yd_ref,
                  gpre_ref, wg_ref, bg_ref, wbr_ref, wo_ref, gpost_ref, gffn_ref, wr_ref, wrt_ref, gml_ref, ggla_ref,
                  xm_ref, h2_ref, aff_ref, afft_ref):
    x = x_ref[0]
    tm = x.shape[0]
    mod = mod_ref[0]
    hb = (_rms(x) * gpre_ref[...] * (1.0 + mod[1:2]) + mod[0:1]).astype(BF16)

    hs = hf_ref[0] + hb_ref[0]
    y_ml = _sigmoid(lo_ref[0].astype(F32)) * (hs * lax.rsqrt(_head_rms_expanded(hs, ML_DH) + EPS) * gml_ref[...])
    gs = gf_ref[0] + gb_ref[0]
    rr = gr_ref[0].astype(F32)
    y_gla = rr * _sigmoid(rr) * (gs * lax.rsqrt(_head_rms_expanded(gs, GLA_DV) + EPS) * ggla_ref[...])

    branches = ((ya_ref[0], 0, 512), (y_ml.astype(BF16), 512, 256), (y_gla.astype(BF16), 768, 256), (yd_ref[0], 1024, 512))
    mix = jnp.zeros((tm, D), F32)
    for nb, (yb, r0, rw) in enumerate(branches):
        gate = _sigmoid(_dot(hb, wg_ref[:, nb * D:(nb + 1) * D]) + bg_ref[:, nb * D:(nb + 1) * D])
        mix = mix + gate * _dot(yb, wbr_ref[r0:r0 + rw, :])
    y = _dot(mix.astype(BF16), wo_ref[...])
    xm = x + mod[2:3] * (_rms(y) * gpost_ref[...])
    xm_ref[0] = xm

    h2 = (_rms(xm) * gffn_ref[...] * (1.0 + mod[4:5]) + mod[3:4]).astype(BF16)
    h2_ref[0] = h2
    lane = lax.broadcasted_iota(I32, (tm, LANE), 1)
    lg = jnp.where(lane < N_EXPERTS, _dot(h2, wr_ref[...]), NEG)
    e = jnp.exp(lg - jnp.max(lg, axis=-1, keepdims=True))
    aff_ref[0] = (e / jnp.sum(e, axis=-1, keepdims=True))[:, :N_EXPERTS]
    lt = _dot_nt(wrt_ref[...], h2)
    et = jnp.exp(lt - jnp.max(lt, axis=0, keepdims=True))
    afft_ref[0] = et / jnp.sum(et, axis=0, keepdims=True)


def _merge(x, mod, ya, ml, lo, gl, gr, yd, lw, tm):
    b, t, _ = x.shape
    tok = lambda w: pl.BlockSpec((1, tm, w), lambda bi, i: (bi, i, 0))
    consts = [lw["g_mix_pre"], lw["w_gate"], lw["b_gate"], lw["wbr"], lw["w_out"], lw["g_mix_post"], lw["g_ffn_pre"],
              lw["w_router"], lw["w_router_t"], lw["g_mlstm_out"], lw["g_gla_out"]]
    in_specs = [tok(D), pl.BlockSpec((1, 8, D), lambda bi, i: (bi, 0, 0)), tok(512), tok(256), tok(256), tok(256),
                tok(256), tok(256), tok(256), tok(512)] + [_const_spec(c.shape) for c in consts]
    out_specs = [tok(D), tok(D), tok(N_EXPERTS), pl.BlockSpec((1, N_EXPERTS, tm), lambda bi, i: (bi, 0, i))]
    out_shape = [jax.ShapeDtypeStruct((b, t, D), F32), jax.ShapeDtypeStruct((b, t, D), BF16),
                 jax.ShapeDtypeStruct((b, t, N_EXPERTS), F32), jax.ShapeDtypeStruct((b, N_EXPERTS, t), F32)]
    return pl.pallas_call(
        _merge_kernel, name="merge", grid=(b, t // tm), in_specs=in_specs, out_specs=out_specs, out_shape=out_shape,
        compiler_params=_cparams(("parallel", "arbitrary")),
    )(x, mod, ya, ml[0], ml[1], lo, gl[0], gl[1], gr, yd, *consts)


def _topk_kernel(a_ref, pos_ref, s0_ref, *, cap):
    nblk = a_ref.shape[1]
    bits = pltpu.bitcast(a_ref[0], I32)

    def bisect(i, thr):
        cand = thr | (1 << (30 - i))
        cnt = jnp.sum((bits >= cand).astype(I32), axis=(0, 2), keepdims=True)
        return jnp.where(cnt >= cap, cand, thr)

    thr3 = lax.fori_loop(0, 31, bisect, jnp.zeros((1, N_EXPERTS, 1), I32))
    need3 = cap - jnp.sum((bits > thr3).astype(I32), axis=(0, 2), keepdims=True)
    thr, need = thr3[0], need3[0].astype(F32)
    upper = (lax.broadcasted_iota(I32, (TOK_BLK, TOK_BLK), 0) <= lax.broadcasted_iota(I32, (TOK_BLK, TOK_BLK), 1)).astype(BF16)

    def blk(j, carry):
        c_eq, c_sel = carry
        bj = pltpu.bitcast(a_ref[0, j], I32)
        gt, eq = bj > thr, bj == thr
        cum_eq = _dot(eq.astype(BF16), upper) + c_eq
        sel = gt | (eq & (cum_eq <= need))
        cum_sel = _dot(sel.astype(BF16), upper) + c_sel
        pos_ref[0, j] = jnp.where(sel, cum_sel - 1.0, -1.0).astype(I32)
        s0_ref[0, j] = jnp.broadcast_to(c_sel, (N_EXPERTS, LANE)).astype(I32)
        return cum_eq[:, TOK_BLK - 1:TOK_BLK], cum_sel[:, TOK_BLK - 1:TOK_BLK]

    zero = jnp.zeros((N_EXPERTS, 1), F32)
    lax.fori_loop(0, nblk, blk, (zero, zero))


def _topk(aff_t, cap):
    b, _, t = aff_t.shape
    nblk = t // TOK_BLK
    a4 = aff_t.reshape(b, N_EXPERTS, nblk, TOK_BLK).transpose(0, 2, 1, 3)
    spec = lambda w: pl.BlockSpec((1, nblk, N_EXPERTS, w), lambda bi: (bi, 0, 0, 0))
    return pl.pallas_call(
        functools.partial(_topk_kernel, cap=cap), name="topk", grid=(b,),
        in_specs=[spec(TOK_BLK)], out_specs=[spec(TOK_BLK), spec(LANE)],
        out_shape=[jax.ShapeDtypeStruct((b, nblk, N_EXPERTS, TOK_BLK), I32),
                   jax.ShapeDtypeStruct((b, nblk, N_EXPERTS, LANE), I32)],
        compiler_params=_cparams(("parallel",)),
    )(a4)


def _moe_kernel(s0_ref, pos_ref, h_ref, wg_ref, wu_ref, wd_ref, ys_ref, xs_ref, *, nblk, nsub, capp):
    bi, e, tb = pl.program_id(0), pl.program_id(1), pl.program_id(2)

    @pl.when(tb == 0)
    def _():
        xs_ref[...] = jnp.zeros(xs_ref.shape, BF16)

    base = (bi * N_EXPERTS + e) * (nblk + 1) + tb * nsub
    for sb in range(nsub):
        s0 = s0_ref[base + sb]
        s1 = s0_ref[base + sb + 1]
        a0 = pl.multiple_of((s0 // 16) * 16, 16)

        def gather(win, sb=sb, a0=a0):
            prow = pos_ref[0, sb, pl.ds(e, 1), :]
            slot = lax.broadcasted_iota(I32, (win, TOK_BLK), 0) + a0
            rows = _dot((slot == prow).astype(BF16), h_ref[0, sb * TOK_BLK:(sb + 1) * TOK_BLK, :])
            xs_ref[pl.ds(a0, win), :] = xs_ref[pl.ds(a0, win), :] + rows.astype(BF16)

        pl.when((s1 > s0) & (s1 - a0 <= GATHER_WIN_SMALL))(functools.partial(gather, GATHER_WIN_SMALL))
        pl.when(s1 - a0 > GATHER_WIN_SMALL)(functools.partial(gather, GATHER_WIN))

    @pl.when(tb == pl.num_programs(2) - 1)
    def _():
        def chunk(c, carry):
            off = pl.multiple_of(c * TOK_BLK, TOK_BLK)
            xc = xs_ref[pl.ds(off, TOK_BLK), :]
            hg = _dot(xc, wg_ref[0])
            hid = (hg * _sigmoid(hg) * _dot(xc, wu_ref[0])).astype(BF16)
            ys_ref[0, 0, pl.ds(off, TOK_BLK), :] = _dot(hid, wd_ref[0]).astype(BF16)
            return carry
        lax.fori_loop(0, capp // TOK_BLK, chunk, 0)


def _moe(s0_flat, pos4, h2, lw, capp):
    b, t, _ = h2.shape
    nblk = t // TOK_BLK
    tok = min(MOE_TOK, t)
    nsub = tok // TOK_BLK
    grid_spec = pltpu.PrefetchScalarGridSpec(
        num_scalar_prefetch=1, grid=(b, N_EXPERTS, t // tok),
        in_specs=[pl.BlockSpec((1, nsub, N_EXPERTS, TOK_BLK), lambda bi, e, tb, s: (bi, tb, 0, 0)),
                  pl.BlockSpec((1, tok, D), lambda bi, e, tb, s: (bi, tb, 0)),
                  pl.BlockSpec((1, D, EXPERT_FF), lambda bi, e, tb, s: (e, 0, 0)),
                  pl.BlockSpec((1, D, EXPERT_FF), lambda bi, e, tb, s: (e, 0, 0)),
                  pl.BlockSpec((1, EXPERT_FF, D), lambda bi, e, tb, s: (e, 0, 0))],
        out_specs=pl.BlockSpec((1, 1, capp, D), lambda bi, e, tb, s: (bi, e, 0, 0)),
        scratch_shapes=[pltpu.VMEM((capp + GATHER_WIN, D), BF16)])
    return pl.pallas_call(
        functools.partial(_moe_kernel, nblk=nblk, nsub=nsub, capp=capp), name="moe", grid_spec=grid_spec,
        out_shape=jax.ShapeDtypeStruct((b, N_EXPERTS, capp, D), BF16),
        compiler_params=_cparams(("arbitrary", "arbitrary", "arbitrary")),
    )(s0_flat, pos4, h2, lw["w_e_gate"], lw["w_e_up"], lw["w_e_down"])


def _combine_kernel(s0_ref, *refs, nblk, nb, sblk):
    ys_refs = refs[:2 * N_EXPERTS]
    pos_ref, aff_ref, xm_ref, mod_ref, g_ref, o_ref = refs[2 * N_EXPERTS:]
    bi, tb = pl.program_id(0), pl.program_id(1)
    pos = pos_ref[0]
    aff = aff_ref[0]
    lane = lax.broadcasted_iota(I32, (TOK_BLK, 2 * sblk), 1)
    acc = jnp.zeros((TOK_BLK, D), F32)
    for e in range(N_EXPERTS):
        s0 = s0_ref[(bi * N_EXPERTS + e) * (nblk + 1) + tb]
        blk0 = jnp.minimum(s0 // sblk, nb - 1)
        rel = pos[:, e:e + 1] - blk0 * sblk
        ysw = jnp.concatenate([ys_refs[2 * e][0, 0], ys_refs[2 * e + 1][0, 0]], axis=0)
        acc = acc + aff[:, e:e + 1] * _dot((lane == rel).astype(BF16), ysw)
    mod = mod_ref[0]
    o_ref[0] = xm_ref[0] + mod[5:6] * (_rms(acc) * g_ref[...])


def _combine(s0_flat, ys, pos_t, aff, xm, mod, g_post, sblk):
    b, t, _ = xm.shape
    nblk = t // TOK_BLK
    nb = ys.shape[2] // sblk

    def ys_spec(e, k):
        def ix(bi, tb, s):
            blk0 = jnp.minimum(s[(bi * N_EXPERTS + e) * (nblk + 1) + tb] // sblk, nb - 1)
            return (bi, e, jnp.minimum(blk0 + k, nb - 1), 0)
        return pl.BlockSpec((1, 1, sblk, D), ix)

    tok = lambda w: pl.BlockSpec((1, TOK_BLK, w), lambda bi, tb, s: (bi, tb, 0))
    in_specs = [ys_spec(e, k) for e in range(N_EXPERTS) for k in range(2)]
    in_specs += [tok(N_EXPERTS), tok(N_EXPERTS), tok(D), pl.BlockSpec((1, 8, D), lambda bi, tb, s: (bi, 0, 0)),
                 pl.BlockSpec((1, D), lambda bi, tb, s: (0, 0))]
    grid_spec = pltpu.PrefetchScalarGridSpec(num_scalar_prefetch=1, grid=(b, nblk), in_specs=in_specs, out_specs=tok(D))
    return pl.pallas_call(
        functools.partial(_combine_kernel, nblk=nblk, nb=nb, sblk=sblk), name="combine", grid_spec=grid_spec,
        out_shape=jax.ShapeDtypeStruct((b, t, D), F32),
        compiler_params=_cparams(("arbitrary", "arbitrary")),
    )(s0_flat, *([ys] * (2 * N_EXPERTS)), pos_t, aff, xm, mod, g_post)


def _rope_table(t):
    nf = ROPE_DIM // 4
    pos = jnp.arange(t)
    inv = ROPE_BASE ** (-jnp.arange(nf, dtype=F32) / nf)
    ang = jnp.stack([pos // GRID_W, pos % GRID_W], axis=-1).astype(F32)[..., None] * inv
    cos, sin = jnp.cos(ang), jnp.sin(ang)
    c32 = jnp.stack([cos, cos], axis=2).reshape(t, ROPE_DIM)
    s32 = jnp.stack([-sin, sin], axis=2).reshape(t, ROPE_DIM)
    one, zero = jnp.ones((t, 64), F32), jnp.zeros((t, 32), F32)
    ct = jnp.concatenate([one, c32, zero], axis=1)
    st = jnp.concatenate([0.0 * one, s32, zero], axis=1)
    return jnp.concatenate([ct, st, jnp.tile(c32, (1, 4)), jnp.tile(s32, (1, 4))], axis=1)


def _identity_table(t):
    one, zero = jnp.ones((t, 128), F32), jnp.zeros((t, 128), F32)
    ct = jnp.concatenate([jnp.ones((t, 96), F32), jnp.zeros((t, 32), F32)], axis=1)
    return jnp.concatenate([ct, zero, one, zero], axis=1)


def _layer_weights(i, p):
    lw = {}
    row = lambda a: a.reshape(1, -1)
    for name in ("g_mix_pre", "g_mix_post", "g_ffn_pre", "g_ffn_post", "g_q_lat", "g_kv_lat", "g_mlstm_out", "g_gla_out"):
        lw[name] = row(p[name][i])
    lw["w_ext"] = _gather_cols(p["w_in"][i], _WIN_IDX).astype(BF16)
    lw["wgt"] = p["w_in"][i][:, 416 + 1024:416 + 1040].T.astype(BF16)
    gb = jnp.concatenate([p["b_igate"][i].reshape(-1), p["b_fgate"][i].reshape(-1)])
    lw["gate_bias_row"] = jnp.pad(gb, (0, LANE - 16)).reshape(1, LANE)
    lw["gate_bias_col"] = gb.reshape(16, 1)
    wuq = p["w_uq"][i]
    qi = -np.ones((512,), np.int64)
    qsi = -np.ones((512,), np.int64)
    for h in range(N_HEADS):
        qi[128 * h:128 * h + 96] = 96 * h + np.arange(96)
        qsi[128 * h + 64:128 * h + 96] = 96 * h + 64 + (np.arange(32) ^ 8)
    lw["wq"] = _gather_cols(wuq, qi).astype(BF16)
    lw["wqs"] = _gather_cols(wuq, qsi).astype(BF16)
    ki = -np.ones((512,), np.int64)
    for h in range(N_HEADS):
        ki[128 * h:128 * h + 64] = 128 * h + np.arange(64)
    lw["wk"] = _gather_cols(p["w_ukv"][i], ki).astype(BF16)
    lw["wvt"] = _vt_rows(p["w_ukv"][i].reshape(MLA_KV_LORA, N_HEADS, 128)[:, :, 64:].reshape(MLA_KV_LORA, 256)).astype(BF16)
    lw["wdvt"] = _vt_rows(p["w_in"][i][:, 2768:3024]).astype(BF16)
    wa = p["w_alpha2"][i]
    wal = jnp.zeros((LANE, 256), F32).at[0:16, 0:128].set(wa[0]).at[16:32, 128:256].set(wa[1])
    lw["walpha"] = wal.astype(BF16)
    lw["balpha"] = p["b_alpha"][i].reshape(1, 256)
    lw["w_conv"] = p["w_conv"][i]
    lw["b_conv"] = row(p["b_conv"][i])
    lw["dlam"] = p["diff_lambda"][i]
    lw["g_diff"] = jnp.broadcast_to(p["g_diff_out"][i].reshape(N_HEADS, DIFF_DV, 1), (N_HEADS, DIFF_DV, LANE))
    wb = p["w_branch"][i]
    lw["wbr"] = jnp.concatenate([_pad_heads_rows(wb[0], 64), wb[1], wb[2], _pad_heads_rows(wb[3], 64)], axis=0).astype(BF16)
    lw["w_gate"] = p["w_gate"][i].astype(BF16)
    lw["b_gate"] = row(p["b_gate"][i])
    lw["w_out"] = p["w_out"][i].astype(BF16)
    lw["w_router"] = jnp.pad(p["w_router"][i], ((0, 0), (0, LANE - N_EXPERTS))).astype(BF16)
    lw["w_router_t"] = p["w_router"][i].T.astype(BF16)
    lw["w_e_gate"] = p["w_e_gate"][i].astype(BF16)
    lw["w_e_up"] = p["w_e_up"][i].astype(BF16)
    lw["w_e_down"] = p["w_e_down"][i].astype(BF16)
    return lw


def _ffn(xm, h2, aff, aff_t, mod, lw):
    b, t, _ = xm.shape
    nblk = t // TOK_BLK
    cap = EC_CAPACITY * t // N_EXPERTS
    capp = -(-cap // TOK_BLK) * TOK_BLK
    pos4, s04 = _topk(aff_t, cap)
    s0_be = jnp.concatenate([s04[..., 0].transpose(0, 2, 1), jnp.full((b, N_EXPERTS, 1), cap, I32)], axis=-1)
    s0_flat = s0_be.reshape(-1)
    ys = _moe(s0_flat, pos4, h2, lw, capp)
    pos_t = pos4.transpose(0, 1, 3, 2).reshape(b, t, N_EXPERTS)
    first = jnp.minimum(s0_be[..., :-1] // SLOT_BLK, capp // SLOT_BLK - 1)
    fits = jnp.all(s0_be[..., 1:] <= (first + 2) * SLOT_BLK)
    args = (s0_flat, ys, pos_t, aff, xm, mod, lw["g_ffn_post"])
    return lax.cond(fits, functools.partial(_combine, sblk=SLOT_BLK), functools.partial(_combine, sblk=TOK_BLK), *args)


def _hybrid_layer(i, x_c, x_l, c8, need_ctx, p):
    lw = _layer_weights(i, p)
    b, t, _ = x_l.shape
    tc = x_c.shape[1]
    lam_init = 0.8 - 0.6 * math.exp(-0.3 * i)
    mod8 = _ada(c8, p["w_ada"][i], p["b_ada"][i])
    pad = lambda m: jnp.pad(m.reshape(b, 6, D), ((0, 0), (0, 2), (0, 0)))
    mod_l = pad(mod8[:b])
    mod_c = pad(jnp.broadcast_to(mod8[b:b + 1], (b, 6 * D)))

    pc = _proj(x_c, mod_c, _identity_table(tc), lw)
    pt = _proj(x_l, mod_l, _rope_table(t), lw)

    zc = jnp.zeros((b, 2, 256, 512), F32)
    zm = jnp.full((b, 2, 8, 256), NEG, F32)
    zs = jnp.zeros((b, 2, 128, 256), F32)
    hf_c, hb_c, c_fin, m_fin = _mlstm(pc, lw["w_conv"], lw["b_conv"], zc, zm)
    hf_l, hb_l, _, _ = _mlstm(pt, lw["w_conv"], lw["b_conv"], c_fin, m_fin)
    gf_c, gb_c, s_fin = _gla(pc, zs)
    gf_l, gb_l, _ = _gla(pt, s_fin)

    one_g = jnp.ones((N_HEADS, MLA_V, LANE), F32)
    fl = functools.partial(_flash, tk=FLASH_KEYS)
    ya_l = fl(pt["mq"], pc["mk"], pc["mvt"], pt["mk"], pt["mvt"], lw["dlam"], one_g, nmap=1, finish=False, post=1.0,
              tq=min(FLASH_ROWS, t))
    yd_l = fl(pt["dq"], pc["dk"], pc["dvt"], pt["dk"], pt["dvt"], lw["dlam"], lw["g_diff"], nmap=2, finish=True,
              post=1.0 - lam_init, tq=min(FLASH_ROWS // 2, t))
    xm, h2, aff, aff_t = _merge(x_l, mod_l, ya_l, (hf_l, hb_l), pt["lo"], (gf_l, gb_l), pt["gr"], yd_l, lw, tm=256)
    x_l = _ffn(xm, h2, aff, aff_t, mod_l, lw)

    if need_ctx:
        ya_c = fl(pc["mq"], pc["mk"], pc["mvt"], None, None, lw["dlam"], one_g, nmap=1, finish=False, post=1.0, tq=tc)
        yd_c = fl(pc["dq"], pc["dk"], pc["dvt"], None, None, lw["dlam"], lw["g_diff"], nmap=2, finish=True,
                  post=1.0 - lam_init, tq=tc)
        xm, h2, aff, aff_t = _merge(x_c, mod_c, ya_c, (hf_c, hb_c), pc["lo"], (gf_c, gb_c), pc["gr"], yd_c, lw, tm=tc)
        x_c = _ffn(xm, h2, aff, aff_t, mod_c, lw)
    return x_c, x_l


def kernel(x, c, ctx, c_ctx, w_ada, b_ada, g_mix_pre, g_mix_post, g_ffn_pre, g_ffn_post, w_in, g_q_lat, w_uq, g_kv_lat, w_ukv, w_conv, b_conv, b_igate, b_fgate, g_mlstm_out, w_alpha2, b_alpha, g_gla_out, diff_lambda, g_diff_out, w_branch, w_gate, b_gate, w_out, w_router, w_e_gate, w_e_up, w_e_down):
    p = dict(w_ada=w_ada, b_ada=b_ada, g_mix_pre=g_mix_pre, g_mix_post=g_mix_post, g_ffn_pre=g_ffn_pre,
             g_ffn_post=g_ffn_post, w_in=w_in, g_q_lat=g_q_lat, w_uq=w_uq, g_kv_lat=g_kv_lat, w_ukv=w_ukv,
             w_conv=w_conv, b_conv=b_conv, b_igate=b_igate, b_fgate=b_fgate, g_mlstm_out=g_mlstm_out,
             w_alpha2=w_alpha2, b_alpha=b_alpha, g_gla_out=g_gla_out, diff_lambda=diff_lambda, g_diff_out=g_diff_out,
             w_branch=w_branch, w_gate=w_gate, b_gate=b_gate, w_out=w_out, w_router=w_router,
             w_e_gate=w_e_gate, w_e_up=w_e_up, w_e_down=w_e_down)
    b = x.shape[0]
    c8 = jnp.concatenate([c, c_ctx[None], jnp.zeros((8 - b - 1, D), F32)], axis=0)
    x_c, x_l = ctx, x
    for i in range(DEPTH):
        x_c, x_l = _hybrid_layer(i, x_c, x_l, c8, i < DEPTH - 1, p)
    return x_l
```

```python
import functools
import math

import numpy as np
import jax
import jax.numpy as jnp
from jax import lax
from jax.experimental import pallas as pl
from jax.experimental.pallas import tpu as pltpu

F32 = jnp.float32
BF16 = jnp.bfloat16
I32 = jnp.int32
HIGHEST = lax.Precision.HIGHEST

D = 1024
DEPTH = 2
GRID_W = 64
N_HEADS = 4
MLA_NOPE, MLA_ROPE, MLA_V = 64, 32, 64
MLA_Q_LORA, MLA_KV_LORA = 256, 128
ML_DH = 64
GLA_DK, GLA_DV, GLA_RANK, GLA_TAU = 32, 64, 16, 16.0
DIFF_DQK, DIFF_DV = 32, 64
ROPE_DIM, ROPE_BASE = 32, 10000.0
N_EXPERTS, EC_CAPACITY, EXPERT_FF = 16, 2, 1408
NEG = -1e30
EPS = 1e-6
LOG2E = 1.4426950408889634

LANE = 128
HEAD_SLAB = 128
TOK_BLK = 256
ML_CHUNK = 128
GLA_CHUNK = 64
GLA_BLOCK = 256
GATHER_WIN = TOK_BLK + 16
GATHER_WIN_SMALL = 64
MOE_TOK = 1024
SLOT_BLK = 128
VMEM_LIMIT = 56 * 1024 * 1024

ZQ, ZKV, ZKRA, ZKRB, ZMLQK, ZMLV, ZMLO, ZGATE, ZGA = 0, 256, 384, 512, 640, 1152, 1408, 1664, 1792
ZGQ, ZGK, ZGV, ZGR, ZDQ, ZDQS, ZDK, ZDKS, NZ = 1920, 2048, 2176, 2432, 2688, 2944, 3200, 3456, 3712
KV_CHUNK = 256
VT_ROWS = 80
FLASH_ROWS = 512
FLASH_KEYS = 512
FLASH_UNROLL = 16


def _swap32(c):
    return (c // 32) * 32 + ((c % 32) ^ 8)


def _win_index():
    idx = -np.ones((NZ,), np.int64)
    idx[ZQ:ZQ + 256] = np.arange(0, 256)
    idx[ZKV:ZKV + 128] = np.arange(256, 384)
    r = np.arange(32)
    idx[ZKRA + 64:ZKRA + 96] = 384 + r
    idx[ZKRB + 64:ZKRB + 96] = 384 + (r ^ 8)
    ml = 416
    idx[ZMLQK:ZMLQK + 512] = ml + np.arange(512)
    idx[ZMLV:ZMLV + 256] = ml + 512 + np.arange(256)
    idx[ZMLO:ZMLO + 256] = ml + 768 + np.arange(256)
    idx[ZGATE:ZGATE + 16] = ml + 1024 + np.arange(16)
    gl = 1456
    idx[ZGQ:ZGQ + 128] = gl + np.arange(128)
    idx[ZGK:ZGK + 128] = gl + 128 + np.arange(128)
    idx[ZGV:ZGV + 256] = gl + 256 + np.arange(256)
    idx[ZGR:ZGR + 256] = gl + 512 + np.arange(256)
    idx[ZGA:ZGA + 32] = gl + 768 + np.arange(32)
    df = 2256
    c = np.arange(256)
    idx[ZDQ:ZDQ + 256] = df + c
    idx[ZDQS:ZDQS + 256] = df + _swap32(c)
    idx[ZDK:ZDK + 256] = df + 256 + c
    idx[ZDKS:ZDKS + 256] = df + 256 + _swap32(c)
    return idx


def _vt_rows(w_cols):
    n = w_cols.shape[0]
    w4 = w_cols.T.reshape(N_HEADS, 64, n)
    return jnp.pad(w4, ((0, 0), (0, VT_ROWS - 64), (0, 0))).reshape(N_HEADS * VT_ROWS, n)


_WIN_IDX = _win_index()


def _gather_cols(w, idx):
    safe = np.maximum(idx, 0)
    return jnp.where(jnp.asarray(idx >= 0)[None, :], w[:, safe], 0.0)


def _pad_heads_rows(w, width):
    n = w.shape[1]
    w4 = w.reshape(N_HEADS, width, n)
    return jnp.pad(w4, ((0, 0), (0, HEAD_SLAB - width), (0, 0))).reshape(N_HEADS * HEAD_SLAB, n)


def _cparams(sem):
    return pltpu.CompilerParams(dimension_semantics=sem, vmem_limit_bytes=VMEM_LIMIT)


def _rms(x):
    return x * lax.rsqrt(jnp.mean(x * x, axis=-1, keepdims=True) + EPS)


def _sigmoid(x):
    return 1.0 / (1.0 + jnp.exp(-x))


def _log_sigmoid(x):
    return jnp.minimum(x, 0.0) - jnp.log1p(jnp.exp(-jnp.abs(x)))


def _dot(a, b, precision=None):
    return jnp.dot(a, b, preferred_element_type=F32, precision=precision)


def _dot_nt(a, b, precision=None):
    return lax.dot_general(a, b, (((1,), (1,)), ((), ())), preferred_element_type=F32, precision=precision)


def _dot_tn(a, b, precision=None):
    return lax.dot_general(a, b, (((0,), (0,)), ((), ())), preferred_element_type=F32, precision=precision)


def _const_spec(shape):
    nd = len(shape)
    return pl.BlockSpec(shape, lambda *_: (0,) * nd)


def _ada_kernel(c_ref, w_ref, b_ref, o_ref):
    cv = c_ref[...]
    s = (cv * _sigmoid(cv)).astype(BF16)
    o_ref[...] = _dot(s, w_ref[...].astype(BF16)) + b_ref[...]


def _ada(c8, w_ada, b_ada):
    n, tn = 6 * D, 1024
    return pl.pallas_call(
        _ada_kernel, name="ada", grid=(n // tn,),
        in_specs=[pl.BlockSpec((8, D), lambda j: (0, 0)), pl.BlockSpec((D, tn), lambda j: (0, j)),
                  pl.BlockSpec((1, tn), lambda j: (0, j))],
        out_specs=pl.BlockSpec((8, tn), lambda j: (0, j)),
        out_shape=jax.ShapeDtypeStruct((8, n), F32), compiler_params=_cparams(("arbitrary",)),
    )(c8, w_ada, b_ada.reshape(1, n))


_PROJ_OUT = (
    ("mq", 512, BF16), ("mk", 512, BF16),
    ("lqk", 512, F32), ("lv", 256, BF16), ("lo", 256, BF16), ("gc", 128, F32),
    ("gq", 128, BF16), ("gk", 128, BF16), ("gv", 256, BF16), ("gr", 256, BF16), ("glg", 256, F32),
    ("dq", 1024, BF16), ("dk", 512, BF16),
)


def _proj_kernel(x_ref, mod_ref, g_ref, w_ref, tab_ref, gq_ref, wq_ref, wqs_ref, gkv_ref, wk_ref, wvt_ref,
                 wgt_ref, gbr_ref, gbc_ref, wal_ref, bal_ref, wdvt_ref,
                 mq_ref, mk_ref, lqk_ref, lv_ref, lo_ref, gc_ref, gq_o, gk_o, gv_o, gr_o, glg_o,
                 dq_ref, dk_ref, grow_ref, mvt_ref, dvt_ref):
    x = x_ref[0]
    tm = x.shape[0]
    mod = mod_ref[0]
    h = _rms(x) * g_ref[...] * (1.0 + mod[1:2]) + mod[0:1]
    hb = h.astype(BF16)
    z = _dot(hb, w_ref[...])
    tab = tab_ref[...]
    ct, st, cd, sd = tab[:, 0:128], tab[:, 128:256], tab[:, 256:384], tab[:, 384:512]
    lane = lax.broadcasted_iota(I32, (tm, LANE), 1)

    qn = (_rms(z[:, ZQ:ZQ + 256]) * gq_ref[...]).astype(BF16)
    qa = _dot(qn, wq_ref[...])
    qb = _dot(qn, wqs_ref[...])
    qscale = (MLA_NOPE + MLA_ROPE) ** -0.5 * LOG2E
    for hh in range(N_HEADS):
        sl = slice(HEAD_SLAB * hh, HEAD_SLAB * (hh + 1))
        mq_ref[0, :, sl] = ((qa[:, sl] * ct + qb[:, sl] * st) * qscale).astype(BF16)
    kvn = (_rms(z[:, ZKV:ZKV + 128]) * gkv_ref[...]).astype(BF16)
    kk = _dot(kvn, wk_ref[...])
    kr = z[:, ZKRA:ZKRA + 128] * ct + z[:, ZKRB:ZKRB + 128] * st
    for hh in range(N_HEADS):
        sl = slice(HEAD_SLAB * hh, HEAD_SLAB * (hh + 1))
        mk_ref[0, :, sl] = (kk[:, sl] + kr).astype(BF16)
    ones_row = lax.broadcasted_iota(I32, (N_HEADS * VT_ROWS, tm), 0) % VT_ROWS == MLA_V
    mvt_ref[0, 0] = jnp.where(ones_row, 1.0, _dot_nt(wvt_ref[...], kvn)).astype(BF16)
    dvt_ref[0, 0] = jnp.where(ones_row, 1.0, _dot_nt(wdvt_ref[...], hb)).astype(BF16)

    lqk_ref[0] = z[:, ZMLQK:ZMLQK + 512]
    lv_ref[0] = z[:, ZMLV:ZMLV + 256].astype(BF16)
    lo_ref[0] = z[:, ZMLO:ZMLO + 256].astype(BF16)
    gcol = z[:, ZGATE:ZGATE + 128] + gbr_ref[...]
    gc_ref[0] = jnp.where(lane < 8, gcol, jnp.where(lane < 16, _log_sigmoid(gcol), 0.0))
    zr = _dot_nt(wgt_ref[...], hb) + gbc_ref[...]
    rowi = lax.broadcasted_iota(I32, zr.shape, 0)
    grow_ref[0] = jnp.where(rowi < 8, zr, _log_sigmoid(zr))

    gq_o[0] = (z[:, ZGQ:ZGQ + 128] * GLA_DK ** -0.5).astype(BF16)
    gk_o[0] = z[:, ZGK:ZGK + 128].astype(BF16)
    gv_o[0] = z[:, ZGV:ZGV + 256].astype(BF16)
    gr_o[0] = z[:, ZGR:ZGR + 256].astype(BF16)
    zg = _dot(z[:, ZGA:ZGA + 128].astype(BF16), wal_ref[...]) + bal_ref[...]
    glg_o[0] = _log_sigmoid(zg) * (1.0 / GLA_TAU)

    dscale = DIFF_DQK ** -0.5 * LOG2E
    for g in range(2):
        gs = slice(128 * g, 128 * (g + 1))
        qg = (z[:, ZDQ:ZDQ + 256][:, gs] * cd + z[:, ZDQS:ZDQS + 256][:, gs] * sd) * dscale
        kg = z[:, ZDK:ZDK + 256][:, gs] * cd + z[:, ZDKS:ZDKS + 256][:, gs] * sd
        for hl in range(2):
            hh = 2 * g + hl
            for m in range(2):
                lo = 64 * hl + 32 * m
                s0 = (2 * hh + m) * HEAD_SLAB
                dq_ref[0, :, s0:s0 + HEAD_SLAB] = jnp.where((lane >= lo) & (lane < lo + 32), qg, 0.0).astype(BF16)
            dk_ref[0, :, HEAD_SLAB * hh:HEAD_SLAB * (hh + 1)] = jnp.where(
                (lane >= 64 * hl) & (lane < 64 * hl + 64), kg, 0.0).astype(BF16)


def _proj(x, mod, tab, lw):
    b, t, _ = x.shape
    tm = KV_CHUNK
    consts = [lw["g_mix_pre"], lw["w_ext"], None, lw["g_q_lat"], lw["wq"], lw["wqs"], lw["g_kv_lat"], lw["wk"], lw["wvt"],
              lw["wgt"], lw["gate_bias_row"], lw["gate_bias_col"], lw["walpha"], lw["balpha"], lw["wdvt"]]
    in_specs = [pl.BlockSpec((1, tm, D), lambda bi, i: (bi, i, 0)), pl.BlockSpec((1, 8, D), lambda bi, i: (bi, 0, 0))]
    args = [x, mod]
    for cst in consts:
        if cst is None:
            in_specs.append(pl.BlockSpec((tm, 512), lambda bi, i: (i, 0)))
            args.append(tab)
        else:
            in_specs.append(_const_spec(cst.shape))
            args.append(cst)
    out_specs = [pl.BlockSpec((1, tm, w), lambda bi, i: (bi, i, 0)) for _, w, _ in _PROJ_OUT]
    out_shape = [jax.ShapeDtypeStruct((b, t, w), dt) for _, w, dt in _PROJ_OUT]
    out_specs.append(pl.BlockSpec((1, 16, tm), lambda bi, i: (bi, 0, i)))
    out_shape.append(jax.ShapeDtypeStruct((b, 16, t), F32))
    for _ in range(2):
        out_specs.append(pl.BlockSpec((1, 1, N_HEADS * VT_ROWS, tm), lambda bi, i: (bi, i, 0, 0)))
        out_shape.append(jax.ShapeDtypeStruct((b, t // tm, N_HEADS * VT_ROWS, tm), BF16))
    outs = pl.pallas_call(
        _proj_kernel, name="proj", grid=(b, t // tm), in_specs=in_specs, out_specs=out_specs, out_shape=out_shape,
        compiler_params=_cparams(("parallel", "arbitrary")),
    )(*args)
    res = {name: o for (name, _, _), o in zip(_PROJ_OUT, outs[:-3])}
    res["grow"], res["mvt"], res["dvt"] = outs[-3:]
    return res


def _flash_kernel(*refs, nmap, has_lat, tk, finish, post):
    if has_lat:
        q_ref, kc_ref, vc_ref, kl_ref, vl_ref, dl_ref, g_ref, o_ref, s_ref = refs
    else:
        q_ref, kc_ref, vc_ref, dl_ref, g_ref, o_ref = refs
    qb = q_ref[0]
    tq = qb.shape[0]
    qt = jnp.concatenate([qb[:, HEAD_SLAB * mm:HEAD_SLAB * (mm + 1)].astype(F32).T for mm in range(nmap)],
                         axis=1).astype(BF16)
    rows = nmap * tq
    sub = tk // KV_CHUNK

    def softmax(s, smax, m):
        m_new = jnp.maximum(m, smax)
        return m_new, jnp.exp2(m - m_new), jnp.exp2(s - m_new).astype(BF16)

    def pv(p, vts):
        out = _dot(vts[0], p[0:KV_CHUNK])
        for c in range(1, len(vts)):
            out = out + _dot(vts[c], p[c * KV_CHUNK:(c + 1) * KV_CHUNK])
        return out

    s_ctx = _dot(kc_ref[0], qt)
    m, _, p = softmax(s_ctx, jnp.max(s_ctx, axis=0, keepdims=True), jnp.full((1, rows), NEG, F32))
    acc = pv(p, [vc_ref[0, 0]])
    if has_lat:
        n = kl_ref.shape[1] // tk

        unroll = min(FLASH_UNROLL, n)

        def scores(j):
            if isinstance(j, int):
                return _dot(kl_ref[0, j * tk:(j + 1) * tk, :], qt)
            off = pl.multiple_of(j * tk, tk)
            return _dot(kl_ref[0, pl.ds(off, tk), :], qt)

        def values(j):
            return [vl_ref[0, j * sub + c] for c in range(sub)]

        def produce(slot, j):
            s = scores(j)
            s_ref[slot] = s
            return jnp.max(s, axis=0, keepdims=True)

        smax0 = produce(0, 0)

        def body(jj, carry):
            m, acc, smax = carry
            j = unroll * jj
            for u in range(unroll):
                smax_next = smax
                if not isinstance(j, int):
                    smax_next = produce((u + 1) % 2, jnp.minimum(j + u + 1, n - 1))
                elif j + u + 1 < n:
                    smax_next = produce((u + 1) % 2, j + u + 1)
                m, alpha, p = softmax(s_ref[u % 2], smax, m)
                acc = alpha * acc + pv(p, values(j + u))
                smax = smax_next
            return m, acc, smax

        if unroll == n:
            m, acc, _ = body(0, (m, acc, smax0))
        else:
            m, acc, _ = lax.fori_loop(0, n // unroll, body, (m, acc, smax0))

    o = acc[0:MLA_V, :] / acc[MLA_V:MLA_V + 1, :]
    if nmap == 2:
        lv = dl_ref[...]
        lam = (jnp.exp(jnp.sum(lv[0:1] * lv[1:2], axis=-1, keepdims=True))
               - jnp.exp(jnp.sum(lv[2:3] * lv[3:4], axis=-1, keepdims=True)) + (1.0 - post))
        o = o[:, :tq] - lam * o[:, tq:]
    if finish:
        ms = jnp.mean(o * o, axis=0, keepdims=True)
        o = o * lax.rsqrt(ms + EPS) * jnp.concatenate([g_ref[0]] * (tq // LANE), axis=1) * post
    o_pad = jnp.concatenate([o, jnp.zeros((HEAD_SLAB - MLA_V, tq), F32)], axis=0)
    o_ref[0] = o_pad.T.astype(BF16)


def _flash(q, kc, vct, kl, vlt, dlam, g_out, *, nmap, finish, post, tq, tk):
    b, t, _ = q.shape
    has_lat = kl is not None
    assert kc.shape[1] == KV_CHUNK and tk % KV_CHUNK == 0
    qw = nmap * HEAD_SLAB
    kspec = lambda n: pl.BlockSpec((1, n, HEAD_SLAB), lambda bi, h, i: (bi, 0, h))
    vspec = lambda n: pl.BlockSpec((1, n // KV_CHUNK, VT_ROWS, KV_CHUNK), lambda bi, h, i: (bi, 0, h, 0))
    in_specs = [pl.BlockSpec((1, tq, qw), lambda bi, h, i: (bi, i, h)), kspec(KV_CHUNK), vspec(KV_CHUNK)]
    args = [q, kc, vct]
    scratch = []
    if has_lat:
        tl = kl.shape[1]
        assert (tl // tk) % min(FLASH_UNROLL, tl // tk) == 0
        in_specs += [kspec(tl), vspec(tl)]
        args += [kl, vlt]
        scratch = [pltpu.VMEM((2, tk, nmap * tq), F32)]
    in_specs += [_const_spec(dlam.shape), pl.BlockSpec((1, MLA_V, LANE), lambda bi, h, i: (h, 0, 0))]
    args += [dlam, g_out]
    return pl.pallas_call(
        functools.partial(_flash_kernel, nmap=nmap, has_lat=has_lat, tk=tk, finish=finish, post=post),
        name="flash_diff" if nmap == 2 else "flash_mla",
        grid=(b, N_HEADS, t // tq), in_specs=in_specs,
        out_specs=pl.BlockSpec((1, tq, HEAD_SLAB), lambda bi, h, i: (bi, i, h)),
        out_shape=jax.ShapeDtypeStruct((b, t, N_HEADS * HEAD_SLAB), BF16),
        scratch_shapes=scratch,
        compiler_params=_cparams(("parallel", "parallel", "arbitrary")),
    )(*args)


def _head_of(shape, axis, width):
    return (lax.broadcasted_iota(I32, shape, axis) % (N_HEADS * width)) // width


def _mlstm_dir(d, first, last, x, xprev, xnext, v, gcol, grow, wc, bcv, c_ref, m_ref):
    L = x.shape[0]
    row = lax.broadcasted_iota(I32, x.shape, 0)
    pr = jnp.where(first, 0.0, xprev[7:8, :])
    nx = jnp.where(last, 0.0, xnext[0:1, :])
    xm = jnp.where(row == 0, pr, pltpu.roll(x, 1, 0))
    xp = jnp.where(row == L - 1, nx, pltpu.roll(x, L - 1, 0))
    y = xm * wc[0:1] + x * wc[1:2] + xp * wc[2:3] + bcv
    qk = y * _sigmoid(y)
    q = qk[:, :256]
    k = qk[:, 256:] * ML_DH ** -0.5

    li = lax.broadcasted_iota(I32, (L, L), 0)
    si = lax.broadcasted_iota(I32, (L, L), 1)
    tin = (si <= li) if d == 0 else (si >= li)
    tinf = tin.astype(F32)
    bcol = _dot(tinf, gcol, HIGHEST)
    brow = _dot_nt(grow, tinf, HIGHEST)
    m0e = m_ref[0, d]
    cb = c_ref[0, d]
    hm256 = _head_of((L, 256), 1, ML_DH)
    hm512 = _head_of((L, 512), 1, ML_DH)
    e_idx = L - 1 if d == 0 else 0

    d_blk, inter_blk = [], []
    for hh in range(N_HEADS):
        c = 4 * d + hh
        bc = bcol[:, 8 + c:9 + c]
        d_blk.append(jnp.where(tin, bc - brow[8 + c:9 + c, :] + grow[c:c + 1, :], NEG))
        inter_blk.append(bc + m0e[0:1, 64 * hh:64 * hh + 1])
    d_st = jnp.concatenate(d_blk, axis=0)
    inter_st = jnp.concatenate(inter_blk, axis=0)
    mt = jnp.maximum(inter_st, jnp.max(d_st, axis=-1, keepdims=True))
    q_st = jnp.concatenate([jnp.where(hm256 == hh, q, 0.0) for hh in range(N_HEADS)], axis=0).astype(BF16)
    s_st = (jnp.exp(d_st - mt) * _dot_nt(q_st, k.astype(BF16))).astype(BF16)
    vext = jnp.concatenate([v, jnp.ones((L, 256), BF16)], axis=1)
    r = _dot(s_st, vext)
    aint = jnp.exp(inter_st - mt)
    p = _dot(q.astype(BF16), cb.astype(BF16))
    tot = jnp.zeros((L, 512), F32)
    mte = jnp.zeros((L, 256), F32)
    for hh in range(N_HEADS):
        rs = slice(hh * L, (hh + 1) * L)
        tot = jnp.where(hm512 == hh, r[rs] + aint[rs] * p, tot)
        mte = jnp.where(hm256 == hh, mt[rs], mte)
    hout = tot[:, :256] / jnp.maximum(jnp.abs(tot[:, 256:]), jnp.exp(-mte))

    wexp = jnp.zeros((L, 256), F32)
    arow = jnp.zeros((1, 512), F32)
    grw = jnp.zeros((1, 512), F32)
    mnew = jnp.zeros((1, 256), F32)
    hr512 = _head_of((1, 512), 1, ML_DH)
    hr256 = _head_of((1, 256), 1, ML_DH)
    for hh in range(N_HEADS):
        c = 4 * d + hh
        bc = bcol[:, 8 + c:9 + c]
        be = bc[e_idx:e_idx + 1, :]
        wl = be - bc + gcol[:, c:c + 1]
        mloc = jnp.max(wl, axis=0, keepdims=True)
        m0h = m0e[0:1, 64 * hh:64 * hh + 1]
        mn = jnp.maximum(be + m0h, mloc)
        wexp = jnp.where(hm256 == hh, jnp.exp(wl - mloc), wexp)
        arow = jnp.where(hr512 == hh, jnp.exp(be + m0h - mn), arow)
        grw = jnp.where(hr512 == hh, jnp.exp(mloc - mn), grw)
        mnew = jnp.where(hr256 == hh, mn, mnew)
    cl = _dot_tn((k * wexp).astype(BF16), vext)
    bd = lax.broadcasted_iota(I32, (256, 512), 0) // ML_DH == _head_of((256, 512), 1, ML_DH)
    c_ref[0, d] = arow * cb + jnp.where(bd, grw * cl, 0.0)
    m_ref[0, d] = jnp.broadcast_to(mnew, (8, 256))
    return hout


def _mlstm_kernel(xf, xfp, xfn, xb, xbp, xbn, vf, vb, gcf, gcb, grf, grb, wc_ref, bc_ref, c0_ref, m0_ref,
                  hf_ref, hb_ref, c_ref, m_ref):
    i = pl.program_id(1)
    n = pl.num_programs(1)

    @pl.when(i == 0)
    def _():
        c_ref[...] = c0_ref[...]
        m_ref[...] = m0_ref[...]

    wc = wc_ref[...]
    bcv = bc_ref[...]
    hf_ref[0] = _mlstm_dir(0, i == 0, i == n - 1, xf[0], xfp[0], xfn[0], vf[0], gcf[0], grf[0], wc, bcv, c_ref, m_ref)
    hb_ref[0] = _mlstm_dir(1, i == n - 1, i == 0, xb[0], xbp[0], xbn[0], vb[0], gcb[0], grb[0], wc, bcv, c_ref, m_ref)


def _mlstm(pr, w_conv, b_conv, c0, m0):
    x, v, gc, gr = pr["lqk"], pr["lv"], pr["gc"], pr["grow"]
    b, t, _ = x.shape
    L = ML_CHUNK
    n = t // L
    r8 = L // 8
    last8 = t // 8 - 1

    def fw(bi, i):
        return (bi, i, 0)

    def bw(bi, i):
        return (bi, n - 1 - i, 0)

    def halo(ix, shift):
        def f(bi, i):
            blk = ix(bi, i)[1]
            return (bi, jnp.clip(blk * r8 + shift, 0, last8), 0)
        return f

    main = lambda w, ix: pl.BlockSpec((1, L, w), ix)
    in_specs = [main(512, fw), pl.BlockSpec((1, 8, 512), halo(fw, -1)), pl.BlockSpec((1, 8, 512), halo(fw, r8)),
                main(512, bw), pl.BlockSpec((1, 8, 512), halo(bw, -1)), pl.BlockSpec((1, 8, 512), halo(bw, r8)),
                main(256, fw), main(256, bw), main(128, fw), main(128, bw),
                pl.BlockSpec((1, 16, L), lambda bi, i: (bi, 0, i)), pl.BlockSpec((1, 16, L), lambda bi, i: (bi, 0, n - 1 - i)),
                _const_spec(w_conv.shape), _const_spec(b_conv.shape),
                pl.BlockSpec((1, 2, 256, 512), lambda bi, i: (bi, 0, 0, 0)), pl.BlockSpec((1, 2, 8, 256), lambda bi, i: (bi, 0, 0, 0))]
    out_specs = [main(256, fw), main(256, bw),
                 pl.BlockSpec((1, 2, 256, 512), lambda bi, i: (bi, 0, 0, 0)), pl.BlockSpec((1, 2, 8, 256), lambda bi, i: (bi, 0, 0, 0))]
    out_shape = [jax.ShapeDtypeStruct((b, t, 256), F32), jax.ShapeDtypeStruct((b, t, 256), F32),
                 jax.ShapeDtypeStruct(c0.shape, F32), jax.ShapeDtypeStruct(m0.shape, F32)]
    return pl.pallas_call(
        _mlstm_kernel, name="mlstm", grid=(b, n), in_specs=in_specs, out_specs=out_specs, out_shape=out_shape,
        compiler_params=_cparams(("parallel", "arbitrary")),
    )(x, x, x, x, x, x, v, v, gc, gc, gr, gr, w_conv, b_conv, c0, m0)


def _gla_chunk(d, q, k, v, lg, sb):
    L = q.shape[0]
    li = lax.broadcasted_iota(I32, (L, L), 0)
    si = lax.broadcasted_iota(I32, (L, L), 1)
    tin = (si <= li) if d == 0 else (si >= li)
    tinf = tin.astype(F32)
    lgd = lg[:, 128 * d:128 * (d + 1)]
    gcum = _dot(tinf, lgd, HIGHEST)
    e_idx = L - 1 if d == 0 else 0
    gend = gcum[e_idx:e_idx + 1, :]
    qf, kf = q.astype(F32), k.astype(F32)
    q_dec = qf * jnp.exp(gcum)
    k_dec = (kf * jnp.exp(-gcum)).astype(BF16)
    k_end = (kf * jnp.exp(gend - gcum)).astype(BF16)
    hm128 = _head_of((L, 128), 1, GLA_DK)
    hm256 = _head_of((L, 256), 1, GLA_DV)
    q_st = jnp.concatenate([jnp.where(hm128 == hh, q_dec, 0.0) for hh in range(N_HEADS)], axis=0).astype(BF16)
    att = _dot_nt(q_st, k_dec)
    tin4 = jnp.concatenate([tin] * N_HEADS, axis=0)
    o_st = _dot(jnp.where(tin4, att, 0.0).astype(BF16), v)
    o = _dot(q_dec.astype(BF16), sb.astype(BF16))
    for hh in range(N_HEADS):
        o = o + jnp.where(hm256 == hh, o_st[hh * L:(hh + 1) * L], 0.0)
    gend_col = _dot_tn(lgd, jnp.ones((L, 256), F32), HIGHEST)
    bd = lax.broadcasted_iota(I32, (128, 256), 0) // GLA_DK == _head_of((128, 256), 1, GLA_DV)
    return o, jnp.exp(gend_col) * sb + jnp.where(bd, _dot_tn(k_end, v), 0.0)


def _gla_kernel(qf, kf, vf, lf, qb, kb, vb, lb, s0_ref, of_ref, ob_ref, s_ref):
    i = pl.program_id(1)

    @pl.when(i == 0)
    def _():
        s_ref[...] = s0_ref[...]

    L = GLA_CHUNK
    nsub = qf.shape[1] // L
    for d, (q, k, v, lg, o_ref) in enumerate(((qf, kf, vf, lf, of_ref), (qb, kb, vb, lb, ob_ref))):
        sb = s_ref[0, d]
        for c in (range(nsub) if d == 0 else reversed(range(nsub))):
            sl = slice(c * L, (c + 1) * L)
            o_ref[0, sl, :], sb = _gla_chunk(d, q[0, sl, :], k[0, sl, :], v[0, sl, :], lg[0, sl, :], sb)
        s_ref[0, d] = sb


def _gla(pr, s0):
    q, k, v, lg = pr["gq"], pr["gk"], pr["gv"], pr["glg"]
    b, t, _ = q.shape
    L = min(GLA_BLOCK, t)
    n = t // L
    fw = lambda bi, i: (bi, i, 0)
    bw = lambda bi, i: (bi, n - 1 - i, 0)
    blk = lambda w, ix: pl.BlockSpec((1, L, w), ix)
    st_spec = pl.BlockSpec((1, 2, 128, 256), lambda bi, i: (bi, 0, 0, 0))
    return pl.pallas_call(
        _gla_kernel, name="gla", grid=(b, n),
        in_specs=[blk(128, fw), blk(128, fw), blk(256, fw), blk(256, fw),
                  blk(128, bw), blk(128, bw), blk(256, bw), blk(256, bw), st_spec],
        out_specs=[blk(256, fw), blk(256, bw), st_spec],
        out_shape=[jax.ShapeDtypeStruct((b, t, 256), F32), jax.ShapeDtypeStruct((b, t, 256), F32),
                   jax.ShapeDtypeStruct(s0.shape, F32)],
        compiler_params=_cparams(("parallel", "arbitrary")),
    )(q, k, v, lg, q, k, v, lg, s0)


def _head_rms_expanded(x, width):
    n = x.shape[1]
    bd = (lax.broadcasted_iota(I32, (n, n), 0) // width == lax.broadcasted_iota(I32, (n, n), 1) // width).astype(F32)
    return _dot(x * x, bd, HIGHEST) * (1.0 / width)


def _merge_kernel(x_ref, mod_ref, ya_ref, hf_ref, hb_ref, lo_ref, gf_ref, gb_ref, gr_ref, yd_ref,
                  gpre_ref, wg_ref, bg_ref, wbr_ref, wo_ref, gpost_ref, gffn_ref, wr_ref, wrt_ref, gml_ref, ggla_ref,
                  xm_ref, h2_ref, aff_ref, afft_ref):
    x = x_ref[0]
    tm = x.shape[0]
    mod = mod_ref[0]
    hb = (_rms(x) * gpre_ref[...] * (1.0 + mod[1:2]) + mod[0:1]).astype(BF16)

    hs = hf_ref[0] + hb_ref[0]
    y_ml = _sigmoid(lo_ref[0].astype(F32)) * (hs * lax.rsqrt(_head_rms_expanded(hs, ML_DH) + EPS) * gml_ref[...])
    gs = gf_ref[0] + gb_ref[0]
    rr = gr_ref[0].astype(F32)
    y_gla = rr * _sigmoid(rr) * (gs * lax.rsqrt(_head_rms_expanded(gs, GLA_DV) + EPS) * ggla_ref[...])

    branches = ((ya_ref[0], 0, 512), (y_ml.astype(BF16), 512, 256), (y_gla.astype(BF16), 768, 256), (yd_ref[0], 1024, 512))
    mix = jnp.zeros((tm, D), F32)
    for nb, (yb, r0, rw) in enumerate(branches):
        gate = _sigmoid(_dot(hb, wg_ref[:, nb * D:(nb + 1) * D]) + bg_ref[:, nb * D:(nb + 1) * D])
        mix = mix + gate * _dot(yb, wbr_ref[r0:r0 + rw, :])
    y = _dot(mix.astype(BF16), wo_ref[...])
    xm = x + mod[2:3] * (_rms(y) * gpost_ref[...])
    xm_ref[0] = xm

    h2 = (_rms(xm) * gffn_ref[...] * (1.0 + mod[4:5]) + mod[3:4]).astype(BF16)
    h2_ref[0] = h2
    lane = lax.broadcasted_iota(I32, (tm, LANE), 1)
    lg = jnp.where(lane < N_EXPERTS, _dot(h2, wr_ref[...]), NEG)
    e = jnp.exp(lg - jnp.max(lg, axis=-1, keepdims=True))
    aff_ref[0] = (e / jnp.sum(e, axis=-1, keepdims=True))[:, :N_EXPERTS]
    lt = _dot_nt(wrt_ref[...], h2)
    et = jnp.exp(lt - jnp.max(lt, axis=0, keepdims=True))
    afft_ref[0] = et / jnp.sum(et, axis=0, keepdims=True)


def _merge(x, mod, ya, ml, lo, gl, gr, yd, lw, tm):
    b, t, _ = x.shape
    tok = lambda w: pl.BlockSpec((1, tm, w), lambda bi, i: (bi, i, 0))
    consts = [lw["g_mix_pre"], lw["w_gate"], lw["b_gate"], lw["wbr"], lw["w_out"], lw["g_mix_post"], lw["g_ffn_pre"],
              lw["w_router"], lw["w_router_t"], lw["g_mlstm_out"], lw["g_gla_out"]]
    in_specs = [tok(D), pl.BlockSpec((1, 8, D), lambda bi, i: (bi, 0, 0)), tok(512), tok(256), tok(256), tok(256),
                tok(256), tok(256), tok(256), tok(512)] + [_const_spec(c.shape) for c in consts]
    out_specs = [tok(D), tok(D), tok(N_EXPERTS), pl.BlockSpec((1, N_EXPERTS, tm), lambda bi, i: (bi, 0, i))]
    out_shape = [jax.ShapeDtypeStruct((b, t, D), F32), jax.ShapeDtypeStruct((b, t, D), BF16),
                 jax.ShapeDtypeStruct((b, t, N_EXPERTS), F32), jax.ShapeDtypeStruct((b, N_EXPERTS, t), F32)]
    return pl.pallas_call(
        _merge_kernel, name="merge", grid=(b, t // tm), in_specs=in_specs, out_specs=out_specs, out_shape=out_shape,
        compiler_params=_cparams(("parallel", "arbitrary")),
    )(x, mod, ya, ml[0], ml[1], lo, gl[0], gl[1], gr, yd, *consts)


def _topk_kernel(a_ref, pos_ref, s0_ref, *, cap):
    nblk = a_ref.shape[1]
    bits = pltpu.bitcast(a_ref[0], I32)

    def bisect(i, thr):
        cand = thr | (1 << (30 - i))
        cnt = jnp.sum((bits >= cand).astype(I32), axis=(0, 2), keepdims=True)
        return jnp.where(cnt >= cap, cand, thr)

    thr3 = lax.fori_loop(0, 31, bisect, jnp.zeros((1, N_EXPERTS, 1), I32))
    need3 = cap - jnp.sum((bits > thr3).astype(I32), axis=(0, 2), keepdims=True)
    thr, need = thr3[0], need3[0].astype(F32)
    upper = (lax.broadcasted_iota(I32, (TOK_BLK, TOK_BLK), 0) <= lax.broadcasted_iota(I32, (TOK_BLK, TOK_BLK), 1)).astype(BF16)

    def blk(j, carry):
        c_eq, c_sel = carry
        bj = pltpu.bitcast(a_ref[0, j], I32)
        gt, eq = bj > thr, bj == thr
        cum_eq = _dot(eq.astype(BF16), upper) + c_eq
        sel = gt | (eq & (cum_eq <= need))
        cum_sel = _dot(sel.astype(BF16), upper) + c_sel
        pos_ref[0, j] = jnp.where(sel, cum_sel - 1.0, -1.0).astype(I32)
        s0_ref[0, j] = jnp.broadcast_to(c_sel, (N_EXPERTS, LANE)).astype(I32)
        return cum_eq[:, TOK_BLK - 1:TOK_BLK], cum_sel[:, TOK_BLK - 1:TOK_BLK]

    zero = jnp.zeros((N_EXPERTS, 1), F32)
    lax.fori_loop(0, nblk, blk, (zero, zero))


def _topk(aff_t, cap):
    b, _, t = aff_t.shape
    nblk = t // TOK_BLK
    a4 = aff_t.reshape(b, N_EXPERTS, nblk, TOK_BLK).transpose(0, 2, 1, 3)
    spec = lambda w: pl.BlockSpec((1, nblk, N_EXPERTS, w), lambda bi: (bi, 0, 0, 0))
    return pl.pallas_call(
        functools.partial(_topk_kernel, cap=cap), name="topk", grid=(b,),
        in_specs=[spec(TOK_BLK)], out_specs=[spec(TOK_BLK), spec(LANE)],
        out_shape=[jax.ShapeDtypeStruct((b, nblk, N_EXPERTS, TOK_BLK), I32),
                   jax.ShapeDtypeStruct((b, nblk, N_EXPERTS, LANE), I32)],
        compiler_params=_cparams(("parallel",)),
    )(a4)


def _moe_kernel(s0_ref, pos_ref, h_ref, wg_ref, wu_ref, wd_ref, ys_ref, xs_ref, *, nblk, nsub, rows_step):
    bi, e, tb = pl.program_id(0), pl.program_id(1), pl.program_id(2)
    cur = e % 2

    @pl.when(e < N_EXPERTS)
    def _():
        @pl.when(tb == 0)
        def _():
            xs_ref[cur] = jnp.zeros(xs_ref.shape[1:], BF16)

        base = (bi * N_EXPERTS + e) * (nblk + 1) + tb * nsub
        for sb in range(nsub):
            s0 = s0_ref[base + sb]
            s1 = s0_ref[base + sb + 1]
            a0 = pl.multiple_of((s0 // 16) * 16, 16)

            def gather(win, sb=sb, a0=a0):
                prow = pos_ref[0, sb, pl.ds(e, 1), :]
                slot = lax.broadcasted_iota(I32, (win, TOK_BLK), 0) + a0
                rows = _dot((slot == prow).astype(BF16), h_ref[0, sb * TOK_BLK:(sb + 1) * TOK_BLK, :])
                xs_ref[cur, pl.ds(a0, win), :] = xs_ref[cur, pl.ds(a0, win), :] + rows.astype(BF16)

            pl.when((s1 > s0) & (s1 - a0 <= GATHER_WIN_SMALL))(functools.partial(gather, GATHER_WIN_SMALL))
            pl.when(s1 - a0 > GATHER_WIN_SMALL)(functools.partial(gather, GATHER_WIN))

    @pl.when(e > 0)
    def _():
        off = pl.multiple_of(tb * rows_step, rows_step)
        xc = xs_ref[1 - cur, pl.ds(off, rows_step), :]
        hg = _dot(xc, wg_ref[0])
        hid = (hg * _sigmoid(hg) * _dot(xc, wu_ref[0])).astype(BF16)
        ys_ref[0, 0, pl.ds(off, rows_step), :] = _dot(hid, wd_ref[0]).astype(BF16)


def _moe(s0_flat, pos4, h2, lw, capp):
    b, t, _ = h2.shape
    nblk = t // TOK_BLK
    tok = min(MOE_TOK, t)
    nsub = tok // TOK_BLK
    nstep = t // tok
    prev = lambda e: jnp.maximum(e - 1, 0)
    grid_spec = pltpu.PrefetchScalarGridSpec(
        num_scalar_prefetch=1, grid=(b, N_EXPERTS + 1, nstep),
        in_specs=[pl.BlockSpec((1, nsub, N_EXPERTS, TOK_BLK), lambda bi, e, tb, s: (bi, tb, 0, 0)),
                  pl.BlockSpec((1, tok, D), lambda bi, e, tb, s: (bi, tb, 0)),
                  pl.BlockSpec((1, D, EXPERT_FF), lambda bi, e, tb, s: (prev(e), 0, 0)),
                  pl.BlockSpec((1, D, EXPERT_FF), lambda bi, e, tb, s: (prev(e), 0, 0)),
                  pl.BlockSpec((1, EXPERT_FF, D), lambda bi, e, tb, s: (prev(e), 0, 0))],
        out_specs=pl.BlockSpec((1, 1, capp, D), lambda bi, e, tb, s: (bi, prev(e), 0, 0)),
        scratch_shapes=[pltpu.VMEM((2, capp + GATHER_WIN, D), BF16)])
    return pl.pallas_call(
        functools.partial(_moe_kernel, nblk=nblk, nsub=nsub, rows_step=capp // nstep), name="moe", grid_spec=grid_spec,
        out_shape=jax.ShapeDtypeStruct((b, N_EXPERTS, capp, D), BF16),
        compiler_params=_cparams(("arbitrary", "arbitrary", "arbitrary")),
    )(s0_flat, pos4, h2, lw["w_e_gate"], lw["w_e_up"], lw["w_e_down"])


def _combine_kernel(s0_ref, *refs, nblk, nb, sblk):
    ys_refs = refs[:2 * N_EXPERTS]
    pos_ref, aff_ref, xm_ref, mod_ref, g_ref, o_ref = refs[2 * N_EXPERTS:]
    bi, tb = pl.program_id(0), pl.program_id(1)
    pos = pos_ref[0]
    aff = aff_ref[0]
    lane = lax.broadcasted_iota(I32, (TOK_BLK, 2 * sblk), 1)
    acc = jnp.zeros((TOK_BLK, D), F32)
    for e in range(N_EXPERTS):
        s0 = s0_ref[(bi * N_EXPERTS + e) * (nblk + 1) + tb]
        blk0 = jnp.minimum(s0 // sblk, nb - 1)
        rel = pos[:, e:e + 1] - blk0 * sblk
        ysw = jnp.concatenate([ys_refs[2 * e][0, 0], ys_refs[2 * e + 1][0, 0]], axis=0)
        acc = acc + aff[:, e:e + 1] * _dot((lane == rel).astype(BF16), ysw)
    mod = mod_ref[0]
    o_ref[0] = xm_ref[0] + mod[5:6] * (_rms(acc) * g_ref[...])


def _combine(s0_flat, ys, pos_t, aff, xm, mod, g_post, sblk):
    b, t, _ = xm.shape
    nblk = t // TOK_BLK
    nb = ys.shape[2] // sblk

    def ys_spec(e, k):
        def ix(bi, tb, s):
            blk0 = jnp.minimum(s[(bi * N_EXPERTS + e) * (nblk + 1) + tb] // sblk, nb - 1)
            return (bi, e, jnp.minimum(blk0 + k, nb - 1), 0)
        return pl.BlockSpec((1, 1, sblk, D), ix)

    tok = lambda w: pl.BlockSpec((1, TOK_BLK, w), lambda bi, tb, s: (bi, tb, 0))
    in_specs = [ys_spec(e, k) for e in range(N_EXPERTS) for k in range(2)]
    in_specs += [tok(N_EXPERTS), tok(N_EXPERTS), tok(D), pl.BlockSpec((1, 8, D), lambda bi, tb, s: (bi, 0, 0)),
                 pl.BlockSpec((1, D), lambda bi, tb, s: (0, 0))]
    grid_spec = pltpu.PrefetchScalarGridSpec(num_scalar_prefetch=1, grid=(b, nblk), in_specs=in_specs, out_specs=tok(D))
    return pl.pallas_call(
        functools.partial(_combine_kernel, nblk=nblk, nb=nb, sblk=sblk), name="combine", grid_spec=grid_spec,
        out_shape=jax.ShapeDtypeStruct((b, t, D), F32),
        compiler_params=_cparams(("arbitrary", "arbitrary")),
    )(s0_flat, *([ys] * (2 * N_EXPERTS)), pos_t, aff, xm, mod, g_post)


def _rope_table(t):
    nf = ROPE_DIM // 4
    pos = jnp.arange(t)
    inv = ROPE_BASE ** (-jnp.arange(nf, dtype=F32) / nf)
    ang = jnp.stack([pos // GRID_W, pos % GRID_W], axis=-1).astype(F32)[..., None] * inv
    cos, sin = jnp.cos(ang), jnp.sin(ang)
    c32 = jnp.stack([cos, cos], axis=2).reshape(t, ROPE_DIM)
    s32 = jnp.stack([-sin, sin], axis=2).reshape(t, ROPE_DIM)
    one, zero = jnp.ones((t, 64), F32), jnp.zeros((t, 32), F32)
    ct = jnp.concatenate([one, c32, zero], axis=1)
    st = jnp.concatenate([0.0 * one, s32, zero], axis=1)
    return jnp.concatenate([ct, st, jnp.tile(c32, (1, 4)), jnp.tile(s32, (1, 4))], axis=1)


def _identity_table(t):
    one, zero = jnp.ones((t, 128), F32), jnp.zeros((t, 128), F32)
    ct = jnp.concatenate([jnp.ones((t, 96), F32), jnp.zeros((t, 32), F32)], axis=1)
    return jnp.concatenate([ct, zero, one, zero], axis=1)


def _layer_weights(i, p):
    lw = {}
    row = lambda a: a.reshape(1, -1)
    for name in ("g_mix_pre", "g_mix_post", "g_ffn_pre", "g_ffn_post", "g_q_lat", "g_kv_lat", "g_mlstm_out", "g_gla_out"):
        lw[name] = row(p[name][i])
    lw["w_ext"] = _gather_cols(p["w_in"][i], _WIN_IDX).astype(BF16)
    lw["wgt"] = p["w_in"][i][:, 416 + 1024:416 + 1040].T.astype(BF16)
    gb = jnp.concatenate([p["b_igate"][i].reshape(-1), p["b_fgate"][i].reshape(-1)])
    lw["gate_bias_row"] = jnp.pad(gb, (0, LANE - 16)).reshape(1, LANE)
    lw["gate_bias_col"] = gb.reshape(16, 1)
    wuq = p["w_uq"][i]
    qi = -np.ones((512,), np.int64)
    qsi = -np.ones((512,), np.int64)
    for h in range(N_HEADS):
        qi[128 * h:128 * h + 96] = 96 * h + np.arange(96)
        qsi[128 * h + 64:128 * h + 96] = 96 * h + 64 + (np.arange(32) ^ 8)
    lw["wq"] = _gather_cols(wuq, qi).astype(BF16)
    lw["wqs"] = _gather_cols(wuq, qsi).astype(BF16)
    ki = -np.ones((512,), np.int64)
    for h in range(N_HEADS):
        ki[128 * h:128 * h + 64] = 128 * h + np.arange(64)
    lw["wk"] = _gather_cols(p["w_ukv"][i], ki).astype(BF16)
    lw["wvt"] = _vt_rows(p["w_ukv"][i].reshape(MLA_KV_LORA, N_HEADS, 128)[:, :, 64:].reshape(MLA_KV_LORA, 256)).astype(BF16)
    lw["wdvt"] = _vt_rows(p["w_in"][i][:, 2768:3024]).astype(BF16)
    wa = p["w_alpha2"][i]
    wal = jnp.zeros((LANE, 256), F32).at[0:16, 0:128].set(wa[0]).at[16:32, 128:256].set(wa[1])
    lw["walpha"] = wal.astype(BF16)
    lw["balpha"] = p["b_alpha"][i].reshape(1, 256)
    lw["w_conv"] = p["w_conv"][i]
    lw["b_conv"] = row(p["b_conv"][i])
    lw["dlam"] = p["diff_lambda"][i]
    lw["g_diff"] = jnp.broadcast_to(p["g_diff_out"][i].reshape(N_HEADS, DIFF_DV, 1), (N_HEADS, DIFF_DV, LANE))
    wb = p["w_branch"][i]
    lw["wbr"] = jnp.concatenate([_pad_heads_rows(wb[0], 64), wb[1], wb[2], _pad_heads_rows(wb[3], 64)], axis=0).astype(BF16)
    lw["w_gate"] = p["w_gate"][i].astype(BF16)
    lw["b_gate"] = row(p["b_gate"][i])
    lw["w_out"] = p["w_out"][i].astype(BF16)
    lw["w_router"] = jnp.pad(p["w_router"][i], ((0, 0), (0, LANE - N_EXPERTS))).astype(BF16)
    lw["w_router_t"] = p["w_router"][i].T.astype(BF16)
    lw["w_e_gate"] = p["w_e_gate"][i].astype(BF16)
    lw["w_e_up"] = p["w_e_up"][i].astype(BF16)
    lw["w_e_down"] = p["w_e_down"][i].astype(BF16)
    return lw


def _ffn(xm, h2, aff, aff_t, mod, lw):
    b, t, _ = xm.shape
    nblk = t // TOK_BLK
    cap = EC_CAPACITY * t // N_EXPERTS
    capp = -(-cap // TOK_BLK) * TOK_BLK
    pos4, s04 = _topk(aff_t, cap)
    s0_be = jnp.concatenate([s04[..., 0].transpose(0, 2, 1), jnp.full((b, N_EXPERTS, 1), cap, I32)], axis=-1)
    s0_flat = s0_be.reshape(-1)
    ys = _moe(s0_flat, pos4, h2, lw, capp)
    pos_t = pos4.transpose(0, 1, 3, 2).reshape(b, t, N_EXPERTS)
    first = jnp.minimum(s0_be[..., :-1] // SLOT_BLK, capp // SLOT_BLK - 1)
    fits = jnp.all(s0_be[..., 1:] <= (first + 2) * SLOT_BLK)
    args = (s0_flat, ys, pos_t, aff, xm, mod, lw["g_ffn_post"])
    return lax.cond(fits, functools.partial(_combine, sblk=SLOT_BLK), functools.partial(_combine, sblk=TOK_BLK), *args)


def _hybrid_layer(i, x_c, x_l, c8, need_ctx, p):
    lw = _layer_weights(i, p)
    b, t, _ = x_l.shape
    tc = x_c.shape[1]
    lam_init = 0.8 - 0.6 * math.exp(-0.3 * i)
    mod8 = _ada(c8, p["w_ada"][i], p["b_ada"][i])
    pad = lambda m: jnp.pad(m.reshape(b, 6, D), ((0, 0), (0, 2), (0, 0)))
    mod_l = pad(mod8[:b])
    mod_c = pad(jnp.broadcast_to(mod8[b:b + 1], (b, 6 * D)))

    pc = _proj(x_c, mod_c, _identity_table(tc), lw)
    pt = _proj(x_l, mod_l, _rope_table(t), lw)

    zc = jnp.zeros((b, 2, 256, 512), F32)
    zm = jnp.full((b, 2, 8, 256), NEG, F32)
    zs = jnp.zeros((b, 2, 128, 256), F32)
    hf_c, hb_c, c_fin, m_fin = _mlstm(pc, lw["w_conv"], lw["b_conv"], zc, zm)
    hf_l, hb_l, _, _ = _mlstm(pt, lw["w_conv"], lw["b_conv"], c_fin, m_fin)
    gf_c, gb_c, s_fin = _gla(pc, zs)
    gf_l, gb_l, _ = _gla(pt, s_fin)

    one_g = jnp.ones((N_HEADS, MLA_V, LANE), F32)
    fl = functools.partial(_flash, tk=FLASH_KEYS)
    ya_l = fl(pt["mq"], pc["mk"], pc["mvt"], pt["mk"], pt["mvt"], lw["dlam"], one_g, nmap=1, finish=False, post=1.0,
              tq=min(FLASH_ROWS, t))
    yd_l = fl(pt["dq"], pc["dk"], pc["dvt"], pt["dk"], pt["dvt"], lw["dlam"], lw["g_diff"], nmap=2, finish=True,
              post=1.0 - lam_init, tq=min(FLASH_ROWS // 2, t))
    xm, h2, aff, aff_t = _merge(x_l, mod_l, ya_l, (hf_l, hb_l), pt["lo"], (gf_l, gb_l), pt["gr"], yd_l, lw, tm=256)
    x_l = _ffn(xm, h2, aff, aff_t, mod_l, lw)

    if need_ctx:
        ya_c = fl(pc["mq"], pc["mk"], pc["mvt"], None, None, lw["dlam"], one_g, nmap=1, finish=False, post=1.0, tq=tc)
        yd_c = fl(pc["dq"], pc["dk"], pc["dvt"], None, None, lw["dlam"], lw["g_diff"], nmap=2, finish=True,
                  post=1.0 - lam_init, tq=tc)
        xm, h2, aff, aff_t = _merge(x_c, mod_c, ya_c, (hf_c, hb_c), pc["lo"], (gf_c, gb_c), pc["gr"], yd_c, lw, tm=tc)
        x_c = _ffn(xm, h2, aff, aff_t, mod_c, lw)
    return x_c, x_l


def kernel(x, c, ctx, c_ctx, w_ada, b_ada, g_mix_pre, g_mix_post, g_ffn_pre, g_ffn_post, w_in, g_q_lat, w_uq, g_kv_lat, w_ukv, w_conv, b_conv, b_igate, b_fgate, g_mlstm_out, w_alpha2, b_alpha, g_gla_out, diff_lambda, g_diff_out, w_branch, w_gate, b_gate, w_out, w_router, w_e_gate, w_e_up, w_e_down):
    p = dict(w_ada=w_ada, b_ada=b_ada, g_mix_pre=g_mix_pre, g_mix_post=g_mix_post, g_ffn_pre=g_ffn_pre,
             g_ffn_post=g_ffn_post, w_in=w_in, g_q_lat=g_q_lat, w_uq=w_uq, g_kv_lat=g_kv_lat, w_ukv=w_ukv,
             w_conv=w_conv, b_conv=b_conv, b_igate=b_igate, b_fgate=b_fgate, g_mlstm_out=g_mlstm_out,
             w_alpha2=w_alpha2, b_alpha=b_alpha, g_gla_out=g_gla_out, diff_lambda=diff_lambda, g_diff_out=g_diff_out,
             w_branch=w_branch, w_gate=w_gate, b_gate=b_gate, w_out=w_out, w_router=w_router,
             w_e_gate=w_e_gate, w_e_up=w_e_up, w_e_down=w_e_down)
    b = x.shape[0]
    c8 = jnp.concatenate([c, c_ctx[None], jnp.zeros((8 - b - 1, D), F32)], axis=0)
    x_c, x_l = ctx, x
    for i in range(DEPTH):
        x_c, x_l = _hybrid_layer(i, x_c, x_l, c8, i < DEPTH - 1, p)
    return x_l
```

```python
import functools
import math

import numpy as np
import jax
import jax.numpy as jnp
from jax import lax
from jax.experimental import pallas as pl
from jax.experimental.pallas import tpu as pltpu

F32 = jnp.float32
BF16 = jnp.bfloat16
I32 = jnp.int32

D = 1024
DEPTH = 2
GRID_W = 64
N_HEADS = 4
MLA_NOPE, MLA_ROPE, MLA_V = 64, 32, 64
MLA_Q_LORA, MLA_KV_LORA = 256, 128
ML_DH = 64
GLA_DK, GLA_DV, GLA_RANK, GLA_TAU = 32, 64, 16, 16.0
DIFF_DQK, DIFF_DV = 32, 64
ROPE_DIM, ROPE_BASE = 32, 10000.0
N_EXPERTS, EC_CAPACITY, EXPERT_FF = 16, 2, 1408
NEG = -1e30
EPS = 1e-6
LOG2E = 1.4426950408889634

LANE = 128
HEAD_SLAB = 128
TOK_BLK = 256
ML_CHUNK = 128
GLA_CHUNK = 64
GLA_BLOCK = 256
GATHER_WIN = TOK_BLK + 16
GATHER_WIN_SMALL = 64
MOE_TOK = 2048
SLOT_BLK = 128
VMEM_LIMIT = 56 * 1024 * 1024

ZQ, ZKV, ZKRA, ZKRB, ZMLQK, ZMLV, ZMLO, ZGATE, ZGA = 0, 256, 384, 512, 640, 1152, 1408, 1664, 1792
ZGQ, ZGK, ZGV, ZGR, ZDQ, ZDQS, ZDK, ZDKS, NZ = 1920, 2048, 2176, 2432, 2688, 2944, 3200, 3456, 3712
KV_CHUNK = 256
VT_ROWS = 80
FLASH_ROWS = 1024
FLASH_KEYS = 512
FLASH_UNROLL = 16


def _swap32(c):
    return (c // 32) * 32 + ((c % 32) ^ 8)


def _win_index():
    idx = -np.ones((NZ,), np.int64)
    idx[ZQ:ZQ + 256] = np.arange(0, 256)
    idx[ZKV:ZKV + 128] = np.arange(256, 384)
    r = np.arange(32)
    idx[ZKRA + 64:ZKRA + 96] = 384 + r
    idx[ZKRB + 64:ZKRB + 96] = 384 + (r ^ 8)
    ml = 416
    idx[ZMLQK:ZMLQK + 512] = ml + np.arange(512)
    idx[ZMLV:ZMLV + 256] = ml + 512 + np.arange(256)
    idx[ZMLO:ZMLO + 256] = ml + 768 + np.arange(256)
    idx[ZGATE:ZGATE + 16] = ml + 1024 + np.arange(16)
    gl = 1456
    idx[ZGQ:ZGQ + 128] = gl + np.arange(128)
    idx[ZGK:ZGK + 128] = gl + 128 + np.arange(128)
    idx[ZGV:ZGV + 256] = gl + 256 + np.arange(256)
    idx[ZGR:ZGR + 256] = gl + 512 + np.arange(256)
    idx[ZGA:ZGA + 32] = gl + 768 + np.arange(32)
    df = 2256
    c = np.arange(256)
    idx[ZDQ:ZDQ + 256] = df + c
    idx[ZDQS:ZDQS + 256] = df + _swap32(c)
    idx[ZDK:ZDK + 256] = df + 256 + c
    idx[ZDKS:ZDKS + 256] = df + 256 + _swap32(c)
    return idx


def _vt_rows(w_cols):
    n = w_cols.shape[0]
    w4 = w_cols.T.reshape(N_HEADS, 64, n)
    return jnp.pad(w4, ((0, 0), (0, VT_ROWS - 64), (0, 0))).reshape(N_HEADS * VT_ROWS, n)


_WIN_IDX = _win_index()


def _gather_cols(w, idx):
    safe = np.maximum(idx, 0)
    return jnp.where(jnp.asarray(idx >= 0)[None, :], w[:, safe], 0.0)


def _pad_heads_rows(w, width):
    n = w.shape[1]
    w4 = w.reshape(N_HEADS, width, n)
    return jnp.pad(w4, ((0, 0), (0, HEAD_SLAB - width), (0, 0))).reshape(N_HEADS * HEAD_SLAB, n)


def _cparams(sem):
    return pltpu.CompilerParams(dimension_semantics=sem, vmem_limit_bytes=VMEM_LIMIT)


def _rms(x):
    return x * lax.rsqrt(jnp.mean(x * x, axis=-1, keepdims=True) + EPS)


def _sigmoid(x):
    return 1.0 / (1.0 + jnp.exp(-x))


def _log_sigmoid(x):
    return jnp.minimum(x, 0.0) - jnp.log1p(jnp.exp(-jnp.abs(x)))


def _dot(a, b, precision=None):
    return jnp.dot(a, b, preferred_element_type=F32, precision=precision)


def _dot_nt(a, b, precision=None):
    return lax.dot_general(a, b, (((1,), (1,)), ((), ())), preferred_element_type=F32, precision=precision)


def _dot_tn(a, b, precision=None):
    return lax.dot_general(a, b, (((0,), (0,)), ((), ())), preferred_element_type=F32, precision=precision)


def _split_bf16(x, parts):
    out, r = [], x
    for _ in range(parts):
        t = r.astype(BF16)
        out.append(t)
        r = r - t.astype(F32)
    return out


def _dot_sel(sel, x, parts=3):
    return sum(_dot(sel, t) for t in _split_bf16(x, parts))


def _const_spec(shape):
    nd = len(shape)
    return pl.BlockSpec(shape, lambda *_: (0,) * nd)


def _ada_kernel(c_ref, w_ref, b_ref, o_ref):
    cv = c_ref[...]
    s = (cv * _sigmoid(cv)).astype(BF16)
    o_ref[...] = _dot(s, w_ref[...].astype(BF16)) + b_ref[...]


def _ada(c8, w_ada, b_ada):
    n, tn = 6 * D, 1024
    return pl.pallas_call(
        _ada_kernel, name="ada", grid=(n // tn,),
        in_specs=[pl.BlockSpec((8, D), lambda j: (0, 0)), pl.BlockSpec((D, tn), lambda j: (0, j)),
                  pl.BlockSpec((1, tn), lambda j: (0, j))],
        out_specs=pl.BlockSpec((8, tn), lambda j: (0, j)),
        out_shape=jax.ShapeDtypeStruct((8, n), F32), compiler_params=_cparams(("arbitrary",)),
    )(c8, w_ada, b_ada.reshape(1, n))


_PROJ_OUT = (
    ("mq", 512, BF16), ("mk", 512, BF16),
    ("lqk", 512, F32), ("lv", 256, BF16), ("lo", 256, BF16), ("gc", 128, F32),
    ("gq", 128, BF16), ("gk", 128, BF16), ("gv", 256, BF16), ("gr", 256, BF16), ("glg", 256, F32),
    ("dq", 1024, BF16), ("dk", 512, BF16),
)


def _proj_kernel(x_ref, mod_ref, g_ref, w_ref, tab_ref, gq_ref, wq_ref, wqs_ref, gkv_ref, wk_ref, wvt_ref,
                 wgt_ref, gbr_ref, gbc_ref, wal_ref, bal_ref, wdvt_ref,
                 mq_ref, mk_ref, lqk_ref, lv_ref, lo_ref, gc_ref, gq_o, gk_o, gv_o, gr_o, glg_o,
                 dq_ref, dk_ref, grow_ref, mvt_ref, dvt_ref):
    x = x_ref[0]
    tm = x.shape[0]
    mod = mod_ref[0]
    h = _rms(x) * g_ref[...] * (1.0 + mod[1:2]) + mod[0:1]
    hb = h.astype(BF16)
    z = _dot(hb, w_ref[...])
    tab = tab_ref[...]
    ct, st, cd, sd = tab[:, 0:128], tab[:, 128:256], tab[:, 256:384], tab[:, 384:512]
    lane = lax.broadcasted_iota(I32, (tm, LANE), 1)

    qn = (_rms(z[:, ZQ:ZQ + 256]) * gq_ref[...]).astype(BF16)
    qa = _dot(qn, wq_ref[...])
    qb = _dot(qn, wqs_ref[...])
    qscale = (MLA_NOPE + MLA_ROPE) ** -0.5 * LOG2E
    for hh in range(N_HEADS):
        sl = slice(HEAD_SLAB * hh, HEAD_SLAB * (hh + 1))
        mq_ref[0, :, sl] = ((qa[:, sl] * ct + qb[:, sl] * st) * qscale).astype(BF16)
    kvn = (_rms(z[:, ZKV:ZKV + 128]) * gkv_ref[...]).astype(BF16)
    kk = _dot(kvn, wk_ref[...])
    kr = z[:, ZKRA:ZKRA + 128] * ct + z[:, ZKRB:ZKRB + 128] * st
    for hh in range(N_HEADS):
        sl = slice(HEAD_SLAB * hh, HEAD_SLAB * (hh + 1))
        mk_ref[0, :, sl] = (kk[:, sl] + kr).astype(BF16)
    ones_row = lax.broadcasted_iota(I32, (N_HEADS * VT_ROWS, tm), 0) % VT_ROWS == MLA_V
    mvt_ref[0, 0] = jnp.where(ones_row, 1.0, _dot_nt(wvt_ref[...], kvn)).astype(BF16)
    dvt_ref[0, 0] = jnp.where(ones_row, 1.0, _dot_nt(wdvt_ref[...], hb)).astype(BF16)

    lqk_ref[0] = z[:, ZMLQK:ZMLQK + 512]
    lv_ref[0] = z[:, ZMLV:ZMLV + 256].astype(BF16)
    lo_ref[0] = z[:, ZMLO:ZMLO + 256].astype(BF16)
    gcol = z[:, ZGATE:ZGATE + 128] + gbr_ref[...]
    gc_ref[0] = jnp.where(lane < 8, gcol, jnp.where(lane < 16, _log_sigmoid(gcol), 0.0))
    zr = _dot_nt(wgt_ref[...], hb) + gbc_ref[...]
    rowi = lax.broadcasted_iota(I32, zr.shape, 0)
    grow_ref[0] = jnp.where(rowi < 8, zr, _log_sigmoid(zr))

    gq_o[0] = (z[:, ZGQ:ZGQ + 128] * GLA_DK ** -0.5).astype(BF16)
    gk_o[0] = z[:, ZGK:ZGK + 128].astype(BF16)
    gv_o[0] = z[:, ZGV:ZGV + 256].astype(BF16)
    gr_o[0] = z[:, ZGR:ZGR + 256].astype(BF16)
    zg = _dot(z[:, ZGA:ZGA + 128].astype(BF16), wal_ref[...]) + bal_ref[...]
    glg_o[0] = _log_sigmoid(zg) * (1.0 / GLA_TAU)

    dscale = DIFF_DQK ** -0.5 * LOG2E
    for g in range(2):
        gs = slice(128 * g, 128 * (g + 1))
        qg = (z[:, ZDQ:ZDQ + 256][:, gs] * cd + z[:, ZDQS:ZDQS + 256][:, gs] * sd) * dscale
        kg = z[:, ZDK:ZDK + 256][:, gs] * cd + z[:, ZDKS:ZDKS + 256][:, gs] * sd
        for hl in range(2):
            hh = 2 * g + hl
            for m in range(2):
                lo = 64 * hl + 32 * m
                s0 = (2 * hh + m) * HEAD_SLAB
                dq_ref[0, :, s0:s0 + HEAD_SLAB] = jnp.where((lane >= lo) & (lane < lo + 32), qg, 0.0).astype(BF16)
            dk_ref[0, :, HEAD_SLAB * hh:HEAD_SLAB * (hh + 1)] = jnp.where(
                (lane >= 64 * hl) & (lane < 64 * hl + 64), kg, 0.0).astype(BF16)


def _proj(x, mod, tab, lw):
    b, t, _ = x.shape
    tm = KV_CHUNK
    consts = [lw["g_mix_pre"], lw["w_ext"], None, lw["g_q_lat"], lw["wq"], lw["wqs"], lw["g_kv_lat"], lw["wk"], lw["wvt"],
              lw["wgt"], lw["gate_bias_row"], lw["gate_bias_col"], lw["walpha"], lw["balpha"], lw["wdvt"]]
    in_specs = [pl.BlockSpec((1, tm, D), lambda bi, i: (bi, i, 0)), pl.BlockSpec((1, 8, D), lambda bi, i: (bi, 0, 0))]
    args = [x, mod]
    for cst in consts:
        if cst is None:
            in_specs.append(pl.BlockSpec((tm, 512), lambda bi, i: (i, 0)))
            args.append(tab)
        else:
            in_specs.append(_const_spec(cst.shape))
            args.append(cst)
    out_specs = [pl.BlockSpec((1, tm, w), lambda bi, i: (bi, i, 0)) for _, w, _ in _PROJ_OUT]
    out_shape = [jax.ShapeDtypeStruct((b, t, w), dt) for _, w, dt in _PROJ_OUT]
    out_specs.append(pl.BlockSpec((1, 16, tm), lambda bi, i: (bi, 0, i)))
    out_shape.append(jax.ShapeDtypeStruct((b, 16, t), F32))
    for _ in range(2):
        out_specs.append(pl.BlockSpec((1, 1, N_HEADS * VT_ROWS, tm), lambda bi, i: (bi, i, 0, 0)))
        out_shape.append(jax.ShapeDtypeStruct((b, t // tm, N_HEADS * VT_ROWS, tm), BF16))
    outs = pl.pallas_call(
        _proj_kernel, name="proj", grid=(b, t // tm), in_specs=in_specs, out_specs=out_specs, out_shape=out_shape,
        compiler_params=_cparams(("parallel", "arbitrary")),
    )(*args)
    res = {name: o for (name, _, _), o in zip(_PROJ_OUT, outs[:-3])}
    res["grow"], res["mvt"], res["dvt"] = outs[-3:]
    return res


def _flash_kernel(*refs, nmap, has_lat, tk, finish, post):
    if has_lat:
        q_ref, kc_ref, vc_ref, kl_ref, vl_ref, dl_ref, g_ref, o_ref, s_ref = refs
    else:
        q_ref, kc_ref, vc_ref, dl_ref, g_ref, o_ref = refs
    qb = q_ref[0]
    tq = qb.shape[0]
    qt = jnp.concatenate([qb[:, HEAD_SLAB * mm:HEAD_SLAB * (mm + 1)].astype(F32).T for mm in range(nmap)],
                         axis=1).astype(BF16)
    rows = nmap * tq
    sub = tk // KV_CHUNK

    def softmax(s, smax, m):
        m_new = jnp.maximum(m, smax)
        return m_new, jnp.exp2(m - m_new), jnp.exp2(s - m_new).astype(BF16)

    def pv(p, vts):
        out = _dot(vts[0], p[0:KV_CHUNK])
        for c in range(1, len(vts)):
            out = out + _dot(vts[c], p[c * KV_CHUNK:(c + 1) * KV_CHUNK])
        return out

    s_ctx = _dot(kc_ref[0], qt)
    m, _, p = softmax(s_ctx, jnp.max(s_ctx, axis=0, keepdims=True), jnp.full((1, rows), NEG, F32))
    acc = pv(p, [vc_ref[0, 0]])
    if has_lat:
        n = kl_ref.shape[1] // tk

        unroll = min(FLASH_UNROLL, n)

        def scores(j):
            if isinstance(j, int):
                return _dot(kl_ref[0, j * tk:(j + 1) * tk, :], qt)
            off = pl.multiple_of(j * tk, tk)
            return _dot(kl_ref[0, pl.ds(off, tk), :], qt)

        def values(j):
            return [vl_ref[0, j * sub + c] for c in range(sub)]

        def produce(slot, j):
            s = scores(j)
            s_ref[slot] = s
            return jnp.max(s, axis=0, keepdims=True)

        smax0 = produce(0, 0)

        def body(jj, carry):
            m, acc, smax = carry
            j = unroll * jj
            for u in range(unroll):
                smax_next = smax
                if not isinstance(j, int):
                    smax_next = produce((u + 1) % 2, jnp.minimum(j + u + 1, n - 1))
                elif j + u + 1 < n:
                    smax_next = produce((u + 1) % 2, j + u + 1)
                m, alpha, p = softmax(s_ref[u % 2], smax, m)
                acc = alpha * acc + pv(p, values(j + u))
                smax = smax_next
            return m, acc, smax

        if unroll == n:
            m, acc, _ = body(0, (m, acc, smax0))
        else:
            m, acc, _ = lax.fori_loop(0, n // unroll, body, (m, acc, smax0))

    o = acc[0:MLA_V, :] / acc[MLA_V:MLA_V + 1, :]
    if nmap == 2:
        lv = dl_ref[...]
        lam = (jnp.exp(jnp.sum(lv[0:1] * lv[1:2], axis=-1, keepdims=True))
               - jnp.exp(jnp.sum(lv[2:3] * lv[3:4], axis=-1, keepdims=True)) + (1.0 - post))
        o = o[:, :tq] - lam * o[:, tq:]
    if finish:
        ms = jnp.mean(o * o, axis=0, keepdims=True)
        o = o * lax.rsqrt(ms + EPS) * jnp.concatenate([g_ref[0]] * (tq // LANE), axis=1) * post
    o_pad = jnp.concatenate([o, jnp.zeros((HEAD_SLAB - MLA_V, tq), F32)], axis=0)
    o_ref[0] = o_pad.T.astype(BF16)


def _flash(q, kc, vct, kl, vlt, dlam, g_out, *, nmap, finish, post, tq, tk):
    b, t, _ = q.shape
    has_lat = kl is not None
    assert kc.shape[1] == KV_CHUNK and tk % KV_CHUNK == 0
    qw = nmap * HEAD_SLAB
    kspec = lambda n: pl.BlockSpec((1, n, HEAD_SLAB), lambda bi, h, i: (bi, 0, h))
    vspec = lambda n: pl.BlockSpec((1, n // KV_CHUNK, VT_ROWS, KV_CHUNK), lambda bi, h, i: (bi, 0, h, 0))
    in_specs = [pl.BlockSpec((1, tq, qw), lambda bi, h, i: (bi, i, h)), kspec(KV_CHUNK), vspec(KV_CHUNK)]
    args = [q, kc, vct]
    scratch = []
    if has_lat:
        tl = kl.shape[1]
        assert (tl // tk) % min(FLASH_UNROLL, tl // tk) == 0
        in_specs += [kspec(tl), vspec(tl)]
        args += [kl, vlt]
        scratch = [pltpu.VMEM((2, tk, nmap * tq), F32)]
    in_specs += [_const_spec(dlam.shape), pl.BlockSpec((1, MLA_V, LANE), lambda bi, h, i: (h, 0, 0))]
    args += [dlam, g_out]
    return pl.pallas_call(
        functools.partial(_flash_kernel, nmap=nmap, has_lat=has_lat, tk=tk, finish=finish, post=post),
        name="flash_diff" if nmap == 2 else "flash_mla",
        grid=(b, N_HEADS, t // tq), in_specs=in_specs,
        out_specs=pl.BlockSpec((1, tq, HEAD_SLAB), lambda bi, h, i: (bi, i, h)),
        out_shape=jax.ShapeDtypeStruct((b, t, N_HEADS * HEAD_SLAB), BF16),
        scratch_shapes=scratch,
        compiler_params=_cparams(("parallel", "parallel", "arbitrary")),
    )(*args)


def _head_of(shape, axis, width):
    return (lax.broadcasted_iota(I32, shape, axis) % (N_HEADS * width)) // width


def _mlstm_dir(d, first, last, x, xprev, xnext, v, gcol, grow, wc, bcv, c_ref, m_ref):
    L = x.shape[0]
    row = lax.broadcasted_iota(I32, x.shape, 0)
    pr = jnp.where(first, 0.0, xprev[7:8, :])
    nx = jnp.where(last, 0.0, xnext[0:1, :])
    xm = jnp.where(row == 0, pr, pltpu.roll(x, 1, 0))
    xp = jnp.where(row == L - 1, nx, pltpu.roll(x, L - 1, 0))
    y = xm * wc[0:1] + x * wc[1:2] + xp * wc[2:3] + bcv
    qk = y * _sigmoid(y)
    q = qk[:, :256]
    k = qk[:, 256:] * ML_DH ** -0.5

    li = lax.broadcasted_iota(I32, (L, L), 0)
    si = lax.broadcasted_iota(I32, (L, L), 1)
    tin = (si <= li) if d == 0 else (si >= li)
    tinb = tin.astype(BF16)
    bcol = _dot_sel(tinb, gcol)
    brow = sum(_dot_nt(t, tinb) for t in _split_bf16(grow, 3))
    m0e = m_ref[0, d]
    cb = c_ref[0, d]
    hm256 = _head_of((L, 256), 1, ML_DH)
    hm512 = _head_of((L, 512), 1, ML_DH)
    e_idx = L - 1 if d == 0 else 0

    d_blk, inter_blk = [], []
    for hh in range(N_HEADS):
        c = 4 * d + hh
        bc = bcol[:, 8 + c:9 + c]
        d_blk.append(jnp.where(tin, bc - brow[8 + c:9 + c, :] + grow[c:c + 1, :], NEG))
        inter_blk.append(bc + m0e[0:1, 64 * hh:64 * hh + 1])
    d_st = jnp.concatenate(d_blk, axis=0)
    inter_st = jnp.concatenate(inter_blk, axis=0)
    mt = jnp.maximum(inter_st, jnp.max(d_st, axis=-1, keepdims=True))
    q_st = jnp.concatenate([jnp.where(hm256 == hh, q, 0.0) for hh in range(N_HEADS)], axis=0).astype(BF16)
    s_st = (jnp.exp(d_st - mt) * _dot_nt(q_st, k.astype(BF16))).astype(BF16)
    vext = jnp.concatenate([v, jnp.ones((L, 256), BF16)], axis=1)
    r = _dot(s_st, vext)
    aint = jnp.exp(inter_st - mt)
    p = _dot(q.astype(BF16), cb.astype(BF16))
    tot = jnp.zeros((L, 512), F32)
    mte = jnp.zeros((L, 256), F32)
    for hh in range(N_HEADS):
        rs = slice(hh * L, (hh + 1) * L)
        tot = jnp.where(hm512 == hh, r[rs] + aint[rs] * p, tot)
        mte = jnp.where(hm256 == hh, mt[rs], mte)
    hout = tot[:, :256] / jnp.maximum(jnp.abs(tot[:, 256:]), jnp.exp(-mte))

    wexp = jnp.zeros((L, 256), F32)
    arow = jnp.zeros((1, 512), F32)
    grw = jnp.zeros((1, 512), F32)
    mnew = jnp.zeros((1, 256), F32)
    hr512 = _head_of((1, 512), 1, ML_DH)
    hr256 = _head_of((1, 256), 1, ML_DH)
    for hh in range(N_HEADS):
        c = 4 * d + hh
        bc = bcol[:, 8 + c:9 + c]
        be = bc[e_idx:e_idx + 1, :]
        wl = be - bc + gcol[:, c:c + 1]
        mloc = jnp.max(wl, axis=0, keepdims=True)
        m0h = m0e[0:1, 64 * hh:64 * hh + 1]
        mn = jnp.maximum(be + m0h, mloc)
        wexp = jnp.where(hm256 == hh, jnp.exp(wl - mloc), wexp)
        arow = jnp.where(hr512 == hh, jnp.exp(be + m0h - mn), arow)
        grw = jnp.where(hr512 == hh, jnp.exp(mloc - mn), grw)
        mnew = jnp.where(hr256 == hh, mn, mnew)
    cl = _dot_tn((k * wexp).astype(BF16), vext)
    bd = lax.broadcasted_iota(I32, (256, 512), 0) // ML_DH == _head_of((256, 512), 1, ML_DH)
    c_ref[0, d] = arow * cb + jnp.where(bd, grw * cl, 0.0)
    m_ref[0, d] = jnp.broadcast_to(mnew, (8, 256))
    return hout


def _mlstm_kernel(xf, xfp, xfn, xb, xbp, xbn, vf, vb, gcf, gcb, grf, grb, wc_ref, bc_ref, c0_ref, m0_ref,
                  hf_ref, hb_ref, c_ref, m_ref):
    i = pl.program_id(1)
    n = pl.num_programs(1)

    @pl.when(i == 0)
    def _():
        c_ref[...] = c0_ref[...]
        m_ref[...] = m0_ref[...]

    wc = wc_ref[...]
    bcv = bc_ref[...]
    hf_ref[0] = _mlstm_dir(0, i == 0, i == n - 1, xf[0], xfp[0], xfn[0], vf[0], gcf[0], grf[0], wc, bcv, c_ref, m_ref)
    hb_ref[0] = _mlstm_dir(1, i == n - 1, i == 0, xb[0], xbp[0], xbn[0], vb[0], gcb[0], grb[0], wc, bcv, c_ref, m_ref)


def _mlstm(pr, w_conv, b_conv, c0, m0):
    x, v, gc, gr = pr["lqk"], pr["lv"], pr["gc"], pr["grow"]
    b, t, _ = x.shape
    L = ML_CHUNK
    n = t // L
    r8 = L // 8
    last8 = t // 8 - 1

    def fw(bi, i):
        return (bi, i, 0)

    def bw(bi, i):
        return (bi, n - 1 - i, 0)

    def halo(ix, shift):
        def f(bi, i):
            blk = ix(bi, i)[1]
            return (bi, jnp.clip(blk * r8 + shift, 0, last8), 0)
        return f

    main = lambda w, ix: pl.BlockSpec((1, L, w), ix)
    in_specs = [main(512, fw), pl.BlockSpec((1, 8, 512), halo(fw, -1)), pl.BlockSpec((1, 8, 512), halo(fw, r8)),
                main(512, bw), pl.BlockSpec((1, 8, 512), halo(bw, -1)), pl.BlockSpec((1, 8, 512), halo(bw, r8)),
                main(256, fw), main(256, bw), main(128, fw), main(128, bw),
                pl.BlockSpec((1, 16, L), lambda bi, i: (bi, 0, i)), pl.BlockSpec((1, 16, L), lambda bi, i: (bi, 0, n - 1 - i)),
                _const_spec(w_conv.shape), _const_spec(b_conv.shape),
                pl.BlockSpec((1, 2, 256, 512), lambda bi, i: (bi, 0, 0, 0)), pl.BlockSpec((1, 2, 8, 256), lambda bi, i: (bi, 0, 0, 0))]
    out_specs = [main(256, fw), main(256, bw),
                 pl.BlockSpec((1, 2, 256, 512), lambda bi, i: (bi, 0, 0, 0)), pl.BlockSpec((1, 2, 8, 256), lambda bi, i: (bi, 0, 0, 0))]
    out_shape = [jax.ShapeDtypeStruct((b, t, 256), F32), jax.ShapeDtypeStruct((b, t, 256), F32),
                 jax.ShapeDtypeStruct(c0.shape, F32), jax.ShapeDtypeStruct(m0.shape, F32)]
    return pl.pallas_call(
        _mlstm_kernel, name="mlstm", grid=(b, n), in_specs=in_specs, out_specs=out_specs, out_shape=out_shape,
        compiler_params=_cparams(("parallel", "arbitrary")),
    )(x, x, x, x, x, x, v, v, gc, gc, gr, gr, w_conv, b_conv, c0, m0)


def _gla_chunk(d, q, k, v, lg, sb):
    L = q.shape[0]
    li = lax.broadcasted_iota(I32, (L, L), 0)
    si = lax.broadcasted_iota(I32, (L, L), 1)
    tin = (si <= li) if d == 0 else (si >= li)
    lgd = lg[:, 128 * d:128 * (d + 1)]
    gcum = _dot_sel(tin.astype(BF16), lgd)
    e_idx = L - 1 if d == 0 else 0
    gend = gcum[e_idx:e_idx + 1, :]
    qf, kf = q.astype(F32), k.astype(F32)
    q_dec = qf * jnp.exp(gcum)
    k_dec = (kf * jnp.exp(-gcum)).astype(BF16)
    k_end = (kf * jnp.exp(gend - gcum)).astype(BF16)
    hm128 = _head_of((L, 128), 1, GLA_DK)
    hm256 = _head_of((L, 256), 1, GLA_DV)
    q_st = jnp.concatenate([jnp.where(hm128 == hh, q_dec, 0.0) for hh in range(N_HEADS)], axis=0).astype(BF16)
    att = _dot_nt(q_st, k_dec)
    tin4 = jnp.concatenate([tin] * N_HEADS, axis=0)
    o_st = _dot(jnp.where(tin4, att, 0.0).astype(BF16), v)
    o = _dot_nt(q_dec.astype(BF16), sb.astype(BF16))
    for hh in range(N_HEADS):
        o = o + jnp.where(hm256 == hh, o_st[hh * L:(hh + 1) * L], 0.0)
    bd = lax.broadcasted_iota(I32, (256, 128), 0) // GLA_DV == _head_of((256, 128), 1, GLA_DK)
    return o, jnp.exp(gend) * sb + jnp.where(bd, _dot_tn(v, k_end), 0.0)


def _gla_kernel(qf, kf, vf, lf, qb, kb, vb, lb, s0_ref, of_ref, ob_ref, s_ref):
    i = pl.program_id(1)

    @pl.when(i == 0)
    def _():
        s_ref[...] = s0_ref[...]

    L = GLA_CHUNK
    nsub = qf.shape[1] // L
    for d, (q, k, v, lg, o_ref) in enumerate(((qf, kf, vf, lf, of_ref), (qb, kb, vb, lb, ob_ref))):
        sb = s_ref[0, d]
        for c in (range(nsub) if d == 0 else reversed(range(nsub))):
            sl = slice(c * L, (c + 1) * L)
            o_ref[0, sl, :], sb = _gla_chunk(d, q[0, sl, :], k[0, sl, :], v[0, sl, :], lg[0, sl, :], sb)
        s_ref[0, d] = sb


def _gla(pr, s0):
    q, k, v, lg = pr["gq"], pr["gk"], pr["gv"], pr["glg"]
    b, t, _ = q.shape
    L = min(GLA_BLOCK, t)
    n = t // L
    fw = lambda bi, i: (bi, i, 0)
    bw = lambda bi, i: (bi, n - 1 - i, 0)
    blk = lambda w, ix: pl.BlockSpec((1, L, w), ix)
    st_spec = pl.BlockSpec((1, 2, 256, 128), lambda bi, i: (bi, 0, 0, 0))
    return pl.pallas_call(
        _gla_kernel, name="gla", grid=(b, n),
        in_specs=[blk(128, fw), blk(128, fw), blk(256, fw), blk(256, fw),
                  blk(128, bw), blk(128, bw), blk(256, bw), blk(256, bw), st_spec],
        out_specs=[blk(256, fw), blk(256, bw), st_spec],
        out_shape=[jax.ShapeDtypeStruct((b, t, 256), F32), jax.ShapeDtypeStruct((b, t, 256), F32),
                   jax.ShapeDtypeStruct(s0.shape, F32)],
        compiler_params=_cparams(("parallel", "arbitrary")),
    )(q, k, v, lg, q, k, v, lg, s0)


def _head_rms_expanded(x, width):
    n = x.shape[1]
    bd = (lax.broadcasted_iota(I32, (n, n), 0) // width == lax.broadcasted_iota(I32, (n, n), 1) // width).astype(BF16)
    return sum(_dot(t, bd) for t in _split_bf16(x * x, 2)) * (1.0 / width)


def _merge_kernel(x_ref, mod_ref, ya_ref, hf_ref, hb_ref, lo_ref, gf_ref, gb_ref, gr_ref, yd_ref,
                  gpre_ref, wg_ref, bg_ref, wbr_ref, wo_ref, gpost_ref, gffn_ref, wr_ref, wrt_ref, gml_ref, ggla_ref,
                  xm_ref, h2_ref, aff_ref, afft_ref):
    x = x_ref[0]
    tm = x.shape[0]
    mod = mod_ref[0]
    hb = (_rms(x) * gpre_ref[...] * (1.0 + mod[1:2]) + mod[0:1]).astype(BF16)

    hs = hf_ref[0] + hb_ref[0]
    y_ml = _sigmoid(lo_ref[0].astype(F32)) * (hs * lax.rsqrt(_head_rms_expanded(hs, ML_DH) + EPS) * gml_ref[...])
    gs = gf_ref[0] + gb_ref[0]
    rr = gr_ref[0].astype(F32)
    y_gla = rr * _sigmoid(rr) * (gs * lax.rsqrt(_head_rms_expanded(gs, GLA_DV) + EPS) * ggla_ref[...])

    branches = ((ya_ref[0], 0, 512), (y_ml.astype(BF16), 512, 256), (y_gla.astype(BF16), 768, 256), (yd_ref[0], 1024, 512))
    mix = jnp.zeros((tm, D), F32)
    for nb, (yb, r0, rw) in enumerate(branches):
        gate = _sigmoid(_dot(hb, wg_ref[:, nb * D:(nb + 1) * D]) + bg_ref[:, nb * D:(nb + 1) * D])
        mix = mix + gate * _dot(yb, wbr_ref[r0:r0 + rw, :])
    y = _dot(mix.astype(BF16), wo_ref[...])
    xm = x + mod[2:3] * (_rms(y) * gpost_ref[...])
    xm_ref[0] = xm

    h2 = (_rms(xm) * gffn_ref[...] * (1.0 + mod[4:5]) + mod[3:4]).astype(BF16)
    h2_ref[0] = h2
    lane = lax.broadcasted_iota(I32, (tm, LANE), 1)
    lg = jnp.where(lane < N_EXPERTS, _dot(h2, wr_ref[...]), NEG)
    e = jnp.exp(lg - jnp.max(lg, axis=-1, keepdims=True))
    aff_ref[0] = (e / jnp.sum(e, axis=-1, keepdims=True))[:, :N_EXPERTS]
    lt = _dot_nt(wrt_ref[...], h2)
    et = jnp.exp(lt - jnp.max(lt, axis=0, keepdims=True))
    afft_ref[0] = et / jnp.sum(et, axis=0, keepdims=True)


def _merge(x, mod, ya, ml, lo, gl, gr, yd, lw, tm):
    b, t, _ = x.shape
    tok = lambda w: pl.BlockSpec((1, tm, w), lambda bi, i: (bi, i, 0))
    consts = [lw["g_mix_pre"], lw["w_gate"], lw["b_gate"], lw["wbr"], lw["w_out"], lw["g_mix_post"], lw["g_ffn_pre"],
              lw["w_router"], lw["w_router_t"], lw["g_mlstm_out"], lw["g_gla_out"]]
    in_specs = [tok(D), pl.BlockSpec((1, 8, D), lambda bi, i: (bi, 0, 0)), tok(512), tok(256), tok(256), tok(256),
                tok(256), tok(256), tok(256), tok(512)] + [_const_spec(c.shape) for c in consts]
    out_specs = [tok(D), tok(D), tok(N_EXPERTS), pl.BlockSpec((1, N_EXPERTS, tm), lambda bi, i: (bi, 0, i))]
    out_shape = [jax.ShapeDtypeStruct((b, t, D), F32), jax.ShapeDtypeStruct((b, t, D), BF16),
                 jax.ShapeDtypeStruct((b, t, N_EXPERTS), F32), jax.ShapeDtypeStruct((b, N_EXPERTS, t), F32)]
    return pl.pallas_call(
        _merge_kernel, name="merge", grid=(b, t // tm), in_specs=in_specs, out_specs=out_specs, out_shape=out_shape,
        compiler_params=_cparams(("parallel", "arbitrary")),
    )(x, mod, ya, ml[0], ml[1], lo, gl[0], gl[1], gr, yd, *consts)


def _topk_kernel(a_ref, pos_ref, s0_ref, *, cap):
    nblk = a_ref.shape[1]
    bits = pltpu.bitcast(a_ref[0], I32)

    def bisect(i, thr):
        cand = thr | (1 << (30 - i))
        cnt = jnp.sum((bits >= cand).astype(I32), axis=(0, 2), keepdims=True)
        return jnp.where(cnt >= cap, cand, thr)

    thr3 = lax.fori_loop(0, 31, bisect, jnp.zeros((1, N_EXPERTS, 1), I32))
    need3 = cap - jnp.sum((bits > thr3).astype(I32), axis=(0, 2), keepdims=True)
    thr, need = thr3[0], need3[0].astype(F32)
    upper = (lax.broadcasted_iota(I32, (TOK_BLK, TOK_BLK), 0) <= lax.broadcasted_iota(I32, (TOK_BLK, TOK_BLK), 1)).astype(BF16)

    def blk(j, carry):
        c_eq, c_sel = carry
        bj = pltpu.bitcast(a_ref[0, j], I32)
        gt, eq = bj > thr, bj == thr
        cum_eq = _dot(eq.astype(BF16), upper) + c_eq
        sel = gt | (eq & (cum_eq <= need))
        cum_sel = _dot(sel.astype(BF16), upper) + c_sel
        pos_ref[0, j] = jnp.where(sel, cum_sel - 1.0, -1.0).astype(I32)
        s0_ref[0, j] = jnp.broadcast_to(c_sel, (N_EXPERTS, LANE)).astype(I32)
        return cum_eq[:, TOK_BLK - 1:TOK_BLK], cum_sel[:, TOK_BLK - 1:TOK_BLK]

    zero = jnp.zeros((N_EXPERTS, 1), F32)
    lax.fori_loop(0, nblk, blk, (zero, zero))


def _topk(aff_t, cap):
    b, _, t = aff_t.shape
    nblk = t // TOK_BLK
    a4 = aff_t.reshape(b, N_EXPERTS, nblk, TOK_BLK).transpose(0, 2, 1, 3)
    spec = lambda w: pl.BlockSpec((1, nblk, N_EXPERTS, w), lambda bi: (bi, 0, 0, 0))
    return pl.pallas_call(
        functools.partial(_topk_kernel, cap=cap), name="topk", grid=(b,),
        in_specs=[spec(TOK_BLK)], out_specs=[spec(TOK_BLK), spec(LANE)],
        out_shape=[jax.ShapeDtypeStruct((b, nblk, N_EXPERTS, TOK_BLK), I32),
                   jax.ShapeDtypeStruct((b, nblk, N_EXPERTS, LANE), I32)],
        compiler_params=_cparams(("parallel",)),
    )(a4)


def _moe_kernel(s0_ref, pos_ref, h_ref, wg_ref, wu_ref, wd_ref, ys_ref, xs_ref, *, nblk, nsub, capp):
    bi, e, tb = pl.program_id(0), pl.program_id(1), pl.program_id(2)

    @pl.when(tb == 0)
    def _():
        xs_ref[...] = jnp.zeros(xs_ref.shape, BF16)

    base = (bi * N_EXPERTS + e) * (nblk + 1) + tb * nsub
    for sb in range(nsub):
        s0 = s0_ref[base + sb]
        s1 = s0_ref[base + sb + 1]
        a0 = pl.multiple_of((s0 // 16) * 16, 16)

        def gather(win, sb=sb, a0=a0):
            prow = pos_ref[0, sb, pl.ds(e, 1), :]
            slot = lax.broadcasted_iota(I32, (win, TOK_BLK), 0) + a0
            rows = _dot((slot == prow).astype(BF16), h_ref[0, sb * TOK_BLK:(sb + 1) * TOK_BLK, :])
            xs_ref[pl.ds(a0, win), :] = xs_ref[pl.ds(a0, win), :] + rows.astype(BF16)

        pl.when((s1 > s0) & (s1 - a0 <= GATHER_WIN_SMALL))(functools.partial(gather, GATHER_WIN_SMALL))
        pl.when(s1 - a0 > GATHER_WIN_SMALL)(functools.partial(gather, GATHER_WIN))

    @pl.when(tb == pl.num_programs(2) - 1)
    def _():
        def chunk(c, carry):
            off = pl.multiple_of(c * TOK_BLK, TOK_BLK)
            xc = xs_ref[pl.ds(off, TOK_BLK), :]
            hg = _dot(xc, wg_ref[0])
            hid = (hg * _sigmoid(hg) * _dot(xc, wu_ref[0])).astype(BF16)
            ys_ref[0, 0, pl.ds(off, TOK_BLK), :] = _dot(hid, wd_ref[0]).astype(BF16)
            return carry
        lax.fori_loop(0, capp // TOK_BLK, chunk, 0)


def _moe(s0_flat, pos4, h2, lw, capp):
    b, t, _ = h2.shape
    nblk = t // TOK_BLK
    tok = min(MOE_TOK, t)
    nsub = tok // TOK_BLK
    grid_spec = pltpu.PrefetchScalarGridSpec(
        num_scalar_prefetch=1, grid=(b, N_EXPERTS, t // tok),
        in_specs=[pl.BlockSpec((1, nsub, N_EXPERTS, TOK_BLK), lambda bi, e, tb, s: (bi, tb, 0, 0)),
                  pl.BlockSpec((1, tok, D), lambda bi, e, tb, s: (bi, tb, 0)),
                  pl.BlockSpec((1, D, EXPERT_FF), lambda bi, e, tb, s: (e, 0, 0)),
                  pl.BlockSpec((1, D, EXPERT_FF), lambda bi, e, tb, s: (e, 0, 0)),
                  pl.BlockSpec((1, EXPERT_FF, D), lambda bi, e, tb, s: (e, 0, 0))],
        out_specs=pl.BlockSpec((1, 1, capp, D), lambda bi, e, tb, s: (bi, e, 0, 0)),
        scratch_shapes=[pltpu.VMEM((capp + GATHER_WIN, D), BF16)])
    return pl.pallas_call(
        functools.partial(_moe_kernel, nblk=nblk, nsub=nsub, capp=capp), name="moe", grid_spec=grid_spec,
        out_shape=jax.ShapeDtypeStruct((b, N_EXPERTS, capp, D), BF16),
        compiler_params=_cparams(("arbitrary", "arbitrary", "arbitrary")),
    )(s0_flat, pos4, h2, lw["w_e_gate"], lw["w_e_up"], lw["w_e_down"])


def _combine_kernel(s0_ref, *refs, nblk, nb, sblk):
    ys_refs = refs[:2 * N_EXPERTS]
    pos_ref, aff_ref, xm_ref, mod_ref, g_ref, o_ref = refs[2 * N_EXPERTS:]
    bi, tb = pl.program_id(0), pl.program_id(1)
    pos = pos_ref[0]
    aff = aff_ref[0]
    lane = lax.broadcasted_iota(I32, (TOK_BLK, 2 * sblk), 1)
    acc = jnp.zeros((TOK_BLK, D), F32)
    for e in range(N_EXPERTS):
        s0 = s0_ref[(bi * N_EXPERTS + e) * (nblk + 1) + tb]
        blk0 = jnp.minimum(s0 // sblk, nb - 1)
        rel = pos[:, e:e + 1] - blk0 * sblk
        ysw = jnp.concatenate([ys_refs[2 * e][0, 0], ys_refs[2 * e + 1][0, 0]], axis=0)
        acc = acc + aff[:, e:e + 1] * _dot((lane == rel).astype(BF16), ysw)
    mod = mod_ref[0]
    o_ref[0] = xm_ref[0] + mod[5:6] * (_rms(acc) * g_ref[...])


def _combine(s0_flat, ys, pos_t, aff, xm, mod, g_post, sblk):
    b, t, _ = xm.shape
    nblk = t // TOK_BLK
    nb = ys.shape[2] // sblk

    def ys_spec(e, k):
        def ix(bi, tb, s):
            blk0 = jnp.minimum(s[(bi * N_EXPERTS + e) * (nblk + 1) + tb] // sblk, nb - 1)
            return (bi, e, jnp.minimum(blk0 + k, nb - 1), 0)
        return pl.BlockSpec((1, 1, sblk, D), ix)

    tok = lambda w: pl.BlockSpec((1, TOK_BLK, w), lambda bi, tb, s: (bi, tb, 0))
    in_specs = [ys_spec(e, k) for e in range(N_EXPERTS) for k in range(2)]
    in_specs += [tok(N_EXPERTS), tok(N_EXPERTS), tok(D), pl.BlockSpec((1, 8, D), lambda bi, tb, s: (bi, 0, 0)),
                 pl.BlockSpec((1, D), lambda bi, tb, s: (0, 0))]
    grid_spec = pltpu.PrefetchScalarGridSpec(num_scalar_prefetch=1, grid=(b, nblk), in_specs=in_specs, out_specs=tok(D))
    return pl.pallas_call(
        functools.partial(_combine_kernel, nblk=nblk, nb=nb, sblk=sblk), name="combine", grid_spec=grid_spec,
        out_shape=jax.ShapeDtypeStruct((b, t, D), F32),
        compiler_params=_cparams(("arbitrary", "arbitrary")),
    )(s0_flat, *([ys] * (2 * N_EXPERTS)), pos_t, aff, xm, mod, g_post)


def _rope_table(t):
    nf = ROPE_DIM // 4
    pos = jnp.arange(t)
    inv = ROPE_BASE ** (-jnp.arange(nf, dtype=F32) / nf)
    ang = jnp.stack([pos // GRID_W, pos % GRID_W], axis=-1).astype(F32)[..., None] * inv
    cos, sin = jnp.cos(ang), jnp.sin(ang)
    c32 = jnp.stack([cos, cos], axis=2).reshape(t, ROPE_DIM)
    s32 = jnp.stack([-sin, sin], axis=2).reshape(t, ROPE_DIM)
    one, zero = jnp.ones((t, 64), F32), jnp.zeros((t, 32), F32)
    ct = jnp.concatenate([one, c32, zero], axis=1)
    st = jnp.concatenate([0.0 * one, s32, zero], axis=1)
    return jnp.concatenate([ct, st, jnp.tile(c32, (1, 4)), jnp.tile(s32, (1, 4))], axis=1)


def _identity_table(t):
    one, zero = jnp.ones((t, 128), F32), jnp.zeros((t, 128), F32)
    ct = jnp.concatenate([jnp.ones((t, 96), F32), jnp.zeros((t, 32), F32)], axis=1)
    return jnp.concatenate([ct, zero, one, zero], axis=1)


def _layer_weights(i, p):
    lw = {}
    row = lambda a: a.reshape(1, -1)
    for name in ("g_mix_pre", "g_mix_post", "g_ffn_pre", "g_ffn_post", "g_q_lat", "g_kv_lat", "g_mlstm_out", "g_gla_out"):
        lw[name] = row(p[name][i])
    lw["w_ext"] = _gather_cols(p["w_in"][i], _WIN_IDX).astype(BF16)
    lw["wgt"] = p["w_in"][i][:, 416 + 1024:416 + 1040].T.astype(BF16)
    gb = jnp.concatenate([p["b_igate"][i].reshape(-1), p["b_fgate"][i].reshape(-1)])
    lw["gate_bias_row"] = jnp.pad(gb, (0, LANE - 16)).reshape(1, LANE)
    lw["gate_bias_col"] = gb.reshape(16, 1)
    wuq = p["w_uq"][i]
    qi = -np.ones((512,), np.int64)
    qsi = -np.ones((512,), np.int64)
    for h in range(N_HEADS):
        qi[128 * h:128 * h + 96] = 96 * h + np.arange(96)
        qsi[128 * h + 64:128 * h + 96] = 96 * h + 64 + (np.arange(32) ^ 8)
    lw["wq"] = _gather_cols(wuq, qi).astype(BF16)
    lw["wqs"] = _gather_cols(wuq, qsi).astype(BF16)
    ki = -np.ones((512,), np.int64)
    for h in range(N_HEADS):
        ki[128 * h:128 * h + 64] = 128 * h + np.arange(64)
    lw["wk"] = _gather_cols(p["w_ukv"][i], ki).astype(BF16)
    lw["wvt"] = _vt_rows(p["w_ukv"][i].reshape(MLA_KV_LORA, N_HEADS, 128)[:, :, 64:].reshape(MLA_KV_LORA, 256)).astype(BF16)
    lw["wdvt"] = _vt_rows(p["w_in"][i][:, 2768:3024]).astype(BF16)
    wa = p["w_alpha2"][i]
    wal = jnp.zeros((LANE, 256), F32).at[0:16, 0:128].set(wa[0]).at[16:32, 128:256].set(wa[1])
    lw["walpha"] = wal.astype(BF16)
    lw["balpha"] = p["b_alpha"][i].reshape(1, 256)
    lw["w_conv"] = p["w_conv"][i]
    lw["b_conv"] = row(p["b_conv"][i])
    lw["dlam"] = p["diff_lambda"][i]
    lw["g_diff"] = jnp.broadcast_to(p["g_diff_out"][i].reshape(N_HEADS, DIFF_DV, 1), (N_HEADS, DIFF_DV, LANE))
    wb = p["w_branch"][i]
    lw["wbr"] = jnp.concatenate([_pad_heads_rows(wb[0], 64), wb[1], wb[2], _pad_heads_rows(wb[3], 64)], axis=0).astype(BF16)
    lw["w_gate"] = p["w_gate"][i].astype(BF16)
    lw["b_gate"] = row(p["b_gate"][i])
    lw["w_out"] = p["w_out"][i].astype(BF16)
    lw["w_router"] = jnp.pad(p["w_router"][i], ((0, 0), (0, LANE - N_EXPERTS))).astype(BF16)
    lw["w_router_t"] = p["w_router"][i].T.astype(BF16)
    lw["w_e_gate"] = p["w_e_gate"][i].astype(BF16)
    lw["w_e_up"] = p["w_e_up"][i].astype(BF16)
    lw["w_e_down"] = p["w_e_down"][i].astype(BF16)
    return lw


def _ffn(xm, h2, aff, aff_t, mod, lw):
    b, t, _ = xm.shape
    nblk = t // TOK_BLK
    cap = EC_CAPACITY * t // N_EXPERTS
    capp = -(-cap // TOK_BLK) * TOK_BLK
    pos4, s04 = _topk(aff_t, cap)
    s0_be = jnp.concatenate([s04[..., 0].transpose(0, 2, 1), jnp.full((b, N_EXPERTS, 1), cap, I32)], axis=-1)
    s0_flat = s0_be.reshape(-1)
    ys = _moe(s0_flat, pos4, h2, lw, capp)
    pos_t = pos4.transpose(0, 1, 3, 2).reshape(b, t, N_EXPERTS)
    first = jnp.minimum(s0_be[..., :-1] // SLOT_BLK, capp // SLOT_BLK - 1)
    fits = jnp.all(s0_be[..., 1:] <= (first + 2) * SLOT_BLK)
    args = (s0_flat, ys, pos_t, aff, xm, mod, lw["g_ffn_post"])
    return lax.cond(fits, functools.partial(_combine, sblk=SLOT_BLK), functools.partial(_combine, sblk=TOK_BLK), *args)


def _hybrid_layer(i, x_c, x_l, c8, need_ctx, p):
    lw = _layer_weights(i, p)
    b, t, _ = x_l.shape
    tc = x_c.shape[1]
    lam_init = 0.8 - 0.6 * math.exp(-0.3 * i)
    mod8 = _ada(c8, p["w_ada"][i], p["b_ada"][i])
    pad = lambda m: jnp.pad(m.reshape(b, 6, D), ((0, 0), (0, 2), (0, 0)))
    mod_l = pad(mod8[:b])
    mod_c = pad(jnp.broadcast_to(mod8[b:b + 1], (b, 6 * D)))

    pc = _proj(x_c, mod_c, _identity_table(tc), lw)
    pt = _proj(x_l, mod_l, _rope_table(t), lw)

    zc = jnp.zeros((b, 2, 256, 512), F32)
    zm = jnp.full((b, 2, 8, 256), NEG, F32)
    zs = jnp.zeros((b, 2, 256, 128), F32)
    hf_c, hb_c, c_fin, m_fin = _mlstm(pc, lw["w_conv"], lw["b_conv"], zc, zm)
    hf_l, hb_l, _, _ = _mlstm(pt, lw["w_conv"], lw["b_conv"], c_fin, m_fin)
    gf_c, gb_c, s_fin = _gla(pc, zs)
    gf_l, gb_l, _ = _gla(pt, s_fin)

    one_g = jnp.ones((N_HEADS, MLA_V, LANE), F32)
    fl = functools.partial(_flash, tk=FLASH_KEYS)
    ya_l = fl(pt["mq"], pc["mk"], pc["mvt"], pt["mk"], pt["mvt"], lw["dlam"], one_g, nmap=1, finish=False, post=1.0,
              tq=min(FLASH_ROWS, t))
    yd_l = fl(pt["dq"], pc["dk"], pc["dvt"], pt["dk"], pt["dvt"], lw["dlam"], lw["g_diff"], nmap=2, finish=True,
              post=1.0 - lam_init, tq=min(FLASH_ROWS // 2, t))
    xm, h2, aff, aff_t = _merge(x_l, mod_l, ya_l, (hf_l, hb_l), pt["lo"], (gf_l, gb_l), pt["gr"], yd_l, lw, tm=256)
    x_l = _ffn(xm, h2, aff, aff_t, mod_l, lw)

    if need_ctx:
        ya_c = fl(pc["mq"], pc["mk"], pc["mvt"], None, None, lw["dlam"], one_g, nmap=1, finish=False, post=1.0, tq=tc)
        yd_c = fl(pc["dq"], pc["dk"], pc["dvt"], None, None, lw["dlam"], lw["g_diff"], nmap=2, finish=True,
                  post=1.0 - lam_init, tq=tc)
        xm, h2, aff, aff_t = _merge(x_c, mod_c, ya_c, (hf_c, hb_c), pc["lo"], (gf_c, gb_c), pc["gr"], yd_c, lw, tm=tc)
        x_c = _ffn(xm, h2, aff, aff_t, mod_c, lw)
    return x_c, x_l


def kernel(x, c, ctx, c_ctx, w_ada, b_ada, g_mix_pre, g_mix_post, g_ffn_pre, g_ffn_post, w_in, g_q_lat, w_uq, g_kv_lat, w_ukv, w_conv, b_conv, b_igate, b_fgate, g_mlstm_out, w_alpha2, b_alpha, g_gla_out, diff_lambda, g_diff_out, w_branch, w_gate, b_gate, w_out, w_router, w_e_gate, w_e_up, w_e_down):
    p = dict(w_ada=w_ada, b_ada=b_ada, g_mix_pre=g_mix_pre, g_mix_post=g_mix_post, g_ffn_pre=g_ffn_pre,
             g_ffn_post=g_ffn_post, w_in=w_in, g_q_lat=g_q_lat, w_uq=w_uq, g_kv_lat=g_kv_lat, w_ukv=w_ukv,
             w_conv=w_conv, b_conv=b_conv, b_igate=b_igate, b_fgate=b_fgate, g_mlstm_out=g_mlstm_out,
             w_alpha2=w_alpha2, b_alpha=b_alpha, g_gla_out=g_gla_out, diff_lambda=diff_lambda, g_diff_out=g_diff_out,
             w_branch=w_branch, w_gate=w_gate, b_gate=b_gate, w_out=w_out, w_router=w_router,
             w_e_gate=w_e_gate, w_e_up=w_e_up, w_e_down=w_e_down)
    b = x.shape[0]
    c8 = jnp.concatenate([c, c_ctx[None], jnp.zeros((8 - b - 1, D), F32)], axis=0)
    x_c, x_l = ctx, x
    for i in range(DEPTH):
        x_c, x_l = _hybrid_layer(i, x_c, x_l, c8, i < DEPTH - 1, p)
    return x_l
```

```python
import functools
import math

import numpy as np
import jax
import jax.numpy as jnp
from jax import lax
from jax.experimental import pallas as pl
from jax.experimental.pallas import tpu as pltpu

F32 = jnp.float32
BF16 = jnp.bfloat16
I32 = jnp.int32

D = 1024
DEPTH = 2
GRID_W = 64
N_HEADS = 4
MLA_NOPE, MLA_ROPE, MLA_V = 64, 32, 64
MLA_Q_LORA, MLA_KV_LORA = 256, 128
ML_DH = 64
GLA_DK, GLA_DV, GLA_RANK, GLA_TAU = 32, 64, 16, 16.0
DIFF_DQK, DIFF_DV = 32, 64
ROPE_DIM, ROPE_BASE = 32, 10000.0
N_EXPERTS, EC_CAPACITY, EXPERT_FF = 16, 2, 1408
NEG = -1e30
EPS = 1e-6
LOG2E = 1.4426950408889634

LANE = 128
HEAD_SLAB = 128
TOK_BLK = 256
ML_CHUNK = 128
GLA_CHUNK = 64
GLA_BLOCK = 256
GATHER_WIN = TOK_BLK + 16
GATHER_WIN_SMALL = 64
MOE_TOK = 4096
SLOT_BLK = 128
VMEM_LIMIT = 56 * 1024 * 1024

ZQ, ZKV, ZKRA, ZKRB, ZMLQK, ZMLV, ZMLO, ZGATE, ZGA = 0, 256, 384, 512, 640, 1152, 1408, 1664, 1792
ZGQ, ZGK, ZGV, ZGR, ZDQ, ZDQS, ZDK, ZDKS, NZ = 1920, 2048, 2176, 2432, 2688, 2944, 3200, 3456, 3712
KV_CHUNK = 256
VT_ROWS = 80
FLASH_ROWS = 512
FLASH_KEYS = 512
FLASH_UNROLL = 16


def _swap32(c):
    return (c // 32) * 32 + ((c % 32) ^ 8)


def _win_index():
    idx = -np.ones((NZ,), np.int64)
    idx[ZQ:ZQ + 256] = np.arange(0, 256)
    idx[ZKV:ZKV + 128] = np.arange(256, 384)
    r = np.arange(32)
    idx[ZKRA + 64:ZKRA + 96] = 384 + r
    idx[ZKRB + 64:ZKRB + 96] = 384 + (r ^ 8)
    ml = 416
    idx[ZMLQK:ZMLQK + 512] = ml + np.arange(512)
    idx[ZMLV:ZMLV + 256] = ml + 512 + np.arange(256)
    idx[ZMLO:ZMLO + 256] = ml + 768 + np.arange(256)
    idx[ZGATE:ZGATE + 16] = ml + 1024 + np.arange(16)
    gl = 1456
    idx[ZGQ:ZGQ + 128] = gl + np.arange(128)
    idx[ZGK:ZGK + 128] = gl + 128 + np.arange(128)
    idx[ZGV:ZGV + 256] = gl + 256 + np.arange(256)
    idx[ZGR:ZGR + 256] = gl + 512 + np.arange(256)
    idx[ZGA:ZGA + 32] = gl + 768 + np.arange(32)
    df = 2256
    c = np.arange(256)
    idx[ZDQ:ZDQ + 256] = df + c
    idx[ZDQS:ZDQS + 256] = df + _swap32(c)
    idx[ZDK:ZDK + 256] = df + 256 + c
    idx[ZDKS:ZDKS + 256] = df + 256 + _swap32(c)
    return idx


def _vt_rows(w_cols):
    n = w_cols.shape[0]
    w4 = w_cols.T.reshape(N_HEADS, 64, n)
    return jnp.pad(w4, ((0, 0), (0, VT_ROWS - 64), (0, 0))).reshape(N_HEADS * VT_ROWS, n)


_WIN_IDX = _win_index()


def _gather_cols(w, idx):
    safe = np.maximum(idx, 0)
    return jnp.where(jnp.asarray(idx >= 0)[None, :], w[:, safe], 0.0)


def _pad_heads_rows(w, width):
    n = w.shape[1]
    w4 = w.reshape(N_HEADS, width, n)
    return jnp.pad(w4, ((0, 0), (0, HEAD_SLAB - width), (0, 0))).reshape(N_HEADS * HEAD_SLAB, n)


def _cparams(sem):
    return pltpu.CompilerParams(dimension_semantics=sem, vmem_limit_bytes=VMEM_LIMIT)


def _rms(x):
    return x * lax.rsqrt(jnp.mean(x * x, axis=-1, keepdims=True) + EPS)


def _sigmoid(x):
    return 1.0 / (1.0 + jnp.exp(-x))


def _log_sigmoid(x):
    return jnp.minimum(x, 0.0) - jnp.log1p(jnp.exp(-jnp.abs(x)))


def _dot(a, b, precision=None):
    return jnp.dot(a, b, preferred_element_type=F32, precision=precision)


def _dot_nt(a, b, precision=None):
    return lax.dot_general(a, b, (((1,), (1,)), ((), ())), preferred_element_type=F32, precision=precision)


def _dot_tn(a, b, precision=None):
    return lax.dot_general(a, b, (((0,), (0,)), ((), ())), preferred_element_type=F32, precision=precision)


def _split_bf16(x, parts):
    out, r = [], x
    for _ in range(parts):
        t = r.astype(BF16)
        out.append(t)
        r = r - t.astype(F32)
    return out


def _dot_sel(sel, x, parts=3):
    return sum(_dot(sel, t) for t in _split_bf16(x, parts))


def _const_spec(shape):
    nd = len(shape)
    return pl.BlockSpec(shape, lambda *_: (0,) * nd)


def _ada_kernel(c_ref, w_ref, b_ref, o_ref):
    cv = c_ref[...]
    s = (cv * _sigmoid(cv)).astype(BF16)
    o_ref[...] = _dot(s, w_ref[...].astype(BF16)) + b_ref[...]


def _ada(c8, w_ada, b_ada):
    n, tn = 6 * D, 1024
    return pl.pallas_call(
        _ada_kernel, name="ada", grid=(n // tn,),
        in_specs=[pl.BlockSpec((8, D), lambda j: (0, 0)), pl.BlockSpec((D, tn), lambda j: (0, j)),
                  pl.BlockSpec((1, tn), lambda j: (0, j))],
        out_specs=pl.BlockSpec((8, tn), lambda j: (0, j)),
        out_shape=jax.ShapeDtypeStruct((8, n), F32), compiler_params=_cparams(("arbitrary",)),
    )(c8, w_ada, b_ada.reshape(1, n))


_PROJ_OUT = (
    ("mq", 512, BF16), ("mk", 512, BF16),
    ("lqk", 512, F32), ("lv", 256, BF16), ("lo", 256, BF16), ("gc", 128, F32),
    ("gq", 128, BF16), ("gk", 128, BF16), ("gv", 256, BF16), ("gr", 256, BF16), ("glg", 256, F32),
    ("dq", 1024, BF16), ("dk", 512, BF16),
)


def _proj_kernel(x_ref, mod_ref, g_ref, w_ref, tab_ref, gq_ref, wq_ref, wqs_ref, gkv_ref, wk_ref, wvt_ref,
                 wgt_ref, gbr_ref, gbc_ref, wal_ref, bal_ref, wdvt_ref,
                 mq_ref, mk_ref, lqk_ref, lv_ref, lo_ref, gc_ref, gq_o, gk_o, gv_o, gr_o, glg_o,
                 dq_ref, dk_ref, grow_ref, mvt_ref, dvt_ref):
    x = x_ref[0]
    tm = x.shape[0]
    mod = mod_ref[0]
    h = _rms(x) * g_ref[...] * (1.0 + mod[1:2]) + mod[0:1]
    hb = h.astype(BF16)
    z = _dot(hb, w_ref[...])
    tab = tab_ref[...]
    ct, st, cd, sd = tab[:, 0:128], tab[:, 128:256], tab[:, 256:384], tab[:, 384:512]
    lane = lax.broadcasted_iota(I32, (tm, LANE), 1)

    qn = (_rms(z[:, ZQ:ZQ + 256]) * gq_ref[...]).astype(BF16)
    qa = _dot(qn, wq_ref[...])
    qb = _dot(qn, wqs_ref[...])
    qscale = (MLA_NOPE + MLA_ROPE) ** -0.5 * LOG2E
    for hh in range(N_HEADS):
        sl = slice(HEAD_SLAB * hh, HEAD_SLAB * (hh + 1))
        mq_ref[0, :, sl] = ((qa[:, sl] * ct + qb[:, sl] * st) * qscale).astype(BF16)
    kvn = (_rms(z[:, ZKV:ZKV + 128]) * gkv_ref[...]).astype(BF16)
    kk = _dot(kvn, wk_ref[...])
    kr = z[:, ZKRA:ZKRA + 128] * ct + z[:, ZKRB:ZKRB + 128] * st
    for hh in range(N_HEADS):
        sl = slice(HEAD_SLAB * hh, HEAD_SLAB * (hh + 1))
        mk_ref[0, :, sl] = (kk[:, sl] + kr).astype(BF16)
    ones_row = lax.broadcasted_iota(I32, (N_HEADS * VT_ROWS, tm), 0) % VT_ROWS == MLA_V
    mvt_ref[0, 0] = jnp.where(ones_row, 1.0, _dot_nt(wvt_ref[...], kvn)).astype(BF16)
    dvt_ref[0, 0] = jnp.where(ones_row, 1.0, _dot_nt(wdvt_ref[...], hb)).astype(BF16)

    lqk_ref[0] = z[:, ZMLQK:ZMLQK + 512]
    lv_ref[0] = z[:, ZMLV:ZMLV + 256].astype(BF16)
    lo_ref[0] = z[:, ZMLO:ZMLO + 256].astype(BF16)
    gcol = z[:, ZGATE:ZGATE + 128] + gbr_ref[...]
    gc_ref[0] = jnp.where(lane < 8, gcol, jnp.where(lane < 16, _log_sigmoid(gcol), 0.0))
    zr = _dot_nt(wgt_ref[...], hb) + gbc_ref[...]
    rowi = lax.broadcasted_iota(I32, zr.shape, 0)
    grow_ref[0] = jnp.where(rowi < 8, zr, _log_sigmoid(zr))

    gq_o[0] = (z[:, ZGQ:ZGQ + 128] * GLA_DK ** -0.5).astype(BF16)
    gk_o[0] = z[:, ZGK:ZGK + 128].astype(BF16)
    gv_o[0] = z[:, ZGV:ZGV + 256].astype(BF16)
    gr_o[0] = z[:, ZGR:ZGR + 256].astype(BF16)
    zg = _dot(z[:, ZGA:ZGA + 128].astype(BF16), wal_ref[...]) + bal_ref[...]
    glg_o[0] = _log_sigmoid(zg) * (1.0 / GLA_TAU)

    dscale = DIFF_DQK ** -0.5 * LOG2E
    for g in range(2):
        gs = slice(128 * g, 128 * (g + 1))
        qg = (z[:, ZDQ:ZDQ + 256][:, gs] * cd + z[:, ZDQS:ZDQS + 256][:, gs] * sd) * dscale
        kg = z[:, ZDK:ZDK + 256][:, gs] * cd + z[:, ZDKS:ZDKS + 256][:, gs] * sd
        for hl in range(2):
            hh = 2 * g + hl
            for m in range(2):
                lo = 64 * hl + 32 * m
                s0 = (2 * hh + m) * HEAD_SLAB
                dq_ref[0, :, s0:s0 + HEAD_SLAB] = jnp.where((lane >= lo) & (lane < lo + 32), qg, 0.0).astype(BF16)
            dk_ref[0, :, HEAD_SLAB * hh:HEAD_SLAB * (hh + 1)] = jnp.where(
                (lane >= 64 * hl) & (lane < 64 * hl + 64), kg, 0.0).astype(BF16)


def _proj(x, mod, tab, lw):
    b, t, _ = x.shape
    tm = KV_CHUNK
    consts = [lw["g_mix_pre"], lw["w_ext"], None, lw["g_q_lat"], lw["wq"], lw["wqs"], lw["g_kv_lat"], lw["wk"], lw["wvt"],
              lw["wgt"], lw["gate_bias_row"], lw["gate_bias_col"], lw["walpha"], lw["balpha"], lw["wdvt"]]
    in_specs = [pl.BlockSpec((1, tm, D), lambda bi, i: (bi, i, 0)), pl.BlockSpec((1, 8, D), lambda bi, i: (bi, 0, 0))]
    args = [x, mod]
    for cst in consts:
        if cst is None:
            in_specs.append(pl.BlockSpec((tm, 512), lambda bi, i: (i, 0)))
            args.append(tab)
        else:
            in_specs.append(_const_spec(cst.shape))
            args.append(cst)
    out_specs = [pl.BlockSpec((1, tm, w), lambda bi, i: (bi, i, 0)) for _, w, _ in _PROJ_OUT]
    out_shape = [jax.ShapeDtypeStruct((b, t, w), dt) for _, w, dt in _PROJ_OUT]
    out_specs.append(pl.BlockSpec((1, 16, tm), lambda bi, i: (bi, 0, i)))
    out_shape.append(jax.ShapeDtypeStruct((b, 16, t), F32))
    for _ in range(2):
        out_specs.append(pl.BlockSpec((1, 1, N_HEADS * VT_ROWS, tm), lambda bi, i: (bi, i, 0, 0)))
        out_shape.append(jax.ShapeDtypeStruct((b, t // tm, N_HEADS * VT_ROWS, tm), BF16))
    outs = pl.pallas_call(
        _proj_kernel, name="proj", grid=(b, t // tm), in_specs=in_specs, out_specs=out_specs, out_shape=out_shape,
        compiler_params=_cparams(("parallel", "arbitrary")),
    )(*args)
    res = {name: o for (name, _, _), o in zip(_PROJ_OUT, outs[:-3])}
    res["grow"], res["mvt"], res["dvt"] = outs[-3:]
    return res


def _flash_kernel(*refs, nmap, has_lat, tk, finish, post):
    if has_lat:
        q_ref, kc_ref, vc_ref, kl_ref, vl_ref, dl_ref, g_ref, o_ref, s_ref = refs
    else:
        q_ref, kc_ref, vc_ref, dl_ref, g_ref, o_ref = refs
    qb = q_ref[0]
    tq = qb.shape[0]
    qt = jnp.concatenate([qb[:, HEAD_SLAB * mm:HEAD_SLAB * (mm + 1)].astype(F32).T for mm in range(nmap)],
                         axis=1).astype(BF16)
    rows = nmap * tq
    sub = tk // KV_CHUNK

    def softmax(s, smax, m):
        m_new = jnp.maximum(m, smax)
        return m_new, jnp.exp2(m - m_new), jnp.exp2(s - m_new).astype(BF16)

    def pv(p, vts):
        out = _dot(vts[0], p[0:KV_CHUNK])
        for c in range(1, len(vts)):
            out = out + _dot(vts[c], p[c * KV_CHUNK:(c + 1) * KV_CHUNK])
        return out

    s_ctx = _dot(kc_ref[0], qt)
    m, _, p = softmax(s_ctx, jnp.max(s_ctx, axis=0, keepdims=True), jnp.full((1, rows), NEG, F32))
    acc = pv(p, [vc_ref[0, 0]])
    if has_lat:
        n = kl_ref.shape[1] // tk

        unroll = min(FLASH_UNROLL, n)

        def scores(j):
            if isinstance(j, int):
                return _dot(kl_ref[0, j * tk:(j + 1) * tk, :], qt)
            off = pl.multiple_of(j * tk, tk)
            return _dot(kl_ref[0, pl.ds(off, tk), :], qt)

        def values(j):
            return [vl_ref[0, j * sub + c] for c in range(sub)]

        def produce(slot, j):
            s = scores(j)
            s_ref[slot] = s
            return jnp.max(s, axis=0, keepdims=True)

        smax0 = produce(0, 0)

        def body(jj, carry):
            m, acc, smax = carry
            j = unroll * jj
            for u in range(unroll):
                smax_next = smax
                if not isinstance(j, int):
                    smax_next = produce((u + 1) % 2, jnp.minimum(j + u + 1, n - 1))
                elif j + u + 1 < n:
                    smax_next = produce((u + 1) % 2, j + u + 1)
                m, alpha, p = softmax(s_ref[u % 2], smax, m)
                acc = alpha * acc + pv(p, values(j + u))
                smax = smax_next
            return m, acc, smax

        if unroll == n:
            m, acc, _ = body(0, (m, acc, smax0))
        else:
            m, acc, _ = lax.fori_loop(0, n // unroll, body, (m, acc, smax0))

    o = acc[0:MLA_V, :] / acc[MLA_V:MLA_V + 1, :]
    if nmap == 2:
        lv = dl_ref[...]
        lam = (jnp.exp(jnp.sum(lv[0:1] * lv[1:2], axis=-1, keepdims=True))
               - jnp.exp(jnp.sum(lv[2:3] * lv[3:4], axis=-1, keepdims=True)) + (1.0 - post))
        o = o[:, :tq] - lam * o[:, tq:]
    if finish:
        ms = jnp.mean(o * o, axis=0, keepdims=True)
        o = o * lax.rsqrt(ms + EPS) * jnp.concatenate([g_ref[0]] * (tq // LANE), axis=1) * post
    o_pad = jnp.concatenate([o, jnp.zeros((HEAD_SLAB - MLA_V, tq), F32)], axis=0)
    o_ref[0] = o_pad.T.astype(BF16)


def _flash(q, kc, vct, kl, vlt, dlam, g_out, *, nmap, finish, post, tq, tk):
    b, t, _ = q.shape
    has_lat = kl is not None
    assert kc.shape[1] == KV_CHUNK and tk % KV_CHUNK == 0
    qw = nmap * HEAD_SLAB
    kspec = lambda n: pl.BlockSpec((1, n, HEAD_SLAB), lambda bi, h, i: (bi, 0, h))
    vspec = lambda n: pl.BlockSpec((1, n // KV_CHUNK, VT_ROWS, KV_CHUNK), lambda bi, h, i: (bi, 0, h, 0))
    in_specs = [pl.BlockSpec((1, tq, qw), lambda bi, h, i: (bi, i, h)), kspec(KV_CHUNK), vspec(KV_CHUNK)]
    args = [q, kc, vct]
    scratch = []
    if has_lat:
        tl = kl.shape[1]
        assert (tl // tk) % min(FLASH_UNROLL, tl // tk) == 0
        in_specs += [kspec(tl), vspec(tl)]
        args += [kl, vlt]
        scratch = [pltpu.VMEM((2, tk, nmap * tq), F32)]
    in_specs += [_const_spec(dlam.shape), pl.BlockSpec((1, MLA_V, LANE), lambda bi, h, i: (h, 0, 0))]
    args += [dlam, g_out]
    return pl.pallas_call(
        functools.partial(_flash_kernel, nmap=nmap, has_lat=has_lat, tk=tk, finish=finish, post=post),
        name="flash_diff" if nmap == 2 else "flash_mla",
        grid=(b, N_HEADS, t // tq), in_specs=in_specs,
        out_specs=pl.BlockSpec((1, tq, HEAD_SLAB), lambda bi, h, i: (bi, i, h)),
        out_shape=jax.ShapeDtypeStruct((b, t, N_HEADS * HEAD_SLAB), BF16),
        scratch_shapes=scratch,
        compiler_params=_cparams(("parallel", "parallel", "arbitrary")),
    )(*args)


def _head_of(shape, axis, width):
    return (lax.broadcasted_iota(I32, shape, axis) % (N_HEADS * width)) // width


def _mlstm_dir(d, first, last, x, xprev, xnext, v, gcol, grow, wc, bcv, c_ref, m_ref):
    L = x.shape[0]
    row = lax.broadcasted_iota(I32, x.shape, 0)
    pr = jnp.where(first, 0.0, xprev[7:8, :])
    nx = jnp.where(last, 0.0, xnext[0:1, :])
    xm = jnp.where(row == 0, pr, pltpu.roll(x, 1, 0))
    xp = jnp.where(row == L - 1, nx, pltpu.roll(x, L - 1, 0))
    y = xm * wc[0:1] + x * wc[1:2] + xp * wc[2:3] + bcv
    qk = y * _sigmoid(y)
    q = qk[:, :256]
    k = qk[:, 256:] * ML_DH ** -0.5

    li = lax.broadcasted_iota(I32, (L, L), 0)
    si = lax.broadcasted_iota(I32, (L, L), 1)
    tin = (si <= li) if d == 0 else (si >= li)
    tinb = tin.astype(BF16)
    bcol = _dot_sel(tinb, gcol)
    brow = sum(_dot_nt(t, tinb) for t in _split_bf16(grow, 3))
    m0e = m_ref[0, d]
    cb = c_ref[0, d]
    hm256 = _head_of((L, 256), 1, ML_DH)
    hm512 = _head_of((L, 512), 1, ML_DH)
    e_idx = L - 1 if d == 0 else 0

    d_blk, inter_blk = [], []
    for hh in range(N_HEADS):
        c = 4 * d + hh
        bc = bcol[:, 8 + c:9 + c]
        d_blk.append(jnp.where(tin, bc - brow[8 + c:9 + c, :] + grow[c:c + 1, :], NEG))
        inter_blk.append(bc + m0e[0:1, 64 * hh:64 * hh + 1])
    d_st = jnp.concatenate(d_blk, axis=0)
    inter_st = jnp.concatenate(inter_blk, axis=0)
    mt = jnp.maximum(inter_st, jnp.max(d_st, axis=-1, keepdims=True))
    q_st = jnp.concatenate([jnp.where(hm256 == hh, q, 0.0) for hh in range(N_HEADS)], axis=0).astype(BF16)
    s_st = (jnp.exp(d_st - mt) * _dot_nt(q_st, k.astype(BF16))).astype(BF16)
    vext = jnp.concatenate([v, jnp.ones((L, 256), BF16)], axis=1)
    r = _dot(s_st, vext)
    aint = jnp.exp(inter_st - mt)
    p = _dot(q.astype(BF16), cb.astype(BF16))
    tot = jnp.zeros((L, 512), F32)
    mte = jnp.zeros((L, 256), F32)
    for hh in range(N_HEADS):
        rs = slice(hh * L, (hh + 1) * L)
        tot = jnp.where(hm512 == hh, r[rs] + aint[rs] * p, tot)
        mte = jnp.where(hm256 == hh, mt[rs], mte)
    hout = tot[:, :256] / jnp.maximum(jnp.abs(tot[:, 256:]), jnp.exp(-mte))

    wexp = jnp.zeros((L, 256), F32)
    arow = jnp.zeros((1, 512), F32)
    grw = jnp.zeros((1, 512), F32)
    mnew = jnp.zeros((1, 256), F32)
    hr512 = _head_of((1, 512), 1, ML_DH)
    hr256 = _head_of((1, 256), 1, ML_DH)
    for hh in range(N_HEADS):
        c = 4 * d + hh
        bc = bcol[:, 8 + c:9 + c]
        be = bc[e_idx:e_idx + 1, :]
        wl = be - bc + gcol[:, c:c + 1]
        mloc = jnp.max(wl, axis=0, keepdims=True)
        m0h = m0e[0:1, 64 * hh:64 * hh + 1]
        mn = jnp.maximum(be + m0h, mloc)
        wexp = jnp.where(hm256 == hh, jnp.exp(wl - mloc), wexp)
        arow = jnp.where(hr512 == hh, jnp.exp(be + m0h - mn), arow)
        grw = jnp.where(hr512 == hh, jnp.exp(mloc - mn), grw)
        mnew = jnp.where(hr256 == hh, mn, mnew)
    cl = _dot_tn((k * wexp).astype(BF16), vext)
    bd = lax.broadcasted_iota(I32, (256, 512), 0) // ML_DH == _head_of((256, 512), 1, ML_DH)
    c_ref[0, d] = arow * cb + jnp.where(bd, grw * cl, 0.0)
    m_ref[0, d] = jnp.broadcast_to(mnew, (8, 256))
    return hout


def _mlstm_kernel(xf, xfp, xfn, xb, xbp, xbn, vf, vb, gcf, gcb, grf, grb, wc_ref, bc_ref, c0_ref, m0_ref,
                  hf_ref, hb_ref, c_ref, m_ref):
    i = pl.program_id(1)
    n = pl.num_programs(1)

    @pl.when(i == 0)
    def _():
        c_ref[...] = c0_ref[...]
        m_ref[...] = m0_ref[...]

    wc = wc_ref[...]
    bcv = bc_ref[...]
    hf_ref[0] = _mlstm_dir(0, i == 0, i == n - 1, xf[0], xfp[0], xfn[0], vf[0], gcf[0], grf[0], wc, bcv, c_ref, m_ref)
    hb_ref[0] = _mlstm_dir(1, i == n - 1, i == 0, xb[0], xbp[0], xbn[0], vb[0], gcb[0], grb[0], wc, bcv, c_ref, m_ref)


def _mlstm(pr, w_conv, b_conv, c0, m0):
    x, v, gc, gr = pr["lqk"], pr["lv"], pr["gc"], pr["grow"]
    b, t, _ = x.shape
    L = ML_CHUNK
    n = t // L
    r8 = L // 8
    last8 = t // 8 - 1

    def fw(bi, i):
        return (bi, i, 0)

    def bw(bi, i):
        return (bi, n - 1 - i, 0)

    def halo(ix, shift):
        def f(bi, i):
            blk = ix(bi, i)[1]
            return (bi, jnp.clip(blk * r8 + shift, 0, last8), 0)
        return f

    main = lambda w, ix: pl.BlockSpec((1, L, w), ix)
    in_specs = [main(512, fw), pl.BlockSpec((1, 8, 512), halo(fw, -1)), pl.BlockSpec((1, 8, 512), halo(fw, r8)),
                main(512, bw), pl.BlockSpec((1, 8, 512), halo(bw, -1)), pl.BlockSpec((1, 8, 512), halo(bw, r8)),
                main(256, fw), main(256, bw), main(128, fw), main(128, bw),
                pl.BlockSpec((1, 16, L), lambda bi, i: (bi, 0, i)), pl.BlockSpec((1, 16, L), lambda bi, i: (bi, 0, n - 1 - i)),
                _const_spec(w_conv.shape), _const_spec(b_conv.shape),
                pl.BlockSpec((1, 2, 256, 512), lambda bi, i: (bi, 0, 0, 0)), pl.BlockSpec((1, 2, 8, 256), lambda bi, i: (bi, 0, 0, 0))]
    out_specs = [main(256, fw), main(256, bw),
                 pl.BlockSpec((1, 2, 256, 512), lambda bi, i: (bi, 0, 0, 0)), pl.BlockSpec((1, 2, 8, 256), lambda bi, i: (bi, 0, 0, 0))]
    out_shape = [jax.ShapeDtypeStruct((b, t, 256), F32), jax.ShapeDtypeStruct((b, t, 256), F32),
                 jax.ShapeDtypeStruct(c0.shape, F32), jax.ShapeDtypeStruct(m0.shape, F32)]
    return pl.pallas_call(
        _mlstm_kernel, name="mlstm", grid=(b, n), in_specs=in_specs, out_specs=out_specs, out_shape=out_shape,
        compiler_params=_cparams(("parallel", "arbitrary")),
    )(x, x, x, x, x, x, v, v, gc, gc, gr, gr, w_conv, b_conv, c0, m0)


def _gla_chunk(d, q, k, v, lg, sb):
    L = q.shape[0]
    li = lax.broadcasted_iota(I32, (L, L), 0)
    si = lax.broadcasted_iota(I32, (L, L), 1)
    tin = (si <= li) if d == 0 else (si >= li)
    lgd = lg[:, 128 * d:128 * (d + 1)]
    gcum = _dot_sel(tin.astype(BF16), lgd)
    e_idx = L - 1 if d == 0 else 0
    gend = gcum[e_idx:e_idx + 1, :]
    qf, kf = q.astype(F32), k.astype(F32)
    q_dec = qf * jnp.exp(gcum)
    k_dec = (kf * jnp.exp(-gcum)).astype(BF16)
    k_end = (kf * jnp.exp(gend - gcum)).astype(BF16)
    hm128 = _head_of((L, 128), 1, GLA_DK)
    hm256 = _head_of((L, 256), 1, GLA_DV)
    q_st = jnp.concatenate([jnp.where(hm128 == hh, q_dec, 0.0) for hh in range(N_HEADS)], axis=0).astype(BF16)
    att = _dot_nt(q_st, k_dec)
    tin4 = jnp.concatenate([tin] * N_HEADS, axis=0)
    o_st = _dot(jnp.where(tin4, att, 0.0).astype(BF16), v)
    o = _dot_nt(q_dec.astype(BF16), sb.astype(BF16))
    for hh in range(N_HEADS):
        o = o + jnp.where(hm256 == hh, o_st[hh * L:(hh + 1) * L], 0.0)
    bd = lax.broadcasted_iota(I32, (256, 128), 0) // GLA_DV == _head_of((256, 128), 1, GLA_DK)
    return o, jnp.exp(gend) * sb + jnp.where(bd, _dot_tn(v, k_end), 0.0)


def _gla_kernel(qf, kf, vf, lf, qb, kb, vb, lb, s0_ref, of_ref, ob_ref, s_ref):
    i = pl.program_id(1)

    @pl.when(i == 0)
    def _():
        s_ref[...] = s0_ref[...]

    L = GLA_CHUNK
    nsub = qf.shape[1] // L
    for d, (q, k, v, lg, o_ref) in enumerate(((qf, kf, vf, lf, of_ref), (qb, kb, vb, lb, ob_ref))):
        sb = s_ref[0, d]
        for c in (range(nsub) if d == 0 else reversed(range(nsub))):
            sl = slice(c * L, (c + 1) * L)
            o_ref[0, sl, :], sb = _gla_chunk(d, q[0, sl, :], k[0, sl, :], v[0, sl, :], lg[0, sl, :], sb)
        s_ref[0, d] = sb


def _gla(pr, s0):
    q, k, v, lg = pr["gq"], pr["gk"], pr["gv"], pr["glg"]
    b, t, _ = q.shape
    L = min(GLA_BLOCK, t)
    n = t // L
    fw = lambda bi, i: (bi, i, 0)
    bw = lambda bi, i: (bi, n - 1 - i, 0)
    blk = lambda w, ix: pl.BlockSpec((1, L, w), ix)
    st_spec = pl.BlockSpec((1, 2, 256, 128), lambda bi, i: (bi, 0, 0, 0))
    return pl.pallas_call(
        _gla_kernel, name="gla", grid=(b, n),
        in_specs=[blk(128, fw), blk(128, fw), blk(256, fw), blk(256, fw),
                  blk(128, bw), blk(128, bw), blk(256, bw), blk(256, bw), st_spec],
        out_specs=[blk(256, fw), blk(256, bw), st_spec],
        out_shape=[jax.ShapeDtypeStruct((b, t, 256), F32), jax.ShapeDtypeStruct((b, t, 256), F32),
                   jax.ShapeDtypeStruct(s0.shape, F32)],
        compiler_params=_cparams(("parallel", "arbitrary")),
    )(q, k, v, lg, q, k, v, lg, s0)


def _head_rms_expanded(x, width):
    n = x.shape[1]
    bd = (lax.broadcasted_iota(I32, (n, n), 0) // width == lax.broadcasted_iota(I32, (n, n), 1) // width).astype(BF16)
    return sum(_dot(t, bd) for t in _split_bf16(x * x, 2)) * (1.0 / width)


def _merge_kernel(x_ref, mod_ref, ya_ref, hf_ref, hb_ref, lo_ref, gf_ref, gb_ref, gr_ref, yd_ref,
                  gpre_ref, wg_ref, bg_ref, wbr_ref, wo_ref, gpost_ref, gffn_ref, wr_ref, wrt_ref, gml_ref, ggla_ref,
                  xm_ref, h2_ref, aff_ref, afft_ref):
    x = x_ref[0]
    tm = x.shape[0]
    mod = mod_ref[0]
    hb = (_rms(x) * gpre_ref[...] * (1.0 + mod[1:2]) + mod[0:1]).astype(BF16)

    hs = hf_ref[0] + hb_ref[0]
    y_ml = _sigmoid(lo_ref[0].astype(F32)) * (hs * lax.rsqrt(_head_rms_expanded(hs, ML_DH) + EPS) * gml_ref[...])
    gs = gf_ref[0] + gb_ref[0]
    rr = gr_ref[0].astype(F32)
    y_gla = rr * _sigmoid(rr) * (gs * lax.rsqrt(_head_rms_expanded(gs, GLA_DV) + EPS) * ggla_ref[...])

    branches = ((ya_ref[0], 0, 512), (y_ml.astype(BF16), 512, 256), (y_gla.astype(BF16), 768, 256), (yd_ref[0], 1024, 512))
    mix = jnp.zeros((tm, D), F32)
    for nb, (yb, r0, rw) in enumerate(branches):
        gate = _sigmoid(_dot(hb, wg_ref[:, nb * D:(nb + 1) * D]) + bg_ref[:, nb * D:(nb + 1) * D])
        mix = mix + gate * _dot(yb, wbr_ref[r0:r0 + rw, :])
    y = _dot(mix.astype(BF16), wo_ref[...])
    xm = x + mod[2:3] * (_rms(y) * gpost_ref[...])
    xm_ref[0] = xm

    h2 = (_rms(xm) * gffn_ref[...] * (1.0 + mod[4:5]) + mod[3:4]).astype(BF16)
    h2_ref[0] = h2
    lane = lax.broadcasted_iota(I32, (tm, LANE), 1)
    lg = jnp.where(lane < N_EXPERTS, _dot(h2, wr_ref[...]), NEG)
    e = jnp.exp(lg - jnp.max(lg, axis=-1, keepdims=True))
    aff_ref[0] = (e / jnp.sum(e, axis=-1, keepdims=True))[:, :N_EXPERTS]
    lt = _dot_nt(wrt_ref[...], h2)
    et = jnp.exp(lt - jnp.max(lt, axis=0, keepdims=True))
    afft_ref[0] = et / jnp.sum(et, axis=0, keepdims=True)


def _merge(x, mod, ya, ml, lo, gl, gr, yd, lw, tm):
    b, t, _ = x.shape
    tok = lambda w: pl.BlockSpec((1, tm, w), lambda bi, i: (bi, i, 0))
    consts = [lw["g_mix_pre"], lw["w_gate"], lw["b_gate"], lw["wbr"], lw["w_out"], lw["g_mix_post"], lw["g_ffn_pre"],
              lw["w_router"], lw["w_router_t"], lw["g_mlstm_out"], lw["g_gla_out"]]
    in_specs = [tok(D), pl.BlockSpec((1, 8, D), lambda bi, i: (bi, 0, 0)), tok(512), tok(256), tok(256), tok(256),
                tok(256), tok(256), tok(256), tok(512)] + [_const_spec(c.shape) for c in consts]
    out_specs = [tok(D), tok(D), tok(N_EXPERTS), pl.BlockSpec((1, N_EXPERTS, tm), lambda bi, i: (bi, 0, i))]
    out_shape = [jax.ShapeDtypeStruct((b, t, D), F32), jax.ShapeDtypeStruct((b, t, D), BF16),
                 jax.ShapeDtypeStruct((b, t, N_EXPERTS), F32), jax.ShapeDtypeStruct((b, N_EXPERTS, t), F32)]
    return pl.pallas_call(
        _merge_kernel, name="merge", grid=(b, t // tm), in_specs=in_specs, out_specs=out_specs, out_shape=out_shape,
        compiler_params=_cparams(("parallel", "arbitrary")),
    )(x, mod, ya, ml[0], ml[1], lo, gl[0], gl[1], gr, yd, *consts)


def _topk_kernel(a_ref, pos_ref, s0_ref, *, cap):
    nblk = a_ref.shape[1]
    bits = pltpu.bitcast(a_ref[0], I32)

    def bisect(i, thr):
        cand = thr | (1 << (30 - i))
        cnt = jnp.sum((bits >= cand).astype(I32), axis=(0, 2), keepdims=True)
        return jnp.where(cnt >= cap, cand, thr)

    thr3 = lax.fori_loop(0, 31, bisect, jnp.zeros((1, N_EXPERTS, 1), I32))
    need3 = cap - jnp.sum((bits > thr3).astype(I32), axis=(0, 2), keepdims=True)
    thr, need = thr3[0], need3[0].astype(F32)
    upper = (lax.broadcasted_iota(I32, (TOK_BLK, TOK_BLK), 0) <= lax.broadcasted_iota(I32, (TOK_BLK, TOK_BLK), 1)).astype(BF16)

    def blk(j, carry):
        c_eq, c_sel = carry
        bj = pltpu.bitcast(a_ref[0, j], I32)
        gt, eq = bj > thr, bj == thr
        cum_eq = _dot(eq.astype(BF16), upper) + c_eq
        sel = gt | (eq & (cum_eq <= need))
        cum_sel = _dot(sel.astype(BF16), upper) + c_sel
        pos_ref[0, j] = jnp.where(sel, cum_sel - 1.0, -1.0).astype(I32)
        s0_ref[0, j] = jnp.broadcast_to(c_sel, (N_EXPERTS, LANE)).astype(I32)
        return cum_eq[:, TOK_BLK - 1:TOK_BLK], cum_sel[:, TOK_BLK - 1:TOK_BLK]

    zero = jnp.zeros((N_EXPERTS, 1), F32)
    lax.fori_loop(0, nblk, blk, (zero, zero))


def _topk(aff_t, cap):
    b, _, t = aff_t.shape
    nblk = t // TOK_BLK
    a4 = aff_t.reshape(b, N_EXPERTS, nblk, TOK_BLK).transpose(0, 2, 1, 3)
    spec = lambda w: pl.BlockSpec((1, nblk, N_EXPERTS, w), lambda bi: (bi, 0, 0, 0))
    return pl.pallas_call(
        functools.partial(_topk_kernel, cap=cap), name="topk", grid=(b,),
        in_specs=[spec(TOK_BLK)], out_specs=[spec(TOK_BLK), spec(LANE)],
        out_shape=[jax.ShapeDtypeStruct((b, nblk, N_EXPERTS, TOK_BLK), I32),
                   jax.ShapeDtypeStruct((b, nblk, N_EXPERTS, LANE), I32)],
        compiler_params=_cparams(("parallel",)),
    )(a4)


def _moe_kernel(s0_ref, pos_ref, h_ref, wg_ref, wu_ref, wd_ref, ys_ref, xs_ref, *, nblk, nsub, capp):
    bi, e, tb = pl.program_id(0), pl.program_id(1), pl.program_id(2)

    @pl.when(tb == 0)
    def _():
        xs_ref[...] = jnp.zeros(xs_ref.shape, BF16)

    base = (bi * N_EXPERTS + e) * (nblk + 1) + tb * nsub
    s0s = [s0_ref[base + sb] for sb in range(nsub + 1)]
    a0s = [pl.multiple_of((s0 // 16) * 16, 16) for s0 in s0s[:-1]]
    spans = [s0s[sb + 1] - a0s[sb] for sb in range(nsub)]

    def gather(win, sb):
        a0 = a0s[sb]
        prow = pos_ref[0, sb, pl.ds(e, 1), :]
        slot = lax.broadcasted_iota(I32, (win, TOK_BLK), 0) + a0
        rows = _dot((slot == prow).astype(BF16), h_ref[0, sb * TOK_BLK:(sb + 1) * TOK_BLK, :])
        xs_ref[pl.ds(a0, win), :] = xs_ref[pl.ds(a0, win), :] + rows.astype(BF16)

    all_small = functools.reduce(jnp.logical_and, [sp <= GATHER_WIN_SMALL for sp in spans])

    @pl.when(all_small)
    def _():
        for sb in range(nsub):
            gather(GATHER_WIN_SMALL, sb)

    @pl.when(jnp.logical_not(all_small))
    def _():
        for sb in range(nsub):
            nonempty = s0s[sb + 1] > s0s[sb]
            pl.when(nonempty & (spans[sb] <= GATHER_WIN_SMALL))(functools.partial(gather, GATHER_WIN_SMALL, sb))
            pl.when(spans[sb] > GATHER_WIN_SMALL)(functools.partial(gather, GATHER_WIN, sb))

    @pl.when(tb == pl.num_programs(2) - 1)
    def _():
        def chunk(c, carry):
            off = pl.multiple_of(c * TOK_BLK, TOK_BLK)
            xc = xs_ref[pl.ds(off, TOK_BLK), :]
            hg = _dot(xc, wg_ref[0])
            hid = (hg * _sigmoid(hg) * _dot(xc, wu_ref[0])).astype(BF16)
            ys_ref[0, 0, pl.ds(off, TOK_BLK), :] = _dot(hid, wd_ref[0]).astype(BF16)
            return carry
        lax.fori_loop(0, capp // TOK_BLK, chunk, 0)


def _moe(s0_flat, pos4, h2, lw, capp):
    b, t, _ = h2.shape
    nblk = t // TOK_BLK
    tok = min(MOE_TOK, t)
    nsub = tok // TOK_BLK
    grid_spec = pltpu.PrefetchScalarGridSpec(
        num_scalar_prefetch=1, grid=(b, N_EXPERTS, t // tok),
        in_specs=[pl.BlockSpec((1, nsub, N_EXPERTS, TOK_BLK), lambda bi, e, tb, s: (bi, tb, 0, 0)),
                  pl.BlockSpec((1, tok, D), lambda bi, e, tb, s: (bi, tb, 0)),
                  pl.BlockSpec((1, D, EXPERT_FF), lambda bi, e, tb, s: (e, 0, 0)),
                  pl.BlockSpec((1, D, EXPERT_FF), lambda bi, e, tb, s: (e, 0, 0)),
                  pl.BlockSpec((1, EXPERT_FF, D), lambda bi, e, tb, s: (e, 0, 0))],
        out_specs=pl.BlockSpec((1, 1, capp, D), lambda bi, e, tb, s: (bi, e, 0, 0)),
        scratch_shapes=[pltpu.VMEM((capp + GATHER_WIN, D), BF16)])
    return pl.pallas_call(
        functools.partial(_moe_kernel, nblk=nblk, nsub=nsub, capp=capp), name="moe", grid_spec=grid_spec,
        out_shape=jax.ShapeDtypeStruct((b, N_EXPERTS, capp, D), BF16),
        compiler_params=_cparams(("arbitrary", "arbitrary", "arbitrary")),
    )(s0_flat, pos4, h2, lw["w_e_gate"], lw["w_e_up"], lw["w_e_down"])


def _combine_kernel(s0_ref, *refs, nblk, nb, sblk):
    ys_refs = refs[:2 * N_EXPERTS]
    pos_ref, aff_ref, xm_ref, mod_ref, g_ref, o_ref = refs[2 * N_EXPERTS:]
    bi, tb = pl.program_id(0), pl.program_id(1)
    pos = pos_ref[0]
    aff = aff_ref[0]
    lane = lax.broadcasted_iota(I32, (TOK_BLK, 2 * sblk), 1)
    acc = jnp.zeros((TOK_BLK, D), F32)
    for e in range(N_EXPERTS):
        s0 = s0_ref[(bi * N_EXPERTS + e) * (nblk + 1) + tb]
        blk0 = jnp.minimum(s0 // sblk, nb - 1)
        rel = pos[:, e:e + 1] - blk0 * sblk
        ysw = jnp.concatenate([ys_refs[2 * e][0, 0], ys_refs[2 * e + 1][0, 0]], axis=0)
        acc = acc + aff[:, e:e + 1] * _dot((lane == rel).astype(BF16), ysw)
    mod = mod_ref[0]
    o_ref[0] = xm_ref[0] + mod[5:6] * (_rms(acc) * g_ref[...])


def _combine(s0_flat, ys, pos_t, aff, xm, mod, g_post, sblk):
    b, t, _ = xm.shape
    nblk = t // TOK_BLK
    nb = ys.shape[2] // sblk

    def ys_spec(e, k):
        def ix(bi, tb, s):
            blk0 = jnp.minimum(s[(bi * N_EXPERTS + e) * (nblk + 1) + tb] // sblk, nb - 1)
            return (bi, e, jnp.minimum(blk0 + k, nb - 1), 0)
        return pl.BlockSpec((1, 1, sblk, D), ix)

    tok = lambda w: pl.BlockSpec((1, TOK_BLK, w), lambda bi, tb, s: (bi, tb, 0))
    in_specs = [ys_spec(e, k) for e in range(N_EXPERTS) for k in range(2)]
    in_specs += [tok(N_EXPERTS), tok(N_EXPERTS), tok(D), pl.BlockSpec((1, 8, D), lambda bi, tb, s: (bi, 0, 0)),
                 pl.BlockSpec((1, D), lambda bi, tb, s: (0, 0))]
    grid_spec = pltpu.PrefetchScalarGridSpec(num_scalar_prefetch=1, grid=(b, nblk), in_specs=in_specs, out_specs=tok(D))
    return pl.pallas_call(
        functools.partial(_combine_kernel, nblk=nblk, nb=nb, sblk=sblk), name="combine", grid_spec=grid_spec,
        out_shape=jax.ShapeDtypeStruct((b, t, D), F32),
        compiler_params=_cparams(("arbitrary", "arbitrary")),
    )(s0_flat, *([ys] * (2 * N_EXPERTS)), pos_t, aff, xm, mod, g_post)


def _rope_table(t):
    nf = ROPE_DIM // 4
    pos = jnp.arange(t)
    inv = ROPE_BASE ** (-jnp.arange(nf, dtype=F32) / nf)
    ang = jnp.stack([pos // GRID_W, pos % GRID_W], axis=-1).astype(F32)[..., None] * inv
    cos, sin = jnp.cos(ang), jnp.sin(ang)
    c32 = jnp.stack([cos, cos], axis=2).reshape(t, ROPE_DIM)
    s32 = jnp.stack([-sin, sin], axis=2).reshape(t, ROPE_DIM)
    one, zero = jnp.ones((t, 64), F32), jnp.zeros((t, 32), F32)
    ct = jnp.concatenate([one, c32, zero], axis=1)
    st = jnp.concatenate([0.0 * one, s32, zero], axis=1)
    return jnp.concatenate([ct, st, jnp.tile(c32, (1, 4)), jnp.tile(s32, (1, 4))], axis=1)


def _identity_table(t):
    one, zero = jnp.ones((t, 128), F32), jnp.zeros((t, 128), F32)
    ct = jnp.concatenate([jnp.ones((t, 96), F32), jnp.zeros((t, 32), F32)], axis=1)
    return jnp.concatenate([ct, zero, one, zero], axis=1)


def _layer_weights(i, p):
    lw = {}
    row = lambda a: a.reshape(1, -1)
    for name in ("g_mix_pre", "g_mix_post", "g_ffn_pre", "g_ffn_post", "g_q_lat", "g_kv_lat", "g_mlstm_out", "g_gla_out"):
        lw[name] = row(p[name][i])
    lw["w_ext"] = _gather_cols(p["w_in"][i], _WIN_IDX).astype(BF16)
    lw["wgt"] = p["w_in"][i][:, 416 + 1024:416 + 1040].T.astype(BF16)
    gb = jnp.concatenate([p["b_igate"][i].reshape(-1), p["b_fgate"][i].reshape(-1)])
    lw["gate_bias_row"] = jnp.pad(gb, (0, LANE - 16)).reshape(1, LANE)
    lw["gate_bias_col"] = gb.reshape(16, 1)
    wuq = p["w_uq"][i]
    qi = -np.ones((512,), np.int64)
    qsi = -np.ones((512,), np.int64)
    for h in range(N_HEADS):
        qi[128 * h:128 * h + 96] = 96 * h + np.arange(96)
        qsi[128 * h + 64:128 * h + 96] = 96 * h + 64 + (np.arange(32) ^ 8)
    lw["wq"] = _gather_cols(wuq, qi).astype(BF16)
    lw["wqs"] = _gather_cols(wuq, qsi).astype(BF16)
    ki = -np.ones((512,), np.int64)
    for h in range(N_HEADS):
        ki[128 * h:128 * h + 64] = 128 * h + np.arange(64)
    lw["wk"] = _gather_cols(p["w_ukv"][i], ki).astype(BF16)
    lw["wvt"] = _vt_rows(p["w_ukv"][i].reshape(MLA_KV_LORA, N_HEADS, 128)[:, :, 64:].reshape(MLA_KV_LORA, 256)).astype(BF16)
    lw["wdvt"] = _vt_rows(p["w_in"][i][:, 2768:3024]).astype(BF16)
    wa = p["w_alpha2"][i]
    wal = jnp.zeros((LANE, 256), F32).at[0:16, 0:128].set(wa[0]).at[16:32, 128:256].set(wa[1])
    lw["walpha"] = wal.astype(BF16)
    lw["balpha"] = p["b_alpha"][i].reshape(1, 256)
    lw["w_conv"] = p["w_conv"][i]
    lw["b_conv"] = row(p["b_conv"][i])
    lw["dlam"] = p["diff_lambda"][i]
    lw["g_diff"] = jnp.broadcast_to(p["g_diff_out"][i].reshape(N_HEADS, DIFF_DV, 1), (N_HEADS, DIFF_DV, LANE))
    wb = p["w_branch"][i]
    lw["wbr"] = jnp.concatenate([_pad_heads_rows(wb[0], 64), wb[1], wb[2], _pad_heads_rows(wb[3], 64)], axis=0).astype(BF16)
    lw["w_gate"] = p["w_gate"][i].astype(BF16)
    lw["b_gate"] = row(p["b_gate"][i])
    lw["w_out"] = p["w_out"][i].astype(BF16)
    lw["w_router"] = jnp.pad(p["w_router"][i], ((0, 0), (0, LANE - N_EXPERTS))).astype(BF16)
    lw["w_router_t"] = p["w_router"][i].T.astype(BF16)
    lw["w_e_gate"] = p["w_e_gate"][i]
    lw["w_e_up"] = p["w_e_up"][i]
    lw["w_e_down"] = p["w_e_down"][i]
    return lw


def _ffn(xm, h2, aff, aff_t, mod, lw):
    b, t, _ = xm.shape
    nblk = t // TOK_BLK
    cap = EC_CAPACITY * t // N_EXPERTS
    capp = -(-cap // TOK_BLK) * TOK_BLK
    pos4, s04 = _topk(aff_t, cap)
    s0_be = jnp.concatenate([s04[..., 0].transpose(0, 2, 1), jnp.full((b, N_EXPERTS, 1), cap, I32)], axis=-1)
    s0_flat = s0_be.reshape(-1)
    ys = _moe(s0_flat, pos4, h2, lw, capp)
    pos_t = pos4.transpose(0, 1, 3, 2).reshape(b, t, N_EXPERTS)
    first = jnp.minimum(s0_be[..., :-1] // SLOT_BLK, capp // SLOT_BLK - 1)
    fits = jnp.all(s0_be[..., 1:] <= (first + 2) * SLOT_BLK)
    args = (s0_flat, ys, pos_t, aff, xm, mod, lw["g_ffn_post"])
    return lax.cond(fits, functools.partial(_combine, sblk=SLOT_BLK), functools.partial(_combine, sblk=TOK_BLK), *args)


def _hybrid_layer(i, x_c, x_l, c8, need_ctx, p):
    lw = _layer_weights(i, p)
    b, t, _ = x_l.shape
    tc = x_c.shape[1]
    lam_init = 0.8 - 0.6 * math.exp(-0.3 * i)
    mod8 = _ada(c8, p["w_ada"][i], p["b_ada"][i])
    pad = lambda m: jnp.pad(m.reshape(b, 6, D), ((0, 0), (0, 2), (0, 0)))
    mod_l = pad(mod8[:b])
    mod_c = pad(jnp.broadcast_to(mod8[b:b + 1], (b, 6 * D)))

    pc = _proj(x_c, mod_c, _identity_table(tc), lw)
    pt = _proj(x_l, mod_l, _rope_table(t), lw)

    zc = jnp.zeros((b, 2, 256, 512), F32)
    zm = jnp.full((b, 2, 8, 256), NEG, F32)
    zs = jnp.zeros((b, 2, 256, 128), F32)
    hf_c, hb_c, c_fin, m_fin = _mlstm(pc, lw["w_conv"], lw["b_conv"], zc, zm)
    hf_l, hb_l, _, _ = _mlstm(pt, lw["w_conv"], lw["b_conv"], c_fin, m_fin)
    gf_c, gb_c, s_fin = _gla(pc, zs)
    gf_l, gb_l, _ = _gla(pt, s_fin)

    one_g = jnp.ones((N_HEADS, MLA_V, LANE), F32)
    fl = functools.partial(_flash, tk=FLASH_KEYS)
    ya_l = fl(pt["mq"], pc["mk"], pc["mvt"], pt["mk"], pt["mvt"], lw["dlam"], one_g, nmap=1, finish=False, post=1.0,
              tq=min(FLASH_ROWS, t))
    yd_l = fl(pt["dq"], pc["dk"], pc["dvt"], pt["dk"], pt["dvt"], lw["dlam"], lw["g_diff"], nmap=2, finish=True,
              post=1.0 - lam_init, tq=min(FLASH_ROWS // 2, t))
    xm, h2, aff, aff_t = _merge(x_l, mod_l, ya_l, (hf_l, hb_l), pt["lo"], (gf_l, gb_l), pt["gr"], yd_l, lw, tm=256)
    x_l = _ffn(xm, h2, aff, aff_t, mod_l, lw)

    if need_ctx:
        ya_c = fl(pc["mq"], pc["mk"], pc["mvt"], None, None, lw["dlam"], one_g, nmap=1, finish=False, post=1.0, tq=tc)
        yd_c = fl(pc["dq"], pc["dk"], pc["dvt"], None, None, lw["dlam"], lw["g_diff"], nmap=2, finish=True,
                  post=1.0 - lam_init, tq=tc)
        xm, h2, aff, aff_t = _merge(x_c, mod_c, ya_c, (hf_c, hb_c), pc["lo"], (gf_c, gb_c), pc["gr"], yd_c, lw, tm=tc)
        x_c = _ffn(xm, h2, aff, aff_t, mod_c, lw)
    return x_c, x_l


def kernel(x, c, ctx, c_ctx, w_ada, b_ada, g_mix_pre, g_mix_post, g_ffn_pre, g_ffn_post, w_in, g_q_lat, w_uq, g_kv_lat, w_ukv, w_conv, b_conv, b_igate, b_fgate, g_mlstm_out, w_alpha2, b_alpha, g_gla_out, diff_lambda, g_diff_out, w_branch, w_gate, b_gate, w_out, w_router, w_e_gate, w_e_up, w_e_down):
    p = dict(w_ada=w_ada, b_ada=b_ada, g_mix_pre=g_mix_pre, g_mix_post=g_mix_post, g_ffn_pre=g_ffn_pre,
             g_ffn_post=g_ffn_post, w_in=w_in, g_q_lat=g_q_lat, w_uq=w_uq, g_kv_lat=g_kv_lat, w_ukv=w_ukv,
             w_conv=w_conv, b_conv=b_conv, b_igate=b_igate, b_fgate=b_fgate, g_mlstm_out=g_mlstm_out,
             w_alpha2=w_alpha2, b_alpha=b_alpha, g_gla_out=g_gla_out, diff_lambda=diff_lambda, g_diff_out=g_diff_out,
             w_branch=w_branch, w_gate=w_gate, b_gate=b_gate, w_out=w_out, w_router=w_router,
             w_e_gate=w_e_gate.astype(BF16), w_e_up=w_e_up.astype(BF16), w_e_down=w_e_down.astype(BF16))
    b = x.shape[0]
    c8 = jnp.concatenate([c, c_ctx[None], jnp.zeros((8 - b - 1, D), F32)], axis=0)
    x_c, x_l = ctx, x
    for i in range(DEPTH):
        x_c, x_l = _hybrid_layer(i, x_c, x_l, c8, i < DEPTH - 1, p)
    return x_l
```

```python
import functools
import math

import numpy as np
import jax
import jax.numpy as jnp
from jax import lax
from jax.experimental import pallas as pl
from jax.experimental.pallas import tpu as pltpu

F32 = jnp.float32
BF16 = jnp.bfloat16
I32 = jnp.int32

D = 1024
DEPTH = 2
GRID_W = 64
N_HEADS = 4
MLA_NOPE, MLA_ROPE, MLA_V = 64, 32, 64
MLA_Q_LORA, MLA_KV_LORA = 256, 128
ML_DH = 64
GLA_DK, GLA_DV, GLA_RANK, GLA_TAU = 32, 64, 16, 16.0
DIFF_DQK, DIFF_DV = 32, 64
ROPE_DIM, ROPE_BASE = 32, 10000.0
N_EXPERTS, EC_CAPACITY, EXPERT_FF = 16, 2, 1408
NEG = -1e30
EPS = 1e-6
LOG2E = 1.4426950408889634

LANE = 128
HEAD_SLAB = 128
TOK_BLK = 256
ML_CHUNK = 128
GLA_CHUNK = 64
GLA_BLOCK = 256
GATHER_WIN = TOK_BLK + 16
GATHER_WIN_SMALL = 64
MOE_TOK = 4096
SLOT_BLK = 128
VMEM_LIMIT = 56 * 1024 * 1024

ZQ, ZKV, ZKRA, ZKRB, ZMLQK, ZMLV, ZMLO, ZGATE, ZGA = 0, 256, 384, 512, 640, 1152, 1408, 1664, 1792
ZGQ, ZGK, ZGV, ZGR, ZDQ, ZDQS, ZDK, ZDKS, NZ = 1920, 2048, 2176, 2432, 2688, 2944, 3200, 3456, 3712
KV_CHUNK = 256
VT_ROWS = 80
FLASH_ROWS = 512
FLASH_KEYS = 512
FLASH_UNROLL = 16


def _swap32(c):
    return (c // 32) * 32 + ((c % 32) ^ 8)


def _win_index():
    idx = -np.ones((NZ,), np.int64)
    idx[ZQ:ZQ + 256] = np.arange(0, 256)
    idx[ZKV:ZKV + 128] = np.arange(256, 384)
    r = np.arange(32)
    idx[ZKRA + 64:ZKRA + 96] = 384 + r
    idx[ZKRB + 64:ZKRB + 96] = 384 + (r ^ 8)
    ml = 416
    idx[ZMLQK:ZMLQK + 512] = ml + np.arange(512)
    idx[ZMLV:ZMLV + 256] = ml + 512 + np.arange(256)
    idx[ZMLO:ZMLO + 256] = ml + 768 + np.arange(256)
    idx[ZGATE:ZGATE + 16] = ml + 1024 + np.arange(16)
    gl = 1456
    idx[ZGQ:ZGQ + 128] = gl + np.arange(128)
    idx[ZGK:ZGK + 128] = gl + 128 + np.arange(128)
    idx[ZGV:ZGV + 256] = gl + 256 + np.arange(256)
    idx[ZGR:ZGR + 256] = gl + 512 + np.arange(256)
    idx[ZGA:ZGA + 32] = gl + 768 + np.arange(32)
    df = 2256
    c = np.arange(256)
    idx[ZDQ:ZDQ + 256] = df + c
    idx[ZDQS:ZDQS + 256] = df + _swap32(c)
    idx[ZDK:ZDK + 256] = df + 256 + c
    idx[ZDKS:ZDKS + 256] = df + 256 + _swap32(c)
    return idx


def _vt_rows(w_cols):
    n = w_cols.shape[0]
    w4 = w_cols.T.reshape(N_HEADS, 64, n)
    return jnp.pad(w4, ((0, 0), (0, VT_ROWS - 64), (0, 0))).reshape(N_HEADS * VT_ROWS, n)


_WIN_IDX = _win_index()


def _gather_cols(w, idx):
    safe = np.maximum(idx, 0)
    return jnp.where(jnp.asarray(idx >= 0)[None, :], w[:, safe], 0.0)


def _pad_heads_rows(w, width):
    n = w.shape[1]
    w4 = w.reshape(N_HEADS, width, n)
    return jnp.pad(w4, ((0, 0), (0, HEAD_SLAB - width), (0, 0))).reshape(N_HEADS * HEAD_SLAB, n)


def _cparams(sem):
    return pltpu.CompilerParams(dimension_semantics=sem, vmem_limit_bytes=VMEM_LIMIT)


def _rms(x):
    return x * lax.rsqrt(jnp.mean(x * x, axis=-1, keepdims=True) + EPS)


def _sigmoid(x):
    return 1.0 / (1.0 + jnp.exp(-x))


def _log_sigmoid(x):
    return jnp.minimum(x, 0.0) - jnp.log1p(jnp.exp(-jnp.abs(x)))


def _dot(a, b, precision=None):
    return jnp.dot(a, b, preferred_element_type=F32, precision=precision)


def _dot_nt(a, b, precision=None):
    return lax.dot_general(a, b, (((1,), (1,)), ((), ())), preferred_element_type=F32, precision=precision)


def _dot_tn(a, b, precision=None):
    return lax.dot_general(a, b, (((0,), (0,)), ((), ())), preferred_element_type=F32, precision=precision)


def _split_bf16(x, parts):
    out, r = [], x
    for _ in range(parts):
        t = r.astype(BF16)
        out.append(t)
        r = r - t.astype(F32)
    return out


def _dot_sel(sel, x, parts=3):
    return sum(_dot(sel, t) for t in _split_bf16(x, parts))


def _const_spec(shape):
    nd = len(shape)
    return pl.BlockSpec(shape, lambda *_: (0,) * nd)


def _ada_kernel(c_ref, w_ref, b_ref, o_ref):
    cv = c_ref[...]
    s = (cv * _sigmoid(cv)).astype(BF16)
    o_ref[...] = _dot(s, w_ref[...].astype(BF16)) + b_ref[...]


def _ada(c8, w_ada, b_ada):
    n, tn = 6 * D, 1024
    return pl.pallas_call(
        _ada_kernel, name="ada", grid=(n // tn,),
        in_specs=[pl.BlockSpec((8, D), lambda j: (0, 0)), pl.BlockSpec((D, tn), lambda j: (0, j)),
                  pl.BlockSpec((1, tn), lambda j: (0, j))],
        out_specs=pl.BlockSpec((8, tn), lambda j: (0, j)),
        out_shape=jax.ShapeDtypeStruct((8, n), F32), compiler_params=_cparams(("arbitrary",)),
    )(c8, w_ada, b_ada.reshape(1, n))


_PROJ_OUT = (
    ("mq", 512, BF16), ("mk", 512, BF16),
    ("lqk", 512, F32), ("lv", 256, BF16), ("lo", 256, BF16), ("gc", 128, F32),
    ("gq", 128, BF16), ("gk", 128, BF16), ("gv", 256, BF16), ("gr", 256, BF16), ("glg", 256, F32),
    ("dq", 1024, BF16), ("dk", 512, BF16),
)


def _proj_kernel(x_ref, mod_ref, g_ref, w_ref, tab_ref, gq_ref, wq_ref, wqs_ref, gkv_ref, wk_ref, wvt_ref,
                 wgt_ref, gbr_ref, gbc_ref, wal_ref, bal_ref, wdvt_ref,
                 mq_ref, mk_ref, lqk_ref, lv_ref, lo_ref, gc_ref, gq_o, gk_o, gv_o, gr_o, glg_o,
                 dq_ref, dk_ref, grow_ref, mvt_ref, dvt_ref):
    x = x_ref[0]
    tm = x.shape[0]
    mod = mod_ref[0]
    h = _rms(x) * g_ref[...] * (1.0 + mod[1:2]) + mod[0:1]
    hb = h.astype(BF16)
    z = _dot(hb, w_ref[...])
    tab = tab_ref[...]
    ct, st, cd, sd = tab[:, 0:128], tab[:, 128:256], tab[:, 256:384], tab[:, 384:512]
    lane = lax.broadcasted_iota(I32, (tm, LANE), 1)

    qn = (_rms(z[:, ZQ:ZQ + 256]) * gq_ref[...]).astype(BF16)
    qa = _dot(qn, wq_ref[...])
    qb = _dot(qn, wqs_ref[...])
    qscale = (MLA_NOPE + MLA_ROPE) ** -0.5 * LOG2E
    for hh in range(N_HEADS):
        sl = slice(HEAD_SLAB * hh, HEAD_SLAB * (hh + 1))
        mq_ref[0, :, sl] = ((qa[:, sl] * ct + qb[:, sl] * st) * qscale).astype(BF16)
    kvn = (_rms(z[:, ZKV:ZKV + 128]) * gkv_ref[...]).astype(BF16)
    kk = _dot(kvn, wk_ref[...])
    kr = z[:, ZKRA:ZKRA + 128] * ct + z[:, ZKRB:ZKRB + 128] * st
    for hh in range(N_HEADS):
        sl = slice(HEAD_SLAB * hh, HEAD_SLAB * (hh + 1))
        mk_ref[0, :, sl] = (kk[:, sl] + kr).astype(BF16)
    ones_row = lax.broadcasted_iota(I32, (N_HEADS * VT_ROWS, tm), 0) % VT_ROWS == MLA_V
    mvt_ref[0, 0] = jnp.where(ones_row, 1.0, _dot_nt(wvt_ref[...], kvn)).astype(BF16)
    dvt_ref[0, 0] = jnp.where(ones_row, 1.0, _dot_nt(wdvt_ref[...], hb)).astype(BF16)

    lqk_ref[0] = z[:, ZMLQK:ZMLQK + 512]
    lv_ref[0] = z[:, ZMLV:ZMLV + 256].astype(BF16)
    lo_ref[0] = z[:, ZMLO:ZMLO + 256].astype(BF16)
    gcol = z[:, ZGATE:ZGATE + 128] + gbr_ref[...]
    gc_ref[0] = jnp.where(lane < 8, gcol, jnp.where(lane < 16, _log_sigmoid(gcol), 0.0))
    zr = _dot_nt(wgt_ref[...], hb) + gbc_ref[...]
    rowi = lax.broadcasted_iota(I32, zr.shape, 0)
    grow_ref[0] = jnp.where(rowi < 8, zr, _log_sigmoid(zr))

    gq_o[0] = (z[:, ZGQ:ZGQ + 128] * GLA_DK ** -0.5).astype(BF16)
    gk_o[0] = z[:, ZGK:ZGK + 128].astype(BF16)
    gv_o[0] = z[:, ZGV:ZGV + 256].astype(BF16)
    gr_o[0] = z[:, ZGR:ZGR + 256].astype(BF16)
    zg = _dot(z[:, ZGA:ZGA + 128].astype(BF16), wal_ref[...]) + bal_ref[...]
    glg_o[0] = _log_sigmoid(zg) * (1.0 / GLA_TAU)

    dscale = DIFF_DQK ** -0.5 * LOG2E
    for g in range(2):
        gs = slice(128 * g, 128 * (g + 1))
        qg = (z[:, ZDQ:ZDQ + 256][:, gs] * cd + z[:, ZDQS:ZDQS + 256][:, gs] * sd) * dscale
        kg = z[:, ZDK:ZDK + 256][:, gs] * cd + z[:, ZDKS:ZDKS + 256][:, gs] * sd
        for hl in range(2):
            hh = 2 * g + hl
            for m in range(2):
                lo = 64 * hl + 32 * m
                s0 = (2 * hh + m) * HEAD_SLAB
                dq_ref[0, :, s0:s0 + HEAD_SLAB] = jnp.where((lane >= lo) & (lane < lo + 32), qg, 0.0).astype(BF16)
            dk_ref[0, :, HEAD_SLAB * hh:HEAD_SLAB * (hh + 1)] = jnp.where(
                (lane >= 64 * hl) & (lane < 64 * hl + 64), kg, 0.0).astype(BF16)


def _proj(x, mod, tab, lw):
    b, t, _ = x.shape
    tm = KV_CHUNK
    consts = [lw["g_mix_pre"], lw["w_ext"], None, lw["g_q_lat"], lw["wq"], lw["wqs"], lw["g_kv_lat"], lw["wk"], lw["wvt"],
              lw["wgt"], lw["gate_bias_row"], lw["gate_bias_col"], lw["walpha"], lw["balpha"], lw["wdvt"]]
    in_specs = [pl.BlockSpec((1, tm, D), lambda bi, i: (bi, i, 0)), pl.BlockSpec((1, 8, D), lambda bi, i: (bi, 0, 0))]
    args = [x, mod]
    for cst in consts:
        if cst is None:
            in_specs.append(pl.BlockSpec((tm, 512), lambda bi, i: (i, 0)))
            args.append(tab)
        else:
            in_specs.append(_const_spec(cst.shape))
            args.append(cst)
    out_specs = [pl.BlockSpec((1, tm, w), lambda bi, i: (bi, i, 0)) for _, w, _ in _PROJ_OUT]
    out_shape = [jax.ShapeDtypeStruct((b, t, w), dt) for _, w, dt in _PROJ_OUT]
    out_specs.append(pl.BlockSpec((1, 16, tm), lambda bi, i: (bi, 0, i)))
    out_shape.append(jax.ShapeDtypeStruct((b, 16, t), F32))
    for _ in range(2):
        out_specs.append(pl.BlockSpec((1, 1, N_HEADS * VT_ROWS, tm), lambda bi, i: (bi, i, 0, 0)))
        out_shape.append(jax.ShapeDtypeStruct((b, t // tm, N_HEADS * VT_ROWS, tm), BF16))
    outs = pl.pallas_call(
        _proj_kernel, name="proj", grid=(b, t // tm), in_specs=in_specs, out_specs=out_specs, out_shape=out_shape,
        compiler_params=_cparams(("parallel", "arbitrary")),
    )(*args)
    res = {name: o for (name, _, _), o in zip(_PROJ_OUT, outs[:-3])}
    res["grow"], res["mvt"], res["dvt"] = outs[-3:]
    return res


def _flash_kernel(*refs, nmap, has_lat, tk, finish, post):
    if has_lat:
        q_ref, kc_ref, vc_ref, kl_ref, vl_ref, dl_ref, g_ref, o_ref, s_ref = refs
    else:
        q_ref, kc_ref, vc_ref, dl_ref, g_ref, o_ref = refs
    qb = q_ref[0]
    tq = qb.shape[0]
    qt = jnp.concatenate([qb[:, HEAD_SLAB * mm:HEAD_SLAB * (mm + 1)].astype(F32).T for mm in range(nmap)],
                         axis=1).astype(BF16)
    rows = nmap * tq
    sub = tk // KV_CHUNK

    def softmax(s, smax, m):
        m_new = jnp.maximum(m, smax)
        return m_new, jnp.exp2(m - m_new), jnp.exp2(s - m_new).astype(BF16)

    def pv(p, vts):
        out = _dot(vts[0], p[0:KV_CHUNK])
        for c in range(1, len(vts)):
            out = out + _dot(vts[c], p[c * KV_CHUNK:(c + 1) * KV_CHUNK])
        return out

    s_ctx = _dot(kc_ref[0], qt)
    m, _, p = softmax(s_ctx, jnp.max(s_ctx, axis=0, keepdims=True), jnp.full((1, rows), NEG, F32))
    acc = pv(p, [vc_ref[0, 0]])
    if has_lat:
        n = kl_ref.shape[1] // tk

        unroll = min(FLASH_UNROLL, n)

        def scores(j):
            if isinstance(j, int):
                return _dot(kl_ref[0, j * tk:(j + 1) * tk, :], qt)
            off = pl.multiple_of(j * tk, tk)
            return _dot(kl_ref[0, pl.ds(off, tk), :], qt)

        def values(j):
            return [vl_ref[0, j * sub + c] for c in range(sub)]

        def produce(slot, j):
            s = scores(j)
            s_ref[slot] = s
            return jnp.max(s, axis=0, keepdims=True)

        smax0 = produce(0, 0)

        def body(jj, carry):
            m, acc, smax = carry
            j = unroll * jj
            for u in range(unroll):
                smax_next = smax
                if not isinstance(j, int):
                    smax_next = produce((u + 1) % 2, jnp.minimum(j + u + 1, n - 1))
                elif j + u + 1 < n:
                    smax_next = produce((u + 1) % 2, j + u + 1)
                m, alpha, p = softmax(s_ref[u % 2], smax, m)
                acc = alpha * acc + pv(p, values(j + u))
                smax = smax_next
            return m, acc, smax

        if unroll == n:
            m, acc, _ = body(0, (m, acc, smax0))
        else:
            m, acc, _ = lax.fori_loop(0, n // unroll, body, (m, acc, smax0))

    o = acc[0:MLA_V, :] / acc[MLA_V:MLA_V + 1, :]
    if nmap == 2:
        lv = dl_ref[...]
        lam = (jnp.exp(jnp.sum(lv[0:1] * lv[1:2], axis=-1, keepdims=True))
               - jnp.exp(jnp.sum(lv[2:3] * lv[3:4], axis=-1, keepdims=True)) + (1.0 - post))
        o = o[:, :tq] - lam * o[:, tq:]
    if finish:
        ms = jnp.mean(o * o, axis=0, keepdims=True)
        o = o * lax.rsqrt(ms + EPS) * jnp.concatenate([g_ref[0]] * (tq // LANE), axis=1) * post
    o_pad = jnp.concatenate([o, jnp.zeros((HEAD_SLAB - MLA_V, tq), F32)], axis=0)
    o_ref[0] = o_pad.T.astype(BF16)


def _flash(q, kc, vct, kl, vlt, dlam, g_out, *, nmap, finish, post, tq, tk):
    b, t, _ = q.shape
    has_lat = kl is not None
    assert kc.shape[1] == KV_CHUNK and tk % KV_CHUNK == 0
    qw = nmap * HEAD_SLAB
    kspec = lambda n: pl.BlockSpec((1, n, HEAD_SLAB), lambda bi, h, i: (bi, 0, h))
    vspec = lambda n: pl.BlockSpec((1, n // KV_CHUNK, VT_ROWS, KV_CHUNK), lambda bi, h, i: (bi, 0, h, 0))
    in_specs = [pl.BlockSpec((1, tq, qw), lambda bi, h, i: (bi, i, h)), kspec(KV_CHUNK), vspec(KV_CHUNK)]
    args = [q, kc, vct]
    scratch = []
    if has_lat:
        tl = kl.shape[1]
        assert (tl // tk) % min(FLASH_UNROLL, tl // tk) == 0
        in_specs += [kspec(tl), vspec(tl)]
        args += [kl, vlt]
        scratch = [pltpu.VMEM((2, tk, nmap * tq), F32)]
    in_specs += [_const_spec(dlam.shape), pl.BlockSpec((1, MLA_V, LANE), lambda bi, h, i: (h, 0, 0))]
    args += [dlam, g_out]
    return pl.pallas_call(
        functools.partial(_flash_kernel, nmap=nmap, has_lat=has_lat, tk=tk, finish=finish, post=post),
        name="flash_diff" if nmap == 2 else "flash_mla",
        grid=(b, N_HEADS, t // tq), in_specs=in_specs,
        out_specs=pl.BlockSpec((1, tq, HEAD_SLAB), lambda bi, h, i: (bi, i, h)),
        out_shape=jax.ShapeDtypeStruct((b, t, N_HEADS * HEAD_SLAB), BF16),
        scratch_shapes=scratch,
        compiler_params=_cparams(("parallel", "parallel", "arbitrary")),
    )(*args)


def _head_of(shape, axis, width):
    return (lax.broadcasted_iota(I32, shape, axis) % (N_HEADS * width)) // width


def _mlstm_dir(d, first, last, x, xprev, xnext, v, gcol, grow, wc, bcv, c_ref, m_ref):
    L = x.shape[0]
    row = lax.broadcasted_iota(I32, x.shape, 0)
    pr = jnp.where(first, 0.0, xprev[7:8, :])
    nx = jnp.where(last, 0.0, xnext[0:1, :])
    xm = jnp.where(row == 0, pr, pltpu.roll(x, 1, 0))
    xp = jnp.where(row == L - 1, nx, pltpu.roll(x, L - 1, 0))
    y = xm * wc[0:1] + x * wc[1:2] + xp * wc[2:3] + bcv
    qk = y * _sigmoid(y)
    q = qk[:, :256]
    k = qk[:, 256:] * ML_DH ** -0.5

    li = lax.broadcasted_iota(I32, (L, L), 0)
    si = lax.broadcasted_iota(I32, (L, L), 1)
    tin = (si <= li) if d == 0 else (si >= li)
    tinb = tin.astype(BF16)
    bcol = _dot_sel(tinb, gcol)
    brow = sum(_dot_nt(t, tinb) for t in _split_bf16(grow, 3))
    m0e = m_ref[0, d]
    cb = c_ref[0, d]
    hm256 = _head_of((L, 256), 1, ML_DH)
    hm512 = _head_of((L, 512), 1, ML_DH)
    e_idx = L - 1 if d == 0 else 0

    d_blk, inter_blk = [], []
    for hh in range(N_HEADS):
        c = 4 * d + hh
        bc = bcol[:, 8 + c:9 + c]
        d_blk.append(jnp.where(tin, bc - brow[8 + c:9 + c, :] + grow[c:c + 1, :], NEG))
        inter_blk.append(bc + m0e[0:1, 64 * hh:64 * hh + 1])
    d_st = jnp.concatenate(d_blk, axis=0)
    inter_st = jnp.concatenate(inter_blk, axis=0)
    mt = jnp.maximum(inter_st, jnp.max(d_st, axis=-1, keepdims=True))
    q_st = jnp.concatenate([jnp.where(hm256 == hh, q, 0.0) for hh in range(N_HEADS)], axis=0).astype(BF16)
    s_st = (jnp.exp(d_st - mt) * _dot_nt(q_st, k.astype(BF16))).astype(BF16)
    vext = jnp.concatenate([v, jnp.ones((L, 256), BF16)], axis=1)
    r = _dot(s_st, vext)
    aint = jnp.exp(inter_st - mt)
    p = _dot(q.astype(BF16), cb.astype(BF16))
    tot = jnp.zeros((L, 512), F32)
    mte = jnp.zeros((L, 256), F32)
    for hh in range(N_HEADS):
        rs = slice(hh * L, (hh + 1) * L)
        tot = jnp.where(hm512 == hh, r[rs] + aint[rs] * p, tot)
        mte = jnp.where(hm256 == hh, mt[rs], mte)
    hout = tot[:, :256] / jnp.maximum(jnp.abs(tot[:, 256:]), jnp.exp(-mte))

    wexp = jnp.zeros((L, 256), F32)
    arow = jnp.zeros((1, 512), F32)
    grw = jnp.zeros((1, 512), F32)
    mnew = jnp.zeros((1, 256), F32)
    hr512 = _head_of((1, 512), 1, ML_DH)
    hr256 = _head_of((1, 256), 1, ML_DH)
    for hh in range(N_HEADS):
        c = 4 * d + hh
        bc = bcol[:, 8 + c:9 + c]
        be = bc[e_idx:e_idx + 1, :]
        wl = be - bc + gcol[:, c:c + 1]
        mloc = jnp.max(wl, axis=0, keepdims=True)
        m0h = m0e[0:1, 64 * hh:64 * hh + 1]
        mn = jnp.maximum(be + m0h, mloc)
        wexp = jnp.where(hm256 == hh, jnp.exp(wl - mloc), wexp)
        arow = jnp.where(hr512 == hh, jnp.exp(be + m0h - mn), arow)
        grw = jnp.where(hr512 == hh, jnp.exp(mloc - mn), grw)
        mnew = jnp.where(hr256 == hh, mn, mnew)
    cl = _dot_tn((k * wexp).astype(BF16), vext)
    bd = lax.broadcasted_iota(I32, (256, 512), 0) // ML_DH == _head_of((256, 512), 1, ML_DH)
    c_ref[0, d] = arow * cb + jnp.where(bd, grw * cl, 0.0)
    m_ref[0, d] = jnp.broadcast_to(mnew, (8, 256))
    return hout


def _mlstm_kernel(xf, xfp, xfn, xb, xbp, xbn, vf, vb, gcf, gcb, grf, grb, wc_ref, bc_ref, c0_ref, m0_ref,
                  hf_ref, hb_ref, c_ref, m_ref):
    i = pl.program_id(1)
    n = pl.num_programs(1)

    @pl.when(i == 0)
    def _():
        c_ref[...] = c0_ref[...]
        m_ref[...] = m0_ref[...]

    wc = wc_ref[...]
    bcv = bc_ref[...]
    hf_ref[0] = _mlstm_dir(0, i == 0, i == n - 1, xf[0], xfp[0], xfn[0], vf[0], gcf[0], grf[0], wc, bcv, c_ref, m_ref)
    hb_ref[0] = _mlstm_dir(1, i == n - 1, i == 0, xb[0], xbp[0], xbn[0], vb[0], gcb[0], grb[0], wc, bcv, c_ref, m_ref)


def _mlstm(pr, w_conv, b_conv, c0, m0):
    x, v, gc, gr = pr["lqk"], pr["lv"], pr["gc"], pr["grow"]
    b, t, _ = x.shape
    L = ML_CHUNK
    n = t // L
    r8 = L // 8
    last8 = t // 8 - 1

    def fw(bi, i):
        return (bi, i, 0)

    def bw(bi, i):
        return (bi, n - 1 - i, 0)

    def halo(ix, shift):
        def f(bi, i):
            blk = ix(bi, i)[1]
            return (bi, jnp.clip(blk * r8 + shift, 0, last8), 0)
        return f

    main = lambda w, ix: pl.BlockSpec((1, L, w), ix)
    in_specs = [main(512, fw), pl.BlockSpec((1, 8, 512), halo(fw, -1)), pl.BlockSpec((1, 8, 512), halo(fw, r8)),
                main(512, bw), pl.BlockSpec((1, 8, 512), halo(bw, -1)), pl.BlockSpec((1, 8, 512), halo(bw, r8)),
                main(256, fw), main(256, bw), main(128, fw), main(128, bw),
                pl.BlockSpec((1, 16, L), lambda bi, i: (bi, 0, i)), pl.BlockSpec((1, 16, L), lambda bi, i: (bi, 0, n - 1 - i)),
                _const_spec(w_conv.shape), _const_spec(b_conv.shape),
                pl.BlockSpec((1, 2, 256, 512), lambda bi, i: (bi, 0, 0, 0)), pl.BlockSpec((1, 2, 8, 256), lambda bi, i: (bi, 0, 0, 0))]
    out_specs = [main(256, fw), main(256, bw),
                 pl.BlockSpec((1, 2, 256, 512), lambda bi, i: (bi, 0, 0, 0)), pl.BlockSpec((1, 2, 8, 256), lambda bi, i: (bi, 0, 0, 0))]
    out_shape = [jax.ShapeDtypeStruct((b, t, 256), F32), jax.ShapeDtypeStruct((b, t, 256), F32),
                 jax.ShapeDtypeStruct(c0.shape, F32), jax.ShapeDtypeStruct(m0.shape, F32)]
    return pl.pallas_call(
        _mlstm_kernel, name="mlstm", grid=(b, n), in_specs=in_specs, out_specs=out_specs, out_shape=out_shape,
        compiler_params=_cparams(("parallel", "arbitrary")),
    )(x, x, x, x, x, x, v, v, gc, gc, gr, gr, w_conv, b_conv, c0, m0)


def _gla_chunk(d, q, k, v, lg, sb):
    L = q.shape[0]
    li = lax.broadcasted_iota(I32, (L, L), 0)
    si = lax.broadcasted_iota(I32, (L, L), 1)
    tin = (si <= li) if d == 0 else (si >= li)
    lgd = lg[:, 128 * d:128 * (d + 1)]
    gcum = _dot_sel(tin.astype(BF16), lgd)
    e_idx = L - 1 if d == 0 else 0
    gend = gcum[e_idx:e_idx + 1, :]
    qf, kf = q.astype(F32), k.astype(F32)
    q_dec = qf * jnp.exp(gcum)
    k_dec = (kf * jnp.exp(-gcum)).astype(BF16)
    k_end = (kf * jnp.exp(gend - gcum)).astype(BF16)
    hm128 = _head_of((L, 128), 1, GLA_DK)
    hm256 = _head_of((L, 256), 1, GLA_DV)
    q_st = jnp.concatenate([jnp.where(hm128 == hh, q_dec, 0.0) for hh in range(N_HEADS)], axis=0).astype(BF16)
    att = _dot_nt(q_st, k_dec)
    tin4 = jnp.concatenate([tin] * N_HEADS, axis=0)
    o_st = _dot(jnp.where(tin4, att, 0.0).astype(BF16), v)
    o = _dot_nt(q_dec.astype(BF16), sb.astype(BF16))
    for hh in range(N_HEADS):
        o = o + jnp.where(hm256 == hh, o_st[hh * L:(hh + 1) * L], 0.0)
    bd = lax.broadcasted_iota(I32, (256, 128), 0) // GLA_DV == _head_of((256, 128), 1, GLA_DK)
    return o, jnp.exp(gend) * sb + jnp.where(bd, _dot_tn(v, k_end), 0.0)


def _gla_kernel(qf, kf, vf, lf, qb, kb, vb, lb, s0_ref, of_ref, ob_ref, s_ref):
    i = pl.program_id(1)

    @pl.when(i == 0)
    def _():
        s_ref[...] = s0_ref[...]

    L = GLA_CHUNK
    nsub = qf.shape[1] // L
    for d, (q, k, v, lg, o_ref) in enumerate(((qf, kf, vf, lf, of_ref), (qb, kb, vb, lb, ob_ref))):
        sb = s_ref[0, d]
        for c in (range(nsub) if d == 0 else reversed(range(nsub))):
            sl = slice(c * L, (c + 1) * L)
            o_ref[0, sl, :], sb = _gla_chunk(d, q[0, sl, :], k[0, sl, :], v[0, sl, :], lg[0, sl, :], sb)
        s_ref[0, d] = sb


def _gla(pr, s0):
    q, k, v, lg = pr["gq"], pr["gk"], pr["gv"], pr["glg"]
    b, t, _ = q.shape
    L = min(GLA_BLOCK, t)
    n = t // L
    fw = lambda bi, i: (bi, i, 0)
    bw = lambda bi, i: (bi, n - 1 - i, 0)
    blk = lambda w, ix: pl.BlockSpec((1, L, w), ix)
    st_spec = pl.BlockSpec((1, 2, 256, 128), lambda bi, i: (bi, 0, 0, 0))
    return pl.pallas_call(
        _gla_kernel, name="gla", grid=(b, n),
        in_specs=[blk(128, fw), blk(128, fw), blk(256, fw), blk(256, fw),
                  blk(128, bw), blk(128, bw), blk(256, bw), blk(256, bw), st_spec],
        out_specs=[blk(256, fw), blk(256, bw), st_spec],
        out_shape=[jax.ShapeDtypeStruct((b, t, 256), F32), jax.ShapeDtypeStruct((b, t, 256), F32),
                   jax.ShapeDtypeStruct(s0.shape, F32)],
        compiler_params=_cparams(("parallel", "arbitrary")),
    )(q, k, v, lg, q, k, v, lg, s0)


def _head_rms_expanded(x, width):
    n = x.shape[1]
    bd = (lax.broadcasted_iota(I32, (n, n), 0) // width == lax.broadcasted_iota(I32, (n, n), 1) // width).astype(BF16)
    return sum(_dot(t, bd) for t in _split_bf16(x * x, 2)) * (1.0 / width)


def _merge_kernel(x_ref, mod_ref, ya_ref, hf_ref, hb_ref, lo_ref, gf_ref, gb_ref, gr_ref, yd_ref,
                  gpre_ref, wg_ref, bg_ref, wbr_ref, wo_ref, gpost_ref, gffn_ref, wr_ref, wrt_ref, gml_ref, ggla_ref,
                  xm_ref, h2_ref, aff_ref, afft_ref):
    x = x_ref[0]
    tm = x.shape[0]
    mod = mod_ref[0]
    hb = (_rms(x) * gpre_ref[...] * (1.0 + mod[1:2]) + mod[0:1]).astype(BF16)

    hs = hf_ref[0] + hb_ref[0]
    y_ml = _sigmoid(lo_ref[0].astype(F32)) * (hs * lax.rsqrt(_head_rms_expanded(hs, ML_DH) + EPS) * gml_ref[...])
    gs = gf_ref[0] + gb_ref[0]
    rr = gr_ref[0].astype(F32)
    y_gla = rr * _sigmoid(rr) * (gs * lax.rsqrt(_head_rms_expanded(gs, GLA_DV) + EPS) * ggla_ref[...])

    branches = ((ya_ref[0], 0, 512), (y_ml.astype(BF16), 512, 256), (y_gla.astype(BF16), 768, 256), (yd_ref[0], 1024, 512))
    mix = jnp.zeros((tm, D), F32)
    for nb, (yb, r0, rw) in enumerate(branches):
        gate = _sigmoid(_dot(hb, wg_ref[:, nb * D:(nb + 1) * D]) + bg_ref[:, nb * D:(nb + 1) * D])
        mix = mix + gate * _dot(yb, wbr_ref[r0:r0 + rw, :])
    y = _dot(mix.astype(BF16), wo_ref[...])
    xm = x + mod[2:3] * (_rms(y) * gpost_ref[...])
    xm_ref[0] = xm

    h2 = (_rms(xm) * gffn_ref[...] * (1.0 + mod[4:5]) + mod[3:4]).astype(BF16)
    h2_ref[0] = h2
    lane = lax.broadcasted_iota(I32, (tm, LANE), 1)
    lg = jnp.where(lane < N_EXPERTS, _dot(h2, wr_ref[...]), NEG)
    e = jnp.exp(lg - jnp.max(lg, axis=-1, keepdims=True))
    aff_ref[0] = (e / jnp.sum(e, axis=-1, keepdims=True))[:, :N_EXPERTS]
    lt = _dot_nt(wrt_ref[...], h2)
    et = jnp.exp(lt - jnp.max(lt, axis=0, keepdims=True))
    afft_ref[0] = et / jnp.sum(et, axis=0, keepdims=True)


def _merge(x, mod, ya, ml, lo, gl, gr, yd, lw, tm):
    b, t, _ = x.shape
    tok = lambda w: pl.BlockSpec((1, tm, w), lambda bi, i: (bi, i, 0))
    consts = [lw["g_mix_pre"], lw["w_gate"], lw["b_gate"], lw["wbr"], lw["w_out"], lw["g_mix_post"], lw["g_ffn_pre"],
              lw["w_router"], lw["w_router_t"], lw["g_mlstm_out"], lw["g_gla_out"]]
    in_specs = [tok(D), pl.BlockSpec((1, 8, D), lambda bi, i: (bi, 0, 0)), tok(512), tok(256), tok(256), tok(256),
                tok(256), tok(256), tok(256), tok(512)] + [_const_spec(c.shape) for c in consts]
    out_specs = [tok(D), tok(D), tok(N_EXPERTS), pl.BlockSpec((1, N_EXPERTS, tm), lambda bi, i: (bi, 0, i))]
    out_shape = [jax.ShapeDtypeStruct((b, t, D), F32), jax.ShapeDtypeStruct((b, t, D), BF16),
                 jax.ShapeDtypeStruct((b, t, N_EXPERTS), F32), jax.ShapeDtypeStruct((b, N_EXPERTS, t), F32)]
    return pl.pallas_call(
        _merge_kernel, name="merge", grid=(b, t // tm), in_specs=in_specs, out_specs=out_specs, out_shape=out_shape,
        compiler_params=_cparams(("parallel", "arbitrary")),
    )(x, mod, ya, ml[0], ml[1], lo, gl[0], gl[1], gr, yd, *consts)


def _topk_kernel(a_ref, pos_ref, s0_ref, *, cap):
    nblk = a_ref.shape[1]
    bits = pltpu.bitcast(a_ref[0], I32)

    def bisect(i, thr):
        cand = thr | (1 << (30 - i))
        cnt = jnp.sum((bits >= cand).astype(I32), axis=(0, 2), keepdims=True)
        return jnp.where(cnt >= cap, cand, thr)

    thr3 = lax.fori_loop(0, 31, bisect, jnp.zeros((1, N_EXPERTS, 1), I32))
    need3 = cap - jnp.sum((bits > thr3).astype(I32), axis=(0, 2), keepdims=True)
    thr, need = thr3[0], need3[0].astype(F32)
    upper = (lax.broadcasted_iota(I32, (TOK_BLK, TOK_BLK), 0) <= lax.broadcasted_iota(I32, (TOK_BLK, TOK_BLK), 1)).astype(BF16)

    def blk(j, carry):
        c_eq, c_sel = carry
        bj = pltpu.bitcast(a_ref[0, j], I32)
        gt, eq = bj > thr, bj == thr
        cum_eq = _dot(eq.astype(BF16), upper) + c_eq
        sel = gt | (eq & (cum_eq <= need))
        cum_sel = _dot(sel.astype(BF16), upper) + c_sel
        pos_ref[0, j] = jnp.where(sel, cum_sel - 1.0, -1.0).astype(I32)
        s0_ref[0, j] = jnp.broadcast_to(c_sel, (N_EXPERTS, LANE)).astype(I32)
        return cum_eq[:, TOK_BLK - 1:TOK_BLK], cum_sel[:, TOK_BLK - 1:TOK_BLK]

    zero = jnp.zeros((N_EXPERTS, 1), F32)
    lax.fori_loop(0, nblk, blk, (zero, zero))


def _topk(aff_t, cap):
    b, _, t = aff_t.shape
    nblk = t // TOK_BLK
    a4 = aff_t.reshape(b, N_EXPERTS, nblk, TOK_BLK).transpose(0, 2, 1, 3)
    spec = lambda w: pl.BlockSpec((1, nblk, N_EXPERTS, w), lambda bi: (bi, 0, 0, 0))
    return pl.pallas_call(
        functools.partial(_topk_kernel, cap=cap), name="topk", grid=(b,),
        in_specs=[spec(TOK_BLK)], out_specs=[spec(TOK_BLK), spec(LANE)],
        out_shape=[jax.ShapeDtypeStruct((b, nblk, N_EXPERTS, TOK_BLK), I32),
                   jax.ShapeDtypeStruct((b, nblk, N_EXPERTS, LANE), I32)],
        compiler_params=_cparams(("parallel",)),
    )(a4)


def _moe_kernel(s0_ref, pos_ref, h_ref, wg_ref, wu_ref, wd_ref, ys_ref, xs_ref, *, nblk, nsub, capp):
    bi, e, tb = pl.program_id(0), pl.program_id(1), pl.program_id(2)

    @pl.when(tb == 0)
    def _():
        xs_ref[...] = jnp.zeros(xs_ref.shape, BF16)

    base = (bi * N_EXPERTS + e) * (nblk + 1) + tb * nsub
    s0s = [s0_ref[base + sb] for sb in range(nsub + 1)]
    a0s = [pl.multiple_of((s0 // 16) * 16, 16) for s0 in s0s[:-1]]
    spans = [s0s[sb + 1] - a0s[sb] for sb in range(nsub)]

    def gather(win, sb):
        a0 = a0s[sb]
        prow = pos_ref[0, sb, pl.ds(e, 1), :]
        slot = lax.broadcasted_iota(I32, (win, TOK_BLK), 0) + a0
        rows = _dot((slot == prow).astype(BF16), h_ref[0, sb * TOK_BLK:(sb + 1) * TOK_BLK, :])
        xs_ref[pl.ds(a0, win), :] = xs_ref[pl.ds(a0, win), :] + rows.astype(BF16)

    all_small = functools.reduce(jnp.logical_and, [sp <= GATHER_WIN_SMALL for sp in spans])

    @pl.when(all_small)
    def _():
        for sb in range(nsub):
            gather(GATHER_WIN_SMALL, sb)

    @pl.when(jnp.logical_not(all_small))
    def _():
        for sb in range(nsub):
            nonempty = s0s[sb + 1] > s0s[sb]
            pl.when(nonempty & (spans[sb] <= GATHER_WIN_SMALL))(functools.partial(gather, GATHER_WIN_SMALL, sb))
            pl.when(spans[sb] > GATHER_WIN_SMALL)(functools.partial(gather, GATHER_WIN, sb))

    @pl.when(tb == pl.num_programs(2) - 1)
    def _():
        def chunk(c, carry):
            off = pl.multiple_of(c * TOK_BLK, TOK_BLK)
            xc = xs_ref[pl.ds(off, TOK_BLK), :]
            hg = _dot(xc, wg_ref[0, 0])
            hid = (hg * _sigmoid(hg) * _dot(xc, wu_ref[0, 0])).astype(BF16)
            ys_ref[0, 0, pl.ds(off, TOK_BLK), :] = _dot(hid, wd_ref[0, 0]).astype(BF16)
            return carry
        lax.fori_loop(0, capp // TOK_BLK, chunk, 0)


def _moe(s0_flat, pos4, h2, lw, capp):
    b, t, _ = h2.shape
    layer = lw["layer"]
    nblk = t // TOK_BLK
    tok = min(MOE_TOK, t)
    nsub = tok // TOK_BLK
    grid_spec = pltpu.PrefetchScalarGridSpec(
        num_scalar_prefetch=1, grid=(b, N_EXPERTS, t // tok),
        in_specs=[pl.BlockSpec((1, nsub, N_EXPERTS, TOK_BLK), lambda bi, e, tb, s: (bi, tb, 0, 0)),
                  pl.BlockSpec((1, tok, D), lambda bi, e, tb, s: (bi, tb, 0)),
                  pl.BlockSpec((1, 1, D, EXPERT_FF), lambda bi, e, tb, s: (layer, e, 0, 0)),
                  pl.BlockSpec((1, 1, D, EXPERT_FF), lambda bi, e, tb, s: (layer, e, 0, 0)),
                  pl.BlockSpec((1, 1, EXPERT_FF, D), lambda bi, e, tb, s: (layer, e, 0, 0))],
        out_specs=pl.BlockSpec((1, 1, capp, D), lambda bi, e, tb, s: (bi, e, 0, 0)),
        scratch_shapes=[pltpu.VMEM((capp + GATHER_WIN, D), BF16)])
    return pl.pallas_call(
        functools.partial(_moe_kernel, nblk=nblk, nsub=nsub, capp=capp), name="moe", grid_spec=grid_spec,
        out_shape=jax.ShapeDtypeStruct((b, N_EXPERTS, capp, D), BF16),
        compiler_params=_cparams(("arbitrary", "arbitrary", "arbitrary")),
    )(s0_flat, pos4, h2, lw["w_e_gate"], lw["w_e_up"], lw["w_e_down"])


def _combine_kernel(s0_ref, *refs, nblk, nb, sblk):
    ys_refs = refs[:2 * N_EXPERTS]
    pos_ref, aff_ref, xm_ref, mod_ref, g_ref, o_ref = refs[2 * N_EXPERTS:]
    bi, tb = pl.program_id(0), pl.program_id(1)
    pos = pos_ref[0]
    aff = aff_ref[0]
    lane = lax.broadcasted_iota(I32, (TOK_BLK, 2 * sblk), 1)
    acc = jnp.zeros((TOK_BLK, D), F32)
    for e in range(N_EXPERTS):
        s0 = s0_ref[(bi * N_EXPERTS + e) * (nblk + 1) + tb]
        blk0 = jnp.minimum(s0 // sblk, nb - 1)
        rel = pos[:, e:e + 1] - blk0 * sblk
        ysw = jnp.concatenate([ys_refs[2 * e][0, 0], ys_refs[2 * e + 1][0, 0]], axis=0)
        acc = acc + aff[:, e:e + 1] * _dot((lane == rel).astype(BF16), ysw)
    mod = mod_ref[0]
    o_ref[0] = xm_ref[0] + mod[5:6] * (_rms(acc) * g_ref[...])


def _combine(s0_flat, ys, pos_t, aff, xm, mod, g_post, sblk):
    b, t, _ = xm.shape
    nblk = t // TOK_BLK
    nb = ys.shape[2] // sblk

    def ys_spec(e, k):
        def ix(bi, tb, s):
            blk0 = jnp.minimum(s[(bi * N_EXPERTS + e) * (nblk + 1) + tb] // sblk, nb - 1)
            return (bi, e, jnp.minimum(blk0 + k, nb - 1), 0)
        return pl.BlockSpec((1, 1, sblk, D), ix)

    tok = lambda w: pl.BlockSpec((1, TOK_BLK, w), lambda bi, tb, s: (bi, tb, 0))
    in_specs = [ys_spec(e, k) for e in range(N_EXPERTS) for k in range(2)]
    in_specs += [tok(N_EXPERTS), tok(N_EXPERTS), tok(D), pl.BlockSpec((1, 8, D), lambda bi, tb, s: (bi, 0, 0)),
                 pl.BlockSpec((1, D), lambda bi, tb, s: (0, 0))]
    grid_spec = pltpu.PrefetchScalarGridSpec(num_scalar_prefetch=1, grid=(b, nblk), in_specs=in_specs, out_specs=tok(D))
    return pl.pallas_call(
        functools.partial(_combine_kernel, nblk=nblk, nb=nb, sblk=sblk), name="combine", grid_spec=grid_spec,
        out_shape=jax.ShapeDtypeStruct((b, t, D), F32),
        compiler_params=_cparams(("arbitrary", "arbitrary")),
    )(s0_flat, *([ys] * (2 * N_EXPERTS)), pos_t, aff, xm, mod, g_post)


def _rope_table(t):
    nf = ROPE_DIM // 4
    pos = jnp.arange(t)
    inv = ROPE_BASE ** (-jnp.arange(nf, dtype=F32) / nf)
    ang = jnp.stack([pos // GRID_W, pos % GRID_W], axis=-1).astype(F32)[..., None] * inv
    cos, sin = jnp.cos(ang), jnp.sin(ang)
    c32 = jnp.stack([cos, cos], axis=2).reshape(t, ROPE_DIM)
    s32 = jnp.stack([-sin, sin], axis=2).reshape(t, ROPE_DIM)
    one, zero = jnp.ones((t, 64), F32), jnp.zeros((t, 32), F32)
    ct = jnp.concatenate([one, c32, zero], axis=1)
    st = jnp.concatenate([0.0 * one, s32, zero], axis=1)
    return jnp.concatenate([ct, st, jnp.tile(c32, (1, 4)), jnp.tile(s32, (1, 4))], axis=1)


def _identity_table(t):
    one, zero = jnp.ones((t, 128), F32), jnp.zeros((t, 128), F32)
    ct = jnp.concatenate([jnp.ones((t, 96), F32), jnp.zeros((t, 32), F32)], axis=1)
    return jnp.concatenate([ct, zero, one, zero], axis=1)


def _layer_weights(i, p):
    lw = {}
    row = lambda a: a.reshape(1, -1)
    for name in ("g_mix_pre", "g_mix_post", "g_ffn_pre", "g_ffn_post", "g_q_lat", "g_kv_lat", "g_mlstm_out", "g_gla_out"):
        lw[name] = row(p[name][i])
    lw["w_ext"] = _gather_cols(p["w_in"][i], _WIN_IDX).astype(BF16)
    lw["wgt"] = p["w_in"][i][:, 416 + 1024:416 + 1040].T.astype(BF16)
    gb = jnp.concatenate([p["b_igate"][i].reshape(-1), p["b_fgate"][i].reshape(-1)])
    lw["gate_bias_row"] = jnp.pad(gb, (0, LANE - 16)).reshape(1, LANE)
    lw["gate_bias_col"] = gb.reshape(16, 1)
    wuq = p["w_uq"][i]
    qi = -np.ones((512,), np.int64)
    qsi = -np.ones((512,), np.int64)
    for h in range(N_HEADS):
        qi[128 * h:128 * h + 96] = 96 * h + np.arange(96)
        qsi[128 * h + 64:128 * h + 96] = 96 * h + 64 + (np.arange(32) ^ 8)
    lw["wq"] = _gather_cols(wuq, qi).astype(BF16)
    lw["wqs"] = _gather_cols(wuq, qsi).astype(BF16)
    ki = -np.ones((512,), np.int64)
    for h in range(N_HEADS):
        ki[128 * h:128 * h + 64] = 128 * h + np.arange(64)
    lw["wk"] = _gather_cols(p["w_ukv"][i], ki).astype(BF16)
    lw["wvt"] = _vt_rows(p["w_ukv"][i].reshape(MLA_KV_LORA, N_HEADS, 128)[:, :, 64:].reshape(MLA_KV_LORA, 256)).astype(BF16)
    lw["wdvt"] = _vt_rows(p["w_in"][i][:, 2768:3024]).astype(BF16)
    wa = p["w_alpha2"][i]
    wal = jnp.zeros((LANE, 256), F32).at[0:16, 0:128].set(wa[0]).at[16:32, 128:256].set(wa[1])
    lw["walpha"] = wal.astype(BF16)
    lw["balpha"] = p["b_alpha"][i].reshape(1, 256)
    lw["w_conv"] = p["w_conv"][i]
    lw["b_conv"] = row(p["b_conv"][i])
    lw["dlam"] = p["diff_lambda"][i]
    lw["g_diff"] = jnp.broadcast_to(p["g_diff_out"][i].reshape(N_HEADS, DIFF_DV, 1), (N_HEADS, DIFF_DV, LANE))
    wb = p["w_branch"][i]
    lw["wbr"] = jnp.concatenate([_pad_heads_rows(wb[0], 64), wb[1], wb[2], _pad_heads_rows(wb[3], 64)], axis=0).astype(BF16)
    lw["w_gate"] = p["w_gate"][i].astype(BF16)
    lw["b_gate"] = row(p["b_gate"][i])
    lw["w_out"] = p["w_out"][i].astype(BF16)
    lw["w_router"] = jnp.pad(p["w_router"][i], ((0, 0), (0, LANE - N_EXPERTS))).astype(BF16)
    lw["w_router_t"] = p["w_router"][i].T.astype(BF16)
    lw["layer"] = i
    lw["w_e_gate"], lw["w_e_up"], lw["w_e_down"] = p["w_e_gate"], p["w_e_up"], p["w_e_down"]
    return lw


def _ffn(xm, h2, aff, aff_t, mod, lw):
    b, t, _ = xm.shape
    nblk = t // TOK_BLK
    cap = EC_CAPACITY * t // N_EXPERTS
    capp = -(-cap // TOK_BLK) * TOK_BLK
    pos4, s04 = _topk(aff_t, cap)
    s0_be = jnp.concatenate([s04[..., 0].transpose(0, 2, 1), jnp.full((b, N_EXPERTS, 1), cap, I32)], axis=-1)
    s0_flat = s0_be.reshape(-1)
    ys = _moe(s0_flat, pos4, h2, lw, capp)
    pos_t = pos4.transpose(0, 1, 3, 2).reshape(b, t, N_EXPERTS)
    first = jnp.minimum(s0_be[..., :-1] // SLOT_BLK, capp // SLOT_BLK - 1)
    fits = jnp.all(s0_be[..., 1:] <= (first + 2) * SLOT_BLK)
    args = (s0_flat, ys, pos_t, aff, xm, mod, lw["g_ffn_post"])
    return lax.cond(fits, functools.partial(_combine, sblk=SLOT_BLK), functools.partial(_combine, sblk=TOK_BLK), *args)


def _hybrid_layer(i, x_c, x_l, c8, need_ctx, p):
    lw = _layer_weights(i, p)
    b, t, _ = x_l.shape
    tc = x_c.shape[1]
    lam_init = 0.8 - 0.6 * math.exp(-0.3 * i)
    mod8 = _ada(c8, p["w_ada"][i], p["b_ada"][i])
    pad = lambda m: jnp.pad(m.reshape(b, 6, D), ((0, 0), (0, 2), (0, 0)))
    mod_l = pad(mod8[:b])
    mod_c = pad(jnp.broadcast_to(mod8[b:b + 1], (b, 6 * D)))

    pc = _proj(x_c, mod_c, _identity_table(tc), lw)
    pt = _proj(x_l, mod_l, _rope_table(t), lw)

    zc = jnp.zeros((b, 2, 256, 512), F32)
    zm = jnp.full((b, 2, 8, 256), NEG, F32)
    zs = jnp.zeros((b, 2, 256, 128), F32)
    hf_c, hb_c, c_fin, m_fin = _mlstm(pc, lw["w_conv"], lw["b_conv"], zc, zm)
    hf_l, hb_l, _, _ = _mlstm(pt, lw["w_conv"], lw["b_conv"], c_fin, m_fin)
    gf_c, gb_c, s_fin = _gla(pc, zs)
    gf_l, gb_l, _ = _gla(pt, s_fin)

    one_g = jnp.ones((N_HEADS, MLA_V, LANE), F32)
    fl = functools.partial(_flash, tk=FLASH_KEYS)
    ya_l = fl(pt["mq"], pc["mk"], pc["mvt"], pt["mk"], pt["mvt"], lw["dlam"], one_g, nmap=1, finish=False, post=1.0,
              tq=min(FLASH_ROWS, t))
    yd_l = fl(pt["dq"], pc["dk"], pc["dvt"], pt["dk"], pt["dvt"], lw["dlam"], lw["g_diff"], nmap=2, finish=True,
              post=1.0 - lam_init, tq=min(FLASH_ROWS // 2, t))
    xm, h2, aff, aff_t = _merge(x_l, mod_l, ya_l, (hf_l, hb_l), pt["lo"], (gf_l, gb_l), pt["gr"], yd_l, lw, tm=256)
    x_l = _ffn(xm, h2, aff, aff_t, mod_l, lw)

    if need_ctx:
        ya_c = fl(pc["mq"], pc["mk"], pc["mvt"], None, None, lw["dlam"], one_g, nmap=1, finish=False, post=1.0, tq=tc)
        yd_c = fl(pc["dq"], pc["dk"], pc["dvt"], None, None, lw["dlam"], lw["g_diff"], nmap=2, finish=True,
                  post=1.0 - lam_init, tq=tc)
        xm, h2, aff, aff_t = _merge(x_c, mod_c, ya_c, (hf_c, hb_c), pc["lo"], (gf_c, gb_c), pc["gr"], yd_c, lw, tm=tc)
        x_c = _ffn(xm, h2, aff, aff_t, mod_c, lw)
    return x_c, x_l


def kernel(x, c, ctx, c_ctx, w_ada, b_ada, g_mix_pre, g_mix_post, g_ffn_pre, g_ffn_post, w_in, g_q_lat, w_uq, g_kv_lat, w_ukv, w_conv, b_conv, b_igate, b_fgate, g_mlstm_out, w_alpha2, b_alpha, g_gla_out, diff_lambda, g_diff_out, w_branch, w_gate, b_gate, w_out, w_router, w_e_gate, w_e_up, w_e_down):
    p = dict(w_ada=w_ada, b_ada=b_ada, g_mix_pre=g_mix_pre, g_mix_post=g_mix_post, g_ffn_pre=g_ffn_pre,
             g_ffn_post=g_ffn_post, w_in=w_in, g_q_lat=g_q_lat, w_uq=w_uq, g_kv_lat=g_kv_lat, w_ukv=w_ukv,
             w_conv=w_conv, b_conv=b_conv, b_igate=b_igate, b_fgate=b_fgate, g_mlstm_out=g_mlstm_out,
             w_alpha2=w_alpha2, b_alpha=b_alpha, g_gla_out=g_gla_out, diff_lambda=diff_lambda, g_diff_out=g_diff_out,
             w_branch=w_branch, w_gate=w_gate, b_gate=b_gate, w_out=w_out, w_router=w_router,
             w_e_gate=w_e_gate.astype(BF16), w_e_up=w_e_up.astype(BF16), w_e_down=w_e_down.astype(BF16))
    b = x.shape[0]
    c8 = jnp.concatenate([c, c_ctx[None], jnp.zeros((8 - b - 1, D), F32)], axis=0)
    x_c, x_l = ctx, x
    for i in range(DEPTH):
        x_c, x_l = _hybrid_layer(i, x_c, x_l, c8, i < DEPTH - 1, p)
    return x_l
```

```python
import functools
import math

import numpy as np
import jax
import jax.numpy as jnp
from jax import lax
from jax.experimental import pallas as pl
from jax.experimental.pallas import tpu as pltpu

F32 = jnp.float32
BF16 = jnp.bfloat16
I32 = jnp.int32

D = 1024
DEPTH = 2
GRID_W = 64
N_HEADS = 4
MLA_NOPE, MLA_ROPE, MLA_V = 64, 32, 64
MLA_Q_LORA, MLA_KV_LORA = 256, 128
ML_DH = 64
GLA_DK, GLA_DV, GLA_RANK, GLA_TAU = 32, 64, 16, 16.0
DIFF_DQK, DIFF_DV = 32, 64
ROPE_DIM, ROPE_BASE = 32, 10000.0
N_EXPERTS, EC_CAPACITY, EXPERT_FF = 16, 2, 1408
NEG = -1e30
EPS = 1e-6
LOG2E = 1.4426950408889634

LANE = 128
HEAD_SLAB = 128
TOK_BLK = 256
ML_CHUNK = 128
ML_BLOCK = 256
GLA_CHUNK = 64
GLA_BLOCK = 256
GATHER_WIN = TOK_BLK + 16
GATHER_WIN_SMALL = 64
MOE_TOK = 4096
SLOT_BLK = 128
VMEM_LIMIT = 56 * 1024 * 1024

ZQ, ZKV, ZKRA, ZKRB, ZMLQK, ZMLV, ZMLO, ZGATE, ZGA = 0, 256, 384, 512, 640, 1152, 1408, 1664, 1792
ZGQ, ZGK, ZGV, ZGR, ZDQ, ZDQS, ZDK, ZDKS, NZ = 1920, 2048, 2176, 2432, 2688, 2944, 3200, 3456, 3712
KV_CHUNK = 256
VT_ROWS = 80
FLASH_ROWS = 512
FLASH_KEYS = 512
FLASH_UNROLL = 16


def _swap32(c):
    return (c // 32) * 32 + ((c % 32) ^ 8)


def _win_index():
    idx = -np.ones((NZ,), np.int64)
    idx[ZQ:ZQ + 256] = np.arange(0, 256)
    idx[ZKV:ZKV + 128] = np.arange(256, 384)
    r = np.arange(32)
    idx[ZKRA + 64:ZKRA + 96] = 384 + r
    idx[ZKRB + 64:ZKRB + 96] = 384 + (r ^ 8)
    ml = 416
    idx[ZMLQK:ZMLQK + 512] = ml + np.arange(512)
    idx[ZMLV:ZMLV + 256] = ml + 512 + np.arange(256)
    idx[ZMLO:ZMLO + 256] = ml + 768 + np.arange(256)
    idx[ZGATE:ZGATE + 16] = ml + 1024 + np.arange(16)
    gl = 1456
    idx[ZGQ:ZGQ + 128] = gl + np.arange(128)
    idx[ZGK:ZGK + 128] = gl + 128 + np.arange(128)
    idx[ZGV:ZGV + 256] = gl + 256 + np.arange(256)
    idx[ZGR:ZGR + 256] = gl + 512 + np.arange(256)
    idx[ZGA:ZGA + 32] = gl + 768 + np.arange(32)
    df = 2256
    c = np.arange(256)
    idx[ZDQ:ZDQ + 256] = df + c
    idx[ZDQS:ZDQS + 256] = df + _swap32(c)
    idx[ZDK:ZDK + 256] = df + 256 + c
    idx[ZDKS:ZDKS + 256] = df + 256 + _swap32(c)
    return idx


def _vt_rows(w_cols):
    n = w_cols.shape[0]
    w4 = w_cols.T.reshape(N_HEADS, 64, n)
    return jnp.pad(w4, ((0, 0), (0, VT_ROWS - 64), (0, 0))).reshape(N_HEADS * VT_ROWS, n)


_WIN_IDX = _win_index()


def _gather_cols(w, idx):
    safe = np.maximum(idx, 0)
    return jnp.where(jnp.asarray(idx >= 0)[None, :], w[:, safe], 0.0)


def _pad_heads_rows(w, width):
    n = w.shape[1]
    w4 = w.reshape(N_HEADS, width, n)
    return jnp.pad(w4, ((0, 0), (0, HEAD_SLAB - width), (0, 0))).reshape(N_HEADS * HEAD_SLAB, n)


def _cparams(sem):
    return pltpu.CompilerParams(dimension_semantics=sem, vmem_limit_bytes=VMEM_LIMIT)


def _rms(x):
    return x * lax.rsqrt(jnp.mean(x * x, axis=-1, keepdims=True) + EPS)


def _sigmoid(x):
    return 0.5 * jnp.tanh(0.5 * x) + 0.5


def _log_sigmoid(x):
    return jnp.minimum(x, 0.0) - jnp.log1p(jnp.exp(-jnp.abs(x)))


def _dot(a, b, precision=None):
    return jnp.dot(a, b, preferred_element_type=F32, precision=precision)


def _dot_nt(a, b, precision=None):
    return lax.dot_general(a, b, (((1,), (1,)), ((), ())), preferred_element_type=F32, precision=precision)


def _dot_tn(a, b, precision=None):
    return lax.dot_general(a, b, (((0,), (0,)), ((), ())), preferred_element_type=F32, precision=precision)


def _split_bf16(x, parts):
    out, r = [], x
    for _ in range(parts):
        t = r.astype(BF16)
        out.append(t)
        r = r - t.astype(F32)
    return out


def _dot_sel(sel, x, parts=3):
    return sum(_dot(sel, t) for t in _split_bf16(x, parts))


def _const_spec(shape):
    nd = len(shape)
    return pl.BlockSpec(shape, lambda *_: (0,) * nd)


def _ada_kernel(c_ref, w_ref, b_ref, o_ref):
    cv = c_ref[...]
    s = (cv * _sigmoid(cv)).astype(BF16)
    o_ref[...] = _dot(s, w_ref[...].astype(BF16)) + b_ref[...]


def _ada(c8, w_ada, b_ada):
    n, tn = 6 * D, 1024
    return pl.pallas_call(
        _ada_kernel, name="ada", grid=(n // tn,),
        in_specs=[pl.BlockSpec((8, D), lambda j: (0, 0)), pl.BlockSpec((D, tn), lambda j: (0, j)),
                  pl.BlockSpec((1, tn), lambda j: (0, j))],
        out_specs=pl.BlockSpec((8, tn), lambda j: (0, j)),
        out_shape=jax.ShapeDtypeStruct((8, n), F32), compiler_params=_cparams(("arbitrary",)),
    )(c8, w_ada, b_ada.reshape(1, n))


_PROJ_OUT = (
    ("mq", 512, BF16), ("mk", 512, BF16),
    ("lqk", 512, F32), ("lv", 256, BF16), ("lo", 256, BF16), ("gc", 128, F32),
    ("gq", 128, BF16), ("gk", 128, BF16), ("gv", 256, BF16), ("gr", 256, BF16), ("glg", 256, F32),
    ("dq", 1024, BF16), ("dk", 512, BF16),
)


def _proj_kernel(x_ref, mod_ref, g_ref, w_ref, tab_ref, gq_ref, wq_ref, wqs_ref, gkv_ref, wk_ref, wvt_ref,
                 wgt_ref, gbr_ref, gbc_ref, wal_ref, bal_ref, wdvt_ref,
                 mq_ref, mk_ref, lqk_ref, lv_ref, lo_ref, gc_ref, gq_o, gk_o, gv_o, gr_o, glg_o,
                 dq_ref, dk_ref, grow_ref, mvt_ref, dvt_ref):
    x = x_ref[0]
    tm = x.shape[0]
    mod = mod_ref[0]
    h = _rms(x) * g_ref[...] * (1.0 + mod[1:2]) + mod[0:1]
    hb = h.astype(BF16)
    z = _dot(hb, w_ref[...])
    tab = tab_ref[...]
    ct, st, cd, sd = tab[:, 0:128], tab[:, 128:256], tab[:, 256:384], tab[:, 384:512]
    lane = lax.broadcasted_iota(I32, (tm, LANE), 1)

    qn = (_rms(z[:, ZQ:ZQ + 256]) * gq_ref[...]).astype(BF16)
    qa = _dot(qn, wq_ref[...])
    qb = _dot(qn, wqs_ref[...])
    qscale = (MLA_NOPE + MLA_ROPE) ** -0.5 * LOG2E
    for hh in range(N_HEADS):
        sl = slice(HEAD_SLAB * hh, HEAD_SLAB * (hh + 1))
        mq_ref[0, :, sl] = ((qa[:, sl] * ct + qb[:, sl] * st) * qscale).astype(BF16)
    kvn = (_rms(z[:, ZKV:ZKV + 128]) * gkv_ref[...]).astype(BF16)
    kk = _dot(kvn, wk_ref[...])
    kr = z[:, ZKRA:ZKRA + 128] * ct + z[:, ZKRB:ZKRB + 128] * st
    for hh in range(N_HEADS):
        sl = slice(HEAD_SLAB * hh, HEAD_SLAB * (hh + 1))
        mk_ref[0, :, sl] = (kk[:, sl] + kr).astype(BF16)
    ones_row = lax.broadcasted_iota(I32, (N_HEADS * VT_ROWS, tm), 0) % VT_ROWS == MLA_V
    mvt_ref[0, 0] = jnp.where(ones_row, 1.0, _dot_nt(wvt_ref[...], kvn)).astype(BF16)
    dvt_ref[0, 0] = jnp.where(ones_row, 1.0, _dot_nt(wdvt_ref[...], hb)).astype(BF16)

    lqk_ref[0] = z[:, ZMLQK:ZMLQK + 512]
    lv_ref[0] = z[:, ZMLV:ZMLV + 256].astype(BF16)
    lo_ref[0] = z[:, ZMLO:ZMLO + 256].astype(BF16)
    gcol = z[:, ZGATE:ZGATE + 128] + gbr_ref[...]
    gc_ref[0] = jnp.where(lane < 8, gcol, jnp.where(lane < 16, _log_sigmoid(gcol), 0.0))
    zr = _dot_nt(wgt_ref[...], hb) + gbc_ref[...]
    rowi = lax.broadcasted_iota(I32, zr.shape, 0)
    grow_ref[0] = jnp.where(rowi < 8, zr, _log_sigmoid(zr))

    gq_o[0] = (z[:, ZGQ:ZGQ + 128] * GLA_DK ** -0.5).astype(BF16)
    gk_o[0] = z[:, ZGK:ZGK + 128].astype(BF16)
    gv_o[0] = z[:, ZGV:ZGV + 256].astype(BF16)
    gr_o[0] = z[:, ZGR:ZGR + 256].astype(BF16)
    zg = _dot(z[:, ZGA:ZGA + 128].astype(BF16), wal_ref[...]) + bal_ref[...]
    glg_o[0] = _log_sigmoid(zg) * (1.0 / GLA_TAU)

    dscale = DIFF_DQK ** -0.5 * LOG2E
    for g in range(2):
        gs = slice(128 * g, 128 * (g + 1))
        qg = (z[:, ZDQ:ZDQ + 256][:, gs] * cd + z[:, ZDQS:ZDQS + 256][:, gs] * sd) * dscale
        kg = z[:, ZDK:ZDK + 256][:, gs] * cd + z[:, ZDKS:ZDKS + 256][:, gs] * sd
        for hl in range(2):
            hh = 2 * g + hl
            for m in range(2):
                lo = 64 * hl + 32 * m
                s0 = (2 * hh + m) * HEAD_SLAB
                dq_ref[0, :, s0:s0 + HEAD_SLAB] = jnp.where((lane >= lo) & (lane < lo + 32), qg, 0.0).astype(BF16)
            dk_ref[0, :, HEAD_SLAB * hh:HEAD_SLAB * (hh + 1)] = jnp.where(
                (lane >= 64 * hl) & (lane < 64 * hl + 64), kg, 0.0).astype(BF16)


def _proj(x, mod, tab, lw):
    b, t, _ = x.shape
    tm = KV_CHUNK
    consts = [lw["g_mix_pre"], lw["w_ext"], None, lw["g_q_lat"], lw["wq"], lw["wqs"], lw["g_kv_lat"], lw["wk"], lw["wvt"],
              lw["wgt"], lw["gate_bias_row"], lw["gate_bias_col"], lw["walpha"], lw["balpha"], lw["wdvt"]]
    in_specs = [pl.BlockSpec((1, tm, D), lambda bi, i: (bi, i, 0)), pl.BlockSpec((1, 8, D), lambda bi, i: (bi, 0, 0))]
    args = [x, mod]
    for cst in consts:
        if cst is None:
            in_specs.append(pl.BlockSpec((tm, 512), lambda bi, i: (i, 0)))
            args.append(tab)
        else:
            in_specs.append(_const_spec(cst.shape))
            args.append(cst)
    out_specs = [pl.BlockSpec((1, tm, w), lambda bi, i: (bi, i, 0)) for _, w, _ in _PROJ_OUT]
    out_shape = [jax.ShapeDtypeStruct((b, t, w), dt) for _, w, dt in _PROJ_OUT]
    out_specs.append(pl.BlockSpec((1, 16, tm), lambda bi, i: (bi, 0, i)))
    out_shape.append(jax.ShapeDtypeStruct((b, 16, t), F32))
    for _ in range(2):
        out_specs.append(pl.BlockSpec((1, 1, N_HEADS * VT_ROWS, tm), lambda bi, i: (bi, i, 0, 0)))
        out_shape.append(jax.ShapeDtypeStruct((b, t // tm, N_HEADS * VT_ROWS, tm), BF16))
    outs = pl.pallas_call(
        _proj_kernel, name="proj", grid=(b, t // tm), in_specs=in_specs, out_specs=out_specs, out_shape=out_shape,
        compiler_params=_cparams(("parallel", "arbitrary")),
    )(*args)
    res = {name: o for (name, _, _), o in zip(_PROJ_OUT, outs[:-3])}
    res["grow"], res["mvt"], res["dvt"] = outs[-3:]
    return res


def _flash_kernel(*refs, nmap, has_lat, tk, finish, post):
    if has_lat:
        q_ref, kc_ref, vc_ref, kl_ref, vl_ref, dl_ref, g_ref, o_ref, s_ref = refs
    else:
        q_ref, kc_ref, vc_ref, dl_ref, g_ref, o_ref = refs
    qb = q_ref[0]
    tq = qb.shape[0]
    qt = jnp.concatenate([qb[:, HEAD_SLAB * mm:HEAD_SLAB * (mm + 1)].astype(F32).T for mm in range(nmap)],
                         axis=1).astype(BF16)
    rows = nmap * tq
    sub = tk // KV_CHUNK

    def softmax(s, smax, m):
        m_new = jnp.maximum(m, smax)
        return m_new, jnp.exp2(m - m_new), jnp.exp2(s - m_new).astype(BF16)

    def pv(p, vts):
        out = _dot(vts[0], p[0:KV_CHUNK])
        for c in range(1, len(vts)):
            out = out + _dot(vts[c], p[c * KV_CHUNK:(c + 1) * KV_CHUNK])
        return out

    s_ctx = _dot(kc_ref[0], qt)
    m, _, p = softmax(s_ctx, jnp.max(s_ctx, axis=0, keepdims=True), jnp.full((1, rows), NEG, F32))
    acc = pv(p, [vc_ref[0, 0]])
    if has_lat:
        n = kl_ref.shape[1] // tk

        unroll = min(FLASH_UNROLL, n)

        def scores(j):
            if isinstance(j, int):
                return _dot(kl_ref[0, j * tk:(j + 1) * tk, :], qt)
            off = pl.multiple_of(j * tk, tk)
            return _dot(kl_ref[0, pl.ds(off, tk), :], qt)

        def values(j):
            return [vl_ref[0, j * sub + c] for c in range(sub)]

        def produce(slot, j):
            s = scores(j)
            s_ref[slot] = s
            return jnp.max(s, axis=0, keepdims=True)

        smax0 = produce(0, 0)

        def body(jj, carry):
            m, acc, smax = carry
            j = unroll * jj
            for u in range(unroll):
                smax_next = smax
                if not isinstance(j, int):
                    smax_next = produce((u + 1) % 2, jnp.minimum(j + u + 1, n - 1))
                elif j + u + 1 < n:
                    smax_next = produce((u + 1) % 2, j + u + 1)
                m, alpha, p = softmax(s_ref[u % 2], smax, m)
                acc = alpha * acc + pv(p, values(j + u))
                smax = smax_next
            return m, acc, smax

        if unroll == n:
            m, acc, _ = body(0, (m, acc, smax0))
        else:
            m, acc, _ = lax.fori_loop(0, n // unroll, body, (m, acc, smax0))

    o = acc[0:MLA_V, :] / acc[MLA_V:MLA_V + 1, :]
    if nmap == 2:
        lv = dl_ref[...]
        lam = (jnp.exp(jnp.sum(lv[0:1] * lv[1:2], axis=-1, keepdims=True))
               - jnp.exp(jnp.sum(lv[2:3] * lv[3:4], axis=-1, keepdims=True)) + (1.0 - post))
        o = o[:, :tq] - lam * o[:, tq:]
    if finish:
        ms = jnp.mean(o * o, axis=0, keepdims=True)
        o = o * lax.rsqrt(ms + EPS) * jnp.concatenate([g_ref[0]] * (tq // LANE), axis=1) * post
    o_pad = jnp.concatenate([o, jnp.zeros((HEAD_SLAB - MLA_V, tq), F32)], axis=0)
    o_ref[0] = o_pad.T.astype(BF16)


def _flash(q, kc, vct, kl, vlt, dlam, g_out, *, nmap, finish, post, tq, tk):
    b, t, _ = q.shape
    has_lat = kl is not None
    assert kc.shape[1] == KV_CHUNK and tk % KV_CHUNK == 0
    qw = nmap * HEAD_SLAB
    kspec = lambda n: pl.BlockSpec((1, n, HEAD_SLAB), lambda bi, h, i: (bi, 0, h))
    vspec = lambda n: pl.BlockSpec((1, n // KV_CHUNK, VT_ROWS, KV_CHUNK), lambda bi, h, i: (bi, 0, h, 0))
    in_specs = [pl.BlockSpec((1, tq, qw), lambda bi, h, i: (bi, i, h)), kspec(KV_CHUNK), vspec(KV_CHUNK)]
    args = [q, kc, vct]
    scratch = []
    if has_lat:
        tl = kl.shape[1]
        assert (tl // tk) % min(FLASH_UNROLL, tl // tk) == 0
        in_specs += [kspec(tl), vspec(tl)]
        args += [kl, vlt]
        scratch = [pltpu.VMEM((2, tk, nmap * tq), F32)]
    in_specs += [_const_spec(dlam.shape), pl.BlockSpec((1, MLA_V, LANE), lambda bi, h, i: (h, 0, 0))]
    args += [dlam, g_out]
    return pl.pallas_call(
        functools.partial(_flash_kernel, nmap=nmap, has_lat=has_lat, tk=tk, finish=finish, post=post),
        name="flash_diff" if nmap == 2 else "flash_mla",
        grid=(b, N_HEADS, t // tq), in_specs=in_specs,
        out_specs=pl.BlockSpec((1, tq, HEAD_SLAB), lambda bi, h, i: (bi, i, h)),
        out_shape=jax.ShapeDtypeStruct((b, t, N_HEADS * HEAD_SLAB), BF16),
        scratch_shapes=scratch,
        compiler_params=_cparams(("parallel", "parallel", "arbitrary")),
    )(*args)


def _head_of(shape, axis, width):
    return (lax.broadcasted_iota(I32, shape, axis) % (N_HEADS * width)) // width


def _mlstm_conv(first, last, x, xprev, xnext, wc, bcv):
    n = x.shape[0]
    row = lax.broadcasted_iota(I32, x.shape, 0)
    pr = jnp.where(first, 0.0, xprev[7:8, :])
    nx = jnp.where(last, 0.0, xnext[0:1, :])
    xm = jnp.where(row == 0, pr, pltpu.roll(x, 1, 0))
    xp = jnp.where(row == n - 1, nx, pltpu.roll(x, n - 1, 0))
    y = xm * wc[0:1] + x * wc[1:2] + xp * wc[2:3] + bcv
    qk = y * _sigmoid(y)
    return qk[:, :256], qk[:, 256:] * ML_DH ** -0.5


def _mlstm_dir(d, q, k, v, gcol, grow, cb, m0e):
    L = q.shape[0]
    li = lax.broadcasted_iota(I32, (L, L), 0)
    si = lax.broadcasted_iota(I32, (L, L), 1)
    tin = (si <= li) if d == 0 else (si >= li)
    tinb = tin.astype(BF16)
    bcol = _dot_sel(tinb, gcol)
    brow = sum(_dot_nt(t, tinb) for t in _split_bf16(grow, 3))
    hm256 = _head_of((L, 256), 1, ML_DH)
    hm512 = _head_of((L, 512), 1, ML_DH)
    e_idx = L - 1 if d == 0 else 0

    d_blk, inter_blk = [], []
    for hh in range(N_HEADS):
        c = 4 * d + hh
        bc = bcol[:, 8 + c:9 + c]
        d_blk.append(jnp.where(tin, bc - brow[8 + c:9 + c, :] + grow[c:c + 1, :], NEG))
        inter_blk.append(bc + m0e[0:1, 64 * hh:64 * hh + 1])
    d_st = jnp.concatenate(d_blk, axis=0)
    inter_st = jnp.concatenate(inter_blk, axis=0)
    mt = jnp.maximum(inter_st, jnp.max(d_st, axis=-1, keepdims=True))
    q_st = jnp.concatenate([jnp.where(hm256 == hh, q, 0.0) for hh in range(N_HEADS)], axis=0).astype(BF16)
    s_st = (jnp.exp(d_st - mt) * _dot_nt(q_st, k.astype(BF16))).astype(BF16)
    vext = jnp.concatenate([v, jnp.ones((L, 256), BF16)], axis=1)
    r = _dot(s_st, vext)
    aint = jnp.exp(inter_st - mt)
    p = _dot(q.astype(BF16), cb.astype(BF16))
    tot = jnp.zeros((L, 512), F32)
    mte = jnp.zeros((L, 256), F32)
    for hh in range(N_HEADS):
        rs = slice(hh * L, (hh + 1) * L)
        tot = jnp.where(hm512 == hh, r[rs] + aint[rs] * p, tot)
        mte = jnp.where(hm256 == hh, mt[rs], mte)
    hout = tot[:, :256] / jnp.maximum(jnp.abs(tot[:, 256:]), jnp.exp(-mte))

    wexp = jnp.zeros((L, 256), F32)
    arow = jnp.zeros((1, 512), F32)
    grw = jnp.zeros((1, 512), F32)
    mnew = jnp.zeros((1, 256), F32)
    hr512 = _head_of((1, 512), 1, ML_DH)
    hr256 = _head_of((1, 256), 1, ML_DH)
    for hh in range(N_HEADS):
        c = 4 * d + hh
        bc = bcol[:, 8 + c:9 + c]
        be = bc[e_idx:e_idx + 1, :]
        wl = be - bc + gcol[:, c:c + 1]
        mloc = jnp.max(wl, axis=0, keepdims=True)
        m0h = m0e[0:1, 64 * hh:64 * hh + 1]
        mn = jnp.maximum(be + m0h, mloc)
        wexp = jnp.where(hm256 == hh, jnp.exp(wl - mloc), wexp)
        arow = jnp.where(hr512 == hh, jnp.exp(be + m0h - mn), arow)
        grw = jnp.where(hr512 == hh, jnp.exp(mloc - mn), grw)
        mnew = jnp.where(hr256 == hh, mn, mnew)
    cl = _dot_tn((k * wexp).astype(BF16), vext)
    bd = lax.broadcasted_iota(I32, (256, 512), 0) // ML_DH == _head_of((256, 512), 1, ML_DH)
    return hout, arow * cb + jnp.where(bd, grw * cl, 0.0), jnp.broadcast_to(mnew, (8, 256))


def _mlstm_kernel(xf, xfp, xfn, xb, xbp, xbn, vf, vb, gcf, gcb, grf, grb, wc_ref, bc_ref, c0_ref, m0_ref,
                  hf_ref, hb_ref, c_ref, m_ref):
    i = pl.program_id(1)
    n = pl.num_programs(1)

    @pl.when(i == 0)
    def _():
        c_ref[...] = c0_ref[...]
        m_ref[...] = m0_ref[...]

    wc = wc_ref[...]
    bcv = bc_ref[...]
    L = ML_CHUNK
    nsub = xf.shape[1] // L
    streams = ((0, i == 0, i == n - 1, xf, xfp, xfn, vf, gcf, grf, hf_ref),
               (1, i == n - 1, i == 0, xb, xbp, xbn, vb, gcb, grb, hb_ref))
    for d, first, last, x, xp, xn, v, gc, gr, h_ref in streams:
        q, k = _mlstm_conv(first, last, x[0], xp[0], xn[0], wc, bcv)
        cb, m0e = c_ref[0, d], m_ref[0, d]
        for c in (range(nsub) if d == 0 else reversed(range(nsub))):
            sl = slice(c * L, (c + 1) * L)
            h_ref[0, sl, :], cb, m0e = _mlstm_dir(d, q[sl], k[sl], v[0, sl, :], gc[0, sl, :], gr[0, :, sl], cb, m0e)
        c_ref[0, d], m_ref[0, d] = cb, m0e


def _mlstm(pr, w_conv, b_conv, c0, m0):
    x, v, gc, gr = pr["lqk"], pr["lv"], pr["gc"], pr["grow"]
    b, t, _ = x.shape
    L = min(ML_BLOCK, t)
    n = t // L
    r8 = L // 8
    last8 = t // 8 - 1

    def fw(bi, i):
        return (bi, i, 0)

    def bw(bi, i):
        return (bi, n - 1 - i, 0)

    def halo(ix, shift):
        def f(bi, i):
            blk = ix(bi, i)[1]
            return (bi, jnp.clip(blk * r8 + shift, 0, last8), 0)
        return f

    main = lambda w, ix: pl.BlockSpec((1, L, w), ix)
    in_specs = [main(512, fw), pl.BlockSpec((1, 8, 512), halo(fw, -1)), pl.BlockSpec((1, 8, 512), halo(fw, r8)),
                main(512, bw), pl.BlockSpec((1, 8, 512), halo(bw, -1)), pl.BlockSpec((1, 8, 512), halo(bw, r8)),
                main(256, fw), main(256, bw), main(128, fw), main(128, bw),
                pl.BlockSpec((1, 16, L), lambda bi, i: (bi, 0, i)), pl.BlockSpec((1, 16, L), lambda bi, i: (bi, 0, n - 1 - i)),
                _const_spec(w_conv.shape), _const_spec(b_conv.shape),
                pl.BlockSpec((1, 2, 256, 512), lambda bi, i: (bi, 0, 0, 0)), pl.BlockSpec((1, 2, 8, 256), lambda bi, i: (bi, 0, 0, 0))]
    out_specs = [main(256, fw), main(256, bw),
                 pl.BlockSpec((1, 2, 256, 512), lambda bi, i: (bi, 0, 0, 0)), pl.BlockSpec((1, 2, 8, 256), lambda bi, i: (bi, 0, 0, 0))]
    out_shape = [jax.ShapeDtypeStruct((b, t, 256), F32), jax.ShapeDtypeStruct((b, t, 256), F32),
                 jax.ShapeDtypeStruct(c0.shape, F32), jax.ShapeDtypeStruct(m0.shape, F32)]
    return pl.pallas_call(
        _mlstm_kernel, name="mlstm", grid=(b, n), in_specs=in_specs, out_specs=out_specs, out_shape=out_shape,
        compiler_params=_cparams(("parallel", "arbitrary")),
    )(x, x, x, x, x, x, v, v, gc, gc, gr, gr, w_conv, b_conv, c0, m0)


def _gla_chunk(d, q, k, v, lg, sb):
    L = q.shape[0]
    li = lax.broadcasted_iota(I32, (L, L), 0)
    si = lax.broadcasted_iota(I32, (L, L), 1)
    tin = (si <= li) if d == 0 else (si >= li)
    lgd = lg[:, 128 * d:128 * (d + 1)]
    gcum = _dot_sel(tin.astype(BF16), lgd)
    e_idx = L - 1 if d == 0 else 0
    gend = gcum[e_idx:e_idx + 1, :]
    qf, kf = q.astype(F32), k.astype(F32)
    q_dec = qf * jnp.exp(gcum)
    k_dec = (kf * jnp.exp(-gcum)).astype(BF16)
    k_end = (kf * jnp.exp(gend - gcum)).astype(BF16)
    hm128 = _head_of((L, 128), 1, GLA_DK)
    hm256 = _head_of((L, 256), 1, GLA_DV)
    q_st = jnp.concatenate([jnp.where(hm128 == hh, q_dec, 0.0) for hh in range(N_HEADS)], axis=0).astype(BF16)
    att = _dot_nt(q_st, k_dec)
    tin4 = jnp.concatenate([tin] * N_HEADS, axis=0)
    o_st = _dot(jnp.where(tin4, att, 0.0).astype(BF16), v)
    o = _dot_nt(q_dec.astype(BF16), sb.astype(BF16))
    for hh in range(N_HEADS):
        o = o + jnp.where(hm256 == hh, o_st[hh * L:(hh + 1) * L], 0.0)
    bd = lax.broadcasted_iota(I32, (256, 128), 0) // GLA_DV == _head_of((256, 128), 1, GLA_DK)
    return o, jnp.exp(gend) * sb + jnp.where(bd, _dot_tn(v, k_end), 0.0)


def _gla_kernel(qf, kf, vf, lf, qb, kb, vb, lb, s0_ref, of_ref, ob_ref, s_ref):
    i = pl.program_id(1)

    @pl.when(i == 0)
    def _():
        s_ref[...] = s0_ref[...]

    L = GLA_CHUNK
    nsub = qf.shape[1] // L
    for d, (q, k, v, lg, o_ref) in enumerate(((qf, kf, vf, lf, of_ref), (qb, kb, vb, lb, ob_ref))):
        sb = s_ref[0, d]
        for c in (range(nsub) if d == 0 else reversed(range(nsub))):
            sl = slice(c * L, (c + 1) * L)
            o_ref[0, sl, :], sb = _gla_chunk(d, q[0, sl, :], k[0, sl, :], v[0, sl, :], lg[0, sl, :], sb)
        s_ref[0, d] = sb


def _gla(pr, s0):
    q, k, v, lg = pr["gq"], pr["gk"], pr["gv"], pr["glg"]
    b, t, _ = q.shape
    L = min(GLA_BLOCK, t)
    n = t // L
    fw = lambda bi, i: (bi, i, 0)
    bw = lambda bi, i: (bi, n - 1 - i, 0)
    blk = lambda w, ix: pl.BlockSpec((1, L, w), ix)
    st_spec = pl.BlockSpec((1, 2, 256, 128), lambda bi, i: (bi, 0, 0, 0))
    return pl.pallas_call(
        _gla_kernel, name="gla", grid=(b, n),
        in_specs=[blk(128, fw), blk(128, fw), blk(256, fw), blk(256, fw),
                  blk(128, bw), blk(128, bw), blk(256, bw), blk(256, bw), st_spec],
        out_specs=[blk(256, fw), blk(256, bw), st_spec],
        out_shape=[jax.ShapeDtypeStruct((b, t, 256), F32), jax.ShapeDtypeStruct((b, t, 256), F32),
                   jax.ShapeDtypeStruct(s0.shape, F32)],
        compiler_params=_cparams(("parallel", "arbitrary")),
    )(q, k, v, lg, q, k, v, lg, s0)


def _head_rms_expanded(x, width):
    n = x.shape[1]
    bd = (lax.broadcasted_iota(I32, (n, n), 0) // width == lax.broadcasted_iota(I32, (n, n), 1) // width).astype(BF16)
    return sum(_dot(t, bd) for t in _split_bf16(x * x, 2)) * (1.0 / width)


def _merge_kernel(x_ref, mod_ref, ya_ref, hf_ref, hb_ref, lo_ref, gf_ref, gb_ref, gr_ref, yd_ref,
                  gpre_ref, wg_ref, bg_ref, wbr_ref, wo_ref, gpost_ref, gffn_ref, wr_ref, wrt_ref, gml_ref, ggla_ref,
                  xm_ref, h2_ref, aff_ref, afft_ref):
    x = x_ref[0]
    tm = x.shape[0]
    mod = mod_ref[0]
    hb = (_rms(x) * gpre_ref[...] * (1.0 + mod[1:2]) + mod[0:1]).astype(BF16)

    hs = hf_ref[0] + hb_ref[0]
    y_ml = _sigmoid(lo_ref[0].astype(F32)) * (hs * lax.rsqrt(_head_rms_expanded(hs, ML_DH) + EPS) * gml_ref[...])
    gs = gf_ref[0] + gb_ref[0]
    rr = gr_ref[0].astype(F32)
    y_gla = rr * _sigmoid(rr) * (gs * lax.rsqrt(_head_rms_expanded(gs, GLA_DV) + EPS) * ggla_ref[...])

    branches = ((ya_ref[0], 0, 512), (y_ml.astype(BF16), 512, 256), (y_gla.astype(BF16), 768, 256), (yd_ref[0], 1024, 512))
    mix = jnp.zeros((tm, D), F32)
    for nb, (yb, r0, rw) in enumerate(branches):
        gate = _sigmoid(_dot(hb, wg_ref[:, nb * D:(nb + 1) * D]) + bg_ref[:, nb * D:(nb + 1) * D])
        mix = mix + gate * _dot(yb, wbr_ref[r0:r0 + rw, :])
    y = _dot(mix.astype(BF16), wo_ref[...])
    xm = x + mod[2:3] * (_rms(y) * gpost_ref[...])
    xm_ref[0] = xm

    h2 = (_rms(xm) * gffn_ref[...] * (1.0 + mod[4:5]) + mod[3:4]).astype(BF16)
    h2_ref[0] = h2
    lane = lax.broadcasted_iota(I32, (tm, LANE), 1)
    lg = jnp.where(lane < N_EXPERTS, _dot(h2, wr_ref[...]), NEG)
    e = jnp.exp(lg - jnp.max(lg, axis=-1, keepdims=True))
    aff_ref[0] = (e / jnp.sum(e, axis=-1, keepdims=True))[:, :N_EXPERTS]
    lt = _dot_nt(wrt_ref[...], h2)
    et = jnp.exp(lt - jnp.max(lt, axis=0, keepdims=True))
    afft_ref[0] = et / jnp.sum(et, axis=0, keepdims=True)


def _merge(x, mod, ya, ml, lo, gl, gr, yd, lw, tm):
    b, t, _ = x.shape
    tok = lambda w: pl.BlockSpec((1, tm, w), lambda bi, i: (bi, i, 0))
    consts = [lw["g_mix_pre"], lw["w_gate"], lw["b_gate"], lw["wbr"], lw["w_out"], lw["g_mix_post"], lw["g_ffn_pre"],
              lw["w_router"], lw["w_router_t"], lw["g_mlstm_out"], lw["g_gla_out"]]
    in_specs = [tok(D), pl.BlockSpec((1, 8, D), lambda bi, i: (bi, 0, 0)), tok(512), tok(256), tok(256), tok(256),
                tok(256), tok(256), tok(256), tok(512)] + [_const_spec(c.shape) for c in consts]
    out_specs = [tok(D), tok(D), tok(N_EXPERTS), pl.BlockSpec((1, N_EXPERTS, tm), lambda bi, i: (bi, 0, i))]
    out_shape = [jax.ShapeDtypeStruct((b, t, D), F32), jax.ShapeDtypeStruct((b, t, D), BF16),
                 jax.ShapeDtypeStruct((b, t, N_EXPERTS), F32), jax.ShapeDtypeStruct((b, N_EXPERTS, t), F32)]
    return pl.pallas_call(
        _merge_kernel, name="merge", grid=(b, t // tm), in_specs=in_specs, out_specs=out_specs, out_shape=out_shape,
        compiler_params=_cparams(("parallel", "arbitrary")),
    )(x, mod, ya, ml[0], ml[1], lo, gl[0], gl[1], gr, yd, *consts)


def _topk_kernel(a_ref, pos_ref, s0_ref, *, cap):
    nblk = a_ref.shape[1]
    bits = pltpu.bitcast(a_ref[0], I32)

    def bisect(i, thr):
        cand = thr | (1 << (30 - i))
        cnt = jnp.sum((bits >= cand).astype(I32), axis=(0, 2), keepdims=True)
        return jnp.where(cnt >= cap, cand, thr)

    thr3 = lax.fori_loop(0, 31, bisect, jnp.zeros((1, N_EXPERTS, 1), I32))
    need3 = cap - jnp.sum((bits > thr3).astype(I32), axis=(0, 2), keepdims=True)
    thr, need = thr3[0], need3[0].astype(F32)
    upper = (lax.broadcasted_iota(I32, (TOK_BLK, TOK_BLK), 0) <= lax.broadcasted_iota(I32, (TOK_BLK, TOK_BLK), 1)).astype(BF16)

    def blk(j, carry):
        c_eq, c_sel = carry
        bj = pltpu.bitcast(a_ref[0, j], I32)
        gt, eq = bj > thr, bj == thr
        cum_eq = _dot(eq.astype(BF16), upper) + c_eq
        sel = gt | (eq & (cum_eq <= need))
        cum_sel = _dot(sel.astype(BF16), upper) + c_sel
        pos_ref[0, j] = jnp.where(sel, cum_sel - 1.0, -1.0).astype(I32)
        s0_ref[0, j] = jnp.broadcast_to(c_sel, (N_EXPERTS, LANE)).astype(I32)
        return cum_eq[:, TOK_BLK - 1:TOK_BLK], cum_sel[:, TOK_BLK - 1:TOK_BLK]

    zero = jnp.zeros((N_EXPERTS, 1), F32)
    lax.fori_loop(0, nblk, blk, (zero, zero))


def _topk(aff_t, cap):
    b, _, t = aff_t.shape
    nblk = t // TOK_BLK
    a4 = aff_t.reshape(b, N_EXPERTS, nblk, TOK_BLK).transpose(0, 2, 1, 3)
    spec = lambda w: pl.BlockSpec((1, nblk, N_EXPERTS, w), lambda bi: (bi, 0, 0, 0))
    return pl.pallas_call(
        functools.partial(_topk_kernel, cap=cap), name="topk", grid=(b,),
        in_specs=[spec(TOK_BLK)], out_specs=[spec(TOK_BLK), spec(LANE)],
        out_shape=[jax.ShapeDtypeStruct((b, nblk, N_EXPERTS, TOK_BLK), I32),
                   jax.ShapeDtypeStruct((b, nblk, N_EXPERTS, LANE), I32)],
        compiler_params=_cparams(("parallel",)),
    )(a4)


def _moe_kernel(s0_ref, pos_ref, h_ref, wg_ref, wu_ref, wd_ref, ys_ref, xs_ref, *, nblk, nsub, capp):
    e, bi, tb = pl.program_id(0), pl.program_id(1), pl.program_id(2)

    @pl.when(tb == 0)
    def _():
        xs_ref[...] = jnp.zeros(xs_ref.shape, BF16)

    base = (bi * N_EXPERTS + e) * (nblk + 1) + tb * nsub
    s0s = [s0_ref[base + sb] for sb in range(nsub + 1)]
    a0s = [pl.multiple_of((s0 // 16) * 16, 16) for s0 in s0s[:-1]]
    spans = [s0s[sb + 1] - a0s[sb] for sb in range(nsub)]

    def gather(win, sb):
        a0 = a0s[sb]
        prow = pos_ref[0, sb, pl.ds(e, 1), :]
        slot = lax.broadcasted_iota(I32, (win, TOK_BLK), 0) + a0
        rows = _dot((slot == prow).astype(BF16), h_ref[0, sb * TOK_BLK:(sb + 1) * TOK_BLK, :])
        xs_ref[pl.ds(a0, win), :] = xs_ref[pl.ds(a0, win), :] + rows.astype(BF16)

    all_small = functools.reduce(jnp.logical_and, [sp <= GATHER_WIN_SMALL for sp in spans])

    @pl.when(all_small)
    def _():
        for sb in range(nsub):
            gather(GATHER_WIN_SMALL, sb)

    @pl.when(jnp.logical_not(all_small))
    def _():
        for sb in range(nsub):
            nonempty = s0s[sb + 1] > s0s[sb]
            pl.when(nonempty & (spans[sb] <= GATHER_WIN_SMALL))(functools.partial(gather, GATHER_WIN_SMALL, sb))
            pl.when(spans[sb] > GATHER_WIN_SMALL)(functools.partial(gather, GATHER_WIN, sb))

    @pl.when(tb == pl.num_programs(2) - 1)
    def _():
        def chunk(c, carry):
            off = pl.multiple_of(c * TOK_BLK, TOK_BLK)
            xc = xs_ref[pl.ds(off, TOK_BLK), :]
            hg = _dot(xc, wg_ref[0, 0])
            hid = (hg * _sigmoid(hg) * _dot(xc, wu_ref[0, 0])).astype(BF16)
            ys_ref[0, 0, pl.ds(off, TOK_BLK), :] = _dot(hid, wd_ref[0, 0]).astype(BF16)
            return carry
        lax.fori_loop(0, capp // TOK_BLK, chunk, 0)


def _moe(s0_flat, pos4, h2, lw, capp):
    b, t, _ = h2.shape
    layer = lw["layer"]
    nblk = t // TOK_BLK
    tok = min(MOE_TOK, t)
    nsub = tok // TOK_BLK
    grid_spec = pltpu.PrefetchScalarGridSpec(
        num_scalar_prefetch=1, grid=(N_EXPERTS, b, t // tok),
        in_specs=[pl.BlockSpec((1, nsub, N_EXPERTS, TOK_BLK), lambda e, bi, tb, s: (bi, tb, 0, 0)),
                  pl.BlockSpec((1, tok, D), lambda e, bi, tb, s: (bi, tb, 0)),
                  pl.BlockSpec((1, 1, D, EXPERT_FF), lambda e, bi, tb, s: (layer, e, 0, 0)),
                  pl.BlockSpec((1, 1, D, EXPERT_FF), lambda e, bi, tb, s: (layer, e, 0, 0)),
                  pl.BlockSpec((1, 1, EXPERT_FF, D), lambda e, bi, tb, s: (layer, e, 0, 0))],
        out_specs=pl.BlockSpec((1, 1, capp, D), lambda e, bi, tb, s: (bi, e, 0, 0)),
        scratch_shapes=[pltpu.VMEM((capp + GATHER_WIN, D), BF16)])
    return pl.pallas_call(
        functools.partial(_moe_kernel, nblk=nblk, nsub=nsub, capp=capp), name="moe", grid_spec=grid_spec,
        out_shape=jax.ShapeDtypeStruct((b, N_EXPERTS, capp, D), BF16),
        compiler_params=_cparams(("arbitrary", "arbitrary", "arbitrary")),
    )(s0_flat, pos4, h2, lw["w_e_gate"], lw["w_e_up"], lw["w_e_down"])


def _combine_kernel(s0_ref, *refs, nblk, nb, sblk):
    ys_refs = refs[:2 * N_EXPERTS]
    pos_ref, aff_ref, xm_ref, mod_ref, g_ref, o_ref = refs[2 * N_EXPERTS:]
    bi, tb = pl.program_id(0), pl.program_id(1)
    pos = pos_ref[0]
    aff = aff_ref[0]
    lane = lax.broadcasted_iota(I32, (TOK_BLK, 2 * sblk), 1)
    acc = jnp.zeros((TOK_BLK, D), F32)
    for e in range(N_EXPERTS):
        s0 = s0_ref[(bi * N_EXPERTS + e) * (nblk + 1) + tb]
        blk0 = jnp.minimum(s0 // sblk, nb - 1)
        rel = pos[:, e:e + 1] - blk0 * sblk
        ysw = jnp.concatenate([ys_refs[2 * e][0, 0], ys_refs[2 * e + 1][0, 0]], axis=0)
        acc = acc + aff[:, e:e + 1] * _dot((lane == rel).astype(BF16), ysw)
    mod = mod_ref[0]
    o_ref[0] = xm_ref[0] + mod[5:6] * (_rms(acc) * g_ref[...])


def _combine(s0_flat, ys, pos_t, aff, xm, mod, g_post, sblk):
    b, t, _ = xm.shape
    nblk = t // TOK_BLK
    nb = ys.shape[2] // sblk

    def ys_spec(e, k):
        def ix(bi, tb, s):
            blk0 = jnp.minimum(s[(bi * N_EXPERTS + e) * (nblk + 1) + tb] // sblk, nb - 1)
            return (bi, e, jnp.minimum(blk0 + k, nb - 1), 0)
        return pl.BlockSpec((1, 1, sblk, D), ix)

    tok = lambda w: pl.BlockSpec((1, TOK_BLK, w), lambda bi, tb, s: (bi, tb, 0))
    in_specs = [ys_spec(e, k) for e in range(N_EXPERTS) for k in range(2)]
    in_specs += [tok(N_EXPERTS), tok(N_EXPERTS), tok(D), pl.BlockSpec((1, 8, D), lambda bi, tb, s: (bi, 0, 0)),
                 pl.BlockSpec((1, D), lambda bi, tb, s: (0, 0))]
    grid_spec = pltpu.PrefetchScalarGridSpec(num_scalar_prefetch=1, grid=(b, nblk), in_specs=in_specs, out_specs=tok(D))
    return pl.pallas_call(
        functools.partial(_combine_kernel, nblk=nblk, nb=nb, sblk=sblk), name="combine", grid_spec=grid_spec,
        out_shape=jax.ShapeDtypeStruct((b, t, D), F32),
        compiler_params=_cparams(("arbitrary", "arbitrary")),
    )(s0_flat, *([ys] * (2 * N_EXPERTS)), pos_t, aff, xm, mod, g_post)


def _rope_table(t):
    nf = ROPE_DIM // 4
    pos = jnp.arange(t)
    inv = ROPE_BASE ** (-jnp.arange(nf, dtype=F32) / nf)
    ang = jnp.stack([pos // GRID_W, pos % GRID_W], axis=-1).astype(F32)[..., None] * inv
    cos, sin = jnp.cos(ang), jnp.sin(ang)
    c32 = jnp.stack([cos, cos], axis=2).reshape(t, ROPE_DIM)
    s32 = jnp.stack([-sin, sin], axis=2).reshape(t, ROPE_DIM)
    one, zero = jnp.ones((t, 64), F32), jnp.zeros((t, 32), F32)
    ct = jnp.concatenate([one, c32, zero], axis=1)
    st = jnp.concatenate([0.0 * one, s32, zero], axis=1)
    return jnp.concatenate([ct, st, jnp.tile(c32, (1, 4)), jnp.tile(s32, (1, 4))], axis=1)


def _identity_table(t):
    one, zero = jnp.ones((t, 128), F32), jnp.zeros((t, 128), F32)
    ct = jnp.concatenate([jnp.ones((t, 96), F32), jnp.zeros((t, 32), F32)], axis=1)
    return jnp.concatenate([ct, zero, one, zero], axis=1)


def _layer_weights(i, p):
    lw = {}
    row = lambda a: a.reshape(1, -1)
    for name in ("g_mix_pre", "g_mix_post", "g_ffn_pre", "g_ffn_post", "g_q_lat", "g_kv_lat", "g_mlstm_out", "g_gla_out"):
        lw[name] = row(p[name][i])
    lw["w_ext"] = _gather_cols(p["w_in"][i], _WIN_IDX).astype(BF16)
    lw["wgt"] = p["w_in"][i][:, 416 + 1024:416 + 1040].T.astype(BF16)
    gb = jnp.concatenate([p["b_igate"][i].reshape(-1), p["b_fgate"][i].reshape(-1)])
    lw["gate_bias_row"] = jnp.pad(gb, (0, LANE - 16)).reshape(1, LANE)
    lw["gate_bias_col"] = gb.reshape(16, 1)
    wuq = p["w_uq"][i]
    qi = -np.ones((512,), np.int64)
    qsi = -np.ones((512,), np.int64)
    for h in range(N_HEADS):
        qi[128 * h:128 * h + 96] = 96 * h + np.arange(96)
        qsi[128 * h + 64:128 * h + 96] = 96 * h + 64 + (np.arange(32) ^ 8)
    lw["wq"] = _gather_cols(wuq, qi).astype(BF16)
    lw["wqs"] = _gather_cols(wuq, qsi).astype(BF16)
    ki = -np.ones((512,), np.int64)
    for h in range(N_HEADS):
        ki[128 * h:128 * h + 64] = 128 * h + np.arange(64)
    lw["wk"] = _gather_cols(p["w_ukv"][i], ki).astype(BF16)
    lw["wvt"] = _vt_rows(p["w_ukv"][i].reshape(MLA_KV_LORA, N_HEADS, 128)[:, :, 64:].reshape(MLA_KV_LORA, 256)).astype(BF16)
    lw["wdvt"] = _vt_rows(p["w_in"][i][:, 2768:3024]).astype(BF16)
    wa = p["w_alpha2"][i]
    wal = jnp.zeros((LANE, 256), F32).at[0:16, 0:128].set(wa[0]).at[16:32, 128:256].set(wa[1])
    lw["walpha"] = wal.astype(BF16)
    lw["balpha"] = p["b_alpha"][i].reshape(1, 256)
    lw["w_conv"] = p["w_conv"][i]
    lw["b_conv"] = row(p["b_conv"][i])
    lw["dlam"] = p["diff_lambda"][i]
    lw["g_diff"] = jnp.broadcast_to(p["g_diff_out"][i].reshape(N_HEADS, DIFF_DV, 1), (N_HEADS, DIFF_DV, LANE))
    wb = p["w_branch"][i]
    lw["wbr"] = jnp.concatenate([_pad_heads_rows(wb[0], 64), wb[1], wb[2], _pad_heads_rows(wb[3], 64)], axis=0).astype(BF16)
    lw["w_gate"] = p["w_gate"][i].astype(BF16)
    lw["b_gate"] = row(p["b_gate"][i])
    lw["w_out"] = p["w_out"][i].astype(BF16)
    lw["w_router"] = jnp.pad(p["w_router"][i], ((0, 0), (0, LANE - N_EXPERTS))).astype(BF16)
    lw["w_router_t"] = p["w_router"][i].T.astype(BF16)
    lw["layer"] = i
    lw["w_e_gate"], lw["w_e_up"], lw["w_e_down"] = p["w_e_gate"], p["w_e_up"], p["w_e_down"]
    return lw


def _ffn(xm, h2, aff, aff_t, mod, lw):
    b, t, _ = xm.shape
    nblk = t // TOK_BLK
    cap = EC_CAPACITY * t // N_EXPERTS
    capp = -(-cap // TOK_BLK) * TOK_BLK
    pos4, s04 = _topk(aff_t, cap)
    s0_be = jnp.concatenate([s04[..., 0].transpose(0, 2, 1), jnp.full((b, N_EXPERTS, 1), cap, I32)], axis=-1)
    s0_flat = s0_be.reshape(-1)
    ys = _moe(s0_flat, pos4, h2, lw, capp)
    pos_t = pos4.transpose(0, 1, 3, 2).reshape(b, t, N_EXPERTS)
    first = jnp.minimum(s0_be[..., :-1] // SLOT_BLK, capp // SLOT_BLK - 1)
    fits = jnp.all(s0_be[..., 1:] <= (first + 2) * SLOT_BLK)
    args = (s0_flat, ys, pos_t, aff, xm, mod, lw["g_ffn_post"])
    return lax.cond(fits, functools.partial(_combine, sblk=SLOT_BLK), functools.partial(_combine, sblk=TOK_BLK), *args)


def _hybrid_layer(i, x_c, x_l, c8, need_ctx, p):
    lw = _layer_weights(i, p)
    b, t, _ = x_l.shape
    tc = x_c.shape[1]
    lam_init = 0.8 - 0.6 * math.exp(-0.3 * i)
    mod8 = _ada(c8, p["w_ada"][i], p["b_ada"][i])
    pad = lambda m: jnp.pad(m.reshape(b, 6, D), ((0, 0), (0, 2), (0, 0)))
    mod_l = pad(mod8[:b])
    mod_c = pad(jnp.broadcast_to(mod8[b:b + 1], (b, 6 * D)))

    pc = _proj(x_c, mod_c, _identity_table(tc), lw)
    pt = _proj(x_l, mod_l, _rope_table(t), lw)

    zc = jnp.zeros((b, 2, 256, 512), F32)
    zm = jnp.full((b, 2, 8, 256), NEG, F32)
    zs = jnp.zeros((b, 2, 256, 128), F32)
    hf_c, hb_c, c_fin, m_fin = _mlstm(pc, lw["w_conv"], lw["b_conv"], zc, zm)
    hf_l, hb_l, _, _ = _mlstm(pt, lw["w_conv"], lw["b_conv"], c_fin, m_fin)
    gf_c, gb_c, s_fin = _gla(pc, zs)
    gf_l, gb_l, _ = _gla(pt, s_fin)

    one_g = jnp.ones((N_HEADS, MLA_V, LANE), F32)
    fl = functools.partial(_flash, tk=FLASH_KEYS)
    ya_l = fl(pt["mq"], pc["mk"], pc["mvt"], pt["mk"], pt["mvt"], lw["dlam"], one_g, nmap=1, finish=False, post=1.0,
              tq=min(FLASH_ROWS, t))
    yd_l = fl(pt["dq"], pc["dk"], pc["dvt"], pt["dk"], pt["dvt"], lw["dlam"], lw["g_diff"], nmap=2, finish=True,
              post=1.0 - lam_init, tq=min(FLASH_ROWS // 2, t))
    xm, h2, aff, aff_t = _merge(x_l, mod_l, ya_l, (hf_l, hb_l), pt["lo"], (gf_l, gb_l), pt["gr"], yd_l, lw, tm=256)
    x_l = _ffn(xm, h2, aff, aff_t, mod_l, lw)

    if need_ctx:
        ya_c = fl(pc["mq"], pc["mk"], pc["mvt"], None, None, lw["dlam"], one_g, nmap=1, finish=False, post=1.0, tq=tc)
        yd_c = fl(pc["dq"], pc["dk"], pc["dvt"], None, None, lw["dlam"], lw["g_diff"], nmap=2, finish=True,
                  post=1.0 - lam_init, tq=tc)
        xm, h2, aff, aff_t = _merge(x_c, mod_c, ya_c, (hf_c, hb_c), pc["lo"], (gf_c, gb_c), pc["gr"], yd_c, lw, tm=tc)
        x_c = _ffn(xm, h2, aff, aff_t, mod_c, lw)
    return x_c, x_l


def kernel(x, c, ctx, c_ctx, w_ada, b_ada, g_mix_pre, g_mix_post, g_ffn_pre, g_ffn_post, w_in, g_q_lat, w_uq, g_kv_lat, w_ukv, w_conv, b_conv, b_igate, b_fgate, g_mlstm_out, w_alpha2, b_alpha, g_gla_out, diff_lambda, g_diff_out, w_branch, w_gate, b_gate, w_out, w_router, w_e_gate, w_e_up, w_e_down):
    p = dict(w_ada=w_ada, b_ada=b_ada, g_mix_pre=g_mix_pre, g_mix_post=g_mix_post, g_ffn_pre=g_ffn_pre,
             g_ffn_post=g_ffn_post, w_in=w_in, g_q_lat=g_q_lat, w_uq=w_uq, g_kv_lat=g_kv_lat, w_ukv=w_ukv,
             w_conv=w_conv, b_conv=b_conv, b_igate=b_igate, b_fgate=b_fgate, g_mlstm_out=g_mlstm_out,
             w_alpha2=w_alpha2, b_alpha=b_alpha, g_gla_out=g_gla_out, diff_lambda=diff_lambda, g_diff_out=g_diff_out,
             w_branch=w_branch, w_gate=w_gate, b_gate=b_gate, w_out=w_out, w_router=w_router,
             w_e_gate=w_e_gate.astype(BF16), w_e_up=w_e_up.astype(BF16), w_e_down=w_e_down.astype(BF16))
    b = x.shape[0]
    c8 = jnp.concatenate([c, c_ctx[None], jnp.zeros((8 - b - 1, D), F32)], axis=0)
    x_c, x_l = ctx, x
    for i in range(DEPTH):
        x_c, x_l = _hybrid_layer(i, x_c, x_l, c8, i < DEPTH - 1, p)
    return x_l
```

```python
import functools
import math

import numpy as np
import jax
import jax.numpy as jnp
from jax import lax
from jax.experimental import pallas as pl
from jax.experimental.pallas import tpu as pltpu

F32 = jnp.float32
BF16 = jnp.bfloat16
I32 = jnp.int32

D = 1024
DEPTH = 2
GRID_W = 64
N_HEADS = 4
MLA_NOPE, MLA_ROPE, MLA_V = 64, 32, 64
MLA_Q_LORA, MLA_KV_LORA = 256, 128
ML_DH = 64
GLA_DK, GLA_DV, GLA_RANK, GLA_TAU = 32, 64, 16, 16.0
DIFF_DQK, DIFF_DV = 32, 64
ROPE_DIM, ROPE_BASE = 32, 10000.0
N_EXPERTS, EC_CAPACITY, EXPERT_FF = 16, 2, 1408
NEG = -1e30
EPS = 1e-6
LOG2E = 1.4426950408889634

LANE = 128
HEAD_SLAB = 128
TOK_BLK = 256
ML_CHUNK = 128
ML_BLOCK = 256
GLA_CHUNK = 64
GLA_BLOCK = 256
GATHER_WIN = TOK_BLK + 16
GATHER_WIN_SMALL = 64
MOE_TOK = 4096
SLOT_BLK = 128
VMEM_LIMIT = 56 * 1024 * 1024

ZQ, ZKV, ZKRA, ZKRB, ZMLQK, ZMLV, ZMLO, ZGATE, ZGA = 0, 256, 384, 512, 640, 1152, 1408, 1664, 1792
ZGQ, ZGK, ZGV, ZGR, ZDQ, ZDQS, ZDK, ZDKS, NZ = 1920, 2048, 2176, 2432, 2688, 2944, 3200, 3456, 3712
KV_CHUNK = 256
VT_ROWS = 80
FLASH_ROWS = 512
FLASH_KEYS = 512
FLASH_UNROLL = 16


def _swap32(c):
    return (c // 32) * 32 + ((c % 32) ^ 8)


def _win_index():
    idx = -np.ones((NZ,), np.int64)
    idx[ZQ:ZQ + 256] = np.arange(0, 256)
    idx[ZKV:ZKV + 128] = np.arange(256, 384)
    r = np.arange(32)
    idx[ZKRA + 64:ZKRA + 96] = 384 + r
    idx[ZKRB + 64:ZKRB + 96] = 384 + (r ^ 8)
    ml = 416
    idx[ZMLQK:ZMLQK + 512] = ml + np.arange(512)
    idx[ZMLV:ZMLV + 256] = ml + 512 + np.arange(256)
    idx[ZMLO:ZMLO + 256] = ml + 768 + np.arange(256)
    idx[ZGATE:ZGATE + 16] = ml + 1024 + np.arange(16)
    gl = 1456
    idx[ZGQ:ZGQ + 128] = gl + np.arange(128)
    idx[ZGK:ZGK + 128] = gl + 128 + np.arange(128)
    idx[ZGV:ZGV + 256] = gl + 256 + np.arange(256)
    idx[ZGR:ZGR + 256] = gl + 512 + np.arange(256)
    idx[ZGA:ZGA + 32] = gl + 768 + np.arange(32)
    df = 2256
    c = np.arange(256)
    idx[ZDQ:ZDQ + 256] = df + c
    idx[ZDQS:ZDQS + 256] = df + _swap32(c)
    idx[ZDK:ZDK + 256] = df + 256 + c
    idx[ZDKS:ZDKS + 256] = df + 256 + _swap32(c)
    return idx


def _vt_rows(w_cols):
    n = w_cols.shape[0]
    w4 = w_cols.T.reshape(N_HEADS, 64, n)
    return jnp.pad(w4, ((0, 0), (0, VT_ROWS - 64), (0, 0))).reshape(N_HEADS * VT_ROWS, n)


_WIN_IDX = _win_index()


def _gather_cols(w, idx):
    safe = np.maximum(idx, 0)
    return jnp.where(jnp.asarray(idx >= 0)[None, :], w[:, safe], 0.0)


def _pad_heads_rows(w, width):
    n = w.shape[1]
    w4 = w.reshape(N_HEADS, width, n)
    return jnp.pad(w4, ((0, 0), (0, HEAD_SLAB - width), (0, 0))).reshape(N_HEADS * HEAD_SLAB, n)


def _cparams(sem):
    return pltpu.CompilerParams(dimension_semantics=sem, vmem_limit_bytes=VMEM_LIMIT)


def _rms(x):
    return x * lax.rsqrt(jnp.mean(x * x, axis=-1, keepdims=True) + EPS)


def _sigmoid(x):
    return 0.5 * jnp.tanh(0.5 * x) + 0.5


def _log_sigmoid(x):
    return jnp.minimum(x, 0.0) - jnp.log1p(jnp.exp(-jnp.abs(x)))


def _dot(a, b, precision=None):
    return jnp.dot(a, b, preferred_element_type=F32, precision=precision)


def _dot_nt(a, b, precision=None):
    return lax.dot_general(a, b, (((1,), (1,)), ((), ())), preferred_element_type=F32, precision=precision)


def _dot_tn(a, b, precision=None):
    return lax.dot_general(a, b, (((0,), (0,)), ((), ())), preferred_element_type=F32, precision=precision)


def _split_bf16(x, parts):
    out, r = [], x
    for _ in range(parts):
        t = r.astype(BF16)
        out.append(t)
        r = r - t.astype(F32)
    return out


def _dot_sel(sel, x, parts=3):
    return sum(_dot(sel, t) for t in _split_bf16(x, parts))


def _const_spec(shape):
    nd = len(shape)
    return pl.BlockSpec(shape, lambda *_: (0,) * nd)


def _ada_kernel(c_ref, w_ref, b_ref, o_ref):
    cv = c_ref[...]
    s = (cv * _sigmoid(cv)).astype(BF16)
    o_ref[...] = _dot(s, w_ref[...].astype(BF16)) + b_ref[...]


def _ada(c8, w_ada, b_ada):
    n, tn = 6 * D, 1024
    return pl.pallas_call(
        _ada_kernel, name="ada", grid=(n // tn,),
        in_specs=[pl.BlockSpec((8, D), lambda j: (0, 0)), pl.BlockSpec((D, tn), lambda j: (0, j)),
                  pl.BlockSpec((1, tn), lambda j: (0, j))],
        out_specs=pl.BlockSpec((8, tn), lambda j: (0, j)),
        out_shape=jax.ShapeDtypeStruct((8, n), F32), compiler_params=_cparams(("arbitrary",)),
    )(c8, w_ada, b_ada.reshape(1, n))


_PROJ_OUT = (
    ("mq", 512, BF16), ("mk", 512, BF16),
    ("lqk", 512, F32), ("lv", 256, BF16), ("lo", 256, BF16), ("gc", 128, F32),
    ("gq", 128, BF16), ("gk", 128, BF16), ("gv", 256, BF16), ("gr", 256, BF16), ("glg", 256, F32),
    ("dq", 1024, BF16), ("dk", 512, BF16),
)


def _proj_kernel(x_ref, mod_ref, g_ref, w_ref, tab_ref, gq_ref, wq_ref, wqs_ref, gkv_ref, wk_ref, wvt_ref,
                 wgt_ref, gbr_ref, gbc_ref, wal_ref, bal_ref, wdvt_ref,
                 mq_ref, mk_ref, lqk_ref, lv_ref, lo_ref, gc_ref, gq_o, gk_o, gv_o, gr_o, glg_o,
                 dq_ref, dk_ref, grow_ref, mvt_ref, dvt_ref):
    x = x_ref[0]
    tm = x.shape[0]
    mod = mod_ref[0]
    h = _rms(x) * g_ref[...] * (1.0 + mod[1:2]) + mod[0:1]
    hb = h.astype(BF16)
    z = _dot(hb, w_ref[...])
    tab = tab_ref[...]
    ct, st, cd, sd = tab[:, 0:128], tab[:, 128:256], tab[:, 256:384], tab[:, 384:512]
    lane = lax.broadcasted_iota(I32, (tm, LANE), 1)

    qn = (_rms(z[:, ZQ:ZQ + 256]) * gq_ref[...]).astype(BF16)
    qa = _dot(qn, wq_ref[...])
    qb = _dot(qn, wqs_ref[...])
    qscale = (MLA_NOPE + MLA_ROPE) ** -0.5 * LOG2E
    for hh in range(N_HEADS):
        sl = slice(HEAD_SLAB * hh, HEAD_SLAB * (hh + 1))
        mq_ref[0, :, sl] = ((qa[:, sl] * ct + qb[:, sl] * st) * qscale).astype(BF16)
    kvn = (_rms(z[:, ZKV:ZKV + 128]) * gkv_ref[...]).astype(BF16)
    kk = _dot(kvn, wk_ref[...])
    kr = z[:, ZKRA:ZKRA + 128] * ct + z[:, ZKRB:ZKRB + 128] * st
    for hh in range(N_HEADS):
        sl = slice(HEAD_SLAB * hh, HEAD_SLAB * (hh + 1))
        mk_ref[0, :, sl] = (kk[:, sl] + kr).astype(BF16)
    ones_row = lax.broadcasted_iota(I32, (N_HEADS * VT_ROWS, tm), 0) % VT_ROWS == MLA_V
    mvt_ref[0, 0] = jnp.where(ones_row, 1.0, _dot_nt(wvt_ref[...], kvn)).astype(BF16)
    dvt_ref[0, 0] = jnp.where(ones_row, 1.0, _dot_nt(wdvt_ref[...], hb)).astype(BF16)

    lqk_ref[0] = z[:, ZMLQK:ZMLQK + 512]
    lv_ref[0] = z[:, ZMLV:ZMLV + 256].astype(BF16)
    lo_ref[0] = z[:, ZMLO:ZMLO + 256].astype(BF16)
    gcol = z[:, ZGATE:ZGATE + 128] + gbr_ref[...]
    gc_ref[0] = jnp.where(lane < 8, gcol, jnp.where(lane < 16, _log_sigmoid(gcol), 0.0))
    zr = _dot_nt(wgt_ref[...], hb) + gbc_ref[...]
    rowi = lax.broadcasted_iota(I32, zr.shape, 0)
    grow_ref[0] = jnp.where(rowi < 8, zr, _log_sigmoid(zr))

    gq_o[0] = (z[:, ZGQ:ZGQ + 128] * GLA_DK ** -0.5).astype(BF16)
    gk_o[0] = z[:, ZGK:ZGK + 128].astype(BF16)
    gv_o[0] = z[:, ZGV:ZGV + 256].astype(BF16)
    gr_o[0] = z[:, ZGR:ZGR + 256].astype(BF16)
    zg = _dot(z[:, ZGA:ZGA + 128].astype(BF16), wal_ref[...]) + bal_ref[...]
    glg_o[0] = _log_sigmoid(zg) * (1.0 / GLA_TAU)

    dscale = DIFF_DQK ** -0.5 * LOG2E
    for g in range(2):
        gs = slice(128 * g, 128 * (g + 1))
        qg = (z[:, ZDQ:ZDQ + 256][:, gs] * cd + z[:, ZDQS:ZDQS + 256][:, gs] * sd) * dscale
        kg = z[:, ZDK:ZDK + 256][:, gs] * cd + z[:, ZDKS:ZDKS + 256][:, gs] * sd
        for hl in range(2):
            hh = 2 * g + hl
            for m in range(2):
                lo = 64 * hl + 32 * m
                s0 = (2 * hh + m) * HEAD_SLAB
                dq_ref[0, :, s0:s0 + HEAD_SLAB] = jnp.where((lane >= lo) & (lane < lo + 32), qg, 0.0).astype(BF16)
            dk_ref[0, :, HEAD_SLAB * hh:HEAD_SLAB * (hh + 1)] = jnp.where(
                (lane >= 64 * hl) & (lane < 64 * hl + 64), kg, 0.0).astype(BF16)


def _proj(x, mod, tab, lw):
    b, t, _ = x.shape
    tm = KV_CHUNK
    consts = [lw["g_mix_pre"], lw["w_ext"], None, lw["g_q_lat"], lw["wq"], lw["wqs"], lw["g_kv_lat"], lw["wk"], lw["wvt"],
              lw["wgt"], lw["gate_bias_row"], lw["gate_bias_col"], lw["walpha"], lw["balpha"], lw["wdvt"]]
    in_specs = [pl.BlockSpec((1, tm, D), lambda bi, i: (bi, i, 0)), pl.BlockSpec((1, 8, D), lambda bi, i: (bi, 0, 0))]
    args = [x, mod]
    for cst in consts:
        if cst is None:
            in_specs.append(pl.BlockSpec((tm, 512), lambda bi, i: (i, 0)))
            args.append(tab)
        else:
            in_specs.append(_const_spec(cst.shape))
            args.append(cst)
    out_specs = [pl.BlockSpec((1, tm, w), lambda bi, i: (bi, i, 0)) for _, w, _ in _PROJ_OUT]
    out_shape = [jax.ShapeDtypeStruct((b, t, w), dt) for _, w, dt in _PROJ_OUT]
    out_specs.append(pl.BlockSpec((1, 16, tm), lambda bi, i: (bi, 0, i)))
    out_shape.append(jax.ShapeDtypeStruct((b, 16, t), F32))
    for _ in range(2):
        out_specs.append(pl.BlockSpec((1, 1, N_HEADS * VT_ROWS, tm), lambda bi, i: (bi, i, 0, 0)))
        out_shape.append(jax.ShapeDtypeStruct((b, t // tm, N_HEADS * VT_ROWS, tm), BF16))
    outs = pl.pallas_call(
        _proj_kernel, name="proj", grid=(b, t // tm), in_specs=in_specs, out_specs=out_specs, out_shape=out_shape,
        compiler_params=_cparams(("parallel", "arbitrary")),
    )(*args)
    res = {name: o for (name, _, _), o in zip(_PROJ_OUT, outs[:-3])}
    res["grow"], res["mvt"], res["dvt"] = outs[-3:]
    return res


def _flash_kernel(*refs, nmap, has_lat, tk, finish, post):
    if has_lat:
        q_ref, kc_ref, vc_ref, kl_ref, vl_ref, dl_ref, g_ref, o_ref, s_ref, acc_ref = refs
    else:
        q_ref, kc_ref, vc_ref, dl_ref, g_ref, o_ref = refs
    qb = q_ref[0]
    tq = qb.shape[0]
    qt = jnp.concatenate([qb[:, HEAD_SLAB * mm:HEAD_SLAB * (mm + 1)].astype(F32).T for mm in range(nmap)],
                         axis=1).astype(BF16)
    rows = nmap * tq
    sub = tk // KV_CHUNK

    def softmax(s, smax, m):
        m_new = jnp.maximum(m, smax)
        return m_new, jnp.exp2(m - m_new), jnp.exp2(s - m_new).astype(BF16)

    def pv(p, vts):
        return _dot(vts[0] if len(vts) == 1 else jnp.concatenate(vts, axis=1), p)

    s_ctx = _dot(kc_ref[0], qt)
    if has_lat:
        n = kl_ref.shape[1] // tk
        unroll = min(FLASH_UNROLL, n)

        def scores(j):
            if isinstance(j, int):
                return _dot(kl_ref[0, j * tk:(j + 1) * tk, :], qt)
            off = pl.multiple_of(j * tk, tk)
            return _dot(kl_ref[0, pl.ds(off, tk), :], qt)

        def values(j):
            return [vl_ref[0, j * sub + c] for c in range(sub)]

        def produce(slot, j):
            s = scores(j)
            s_ref[slot] = s
            return jnp.max(s, axis=0, keepdims=True)

        smax0 = produce(0, 0)

    m, _, p = softmax(s_ctx, jnp.max(s_ctx, axis=0, keepdims=True), jnp.full((1, rows), NEG, F32))
    acc = pv(p, [vc_ref[0, 0]])
    if has_lat:
        acc_ref[...] = acc

        def body(jj, carry):
            m, smax = carry
            j = unroll * jj
            for u in range(unroll):
                smax_next = smax
                if not isinstance(j, int):
                    smax_next = produce((u + 1) % 2, jnp.minimum(j + u + 1, n - 1))
                elif j + u + 1 < n:
                    smax_next = produce((u + 1) % 2, j + u + 1)
                m, alpha, p = softmax(s_ref[u % 2], smax, m)
                acc_ref[...] = alpha * acc_ref[...] + pv(p, values(j + u))
                smax = smax_next
            return m, smax

        if unroll == n:
            body(0, (m, smax0))
        else:
            lax.fori_loop(0, n // unroll, body, (m, smax0))
        acc = acc_ref[...]

    o = acc[0:MLA_V, :] / acc[MLA_V:MLA_V + 1, :]
    if nmap == 2:
        lv = dl_ref[...]
        lam = (jnp.exp(jnp.sum(lv[0:1] * lv[1:2], axis=-1, keepdims=True))
               - jnp.exp(jnp.sum(lv[2:3] * lv[3:4], axis=-1, keepdims=True)) + (1.0 - post))
        o = o[:, :tq] - lam * o[:, tq:]
    if finish:
        ms = jnp.mean(o * o, axis=0, keepdims=True)
        o = o * lax.rsqrt(ms + EPS) * jnp.concatenate([g_ref[0]] * (tq // LANE), axis=1) * post
    o_pad = jnp.concatenate([o, jnp.zeros((HEAD_SLAB - MLA_V, tq), F32)], axis=0)
    o_ref[0] = o_pad.T.astype(BF16)


def _flash(q, kc, vct, kl, vlt, dlam, g_out, *, nmap, finish, post, tq, tk):
    b, t, _ = q.shape
    has_lat = kl is not None
    assert kc.shape[1] == KV_CHUNK and tk % KV_CHUNK == 0
    qw = nmap * HEAD_SLAB
    kspec = lambda n: pl.BlockSpec((1, n, HEAD_SLAB), lambda bi, h, i: (bi, 0, h))
    vspec = lambda n: pl.BlockSpec((1, n // KV_CHUNK, VT_ROWS, KV_CHUNK), lambda bi, h, i: (bi, 0, h, 0))
    in_specs = [pl.BlockSpec((1, tq, qw), lambda bi, h, i: (bi, i, h)), kspec(KV_CHUNK), vspec(KV_CHUNK)]
    args = [q, kc, vct]
    scratch = []
    if has_lat:
        tl = kl.shape[1]
        assert (tl // tk) % min(FLASH_UNROLL, tl // tk) == 0
        in_specs += [kspec(tl), vspec(tl)]
        args += [kl, vlt]
        scratch = [pltpu.VMEM((2, tk, nmap * tq), F32), pltpu.VMEM((VT_ROWS, nmap * tq), F32)]
    in_specs += [_const_spec(dlam.shape), pl.BlockSpec((1, MLA_V, LANE), lambda bi, h, i: (h, 0, 0))]
    args += [dlam, g_out]
    return pl.pallas_call(
        functools.partial(_flash_kernel, nmap=nmap, has_lat=has_lat, tk=tk, finish=finish, post=post),
        name="flash_diff" if nmap == 2 else "flash_mla",
        grid=(b, N_HEADS, t // tq), in_specs=in_specs,
        out_specs=pl.BlockSpec((1, tq, HEAD_SLAB), lambda bi, h, i: (bi, i, h)),
        out_shape=jax.ShapeDtypeStruct((b, t, N_HEADS * HEAD_SLAB), BF16),
        scratch_shapes=scratch,
        compiler_params=_cparams(("parallel", "parallel", "arbitrary")),
    )(*args)


def _head_of(shape, axis, width):
    return (lax.broadcasted_iota(I32, shape, axis) % (N_HEADS * width)) // width


def _mlstm_conv(first, last, x, xprev, xnext, wc, bcv):
    n = x.shape[0]
    row = lax.broadcasted_iota(I32, x.shape, 0)
    pr = jnp.where(first, 0.0, xprev[7:8, :])
    nx = jnp.where(last, 0.0, xnext[0:1, :])
    xm = jnp.where(row == 0, pr, pltpu.roll(x, 1, 0))
    xp = jnp.where(row == n - 1, nx, pltpu.roll(x, n - 1, 0))
    y = xm * wc[0:1] + x * wc[1:2] + xp * wc[2:3] + bcv
    qk = y * _sigmoid(y)
    return qk[:, :256], qk[:, 256:] * ML_DH ** -0.5


def _mlstm_dir(d, q, k, v, gcol, grow, cb, m0e):
    L = q.shape[0]
    li = lax.broadcasted_iota(I32, (L, L), 0)
    si = lax.broadcasted_iota(I32, (L, L), 1)
    tin = (si <= li) if d == 0 else (si >= li)
    tinb = tin.astype(BF16)
    bcol = _dot_sel(tinb, gcol)
    brow = sum(_dot_nt(t, tinb) for t in _split_bf16(grow, 3))
    hm256 = _head_of((L, 256), 1, ML_DH)
    hm512 = _head_of((L, 512), 1, ML_DH)
    e_idx = L - 1 if d == 0 else 0

    d_blk, inter_blk = [], []
    for hh in range(N_HEADS):
        c = 4 * d + hh
        bc = bcol[:, 8 + c:9 + c]
        d_blk.append(jnp.where(tin, bc - brow[8 + c:9 + c, :] + grow[c:c + 1, :], NEG))
        inter_blk.append(bc + m0e[0:1, 64 * hh:64 * hh + 1])
    d_st = jnp.concatenate(d_blk, axis=0)
    inter_st = jnp.concatenate(inter_blk, axis=0)
    mt = jnp.maximum(inter_st, jnp.max(d_st, axis=-1, keepdims=True))
    q_st = jnp.concatenate([jnp.where(hm256 == hh, q, 0.0) for hh in range(N_HEADS)], axis=0).astype(BF16)
    s_st = (jnp.exp(d_st - mt) * _dot_nt(q_st, k.astype(BF16))).astype(BF16)
    vext = jnp.concatenate([v, jnp.ones((L, 256), BF16)], axis=1)
    r = _dot(s_st, vext)
    aint = jnp.exp(inter_st - mt)
    p = _dot(q.astype(BF16), cb.astype(BF16))
    tot = jnp.zeros((L, 512), F32)
    mte = jnp.zeros((L, 256), F32)
    for hh in range(N_HEADS):
        rs = slice(hh * L, (hh + 1) * L)
        tot = jnp.where(hm512 == hh, r[rs] + aint[rs] * p, tot)
        mte = jnp.where(hm256 == hh, mt[rs], mte)
    hout = tot[:, :256] / jnp.maximum(jnp.abs(tot[:, 256:]), jnp.exp(-mte))

    wexp = jnp.zeros((L, 256), F32)
    arow = jnp.zeros((1, 512), F32)
    grw = jnp.zeros((1, 512), F32)
    mnew = jnp.zeros((1, 256), F32)
    hr512 = _head_of((1, 512), 1, ML_DH)
    hr256 = _head_of((1, 256), 1, ML_DH)
    for hh in range(N_HEADS):
        c = 4 * d + hh
        bc = bcol[:, 8 + c:9 + c]
        be = bc[e_idx:e_idx + 1, :]
        wl = be - bc + gcol[:, c:c + 1]
        mloc = jnp.max(wl, axis=0, keepdims=True)
        m0h = m0e[0:1, 64 * hh:64 * hh + 1]
        mn = jnp.maximum(be + m0h, mloc)
        wexp = jnp.where(hm256 == hh, jnp.exp(wl - mloc), wexp)
        arow = jnp.where(hr512 == hh, jnp.exp(be + m0h - mn), arow)
        grw = jnp.where(hr512 == hh, jnp.exp(mloc - mn), grw)
        mnew = jnp.where(hr256 == hh, mn, mnew)
    cl = _dot_tn((k * wexp).astype(BF16), vext)
    bd = lax.broadcasted_iota(I32, (256, 512), 0) // ML_DH == _head_of((256, 512), 1, ML_DH)
    return hout, arow * cb + jnp.where(bd, grw * cl, 0.0), jnp.broadcast_to(mnew, (8, 256))


def _mlstm_kernel(xf, xfp, xfn, xb, xbp, xbn, vf, vb, gcf, gcb, grf, grb, wc_ref, bc_ref, c0_ref, m0_ref,
                  hf_ref, hb_ref, c_ref, m_ref):
    i = pl.program_id(1)
    n = pl.num_programs(1)

    @pl.when(i == 0)
    def _():
        c_ref[...] = c0_ref[...]
        m_ref[...] = m0_ref[...]

    wc = wc_ref[...]
    bcv = bc_ref[...]
    L = ML_CHUNK
    nsub = xf.shape[1] // L
    streams = ((0, i == 0, i == n - 1, xf, xfp, xfn, vf, gcf, grf, hf_ref),
               (1, i == n - 1, i == 0, xb, xbp, xbn, vb, gcb, grb, hb_ref))
    for d, first, last, x, xp, xn, v, gc, gr, h_ref in streams:
        q, k = _mlstm_conv(first, last, x[0], xp[0], xn[0], wc, bcv)
        cb, m0e = c_ref[0, d], m_ref[0, d]
        for c in (range(nsub) if d == 0 else reversed(range(nsub))):
            sl = slice(c * L, (c + 1) * L)
            h_ref[0, sl, :], cb, m0e = _mlstm_dir(d, q[sl], k[sl], v[0, sl, :], gc[0, sl, :], gr[0, :, sl], cb, m0e)
        c_ref[0, d], m_ref[0, d] = cb, m0e


def _mlstm(pr, w_conv, b_conv, c0, m0):
    x, v, gc, gr = pr["lqk"], pr["lv"], pr["gc"], pr["grow"]
    b, t, _ = x.shape
    L = min(ML_BLOCK, t)
    n = t // L
    r8 = L // 8
    last8 = t // 8 - 1

    def fw(bi, i):
        return (bi, i, 0)

    def bw(bi, i):
        return (bi, n - 1 - i, 0)

    def halo(ix, shift):
        def f(bi, i):
            blk = ix(bi, i)[1]
            return (bi, jnp.clip(blk * r8 + shift, 0, last8), 0)
        return f

    main = lambda w, ix: pl.BlockSpec((1, L, w), ix)
    in_specs = [main(512, fw), pl.BlockSpec((1, 8, 512), halo(fw, -1)), pl.BlockSpec((1, 8, 512), halo(fw, r8)),
                main(512, bw), pl.BlockSpec((1, 8, 512), halo(bw, -1)), pl.BlockSpec((1, 8, 512), halo(bw, r8)),
                main(256, fw), main(256, bw), main(128, fw), main(128, bw),
                pl.BlockSpec((1, 16, L), lambda bi, i: (bi, 0, i)), pl.BlockSpec((1, 16, L), lambda bi, i: (bi, 0, n - 1 - i)),
                _const_spec(w_conv.shape), _const_spec(b_conv.shape),
                pl.BlockSpec((1, 2, 256, 512), lambda bi, i: (bi, 0, 0, 0)), pl.BlockSpec((1, 2, 8, 256), lambda bi, i: (bi, 0, 0, 0))]
    out_specs = [main(256, fw), main(256, bw),
                 pl.BlockSpec((1, 2, 256, 512), lambda bi, i: (bi, 0, 0, 0)), pl.BlockSpec((1, 2, 8, 256), lambda bi, i: (bi, 0, 0, 0))]
    out_shape = [jax.ShapeDtypeStruct((b, t, 256), F32), jax.ShapeDtypeStruct((b, t, 256), F32),
                 jax.ShapeDtypeStruct(c0.shape, F32), jax.ShapeDtypeStruct(m0.shape, F32)]
    return pl.pallas_call(
        _mlstm_kernel, name="mlstm", grid=(b, n), in_specs=in_specs, out_specs=out_specs, out_shape=out_shape,
        compiler_params=_cparams(("parallel", "arbitrary")),
    )(x, x, x, x, x, x, v, v, gc, gc, gr, gr, w_conv, b_conv, c0, m0)


def _gla_chunk(d, q, k, v, lg, sb):
    L = q.shape[0]
    li = lax.broadcasted_iota(I32, (L, L), 0)
    si = lax.broadcasted_iota(I32, (L, L), 1)
    tin = (si <= li) if d == 0 else (si >= li)
    lgd = lg[:, 128 * d:128 * (d + 1)]
    gcum = _dot_sel(tin.astype(BF16), lgd)
    e_idx = L - 1 if d == 0 else 0
    gend = gcum[e_idx:e_idx + 1, :]
    qf, kf = q.astype(F32), k.astype(F32)
    q_dec = qf * jnp.exp(gcum)
    k_dec = (kf * jnp.exp(-gcum)).astype(BF16)
    k_end = (kf * jnp.exp(gend - gcum)).astype(BF16)
    hm128 = _head_of((L, 128), 1, GLA_DK)
    hm256 = _head_of((L, 256), 1, GLA_DV)
    q_st = jnp.concatenate([jnp.where(hm128 == hh, q_dec, 0.0) for hh in range(N_HEADS)], axis=0).astype(BF16)
    att = _dot_nt(q_st, k_dec)
    tin4 = jnp.concatenate([tin] * N_HEADS, axis=0)
    o_st = _dot(jnp.where(tin4, att, 0.0).astype(BF16), v)
    o = _dot_nt(q_dec.astype(BF16), sb.astype(BF16))
    for hh in range(N_HEADS):
        o = o + jnp.where(hm256 == hh, o_st[hh * L:(hh + 1) * L], 0.0)
    bd = lax.broadcasted_iota(I32, (256, 128), 0) // GLA_DV == _head_of((256, 128), 1, GLA_DK)
    return o, jnp.exp(gend) * sb + jnp.where(bd, _dot_tn(v, k_end), 0.0)


def _gla_kernel(qf, kf, vf, lf, qb, kb, vb, lb, s0_ref, of_ref, ob_ref, s_ref):
    i = pl.program_id(1)

    @pl.when(i == 0)
    def _():
        s_ref[...] = s0_ref[...]

    L = GLA_CHUNK
    nsub = qf.shape[1] // L
    for d, (q, k, v, lg, o_ref) in enumerate(((qf, kf, vf, lf, of_ref), (qb, kb, vb, lb, ob_ref))):
        sb = s_ref[0, d]
        for c in (range(nsub) if d == 0 else reversed(range(nsub))):
            sl = slice(c * L, (c + 1) * L)
            o_ref[0, sl, :], sb = _gla_chunk(d, q[0, sl, :], k[0, sl, :], v[0, sl, :], lg[0, sl, :], sb)
        s_ref[0, d] = sb


def _gla(pr, s0):
    q, k, v, lg = pr["gq"], pr["gk"], pr["gv"], pr["glg"]
    b, t, _ = q.shape
    L = min(GLA_BLOCK, t)
    n = t // L
    fw = lambda bi, i: (bi, i, 0)
    bw = lambda bi, i: (bi, n - 1 - i, 0)
    blk = lambda w, ix: pl.BlockSpec((1, L, w), ix)
    st_spec = pl.BlockSpec((1, 2, 256, 128), lambda bi, i: (bi, 0, 0, 0))
    return pl.pallas_call(
        _gla_kernel, name="gla", grid=(b, n),
        in_specs=[blk(128, fw), blk(128, fw), blk(256, fw), blk(256, fw),
                  blk(128, bw), blk(128, bw), blk(256, bw), blk(256, bw), st_spec],
        out_specs=[blk(256, fw), blk(256, bw), st_spec],
        out_shape=[jax.ShapeDtypeStruct((b, t, 256), F32), jax.ShapeDtypeStruct((b, t, 256), F32),
                   jax.ShapeDtypeStruct(s0.shape, F32)],
        compiler_params=_cparams(("parallel", "arbitrary")),
    )(q, k, v, lg, q, k, v, lg, s0)


def _head_rms_expanded(x, width):
    n = x.shape[1]
    bd = (lax.broadcasted_iota(I32, (n, n), 0) // width == lax.broadcasted_iota(I32, (n, n), 1) // width).astype(BF16)
    return sum(_dot(t, bd) for t in _split_bf16(x * x, 2)) * (1.0 / width)


def _merge_kernel(x_ref, mod_ref, ya_ref, hf_ref, hb_ref, lo_ref, gf_ref, gb_ref, gr_ref, yd_ref,
                  gpre_ref, wg_ref, bg_ref, wbr_ref, wo_ref, gpost_ref, gffn_ref, wr_ref, wrt_ref, gml_ref, ggla_ref,
                  xm_ref, h2_ref, aff_ref, afft_ref):
    x = x_ref[0]
    tm = x.shape[0]
    mod = mod_ref[0]
    hb = (_rms(x) * gpre_ref[...] * (1.0 + mod[1:2]) + mod[0:1]).astype(BF16)

    hs = hf_ref[0] + hb_ref[0]
    y_ml = _sigmoid(lo_ref[0].astype(F32)) * (hs * lax.rsqrt(_head_rms_expanded(hs, ML_DH) + EPS) * gml_ref[...])
    gs = gf_ref[0] + gb_ref[0]
    rr = gr_ref[0].astype(F32)
    y_gla = rr * _sigmoid(rr) * (gs * lax.rsqrt(_head_rms_expanded(gs, GLA_DV) + EPS) * ggla_ref[...])

    branches = ((ya_ref[0], 0, 512), (y_ml.astype(BF16), 512, 256), (y_gla.astype(BF16), 768, 256), (yd_ref[0], 1024, 512))
    mix = jnp.zeros((tm, D), F32)
    for nb, (yb, r0, rw) in enumerate(branches):
        gate = _sigmoid(_dot(hb, wg_ref[:, nb * D:(nb + 1) * D]) + bg_ref[:, nb * D:(nb + 1) * D])
        mix = mix + gate * _dot(yb, wbr_ref[r0:r0 + rw, :])
    y = _dot(mix.astype(BF16), wo_ref[...])
    xm = x + mod[2:3] * (_rms(y) * gpost_ref[...])
    xm_ref[0] = xm

    h2 = (_rms(xm) * gffn_ref[...] * (1.0 + mod[4:5]) + mod[3:4]).astype(BF16)
    h2_ref[0] = h2
    lane = lax.broadcasted_iota(I32, (tm, LANE), 1)
    lg = jnp.where(lane < N_EXPERTS, _dot(h2, wr_ref[...]), NEG)
    e = jnp.exp(lg - jnp.max(lg, axis=-1, keepdims=True))
    aff_ref[0] = (e / jnp.sum(e, axis=-1, keepdims=True))[:, :N_EXPERTS]
    lt = _dot_nt(wrt_ref[...], h2)
    et = jnp.exp(lt - jnp.max(lt, axis=0, keepdims=True))
    afft_ref[0] = et / jnp.sum(et, axis=0, keepdims=True)


def _merge(x, mod, ya, ml, lo, gl, gr, yd, lw, tm):
    b, t, _ = x.shape
    tok = lambda w: pl.BlockSpec((1, tm, w), lambda bi, i: (bi, i, 0))
    consts = [lw["g_mix_pre"], lw["w_gate"], lw["b_gate"], lw["wbr"], lw["w_out"], lw["g_mix_post"], lw["g_ffn_pre"],
              lw["w_router"], lw["w_router_t"], lw["g_mlstm_out"], lw["g_gla_out"]]
    in_specs = [tok(D), pl.BlockSpec((1, 8, D), lambda bi, i: (bi, 0, 0)), tok(512), tok(256), tok(256), tok(256),
                tok(256), tok(256), tok(256), tok(512)] + [_const_spec(c.shape) for c in consts]
    out_specs = [tok(D), tok(D), tok(N_EXPERTS), pl.BlockSpec((1, N_EXPERTS, tm), lambda bi, i: (bi, 0, i))]
    out_shape = [jax.ShapeDtypeStruct((b, t, D), F32), jax.ShapeDtypeStruct((b, t, D), BF16),
                 jax.ShapeDtypeStruct((b, t, N_EXPERTS), F32), jax.ShapeDtypeStruct((b, N_EXPERTS, t), F32)]
    return pl.pallas_call(
        _merge_kernel, name="merge", grid=(b, t // tm), in_specs=in_specs, out_specs=out_specs, out_shape=out_shape,
        compiler_params=_cparams(("parallel", "arbitrary")),
    )(x, mod, ya, ml[0], ml[1], lo, gl[0], gl[1], gr, yd, *consts)


def _topk_kernel(a_ref, pos_ref, s0_ref, *, cap):
    nblk = a_ref.shape[1]
    bits = pltpu.bitcast(a_ref[0], I32)

    def bisect(i, thr):
        cand = thr | (1 << (30 - i))
        cnt = jnp.sum((bits >= cand).astype(I32), axis=(0, 2), keepdims=True)
        return jnp.where(cnt >= cap, cand, thr)

    thr3 = lax.fori_loop(0, 31, bisect, jnp.zeros((1, N_EXPERTS, 1), I32))
    need3 = cap - jnp.sum((bits > thr3).astype(I32), axis=(0, 2), keepdims=True)
    thr, need = thr3[0], need3[0].astype(F32)
    upper = (lax.broadcasted_iota(I32, (TOK_BLK, TOK_BLK), 0) <= lax.broadcasted_iota(I32, (TOK_BLK, TOK_BLK), 1)).astype(BF16)

    def blk(j, carry):
        c_eq, c_sel = carry
        bj = pltpu.bitcast(a_ref[0, j], I32)
        gt, eq = bj > thr, bj == thr
        cum_eq = _dot(eq.astype(BF16), upper) + c_eq
        sel = gt | (eq & (cum_eq <= need))
        cum_sel = _dot(sel.astype(BF16), upper) + c_sel
        pos_ref[0, j] = jnp.where(sel, cum_sel - 1.0, -1.0).astype(I32)
        s0_ref[0, j] = jnp.broadcast_to(c_sel, (N_EXPERTS, LANE)).astype(I32)
        return cum_eq[:, TOK_BLK - 1:TOK_BLK], cum_sel[:, TOK_BLK - 1:TOK_BLK]

    zero = jnp.zeros((N_EXPERTS, 1), F32)
    lax.fori_loop(0, nblk, blk, (zero, zero))


def _topk(aff_t, cap):
    b, _, t = aff_t.shape
    nblk = t // TOK_BLK
    a4 = aff_t.reshape(b, N_EXPERTS, nblk, TOK_BLK).transpose(0, 2, 1, 3)
    spec = lambda w: pl.BlockSpec((1, nblk, N_EXPERTS, w), lambda bi: (bi, 0, 0, 0))
    return pl.pallas_call(
        functools.partial(_topk_kernel, cap=cap), name="topk", grid=(b,),
        in_specs=[spec(TOK_BLK)], out_specs=[spec(TOK_BLK), spec(LANE)],
        out_shape=[jax.ShapeDtypeStruct((b, nblk, N_EXPERTS, TOK_BLK), I32),
                   jax.ShapeDtypeStruct((b, nblk, N_EXPERTS, LANE), I32)],
        compiler_params=_cparams(("parallel",)),
    )(a4)


def _moe_kernel(s0_ref, pos_ref, h_ref, wg_ref, wu_ref, wd_ref, ys_ref, xs_ref, *, nblk, nsub, capp):
    e, bi, tb = pl.program_id(0), pl.program_id(1), pl.program_id(2)

    @pl.when(tb == 0)
    def _():
        xs_ref[...] = jnp.zeros(xs_ref.shape, BF16)

    base = (bi * N_EXPERTS + e) * (nblk + 1) + tb * nsub
    s0s = [s0_ref[base + sb] for sb in range(nsub + 1)]
    a0s = [pl.multiple_of((s0 // 16) * 16, 16) for s0 in s0s[:-1]]
    spans = [s0s[sb + 1] - a0s[sb] for sb in range(nsub)]

    def gather(win, sb):
        a0 = a0s[sb]
        prow = pos_ref[0, sb, pl.ds(e, 1), :]
        slot = lax.broadcasted_iota(I32, (win, TOK_BLK), 0) + a0
        rows = _dot((slot == prow).astype(BF16), h_ref[0, sb * TOK_BLK:(sb + 1) * TOK_BLK, :])
        xs_ref[pl.ds(a0, win), :] = xs_ref[pl.ds(a0, win), :] + rows.astype(BF16)

    all_small = functools.reduce(jnp.logical_and, [sp <= GATHER_WIN_SMALL for sp in spans])

    @pl.when(all_small)
    def _():
        for sb in range(nsub):
            gather(GATHER_WIN_SMALL, sb)

    @pl.when(jnp.logical_not(all_small))
    def _():
        for sb in range(nsub):
            nonempty = s0s[sb + 1] > s0s[sb]
            pl.when(nonempty & (spans[sb] <= GATHER_WIN_SMALL))(functools.partial(gather, GATHER_WIN_SMALL, sb))
            pl.when(spans[sb] > GATHER_WIN_SMALL)(functools.partial(gather, GATHER_WIN, sb))

    @pl.when(tb == pl.num_programs(2) - 1)
    def _():
        def chunk(c, carry):
            off = pl.multiple_of(c * TOK_BLK, TOK_BLK)
            xc = xs_ref[pl.ds(off, TOK_BLK), :]
            hg = _dot(xc, wg_ref[0, 0])
            hid = (hg * _sigmoid(hg) * _dot(xc, wu_ref[0, 0])).astype(BF16)
            ys_ref[0, 0, pl.ds(off, TOK_BLK), :] = _dot(hid, wd_ref[0, 0]).astype(BF16)
            return carry
        lax.fori_loop(0, capp // TOK_BLK, chunk, 0)


def _moe(s0_flat, pos4, h2, lw, capp):
    b, t, _ = h2.shape
    layer = lw["layer"]
    nblk = t // TOK_BLK
    tok = min(MOE_TOK, t)
    nsub = tok // TOK_BLK
    grid_spec = pltpu.PrefetchScalarGridSpec(
        num_scalar_prefetch=1, grid=(N_EXPERTS, b, t // tok),
        in_specs=[pl.BlockSpec((1, nsub, N_EXPERTS, TOK_BLK), lambda e, bi, tb, s: (bi, tb, 0, 0)),
                  pl.BlockSpec((1, tok, D), lambda e, bi, tb, s: (bi, tb, 0)),
                  pl.BlockSpec((1, 1, D, EXPERT_FF), lambda e, bi, tb, s: (layer, e, 0, 0)),
                  pl.BlockSpec((1, 1, D, EXPERT_FF), lambda e, bi, tb, s: (layer, e, 0, 0)),
                  pl.BlockSpec((1, 1, EXPERT_FF, D), lambda e, bi, tb, s: (layer, e, 0, 0))],
        out_specs=pl.BlockSpec((1, 1, capp, D), lambda e, bi, tb, s: (bi, e, 0, 0)),
        scratch_shapes=[pltpu.VMEM((capp + GATHER_WIN, D), BF16)])
    return pl.pallas_call(
        functools.partial(_moe_kernel, nblk=nblk, nsub=nsub, capp=capp), name="moe", grid_spec=grid_spec,
        out_shape=jax.ShapeDtypeStruct((b, N_EXPERTS, capp, D), BF16),
        compiler_params=_cparams(("arbitrary", "arbitrary", "arbitrary")),
    )(s0_flat, pos4, h2, lw["w_e_gate"], lw["w_e_up"], lw["w_e_down"])


def _combine_kernel(s0_ref, *refs, nblk, nb, sblk):
    ys_refs = refs[:2 * N_EXPERTS]
    pos_ref, aff_ref, xm_ref, mod_ref, g_ref, o_ref = refs[2 * N_EXPERTS:]
    bi, tb = pl.program_id(0), pl.program_id(1)
    pos = pos_ref[0]
    aff = aff_ref[0]
    lane = lax.broadcasted_iota(I32, (TOK_BLK, 2 * sblk), 1)
    acc = jnp.zeros((TOK_BLK, D), F32)
    for e in range(N_EXPERTS):
        s0 = s0_ref[(bi * N_EXPERTS + e) * (nblk + 1) + tb]
        blk0 = jnp.minimum(s0 // sblk, nb - 1)
        rel = pos[:, e:e + 1] - blk0 * sblk
        ysw = jnp.concatenate([ys_refs[2 * e][0, 0], ys_refs[2 * e + 1][0, 0]], axis=0)
        acc = acc + aff[:, e:e + 1] * _dot((lane == rel).astype(BF16), ysw)
    mod = mod_ref[0]
    o_ref[0] = xm_ref[0] + mod[5:6] * (_rms(acc) * g_ref[...])


def _combine(s0_flat, ys, pos_t, aff, xm, mod, g_post, sblk):
    b, t, _ = xm.shape
    nblk = t // TOK_BLK
    nb = ys.shape[2] // sblk

    def ys_spec(e, k):
        def ix(bi, tb, s):
            blk0 = jnp.minimum(s[(bi * N_EXPERTS + e) * (nblk + 1) + tb] // sblk, nb - 1)
            return (bi, e, jnp.minimum(blk0 + k, nb - 1), 0)
        return pl.BlockSpec((1, 1, sblk, D), ix)

    tok = lambda w: pl.BlockSpec((1, TOK_BLK, w), lambda bi, tb, s: (bi, tb, 0))
    in_specs = [ys_spec(e, k) for e in range(N_EXPERTS) for k in range(2)]
    in_specs += [tok(N_EXPERTS), tok(N_EXPERTS), tok(D), pl.BlockSpec((1, 8, D), lambda bi, tb, s: (bi, 0, 0)),
                 pl.BlockSpec((1, D), lambda bi, tb, s: (0, 0))]
    grid_spec = pltpu.PrefetchScalarGridSpec(num_scalar_prefetch=1, grid=(b, nblk), in_specs=in_specs, out_specs=tok(D))
    return pl.pallas_call(
        functools.partial(_combine_kernel, nblk=nblk, nb=nb, sblk=sblk), name="combine", grid_spec=grid_spec,
        out_shape=jax.ShapeDtypeStruct((b, t, D), F32),
        compiler_params=_cparams(("arbitrary", "arbitrary")),
    )(s0_flat, *([ys] * (2 * N_EXPERTS)), pos_t, aff, xm, mod, g_post)


def _rope_table(t):
    nf = ROPE_DIM // 4
    pos = jnp.arange(t)
    inv = ROPE_BASE ** (-jnp.arange(nf, dtype=F32) / nf)
    ang = jnp.stack([pos // GRID_W, pos % GRID_W], axis=-1).astype(F32)[..., None] * inv
    cos, sin = jnp.cos(ang), jnp.sin(ang)
    c32 = jnp.stack([cos, cos], axis=2).reshape(t, ROPE_DIM)
    s32 = jnp.stack([-sin, sin], axis=2).reshape(t, ROPE_DIM)
    one, zero = jnp.ones((t, 64), F32), jnp.zeros((t, 32), F32)
    ct = jnp.concatenate([one, c32, zero], axis=1)
    st = jnp.concatenate([0.0 * one, s32, zero], axis=1)
    return jnp.concatenate([ct, st, jnp.tile(c32, (1, 4)), jnp.tile(s32, (1, 4))], axis=1)


def _identity_table(t):
    one, zero = jnp.ones((t, 128), F32), jnp.zeros((t, 128), F32)
    ct = jnp.concatenate([jnp.ones((t, 96), F32), jnp.zeros((t, 32), F32)], axis=1)
    return jnp.concatenate([ct, zero, one, zero], axis=1)


def _layer_weights(i, p):
    lw = {}
    row = lambda a: a.reshape(1, -1)
    for name in ("g_mix_pre", "g_mix_post", "g_ffn_pre", "g_ffn_post", "g_q_lat", "g_kv_lat", "g_mlstm_out", "g_gla_out"):
        lw[name] = row(p[name][i])
    lw["w_ext"] = _gather_cols(p["w_in"][i], _WIN_IDX).astype(BF16)
    lw["wgt"] = p["w_in"][i][:, 416 + 1024:416 + 1040].T.astype(BF16)
    gb = jnp.concatenate([p["b_igate"][i].reshape(-1), p["b_fgate"][i].reshape(-1)])
    lw["gate_bias_row"] = jnp.pad(gb, (0, LANE - 16)).reshape(1, LANE)
    lw["gate_bias_col"] = gb.reshape(16, 1)
    wuq = p["w_uq"][i]
    qi = -np.ones((512,), np.int64)
    qsi = -np.ones((512,), np.int64)
    for h in range(N_HEADS):
        qi[128 * h:128 * h + 96] = 96 * h + np.arange(96)
        qsi[128 * h + 64:128 * h + 96] = 96 * h + 64 + (np.arange(32) ^ 8)
    lw["wq"] = _gather_cols(wuq, qi).astype(BF16)
    lw["wqs"] = _gather_cols(wuq, qsi).astype(BF16)
    ki = -np.ones((512,), np.int64)
    for h in range(N_HEADS):
        ki[128 * h:128 * h + 64] = 128 * h + np.arange(64)
    lw["wk"] = _gather_cols(p["w_ukv"][i], ki).astype(BF16)
    lw["wvt"] = _vt_rows(p["w_ukv"][i].reshape(MLA_KV_LORA, N_HEADS, 128)[:, :, 64:].reshape(MLA_KV_LORA, 256)).astype(BF16)
    lw["wdvt"] = _vt_rows(p["w_in"][i][:, 2768:3024]).astype(BF16)
    wa = p["w_alpha2"][i]
    wal = jnp.zeros((LANE, 256), F32).at[0:16, 0:128].set(wa[0]).at[16:32, 128:256].set(wa[1])
    lw["walpha"] = wal.astype(BF16)
    lw["balpha"] = p["b_alpha"][i].reshape(1, 256)
    lw["w_conv"] = p["w_conv"][i]
    lw["b_conv"] = row(p["b_conv"][i])
    lw["dlam"] = p["diff_lambda"][i]
    lw["g_diff"] = jnp.broadcast_to(p["g_diff_out"][i].reshape(N_HEADS, DIFF_DV, 1), (N_HEADS, DIFF_DV, LANE))
    wb = p["w_branch"][i]
    lw["wbr"] = jnp.concatenate([_pad_heads_rows(wb[0], 64), wb[1], wb[2], _pad_heads_rows(wb[3], 64)], axis=0).astype(BF16)
    lw["w_gate"] = p["w_gate"][i].astype(BF16)
    lw["b_gate"] = row(p["b_gate"][i])
    lw["w_out"] = p["w_out"][i].astype(BF16)
    lw["w_router"] = jnp.pad(p["w_router"][i], ((0, 0), (0, LANE - N_EXPERTS))).astype(BF16)
    lw["w_router_t"] = p["w_router"][i].T.astype(BF16)
    lw["layer"] = i
    lw["w_e_gate"], lw["w_e_up"], lw["w_e_down"] = p["w_e_gate"], p["w_e_up"], p["w_e_down"]
    return lw


def _ffn(xm, h2, aff, aff_t, mod, lw):
    b, t, _ = xm.shape
    nblk = t // TOK_BLK
    cap = EC_CAPACITY * t // N_EXPERTS
    capp = -(-cap // TOK_BLK) * TOK_BLK
    pos4, s04 = _topk(aff_t, cap)
    s0_be = jnp.concatenate([s04[..., 0].transpose(0, 2, 1), jnp.full((b, N_EXPERTS, 1), cap, I32)], axis=-1)
    s0_flat = s0_be.reshape(-1)
    ys = _moe(s0_flat, pos4, h2, lw, capp)
    pos_t = pos4.transpose(0, 1, 3, 2).reshape(b, t, N_EXPERTS)
    first = jnp.minimum(s0_be[..., :-1] // SLOT_BLK, capp // SLOT_BLK - 1)
    fits = jnp.all(s0_be[..., 1:] <= (first + 2) * SLOT_BLK)
    args = (s0_flat, ys, pos_t, aff, xm, mod, lw["g_ffn_post"])
    return lax.cond(fits, functools.partial(_combine, sblk=SLOT_BLK), functools.partial(_combine, sblk=TOK_BLK), *args)


def _hybrid_layer(i, x_c, x_l, c8, need_ctx, p):
    lw = _layer_weights(i, p)
    b, t, _ = x_l.shape
    tc = x_c.shape[1]
    lam_init = 0.8 - 0.6 * math.exp(-0.3 * i)
    mod8 = _ada(c8, p["w_ada"][i], p["b_ada"][i])
    pad = lambda m: jnp.pad(m.reshape(b, 6, D), ((0, 0), (0, 2), (0, 0)))
    mod_l = pad(mod8[:b])
    mod_c = pad(jnp.broadcast_to(mod8[b:b + 1], (b, 6 * D)))

    pc = _proj(x_c, mod_c, _identity_table(tc), lw)
    pt = _proj(x_l, mod_l, _rope_table(t), lw)

    zc = jnp.zeros((b, 2, 256, 512), F32)
    zm = jnp.full((b, 2, 8, 256), NEG, F32)
    zs = jnp.zeros((b, 2, 256, 128), F32)
    hf_c, hb_c, c_fin, m_fin = _mlstm(pc, lw["w_conv"], lw["b_conv"], zc, zm)
    hf_l, hb_l, _, _ = _mlstm(pt, lw["w_conv"], lw["b_conv"], c_fin, m_fin)
    gf_c, gb_c, s_fin = _gla(pc, zs)
    gf_l, gb_l, _ = _gla(pt, s_fin)

    one_g = jnp.ones((N_HEADS, MLA_V, LANE), F32)
    fl = functools.partial(_flash, tk=FLASH_KEYS)
    ya_l = fl(pt["mq"], pc["mk"], pc["mvt"], pt["mk"], pt["mvt"], lw["dlam"], one_g, nmap=1, finish=False, post=1.0,
              tq=min(FLASH_ROWS, t))
    yd_l = fl(pt["dq"], pc["dk"], pc["dvt"], pt["dk"], pt["dvt"], lw["dlam"], lw["g_diff"], nmap=2, finish=True,
              post=1.0 - lam_init, tq=min(FLASH_ROWS // 2, t))
    xm, h2, aff, aff_t = _merge(x_l, mod_l, ya_l, (hf_l, hb_l), pt["lo"], (gf_l, gb_l), pt["gr"], yd_l, lw, tm=256)
    x_l = _ffn(xm, h2, aff, aff_t, mod_l, lw)

    if need_ctx:
        ya_c = fl(pc["mq"], pc["mk"], pc["mvt"], None, None, lw["dlam"], one_g, nmap=1, finish=False, post=1.0, tq=tc)
        yd_c = fl(pc["dq"], pc["dk"], pc["dvt"], None, None, lw["dlam"], lw["g_diff"], nmap=2, finish=True,
                  post=1.0 - lam_init, tq=tc)
        xm, h2, aff, aff_t = _merge(x_c, mod_c, ya_c, (hf_c, hb_c), pc["lo"], (gf_c, gb_c), pc["gr"], yd_c, lw, tm=tc)
        x_c = _ffn(xm, h2, aff, aff_t, mod_c, lw)
    return x_c, x_l


def kernel(x, c, ctx, c_ctx, w_ada, b_ada, g_mix_pre, g_mix_post, g_ffn_pre, g_ffn_post, w_in, g_q_lat, w_uq, g_kv_lat, w_ukv, w_conv, b_conv, b_igate, b_fgate, g_mlstm_out, w_alpha2, b_alpha, g_gla_out, diff_lambda, g_diff_out, w_branch, w_gate, b_gate, w_out, w_router, w_e_gate, w_e_up, w_e_down):
    p = dict(w_ada=w_ada, b_ada=b_ada, g_mix_pre=g_mix_pre, g_mix_post=g_mix_post, g_ffn_pre=g_ffn_pre,
             g_ffn_post=g_ffn_post, w_in=w_in, g_q_lat=g_q_lat, w_uq=w_uq, g_kv_lat=g_kv_lat, w_ukv=w_ukv,
             w_conv=w_conv, b_conv=b_conv, b_igate=b_igate, b_fgate=b_fgate, g_mlstm_out=g_mlstm_out,
             w_alpha2=w_alpha2, b_alpha=b_alpha, g_gla_out=g_gla_out, diff_lambda=diff_lambda, g_diff_out=g_diff_out,
             w_branch=w_branch, w_gate=w_gate, b_gate=b_gate, w_out=w_out, w_router=w_router,
             w_e_gate=w_e_gate.astype(BF16), w_e_up=w_e_up.astype(BF16), w_e_down=w_e_down.astype(BF16))
    b = x.shape[0]
    c8 = jnp.concatenate([c, c_ctx[None], jnp.zeros((8 - b - 1, D), F32)], axis=0)
    x_c, x_l = ctx, x
    for i in range(DEPTH):
        x_c, x_l = _hybrid_layer(i, x_c, x_l, c8, i < DEPTH - 1, p)
    return x_l
```

```python
import functools
import math

import numpy as np
import jax
import jax.numpy as jnp
from jax import lax
from jax.experimental import pallas as pl
from jax.experimental.pallas import tpu as pltpu

F32 = jnp.float32
BF16 = jnp.bfloat16
I32 = jnp.int32

D = 1024
DEPTH = 2
GRID_W = 64
N_HEADS = 4
MLA_NOPE, MLA_ROPE, MLA_V = 64, 32, 64
MLA_Q_LORA, MLA_KV_LORA = 256, 128
ML_DH = 64
GLA_DK, GLA_DV, GLA_RANK, GLA_TAU = 32, 64, 16, 16.0
DIFF_DQK, DIFF_DV = 32, 64
ROPE_DIM, ROPE_BASE = 32, 10000.0
N_EXPERTS, EC_CAPACITY, EXPERT_FF = 16, 2, 1408
NEG = -1e30
EPS = 1e-6
LOG2E = 1.4426950408889634

LANE = 128
HEAD_SLAB = 128
TOK_BLK = 256
ML_CHUNK = 128
ML_BLOCK = 256
GLA_CHUNK = 64
GLA_BLOCK = 256
GATHER_WIN = TOK_BLK + 16
GATHER_WIN_SMALL = 64
MOE_TOK = 4096
SLOT_BLK = 128
VMEM_LIMIT = 56 * 1024 * 1024

ZQ, ZKV, ZKRA, ZKRB, ZMLQK, ZMLV, ZMLO, ZGATE, ZGA = 0, 256, 384, 512, 640, 1152, 1408, 1664, 1792
ZGQ, ZGK, ZGV, ZGR, ZDQ, ZDQS, ZDK, ZDKS, NZ = 1920, 2048, 2176, 2432, 2688, 2944, 3200, 3456, 3712
KV_CHUNK = 256
VT_ROWS = 80
FLASH_ROWS = 512
FLASH_KEYS = 512
FLASH_UNROLL = 16


def _swap32(c):
    return (c // 32) * 32 + ((c % 32) ^ 8)


def _win_index():
    idx = -np.ones((NZ,), np.int64)
    idx[ZQ:ZQ + 256] = np.arange(0, 256)
    idx[ZKV:ZKV + 128] = np.arange(256, 384)
    r = np.arange(32)
    idx[ZKRA + 64:ZKRA + 96] = 384 + r
    idx[ZKRB + 64:ZKRB + 96] = 384 + (r ^ 8)
    ml = 416
    idx[ZMLQK:ZMLQK + 512] = ml + np.arange(512)
    idx[ZMLV:ZMLV + 256] = ml + 512 + np.arange(256)
    idx[ZMLO:ZMLO + 256] = ml + 768 + np.arange(256)
    idx[ZGATE:ZGATE + 16] = ml + 1024 + np.arange(16)
    gl = 1456
    idx[ZGQ:ZGQ + 128] = gl + np.arange(128)
    idx[ZGK:ZGK + 128] = gl + 128 + np.arange(128)
    idx[ZGV:ZGV + 256] = gl + 256 + np.arange(256)
    idx[ZGR:ZGR + 256] = gl + 512 + np.arange(256)
    idx[ZGA:ZGA + 32] = gl + 768 + np.arange(32)
    df = 2256
    c = np.arange(256)
    idx[ZDQ:ZDQ + 256] = df + c
    idx[ZDQS:ZDQS + 256] = df + _swap32(c)
    idx[ZDK:ZDK + 256] = df + 256 + c
    idx[ZDKS:ZDKS + 256] = df + 256 + _swap32(c)
    return idx


def _vt_rows(w_cols):
    n = w_cols.shape[0]
    w4 = w_cols.T.reshape(N_HEADS, 64, n)
    return jnp.pad(w4, ((0, 0), (0, VT_ROWS - 64), (0, 0))).reshape(N_HEADS * VT_ROWS, n)


_WIN_IDX = _win_index()


def _gather_cols(w, idx):
    safe = np.maximum(idx, 0)
    return jnp.where(jnp.asarray(idx >= 0)[None, :], w[:, safe], 0.0)


def _pad_heads_rows(w, width):
    n = w.shape[1]
    w4 = w.reshape(N_HEADS, width, n)
    return jnp.pad(w4, ((0, 0), (0, HEAD_SLAB - width), (0, 0))).reshape(N_HEADS * HEAD_SLAB, n)


def _cparams(sem):
    return pltpu.CompilerParams(dimension_semantics=sem, vmem_limit_bytes=VMEM_LIMIT)


def _rms(x):
    return x * lax.rsqrt(jnp.mean(x * x, axis=-1, keepdims=True) + EPS)


def _sigmoid(x):
    return 0.5 * jnp.tanh(0.5 * x) + 0.5


def _log_sigmoid(x):
    return jnp.minimum(x, 0.0) - jnp.log1p(jnp.exp(-jnp.abs(x)))


def _dot(a, b, precision=None):
    return jnp.dot(a, b, preferred_element_type=F32, precision=precision)


def _dot_nt(a, b, precision=None):
    return lax.dot_general(a, b, (((1,), (1,)), ((), ())), preferred_element_type=F32, precision=precision)


def _dot_tn(a, b, precision=None):
    return lax.dot_general(a, b, (((0,), (0,)), ((), ())), preferred_element_type=F32, precision=precision)


def _split_bf16(x, parts):
    out, r = [], x
    for _ in range(parts):
        t = r.astype(BF16)
        out.append(t)
        r = r - t.astype(F32)
    return out


def _dot_sel(sel, x, parts=3):
    return sum(_dot(sel, t) for t in _split_bf16(x, parts))


def _const_spec(shape):
    nd = len(shape)
    return pl.BlockSpec(shape, lambda *_: (0,) * nd)


def _ada_kernel(c_ref, w_ref, b_ref, o_ref):
    cv = c_ref[...]
    s = (cv * _sigmoid(cv)).astype(BF16)
    o_ref[...] = _dot(s, w_ref[...].astype(BF16)) + b_ref[...]


def _ada(c8, w_ada, b_ada):
    n, tn = 6 * D, 1024
    return pl.pallas_call(
        _ada_kernel, name="ada", grid=(n // tn,),
        in_specs=[pl.BlockSpec((8, D), lambda j: (0, 0)), pl.BlockSpec((D, tn), lambda j: (0, j)),
                  pl.BlockSpec((1, tn), lambda j: (0, j))],
        out_specs=pl.BlockSpec((8, tn), lambda j: (0, j)),
        out_shape=jax.ShapeDtypeStruct((8, n), F32), compiler_params=_cparams(("arbitrary",)),
    )(c8, w_ada, b_ada.reshape(1, n))


_PROJ_OUT = (
    ("mk", 512, BF16),
    ("lqk", 512, F32), ("lv", 256, BF16), ("lo", 256, BF16), ("gc", 128, F32),
    ("gq", 128, BF16), ("gk", 128, BF16), ("gv", 256, BF16), ("gr", 256, BF16), ("glg", 256, F32),
    ("dk", 512, BF16),
)


def _proj_kernel(x_ref, mod_ref, g_ref, w_ref, tab_ref, gq_ref, wq_ref, wqs_ref, gkv_ref, wk_ref, wvt_ref,
                 wgt_ref, gbr_ref, gbc_ref, wal_ref, bal_ref, wdvt_ref,
                 mk_ref, lqk_ref, lv_ref, lo_ref, gc_ref, gq_o, gk_o, gv_o, gr_o, glg_o,
                 dk_ref, grow_ref, mvt_ref, dvt_ref, mqt_ref, dqt_ref):
    x = x_ref[0]
    tm = x.shape[0]
    mod = mod_ref[0]
    h = _rms(x) * g_ref[...] * (1.0 + mod[1:2]) + mod[0:1]
    hb = h.astype(BF16)
    z = _dot(hb, w_ref[...])
    tab = tab_ref[...]
    ct, st, cd, sd = tab[:, 0:128], tab[:, 128:256], tab[:, 256:384], tab[:, 384:512]
    lane = lax.broadcasted_iota(I32, (tm, LANE), 1)

    qn = (_rms(z[:, ZQ:ZQ + 256]) * gq_ref[...]).astype(BF16)
    qa = _dot(qn, wq_ref[...])
    qb = _dot(qn, wqs_ref[...])
    qscale = (MLA_NOPE + MLA_ROPE) ** -0.5 * LOG2E
    for hh in range(N_HEADS):
        sl = slice(HEAD_SLAB * hh, HEAD_SLAB * (hh + 1))
        mqt_ref[0, hh] = ((qa[:, sl] * ct + qb[:, sl] * st) * qscale).T.astype(BF16)
    kvn = (_rms(z[:, ZKV:ZKV + 128]) * gkv_ref[...]).astype(BF16)
    kk = _dot(kvn, wk_ref[...])
    kr = z[:, ZKRA:ZKRA + 128] * ct + z[:, ZKRB:ZKRB + 128] * st
    for hh in range(N_HEADS):
        sl = slice(HEAD_SLAB * hh, HEAD_SLAB * (hh + 1))
        mk_ref[0, :, sl] = (kk[:, sl] + kr).astype(BF16)
    ones_row = lax.broadcasted_iota(I32, (N_HEADS * VT_ROWS, tm), 0) % VT_ROWS == MLA_V
    mvt_ref[0, 0] = jnp.where(ones_row, 1.0, _dot_nt(wvt_ref[...], kvn)).astype(BF16)
    dvt_ref[0, 0] = jnp.where(ones_row, 1.0, _dot_nt(wdvt_ref[...], hb)).astype(BF16)

    lqk_ref[0] = z[:, ZMLQK:ZMLQK + 512]
    lv_ref[0] = z[:, ZMLV:ZMLV + 256].astype(BF16)
    lo_ref[0] = z[:, ZMLO:ZMLO + 256].astype(BF16)
    gcol = z[:, ZGATE:ZGATE + 128] + gbr_ref[...]
    gc_ref[0] = jnp.where(lane < 8, gcol, jnp.where(lane < 16, _log_sigmoid(gcol), 0.0))
    zr = _dot_nt(wgt_ref[...], hb) + gbc_ref[...]
    rowi = lax.broadcasted_iota(I32, zr.shape, 0)
    grow_ref[0] = jnp.where(rowi < 8, zr, _log_sigmoid(zr))

    gq_o[0] = (z[:, ZGQ:ZGQ + 128] * GLA_DK ** -0.5).astype(BF16)
    gk_o[0] = z[:, ZGK:ZGK + 128].astype(BF16)
    gv_o[0] = z[:, ZGV:ZGV + 256].astype(BF16)
    gr_o[0] = z[:, ZGR:ZGR + 256].astype(BF16)
    zg = _dot(z[:, ZGA:ZGA + 128].astype(BF16), wal_ref[...]) + bal_ref[...]
    glg_o[0] = _log_sigmoid(zg) * (1.0 / GLA_TAU)

    dscale = DIFF_DQK ** -0.5 * LOG2E
    for g in range(2):
        gs = slice(128 * g, 128 * (g + 1))
        qg = (z[:, ZDQ:ZDQ + 256][:, gs] * cd + z[:, ZDQS:ZDQS + 256][:, gs] * sd) * dscale
        kg = z[:, ZDK:ZDK + 256][:, gs] * cd + z[:, ZDKS:ZDKS + 256][:, gs] * sd
        for hl in range(2):
            hh = 2 * g + hl
            for m in range(2):
                lo = 64 * hl + 32 * m
                dqt_ref[0, 2 * hh + m] = jnp.where((lane >= lo) & (lane < lo + 32), qg, 0.0).T.astype(BF16)
            dk_ref[0, :, HEAD_SLAB * hh:HEAD_SLAB * (hh + 1)] = jnp.where(
                (lane >= 64 * hl) & (lane < 64 * hl + 64), kg, 0.0).astype(BF16)


def _proj(x, mod, tab, lw):
    b, t, _ = x.shape
    tm = KV_CHUNK
    consts = [lw["g_mix_pre"], lw["w_ext"], None, lw["g_q_lat"], lw["wq"], lw["wqs"], lw["g_kv_lat"], lw["wk"], lw["wvt"],
              lw["wgt"], lw["gate_bias_row"], lw["gate_bias_col"], lw["walpha"], lw["balpha"], lw["wdvt"]]
    in_specs = [pl.BlockSpec((1, tm, D), lambda bi, i: (bi, i, 0)), pl.BlockSpec((1, 8, D), lambda bi, i: (bi, 0, 0))]
    args = [x, mod]
    for cst in consts:
        if cst is None:
            in_specs.append(pl.BlockSpec((tm, 512), lambda bi, i: (i, 0)))
            args.append(tab)
        else:
            in_specs.append(_const_spec(cst.shape))
            args.append(cst)
    out_specs = [pl.BlockSpec((1, tm, w), lambda bi, i: (bi, i, 0)) for _, w, _ in _PROJ_OUT]
    out_shape = [jax.ShapeDtypeStruct((b, t, w), dt) for _, w, dt in _PROJ_OUT]
    out_specs.append(pl.BlockSpec((1, 16, tm), lambda bi, i: (bi, 0, i)))
    out_shape.append(jax.ShapeDtypeStruct((b, 16, t), F32))
    for _ in range(2):
        out_specs.append(pl.BlockSpec((1, 1, N_HEADS * VT_ROWS, tm), lambda bi, i: (bi, i, 0, 0)))
        out_shape.append(jax.ShapeDtypeStruct((b, t // tm, N_HEADS * VT_ROWS, tm), BF16))
    for nslab in (N_HEADS, 2 * N_HEADS):
        out_specs.append(pl.BlockSpec((1, nslab, HEAD_SLAB, tm), lambda bi, i: (bi, 0, 0, i)))
        out_shape.append(jax.ShapeDtypeStruct((b, nslab, HEAD_SLAB, t), BF16))
    outs = pl.pallas_call(
        _proj_kernel, name="proj", grid=(b, t // tm), in_specs=in_specs, out_specs=out_specs, out_shape=out_shape,
        compiler_params=_cparams(("parallel", "arbitrary")),
    )(*args)
    res = {name: o for (name, _, _), o in zip(_PROJ_OUT, outs[:-5])}
    res["grow"], res["mvt"], res["dvt"], res["mqt"], res["dqt"] = outs[-5:]
    return res


def _flash_kernel(*refs, nmap, has_lat, tk, finish, post):
    if has_lat:
        q_ref, kc_ref, vc_ref, kl_ref, vl_ref, dl_ref, g_ref, o_ref, s_ref, acc_ref = refs
    else:
        q_ref, kc_ref, vc_ref, dl_ref, g_ref, o_ref = refs
    tq = q_ref.shape[3]
    qt = q_ref[0, 0] if nmap == 1 else jnp.concatenate([q_ref[0, mm] for mm in range(nmap)], axis=1)
    rows = nmap * tq
    sub = tk // KV_CHUNK

    def softmax(s, smax, m):
        m_new = jnp.maximum(m, smax)
        return m_new, jnp.exp2(m - m_new), jnp.exp2(s - m_new).astype(BF16)

    def pv(p, vts):
        return _dot(vts[0] if len(vts) == 1 else jnp.concatenate(vts, axis=1), p)

    s_ctx = _dot(kc_ref[0], qt)
    if has_lat:
        n = kl_ref.shape[1] // tk
        unroll = min(FLASH_UNROLL, n)

        def scores(j):
            if isinstance(j, int):
                return _dot(kl_ref[0, j * tk:(j + 1) * tk, :], qt)
            off = pl.multiple_of(j * tk, tk)
            return _dot(kl_ref[0, pl.ds(off, tk), :], qt)

        def values(j):
            return [vl_ref[0, j * sub + c] for c in range(sub)]

        def produce(slot, j):
            s = scores(j)
            s_ref[slot] = s
            return jnp.max(s, axis=0, keepdims=True)

        smax0 = produce(0, 0)

    m, _, p = softmax(s_ctx, jnp.max(s_ctx, axis=0, keepdims=True), jnp.full((1, rows), NEG, F32))
    acc = pv(p, [vc_ref[0, 0]])
    if has_lat:
        acc_ref[...] = acc

        def body(jj, carry):
            m, smax = carry
            j = unroll * jj
            for u in range(unroll):
                smax_next = smax
                if not isinstance(j, int):
                    smax_next = produce((u + 1) % 2, jnp.minimum(j + u + 1, n - 1))
                elif j + u + 1 < n:
                    smax_next = produce((u + 1) % 2, j + u + 1)
                m, alpha, p = softmax(s_ref[u % 2], smax, m)
                acc_ref[...] = alpha * acc_ref[...] + pv(p, values(j + u))
                smax = smax_next
            return m, smax

        if unroll == n:
            body(0, (m, smax0))
        else:
            lax.fori_loop(0, n // unroll, body, (m, smax0))
        acc = acc_ref[...]

    o = acc[0:MLA_V, :] / acc[MLA_V:MLA_V + 1, :]
    if nmap == 2:
        lv = dl_ref[...]
        lam = (jnp.exp(jnp.sum(lv[0:1] * lv[1:2], axis=-1, keepdims=True))
               - jnp.exp(jnp.sum(lv[2:3] * lv[3:4], axis=-1, keepdims=True)) + (1.0 - post))
        o = o[:, :tq] - lam * o[:, tq:]
    if finish:
        ms = jnp.mean(o * o, axis=0, keepdims=True)
        o = o * lax.rsqrt(ms + EPS) * jnp.concatenate([g_ref[0]] * (tq // LANE), axis=1) * post
    o_pad = jnp.concatenate([o, jnp.zeros((HEAD_SLAB - MLA_V, tq), F32)], axis=0)
    o_ref[0] = o_pad.T.astype(BF16)


def _flash(q, kc, vct, kl, vlt, dlam, g_out, *, nmap, finish, post, tq, tk):
    b, _, _, t = q.shape
    has_lat = kl is not None
    assert kc.shape[1] == KV_CHUNK and tk % KV_CHUNK == 0
    kspec = lambda n: pl.BlockSpec((1, n, HEAD_SLAB), lambda bi, h, i: (bi, 0, h))
    vspec = lambda n: pl.BlockSpec((1, n // KV_CHUNK, VT_ROWS, KV_CHUNK), lambda bi, h, i: (bi, 0, h, 0))
    in_specs = [pl.BlockSpec((1, nmap, HEAD_SLAB, tq), lambda bi, h, i: (bi, h, 0, i)), kspec(KV_CHUNK), vspec(KV_CHUNK)]
    args = [q, kc, vct]
    scratch = []
    if has_lat:
        tl = kl.shape[1]
        assert (tl // tk) % min(FLASH_UNROLL, tl // tk) == 0
        in_specs += [kspec(tl), vspec(tl)]
        args += [kl, vlt]
        scratch = [pltpu.VMEM((2, tk, nmap * tq), F32), pltpu.VMEM((VT_ROWS, nmap * tq), F32)]
    in_specs += [_const_spec(dlam.shape), pl.BlockSpec((1, MLA_V, LANE), lambda bi, h, i: (h, 0, 0))]
    args += [dlam, g_out]
    return pl.pallas_call(
        functools.partial(_flash_kernel, nmap=nmap, has_lat=has_lat, tk=tk, finish=finish, post=post),
        name="flash_diff" if nmap == 2 else "flash_mla",
        grid=(b, N_HEADS, t // tq), in_specs=in_specs,
        out_specs=pl.BlockSpec((1, tq, HEAD_SLAB), lambda bi, h, i: (bi, i, h)),
        out_shape=jax.ShapeDtypeStruct((b, t, N_HEADS * HEAD_SLAB), BF16),
        scratch_shapes=scratch,
        compiler_params=_cparams(("parallel", "parallel", "arbitrary")),
    )(*args)


def _head_of(shape, axis, width):
    return (lax.broadcasted_iota(I32, shape, axis) % (N_HEADS * width)) // width


def _mlstm_conv(first, last, x, xprev, xnext, wc, bcv):
    n = x.shape[0]
    row = lax.broadcasted_iota(I32, x.shape, 0)
    pr = jnp.where(first, 0.0, xprev[7:8, :])
    nx = jnp.where(last, 0.0, xnext[0:1, :])
    xm = jnp.where(row == 0, pr, pltpu.roll(x, 1, 0))
    xp = jnp.where(row == n - 1, nx, pltpu.roll(x, n - 1, 0))
    y = xm * wc[0:1] + x * wc[1:2] + xp * wc[2:3] + bcv
    qk = y * _sigmoid(y)
    return qk[:, :256], qk[:, 256:] * ML_DH ** -0.5


def _mlstm_dir(d, q, k, v, gcol, grow, cb, m0e):
    L = q.shape[0]
    li = lax.broadcasted_iota(I32, (L, L), 0)
    si = lax.broadcasted_iota(I32, (L, L), 1)
    tin = (si <= li) if d == 0 else (si >= li)
    tinb = tin.astype(BF16)
    bcol = _dot_sel(tinb, gcol)
    brow = sum(_dot_nt(t, tinb) for t in _split_bf16(grow, 3))
    hm256 = _head_of((L, 256), 1, ML_DH)
    hm512 = _head_of((L, 512), 1, ML_DH)
    e_idx = L - 1 if d == 0 else 0

    d_blk, inter_blk = [], []
    for hh in range(N_HEADS):
        c = 4 * d + hh
        bc = bcol[:, 8 + c:9 + c]
        d_blk.append(jnp.where(tin, bc - brow[8 + c:9 + c, :] + grow[c:c + 1, :], NEG))
        inter_blk.append(bc + m0e[0:1, 64 * hh:64 * hh + 1])
    d_st = jnp.concatenate(d_blk, axis=0)
    inter_st = jnp.concatenate(inter_blk, axis=0)
    mt = jnp.maximum(inter_st, jnp.max(d_st, axis=-1, keepdims=True))
    q_st = jnp.concatenate([jnp.where(hm256 == hh, q, 0.0) for hh in range(N_HEADS)], axis=0).astype(BF16)
    s_st = (jnp.exp(d_st - mt) * _dot_nt(q_st, k.astype(BF16))).astype(BF16)
    vext = jnp.concatenate([v, jnp.ones((L, 256), BF16)], axis=1)
    r = _dot(s_st, vext)
    aint = jnp.exp(inter_st - mt)
    p = _dot(q.astype(BF16), cb.astype(BF16))
    tot = jnp.zeros((L, 512), F32)
    mte = jnp.zeros((L, 256), F32)
    for hh in range(N_HEADS):
        rs = slice(hh * L, (hh + 1) * L)
        tot = jnp.where(hm512 == hh, r[rs] + aint[rs] * p, tot)
        mte = jnp.where(hm256 == hh, mt[rs], mte)
    hout = tot[:, :256] / jnp.maximum(jnp.abs(tot[:, 256:]), jnp.exp(-mte))

    wexp = jnp.zeros((L, 256), F32)
    arow = jnp.zeros((1, 512), F32)
    grw = jnp.zeros((1, 512), F32)
    mnew = jnp.zeros((1, 256), F32)
    hr512 = _head_of((1, 512), 1, ML_DH)
    hr256 = _head_of((1, 256), 1, ML_DH)
    for hh in range(N_HEADS):
        c = 4 * d + hh
        bc = bcol[:, 8 + c:9 + c]
        be = bc[e_idx:e_idx + 1, :]
        wl = be - bc + gcol[:, c:c + 1]
        mloc = jnp.max(wl, axis=0, keepdims=True)
        m0h = m0e[0:1, 64 * hh:64 * hh + 1]
        mn = jnp.maximum(be + m0h, mloc)
        wexp = jnp.where(hm256 == hh, jnp.exp(wl - mloc), wexp)
        arow = jnp.where(hr512 == hh, jnp.exp(be + m0h - mn), arow)
        grw = jnp.where(hr512 == hh, jnp.exp(mloc - mn), grw)
        mnew = jnp.where(hr256 == hh, mn, mnew)
    cl = _dot_tn((k * wexp).astype(BF16), vext)
    bd = lax.broadcasted_iota(I32, (256, 512), 0) // ML_DH == _head_of((256, 512), 1, ML_DH)
    return hout, arow * cb + jnp.where(bd, grw * cl, 0.0), jnp.broadcast_to(mnew, (8, 256))


def _mlstm_kernel(xf, xfp, xfn, xb, xbp, xbn, vf, vb, gcf, gcb, grf, grb, wc_ref, bc_ref, c0_ref, m0_ref,
                  hf_ref, hb_ref, c_ref, m_ref):
    i = pl.program_id(1)
    n = pl.num_programs(1)

    @pl.when(i == 0)
    def _():
        c_ref[...] = c0_ref[...]
        m_ref[...] = m0_ref[...]

    wc = wc_ref[...]
    bcv = bc_ref[...]
    L = ML_CHUNK
    nsub = xf.shape[1] // L
    streams = ((0, i == 0, i == n - 1, xf, xfp, xfn, vf, gcf, grf, hf_ref),
               (1, i == n - 1, i == 0, xb, xbp, xbn, vb, gcb, grb, hb_ref))
    for d, first, last, x, xp, xn, v, gc, gr, h_ref in streams:
        q, k = _mlstm_conv(first, last, x[0], xp[0], xn[0], wc, bcv)
        cb, m0e = c_ref[0, d], m_ref[0, d]
        for c in (range(nsub) if d == 0 else reversed(range(nsub))):
            sl = slice(c * L, (c + 1) * L)
            h_ref[0, sl, :], cb, m0e = _mlstm_dir(d, q[sl], k[sl], v[0, sl, :], gc[0, sl, :], gr[0, :, sl], cb, m0e)
        c_ref[0, d], m_ref[0, d] = cb, m0e


def _mlstm(pr, w_conv, b_conv, c0, m0):
    x, v, gc, gr = pr["lqk"], pr["lv"], pr["gc"], pr["grow"]
    b, t, _ = x.shape
    L = min(ML_BLOCK, t)
    n = t // L
    r8 = L // 8
    last8 = t // 8 - 1

    def fw(bi, i):
        return (bi, i, 0)

    def bw(bi, i):
        return (bi, n - 1 - i, 0)

    def halo(ix, shift):
        def f(bi, i):
            blk = ix(bi, i)[1]
            return (bi, jnp.clip(blk * r8 + shift, 0, last8), 0)
        return f

    main = lambda w, ix: pl.BlockSpec((1, L, w), ix)
    in_specs = [main(512, fw), pl.BlockSpec((1, 8, 512), halo(fw, -1)), pl.BlockSpec((1, 8, 512), halo(fw, r8)),
                main(512, bw), pl.BlockSpec((1, 8, 512), halo(bw, -1)), pl.BlockSpec((1, 8, 512), halo(bw, r8)),
                main(256, fw), main(256, bw), main(128, fw), main(128, bw),
                pl.BlockSpec((1, 16, L), lambda bi, i: (bi, 0, i)), pl.BlockSpec((1, 16, L), lambda bi, i: (bi, 0, n - 1 - i)),
                _const_spec(w_conv.shape), _const_spec(b_conv.shape),
                pl.BlockSpec((1, 2, 256, 512), lambda bi, i: (bi, 0, 0, 0)), pl.BlockSpec((1, 2, 8, 256), lambda bi, i: (bi, 0, 0, 0))]
    out_specs = [main(256, fw), main(256, bw),
                 pl.BlockSpec((1, 2, 256, 512), lambda bi, i: (bi, 0, 0, 0)), pl.BlockSpec((1, 2, 8, 256), lambda bi, i: (bi, 0, 0, 0))]
    out_shape = [jax.ShapeDtypeStruct((b, t, 256), F32), jax.ShapeDtypeStruct((b, t, 256), F32),
                 jax.ShapeDtypeStruct(c0.shape, F32), jax.ShapeDtypeStruct(m0.shape, F32)]
    return pl.pallas_call(
        _mlstm_kernel, name="mlstm", grid=(b, n), in_specs=in_specs, out_specs=out_specs, out_shape=out_shape,
        compiler_params=_cparams(("parallel", "arbitrary")),
    )(x, x, x, x, x, x, v, v, gc, gc, gr, gr, w_conv, b_conv, c0, m0)


def _gla_chunk(d, q, k, v, lg, sb):
    L = q.shape[0]
    li = lax.broadcasted_iota(I32, (L, L), 0)
    si = lax.broadcasted_iota(I32, (L, L), 1)
    tin = (si <= li) if d == 0 else (si >= li)
    lgd = lg[:, 128 * d:128 * (d + 1)]
    gcum = _dot_sel(tin.astype(BF16), lgd)
    e_idx = L - 1 if d == 0 else 0
    gend = gcum[e_idx:e_idx + 1, :]
    qf, kf = q.astype(F32), k.astype(F32)
    q_dec = qf * jnp.exp(gcum)
    k_dec = (kf * jnp.exp(-gcum)).astype(BF16)
    k_end = (kf * jnp.exp(gend - gcum)).astype(BF16)
    hm128 = _head_of((L, 128), 1, GLA_DK)
    hm256 = _head_of((L, 256), 1, GLA_DV)
    q_st = jnp.concatenate([jnp.where(hm128 == hh, q_dec, 0.0) for hh in range(N_HEADS)], axis=0).astype(BF16)
    att = _dot_nt(q_st, k_dec)
    tin4 = jnp.concatenate([tin] * N_HEADS, axis=0)
    o_st = _dot(jnp.where(tin4, att, 0.0).astype(BF16), v)
    o = _dot_nt(q_dec.astype(BF16), sb.astype(BF16))
    for hh in range(N_HEADS):
        o = o + jnp.where(hm256 == hh, o_st[hh * L:(hh + 1) * L], 0.0)
    bd = lax.broadcasted_iota(I32, (256, 128), 0) // GLA_DV == _head_of((256, 128), 1, GLA_DK)
    return o, jnp.exp(gend) * sb + jnp.where(bd, _dot_tn(v, k_end), 0.0)


def _gla_kernel(qf, kf, vf, lf, qb, kb, vb, lb, s0_ref, of_ref, ob_ref, s_ref):
    i = pl.program_id(1)

    @pl.when(i == 0)
    def _():
        s_ref[...] = s0_ref[...]

    L = GLA_CHUNK
    nsub = qf.shape[1] // L
    for d, (q, k, v, lg, o_ref) in enumerate(((qf, kf, vf, lf, of_ref), (qb, kb, vb, lb, ob_ref))):
        sb = s_ref[0, d]
        for c in (range(nsub) if d == 0 else reversed(range(nsub))):
            sl = slice(c * L, (c + 1) * L)
            o_ref[0, sl, :], sb = _gla_chunk(d, q[0, sl, :], k[0, sl, :], v[0, sl, :], lg[0, sl, :], sb)
        s_ref[0, d] = sb


def _gla(pr, s0):
    q, k, v, lg = pr["gq"], pr["gk"], pr["gv"], pr["glg"]
    b, t, _ = q.shape
    L = min(GLA_BLOCK, t)
    n = t // L
    fw = lambda bi, i: (bi, i, 0)
    bw = lambda bi, i: (bi, n - 1 - i, 0)
    blk = lambda w, ix: pl.BlockSpec((1, L, w), ix)
    st_spec = pl.BlockSpec((1, 2, 256, 128), lambda bi, i: (bi, 0, 0, 0))
    return pl.pallas_call(
        _gla_kernel, name="gla", grid=(b, n),
        in_specs=[blk(128, fw), blk(128, fw), blk(256, fw), blk(256, fw),
                  blk(128, bw), blk(128, bw), blk(256, bw), blk(256, bw), st_spec],
        out_specs=[blk(256, fw), blk(256, bw), st_spec],
        out_shape=[jax.ShapeDtypeStruct((b, t, 256), F32), jax.ShapeDtypeStruct((b, t, 256), F32),
                   jax.ShapeDtypeStruct(s0.shape, F32)],
        compiler_params=_cparams(("parallel", "arbitrary")),
    )(q, k, v, lg, q, k, v, lg, s0)


def _head_rms_expanded(x, width):
    n = x.shape[1]
    bd = (lax.broadcasted_iota(I32, (n, n), 0) // width == lax.broadcasted_iota(I32, (n, n), 1) // width).astype(BF16)
    return sum(_dot(t, bd) for t in _split_bf16(x * x, 2)) * (1.0 / width)


def _merge_kernel(x_ref, mod_ref, ya_ref, hf_ref, hb_ref, lo_ref, gf_ref, gb_ref, gr_ref, yd_ref,
                  gpre_ref, wg_ref, bg_ref, wbr_ref, wo_ref, gpost_ref, gffn_ref, wr_ref, wrt_ref, gml_ref, ggla_ref,
                  xm_ref, h2_ref, aff_ref, afft_ref):
    x = x_ref[0]
    tm = x.shape[0]
    mod = mod_ref[0]
    hb = (_rms(x) * gpre_ref[...] * (1.0 + mod[1:2]) + mod[0:1]).astype(BF16)

    hs = hf_ref[0] + hb_ref[0]
    y_ml = _sigmoid(lo_ref[0].astype(F32)) * (hs * lax.rsqrt(_head_rms_expanded(hs, ML_DH) + EPS) * gml_ref[...])
    gs = gf_ref[0] + gb_ref[0]
    rr = gr_ref[0].astype(F32)
    y_gla = rr * _sigmoid(rr) * (gs * lax.rsqrt(_head_rms_expanded(gs, GLA_DV) + EPS) * ggla_ref[...])

    branches = ((ya_ref[0], 0, 512), (y_ml.astype(BF16), 512, 256), (y_gla.astype(BF16), 768, 256), (yd_ref[0], 1024, 512))
    mix = jnp.zeros((tm, D), F32)
    for nb, (yb, r0, rw) in enumerate(branches):
        gate = _sigmoid(_dot(hb, wg_ref[:, nb * D:(nb + 1) * D]) + bg_ref[:, nb * D:(nb + 1) * D])
        mix = mix + gate * _dot(yb, wbr_ref[r0:r0 + rw, :])
    y = _dot(mix.astype(BF16), wo_ref[...])
    xm = x + mod[2:3] * (_rms(y) * gpost_ref[...])
    xm_ref[0] = xm

    h2 = (_rms(xm) * gffn_ref[...] * (1.0 + mod[4:5]) + mod[3:4]).astype(BF16)
    h2_ref[0] = h2
    lane = lax.broadcasted_iota(I32, (tm, LANE), 1)
    lg = jnp.where(lane < N_EXPERTS, _dot(h2, wr_ref[...]), NEG)
    e = jnp.exp(lg - jnp.max(lg, axis=-1, keepdims=True))
    aff_ref[0] = (e / jnp.sum(e, axis=-1, keepdims=True))[:, :N_EXPERTS]
    lt = _dot_nt(wrt_ref[...], h2)
    et = jnp.exp(lt - jnp.max(lt, axis=0, keepdims=True))
    afft_ref[0] = et / jnp.sum(et, axis=0, keepdims=True)


def _merge(x, mod, ya, ml, lo, gl, gr, yd, lw, tm):
    b, t, _ = x.shape
    tok = lambda w: pl.BlockSpec((1, tm, w), lambda bi, i: (bi, i, 0))
    consts = [lw["g_mix_pre"], lw["w_gate"], lw["b_gate"], lw["wbr"], lw["w_out"], lw["g_mix_post"], lw["g_ffn_pre"],
              lw["w_router"], lw["w_router_t"], lw["g_mlstm_out"], lw["g_gla_out"]]
    in_specs = [tok(D), pl.BlockSpec((1, 8, D), lambda bi, i: (bi, 0, 0)), tok(512), tok(256), tok(256), tok(256),
                tok(256), tok(256), tok(256), tok(512)] + [_const_spec(c.shape) for c in consts]
    out_specs = [tok(D), tok(D), tok(N_EXPERTS), pl.BlockSpec((1, N_EXPERTS, tm), lambda bi, i: (bi, 0, i))]
    out_shape = [jax.ShapeDtypeStruct((b, t, D), F32), jax.ShapeDtypeStruct((b, t, D), BF16),
                 jax.ShapeDtypeStruct((b, t, N_EXPERTS), F32), jax.ShapeDtypeStruct((b, N_EXPERTS, t), F32)]
    return pl.pallas_call(
        _merge_kernel, name="merge", grid=(b, t // tm), in_specs=in_specs, out_specs=out_specs, out_shape=out_shape,
        compiler_params=_cparams(("parallel", "arbitrary")),
    )(x, mod, ya, ml[0], ml[1], lo, gl[0], gl[1], gr, yd, *consts)


def _topk_kernel(a_ref, pos_ref, s0_ref, *, cap):
    nblk = a_ref.shape[1]
    bits = pltpu.bitcast(a_ref[0], I32)

    def bisect(i, thr):
        cand = thr | (1 << (30 - i))
        cnt = jnp.sum((bits >= cand).astype(I32), axis=(0, 2), keepdims=True)
        return jnp.where(cnt >= cap, cand, thr)

    thr3 = lax.fori_loop(0, 31, bisect, jnp.zeros((1, N_EXPERTS, 1), I32))
    need3 = cap - jnp.sum((bits > thr3).astype(I32), axis=(0, 2), keepdims=True)
    thr, need = thr3[0], need3[0].astype(F32)
    upper = (lax.broadcasted_iota(I32, (TOK_BLK, TOK_BLK), 0) <= lax.broadcasted_iota(I32, (TOK_BLK, TOK_BLK), 1)).astype(BF16)

    def blk(j, carry):
        c_eq, c_sel = carry
        bj = pltpu.bitcast(a_ref[0, j], I32)
        gt, eq = bj > thr, bj == thr
        cum_eq = _dot(eq.astype(BF16), upper) + c_eq
        sel = gt | (eq & (cum_eq <= need))
        cum_sel = _dot(sel.astype(BF16), upper) + c_sel
        pos_ref[0, j] = jnp.where(sel, cum_sel - 1.0, -1.0).astype(I32)
        s0_ref[0, j] = jnp.broadcast_to(c_sel, (N_EXPERTS, LANE)).astype(I32)
        return cum_eq[:, TOK_BLK - 1:TOK_BLK], cum_sel[:, TOK_BLK - 1:TOK_BLK]

    zero = jnp.zeros((N_EXPERTS, 1), F32)
    lax.fori_loop(0, nblk, blk, (zero, zero))


def _topk(aff_t, cap):
    b, _, t = aff_t.shape
    nblk = t // TOK_BLK
    a4 = aff_t.reshape(b, N_EXPERTS, nblk, TOK_BLK).transpose(0, 2, 1, 3)
    spec = lambda w: pl.BlockSpec((1, nblk, N_EXPERTS, w), lambda bi: (bi, 0, 0, 0))
    return pl.pallas_call(
        functools.partial(_topk_kernel, cap=cap), name="topk", grid=(b,),
        in_specs=[spec(TOK_BLK)], out_specs=[spec(TOK_BLK), spec(LANE)],
        out_shape=[jax.ShapeDtypeStruct((b, nblk, N_EXPERTS, TOK_BLK), I32),
                   jax.ShapeDtypeStruct((b, nblk, N_EXPERTS, LANE), I32)],
        compiler_params=_cparams(("parallel",)),
    )(a4)


def _moe_kernel(s0_ref, pos_ref, h_ref, wg_ref, wu_ref, wd_ref, ys_ref, xs_ref, *, nblk, nsub, capp):
    e, bi, tb = pl.program_id(0), pl.program_id(1), pl.program_id(2)

    @pl.when(tb == 0)
    def _():
        xs_ref[...] = jnp.zeros(xs_ref.shape, BF16)

    base = (bi * N_EXPERTS + e) * (nblk + 1) + tb * nsub
    s0s = [s0_ref[base + sb] for sb in range(nsub + 1)]
    a0s = [pl.multiple_of((s0 // 16) * 16, 16) for s0 in s0s[:-1]]
    spans = [s0s[sb + 1] - a0s[sb] for sb in range(nsub)]

    def gather(win, sb):
        a0 = a0s[sb]
        prow = pos_ref[0, sb, pl.ds(e, 1), :]
        slot = lax.broadcasted_iota(I32, (win, TOK_BLK), 0) + a0
        rows = _dot((slot == prow).astype(BF16), h_ref[0, sb * TOK_BLK:(sb + 1) * TOK_BLK, :])
        xs_ref[pl.ds(a0, win), :] = xs_ref[pl.ds(a0, win), :] + rows.astype(BF16)

    all_small = functools.reduce(jnp.logical_and, [sp <= GATHER_WIN_SMALL for sp in spans])

    @pl.when(all_small)
    def _():
        for sb in range(nsub):
            gather(GATHER_WIN_SMALL, sb)

    @pl.when(jnp.logical_not(all_small))
    def _():
        for sb in range(nsub):
            nonempty = s0s[sb + 1] > s0s[sb]
            pl.when(nonempty & (spans[sb] <= GATHER_WIN_SMALL))(functools.partial(gather, GATHER_WIN_SMALL, sb))
            pl.when(spans[sb] > GATHER_WIN_SMALL)(functools.partial(gather, GATHER_WIN, sb))

    @pl.when(tb == pl.num_programs(2) - 1)
    def _():
        def chunk(c, carry):
            off = pl.multiple_of(c * TOK_BLK, TOK_BLK)
            xc = xs_ref[pl.ds(off, TOK_BLK), :]
            hg = _dot(xc, wg_ref[0, 0])
            hid = (hg * _sigmoid(hg) * _dot(xc, wu_ref[0, 0])).astype(BF16)
            ys_ref[0, 0, pl.ds(off, TOK_BLK), :] = _dot(hid, wd_ref[0, 0]).astype(BF16)
            return carry
        lax.fori_loop(0, capp // TOK_BLK, chunk, 0)


def _moe(s0_flat, pos4, h2, lw, capp):
    b, t, _ = h2.shape
    layer = lw["layer"]
    nblk = t // TOK_BLK
    tok = min(MOE_TOK, t)
    nsub = tok // TOK_BLK
    grid_spec = pltpu.PrefetchScalarGridSpec(
        num_scalar_prefetch=1, grid=(N_EXPERTS, b, t // tok),
        in_specs=[pl.BlockSpec((1, nsub, N_EXPERTS, TOK_BLK), lambda e, bi, tb, s: (bi, tb, 0, 0)),
                  pl.BlockSpec((1, tok, D), lambda e, bi, tb, s: (bi, tb, 0)),
                  pl.BlockSpec((1, 1, D, EXPERT_FF), lambda e, bi, tb, s: (layer, e, 0, 0)),
                  pl.BlockSpec((1, 1, D, EXPERT_FF), lambda e, bi, tb, s: (layer, e, 0, 0)),
                  pl.BlockSpec((1, 1, EXPERT_FF, D), lambda e, bi, tb, s: (layer, e, 0, 0))],
        out_specs=pl.BlockSpec((1, 1, capp, D), lambda e, bi, tb, s: (bi, e, 0, 0)),
        scratch_shapes=[pltpu.VMEM((capp + GATHER_WIN, D), BF16)])
    return pl.pallas_call(
        functools.partial(_moe_kernel, nblk=nblk, nsub=nsub, capp=capp), name="moe", grid_spec=grid_spec,
        out_shape=jax.ShapeDtypeStruct((b, N_EXPERTS, capp, D), BF16),
        compiler_params=_cparams(("arbitrary", "arbitrary", "arbitrary")),
    )(s0_flat, pos4, h2, lw["w_e_gate"], lw["w_e_up"], lw["w_e_down"])


def _combine_kernel(s0_ref, *refs, nblk, nb, sblk):
    ys_refs = refs[:2 * N_EXPERTS]
    pos_ref, aff_ref, xm_ref, mod_ref, g_ref, o_ref = refs[2 * N_EXPERTS:]
    bi, tb = pl.program_id(0), pl.program_id(1)
    pos = pos_ref[0]
    aff = aff_ref[0]
    lane = lax.broadcasted_iota(I32, (TOK_BLK, 2 * sblk), 1)
    acc = jnp.zeros((TOK_BLK, D), F32)
    for e in range(N_EXPERTS):
        s0 = s0_ref[(bi * N_EXPERTS + e) * (nblk + 1) + tb]
        blk0 = jnp.minimum(s0 // sblk, nb - 1)
        rel = pos[:, e:e + 1] - blk0 * sblk
        ysw = jnp.concatenate([ys_refs[2 * e][0, 0], ys_refs[2 * e + 1][0, 0]], axis=0)
        acc = acc + aff[:, e:e + 1] * _dot((lane == rel).astype(BF16), ysw)
    mod = mod_ref[0]
    o_ref[0] = xm_ref[0] + mod[5:6] * (_rms(acc) * g_ref[...])


def _combine(s0_flat, ys, pos_t, aff, xm, mod, g_post, sblk):
    b, t, _ = xm.shape
    nblk = t // TOK_BLK
    nb = ys.shape[2] // sblk

    def ys_spec(e, k):
        def ix(bi, tb, s):
            blk0 = jnp.minimum(s[(bi * N_EXPERTS + e) * (nblk + 1) + tb] // sblk, nb - 1)
            return (bi, e, jnp.minimum(blk0 + k, nb - 1), 0)
        return pl.BlockSpec((1, 1, sblk, D), ix)

    tok = lambda w: pl.BlockSpec((1, TOK_BLK, w), lambda bi, tb, s: (bi, tb, 0))
    in_specs = [ys_spec(e, k) for e in range(N_EXPERTS) for k in range(2)]
    in_specs += [tok(N_EXPERTS), tok(N_EXPERTS), tok(D), pl.BlockSpec((1, 8, D), lambda bi, tb, s: (bi, 0, 0)),
                 pl.BlockSpec((1, D), lambda bi, tb, s: (0, 0))]
    grid_spec = pltpu.PrefetchScalarGridSpec(num_scalar_prefetch=1, grid=(b, nblk), in_specs=in_specs, out_specs=tok(D))
    return pl.pallas_call(
        functools.partial(_combine_kernel, nblk=nblk, nb=nb, sblk=sblk), name="combine", grid_spec=grid_spec,
        out_shape=jax.ShapeDtypeStruct((b, t, D), F32),
        compiler_params=_cparams(("arbitrary", "arbitrary")),
    )(s0_flat, *([ys] * (2 * N_EXPERTS)), pos_t, aff, xm, mod, g_post)


def _rope_table(t):
    nf = ROPE_DIM // 4
    pos = jnp.arange(t)
    inv = ROPE_BASE ** (-jnp.arange(nf, dtype=F32) / nf)
    ang = jnp.stack([pos // GRID_W, pos % GRID_W], axis=-1).astype(F32)[..., None] * inv
    cos, sin = jnp.cos(ang), jnp.sin(ang)
    c32 = jnp.stack([cos, cos], axis=2).reshape(t, ROPE_DIM)
    s32 = jnp.stack([-sin, sin], axis=2).reshape(t, ROPE_DIM)
    one, zero = jnp.ones((t, 64), F32), jnp.zeros((t, 32), F32)
    ct = jnp.concatenate([one, c32, zero], axis=1)
    st = jnp.concatenate([0.0 * one, s32, zero], axis=1)
    return jnp.concatenate([ct, st, jnp.tile(c32, (1, 4)), jnp.tile(s32, (1, 4))], axis=1)


def _identity_table(t):
    one, zero = jnp.ones((t, 128), F32), jnp.zeros((t, 128), F32)
    ct = jnp.concatenate([jnp.ones((t, 96), F32), jnp.zeros((t, 32), F32)], axis=1)
    return jnp.concatenate([ct, zero, one, zero], axis=1)


def _layer_weights(i, p):
    lw = {}
    row = lambda a: a.reshape(1, -1)
    for name in ("g_mix_pre", "g_mix_post", "g_ffn_pre", "g_ffn_post", "g_q_lat", "g_kv_lat", "g_mlstm_out", "g_gla_out"):
        lw[name] = row(p[name][i])
    lw["w_ext"] = _gather_cols(p["w_in"][i], _WIN_IDX).astype(BF16)
    lw["wgt"] = p["w_in"][i][:, 416 + 1024:416 + 1040].T.astype(BF16)
    gb = jnp.concatenate([p["b_igate"][i].reshape(-1), p["b_fgate"][i].reshape(-1)])
    lw["gate_bias_row"] = jnp.pad(gb, (0, LANE - 16)).reshape(1, LANE)
    lw["gate_bias_col"] = gb.reshape(16, 1)
    wuq = p["w_uq"][i]
    qi = -np.ones((512,), np.int64)
    qsi = -np.ones((512,), np.int64)
    for h in range(N_HEADS):
        qi[128 * h:128 * h + 96] = 96 * h + np.arange(96)
        qsi[128 * h + 64:128 * h + 96] = 96 * h + 64 + (np.arange(32) ^ 8)
    lw["wq"] = _gather_cols(wuq, qi).astype(BF16)
    lw["wqs"] = _gather_cols(wuq, qsi).astype(BF16)
    ki = -np.ones((512,), np.int64)
    for h in range(N_HEADS):
        ki[128 * h:128 * h + 64] = 128 * h + np.arange(64)
    lw["wk"] = _gather_cols(p["w_ukv"][i], ki).astype(BF16)
    lw["wvt"] = _vt_rows(p["w_ukv"][i].reshape(MLA_KV_LORA, N_HEADS, 128)[:, :, 64:].reshape(MLA_KV_LORA, 256)).astype(BF16)
    lw["wdvt"] = _vt_rows(p["w_in"][i][:, 2768:3024]).astype(BF16)
    wa = p["w_alpha2"][i]
    wal = jnp.zeros((LANE, 256), F32).at[0:16, 0:128].set(wa[0]).at[16:32, 128:256].set(wa[1])
    lw["walpha"] = wal.astype(BF16)
    lw["balpha"] = p["b_alpha"][i].reshape(1, 256)
    lw["w_conv"] = p["w_conv"][i]
    lw["b_conv"] = row(p["b_conv"][i])
    lw["dlam"] = p["diff_lambda"][i]
    lw["g_diff"] = jnp.broadcast_to(p["g_diff_out"][i].reshape(N_HEADS, DIFF_DV, 1), (N_HEADS, DIFF_DV, LANE))
    wb = p["w_branch"][i]
    lw["wbr"] = jnp.concatenate([_pad_heads_rows(wb[0], 64), wb[1], wb[2], _pad_heads_rows(wb[3], 64)], axis=0).astype(BF16)
    lw["w_gate"] = p["w_gate"][i].astype(BF16)
    lw["b_gate"] = row(p["b_gate"][i])
    lw["w_out"] = p["w_out"][i].astype(BF16)
    lw["w_router"] = jnp.pad(p["w_router"][i], ((0, 0), (0, LANE - N_EXPERTS))).astype(BF16)
    lw["w_router_t"] = p["w_router"][i].T.astype(BF16)
    lw["layer"] = i
    lw["w_e_gate"], lw["w_e_up"], lw["w_e_down"] = p["w_e_gate"], p["w_e_up"], p["w_e_down"]
    return lw


def _ffn(xm, h2, aff, aff_t, mod, lw):
    b, t, _ = xm.shape
    nblk = t // TOK_BLK
    cap = EC_CAPACITY * t // N_EXPERTS
    capp = -(-cap // TOK_BLK) * TOK_BLK
    pos4, s04 = _topk(aff_t, cap)
    s0_be = jnp.concatenate([s04[..., 0].transpose(0, 2, 1), jnp.full((b, N_EXPERTS, 1), cap, I32)], axis=-1)
    s0_flat = s0_be.reshape(-1)
    ys = _moe(s0_flat, pos4, h2, lw, capp)
    pos_t = pos4.transpose(0, 1, 3, 2).reshape(b, t, N_EXPERTS)
    first = jnp.minimum(s0_be[..., :-1] // SLOT_BLK, capp // SLOT_BLK - 1)
    fits = jnp.all(s0_be[..., 1:] <= (first + 2) * SLOT_BLK)
    args = (s0_flat, ys, pos_t, aff, xm, mod, lw["g_ffn_post"])
    return lax.cond(fits, functools.partial(_combine, sblk=SLOT_BLK), functools.partial(_combine, sblk=TOK_BLK), *args)


def _hybrid_layer(i, x_c, x_l, c8, need_ctx, p):
    lw = _layer_weights(i, p)
    b, t, _ = x_l.shape
    tc = x_c.shape[1]
    lam_init = 0.8 - 0.6 * math.exp(-0.3 * i)
    mod8 = _ada(c8, p["w_ada"][i], p["b_ada"][i])
    pad = lambda m: jnp.pad(m.reshape(b, 6, D), ((0, 0), (0, 2), (0, 0)))
    mod_l = pad(mod8[:b])
    mod_c = pad(jnp.broadcast_to(mod8[b:b + 1], (b, 6 * D)))

    pc = _proj(x_c, mod_c, _identity_table(tc), lw)
    pt = _proj(x_l, mod_l, _rope_table(t), lw)

    zc = jnp.zeros((b, 2, 256, 512), F32)
    zm = jnp.full((b, 2, 8, 256), NEG, F32)
    zs = jnp.zeros((b, 2, 256, 128), F32)
    hf_c, hb_c, c_fin, m_fin = _mlstm(pc, lw["w_conv"], lw["b_conv"], zc, zm)
    hf_l, hb_l, _, _ = _mlstm(pt, lw["w_conv"], lw["b_conv"], c_fin, m_fin)
    gf_c, gb_c, s_fin = _gla(pc, zs)
    gf_l, gb_l, _ = _gla(pt, s_fin)

    one_g = jnp.ones((N_HEADS, MLA_V, LANE), F32)
    fl = functools.partial(_flash, tk=FLASH_KEYS)
    ya_l = fl(pt["mqt"], pc["mk"], pc["mvt"], pt["mk"], pt["mvt"], lw["dlam"], one_g, nmap=1, finish=False, post=1.0,
              tq=min(FLASH_ROWS, t))
    yd_l = fl(pt["dqt"], pc["dk"], pc["dvt"], pt["dk"], pt["dvt"], lw["dlam"], lw["g_diff"], nmap=2, finish=True,
              post=1.0 - lam_init, tq=min(FLASH_ROWS // 2, t))
    xm, h2, aff, aff_t = _merge(x_l, mod_l, ya_l, (hf_l, hb_l), pt["lo"], (gf_l, gb_l), pt["gr"], yd_l, lw, tm=256)
    x_l = _ffn(xm, h2, aff, aff_t, mod_l, lw)

    if need_ctx:
        ya_c = fl(pc["mqt"], pc["mk"], pc["mvt"], None, None, lw["dlam"], one_g, nmap=1, finish=False, post=1.0, tq=tc)
        yd_c = fl(pc["dqt"], pc["dk"], pc["dvt"], None, None, lw["dlam"], lw["g_diff"], nmap=2, finish=True,
                  post=1.0 - lam_init, tq=tc)
        xm, h2, aff, aff_t = _merge(x_c, mod_c, ya_c, (hf_c, hb_c), pc["lo"], (gf_c, gb_c), pc["gr"], yd_c, lw, tm=tc)
        x_c = _ffn(xm, h2, aff, aff_t, mod_c, lw)
    return x_c, x_l


def kernel(x, c, ctx, c_ctx, w_ada, b_ada, g_mix_pre, g_mix_post, g_ffn_pre, g_ffn_post, w_in, g_q_lat, w_uq, g_kv_lat, w_ukv, w_conv, b_conv, b_igate, b_fgate, g_mlstm_out, w_alpha2, b_alpha, g_gla_out, diff_lambda, g_diff_out, w_branch, w_gate, b_gate, w_out, w_router, w_e_gate, w_e_up, w_e_down):
    p = dict(w_ada=w_ada, b_ada=b_ada, g_mix_pre=g_mix_pre, g_mix_post=g_mix_post, g_ffn_pre=g_ffn_pre,
             g_ffn_post=g_ffn_post, w_in=w_in, g_q_lat=g_q_lat, w_uq=w_uq, g_kv_lat=g_kv_lat, w_ukv=w_ukv,
             w_conv=w_conv, b_conv=b_conv, b_igate=b_igate, b_fgate=b_fgate, g_mlstm_out=g_mlstm_out,
             w_alpha2=w_alpha2, b_alpha=b_alpha, g_gla_out=g_gla_out, diff_lambda=diff_lambda, g_diff_out=g_diff_out,
             w_branch=w_branch, w_gate=w_gate, b_gate=b_gate, w_out=w_out, w_router=w_router,
             w_e_gate=w_e_gate.astype(BF16), w_e_up=w_e_up.astype(BF16), w_e_down=w_e_down.astype(BF16))
    b = x.shape[0]
    c8 = jnp.concatenate([c, c_ctx[None], jnp.zeros((8 - b - 1, D), F32)], axis=0)
    x_c, x_l = ctx, x
    for i in range(DEPTH):
        x_c, x_l = _hybrid_layer(i, x_c, x_l, c8, i < DEPTH - 1, p)
    return x_l
```

```python
import functools
import math

import numpy as np
import jax
import jax.numpy as jnp
from jax import lax
from jax.experimental import pallas as pl
from jax.experimental.pallas import tpu as pltpu

F32 = jnp.float32
BF16 = jnp.bfloat16
I32 = jnp.int32

D = 1024
DEPTH = 2
GRID_W = 64
N_HEADS = 4
MLA_NOPE, MLA_ROPE, MLA_V = 64, 32, 64
MLA_Q_LORA, MLA_KV_LORA = 256, 128
ML_DH = 64
GLA_DK, GLA_DV, GLA_RANK, GLA_TAU = 32, 64, 16, 16.0
DIFF_DQK, DIFF_DV = 32, 64
ROPE_DIM, ROPE_BASE = 32, 10000.0
N_EXPERTS, EC_CAPACITY, EXPERT_FF = 16, 2, 1408
NEG = -1e30
EPS = 1e-6
LOG2E = 1.4426950408889634

LANE = 128
HEAD_SLAB = 128
TOK_BLK = 256
ML_CHUNK = 128
ML_BLOCK = 256
GLA_CHUNK = 64
GLA_BLOCK = 256
GATHER_WIN = TOK_BLK + 16
GATHER_WIN_SMALL = 64
MOE_TOK = 4096
SLOT_BLK = 128
VMEM_LIMIT = 56 * 1024 * 1024

ZQ, ZKV, ZKRA, ZKRB, ZMLQK, ZMLV, ZMLO, ZGATE, ZGA = 0, 256, 384, 512, 640, 1152, 1408, 1664, 1792
ZGQ, ZGK, ZGV, ZGR, ZDQ, ZDQS, ZDK, ZDKS, NZ = 1920, 2048, 2176, 2432, 2688, 2944, 3200, 3456, 3712
KV_CHUNK = 256
VT_ROWS = 80
FLASH_ROWS = 512
FLASH_KEYS = 512
FLASH_UNROLL = 16


def _swap32(c):
    return (c // 32) * 32 + ((c % 32) ^ 8)


def _win_index():
    idx = -np.ones((NZ,), np.int64)
    idx[ZQ:ZQ + 256] = np.arange(0, 256)
    idx[ZKV:ZKV + 128] = np.arange(256, 384)
    r = np.arange(32)
    idx[ZKRA + 64:ZKRA + 96] = 384 + r
    idx[ZKRB + 64:ZKRB + 96] = 384 + (r ^ 8)
    ml = 416
    idx[ZMLQK:ZMLQK + 512] = ml + np.arange(512)
    idx[ZMLV:ZMLV + 256] = ml + 512 + np.arange(256)
    idx[ZMLO:ZMLO + 256] = ml + 768 + np.arange(256)
    idx[ZGATE:ZGATE + 16] = ml + 1024 + np.arange(16)
    gl = 1456
    idx[ZGQ:ZGQ + 128] = gl + np.arange(128)
    idx[ZGK:ZGK + 128] = gl + 128 + np.arange(128)
    idx[ZGV:ZGV + 256] = gl + 256 + np.arange(256)
    idx[ZGR:ZGR + 256] = gl + 512 + np.arange(256)
    idx[ZGA:ZGA + 32] = gl + 768 + np.arange(32)
    df = 2256
    c = np.arange(256)
    idx[ZDQ:ZDQ + 256] = df + c
    idx[ZDQS:ZDQS + 256] = df + _swap32(c)
    idx[ZDK:ZDK + 256] = df + 256 + c
    idx[ZDKS:ZDKS + 256] = df + 256 + _swap32(c)
    return idx


def _vt_rows(w_cols):
    n = w_cols.shape[0]
    w4 = w_cols.T.reshape(N_HEADS, 64, n)
    return jnp.pad(w4, ((0, 0), (0, VT_ROWS - 64), (0, 0))).reshape(N_HEADS * VT_ROWS, n)


_WIN_IDX = _win_index()


def _gather_cols(w, idx):
    safe = np.maximum(idx, 0)
    return jnp.where(jnp.asarray(idx >= 0)[None, :], w[:, safe], 0.0)


def _pad_heads_rows(w, width):
    n = w.shape[1]
    w4 = w.reshape(N_HEADS, width, n)
    return jnp.pad(w4, ((0, 0), (0, HEAD_SLAB - width), (0, 0))).reshape(N_HEADS * HEAD_SLAB, n)


def _cparams(sem):
    return pltpu.CompilerParams(dimension_semantics=sem, vmem_limit_bytes=VMEM_LIMIT)


def _rms(x):
    return x * lax.rsqrt(jnp.mean(x * x, axis=-1, keepdims=True) + EPS)


def _sigmoid(x):
    return 0.5 * jnp.tanh(0.5 * x) + 0.5


def _log_sigmoid(x):
    return jnp.minimum(x, 0.0) - jnp.log1p(jnp.exp(-jnp.abs(x)))


def _dot(a, b, precision=None):
    return jnp.dot(a, b, preferred_element_type=F32, precision=precision)


def _dot_nt(a, b, precision=None):
    return lax.dot_general(a, b, (((1,), (1,)), ((), ())), preferred_element_type=F32, precision=precision)


def _dot_tn(a, b, precision=None):
    return lax.dot_general(a, b, (((0,), (0,)), ((), ())), preferred_element_type=F32, precision=precision)


def _split_bf16(x, parts):
    out, r = [], x
    for _ in range(parts):
        t = r.astype(BF16)
        out.append(t)
        r = r - t.astype(F32)
    return out


def _dot_sel(sel, x, parts=3):
    return sum(_dot(sel, t) for t in _split_bf16(x, parts))


def _const_spec(shape):
    nd = len(shape)
    return pl.BlockSpec(shape, lambda *_: (0,) * nd)


def _ada_kernel(c_ref, w_ref, b_ref, o_ref):
    cv = c_ref[...]
    s = (cv * _sigmoid(cv)).astype(BF16)
    o_ref[...] = _dot(s, w_ref[...].astype(BF16)) + b_ref[...]


def _ada(c8, w_ada, b_ada):
    n, tn = 6 * D, 1024
    return pl.pallas_call(
        _ada_kernel, name="ada", grid=(n // tn,),
        in_specs=[pl.BlockSpec((8, D), lambda j: (0, 0)), pl.BlockSpec((D, tn), lambda j: (0, j)),
                  pl.BlockSpec((1, tn), lambda j: (0, j))],
        out_specs=pl.BlockSpec((8, tn), lambda j: (0, j)),
        out_shape=jax.ShapeDtypeStruct((8, n), F32), compiler_params=_cparams(("arbitrary",)),
    )(c8, w_ada, b_ada.reshape(1, n))


_PROJ_OUT = (
    ("mk", 512, BF16),
    ("lqk", 512, F32), ("lv", 256, BF16), ("lo", 256, BF16), ("gc", 128, F32),
    ("gq", 128, BF16), ("gk", 128, BF16), ("gv", 256, BF16), ("gr", 256, BF16), ("glg", 256, F32),
    ("dk", 512, BF16),
)


def _proj_kernel(x_ref, mod_ref, g_ref, w_ref, tab_ref, gq_ref, wq_ref, wqs_ref, gkv_ref, wk_ref, wvt_ref,
                 wgt_ref, gbr_ref, gbc_ref, wal_ref, bal_ref, wdvt_ref,
                 mk_ref, lqk_ref, lv_ref, lo_ref, gc_ref, gq_o, gk_o, gv_o, gr_o, glg_o,
                 dk_ref, grow_ref, mvt_ref, dvt_ref, mqt_ref, dqt_ref):
    x = x_ref[0]
    tm = x.shape[0]
    mod = mod_ref[0]
    h = _rms(x) * g_ref[...] * (1.0 + mod[1:2]) + mod[0:1]
    hb = h.astype(BF16)
    z = _dot(hb, w_ref[...])
    tab = tab_ref[...]
    ct, st, cd, sd = tab[:, 0:128], tab[:, 128:256], tab[:, 256:384], tab[:, 384:512]
    lane = lax.broadcasted_iota(I32, (tm, LANE), 1)

    qn = (_rms(z[:, ZQ:ZQ + 256]) * gq_ref[...]).astype(BF16)
    qa = _dot(qn, wq_ref[...])
    qb = _dot(qn, wqs_ref[...])
    qscale = (MLA_NOPE + MLA_ROPE) ** -0.5 * LOG2E
    for hh in range(N_HEADS):
        sl = slice(HEAD_SLAB * hh, HEAD_SLAB * (hh + 1))
        mqt_ref[0, hh] = ((qa[:, sl] * ct + qb[:, sl] * st) * qscale).T.astype(BF16)
    kvn = (_rms(z[:, ZKV:ZKV + 128]) * gkv_ref[...]).astype(BF16)
    kk = _dot(kvn, wk_ref[...])
    kr = z[:, ZKRA:ZKRA + 128] * ct + z[:, ZKRB:ZKRB + 128] * st
    for hh in range(N_HEADS):
        sl = slice(HEAD_SLAB * hh, HEAD_SLAB * (hh + 1))
        mk_ref[0, :, sl] = (kk[:, sl] + kr).astype(BF16)
    ones_row = lax.broadcasted_iota(I32, (N_HEADS * VT_ROWS, tm), 0) % VT_ROWS == MLA_V
    mvt_ref[0, 0] = jnp.where(ones_row, 1.0, _dot_nt(wvt_ref[...], kvn)).astype(BF16)
    dvt_ref[0, 0] = jnp.where(ones_row, 1.0, _dot_nt(wdvt_ref[...], hb)).astype(BF16)

    lqk_ref[0] = z[:, ZMLQK:ZMLQK + 512]
    lv_ref[0] = z[:, ZMLV:ZMLV + 256].astype(BF16)
    lo_ref[0] = z[:, ZMLO:ZMLO + 256].astype(BF16)
    gcol = z[:, ZGATE:ZGATE + 128] + gbr_ref[...]
    gc_ref[0] = jnp.where(lane < 8, gcol, jnp.where(lane < 16, _log_sigmoid(gcol), 0.0))
    zr = _dot_nt(wgt_ref[...], hb) + gbc_ref[...]
    rowi = lax.broadcasted_iota(I32, zr.shape, 0)
    grow_ref[0] = jnp.where(rowi < 8, zr, _log_sigmoid(zr))

    gq_o[0] = (z[:, ZGQ:ZGQ + 128] * GLA_DK ** -0.5).astype(BF16)
    gk_o[0] = z[:, ZGK:ZGK + 128].astype(BF16)
    gv_o[0] = z[:, ZGV:ZGV + 256].astype(BF16)
    gr_o[0] = z[:, ZGR:ZGR + 256].astype(BF16)
    zg = _dot(z[:, ZGA:ZGA + 128].astype(BF16), wal_ref[...]) + bal_ref[...]
    glg_o[0] = _log_sigmoid(zg) * (1.0 / GLA_TAU)

    dscale = DIFF_DQK ** -0.5 * LOG2E
    for g in range(2):
        gs = slice(128 * g, 128 * (g + 1))
        qg = (z[:, ZDQ:ZDQ + 256][:, gs] * cd + z[:, ZDQS:ZDQS + 256][:, gs] * sd) * dscale
        kg = z[:, ZDK:ZDK + 256][:, gs] * cd + z[:, ZDKS:ZDKS + 256][:, gs] * sd
        for hl in range(2):
            hh = 2 * g + hl
            for m in range(2):
                lo = 64 * hl + 32 * m
                dqt_ref[0, 2 * hh + m] = jnp.where((lane >= lo) & (lane < lo + 32), qg, 0.0).T.astype(BF16)
            dk_ref[0, :, HEAD_SLAB * hh:HEAD_SLAB * (hh + 1)] = jnp.where(
                (lane >= 64 * hl) & (lane < 64 * hl + 64), kg, 0.0).astype(BF16)


def _proj(x, mod, tab, lw):
    b, t, _ = x.shape
    tm = KV_CHUNK
    consts = [lw["g_mix_pre"], lw["w_ext"], None, lw["g_q_lat"], lw["wq"], lw["wqs"], lw["g_kv_lat"], lw["wk"], lw["wvt"],
              lw["wgt"], lw["gate_bias_row"], lw["gate_bias_col"], lw["walpha"], lw["balpha"], lw["wdvt"]]
    in_specs = [pl.BlockSpec((1, tm, D), lambda bi, i: (bi, i, 0)), pl.BlockSpec((1, 8, D), lambda bi, i: (bi, 0, 0))]
    args = [x, mod]
    for cst in consts:
        if cst is None:
            in_specs.append(pl.BlockSpec((tm, 512), lambda bi, i: (i, 0)))
            args.append(tab)
        else:
            in_specs.append(_const_spec(cst.shape))
            args.append(cst)
    out_specs = [pl.BlockSpec((1, tm, w), lambda bi, i: (bi, i, 0)) for _, w, _ in _PROJ_OUT]
    out_shape = [jax.ShapeDtypeStruct((b, t, w), dt) for _, w, dt in _PROJ_OUT]
    out_specs.append(pl.BlockSpec((1, 16, tm), lambda bi, i: (bi, 0, i)))
    out_shape.append(jax.ShapeDtypeStruct((b, 16, t), F32))
    for _ in range(2):
        out_specs.append(pl.BlockSpec((1, 1, N_HEADS * VT_ROWS, tm), lambda bi, i: (bi, i, 0, 0)))
        out_shape.append(jax.ShapeDtypeStruct((b, t // tm, N_HEADS * VT_ROWS, tm), BF16))
    for nslab in (N_HEADS, 2 * N_HEADS):
        out_specs.append(pl.BlockSpec((1, nslab, HEAD_SLAB, tm), lambda bi, i: (bi, 0, 0, i)))
        out_shape.append(jax.ShapeDtypeStruct((b, nslab, HEAD_SLAB, t), BF16))
    outs = pl.pallas_call(
        _proj_kernel, name="proj", grid=(b, t // tm), in_specs=in_specs, out_specs=out_specs, out_shape=out_shape,
        compiler_params=_cparams(("parallel", "arbitrary")),
    )(*args)
    res = {name: o for (name, _, _), o in zip(_PROJ_OUT, outs[:-5])}
    res["grow"], res["mvt"], res["dvt"], res["mqt"], res["dqt"] = outs[-5:]
    return res


def _flash_kernel(*refs, nmap, has_lat, tk, finish, post):
    if has_lat:
        q_ref, kc_ref, vc_ref, kl_ref, vl_ref, dl_ref, g_ref, o_ref, s_ref, acc_ref = refs
    else:
        q_ref, kc_ref, vc_ref, dl_ref, g_ref, o_ref = refs
    tq = q_ref.shape[3]
    qt = q_ref[0, 0] if nmap == 1 else jnp.concatenate([q_ref[0, mm] for mm in range(nmap)], axis=1)
    rows = nmap * tq
    sub = tk // KV_CHUNK

    def softmax(s, smax, m):
        m_new = jnp.maximum(m, smax)
        return m_new, jnp.exp2(m - m_new), jnp.exp2(s - m_new).astype(BF16)

    def pv(p, vts):
        return _dot(vts[0] if len(vts) == 1 else jnp.concatenate(vts, axis=1), p)

    s_ctx = _dot(kc_ref[0], qt)
    if has_lat:
        n = kl_ref.shape[1] // tk
        unroll = min(FLASH_UNROLL, n)

        def scores(j):
            if isinstance(j, int):
                return _dot(kl_ref[0, j * tk:(j + 1) * tk, :], qt)
            off = pl.multiple_of(j * tk, tk)
            return _dot(kl_ref[0, pl.ds(off, tk), :], qt)

        def values(j):
            return [vl_ref[0, j * sub + c] for c in range(sub)]

        def produce(slot, j):
            s = scores(j)
            s_ref[slot] = s
            return jnp.max(s, axis=0, keepdims=True)

        smax0 = produce(0, 0)

    m, _, p = softmax(s_ctx, jnp.max(s_ctx, axis=0, keepdims=True), jnp.full((1, rows), NEG, F32))
    acc = pv(p, [vc_ref[0, 0]])
    if has_lat:
        acc_ref[...] = acc

        def body(jj, carry):
            m, smax = carry
            j = unroll * jj
            for u in range(unroll):
                smax_next = smax
                if not isinstance(j, int):
                    smax_next = produce((u + 1) % 2, jnp.minimum(j + u + 1, n - 1))
                elif j + u + 1 < n:
                    smax_next = produce((u + 1) % 2, j + u + 1)
                m, alpha, p = softmax(s_ref[u % 2], smax, m)
                acc_ref[...] = alpha * acc_ref[...] + pv(p, values(j + u))
                smax = smax_next
            return m, smax

        if unroll == n:
            body(0, (m, smax0))
        else:
            lax.fori_loop(0, n // unroll, body, (m, smax0))
        acc = acc_ref[...]

    o = acc[0:MLA_V, :] / acc[MLA_V:MLA_V + 1, :]
    if nmap == 2:
        lv = dl_ref[...]
        lam = (jnp.exp(jnp.sum(lv[0:1] * lv[1:2], axis=-1, keepdims=True))
               - jnp.exp(jnp.sum(lv[2:3] * lv[3:4], axis=-1, keepdims=True)) + (1.0 - post))
        o = o[:, :tq] - lam * o[:, tq:]
    if finish:
        ms = jnp.mean(o * o, axis=0, keepdims=True)
        o = o * lax.rsqrt(ms + EPS) * jnp.concatenate([g_ref[0]] * (tq // LANE), axis=1) * post
    o_pad = jnp.concatenate([o, jnp.zeros((HEAD_SLAB - MLA_V, tq), F32)], axis=0)
    o_ref[0] = o_pad.T.astype(BF16)


def _flash(q, kc, vct, kl, vlt, dlam, g_out, *, nmap, finish, post, tq, tk):
    b, _, _, t = q.shape
    has_lat = kl is not None
    assert kc.shape[1] == KV_CHUNK and tk % KV_CHUNK == 0
    kspec = lambda n: pl.BlockSpec((1, n, HEAD_SLAB), lambda bi, h, i: (bi, 0, h))
    vspec = lambda n: pl.BlockSpec((1, n // KV_CHUNK, VT_ROWS, KV_CHUNK), lambda bi, h, i: (bi, 0, h, 0))
    in_specs = [pl.BlockSpec((1, nmap, HEAD_SLAB, tq), lambda bi, h, i: (bi, h, 0, i)), kspec(KV_CHUNK), vspec(KV_CHUNK)]
    args = [q, kc, vct]
    scratch = []
    if has_lat:
        tl = kl.shape[1]
        assert (tl // tk) % min(FLASH_UNROLL, tl // tk) == 0
        in_specs += [kspec(tl), vspec(tl)]
        args += [kl, vlt]
        scratch = [pltpu.VMEM((2, tk, nmap * tq), F32), pltpu.VMEM((VT_ROWS, nmap * tq), F32)]
    in_specs += [_const_spec(dlam.shape), pl.BlockSpec((1, MLA_V, LANE), lambda bi, h, i: (h, 0, 0))]
    args += [dlam, g_out]
    return pl.pallas_call(
        functools.partial(_flash_kernel, nmap=nmap, has_lat=has_lat, tk=tk, finish=finish, post=post),
        name="flash_diff" if nmap == 2 else "flash_mla",
        grid=(b, N_HEADS, t // tq), in_specs=in_specs,
        out_specs=pl.BlockSpec((1, tq, HEAD_SLAB), lambda bi, h, i: (bi, i, h)),
        out_shape=jax.ShapeDtypeStruct((b, t, N_HEADS * HEAD_SLAB), BF16),
        scratch_shapes=scratch,
        compiler_params=_cparams(("parallel", "parallel", "arbitrary")),
    )(*args)


def _head_of(shape, axis, width):
    return (lax.broadcasted_iota(I32, shape, axis) % (N_HEADS * width)) // width


def _mlstm_conv(first, last, x, xprev, xnext, wc, bcv):
    n = x.shape[0]
    row = lax.broadcasted_iota(I32, x.shape, 0)
    pr = jnp.where(first, 0.0, xprev[7:8, :])
    nx = jnp.where(last, 0.0, xnext[0:1, :])
    xm = jnp.where(row == 0, pr, pltpu.roll(x, 1, 0))
    xp = jnp.where(row == n - 1, nx, pltpu.roll(x, n - 1, 0))
    y = xm * wc[0:1] + x * wc[1:2] + xp * wc[2:3] + bcv
    qk = y * _sigmoid(y)
    return qk[:, :256], qk[:, 256:] * ML_DH ** -0.5


def _mlstm_dir(d, q, k, v, gcol, grow, cb, m0e):
    L = q.shape[0]
    li = lax.broadcasted_iota(I32, (L, L), 0)
    si = lax.broadcasted_iota(I32, (L, L), 1)
    tin = (si <= li) if d == 0 else (si >= li)
    tinb = tin.astype(BF16)
    bcol = _dot_sel(tinb, gcol)
    brow = sum(_dot_nt(t, tinb) for t in _split_bf16(grow, 3))
    hm256 = _head_of((L, 256), 1, ML_DH)
    hm512 = _head_of((L, 512), 1, ML_DH)
    e_idx = L - 1 if d == 0 else 0

    d_blk, inter_blk = [], []
    for hh in range(N_HEADS):
        c = 4 * d + hh
        bc = bcol[:, 8 + c:9 + c]
        d_blk.append(jnp.where(tin, bc - brow[8 + c:9 + c, :] + grow[c:c + 1, :], NEG))
        inter_blk.append(bc + m0e[0:1, 64 * hh:64 * hh + 1])
    d_st = jnp.concatenate(d_blk, axis=0)
    inter_st = jnp.concatenate(inter_blk, axis=0)
    mt = jnp.maximum(inter_st, jnp.max(d_st, axis=-1, keepdims=True))
    q_st = jnp.concatenate([jnp.where(hm256 == hh, q, 0.0) for hh in range(N_HEADS)], axis=0).astype(BF16)
    s_st = (jnp.exp(d_st - mt) * _dot_nt(q_st, k.astype(BF16))).astype(BF16)
    vext = jnp.concatenate([v, jnp.ones((L, 256), BF16)], axis=1)
    r = _dot(s_st, vext)
    aint = jnp.exp(inter_st - mt)
    p = _dot(q.astype(BF16), cb.astype(BF16))
    tot = jnp.zeros((L, 512), F32)
    mte = jnp.zeros((L, 256), F32)
    for hh in range(N_HEADS):
        rs = slice(hh * L, (hh + 1) * L)
        tot = jnp.where(hm512 == hh, r[rs] + aint[rs] * p, tot)
        mte = jnp.where(hm256 == hh, mt[rs], mte)
    hout = tot[:, :256] / jnp.maximum(jnp.abs(tot[:, 256:]), jnp.exp(-mte))

    wexp = jnp.zeros((L, 256), F32)
    arow = jnp.zeros((1, 512), F32)
    grw = jnp.zeros((1, 512), F32)
    mnew = jnp.zeros((1, 256), F32)
    hr512 = _head_of((1, 512), 1, ML_DH)
    hr256 = _head_of((1, 256), 1, ML_DH)
    for hh in range(N_HEADS):
        c = 4 * d + hh
        bc = bcol[:, 8 + c:9 + c]
        be = bc[e_idx:e_idx + 1, :]
        wl = be - bc + gcol[:, c:c + 1]
        mloc = jnp.max(wl, axis=0, keepdims=True)
        m0h = m0e[0:1, 64 * hh:64 * hh + 1]
        mn = jnp.maximum(be + m0h, mloc)
        wexp = jnp.where(hm256 == hh, jnp.exp(wl - mloc), wexp)
        arow = jnp.where(hr512 == hh, jnp.exp(be + m0h - mn), arow)
        grw = jnp.where(hr512 == hh, jnp.exp(mloc - mn), grw)
        mnew = jnp.where(hr256 == hh, mn, mnew)
    cl = _dot_tn((k * wexp).astype(BF16), vext)
    bd = lax.broadcasted_iota(I32, (256, 512), 0) // ML_DH == _head_of((256, 512), 1, ML_DH)
    return hout, arow * cb + jnp.where(bd, grw * cl, 0.0), jnp.broadcast_to(mnew, (8, 256))


def _mlstm_kernel(xf, xfp, xfn, xb, xbp, xbn, vf, vb, gcf, gcb, grf, grb, wc_ref, bc_ref, c0_ref, m0_ref,
                  hf_ref, hb_ref, c_ref, m_ref):
    i = pl.program_id(1)
    n = pl.num_programs(1)

    @pl.when(i == 0)
    def _():
        c_ref[...] = c0_ref[...]
        m_ref[...] = m0_ref[...]

    wc = wc_ref[...]
    bcv = bc_ref[...]
    L = ML_CHUNK
    nsub = xf.shape[1] // L
    streams = ((0, i == 0, i == n - 1, xf, xfp, xfn, vf, gcf, grf, hf_ref),
               (1, i == n - 1, i == 0, xb, xbp, xbn, vb, gcb, grb, hb_ref))
    for d, first, last, x, xp, xn, v, gc, gr, h_ref in streams:
        q, k = _mlstm_conv(first, last, x[0], xp[0], xn[0], wc, bcv)
        cb, m0e = c_ref[0, d], m_ref[0, d]
        for c in (range(nsub) if d == 0 else reversed(range(nsub))):
            sl = slice(c * L, (c + 1) * L)
            h_ref[0, sl, :], cb, m0e = _mlstm_dir(d, q[sl], k[sl], v[0, sl, :], gc[0, sl, :], gr[0, :, sl], cb, m0e)
        c_ref[0, d], m_ref[0, d] = cb, m0e


def _mlstm(pr, w_conv, b_conv, c0, m0):
    x, v, gc, gr = pr["lqk"], pr["lv"], pr["gc"], pr["grow"]
    b, t, _ = x.shape
    L = min(ML_BLOCK, t)
    n = t // L
    r8 = L // 8
    last8 = t // 8 - 1

    def fw(bi, i):
        return (bi, i, 0)

    def bw(bi, i):
        return (bi, n - 1 - i, 0)

    def halo(ix, shift):
        def f(bi, i):
            blk = ix(bi, i)[1]
            return (bi, jnp.clip(blk * r8 + shift, 0, last8), 0)
        return f

    main = lambda w, ix: pl.BlockSpec((1, L, w), ix)
    in_specs = [main(512, fw), pl.BlockSpec((1, 8, 512), halo(fw, -1)), pl.BlockSpec((1, 8, 512), halo(fw, r8)),
                main(512, bw), pl.BlockSpec((1, 8, 512), halo(bw, -1)), pl.BlockSpec((1, 8, 512), halo(bw, r8)),
                main(256, fw), main(256, bw), main(128, fw), main(128, bw),
                pl.BlockSpec((1, 16, L), lambda bi, i: (bi, 0, i)), pl.BlockSpec((1, 16, L), lambda bi, i: (bi, 0, n - 1 - i)),
                _const_spec(w_conv.shape), _const_spec(b_conv.shape),
                pl.BlockSpec((1, 2, 256, 512), lambda bi, i: (bi, 0, 0, 0)), pl.BlockSpec((1, 2, 8, 256), lambda bi, i: (bi, 0, 0, 0))]
    out_specs = [main(256, fw), main(256, bw),
                 pl.BlockSpec((1, 2, 256, 512), lambda bi, i: (bi, 0, 0, 0)), pl.BlockSpec((1, 2, 8, 256), lambda bi, i: (bi, 0, 0, 0))]
    out_shape = [jax.ShapeDtypeStruct((b, t, 256), F32), jax.ShapeDtypeStruct((b, t, 256), F32),
                 jax.ShapeDtypeStruct(c0.shape, F32), jax.ShapeDtypeStruct(m0.shape, F32)]
    return pl.pallas_call(
        _mlstm_kernel, name="mlstm", grid=(b, n), in_specs=in_specs, out_specs=out_specs, out_shape=out_shape,
        compiler_params=_cparams(("parallel", "arbitrary")),
    )(x, x, x, x, x, x, v, v, gc, gc, gr, gr, w_conv, b_conv, c0, m0)


def _gla_chunk(d, q, k, v, lg, sb):
    L = q.shape[0]
    li = lax.broadcasted_iota(I32, (L, L), 0)
    si = lax.broadcasted_iota(I32, (L, L), 1)
    tin = (si <= li) if d == 0 else (si >= li)
    lgd = lg[:, 128 * d:128 * (d + 1)]
    gcum = _dot_sel(tin.astype(BF16), lgd)
    e_idx = L - 1 if d == 0 else 0
    gend = gcum[e_idx:e_idx + 1, :]
    qf, kf = q.astype(F32), k.astype(F32)
    q_dec = qf * jnp.exp(gcum)
    k_dec = (kf * jnp.exp(-gcum)).astype(BF16)
    k_end = (kf * jnp.exp(gend - gcum)).astype(BF16)
    hm128 = _head_of((L, 128), 1, GLA_DK)
    hm256 = _head_of((L, 256), 1, GLA_DV)
    q_st = jnp.concatenate([jnp.where(hm128 == hh, q_dec, 0.0) for hh in range(N_HEADS)], axis=0).astype(BF16)
    att = _dot_nt(q_st, k_dec)
    tin4 = jnp.concatenate([tin] * N_HEADS, axis=0)
    o_st = _dot(jnp.where(tin4, att, 0.0).astype(BF16), v)
    o = _dot_nt(q_dec.astype(BF16), sb.astype(BF16))
    for hh in range(N_HEADS):
        o = o + jnp.where(hm256 == hh, o_st[hh * L:(hh + 1) * L], 0.0)
    bd = lax.broadcasted_iota(I32, (256, 128), 0) // GLA_DV == _head_of((256, 128), 1, GLA_DK)
    return o, jnp.exp(gend) * sb + jnp.where(bd, _dot_tn(v, k_end), 0.0)


def _gla_kernel(qf, kf, vf, lf, qb, kb, vb, lb, s0_ref, of_ref, ob_ref, s_ref):
    i = pl.program_id(1)

    @pl.when(i == 0)
    def _():
        s_ref[...] = s0_ref[...]

    L = GLA_CHUNK
    nsub = qf.shape[1] // L
    for d, (q, k, v, lg, o_ref) in enumerate(((qf, kf, vf, lf, of_ref), (qb, kb, vb, lb, ob_ref))):
        sb = s_ref[0, d]
        for c in (range(nsub) if d == 0 else reversed(range(nsub))):
            sl = slice(c * L, (c + 1) * L)
            o_ref[0, sl, :], sb = _gla_chunk(d, q[0, sl, :], k[0, sl, :], v[0, sl, :], lg[0, sl, :], sb)
        s_ref[0, d] = sb


def _gla(pr, s0):
    q, k, v, lg = pr["gq"], pr["gk"], pr["gv"], pr["glg"]
    b, t, _ = q.shape
    L = min(GLA_BLOCK, t)
    n = t // L
    fw = lambda bi, i: (bi, i, 0)
    bw = lambda bi, i: (bi, n - 1 - i, 0)
    blk = lambda w, ix: pl.BlockSpec((1, L, w), ix)
    st_spec = pl.BlockSpec((1, 2, 256, 128), lambda bi, i: (bi, 0, 0, 0))
    return pl.pallas_call(
        _gla_kernel, name="gla", grid=(b, n),
        in_specs=[blk(128, fw), blk(128, fw), blk(256, fw), blk(256, fw),
                  blk(128, bw), blk(128, bw), blk(256, bw), blk(256, bw), st_spec],
        out_specs=[blk(256, fw), blk(256, bw), st_spec],
        out_shape=[jax.ShapeDtypeStruct((b, t, 256), F32), jax.ShapeDtypeStruct((b, t, 256), F32),
                   jax.ShapeDtypeStruct(s0.shape, F32)],
        compiler_params=_cparams(("parallel", "arbitrary")),
    )(q, k, v, lg, q, k, v, lg, s0)


def _head_rms_expanded(x, width):
    n = x.shape[1]
    bd = (lax.broadcasted_iota(I32, (n, n), 0) // width == lax.broadcasted_iota(I32, (n, n), 1) // width).astype(BF16)
    return sum(_dot(t, bd) for t in _split_bf16(x * x, 2)) * (1.0 / width)


def _merge_kernel(x_ref, mod_ref, ya_ref, hf_ref, hb_ref, lo_ref, gf_ref, gb_ref, gr_ref, yd_ref,
                  gpre_ref, wg_ref, bg_ref, wbr_ref, wo_ref, gpost_ref, gffn_ref, wr_ref, wrt_ref, gml_ref, ggla_ref,
                  xm_ref, h2_ref, aff_ref, afft_ref):
    x = x_ref[0]
    tm = x.shape[0]
    mod = mod_ref[0]
    hb = (_rms(x) * gpre_ref[...] * (1.0 + mod[1:2]) + mod[0:1]).astype(BF16)

    hs = hf_ref[0] + hb_ref[0]
    y_ml = _sigmoid(lo_ref[0].astype(F32)) * (hs * lax.rsqrt(_head_rms_expanded(hs, ML_DH) + EPS) * gml_ref[...])
    gs = gf_ref[0] + gb_ref[0]
    rr = gr_ref[0].astype(F32)
    y_gla = rr * _sigmoid(rr) * (gs * lax.rsqrt(_head_rms_expanded(gs, GLA_DV) + EPS) * ggla_ref[...])

    branches = ((ya_ref[0], 0, 512), (y_ml.astype(BF16), 512, 256), (y_gla.astype(BF16), 768, 256), (yd_ref[0], 1024, 512))
    mix = jnp.zeros((tm, D), F32)
    for nb, (yb, r0, rw) in enumerate(branches):
        gate = _sigmoid(_dot(hb, wg_ref[:, nb * D:(nb + 1) * D]) + bg_ref[:, nb * D:(nb + 1) * D])
        mix = mix + gate * _dot(yb, wbr_ref[r0:r0 + rw, :])
    y = _dot(mix.astype(BF16), wo_ref[...])
    xm = x + mod[2:3] * (_rms(y) * gpost_ref[...])
    xm_ref[0] = xm

    h2 = (_rms(xm) * gffn_ref[...] * (1.0 + mod[4:5]) + mod[3:4]).astype(BF16)
    h2_ref[0] = h2
    lane = lax.broadcasted_iota(I32, (tm, LANE), 1)
    lg = jnp.where(lane < N_EXPERTS, _dot(h2, wr_ref[...]), NEG)
    e = jnp.exp(lg - jnp.max(lg, axis=-1, keepdims=True))
    aff_ref[0] = (e / jnp.sum(e, axis=-1, keepdims=True))[:, :N_EXPERTS]
    lt = _dot_nt(wrt_ref[...], h2)
    et = jnp.exp(lt - jnp.max(lt, axis=0, keepdims=True))
    afft_ref[0] = et / jnp.sum(et, axis=0, keepdims=True)


def _merge(x, mod, ya, ml, lo, gl, gr, yd, lw, tm):
    b, t, _ = x.shape
    tok = lambda w: pl.BlockSpec((1, tm, w), lambda bi, i: (bi, i, 0))
    consts = [lw["g_mix_pre"], lw["w_gate"], lw["b_gate"], lw["wbr"], lw["w_out"], lw["g_mix_post"], lw["g_ffn_pre"],
              lw["w_router"], lw["w_router_t"], lw["g_mlstm_out"], lw["g_gla_out"]]
    in_specs = [tok(D), pl.BlockSpec((1, 8, D), lambda bi, i: (bi, 0, 0)), tok(512), tok(256), tok(256), tok(256),
                tok(256), tok(256), tok(256), tok(512)] + [_const_spec(c.shape) for c in consts]
    out_specs = [tok(D), tok(D), tok(N_EXPERTS), pl.BlockSpec((1, N_EXPERTS, tm), lambda bi, i: (bi, 0, i))]
    out_shape = [jax.ShapeDtypeStruct((b, t, D), F32), jax.ShapeDtypeStruct((b, t, D), BF16),
                 jax.ShapeDtypeStruct((b, t, N_EXPERTS), F32), jax.ShapeDtypeStruct((b, N_EXPERTS, t), F32)]
    return pl.pallas_call(
        _merge_kernel, name="merge", grid=(b, t // tm), in_specs=in_specs, out_specs=out_specs, out_shape=out_shape,
        compiler_params=_cparams(("parallel", "arbitrary")),
    )(x, mod, ya, ml[0], ml[1], lo, gl[0], gl[1], gr, yd, *consts)


def _topk_kernel(a_ref, pos_ref, s0_ref, *, cap):
    nblk = a_ref.shape[1]
    bits = pltpu.bitcast(a_ref[0], I32)

    def bisect(i, thr):
        cand = thr | (1 << (30 - i))
        cnt = jnp.sum((bits >= cand).astype(I32), axis=(0, 2), keepdims=True)
        return jnp.where(cnt >= cap, cand, thr)

    thr3 = lax.fori_loop(0, 31, bisect, jnp.zeros((1, N_EXPERTS, 1), I32))
    need3 = cap - jnp.sum((bits > thr3).astype(I32), axis=(0, 2), keepdims=True)
    thr, need = thr3[0], need3[0].astype(F32)
    upper = (lax.broadcasted_iota(I32, (TOK_BLK, TOK_BLK), 0) <= lax.broadcasted_iota(I32, (TOK_BLK, TOK_BLK), 1)).astype(BF16)

    def blk(j, carry):
        c_eq, c_sel = carry
        bj = pltpu.bitcast(a_ref[0, j], I32)
        gt, eq = bj > thr, bj == thr
        cum_eq = _dot(eq.astype(BF16), upper) + c_eq
        sel = gt | (eq & (cum_eq <= need))
        cum_sel = _dot(sel.astype(BF16), upper) + c_sel
        pos_ref[0, j] = jnp.where(sel, cum_sel - 1.0, -1.0).astype(I32)
        s0_ref[0, j] = jnp.broadcast_to(c_sel, (N_EXPERTS, LANE)).astype(I32)
        return cum_eq[:, TOK_BLK - 1:TOK_BLK], cum_sel[:, TOK_BLK - 1:TOK_BLK]

    zero = jnp.zeros((N_EXPERTS, 1), F32)
    lax.fori_loop(0, nblk, blk, (zero, zero))


def _topk(aff_t, cap):
    b, _, t = aff_t.shape
    nblk = t // TOK_BLK
    a4 = aff_t.reshape(b, N_EXPERTS, nblk, TOK_BLK).transpose(0, 2, 1, 3)
    spec = lambda w: pl.BlockSpec((1, nblk, N_EXPERTS, w), lambda bi: (bi, 0, 0, 0))
    return pl.pallas_call(
        functools.partial(_topk_kernel, cap=cap), name="topk", grid=(b,),
        in_specs=[spec(TOK_BLK)], out_specs=[spec(TOK_BLK), spec(LANE)],
        out_shape=[jax.ShapeDtypeStruct((b, nblk, N_EXPERTS, TOK_BLK), I32),
                   jax.ShapeDtypeStruct((b, nblk, N_EXPERTS, LANE), I32)],
        compiler_params=_cparams(("parallel",)),
    )(a4)


def _moe_kernel(s0_ref, pos_ref, h_ref, wg_ref, wu_ref, wd_ref, ys_ref, xs_ref, *, nbatch, nblk, nsub, rows_step):
    q, tb = pl.program_id(0), pl.program_id(1)
    cur = q % 2

    @pl.when(q < N_EXPERTS * nbatch)
    def _():
        e, bi = q // nbatch, q % nbatch

        @pl.when(tb == 0)
        def _():
            xs_ref[cur] = jnp.zeros(xs_ref.shape[1:], BF16)

        base = (bi * N_EXPERTS + e) * (nblk + 1) + tb * nsub
        s0s = [s0_ref[base + sb] for sb in range(nsub + 1)]
        a0s = [pl.multiple_of((s0 // 16) * 16, 16) for s0 in s0s[:-1]]
        spans = [s0s[sb + 1] - a0s[sb] for sb in range(nsub)]

        def gather(win, sb):
            a0 = a0s[sb]
            prow = pos_ref[0, sb, pl.ds(e, 1), :]
            slot = lax.broadcasted_iota(I32, (win, TOK_BLK), 0) + a0
            rows = _dot((slot == prow).astype(BF16), h_ref[0, sb * TOK_BLK:(sb + 1) * TOK_BLK, :])
            xs_ref[cur, pl.ds(a0, win), :] = xs_ref[cur, pl.ds(a0, win), :] + rows.astype(BF16)

        all_small = functools.reduce(jnp.logical_and, [sp <= GATHER_WIN_SMALL for sp in spans])

        @pl.when(all_small)
        def _():
            for sb in range(nsub):
                gather(GATHER_WIN_SMALL, sb)

        @pl.when(jnp.logical_not(all_small))
        def _():
            for sb in range(nsub):
                nonempty = s0s[sb + 1] > s0s[sb]
                pl.when(nonempty & (spans[sb] <= GATHER_WIN_SMALL))(functools.partial(gather, GATHER_WIN_SMALL, sb))
                pl.when(spans[sb] > GATHER_WIN_SMALL)(functools.partial(gather, GATHER_WIN, sb))

    @pl.when(q > 0)
    def _():
        rows = min(TOK_BLK, rows_step)
        for c in range(rows_step // rows):
            off = pl.multiple_of(tb * rows_step + c * rows, rows)
            xc = xs_ref[1 - cur, pl.ds(off, rows), :]
            hg = _dot(xc, wg_ref[0, 0])
            hid = (hg * _sigmoid(hg) * _dot(xc, wu_ref[0, 0])).astype(BF16)
            ys_ref[0, 0, pl.ds(off, rows), :] = _dot(hid, wd_ref[0, 0]).astype(BF16)


def _moe(s0_flat, pos4, h2, lw, capp):
    b, t, _ = h2.shape
    layer = lw["layer"]
    nblk = t // TOK_BLK
    tok = min(MOE_TOK, t)
    nsub = tok // TOK_BLK
    nstep = t // tok
    rows_step = capp // nstep
    assert rows_step % min(TOK_BLK, rows_step) == 0 and rows_step % 16 == 0
    npair = N_EXPERTS * b
    gat = lambda q: jnp.minimum(q, npair - 1)
    ffn = lambda q: jnp.maximum(q - 1, 0)
    grid_spec = pltpu.PrefetchScalarGridSpec(
        num_scalar_prefetch=1, grid=(npair + 1, nstep),
        in_specs=[pl.BlockSpec((1, nsub, N_EXPERTS, TOK_BLK), lambda q, tb, s: (gat(q) % b, tb, 0, 0)),
                  pl.BlockSpec((1, tok, D), lambda q, tb, s: (gat(q) % b, tb, 0)),
                  pl.BlockSpec((1, 1, D, EXPERT_FF), lambda q, tb, s: (layer, ffn(q) // b, 0, 0)),
                  pl.BlockSpec((1, 1, D, EXPERT_FF), lambda q, tb, s: (layer, ffn(q) // b, 0, 0)),
                  pl.BlockSpec((1, 1, EXPERT_FF, D), lambda q, tb, s: (layer, ffn(q) // b, 0, 0))],
        out_specs=pl.BlockSpec((1, 1, capp, D), lambda q, tb, s: (ffn(q) % b, ffn(q) // b, 0, 0)),
        scratch_shapes=[pltpu.VMEM((2, capp + GATHER_WIN, D), BF16)])
    return pl.pallas_call(
        functools.partial(_moe_kernel, nbatch=b, nblk=nblk, nsub=nsub, rows_step=rows_step), name="moe",
        grid_spec=grid_spec, out_shape=jax.ShapeDtypeStruct((b, N_EXPERTS, capp, D), BF16),
        compiler_params=_cparams(("arbitrary", "arbitrary")),
    )(s0_flat, pos4, h2, lw["w_e_gate"], lw["w_e_up"], lw["w_e_down"])


def _combine_kernel(s0_ref, *refs, nblk, nb, sblk):
    ys_refs = refs[:2 * N_EXPERTS]
    pos_ref, aff_ref, xm_ref, mod_ref, g_ref, o_ref = refs[2 * N_EXPERTS:]
    bi, tb = pl.program_id(0), pl.program_id(1)
    pos = pos_ref[0]
    aff = aff_ref[0]
    lane = lax.broadcasted_iota(I32, (TOK_BLK, 2 * sblk), 1)
    acc = jnp.zeros((TOK_BLK, D), F32)
    for e in range(N_EXPERTS):
        s0 = s0_ref[(bi * N_EXPERTS + e) * (nblk + 1) + tb]
        blk0 = jnp.minimum(s0 // sblk, nb - 1)
        rel = pos[:, e:e + 1] - blk0 * sblk
        ysw = jnp.concatenate([ys_refs[2 * e][0, 0], ys_refs[2 * e + 1][0, 0]], axis=0)
        acc = acc + aff[:, e:e + 1] * _dot((lane == rel).astype(BF16), ysw)
    mod = mod_ref[0]
    o_ref[0] = xm_ref[0] + mod[5:6] * (_rms(acc) * g_ref[...])


def _combine(s0_flat, ys, pos_t, aff, xm, mod, g_post, sblk):
    b, t, _ = xm.shape
    nblk = t // TOK_BLK
    nb = ys.shape[2] // sblk

    def ys_spec(e, k):
        def ix(bi, tb, s):
            blk0 = jnp.minimum(s[(bi * N_EXPERTS + e) * (nblk + 1) + tb] // sblk, nb - 1)
            return (bi, e, jnp.minimum(blk0 + k, nb - 1), 0)
        return pl.BlockSpec((1, 1, sblk, D), ix)

    tok = lambda w: pl.BlockSpec((1, TOK_BLK, w), lambda bi, tb, s: (bi, tb, 0))
    in_specs = [ys_spec(e, k) for e in range(N_EXPERTS) for k in range(2)]
    in_specs += [tok(N_EXPERTS), tok(N_EXPERTS), tok(D), pl.BlockSpec((1, 8, D), lambda bi, tb, s: (bi, 0, 0)),
                 pl.BlockSpec((1, D), lambda bi, tb, s: (0, 0))]
    grid_spec = pltpu.PrefetchScalarGridSpec(num_scalar_prefetch=1, grid=(b, nblk), in_specs=in_specs, out_specs=tok(D))
    return pl.pallas_call(
        functools.partial(_combine_kernel, nblk=nblk, nb=nb, sblk=sblk), name="combine", grid_spec=grid_spec,
        out_shape=jax.ShapeDtypeStruct((b, t, D), F32),
        compiler_params=_cparams(("arbitrary", "arbitrary")),
    )(s0_flat, *([ys] * (2 * N_EXPERTS)), pos_t, aff, xm, mod, g_post)


def _rope_table(t):
    nf = ROPE_DIM // 4
    pos = jnp.arange(t)
    inv = ROPE_BASE ** (-jnp.arange(nf, dtype=F32) / nf)
    ang = jnp.stack([pos // GRID_W, pos % GRID_W], axis=-1).astype(F32)[..., None] * inv
    cos, sin = jnp.cos(ang), jnp.sin(ang)
    c32 = jnp.stack([cos, cos], axis=2).reshape(t, ROPE_DIM)
    s32 = jnp.stack([-sin, sin], axis=2).reshape(t, ROPE_DIM)
    one, zero = jnp.ones((t, 64), F32), jnp.zeros((t, 32), F32)
    ct = jnp.concatenate([one, c32, zero], axis=1)
    st = jnp.concatenate([0.0 * one, s32, zero], axis=1)
    return jnp.concatenate([ct, st, jnp.tile(c32, (1, 4)), jnp.tile(s32, (1, 4))], axis=1)


def _identity_table(t):
    one, zero = jnp.ones((t, 128), F32), jnp.zeros((t, 128), F32)
    ct = jnp.concatenate([jnp.ones((t, 96), F32), jnp.zeros((t, 32), F32)], axis=1)
    return jnp.concatenate([ct, zero, one, zero], axis=1)


def _layer_weights(i, p):
    lw = {}
    row = lambda a: a.reshape(1, -1)
    for name in ("g_mix_pre", "g_mix_post", "g_ffn_pre", "g_ffn_post", "g_q_lat", "g_kv_lat", "g_mlstm_out", "g_gla_out"):
        lw[name] = row(p[name][i])
    lw["w_ext"] = _gather_cols(p["w_in"][i], _WIN_IDX).astype(BF16)
    lw["wgt"] = p["w_in"][i][:, 416 + 1024:416 + 1040].T.astype(BF16)
    gb = jnp.concatenate([p["b_igate"][i].reshape(-1), p["b_fgate"][i].reshape(-1)])
    lw["gate_bias_row"] = jnp.pad(gb, (0, LANE - 16)).reshape(1, LANE)
    lw["gate_bias_col"] = gb.reshape(16, 1)
    wuq = p["w_uq"][i]
    qi = -np.ones((512,), np.int64)
    qsi = -np.ones((512,), np.int64)
    for h in range(N_HEADS):
        qi[128 * h:128 * h + 96] = 96 * h + np.arange(96)
        qsi[128 * h + 64:128 * h + 96] = 96 * h + 64 + (np.arange(32) ^ 8)
    lw["wq"] = _gather_cols(wuq, qi).astype(BF16)
    lw["wqs"] = _gather_cols(wuq, qsi).astype(BF16)
    ki = -np.ones((512,), np.int64)
    for h in range(N_HEADS):
        ki[128 * h:128 * h + 64] = 128 * h + np.arange(64)
    lw["wk"] = _gather_cols(p["w_ukv"][i], ki).astype(BF16)
    lw["wvt"] = _vt_rows(p["w_ukv"][i].reshape(MLA_KV_LORA, N_HEADS, 128)[:, :, 64:].reshape(MLA_KV_LORA, 256)).astype(BF16)
    lw["wdvt"] = _vt_rows(p["w_in"][i][:, 2768:3024]).astype(BF16)
    wa = p["w_alpha2"][i]
    wal = jnp.zeros((LANE, 256), F32).at[0:16, 0:128].set(wa[0]).at[16:32, 128:256].set(wa[1])
    lw["walpha"] = wal.astype(BF16)
    lw["balpha"] = p["b_alpha"][i].reshape(1, 256)
    lw["w_conv"] = p["w_conv"][i]
    lw["b_conv"] = row(p["b_conv"][i])
    lw["dlam"] = p["diff_lambda"][i]
    lw["g_diff"] = jnp.broadcast_to(p["g_diff_out"][i].reshape(N_HEADS, DIFF_DV, 1), (N_HEADS, DIFF_DV, LANE))
    wb = p["w_branch"][i]
    lw["wbr"] = jnp.concatenate([_pad_heads_rows(wb[0], 64), wb[1], wb[2], _pad_heads_rows(wb[3], 64)], axis=0).astype(BF16)
    lw["w_gate"] = p["w_gate"][i].astype(BF16)
    lw["b_gate"] = row(p["b_gate"][i])
    lw["w_out"] = p["w_out"][i].astype(BF16)
    lw["w_router"] = jnp.pad(p["w_router"][i], ((0, 0), (0, LANE - N_EXPERTS))).astype(BF16)
    lw["w_router_t"] = p["w_router"][i].T.astype(BF16)
    lw["layer"] = i
    lw["w_e_gate"], lw["w_e_up"], lw["w_e_down"] = p["w_e_gate"], p["w_e_up"], p["w_e_down"]
    return lw


def _ffn(xm, h2, aff, aff_t, mod, lw):
    b, t, _ = xm.shape
    nblk = t // TOK_BLK
    cap = EC_CAPACITY * t // N_EXPERTS
    capp = -(-cap // TOK_BLK) * TOK_BLK
    pos4, s04 = _topk(aff_t, cap)
    s0_be = jnp.concatenate([s04[..., 0].transpose(0, 2, 1), jnp.full((b, N_EXPERTS, 1), cap, I32)], axis=-1)
    s0_flat = s0_be.reshape(-1)
    ys = _moe(s0_flat, pos4, h2, lw, capp)
    pos_t = pos4.transpose(0, 1, 3, 2).reshape(b, t, N_EXPERTS)
    first = jnp.minimum(s0_be[..., :-1] // SLOT_BLK, capp // SLOT_BLK - 1)
    fits = jnp.all(s0_be[..., 1:] <= (first + 2) * SLOT_BLK)
    args = (s0_flat, ys, pos_t, aff, xm, mod, lw["g_ffn_post"])
    return lax.cond(fits, functools.partial(_combine, sblk=SLOT_BLK), functools.partial(_combine, sblk=TOK_BLK), *args)


def _hybrid_layer(i, x_c, x_l, c8, need_ctx, p):
    lw = _layer_weights(i, p)
    b, t, _ = x_l.shape
    tc = x_c.shape[1]
    lam_init = 0.8 - 0.6 * math.exp(-0.3 * i)
    mod8 = _ada(c8, p["w_ada"][i], p["b_ada"][i])
    pad = lambda m: jnp.pad(m.reshape(b, 6, D), ((0, 0), (0, 2), (0, 0)))
    mod_l = pad(mod8[:b])
    mod_c = pad(jnp.broadcast_to(mod8[b:b + 1], (b, 6 * D)))

    pc = _proj(x_c, mod_c, _identity_table(tc), lw)
    pt = _proj(x_l, mod_l, _rope_table(t), lw)

    zc = jnp.zeros((b, 2, 256, 512), F32)
    zm = jnp.full((b, 2, 8, 256), NEG, F32)
    zs = jnp.zeros((b, 2, 256, 128), F32)
    hf_c, hb_c, c_fin, m_fin = _mlstm(pc, lw["w_conv"], lw["b_conv"], zc, zm)
    hf_l, hb_l, _, _ = _mlstm(pt, lw["w_conv"], lw["b_conv"], c_fin, m_fin)
    gf_c, gb_c, s_fin = _gla(pc, zs)
    gf_l, gb_l, _ = _gla(pt, s_fin)

    one_g = jnp.ones((N_HEADS, MLA_V, LANE), F32)
    fl = functools.partial(_flash, tk=FLASH_KEYS)
    ya_l = fl(pt["mqt"], pc["mk"], pc["mvt"], pt["mk"], pt["mvt"], lw["dlam"], one_g, nmap=1, finish=False, post=1.0,
              tq=min(FLASH_ROWS, t))
    yd_l = fl(pt["dqt"], pc["dk"], pc["dvt"], pt["dk"], pt["dvt"], lw["dlam"], lw["g_diff"], nmap=2, finish=True,
              post=1.0 - lam_init, tq=min(FLASH_ROWS // 2, t))
    xm, h2, aff, aff_t = _merge(x_l, mod_l, ya_l, (hf_l, hb_l), pt["lo"], (gf_l, gb_l), pt["gr"], yd_l, lw, tm=256)
    x_l = _ffn(xm, h2, aff, aff_t, mod_l, lw)

    if need_ctx:
        ya_c = fl(pc["mqt"], pc["mk"], pc["mvt"], None, None, lw["dlam"], one_g, nmap=1, finish=False, post=1.0, tq=tc)
        yd_c = fl(pc["dqt"], pc["dk"], pc["dvt"], None, None, lw["dlam"], lw["g_diff"], nmap=2, finish=True,
                  post=1.0 - lam_init, tq=tc)
        xm, h2, aff, aff_t = _merge(x_c, mod_c, ya_c, (hf_c, hb_c), pc["lo"], (gf_c, gb_c), pc["gr"], yd_c, lw, tm=tc)
        x_c = _ffn(xm, h2, aff, aff_t, mod_c, lw)
    return x_c, x_l


def kernel(x, c, ctx, c_ctx, w_ada, b_ada, g_mix_pre, g_mix_post, g_ffn_pre, g_ffn_post, w_in, g_q_lat, w_uq, g_kv_lat, w_ukv, w_conv, b_conv, b_igate, b_fgate, g_mlstm_out, w_alpha2, b_alpha, g_gla_out, diff_lambda, g_diff_out, w_branch, w_gate, b_gate, w_out, w_router, w_e_gate, w_e_up, w_e_down):
    p = dict(w_ada=w_ada, b_ada=b_ada, g_mix_pre=g_mix_pre, g_mix_post=g_mix_post, g_ffn_pre=g_ffn_pre,
             g_ffn_post=g_ffn_post, w_in=w_in, g_q_lat=g_q_lat, w_uq=w_uq, g_kv_lat=g_kv_lat, w_ukv=w_ukv,
             w_conv=w_conv, b_conv=b_conv, b_igate=b_igate, b_fgate=b_fgate, g_mlstm_out=g_mlstm_out,
             w_alpha2=w_alpha2, b_alpha=b_alpha, g_gla_out=g_gla_out, diff_lambda=diff_lambda, g_diff_out=g_diff_out,
             w_branch=w_branch, w_gate=w_gate, b_gate=b_gate, w_out=w_out, w_router=w_router,
             w_e_gate=w_e_gate.astype(BF16), w_e_up=w_e_up.astype(BF16), w_e_down=w_e_down.astype(BF16))
    b = x.shape[0]
    c8 = jnp.concatenate([c, c_ctx[None], jnp.zeros((8 - b - 1, D), F32)], axis=0)
    x_c, x_l = ctx, x
    for i in range(DEPTH):
        x_c, x_l = _hybrid_layer(i, x_c, x_l, c8, i < DEPTH - 1, p)
    return x_l
```

```python
import functools
import math

import numpy as np
import jax
import jax.numpy as jnp
from jax import lax
from jax.experimental import pallas as pl
from jax.experimental.pallas import tpu as pltpu

F32 = jnp.float32
BF16 = jnp.bfloat16
I32 = jnp.int32

D = 1024
DEPTH = 2
GRID_W = 64
N_HEADS = 4
MLA_NOPE, MLA_ROPE, MLA_V = 64, 32, 64
MLA_Q_LORA, MLA_KV_LORA = 256, 128
ML_DH = 64
GLA_DK, GLA_DV, GLA_RANK, GLA_TAU = 32, 64, 16, 16.0
DIFF_DQK, DIFF_DV = 32, 64
ROPE_DIM, ROPE_BASE = 32, 10000.0
N_EXPERTS, EC_CAPACITY, EXPERT_FF = 16, 2, 1408
NEG = -1e30
EPS = 1e-6
LOG2E = 1.4426950408889634

LANE = 128
HEAD_SLAB = 128
TOK_BLK = 256
ML_CHUNK = 256
ML_BLOCK = 256
GLA_CHUNK = 64
GLA_BLOCK = 256
GATHER_WIN = TOK_BLK + 16
GATHER_WIN_SMALL = 64
MOE_TOK = 4096
SLOT_BLK = 128
VMEM_LIMIT = 56 * 1024 * 1024

ZQ, ZKV, ZKRA, ZKRB, ZMLQK, ZMLV, ZMLO, ZGATE, ZGA = 0, 256, 384, 512, 640, 1152, 1408, 1664, 1792
ZGQ, ZGK, ZGV, ZGR, ZDQ, ZDQS, ZDK, ZDKS, NZ = 1920, 2048, 2176, 2432, 2688, 2944, 3200, 3456, 3712
KV_CHUNK = 256
VT_ROWS = 80
FLASH_ROWS = 512
FLASH_KEYS = 512
FLASH_UNROLL = 16


def _swap32(c):
    return (c // 32) * 32 + ((c % 32) ^ 8)


def _win_index():
    idx = -np.ones((NZ,), np.int64)
    idx[ZQ:ZQ + 256] = np.arange(0, 256)
    idx[ZKV:ZKV + 128] = np.arange(256, 384)
    r = np.arange(32)
    idx[ZKRA + 64:ZKRA + 96] = 384 + r
    idx[ZKRB + 64:ZKRB + 96] = 384 + (r ^ 8)
    ml = 416
    idx[ZMLQK:ZMLQK + 512] = ml + np.arange(512)
    idx[ZMLV:ZMLV + 256] = ml + 512 + np.arange(256)
    idx[ZMLO:ZMLO + 256] = ml + 768 + np.arange(256)
    idx[ZGATE:ZGATE + 16] = ml + 1024 + np.arange(16)
    gl = 1456
    idx[ZGQ:ZGQ + 128] = gl + np.arange(128)
    idx[ZGK:ZGK + 128] = gl + 128 + np.arange(128)
    idx[ZGV:ZGV + 256] = gl + 256 + np.arange(256)
    idx[ZGR:ZGR + 256] = gl + 512 + np.arange(256)
    idx[ZGA:ZGA + 32] = gl + 768 + np.arange(32)
    df = 2256
    c = np.arange(256)
    idx[ZDQ:ZDQ + 256] = df + c
    idx[ZDQS:ZDQS + 256] = df + _swap32(c)
    idx[ZDK:ZDK + 256] = df + 256 + c
    idx[ZDKS:ZDKS + 256] = df + 256 + _swap32(c)
    return idx


def _vt_rows(w_cols):
    n = w_cols.shape[0]
    w4 = w_cols.T.reshape(N_HEADS, 64, n)
    return jnp.pad(w4, ((0, 0), (0, VT_ROWS - 64), (0, 0))).reshape(N_HEADS * VT_ROWS, n)


_WIN_IDX = _win_index()


def _gather_cols(w, idx):
    safe = np.maximum(idx, 0)
    return jnp.where(jnp.asarray(idx >= 0)[None, :], w[:, safe], 0.0)


def _pad_heads_rows(w, width):
    n = w.shape[1]
    w4 = w.reshape(N_HEADS, width, n)
    return jnp.pad(w4, ((0, 0), (0, HEAD_SLAB - width), (0, 0))).reshape(N_HEADS * HEAD_SLAB, n)


def _cparams(sem):
    return pltpu.CompilerParams(dimension_semantics=sem, vmem_limit_bytes=VMEM_LIMIT)


def _rms(x):
    return x * lax.rsqrt(jnp.mean(x * x, axis=-1, keepdims=True) + EPS)


def _sigmoid(x):
    return 0.5 * jnp.tanh(0.5 * x) + 0.5


def _log_sigmoid(x):
    return jnp.minimum(x, 0.0) - jnp.log1p(jnp.exp(-jnp.abs(x)))


def _dot(a, b, precision=None):
    return jnp.dot(a, b, preferred_element_type=F32, precision=precision)


def _dot_nt(a, b, precision=None):
    return lax.dot_general(a, b, (((1,), (1,)), ((), ())), preferred_element_type=F32, precision=precision)


def _dot_tn(a, b, precision=None):
    return lax.dot_general(a, b, (((0,), (0,)), ((), ())), preferred_element_type=F32, precision=precision)


def _split_bf16(x, parts):
    out, r = [], x
    for _ in range(parts):
        t = r.astype(BF16)
        out.append(t)
        r = r - t.astype(F32)
    return out


def _dot_sel(sel, x, parts=3):
    return sum(_dot(sel, t) for t in _split_bf16(x, parts))


def _const_spec(shape):
    nd = len(shape)
    return pl.BlockSpec(shape, lambda *_: (0,) * nd)


def _ada_kernel(c_ref, w_ref, b_ref, o_ref):
    cv = c_ref[...]
    s = (cv * _sigmoid(cv)).astype(BF16)
    o_ref[...] = _dot(s, w_ref[...].astype(BF16)) + b_ref[...]


def _ada(c8, w_ada, b_ada):
    n, tn = 6 * D, 1024
    return pl.pallas_call(
        _ada_kernel, name="ada", grid=(n // tn,),
        in_specs=[pl.BlockSpec((8, D), lambda j: (0, 0)), pl.BlockSpec((D, tn), lambda j: (0, j)),
                  pl.BlockSpec((1, tn), lambda j: (0, j))],
        out_specs=pl.BlockSpec((8, tn), lambda j: (0, j)),
        out_shape=jax.ShapeDtypeStruct((8, n), F32), compiler_params=_cparams(("arbitrary",)),
    )(c8, w_ada, b_ada.reshape(1, n))


_PROJ_OUT = (
    ("mk", 512, BF16),
    ("lqk", 512, F32), ("lv", 256, BF16), ("lo", 256, BF16), ("gc", 128, F32),
    ("gq", 128, BF16), ("gk", 128, BF16), ("gv", 256, BF16), ("gr", 256, BF16), ("glg", 256, F32),
    ("dk", 512, BF16),
)


def _proj_kernel(x_ref, mod_ref, g_ref, w_ref, tab_ref, gq_ref, wq_ref, wqs_ref, gkv_ref, wk_ref, wvt_ref,
                 wgt_ref, gbr_ref, gbc_ref, wal_ref, bal_ref, wdvt_ref,
                 mk_ref, lqk_ref, lv_ref, lo_ref, gc_ref, gq_o, gk_o, gv_o, gr_o, glg_o,
                 dk_ref, grow_ref, mvt_ref, dvt_ref, mqt_ref, dqt_ref):
    x = x_ref[0]
    tm = x.shape[0]
    mod = mod_ref[0]
    h = _rms(x) * g_ref[...] * (1.0 + mod[1:2]) + mod[0:1]
    hb = h.astype(BF16)
    z = _dot(hb, w_ref[...])
    tab = tab_ref[...]
    ct, st, cd, sd = tab[:, 0:128], tab[:, 128:256], tab[:, 256:384], tab[:, 384:512]
    lane = lax.broadcasted_iota(I32, (tm, LANE), 1)

    qn = (_rms(z[:, ZQ:ZQ + 256]) * gq_ref[...]).astype(BF16)
    qa = _dot(qn, wq_ref[...])
    qb = _dot(qn, wqs_ref[...])
    qscale = (MLA_NOPE + MLA_ROPE) ** -0.5 * LOG2E
    for hh in range(N_HEADS):
        sl = slice(HEAD_SLAB * hh, HEAD_SLAB * (hh + 1))
        mqt_ref[0, hh] = ((qa[:, sl] * ct + qb[:, sl] * st) * qscale).T.astype(BF16)
    kvn = (_rms(z[:, ZKV:ZKV + 128]) * gkv_ref[...]).astype(BF16)
    kk = _dot(kvn, wk_ref[...])
    kr = z[:, ZKRA:ZKRA + 128] * ct + z[:, ZKRB:ZKRB + 128] * st
    for hh in range(N_HEADS):
        sl = slice(HEAD_SLAB * hh, HEAD_SLAB * (hh + 1))
        mk_ref[0, :, sl] = (kk[:, sl] + kr).astype(BF16)
    ones_row = lax.broadcasted_iota(I32, (N_HEADS * VT_ROWS, tm), 0) % VT_ROWS == MLA_V
    mvt_ref[0, 0] = jnp.where(ones_row, 1.0, _dot_nt(wvt_ref[...], kvn)).astype(BF16)
    dvt_ref[0, 0] = jnp.where(ones_row, 1.0, _dot_nt(wdvt_ref[...], hb)).astype(BF16)

    lqk_ref[0] = z[:, ZMLQK:ZMLQK + 512]
    lv_ref[0] = z[:, ZMLV:ZMLV + 256].astype(BF16)
    lo_ref[0] = z[:, ZMLO:ZMLO + 256].astype(BF16)
    gcol = z[:, ZGATE:ZGATE + 128] + gbr_ref[...]
    gc_ref[0] = jnp.where(lane < 8, gcol, jnp.where(lane < 16, _log_sigmoid(gcol), 0.0))
    zr = _dot_nt(wgt_ref[...], hb) + gbc_ref[...]
    rowi = lax.broadcasted_iota(I32, zr.shape, 0)
    grow_ref[0] = jnp.where(rowi < 8, zr, _log_sigmoid(zr))

    gq_o[0] = (z[:, ZGQ:ZGQ + 128] * GLA_DK ** -0.5).astype(BF16)
    gk_o[0] = z[:, ZGK:ZGK + 128].astype(BF16)
    gv_o[0] = z[:, ZGV:ZGV + 256].astype(BF16)
    gr_o[0] = z[:, ZGR:ZGR + 256].astype(BF16)
    zg = _dot(z[:, ZGA:ZGA + 128].astype(BF16), wal_ref[...]) + bal_ref[...]
    glg_o[0] = _log_sigmoid(zg) * (1.0 / GLA_TAU)

    dscale = DIFF_DQK ** -0.5 * LOG2E
    for g in range(2):
        gs = slice(128 * g, 128 * (g + 1))
        qg = (z[:, ZDQ:ZDQ + 256][:, gs] * cd + z[:, ZDQS:ZDQS + 256][:, gs] * sd) * dscale
        kg = z[:, ZDK:ZDK + 256][:, gs] * cd + z[:, ZDKS:ZDKS + 256][:, gs] * sd
        for hl in range(2):
            hh = 2 * g + hl
            for m in range(2):
                lo = 64 * hl + 32 * m
                dqt_ref[0, 2 * hh + m] = jnp.where((lane >= lo) & (lane < lo + 32), qg, 0.0).T.astype(BF16)
            dk_ref[0, :, HEAD_SLAB * hh:HEAD_SLAB * (hh + 1)] = jnp.where(
                (lane >= 64 * hl) & (lane < 64 * hl + 64), kg, 0.0).astype(BF16)


def _proj(x, mod, tab, lw):
    b, t, _ = x.shape
    tm = KV_CHUNK
    consts = [lw["g_mix_pre"], lw["w_ext"], None, lw["g_q_lat"], lw["wq"], lw["wqs"], lw["g_kv_lat"], lw["wk"], lw["wvt"],
              lw["wgt"], lw["gate_bias_row"], lw["gate_bias_col"], lw["walpha"], lw["balpha"], lw["wdvt"]]
    in_specs = [pl.BlockSpec((1, tm, D), lambda bi, i: (bi, i, 0)), pl.BlockSpec((1, 8, D), lambda bi, i: (bi, 0, 0))]
    args = [x, mod]
    for cst in consts:
        if cst is None:
            in_specs.append(pl.BlockSpec((tm, 512), lambda bi, i: (i, 0)))
            args.append(tab)
        else:
            in_specs.append(_const_spec(cst.shape))
            args.append(cst)
    out_specs = [pl.BlockSpec((1, tm, w), lambda bi, i: (bi, i, 0)) for _, w, _ in _PROJ_OUT]
    out_shape = [jax.ShapeDtypeStruct((b, t, w), dt) for _, w, dt in _PROJ_OUT]
    out_specs.append(pl.BlockSpec((1, 16, tm), lambda bi, i: (bi, 0, i)))
    out_shape.append(jax.ShapeDtypeStruct((b, 16, t), F32))
    for _ in range(2):
        out_specs.append(pl.BlockSpec((1, 1, N_HEADS * VT_ROWS, tm), lambda bi, i: (bi, i, 0, 0)))
        out_shape.append(jax.ShapeDtypeStruct((b, t // tm, N_HEADS * VT_ROWS, tm), BF16))
    for nslab in (N_HEADS, 2 * N_HEADS):
        out_specs.append(pl.BlockSpec((1, nslab, HEAD_SLAB, tm), lambda bi, i: (bi, 0, 0, i)))
        out_shape.append(jax.ShapeDtypeStruct((b, nslab, HEAD_SLAB, t), BF16))
    outs = pl.pallas_call(
        _proj_kernel, name="proj", grid=(b, t // tm), in_specs=in_specs, out_specs=out_specs, out_shape=out_shape,
        compiler_params=_cparams(("parallel", "arbitrary")),
    )(*args)
    res = {name: o for (name, _, _), o in zip(_PROJ_OUT, outs[:-5])}
    res["grow"], res["mvt"], res["dvt"], res["mqt"], res["dqt"] = outs[-5:]
    return res


def _flash_kernel(*refs, nmap, has_lat, tk, finish, post):
    if has_lat:
        q_ref, kc_ref, vc_ref, kl_ref, vl_ref, dl_ref, g_ref, o_ref, s_ref, acc_ref = refs
    else:
        q_ref, kc_ref, vc_ref, dl_ref, g_ref, o_ref = refs
    tq = q_ref.shape[3]
    qt = q_ref[0, 0] if nmap == 1 else jnp.concatenate([q_ref[0, mm] for mm in range(nmap)], axis=1)
    rows = nmap * tq
    sub = tk // KV_CHUNK

    def softmax(s, smax, m):
        m_new = jnp.maximum(m, smax)
        return m_new, jnp.exp2(m - m_new), jnp.exp2(s - m_new).astype(BF16)

    def pv(p, vts):
        return _dot(vts[0] if len(vts) == 1 else jnp.concatenate(vts, axis=1), p)

    s_ctx = _dot(kc_ref[0], qt)
    if has_lat:
        n = kl_ref.shape[1] // tk
        unroll = min(FLASH_UNROLL, n)

        def scores(j):
            if isinstance(j, int):
                return _dot(kl_ref[0, j * tk:(j + 1) * tk, :], qt)
            off = pl.multiple_of(j * tk, tk)
            return _dot(kl_ref[0, pl.ds(off, tk), :], qt)

        def values(j):
            return [vl_ref[0, j * sub + c] for c in range(sub)]

        def produce(slot, j):
            s = scores(j)
            s_ref[slot] = s
            return jnp.max(s, axis=0, keepdims=True)

        smax0 = produce(0, 0)

    m, _, p = softmax(s_ctx, jnp.max(s_ctx, axis=0, keepdims=True), jnp.full((1, rows), NEG, F32))
    acc = pv(p, [vc_ref[0, 0]])
    if has_lat:
        acc_ref[...] = acc

        def body(jj, carry):
            m, smax = carry
            j = unroll * jj
            for u in range(unroll):
                smax_next = smax
                if not isinstance(j, int):
                    smax_next = produce((u + 1) % 2, jnp.minimum(j + u + 1, n - 1))
                elif j + u + 1 < n:
                    smax_next = produce((u + 1) % 2, j + u + 1)
                m, alpha, p = softmax(s_ref[u % 2], smax, m)
                acc_ref[...] = alpha * acc_ref[...] + pv(p, values(j + u))
                smax = smax_next
            return m, smax

        if unroll == n:
            body(0, (m, smax0))
        else:
            lax.fori_loop(0, n // unroll, body, (m, smax0))
        acc = acc_ref[...]

    o = acc[0:MLA_V, :] / acc[MLA_V:MLA_V + 1, :]
    if nmap == 2:
        lv = dl_ref[...]
        lam = (jnp.exp(jnp.sum(lv[0:1] * lv[1:2], axis=-1, keepdims=True))
               - jnp.exp(jnp.sum(lv[2:3] * lv[3:4], axis=-1, keepdims=True)) + (1.0 - post))
        o = o[:, :tq] - lam * o[:, tq:]
    if finish:
        ms = jnp.mean(o * o, axis=0, keepdims=True)
        o = o * lax.rsqrt(ms + EPS) * jnp.concatenate([g_ref[0]] * (tq // LANE), axis=1) * post
    o_pad = jnp.concatenate([o, jnp.zeros((HEAD_SLAB - MLA_V, tq), F32)], axis=0)
    o_ref[0] = o_pad.T.astype(BF16)


def _flash(q, kc, vct, kl, vlt, dlam, g_out, *, nmap, finish, post, tq, tk):
    b, _, _, t = q.shape
    has_lat = kl is not None
    assert kc.shape[1] == KV_CHUNK and tk % KV_CHUNK == 0
    kspec = lambda n: pl.BlockSpec((1, n, HEAD_SLAB), lambda bi, h, i: (bi, 0, h))
    vspec = lambda n: pl.BlockSpec((1, n // KV_CHUNK, VT_ROWS, KV_CHUNK), lambda bi, h, i: (bi, 0, h, 0))
    in_specs = [pl.BlockSpec((1, nmap, HEAD_SLAB, tq), lambda bi, h, i: (bi, h, 0, i)), kspec(KV_CHUNK), vspec(KV_CHUNK)]
    args = [q, kc, vct]
    scratch = []
    if has_lat:
        tl = kl.shape[1]
        assert (tl // tk) % min(FLASH_UNROLL, tl // tk) == 0
        in_specs += [kspec(tl), vspec(tl)]
        args += [kl, vlt]
        scratch = [pltpu.VMEM((2, tk, nmap * tq), F32), pltpu.VMEM((VT_ROWS, nmap * tq), F32)]
    in_specs += [_const_spec(dlam.shape), pl.BlockSpec((1, MLA_V, LANE), lambda bi, h, i: (h, 0, 0))]
    args += [dlam, g_out]
    return pl.pallas_call(
        functools.partial(_flash_kernel, nmap=nmap, has_lat=has_lat, tk=tk, finish=finish, post=post),
        name="flash_diff" if nmap == 2 else "flash_mla",
        grid=(b, N_HEADS, t // tq), in_specs=in_specs,
        out_specs=pl.BlockSpec((1, tq, HEAD_SLAB), lambda bi, h, i: (bi, i, h)),
        out_shape=jax.ShapeDtypeStruct((b, t, N_HEADS * HEAD_SLAB), BF16),
        scratch_shapes=scratch,
        compiler_params=_cparams(("parallel", "parallel", "arbitrary")),
    )(*args)


def _head_of(shape, axis, width):
    return (lax.broadcasted_iota(I32, shape, axis) % (N_HEADS * width)) // width


def _mlstm_conv(first, last, x, xprev, xnext, wc, bcv):
    n = x.shape[0]
    row = lax.broadcasted_iota(I32, x.shape, 0)
    pr = jnp.where(first, 0.0, xprev[7:8, :])
    nx = jnp.where(last, 0.0, xnext[0:1, :])
    xm = jnp.where(row == 0, pr, pltpu.roll(x, 1, 0))
    xp = jnp.where(row == n - 1, nx, pltpu.roll(x, n - 1, 0))
    y = xm * wc[0:1] + x * wc[1:2] + xp * wc[2:3] + bcv
    qk = y * _sigmoid(y)
    return qk[:, :256], qk[:, 256:] * ML_DH ** -0.5


def _mlstm_dir(d, q, k, v, gcol, grow, cb, m0e):
    L = q.shape[0]
    li = lax.broadcasted_iota(I32, (L, L), 0)
    si = lax.broadcasted_iota(I32, (L, L), 1)
    tin = (si <= li) if d == 0 else (si >= li)
    tinb = tin.astype(BF16)
    bcol = _dot_sel(tinb, gcol)
    brow = sum(_dot_nt(t, tinb) for t in _split_bf16(grow, 3))
    hm256 = _head_of((L, 256), 1, ML_DH)
    hm512 = _head_of((L, 512), 1, ML_DH)
    e_idx = L - 1 if d == 0 else 0

    d_blk, inter_blk = [], []
    for hh in range(N_HEADS):
        c = 4 * d + hh
        bc = bcol[:, 8 + c:9 + c]
        d_blk.append(jnp.where(tin, bc - brow[8 + c:9 + c, :] + grow[c:c + 1, :], NEG))
        inter_blk.append(bc + m0e[0:1, 64 * hh:64 * hh + 1])
    d_st = jnp.concatenate(d_blk, axis=0)
    inter_st = jnp.concatenate(inter_blk, axis=0)
    mt = jnp.maximum(inter_st, jnp.max(d_st, axis=-1, keepdims=True))
    q_st = jnp.concatenate([jnp.where(hm256 == hh, q, 0.0) for hh in range(N_HEADS)], axis=0).astype(BF16)
    s_st = (jnp.exp(d_st - mt) * _dot_nt(q_st, k.astype(BF16))).astype(BF16)
    vext = jnp.concatenate([v, jnp.ones((L, 256), BF16)], axis=1)
    r = _dot(s_st, vext)
    aint = jnp.exp(inter_st - mt)
    p = _dot(q.astype(BF16), cb.astype(BF16))
    tot = jnp.zeros((L, 512), F32)
    mte = jnp.zeros((L, 256), F32)
    for hh in range(N_HEADS):
        rs = slice(hh * L, (hh + 1) * L)
        tot = jnp.where(hm512 == hh, r[rs] + aint[rs] * p, tot)
        mte = jnp.where(hm256 == hh, mt[rs], mte)
    hout = tot[:, :256] / jnp.maximum(jnp.abs(tot[:, 256:]), jnp.exp(-mte))

    wexp = jnp.zeros((L, 256), F32)
    arow = jnp.zeros((1, 512), F32)
    grw = jnp.zeros((1, 512), F32)
    mnew = jnp.zeros((1, 256), F32)
    hr512 = _head_of((1, 512), 1, ML_DH)
    hr256 = _head_of((1, 256), 1, ML_DH)
    for hh in range(N_HEADS):
        c = 4 * d + hh
        bc = bcol[:, 8 + c:9 + c]
        be = bc[e_idx:e_idx + 1, :]
        wl = be - bc + gcol[:, c:c + 1]
        mloc = jnp.max(wl, axis=0, keepdims=True)
        m0h = m0e[0:1, 64 * hh:64 * hh + 1]
        mn = jnp.maximum(be + m0h, mloc)
        wexp = jnp.where(hm256 == hh, jnp.exp(wl - mloc), wexp)
        arow = jnp.where(hr512 == hh, jnp.exp(be + m0h - mn), arow)
        grw = jnp.where(hr512 == hh, jnp.exp(mloc - mn), grw)
        mnew = jnp.where(hr256 == hh, mn, mnew)
    cl = _dot_tn((k * wexp).astype(BF16), vext)
    bd = lax.broadcasted_iota(I32, (256, 512), 0) // ML_DH == _head_of((256, 512), 1, ML_DH)
    return hout, arow * cb + jnp.where(bd, grw * cl, 0.0), jnp.broadcast_to(mnew, (8, 256))


def _mlstm_kernel(xf, xfp, xfn, xb, xbp, xbn, vf, vb, gcf, gcb, grf, grb, wc_ref, bc_ref, c0_ref, m0_ref,
                  hf_ref, hb_ref, c_ref, m_ref):
    i = pl.program_id(1)
    n = pl.num_programs(1)

    @pl.when(i == 0)
    def _():
        c_ref[...] = c0_ref[...]
        m_ref[...] = m0_ref[...]

    wc = wc_ref[...]
    bcv = bc_ref[...]
    L = ML_CHUNK
    nsub = xf.shape[1] // L
    streams = ((0, i == 0, i == n - 1, xf, xfp, xfn, vf, gcf, grf, hf_ref),
               (1, i == n - 1, i == 0, xb, xbp, xbn, vb, gcb, grb, hb_ref))
    for d, first, last, x, xp, xn, v, gc, gr, h_ref in streams:
        q, k = _mlstm_conv(first, last, x[0], xp[0], xn[0], wc, bcv)
        cb, m0e = c_ref[0, d], m_ref[0, d]
        for c in (range(nsub) if d == 0 else reversed(range(nsub))):
            sl = slice(c * L, (c + 1) * L)
            h_ref[0, sl, :], cb, m0e = _mlstm_dir(d, q[sl], k[sl], v[0, sl, :], gc[0, sl, :], gr[0, :, sl], cb, m0e)
        c_ref[0, d], m_ref[0, d] = cb, m0e


def _mlstm(pr, w_conv, b_conv, c0, m0):
    x, v, gc, gr = pr["lqk"], pr["lv"], pr["gc"], pr["grow"]
    b, t, _ = x.shape
    L = min(ML_BLOCK, t)
    n = t // L
    r8 = L // 8
    last8 = t // 8 - 1

    def fw(bi, i):
        return (bi, i, 0)

    def bw(bi, i):
        return (bi, n - 1 - i, 0)

    def halo(ix, shift):
        def f(bi, i):
            blk = ix(bi, i)[1]
            return (bi, jnp.clip(blk * r8 + shift, 0, last8), 0)
        return f

    main = lambda w, ix: pl.BlockSpec((1, L, w), ix)
    in_specs = [main(512, fw), pl.BlockSpec((1, 8, 512), halo(fw, -1)), pl.BlockSpec((1, 8, 512), halo(fw, r8)),
                main(512, bw), pl.BlockSpec((1, 8, 512), halo(bw, -1)), pl.BlockSpec((1, 8, 512), halo(bw, r8)),
                main(256, fw), main(256, bw), main(128, fw), main(128, bw),
                pl.BlockSpec((1, 16, L), lambda bi, i: (bi, 0, i)), pl.BlockSpec((1, 16, L), lambda bi, i: (bi, 0, n - 1 - i)),
                _const_spec(w_conv.shape), _const_spec(b_conv.shape),
                pl.BlockSpec((1, 2, 256, 512), lambda bi, i: (bi, 0, 0, 0)), pl.BlockSpec((1, 2, 8, 256), lambda bi, i: (bi, 0, 0, 0))]
    out_specs = [main(256, fw), main(256, bw),
                 pl.BlockSpec((1, 2, 256, 512), lambda bi, i: (bi, 0, 0, 0)), pl.BlockSpec((1, 2, 8, 256), lambda bi, i: (bi, 0, 0, 0))]
    out_shape = [jax.ShapeDtypeStruct((b, t, 256), F32), jax.ShapeDtypeStruct((b, t, 256), F32),
                 jax.ShapeDtypeStruct(c0.shape, F32), jax.ShapeDtypeStruct(m0.shape, F32)]
    return pl.pallas_call(
        _mlstm_kernel, name="mlstm", grid=(b, n), in_specs=in_specs, out_specs=out_specs, out_shape=out_shape,
        compiler_params=_cparams(("parallel", "arbitrary")),
    )(x, x, x, x, x, x, v, v, gc, gc, gr, gr, w_conv, b_conv, c0, m0)


def _gla_chunk(d, q, k, v, lg, sb):
    L = q.shape[0]
    li = lax.broadcasted_iota(I32, (L, L), 0)
    si = lax.broadcasted_iota(I32, (L, L), 1)
    tin = (si <= li) if d == 0 else (si >= li)
    lgd = lg[:, 128 * d:128 * (d + 1)]
    gcum = _dot_sel(tin.astype(BF16), lgd)
    e_idx = L - 1 if d == 0 else 0
    gend = gcum[e_idx:e_idx + 1, :]
    qf, kf = q.astype(F32), k.astype(F32)
    q_dec = qf * jnp.exp(gcum)
    k_dec = (kf * jnp.exp(-gcum)).astype(BF16)
    k_end = (kf * jnp.exp(gend - gcum)).astype(BF16)
    hm128 = _head_of((L, 128), 1, GLA_DK)
    hm256 = _head_of((L, 256), 1, GLA_DV)
    q_st = jnp.concatenate([jnp.where(hm128 == hh, q_dec, 0.0) for hh in range(N_HEADS)], axis=0).astype(BF16)
    att = _dot_nt(q_st, k_dec)
    tin4 = jnp.concatenate([tin] * N_HEADS, axis=0)
    o_st = _dot(jnp.where(tin4, att, 0.0).astype(BF16), v)
    o = _dot_nt(q_dec.astype(BF16), sb.astype(BF16))
    for hh in range(N_HEADS):
        o = o + jnp.where(hm256 == hh, o_st[hh * L:(hh + 1) * L], 0.0)
    bd = lax.broadcasted_iota(I32, (256, 128), 0) // GLA_DV == _head_of((256, 128), 1, GLA_DK)
    return o, jnp.exp(gend) * sb + jnp.where(bd, _dot_tn(v, k_end), 0.0)


def _gla_kernel(qf, kf, vf, lf, qb, kb, vb, lb, s0_ref, of_ref, ob_ref, s_ref):
    i = pl.program_id(1)

    @pl.when(i == 0)
    def _():
        s_ref[...] = s0_ref[...]

    L = GLA_CHUNK
    nsub = qf.shape[1] // L
    for d, (q, k, v, lg, o_ref) in enumerate(((qf, kf, vf, lf, of_ref), (qb, kb, vb, lb, ob_ref))):
        sb = s_ref[0, d]
        for c in (range(nsub) if d == 0 else reversed(range(nsub))):
            sl = slice(c * L, (c + 1) * L)
            o_ref[0, sl, :], sb = _gla_chunk(d, q[0, sl, :], k[0, sl, :], v[0, sl, :], lg[0, sl, :], sb)
        s_ref[0, d] = sb


def _gla(pr, s0):
    q, k, v, lg = pr["gq"], pr["gk"], pr["gv"], pr["glg"]
    b, t, _ = q.shape
    L = min(GLA_BLOCK, t)
    n = t // L
    fw = lambda bi, i: (bi, i, 0)
    bw = lambda bi, i: (bi, n - 1 - i, 0)
    blk = lambda w, ix: pl.BlockSpec((1, L, w), ix)
    st_spec = pl.BlockSpec((1, 2, 256, 128), lambda bi, i: (bi, 0, 0, 0))
    return pl.pallas_call(
        _gla_kernel, name="gla", grid=(b, n),
        in_specs=[blk(128, fw), blk(128, fw), blk(256, fw), blk(256, fw),
                  blk(128, bw), blk(128, bw), blk(256, bw), blk(256, bw), st_spec],
        out_specs=[blk(256, fw), blk(256, bw), st_spec],
        out_shape=[jax.ShapeDtypeStruct((b, t, 256), F32), jax.ShapeDtypeStruct((b, t, 256), F32),
                   jax.ShapeDtypeStruct(s0.shape, F32)],
        compiler_params=_cparams(("parallel", "arbitrary")),
    )(q, k, v, lg, q, k, v, lg, s0)


def _head_rms_expanded(x, width):
    n = x.shape[1]
    bd = (lax.broadcasted_iota(I32, (n, n), 0) // width == lax.broadcasted_iota(I32, (n, n), 1) // width).astype(BF16)
    return sum(_dot(t, bd) for t in _split_bf16(x * x, 2)) * (1.0 / width)


def _merge_kernel(x_ref, mod_ref, ya_ref, hf_ref, hb_ref, lo_ref, gf_ref, gb_ref, gr_ref, yd_ref,
                  gpre_ref, wg_ref, bg_ref, wbr_ref, wo_ref, gpost_ref, gffn_ref, wr_ref, wrt_ref, gml_ref, ggla_ref,
                  xm_ref, h2_ref, aff_ref, afft_ref):
    x = x_ref[0]
    tm = x.shape[0]
    mod = mod_ref[0]
    hb = (_rms(x) * gpre_ref[...] * (1.0 + mod[1:2]) + mod[0:1]).astype(BF16)

    hs = hf_ref[0] + hb_ref[0]
    y_ml = _sigmoid(lo_ref[0].astype(F32)) * (hs * lax.rsqrt(_head_rms_expanded(hs, ML_DH) + EPS) * gml_ref[...])
    gs = gf_ref[0] + gb_ref[0]
    rr = gr_ref[0].astype(F32)
    y_gla = rr * _sigmoid(rr) * (gs * lax.rsqrt(_head_rms_expanded(gs, GLA_DV) + EPS) * ggla_ref[...])

    branches = ((ya_ref[0], 0, 512), (y_ml.astype(BF16), 512, 256), (y_gla.astype(BF16), 768, 256), (yd_ref[0], 1024, 512))
    mix = jnp.zeros((tm, D), F32)
    for nb, (yb, r0, rw) in enumerate(branches):
        gate = _sigmoid(_dot(hb, wg_ref[:, nb * D:(nb + 1) * D]) + bg_ref[:, nb * D:(nb + 1) * D])
        mix = mix + gate * _dot(yb, wbr_ref[r0:r0 + rw, :])
    y = _dot(mix.astype(BF16), wo_ref[...])
    xm = x + mod[2:3] * (_rms(y) * gpost_ref[...])
    xm_ref[0] = xm

    h2 = (_rms(xm) * gffn_ref[...] * (1.0 + mod[4:5]) + mod[3:4]).astype(BF16)
    h2_ref[0] = h2
    lane = lax.broadcasted_iota(I32, (tm, LANE), 1)
    lg = jnp.where(lane < N_EXPERTS, _dot(h2, wr_ref[...]), NEG)
    e = jnp.exp(lg - jnp.max(lg, axis=-1, keepdims=True))
    aff_ref[0] = (e / jnp.sum(e, axis=-1, keepdims=True))[:, :N_EXPERTS]
    lt = _dot_nt(wrt_ref[...], h2)
    et = jnp.exp(lt - jnp.max(lt, axis=0, keepdims=True))
    afft_ref[0] = et / jnp.sum(et, axis=0, keepdims=True)


def _merge(x, mod, ya, ml, lo, gl, gr, yd, lw, tm):
    b, t, _ = x.shape
    tok = lambda w: pl.BlockSpec((1, tm, w), lambda bi, i: (bi, i, 0))
    consts = [lw["g_mix_pre"], lw["w_gate"], lw["b_gate"], lw["wbr"], lw["w_out"], lw["g_mix_post"], lw["g_ffn_pre"],
              lw["w_router"], lw["w_router_t"], lw["g_mlstm_out"], lw["g_gla_out"]]
    in_specs = [tok(D), pl.BlockSpec((1, 8, D), lambda bi, i: (bi, 0, 0)), tok(512), tok(256), tok(256), tok(256),
                tok(256), tok(256), tok(256), tok(512)] + [_const_spec(c.shape) for c in consts]
    out_specs = [tok(D), tok(D), tok(N_EXPERTS), pl.BlockSpec((1, N_EXPERTS, tm), lambda bi, i: (bi, 0, i))]
    out_shape = [jax.ShapeDtypeStruct((b, t, D), F32), jax.ShapeDtypeStruct((b, t, D), BF16),
                 jax.ShapeDtypeStruct((b, t, N_EXPERTS), F32), jax.ShapeDtypeStruct((b, N_EXPERTS, t), F32)]
    return pl.pallas_call(
        _merge_kernel, name="merge", grid=(b, t // tm), in_specs=in_specs, out_specs=out_specs, out_shape=out_shape,
        compiler_params=_cparams(("parallel", "arbitrary")),
    )(x, mod, ya, ml[0], ml[1], lo, gl[0], gl[1], gr, yd, *consts)


def _topk_kernel(a_ref, pos_ref, s0_ref, *, cap):
    nblk = a_ref.shape[1]
    bits = pltpu.bitcast(a_ref[0], I32)

    def bisect(i, thr):
        cand = thr | (1 << (30 - i))
        cnt = jnp.sum((bits >= cand).astype(I32), axis=(0, 2), keepdims=True)
        return jnp.where(cnt >= cap, cand, thr)

    thr3 = lax.fori_loop(0, 31, bisect, jnp.zeros((1, N_EXPERTS, 1), I32))
    need3 = cap - jnp.sum((bits > thr3).astype(I32), axis=(0, 2), keepdims=True)
    thr, need = thr3[0], need3[0].astype(F32)
    upper = (lax.broadcasted_iota(I32, (TOK_BLK, TOK_BLK), 0) <= lax.broadcasted_iota(I32, (TOK_BLK, TOK_BLK), 1)).astype(BF16)

    def blk(j, carry):
        c_eq, c_sel = carry
        bj = pltpu.bitcast(a_ref[0, j], I32)
        gt, eq = bj > thr, bj == thr
        cum_eq = _dot(eq.astype(BF16), upper) + c_eq
        sel = gt | (eq & (cum_eq <= need))
        cum_sel = _dot(sel.astype(BF16), upper) + c_sel
        pos_ref[0, j] = jnp.where(sel, cum_sel - 1.0, -1.0).astype(I32)
        s0_ref[0, j] = jnp.broadcast_to(c_sel, (N_EXPERTS, LANE)).astype(I32)
        return cum_eq[:, TOK_BLK - 1:TOK_BLK], cum_sel[:, TOK_BLK - 1:TOK_BLK]

    zero = jnp.zeros((N_EXPERTS, 1), F32)
    lax.fori_loop(0, nblk, blk, (zero, zero))


def _topk(aff_t, cap):
    b, _, t = aff_t.shape
    nblk = t // TOK_BLK
    a4 = aff_t.reshape(b, N_EXPERTS, nblk, TOK_BLK).transpose(0, 2, 1, 3)
    spec = lambda w: pl.BlockSpec((1, nblk, N_EXPERTS, w), lambda bi: (bi, 0, 0, 0))
    return pl.pallas_call(
        functools.partial(_topk_kernel, cap=cap), name="topk", grid=(b,),
        in_specs=[spec(TOK_BLK)], out_specs=[spec(TOK_BLK), spec(LANE)],
        out_shape=[jax.ShapeDtypeStruct((b, nblk, N_EXPERTS, TOK_BLK), I32),
                   jax.ShapeDtypeStruct((b, nblk, N_EXPERTS, LANE), I32)],
        compiler_params=_cparams(("parallel",)),
    )(a4)


def _moe_kernel(s0_ref, pos_ref, h_ref, wg_ref, wu_ref, wd_ref, ys_ref, xs_ref, *, nbatch, nblk, nsub, rows_step):
    q, tb = pl.program_id(0), pl.program_id(1)
    cur = q % 2

    @pl.when(q < N_EXPERTS * nbatch)
    def _():
        e, bi = q // nbatch, q % nbatch

        @pl.when(tb == 0)
        def _():
            xs_ref[cur] = jnp.zeros(xs_ref.shape[1:], BF16)

        base = (bi * N_EXPERTS + e) * (nblk + 1) + tb * nsub
        s0s = [s0_ref[base + sb] for sb in range(nsub + 1)]
        a0s = [pl.multiple_of((s0 // 16) * 16, 16) for s0 in s0s[:-1]]
        spans = [s0s[sb + 1] - a0s[sb] for sb in range(nsub)]

        def gather(win, sb):
            a0 = a0s[sb]
            prow = pos_ref[0, sb, pl.ds(e, 1), :]
            slot = lax.broadcasted_iota(I32, (win, TOK_BLK), 0) + a0
            rows = _dot((slot == prow).astype(BF16), h_ref[0, sb * TOK_BLK:(sb + 1) * TOK_BLK, :])
            xs_ref[cur, pl.ds(a0, win), :] = xs_ref[cur, pl.ds(a0, win), :] + rows.astype(BF16)

        all_small = functools.reduce(jnp.logical_and, [sp <= GATHER_WIN_SMALL for sp in spans])

        @pl.when(all_small)
        def _():
            for sb in range(nsub):
                gather(GATHER_WIN_SMALL, sb)

        @pl.when(jnp.logical_not(all_small))
        def _():
            for sb in range(nsub):
                nonempty = s0s[sb + 1] > s0s[sb]
                pl.when(nonempty & (spans[sb] <= GATHER_WIN_SMALL))(functools.partial(gather, GATHER_WIN_SMALL, sb))
                pl.when(spans[sb] > GATHER_WIN_SMALL)(functools.partial(gather, GATHER_WIN, sb))

    @pl.when(q > 0)
    def _():
        rows = min(TOK_BLK, rows_step)
        for c in range(rows_step // rows):
            off = pl.multiple_of(tb * rows_step + c * rows, rows)
            xc = xs_ref[1 - cur, pl.ds(off, rows), :]
            hg = _dot(xc, wg_ref[0, 0])
            hid = (hg * _sigmoid(hg) * _dot(xc, wu_ref[0, 0])).astype(BF16)
            ys_ref[0, 0, pl.ds(off, rows), :] = _dot(hid, wd_ref[0, 0]).astype(BF16)


def _moe(s0_flat, pos4, h2, lw, capp):
    b, t, _ = h2.shape
    layer = lw["layer"]
    nblk = t // TOK_BLK
    tok = min(MOE_TOK, t)
    nsub = tok // TOK_BLK
    nstep = t // tok
    rows_step = capp // nstep
    assert rows_step % min(TOK_BLK, rows_step) == 0 and rows_step % 16 == 0
    npair = N_EXPERTS * b
    gat = lambda q: jnp.minimum(q, npair - 1)
    ffn = lambda q: jnp.maximum(q - 1, 0)
    grid_spec = pltpu.PrefetchScalarGridSpec(
        num_scalar_prefetch=1, grid=(npair + 1, nstep),
        in_specs=[pl.BlockSpec((1, nsub, N_EXPERTS, TOK_BLK), lambda q, tb, s: (gat(q) % b, tb, 0, 0)),
                  pl.BlockSpec((1, tok, D), lambda q, tb, s: (gat(q) % b, tb, 0)),
                  pl.BlockSpec((1, 1, D, EXPERT_FF), lambda q, tb, s: (layer, ffn(q) // b, 0, 0)),
                  pl.BlockSpec((1, 1, D, EXPERT_FF), lambda q, tb, s: (layer, ffn(q) // b, 0, 0)),
                  pl.BlockSpec((1, 1, EXPERT_FF, D), lambda q, tb, s: (layer, ffn(q) // b, 0, 0))],
        out_specs=pl.BlockSpec((1, 1, capp, D), lambda q, tb, s: (ffn(q) % b, ffn(q) // b, 0, 0)),
        scratch_shapes=[pltpu.VMEM((2, capp + GATHER_WIN, D), BF16)])
    return pl.pallas_call(
        functools.partial(_moe_kernel, nbatch=b, nblk=nblk, nsub=nsub, rows_step=rows_step), name="moe",
        grid_spec=grid_spec, out_shape=jax.ShapeDtypeStruct((b, N_EXPERTS, capp, D), BF16),
        compiler_params=_cparams(("arbitrary", "arbitrary")),
    )(s0_flat, pos4, h2, lw["w_e_gate"], lw["w_e_up"], lw["w_e_down"])


def _combine_kernel(s0_ref, *refs, nblk, nb, sblk):
    ys_refs = refs[:2 * N_EXPERTS]
    pos_ref, aff_ref, xm_ref, mod_ref, g_ref, o_ref = refs[2 * N_EXPERTS:]
    bi, tb = pl.program_id(0), pl.program_id(1)
    pos = pos_ref[0]
    aff = aff_ref[0]
    lane = lax.broadcasted_iota(I32, (TOK_BLK, 2 * sblk), 1)
    acc = jnp.zeros((TOK_BLK, D), F32)
    for e in range(N_EXPERTS):
        s0 = s0_ref[(bi * N_EXPERTS + e) * (nblk + 1) + tb]
        blk0 = jnp.minimum(s0 // sblk, nb - 1)
        rel = pos[:, e:e + 1] - blk0 * sblk
        ysw = jnp.concatenate([ys_refs[2 * e][0, 0], ys_refs[2 * e + 1][0, 0]], axis=0)
        acc = acc + aff[:, e:e + 1] * _dot((lane == rel).astype(BF16), ysw)
    mod = mod_ref[0]
    o_ref[0] = xm_ref[0] + mod[5:6] * (_rms(acc) * g_ref[...])


def _combine(s0_flat, ys, pos_t, aff, xm, mod, g_post, sblk):
    b, t, _ = xm.shape
    nblk = t // TOK_BLK
    nb = ys.shape[2] // sblk

    def ys_spec(e, k):
        def ix(bi, tb, s):
            blk0 = jnp.minimum(s[(bi * N_EXPERTS + e) * (nblk + 1) + tb] // sblk, nb - 1)
            return (bi, e, jnp.minimum(blk0 + k, nb - 1), 0)
        return pl.BlockSpec((1, 1, sblk, D), ix)

    tok = lambda w: pl.BlockSpec((1, TOK_BLK, w), lambda bi, tb, s: (bi, tb, 0))
    in_specs = [ys_spec(e, k) for e in range(N_EXPERTS) for k in range(2)]
    in_specs += [tok(N_EXPERTS), tok(N_EXPERTS), tok(D), pl.BlockSpec((1, 8, D), lambda bi, tb, s: (bi, 0, 0)),
                 pl.BlockSpec((1, D), lambda bi, tb, s: (0, 0))]
    grid_spec = pltpu.PrefetchScalarGridSpec(num_scalar_prefetch=1, grid=(b, nblk), in_specs=in_specs, out_specs=tok(D))
    return pl.pallas_call(
        functools.partial(_combine_kernel, nblk=nblk, nb=nb, sblk=sblk), name="combine", grid_spec=grid_spec,
        out_shape=jax.ShapeDtypeStruct((b, t, D), F32),
        compiler_params=_cparams(("arbitrary", "arbitrary")),
    )(s0_flat, *([ys] * (2 * N_EXPERTS)), pos_t, aff, xm, mod, g_post)


def _rope_table(t):
    nf = ROPE_DIM // 4
    pos = jnp.arange(t)
    inv = ROPE_BASE ** (-jnp.arange(nf, dtype=F32) / nf)
    ang = jnp.stack([pos // GRID_W, pos % GRID_W], axis=-1).astype(F32)[..., None] * inv
    cos, sin = jnp.cos(ang), jnp.sin(ang)
    c32 = jnp.stack([cos, cos], axis=2).reshape(t, ROPE_DIM)
    s32 = jnp.stack([-sin, sin], axis=2).reshape(t, ROPE_DIM)
    one, zero = jnp.ones((t, 64), F32), jnp.zeros((t, 32), F32)
    ct = jnp.concatenate([one, c32, zero], axis=1)
    st = jnp.concatenate([0.0 * one, s32, zero], axis=1)
    return jnp.concatenate([ct, st, jnp.tile(c32, (1, 4)), jnp.tile(s32, (1, 4))], axis=1)


def _identity_table(t):
    one, zero = jnp.ones((t, 128), F32), jnp.zeros((t, 128), F32)
    ct = jnp.concatenate([jnp.ones((t, 96), F32), jnp.zeros((t, 32), F32)], axis=1)
    return jnp.concatenate([ct, zero, one, zero], axis=1)


def _layer_weights(i, p):
    lw = {}
    row = lambda a: a.reshape(1, -1)
    for name in ("g_mix_pre", "g_mix_post", "g_ffn_pre", "g_ffn_post", "g_q_lat", "g_kv_lat", "g_mlstm_out", "g_gla_out"):
        lw[name] = row(p[name][i])
    lw["w_ext"] = _gather_cols(p["w_in"][i], _WIN_IDX).astype(BF16)
    lw["wgt"] = p["w_in"][i][:, 416 + 1024:416 + 1040].T.astype(BF16)
    gb = jnp.concatenate([p["b_igate"][i].reshape(-1), p["b_fgate"][i].reshape(-1)])
    lw["gate_bias_row"] = jnp.pad(gb, (0, LANE - 16)).reshape(1, LANE)
    lw["gate_bias_col"] = gb.reshape(16, 1)
    wuq = p["w_uq"][i]
    qi = -np.ones((512,), np.int64)
    qsi = -np.ones((512,), np.int64)
    for h in range(N_HEADS):
        qi[128 * h:128 * h + 96] = 96 * h + np.arange(96)
        qsi[128 * h + 64:128 * h + 96] = 96 * h + 64 + (np.arange(32) ^ 8)
    lw["wq"] = _gather_cols(wuq, qi).astype(BF16)
    lw["wqs"] = _gather_cols(wuq, qsi).astype(BF16)
    ki = -np.ones((512,), np.int64)
    for h in range(N_HEADS):
        ki[128 * h:128 * h + 64] = 128 * h + np.arange(64)
    lw["wk"] = _gather_cols(p["w_ukv"][i], ki).astype(BF16)
    lw["wvt"] = _vt_rows(p["w_ukv"][i].reshape(MLA_KV_LORA, N_HEADS, 128)[:, :, 64:].reshape(MLA_KV_LORA, 256)).astype(BF16)
    lw["wdvt"] = _vt_rows(p["w_in"][i][:, 2768:3024]).astype(BF16)
    wa = p["w_alpha2"][i]
    wal = jnp.zeros((LANE, 256), F32).at[0:16, 0:128].set(wa[0]).at[16:32, 128:256].set(wa[1])
    lw["walpha"] = wal.astype(BF16)
    lw["balpha"] = p["b_alpha"][i].reshape(1, 256)
    lw["w_conv"] = p["w_conv"][i]
    lw["b_conv"] = row(p["b_conv"][i])
    lw["dlam"] = p["diff_lambda"][i]
    lw["g_diff"] = jnp.broadcast_to(p["g_diff_out"][i].reshape(N_HEADS, DIFF_DV, 1), (N_HEADS, DIFF_DV, LANE))
    wb = p["w_branch"][i]
    lw["wbr"] = jnp.concatenate([_pad_heads_rows(wb[0], 64), wb[1], wb[2], _pad_heads_rows(wb[3], 64)], axis=0).astype(BF16)
    lw["w_gate"] = p["w_gate"][i].astype(BF16)
    lw["b_gate"] = row(p["b_gate"][i])
    lw["w_out"] = p["w_out"][i].astype(BF16)
    lw["w_router"] = jnp.pad(p["w_router"][i], ((0, 0), (0, LANE - N_EXPERTS))).astype(BF16)
    lw["w_router_t"] = p["w_router"][i].T.astype(BF16)
    lw["layer"] = i
    lw["w_e_gate"], lw["w_e_up"], lw["w_e_down"] = p["w_e_gate"], p["w_e_up"], p["w_e_down"]
    return lw


def _ffn(xm, h2, aff, aff_t, mod, lw):
    b, t, _ = xm.shape
    nblk = t // TOK_BLK
    cap = EC_CAPACITY * t // N_EXPERTS
    capp = -(-cap // TOK_BLK) * TOK_BLK
    pos4, s04 = _topk(aff_t, cap)
    s0_be = jnp.concatenate([s04[..., 0].transpose(0, 2, 1), jnp.full((b, N_EXPERTS, 1), cap, I32)], axis=-1)
    s0_flat = s0_be.reshape(-1)
    ys = _moe(s0_flat, pos4, h2, lw, capp)
    pos_t = pos4.transpose(0, 1, 3, 2).reshape(b, t, N_EXPERTS)
    first = jnp.minimum(s0_be[..., :-1] // SLOT_BLK, capp // SLOT_BLK - 1)
    fits = jnp.all(s0_be[..., 1:] <= (first + 2) * SLOT_BLK)
    args = (s0_flat, ys, pos_t, aff, xm, mod, lw["g_ffn_post"])
    return lax.cond(fits, functools.partial(_combine, sblk=SLOT_BLK), functools.partial(_combine, sblk=TOK_BLK), *args)


def _hybrid_layer(i, x_c, x_l, c8, need_ctx, p):
    lw = _layer_weights(i, p)
    b, t, _ = x_l.shape
    tc = x_c.shape[1]
    lam_init = 0.8 - 0.6 * math.exp(-0.3 * i)
    mod8 = _ada(c8, p["w_ada"][i], p["b_ada"][i])
    pad = lambda m: jnp.pad(m.reshape(b, 6, D), ((0, 0), (0, 2), (0, 0)))
    mod_l = pad(mod8[:b])
    mod_c = pad(jnp.broadcast_to(mod8[b:b + 1], (b, 6 * D)))

    pc = _proj(x_c, mod_c, _identity_table(tc), lw)
    pt = _proj(x_l, mod_l, _rope_table(t), lw)

    zc = jnp.zeros((b, 2, 256, 512), F32)
    zm = jnp.full((b, 2, 8, 256), NEG, F32)
    zs = jnp.zeros((b, 2, 256, 128), F32)
    hf_c, hb_c, c_fin, m_fin = _mlstm(pc, lw["w_conv"], lw["b_conv"], zc, zm)
    hf_l, hb_l, _, _ = _mlstm(pt, lw["w_conv"], lw["b_conv"], c_fin, m_fin)
    gf_c, gb_c, s_fin = _gla(pc, zs)
    gf_l, gb_l, _ = _gla(pt, s_fin)

    one_g = jnp.ones((N_HEADS, MLA_V, LANE), F32)
    fl = functools.partial(_flash, tk=FLASH_KEYS)
    ya_l = fl(pt["mqt"], pc["mk"], pc["mvt"], pt["mk"], pt["mvt"], lw["dlam"], one_g, nmap=1, finish=False, post=1.0,
              tq=min(FLASH_ROWS, t))
    yd_l = fl(pt["dqt"], pc["dk"], pc["dvt"], pt["dk"], pt["dvt"], lw["dlam"], lw["g_diff"], nmap=2, finish=True,
              post=1.0 - lam_init, tq=min(FLASH_ROWS // 2, t))
    xm, h2, aff, aff_t = _merge(x_l, mod_l, ya_l, (hf_l, hb_l), pt["lo"], (gf_l, gb_l), pt["gr"], yd_l, lw, tm=256)
    x_l = _ffn(xm, h2, aff, aff_t, mod_l, lw)

    if need_ctx:
        ya_c = fl(pc["mqt"], pc["mk"], pc["mvt"], None, None, lw["dlam"], one_g, nmap=1, finish=False, post=1.0, tq=tc)
        yd_c = fl(pc["dqt"], pc["dk"], pc["dvt"], None, None, lw["dlam"], lw["g_diff"], nmap=2, finish=True,
                  post=1.0 - lam_init, tq=tc)
        xm, h2, aff, aff_t = _merge(x_c, mod_c, ya_c, (hf_c, hb_c), pc["lo"], (gf_c, gb_c), pc["gr"], yd_c, lw, tm=tc)
        x_c = _ffn(xm, h2, aff, aff_t, mod_c, lw)
    return x_c, x_l


def kernel(x, c, ctx, c_ctx, w_ada, b_ada, g_mix_pre, g_mix_post, g_ffn_pre, g_ffn_post, w_in, g_q_lat, w_uq, g_kv_lat, w_ukv, w_conv, b_conv, b_igate, b_fgate, g_mlstm_out, w_alpha2, b_alpha, g_gla_out, diff_lambda, g_diff_out, w_branch, w_gate, b_gate, w_out, w_router, w_e_gate, w_e_up, w_e_down):
    p = dict(w_ada=w_ada, b_ada=b_ada, g_mix_pre=g_mix_pre, g_mix_post=g_mix_post, g_ffn_pre=g_ffn_pre,
             g_ffn_post=g_ffn_post, w_in=w_in, g_q_lat=g_q_lat, w_uq=w_uq, g_kv_lat=g_kv_lat, w_ukv=w_ukv,
             w_conv=w_conv, b_conv=b_conv, b_igate=b_igate, b_fgate=b_fgate, g_mlstm_out=g_mlstm_out,
             w_alpha2=w_alpha2, b_alpha=b_alpha, g_gla_out=g_gla_out, diff_lambda=diff_lambda, g_diff_out=g_diff_out,
             w_branch=w_branch, w_gate=w_gate, b_gate=b_gate, w_out=w_out, w_router=w_router,
             w_e_gate=w_e_gate.astype(BF16), w_e_up=w_e_up.astype(BF16), w_e_down=w_e_down.astype(BF16))
    b = x.shape[0]
    c8 = jnp.concatenate([c, c_ctx[None], jnp.zeros((8 - b - 1, D), F32)], axis=0)
    x_c, x_l = ctx, x
    for i in range(DEPTH):
        x_c, x_l = _hybrid_layer(i, x_c, x_l, c8, i < DEPTH - 1, p)
    return x_l
```

```python
import functools
import math

import numpy as np
import jax
import jax.numpy as jnp
from jax import lax
from jax.experimental import pallas as pl
from jax.experimental.pallas import tpu as pltpu

F32 = jnp.float32
BF16 = jnp.bfloat16
I32 = jnp.int32

D = 1024
DEPTH = 2
GRID_W = 64
N_HEADS = 4
MLA_NOPE, MLA_ROPE, MLA_V = 64, 32, 64
MLA_Q_LORA, MLA_KV_LORA = 256, 128
ML_DH = 64
GLA_DK, GLA_DV, GLA_RANK, GLA_TAU = 32, 64, 16, 16.0
DIFF_DQK, DIFF_DV = 32, 64
ROPE_DIM, ROPE_BASE = 32, 10000.0
N_EXPERTS, EC_CAPACITY, EXPERT_FF = 16, 2, 1408
NEG = -1e30
EPS = 1e-6
LOG2E = 1.4426950408889634

LANE = 128
HEAD_SLAB = 128
TOK_BLK = 256
ML_CHUNK = 256
ML_BLOCK = 256
GLA_CHUNK = 64
GLA_BLOCK = 512
GATHER_WIN = TOK_BLK + 16
GATHER_WIN_SMALL = 64
MOE_TOK = 4096
SLOT_BLK = 128
VMEM_LIMIT = 56 * 1024 * 1024

ZQ, ZKV, ZKRA, ZKRB, ZMLQK, ZMLV, ZMLO, ZGATE, ZGA = 0, 256, 384, 512, 640, 1152, 1408, 1664, 1792
ZGQ, ZGK, ZGV, ZGR, ZDQ, ZDQS, ZDK, ZDKS, NZ = 1920, 2048, 2176, 2432, 2688, 2944, 3200, 3456, 3712
KV_CHUNK = 256
VT_ROWS = 80
FLASH_ROWS = 512
FLASH_KEYS = 512
FLASH_UNROLL = 16


def _swap32(c):
    return (c // 32) * 32 + ((c % 32) ^ 8)


def _win_index():
    idx = -np.ones((NZ,), np.int64)
    idx[ZQ:ZQ + 256] = np.arange(0, 256)
    idx[ZKV:ZKV + 128] = np.arange(256, 384)
    r = np.arange(32)
    idx[ZKRA + 64:ZKRA + 96] = 384 + r
    idx[ZKRB + 64:ZKRB + 96] = 384 + (r ^ 8)
    ml = 416
    idx[ZMLQK:ZMLQK + 512] = ml + np.arange(512)
    idx[ZMLV:ZMLV + 256] = ml + 512 + np.arange(256)
    idx[ZMLO:ZMLO + 256] = ml + 768 + np.arange(256)
    idx[ZGATE:ZGATE + 16] = ml + 1024 + np.arange(16)
    gl = 1456
    idx[ZGQ:ZGQ + 128] = gl + np.arange(128)
    idx[ZGK:ZGK + 128] = gl + 128 + np.arange(128)
    idx[ZGV:ZGV + 256] = gl + 256 + np.arange(256)
    idx[ZGR:ZGR + 256] = gl + 512 + np.arange(256)
    idx[ZGA:ZGA + 32] = gl + 768 + np.arange(32)
    df = 2256
    c = np.arange(256)
    idx[ZDQ:ZDQ + 256] = df + c
    idx[ZDQS:ZDQS + 256] = df + _swap32(c)
    idx[ZDK:ZDK + 256] = df + 256 + c
    idx[ZDKS:ZDKS + 256] = df + 256 + _swap32(c)
    return idx


def _vt_rows(w_cols):
    n = w_cols.shape[0]
    w4 = w_cols.T.reshape(N_HEADS, 64, n)
    return jnp.pad(w4, ((0, 0), (0, VT_ROWS - 64), (0, 0))).reshape(N_HEADS * VT_ROWS, n)


_WIN_IDX = _win_index()


def _gather_cols(w, idx):
    safe = np.maximum(idx, 0)
    return jnp.where(jnp.asarray(idx >= 0)[None, :], w[:, safe], 0.0)


def _pad_heads_rows(w, width):
    n = w.shape[1]
    w4 = w.reshape(N_HEADS, width, n)
    return jnp.pad(w4, ((0, 0), (0, HEAD_SLAB - width), (0, 0))).reshape(N_HEADS * HEAD_SLAB, n)


def _cparams(sem):
    return pltpu.CompilerParams(dimension_semantics=sem, vmem_limit_bytes=VMEM_LIMIT)


def _rms(x):
    return x * lax.rsqrt(jnp.mean(x * x, axis=-1, keepdims=True) + EPS)


def _sigmoid(x):
    return 0.5 * jnp.tanh(0.5 * x) + 0.5


def _log_sigmoid(x):
    return jnp.minimum(x, 0.0) - jnp.log1p(jnp.exp(-jnp.abs(x)))


def _dot(a, b, precision=None):
    return jnp.dot(a, b, preferred_element_type=F32, precision=precision)


def _dot_nt(a, b, precision=None):
    return lax.dot_general(a, b, (((1,), (1,)), ((), ())), preferred_element_type=F32, precision=precision)


def _dot_tn(a, b, precision=None):
    return lax.dot_general(a, b, (((0,), (0,)), ((), ())), preferred_element_type=F32, precision=precision)


def _split_bf16(x, parts):
    out, r = [], x
    for _ in range(parts):
        t = r.astype(BF16)
        out.append(t)
        r = r - t.astype(F32)
    return out


def _dot_sel(sel, x, parts=3):
    return sum(_dot(sel, t) for t in _split_bf16(x, parts))


def _const_spec(shape):
    nd = len(shape)
    return pl.BlockSpec(shape, lambda *_: (0,) * nd)


def _ada_kernel(c_ref, w_ref, b_ref, o_ref):
    cv = c_ref[...]
    s = (cv * _sigmoid(cv)).astype(BF16)
    o_ref[...] = _dot(s, w_ref[...].astype(BF16)) + b_ref[...]


def _ada(c8, w_ada, b_ada):
    n, tn = 6 * D, 1024
    return pl.pallas_call(
        _ada_kernel, name="ada", grid=(n // tn,),
        in_specs=[pl.BlockSpec((8, D), lambda j: (0, 0)), pl.BlockSpec((D, tn), lambda j: (0, j)),
                  pl.BlockSpec((1, tn), lambda j: (0, j))],
        out_specs=pl.BlockSpec((8, tn), lambda j: (0, j)),
        out_shape=jax.ShapeDtypeStruct((8, n), F32), compiler_params=_cparams(("arbitrary",)),
    )(c8, w_ada, b_ada.reshape(1, n))


_PROJ_OUT = (
    ("mk", 512, BF16),
    ("lqk", 512, F32), ("lv", 256, BF16), ("lo", 256, BF16), ("gc", 128, F32),
    ("gq", 128, BF16), ("gk", 128, BF16), ("gv", 256, BF16), ("gr", 256, BF16), ("glg", 256, F32),
    ("dk", 512, BF16),
)


def _proj_kernel(x_ref, mod_ref, g_ref, w_ref, tab_ref, gq_ref, wq_ref, wqs_ref, gkv_ref, wk_ref, wvt_ref,
                 wgt_ref, gbr_ref, gbc_ref, wal_ref, bal_ref, wdvt_ref,
                 mk_ref, lqk_ref, lv_ref, lo_ref, gc_ref, gq_o, gk_o, gv_o, gr_o, glg_o,
                 dk_ref, grow_ref, mvt_ref, dvt_ref, mqt_ref, dqt_ref):
    x = x_ref[0]
    tm = x.shape[0]
    mod = mod_ref[0]
    h = _rms(x) * g_ref[...] * (1.0 + mod[1:2]) + mod[0:1]
    hb = h.astype(BF16)
    z = _dot(hb, w_ref[...])
    tab = tab_ref[...]
    ct, st, cd, sd = tab[:, 0:128], tab[:, 128:256], tab[:, 256:384], tab[:, 384:512]
    lane = lax.broadcasted_iota(I32, (tm, LANE), 1)

    qn = (_rms(z[:, ZQ:ZQ + 256]) * gq_ref[...]).astype(BF16)
    qa = _dot(qn, wq_ref[...])
    qb = _dot(qn, wqs_ref[...])
    qscale = (MLA_NOPE + MLA_ROPE) ** -0.5 * LOG2E
    for hh in range(N_HEADS):
        sl = slice(HEAD_SLAB * hh, HEAD_SLAB * (hh + 1))
        mqt_ref[0, hh] = ((qa[:, sl] * ct + qb[:, sl] * st) * qscale).T.astype(BF16)
    kvn = (_rms(z[:, ZKV:ZKV + 128]) * gkv_ref[...]).astype(BF16)
    kk = _dot(kvn, wk_ref[...])
    kr = z[:, ZKRA:ZKRA + 128] * ct + z[:, ZKRB:ZKRB + 128] * st
    for hh in range(N_HEADS):
        sl = slice(HEAD_SLAB * hh, HEAD_SLAB * (hh + 1))
        mk_ref[0, :, sl] = (kk[:, sl] + kr).astype(BF16)
    ones_row = lax.broadcasted_iota(I32, (N_HEADS * VT_ROWS, tm), 0) % VT_ROWS == MLA_V
    mvt_ref[0, 0] = jnp.where(ones_row, 1.0, _dot_nt(wvt_ref[...], kvn)).astype(BF16)
    dvt_ref[0, 0] = jnp.where(ones_row, 1.0, _dot_nt(wdvt_ref[...], hb)).astype(BF16)

    lqk_ref[0] = z[:, ZMLQK:ZMLQK + 512]
    lv_ref[0] = z[:, ZMLV:ZMLV + 256].astype(BF16)
    lo_ref[0] = z[:, ZMLO:ZMLO + 256].astype(BF16)
    gcol = z[:, ZGATE:ZGATE + 128] + gbr_ref[...]
    gc_ref[0] = jnp.where(lane < 8, gcol, jnp.where(lane < 16, _log_sigmoid(gcol), 0.0))
    zr = _dot_nt(wgt_ref[...], hb) + gbc_ref[...]
    rowi = lax.broadcasted_iota(I32, zr.shape, 0)
    grow_ref[0] = jnp.where(rowi < 8, zr, _log_sigmoid(zr))

    gq_o[0] = (z[:, ZGQ:ZGQ + 128] * GLA_DK ** -0.5).astype(BF16)
    gk_o[0] = z[:, ZGK:ZGK + 128].astype(BF16)
    gv_o[0] = z[:, ZGV:ZGV + 256].astype(BF16)
    gr_o[0] = z[:, ZGR:ZGR + 256].astype(BF16)
    zg = _dot(z[:, ZGA:ZGA + 128].astype(BF16), wal_ref[...]) + bal_ref[...]
    glg_o[0] = _log_sigmoid(zg) * (1.0 / GLA_TAU)

    dscale = DIFF_DQK ** -0.5 * LOG2E
    for g in range(2):
        gs = slice(128 * g, 128 * (g + 1))
        qg = (z[:, ZDQ:ZDQ + 256][:, gs] * cd + z[:, ZDQS:ZDQS + 256][:, gs] * sd) * dscale
        kg = z[:, ZDK:ZDK + 256][:, gs] * cd + z[:, ZDKS:ZDKS + 256][:, gs] * sd
        for hl in range(2):
            hh = 2 * g + hl
            for m in range(2):
                lo = 64 * hl + 32 * m
                dqt_ref[0, 2 * hh + m] = jnp.where((lane >= lo) & (lane < lo + 32), qg, 0.0).T.astype(BF16)
            dk_ref[0, :, HEAD_SLAB * hh:HEAD_SLAB * (hh + 1)] = jnp.where(
                (lane >= 64 * hl) & (lane < 64 * hl + 64), kg, 0.0).astype(BF16)


def _proj(x, mod, tab, lw):
    b, t, _ = x.shape
    tm = KV_CHUNK
    consts = [lw["g_mix_pre"], lw["w_ext"], None, lw["g_q_lat"], lw["wq"], lw["wqs"], lw["g_kv_lat"], lw["wk"], lw["wvt"],
              lw["wgt"], lw["gate_bias_row"], lw["gate_bias_col"], lw["walpha"], lw["balpha"], lw["wdvt"]]
    in_specs = [pl.BlockSpec((1, tm, D), lambda bi, i: (bi, i, 0)), pl.BlockSpec((1, 8, D), lambda bi, i: (bi, 0, 0))]
    args = [x, mod]
    for cst in consts:
        if cst is None:
            in_specs.append(pl.BlockSpec((tm, 512), lambda bi, i: (i, 0)))
            args.append(tab)
        else:
            in_specs.append(_const_spec(cst.shape))
            args.append(cst)
    out_specs = [pl.BlockSpec((1, tm, w), lambda bi, i: (bi, i, 0)) for _, w, _ in _PROJ_OUT]
    out_shape = [jax.ShapeDtypeStruct((b, t, w), dt) for _, w, dt in _PROJ_OUT]
    out_specs.append(pl.BlockSpec((1, 16, tm), lambda bi, i: (bi, 0, i)))
    out_shape.append(jax.ShapeDtypeStruct((b, 16, t), F32))
    for _ in range(2):
        out_specs.append(pl.BlockSpec((1, 1, N_HEADS * VT_ROWS, tm), lambda bi, i: (bi, i, 0, 0)))
        out_shape.append(jax.ShapeDtypeStruct((b, t // tm, N_HEADS * VT_ROWS, tm), BF16))
    for nslab in (N_HEADS, 2 * N_HEADS):
        out_specs.append(pl.BlockSpec((1, nslab, HEAD_SLAB, tm), lambda bi, i: (bi, 0, 0, i)))
        out_shape.append(jax.ShapeDtypeStruct((b, nslab, HEAD_SLAB, t), BF16))
    outs = pl.pallas_call(
        _proj_kernel, name="proj", grid=(b, t // tm), in_specs=in_specs, out_specs=out_specs, out_shape=out_shape,
        compiler_params=_cparams(("parallel", "arbitrary")),
    )(*args)
    res = {name: o for (name, _, _), o in zip(_PROJ_OUT, outs[:-5])}
    res["grow"], res["mvt"], res["dvt"], res["mqt"], res["dqt"] = outs[-5:]
    return res


def _flash_kernel(*refs, nmap, has_lat, tk, finish, post):
    if has_lat:
        q_ref, kc_ref, vc_ref, kl_ref, vl_ref, dl_ref, g_ref, o_ref, s_ref, acc_ref = refs
    else:
        q_ref, kc_ref, vc_ref, dl_ref, g_ref, o_ref = refs
    tq = q_ref.shape[3]
    qt = q_ref[0, 0] if nmap == 1 else jnp.concatenate([q_ref[0, mm] for mm in range(nmap)], axis=1)
    rows = nmap * tq
    sub = tk // KV_CHUNK

    def softmax(s, smax, m):
        m_new = jnp.maximum(m, smax)
        return m_new, jnp.exp2(m - m_new), jnp.exp2(s - m_new).astype(BF16)

    def pv(p, vts):
        return _dot(vts[0] if len(vts) == 1 else jnp.concatenate(vts, axis=1), p)

    s_ctx = _dot(kc_ref[0], qt)
    if has_lat:
        n = kl_ref.shape[1] // tk
        unroll = min(FLASH_UNROLL, n)

        def scores(j):
            if isinstance(j, int):
                return _dot(kl_ref[0, j * tk:(j + 1) * tk, :], qt)
            off = pl.multiple_of(j * tk, tk)
            return _dot(kl_ref[0, pl.ds(off, tk), :], qt)

        def values(j):
            return [vl_ref[0, j * sub + c] for c in range(sub)]

        def produce(slot, j):
            s = scores(j)
            s_ref[slot] = s
            return jnp.max(s, axis=0, keepdims=True)

        smax0 = produce(0, 0)

    m, _, p = softmax(s_ctx, jnp.max(s_ctx, axis=0, keepdims=True), jnp.full((1, rows), NEG, F32))
    acc = pv(p, [vc_ref[0, 0]])
    if has_lat:
        acc_ref[...] = acc

        def body(jj, carry):
            m, smax = carry
            j = unroll * jj
            for u in range(unroll):
                smax_next = smax
                if not isinstance(j, int):
                    smax_next = produce((u + 1) % 2, jnp.minimum(j + u + 1, n - 1))
                elif j + u + 1 < n:
                    smax_next = produce((u + 1) % 2, j + u + 1)
                m, alpha, p = softmax(s_ref[u % 2], smax, m)
                acc_ref[...] = alpha * acc_ref[...] + pv(p, values(j + u))
                smax = smax_next
            return m, smax

        if unroll == n:
            body(0, (m, smax0))
        else:
            lax.fori_loop(0, n // unroll, body, (m, smax0))
        acc = acc_ref[...]

    o = acc[0:MLA_V, :] / acc[MLA_V:MLA_V + 1, :]
    if nmap == 2:
        lv = dl_ref[...]
        lam = (jnp.exp(jnp.sum(lv[0:1] * lv[1:2], axis=-1, keepdims=True))
               - jnp.exp(jnp.sum(lv[2:3] * lv[3:4], axis=-1, keepdims=True)) + (1.0 - post))
        o = o[:, :tq] - lam * o[:, tq:]
    if finish:
        ms = jnp.mean(o * o, axis=0, keepdims=True)
        o = o * lax.rsqrt(ms + EPS) * jnp.concatenate([g_ref[0]] * (tq // LANE), axis=1) * post
    o_pad = jnp.concatenate([o, jnp.zeros((HEAD_SLAB - MLA_V, tq), F32)], axis=0)
    o_ref[0] = o_pad.T.astype(BF16)


def _flash(q, kc, vct, kl, vlt, dlam, g_out, *, nmap, finish, post, tq, tk):
    b, _, _, t = q.shape
    has_lat = kl is not None
    assert kc.shape[1] == KV_CHUNK and tk % KV_CHUNK == 0
    kspec = lambda n: pl.BlockSpec((1, n, HEAD_SLAB), lambda bi, h, i: (bi, 0, h))
    vspec = lambda n: pl.BlockSpec((1, n // KV_CHUNK, VT_ROWS, KV_CHUNK), lambda bi, h, i: (bi, 0, h, 0))
    in_specs = [pl.BlockSpec((1, nmap, HEAD_SLAB, tq), lambda bi, h, i: (bi, h, 0, i)), kspec(KV_CHUNK), vspec(KV_CHUNK)]
    args = [q, kc, vct]
    scratch = []
    if has_lat:
        tl = kl.shape[1]
        assert (tl // tk) % min(FLASH_UNROLL, tl // tk) == 0
        in_specs += [kspec(tl), vspec(tl)]
        args += [kl, vlt]
        scratch = [pltpu.VMEM((2, tk, nmap * tq), F32), pltpu.VMEM((VT_ROWS, nmap * tq), F32)]
    in_specs += [_const_spec(dlam.shape), pl.BlockSpec((1, MLA_V, LANE), lambda bi, h, i: (h, 0, 0))]
    args += [dlam, g_out]
    return pl.pallas_call(
        functools.partial(_flash_kernel, nmap=nmap, has_lat=has_lat, tk=tk, finish=finish, post=post),
        name="flash_diff" if nmap == 2 else "flash_mla",
        grid=(b, N_HEADS, t // tq), in_specs=in_specs,
        out_specs=pl.BlockSpec((1, tq, HEAD_SLAB), lambda bi, h, i: (bi, i, h)),
        out_shape=jax.ShapeDtypeStruct((b, t, N_HEADS * HEAD_SLAB), BF16),
        scratch_shapes=scratch,
        compiler_params=_cparams(("parallel", "parallel", "arbitrary")),
    )(*args)


def _head_of(shape, axis, width):
    return (lax.broadcasted_iota(I32, shape, axis) % (N_HEADS * width)) // width


def _mlstm_conv(first, last, x, xprev, xnext, wc, bcv):
    n = x.shape[0]
    row = lax.broadcasted_iota(I32, x.shape, 0)
    pr = jnp.where(first, 0.0, xprev[7:8, :])
    nx = jnp.where(last, 0.0, xnext[0:1, :])
    xm = jnp.where(row == 0, pr, pltpu.roll(x, 1, 0))
    xp = jnp.where(row == n - 1, nx, pltpu.roll(x, n - 1, 0))
    y = xm * wc[0:1] + x * wc[1:2] + xp * wc[2:3] + bcv
    qk = y * _sigmoid(y)
    return qk[:, :256], qk[:, 256:] * ML_DH ** -0.5


def _mlstm_dir(d, q, k, v, gcol, grow, cb, m0e):
    L = q.shape[0]
    li = lax.broadcasted_iota(I32, (L, L), 0)
    si = lax.broadcasted_iota(I32, (L, L), 1)
    tin = (si <= li) if d == 0 else (si >= li)
    tinb = tin.astype(BF16)
    bcol = _dot_sel(tinb, gcol)
    brow = sum(_dot_nt(t, tinb) for t in _split_bf16(grow, 3))
    hm256 = _head_of((L, 256), 1, ML_DH)
    hm512 = _head_of((L, 512), 1, ML_DH)
    e_idx = L - 1 if d == 0 else 0

    d_blk, inter_blk = [], []
    for hh in range(N_HEADS):
        c = 4 * d + hh
        bc = bcol[:, 8 + c:9 + c]
        d_blk.append(jnp.where(tin, bc - brow[8 + c:9 + c, :] + grow[c:c + 1, :], NEG))
        inter_blk.append(bc + m0e[0:1, 64 * hh:64 * hh + 1])
    d_st = jnp.concatenate(d_blk, axis=0)
    inter_st = jnp.concatenate(inter_blk, axis=0)
    mt = jnp.maximum(inter_st, jnp.max(d_st, axis=-1, keepdims=True))
    q_st = jnp.concatenate([jnp.where(hm256 == hh, q, 0.0) for hh in range(N_HEADS)], axis=0).astype(BF16)
    s_st = (jnp.exp(d_st - mt) * _dot_nt(q_st, k.astype(BF16))).astype(BF16)
    vext = jnp.concatenate([v, jnp.ones((L, 256), BF16)], axis=1)
    r = _dot(s_st, vext)
    aint = jnp.exp(inter_st - mt)
    p = _dot(q.astype(BF16), cb.astype(BF16))
    tot = jnp.zeros((L, 512), F32)
    mte = jnp.zeros((L, 256), F32)
    for hh in range(N_HEADS):
        rs = slice(hh * L, (hh + 1) * L)
        tot = jnp.where(hm512 == hh, r[rs] + aint[rs] * p, tot)
        mte = jnp.where(hm256 == hh, mt[rs], mte)
    hout = tot[:, :256] / jnp.maximum(jnp.abs(tot[:, 256:]), jnp.exp(-mte))

    wexp = jnp.zeros((L, 256), F32)
    arow = jnp.zeros((1, 512), F32)
    grw = jnp.zeros((1, 512), F32)
    mnew = jnp.zeros((1, 256), F32)
    hr512 = _head_of((1, 512), 1, ML_DH)
    hr256 = _head_of((1, 256), 1, ML_DH)
    for hh in range(N_HEADS):
        c = 4 * d + hh
        bc = bcol[:, 8 + c:9 + c]
        be = bc[e_idx:e_idx + 1, :]
        wl = be - bc + gcol[:, c:c + 1]
        mloc = jnp.max(wl, axis=0, keepdims=True)
        m0h = m0e[0:1, 64 * hh:64 * hh + 1]
        mn = jnp.maximum(be + m0h, mloc)
        wexp = jnp.where(hm256 == hh, jnp.exp(wl - mloc), wexp)
        arow = jnp.where(hr512 == hh, jnp.exp(be + m0h - mn), arow)
        grw = jnp.where(hr512 == hh, jnp.exp(mloc - mn), grw)
        mnew = jnp.where(hr256 == hh, mn, mnew)
    cl = _dot_tn((k * wexp).astype(BF16), vext)
    bd = lax.broadcasted_iota(I32, (256, 512), 0) // ML_DH == _head_of((256, 512), 1, ML_DH)
    return hout, arow * cb + jnp.where(bd, grw * cl, 0.0), jnp.broadcast_to(mnew, (8, 256))


def _mlstm_kernel(xf, xfp, xfn, xb, xbp, xbn, vf, vb, gcf, gcb, grf, grb, wc_ref, bc_ref, c0_ref, m0_ref,
                  hf_ref, hb_ref, c_ref, m_ref):
    i = pl.program_id(1)
    n = pl.num_programs(1)

    @pl.when(i == 0)
    def _():
        c_ref[...] = c0_ref[...]
        m_ref[...] = m0_ref[...]

    wc = wc_ref[...]
    bcv = bc_ref[...]
    L = ML_CHUNK
    nsub = xf.shape[1] // L
    streams = ((0, i == 0, i == n - 1, xf, xfp, xfn, vf, gcf, grf, hf_ref),
               (1, i == n - 1, i == 0, xb, xbp, xbn, vb, gcb, grb, hb_ref))
    for d, first, last, x, xp, xn, v, gc, gr, h_ref in streams:
        q, k = _mlstm_conv(first, last, x[0], xp[0], xn[0], wc, bcv)
        cb, m0e = c_ref[0, d], m_ref[0, d]
        for c in (range(nsub) if d == 0 else reversed(range(nsub))):
            sl = slice(c * L, (c + 1) * L)
            h_ref[0, sl, :], cb, m0e = _mlstm_dir(d, q[sl], k[sl], v[0, sl, :], gc[0, sl, :], gr[0, :, sl], cb, m0e)
        c_ref[0, d], m_ref[0, d] = cb, m0e


def _mlstm(pr, w_conv, b_conv, c0, m0):
    x, v, gc, gr = pr["lqk"], pr["lv"], pr["gc"], pr["grow"]
    b, t, _ = x.shape
    L = min(ML_BLOCK, t)
    n = t // L
    r8 = L // 8
    last8 = t // 8 - 1

    def fw(bi, i):
        return (bi, i, 0)

    def bw(bi, i):
        return (bi, n - 1 - i, 0)

    def halo(ix, shift):
        def f(bi, i):
            blk = ix(bi, i)[1]
            return (bi, jnp.clip(blk * r8 + shift, 0, last8), 0)
        return f

    main = lambda w, ix: pl.BlockSpec((1, L, w), ix)
    in_specs = [main(512, fw), pl.BlockSpec((1, 8, 512), halo(fw, -1)), pl.BlockSpec((1, 8, 512), halo(fw, r8)),
                main(512, bw), pl.BlockSpec((1, 8, 512), halo(bw, -1)), pl.BlockSpec((1, 8, 512), halo(bw, r8)),
                main(256, fw), main(256, bw), main(128, fw), main(128, bw),
                pl.BlockSpec((1, 16, L), lambda bi, i: (bi, 0, i)), pl.BlockSpec((1, 16, L), lambda bi, i: (bi, 0, n - 1 - i)),
                _const_spec(w_conv.shape), _const_spec(b_conv.shape),
                pl.BlockSpec((1, 2, 256, 512), lambda bi, i: (bi, 0, 0, 0)), pl.BlockSpec((1, 2, 8, 256), lambda bi, i: (bi, 0, 0, 0))]
    out_specs = [main(256, fw), main(256, bw),
                 pl.BlockSpec((1, 2, 256, 512), lambda bi, i: (bi, 0, 0, 0)), pl.BlockSpec((1, 2, 8, 256), lambda bi, i: (bi, 0, 0, 0))]
    out_shape = [jax.ShapeDtypeStruct((b, t, 256), F32), jax.ShapeDtypeStruct((b, t, 256), F32),
                 jax.ShapeDtypeStruct(c0.shape, F32), jax.ShapeDtypeStruct(m0.shape, F32)]
    return pl.pallas_call(
        _mlstm_kernel, name="mlstm", grid=(b, n), in_specs=in_specs, out_specs=out_specs, out_shape=out_shape,
        compiler_params=_cparams(("parallel", "arbitrary")),
    )(x, x, x, x, x, x, v, v, gc, gc, gr, gr, w_conv, b_conv, c0, m0)


def _gla_chunk(d, q, k, v, lg, sb):
    L = q.shape[0]
    li = lax.broadcasted_iota(I32, (L, L), 0)
    si = lax.broadcasted_iota(I32, (L, L), 1)
    tin = (si <= li) if d == 0 else (si >= li)
    lgd = lg[:, 128 * d:128 * (d + 1)]
    gcum = _dot_sel(tin.astype(BF16), lgd)
    e_idx = L - 1 if d == 0 else 0
    gend = gcum[e_idx:e_idx + 1, :]
    qf, kf = q.astype(F32), k.astype(F32)
    q_dec = qf * jnp.exp(gcum)
    k_dec = (kf * jnp.exp(-gcum)).astype(BF16)
    k_end = (kf * jnp.exp(gend - gcum)).astype(BF16)
    hm128 = _head_of((L, 128), 1, GLA_DK)
    hm256 = _head_of((L, 256), 1, GLA_DV)
    q_st = jnp.concatenate([jnp.where(hm128 == hh, q_dec, 0.0) for hh in range(N_HEADS)], axis=0).astype(BF16)
    att = _dot_nt(q_st, k_dec)
    tin4 = jnp.concatenate([tin] * N_HEADS, axis=0)
    o_st = _dot(jnp.where(tin4, att, 0.0).astype(BF16), v)
    o = _dot_nt(q_dec.astype(BF16), sb.astype(BF16))
    for hh in range(N_HEADS):
        o = o + jnp.where(hm256 == hh, o_st[hh * L:(hh + 1) * L], 0.0)
    bd = lax.broadcasted_iota(I32, (256, 128), 0) // GLA_DV == _head_of((256, 128), 1, GLA_DK)
    return o, jnp.exp(gend) * sb + jnp.where(bd, _dot_tn(v, k_end), 0.0)


def _gla_kernel(qf, kf, vf, lf, qb, kb, vb, lb, s0_ref, of_ref, ob_ref, s_ref):
    i = pl.program_id(1)

    @pl.when(i == 0)
    def _():
        s_ref[...] = s0_ref[...]

    L = GLA_CHUNK
    nsub = qf.shape[1] // L
    for d, (q, k, v, lg, o_ref) in enumerate(((qf, kf, vf, lf, of_ref), (qb, kb, vb, lb, ob_ref))):
        sb = s_ref[0, d]
        for c in (range(nsub) if d == 0 else reversed(range(nsub))):
            sl = slice(c * L, (c + 1) * L)
            o_ref[0, sl, :], sb = _gla_chunk(d, q[0, sl, :], k[0, sl, :], v[0, sl, :], lg[0, sl, :], sb)
        s_ref[0, d] = sb


def _gla(pr, s0):
    q, k, v, lg = pr["gq"], pr["gk"], pr["gv"], pr["glg"]
    b, t, _ = q.shape
    L = min(GLA_BLOCK, t)
    n = t // L
    fw = lambda bi, i: (bi, i, 0)
    bw = lambda bi, i: (bi, n - 1 - i, 0)
    blk = lambda w, ix: pl.BlockSpec((1, L, w), ix)
    st_spec = pl.BlockSpec((1, 2, 256, 128), lambda bi, i: (bi, 0, 0, 0))
    return pl.pallas_call(
        _gla_kernel, name="gla", grid=(b, n),
        in_specs=[blk(128, fw), blk(128, fw), blk(256, fw), blk(256, fw),
                  blk(128, bw), blk(128, bw), blk(256, bw), blk(256, bw), st_spec],
        out_specs=[blk(256, fw), blk(256, bw), st_spec],
        out_shape=[jax.ShapeDtypeStruct((b, t, 256), F32), jax.ShapeDtypeStruct((b, t, 256), F32),
                   jax.ShapeDtypeStruct(s0.shape, F32)],
        compiler_params=_cparams(("parallel", "arbitrary")),
    )(q, k, v, lg, q, k, v, lg, s0)


def _head_rms_expanded(x, width):
    n = x.shape[1]
    bd = (lax.broadcasted_iota(I32, (n, n), 0) // width == lax.broadcasted_iota(I32, (n, n), 1) // width).astype(BF16)
    return sum(_dot(t, bd) for t in _split_bf16(x * x, 2)) * (1.0 / width)


def _merge_kernel(x_ref, mod_ref, ya_ref, hf_ref, hb_ref, lo_ref, gf_ref, gb_ref, gr_ref, yd_ref,
                  gpre_ref, wg_ref, bg_ref, wbr_ref, wo_ref, gpost_ref, gffn_ref, wr_ref, wrt_ref, gml_ref, ggla_ref,
                  xm_ref, h2_ref, aff_ref, afft_ref):
    x = x_ref[0]
    tm = x.shape[0]
    mod = mod_ref[0]
    hb = (_rms(x) * gpre_ref[...] * (1.0 + mod[1:2]) + mod[0:1]).astype(BF16)

    hs = hf_ref[0] + hb_ref[0]
    y_ml = _sigmoid(lo_ref[0].astype(F32)) * (hs * lax.rsqrt(_head_rms_expanded(hs, ML_DH) + EPS) * gml_ref[...])
    gs = gf_ref[0] + gb_ref[0]
    rr = gr_ref[0].astype(F32)
    y_gla = rr * _sigmoid(rr) * (gs * lax.rsqrt(_head_rms_expanded(gs, GLA_DV) + EPS) * ggla_ref[...])

    branches = ((ya_ref[0], 0, 512), (y_ml.astype(BF16), 512, 256), (y_gla.astype(BF16), 768, 256), (yd_ref[0], 1024, 512))
    mix = jnp.zeros((tm, D), F32)
    for nb, (yb, r0, rw) in enumerate(branches):
        gate = _sigmoid(_dot(hb, wg_ref[:, nb * D:(nb + 1) * D]) + bg_ref[:, nb * D:(nb + 1) * D])
        mix = mix + gate * _dot(yb, wbr_ref[r0:r0 + rw, :])
    y = _dot(mix.astype(BF16), wo_ref[...])
    xm = x + mod[2:3] * (_rms(y) * gpost_ref[...])
    xm_ref[0] = xm

    h2 = (_rms(xm) * gffn_ref[...] * (1.0 + mod[4:5]) + mod[3:4]).astype(BF16)
    h2_ref[0] = h2
    lane = lax.broadcasted_iota(I32, (tm, LANE), 1)
    lg = jnp.where(lane < N_EXPERTS, _dot(h2, wr_ref[...]), NEG)
    e = jnp.exp(lg - jnp.max(lg, axis=-1, keepdims=True))
    aff_ref[0] = (e / jnp.sum(e, axis=-1, keepdims=True))[:, :N_EXPERTS]
    lt = _dot_nt(wrt_ref[...], h2)
    et = jnp.exp(lt - jnp.max(lt, axis=0, keepdims=True))
    afft_ref[0] = et / jnp.sum(et, axis=0, keepdims=True)


def _merge(x, mod, ya, ml, lo, gl, gr, yd, lw, tm):
    b, t, _ = x.shape
    tok = lambda w: pl.BlockSpec((1, tm, w), lambda bi, i: (bi, i, 0))
    consts = [lw["g_mix_pre"], lw["w_gate"], lw["b_gate"], lw["wbr"], lw["w_out"], lw["g_mix_post"], lw["g_ffn_pre"],
              lw["w_router"], lw["w_router_t"], lw["g_mlstm_out"], lw["g_gla_out"]]
    in_specs = [tok(D), pl.BlockSpec((1, 8, D), lambda bi, i: (bi, 0, 0)), tok(512), tok(256), tok(256), tok(256),
                tok(256), tok(256), tok(256), tok(512)] + [_const_spec(c.shape) for c in consts]
    out_specs = [tok(D), tok(D), tok(N_EXPERTS), pl.BlockSpec((1, N_EXPERTS, tm), lambda bi, i: (bi, 0, i))]
    out_shape = [jax.ShapeDtypeStruct((b, t, D), F32), jax.ShapeDtypeStruct((b, t, D), BF16),
                 jax.ShapeDtypeStruct((b, t, N_EXPERTS), F32), jax.ShapeDtypeStruct((b, N_EXPERTS, t), F32)]
    return pl.pallas_call(
        _merge_kernel, name="merge", grid=(b, t // tm), in_specs=in_specs, out_specs=out_specs, out_shape=out_shape,
        compiler_params=_cparams(("parallel", "arbitrary")),
    )(x, mod, ya, ml[0], ml[1], lo, gl[0], gl[1], gr, yd, *consts)


def _topk_kernel(a_ref, pos_ref, s0_ref, *, cap):
    nblk = a_ref.shape[1]
    bits = pltpu.bitcast(a_ref[0], I32)

    def bisect(i, thr):
        cand = thr | (1 << (30 - i))
        cnt = jnp.sum((bits >= cand).astype(I32), axis=(0, 2), keepdims=True)
        return jnp.where(cnt >= cap, cand, thr)

    thr3 = lax.fori_loop(0, 31, bisect, jnp.zeros((1, N_EXPERTS, 1), I32))
    need3 = cap - jnp.sum((bits > thr3).astype(I32), axis=(0, 2), keepdims=True)
    thr, need = thr3[0], need3[0].astype(F32)
    upper = (lax.broadcasted_iota(I32, (TOK_BLK, TOK_BLK), 0) <= lax.broadcasted_iota(I32, (TOK_BLK, TOK_BLK), 1)).astype(BF16)

    def blk(j, carry):
        c_eq, c_sel = carry
        bj = pltpu.bitcast(a_ref[0, j], I32)
        gt, eq = bj > thr, bj == thr
        cum_eq = _dot(eq.astype(BF16), upper) + c_eq
        sel = gt | (eq & (cum_eq <= need))
        cum_sel = _dot(sel.astype(BF16), upper) + c_sel
        pos_ref[0, j] = jnp.where(sel, cum_sel - 1.0, -1.0).astype(I32)
        s0_ref[0, j] = jnp.broadcast_to(c_sel, (N_EXPERTS, LANE)).astype(I32)
        return cum_eq[:, TOK_BLK - 1:TOK_BLK], cum_sel[:, TOK_BLK - 1:TOK_BLK]

    zero = jnp.zeros((N_EXPERTS, 1), F32)
    lax.fori_loop(0, nblk, blk, (zero, zero))


def _topk(aff_t, cap):
    b, _, t = aff_t.shape
    nblk = t // TOK_BLK
    a4 = aff_t.reshape(b, N_EXPERTS, nblk, TOK_BLK).transpose(0, 2, 1, 3)
    spec = lambda w: pl.BlockSpec((1, nblk, N_EXPERTS, w), lambda bi: (bi, 0, 0, 0))
    return pl.pallas_call(
        functools.partial(_topk_kernel, cap=cap), name="topk", grid=(b,),
        in_specs=[spec(TOK_BLK)], out_specs=[spec(TOK_BLK), spec(LANE)],
        out_shape=[jax.ShapeDtypeStruct((b, nblk, N_EXPERTS, TOK_BLK), I32),
                   jax.ShapeDtypeStruct((b, nblk, N_EXPERTS, LANE), I32)],
        compiler_params=_cparams(("parallel",)),
    )(a4)


def _moe_kernel(s0_ref, pos_ref, h_ref, wg_ref, wu_ref, wd_ref, ys_ref, xs_ref, *, nbatch, nblk, nsub, rows_step):
    q, tb = pl.program_id(0), pl.program_id(1)
    cur = q % 2

    @pl.when(q < N_EXPERTS * nbatch)
    def _():
        e, bi = q // nbatch, q % nbatch

        @pl.when(tb == 0)
        def _():
            xs_ref[cur] = jnp.zeros(xs_ref.shape[1:], BF16)

        base = (bi * N_EXPERTS + e) * (nblk + 1) + tb * nsub
        s0s = [s0_ref[base + sb] for sb in range(nsub + 1)]
        a0s = [pl.multiple_of((s0 // 16) * 16, 16) for s0 in s0s[:-1]]
        spans = [s0s[sb + 1] - a0s[sb] for sb in range(nsub)]

        def gather(win, sb):
            a0 = a0s[sb]
            prow = pos_ref[0, sb, pl.ds(e, 1), :]
            slot = lax.broadcasted_iota(I32, (win, TOK_BLK), 0) + a0
            rows = _dot((slot == prow).astype(BF16), h_ref[0, sb * TOK_BLK:(sb + 1) * TOK_BLK, :])
            xs_ref[cur, pl.ds(a0, win), :] = xs_ref[cur, pl.ds(a0, win), :] + rows.astype(BF16)

        all_small = functools.reduce(jnp.logical_and, [sp <= GATHER_WIN_SMALL for sp in spans])

        @pl.when(all_small)
        def _():
            for sb in range(nsub):
                gather(GATHER_WIN_SMALL, sb)

        @pl.when(jnp.logical_not(all_small))
        def _():
            for sb in range(nsub):
                nonempty = s0s[sb + 1] > s0s[sb]
                pl.when(nonempty & (spans[sb] <= GATHER_WIN_SMALL))(functools.partial(gather, GATHER_WIN_SMALL, sb))
                pl.when(spans[sb] > GATHER_WIN_SMALL)(functools.partial(gather, GATHER_WIN, sb))

    @pl.when(q > 0)
    def _():
        rows = min(TOK_BLK, rows_step)
        for c in range(rows_step // rows):
            off = pl.multiple_of(tb * rows_step + c * rows, rows)
            xc = xs_ref[1 - cur, pl.ds(off, rows), :]
            hg = _dot(xc, wg_ref[0, 0])
            hid = (hg * _sigmoid(hg) * _dot(xc, wu_ref[0, 0])).astype(BF16)
            ys_ref[0, 0, pl.ds(off, rows), :] = _dot(hid, wd_ref[0, 0]).astype(BF16)


def _moe(s0_flat, pos4, h2, lw, capp):
    b, t, _ = h2.shape
    layer = lw["layer"]
    nblk = t // TOK_BLK
    tok = min(MOE_TOK, t)
    nsub = tok // TOK_BLK
    nstep = t // tok
    rows_step = capp // nstep
    assert rows_step % min(TOK_BLK, rows_step) == 0 and rows_step % 16 == 0
    npair = N_EXPERTS * b
    gat = lambda q: jnp.minimum(q, npair - 1)
    ffn = lambda q: jnp.maximum(q - 1, 0)
    grid_spec = pltpu.PrefetchScalarGridSpec(
        num_scalar_prefetch=1, grid=(npair + 1, nstep),
        in_specs=[pl.BlockSpec((1, nsub, N_EXPERTS, TOK_BLK), lambda q, tb, s: (gat(q) % b, tb, 0, 0)),
                  pl.BlockSpec((1, tok, D), lambda q, tb, s: (gat(q) % b, tb, 0)),
                  pl.BlockSpec((1, 1, D, EXPERT_FF), lambda q, tb, s: (layer, ffn(q) // b, 0, 0)),
                  pl.BlockSpec((1, 1, D, EXPERT_FF), lambda q, tb, s: (layer, ffn(q) // b, 0, 0)),
                  pl.BlockSpec((1, 1, EXPERT_FF, D), lambda q, tb, s: (layer, ffn(q) // b, 0, 0))],
        out_specs=pl.BlockSpec((1, 1, capp, D), lambda q, tb, s: (ffn(q) % b, ffn(q) // b, 0, 0)),
        scratch_shapes=[pltpu.VMEM((2, capp + GATHER_WIN, D), BF16)])
    return pl.pallas_call(
        functools.partial(_moe_kernel, nbatch=b, nblk=nblk, nsub=nsub, rows_step=rows_step), name="moe",
        grid_spec=grid_spec, out_shape=jax.ShapeDtypeStruct((b, N_EXPERTS, capp, D), BF16),
        compiler_params=_cparams(("arbitrary", "arbitrary")),
    )(s0_flat, pos4, h2, lw["w_e_gate"], lw["w_e_up"], lw["w_e_down"])


def _combine_kernel(s0_ref, *refs, nblk, nb, sblk):
    ys_refs = refs[:2 * N_EXPERTS]
    pos_ref, aff_ref, xm_ref, mod_ref, g_ref, o_ref = refs[2 * N_EXPERTS:]
    bi, tb = pl.program_id(0), pl.program_id(1)
    pos = pos_ref[0]
    aff = aff_ref[0]
    lane = lax.broadcasted_iota(I32, (TOK_BLK, 2 * sblk), 1)
    acc = jnp.zeros((TOK_BLK, D), F32)
    for e in range(N_EXPERTS):
        s0 = s0_ref[(bi * N_EXPERTS + e) * (nblk + 1) + tb]
        blk0 = jnp.minimum(s0 // sblk, nb - 1)
        rel = pos[:, e:e + 1] - blk0 * sblk
        ysw = jnp.concatenate([ys_refs[2 * e][0, 0], ys_refs[2 * e + 1][0, 0]], axis=0)
        acc = acc + aff[:, e:e + 1] * _dot((lane == rel).astype(BF16), ysw)
    mod = mod_ref[0]
    o_ref[0] = xm_ref[0] + mod[5:6] * (_rms(acc) * g_ref[...])


def _combine(s0_flat, ys, pos_t, aff, xm, mod, g_post, sblk):
    b, t, _ = xm.shape
    nblk = t // TOK_BLK
    nb = ys.shape[2] // sblk

    def ys_spec(e, k):
        def ix(bi, tb, s):
            blk0 = jnp.minimum(s[(bi * N_EXPERTS + e) * (nblk + 1) + tb] // sblk, nb - 1)
            return (bi, e, jnp.minimum(blk0 + k, nb - 1), 0)
        return pl.BlockSpec((1, 1, sblk, D), ix)

    tok = lambda w: pl.BlockSpec((1, TOK_BLK, w), lambda bi, tb, s: (bi, tb, 0))
    in_specs = [ys_spec(e, k) for e in range(N_EXPERTS) for k in range(2)]
    in_specs += [tok(N_EXPERTS), tok(N_EXPERTS), tok(D), pl.BlockSpec((1, 8, D), lambda bi, tb, s: (bi, 0, 0)),
                 pl.BlockSpec((1, D), lambda bi, tb, s: (0, 0))]
    grid_spec = pltpu.PrefetchScalarGridSpec(num_scalar_prefetch=1, grid=(b, nblk), in_specs=in_specs, out_specs=tok(D))
    return pl.pallas_call(
        functools.partial(_combine_kernel, nblk=nblk, nb=nb, sblk=sblk), name="combine", grid_spec=grid_spec,
        out_shape=jax.ShapeDtypeStruct((b, t, D), F32),
        compiler_params=_cparams(("arbitrary", "arbitrary")),
    )(s0_flat, *([ys] * (2 * N_EXPERTS)), pos_t, aff, xm, mod, g_post)


def _rope_table(t):
    nf = ROPE_DIM // 4
    pos = jnp.arange(t)
    inv = ROPE_BASE ** (-jnp.arange(nf, dtype=F32) / nf)
    ang = jnp.stack([pos // GRID_W, pos % GRID_W], axis=-1).astype(F32)[..., None] * inv
    cos, sin = jnp.cos(ang), jnp.sin(ang)
    c32 = jnp.stack([cos, cos], axis=2).reshape(t, ROPE_DIM)
    s32 = jnp.stack([-sin, sin], axis=2).reshape(t, ROPE_DIM)
    one, zero = jnp.ones((t, 64), F32), jnp.zeros((t, 32), F32)
    ct = jnp.concatenate([one, c32, zero], axis=1)
    st = jnp.concatenate([0.0 * one, s32, zero], axis=1)
    return jnp.concatenate([ct, st, jnp.tile(c32, (1, 4)), jnp.tile(s32, (1, 4))], axis=1)


def _identity_table(t):
    one, zero = jnp.ones((t, 128), F32), jnp.zeros((t, 128), F32)
    ct = jnp.concatenate([jnp.ones((t, 96), F32), jnp.zeros((t, 32), F32)], axis=1)
    return jnp.concatenate([ct, zero, one, zero], axis=1)


def _layer_weights(i, p):
    lw = {}
    row = lambda a: a.reshape(1, -1)
    for name in ("g_mix_pre", "g_mix_post", "g_ffn_pre", "g_ffn_post", "g_q_lat", "g_kv_lat", "g_mlstm_out", "g_gla_out"):
        lw[name] = row(p[name][i])
    lw["w_ext"] = _gather_cols(p["w_in"][i], _WIN_IDX).astype(BF16)
    lw["wgt"] = p["w_in"][i][:, 416 + 1024:416 + 1040].T.astype(BF16)
    gb = jnp.concatenate([p["b_igate"][i].reshape(-1), p["b_fgate"][i].reshape(-1)])
    lw["gate_bias_row"] = jnp.pad(gb, (0, LANE - 16)).reshape(1, LANE)
    lw["gate_bias_col"] = gb.reshape(16, 1)
    wuq = p["w_uq"][i]
    qi = -np.ones((512,), np.int64)
    qsi = -np.ones((512,), np.int64)
    for h in range(N_HEADS):
        qi[128 * h:128 * h + 96] = 96 * h + np.arange(96)
        qsi[128 * h + 64:128 * h + 96] = 96 * h + 64 + (np.arange(32) ^ 8)
    lw["wq"] = _gather_cols(wuq, qi).astype(BF16)
    lw["wqs"] = _gather_cols(wuq, qsi).astype(BF16)
    ki = -np.ones((512,), np.int64)
    for h in range(N_HEADS):
        ki[128 * h:128 * h + 64] = 128 * h + np.arange(64)
    lw["wk"] = _gather_cols(p["w_ukv"][i], ki).astype(BF16)
    lw["wvt"] = _vt_rows(p["w_ukv"][i].reshape(MLA_KV_LORA, N_HEADS, 128)[:, :, 64:].reshape(MLA_KV_LORA, 256)).astype(BF16)
    lw["wdvt"] = _vt_rows(p["w_in"][i][:, 2768:3024]).astype(BF16)
    wa = p["w_alpha2"][i]
    wal = jnp.zeros((LANE, 256), F32).at[0:16, 0:128].set(wa[0]).at[16:32, 128:256].set(wa[1])
    lw["walpha"] = wal.astype(BF16)
    lw["balpha"] = p["b_alpha"][i].reshape(1, 256)
    lw["w_conv"] = p["w_conv"][i]
    lw["b_conv"] = row(p["b_conv"][i])
    lw["dlam"] = p["diff_lambda"][i]
    lw["g_diff"] = jnp.broadcast_to(p["g_diff_out"][i].reshape(N_HEADS, DIFF_DV, 1), (N_HEADS, DIFF_DV, LANE))
    wb = p["w_branch"][i]
    lw["wbr"] = jnp.concatenate([_pad_heads_rows(wb[0], 64), wb[1], wb[2], _pad_heads_rows(wb[3], 64)], axis=0).astype(BF16)
    lw["w_gate"] = p["w_gate"][i].astype(BF16)
    lw["b_gate"] = row(p["b_gate"][i])
    lw["w_out"] = p["w_out"][i].astype(BF16)
    lw["w_router"] = jnp.pad(p["w_router"][i], ((0, 0), (0, LANE - N_EXPERTS))).astype(BF16)
    lw["w_router_t"] = p["w_router"][i].T.astype(BF16)
    lw["layer"] = i
    lw["w_e_gate"], lw["w_e_up"], lw["w_e_down"] = p["w_e_gate"], p["w_e_up"], p["w_e_down"]
    return lw


def _ffn(xm, h2, aff, aff_t, mod, lw):
    b, t, _ = xm.shape
    nblk = t // TOK_BLK
    cap = EC_CAPACITY * t // N_EXPERTS
    capp = -(-cap // TOK_BLK) * TOK_BLK
    pos4, s04 = _topk(aff_t, cap)
    s0_be = jnp.concatenate([s04[..., 0].transpose(0, 2, 1), jnp.full((b, N_EXPERTS, 1), cap, I32)], axis=-1)
    s0_flat = s0_be.reshape(-1)
    ys = _moe(s0_flat, pos4, h2, lw, capp)
    pos_t = pos4.transpose(0, 1, 3, 2).reshape(b, t, N_EXPERTS)
    first = jnp.minimum(s0_be[..., :-1] // SLOT_BLK, capp // SLOT_BLK - 1)
    fits = jnp.all(s0_be[..., 1:] <= (first + 2) * SLOT_BLK)
    args = (s0_flat, ys, pos_t, aff, xm, mod, lw["g_ffn_post"])
    return lax.cond(fits, functools.partial(_combine, sblk=SLOT_BLK), functools.partial(_combine, sblk=TOK_BLK), *args)


def _hybrid_layer(i, x_c, x_l, c8, need_ctx, p):
    lw = _layer_weights(i, p)
    b, t, _ = x_l.shape
    tc = x_c.shape[1]
    lam_init = 0.8 - 0.6 * math.exp(-0.3 * i)
    mod8 = _ada(c8, p["w_ada"][i], p["b_ada"][i])
    pad = lambda m: jnp.pad(m.reshape(b, 6, D), ((0, 0), (0, 2), (0, 0)))
    mod_l = pad(mod8[:b])
    mod_c = pad(jnp.broadcast_to(mod8[b:b + 1], (b, 6 * D)))

    pc = _proj(x_c, mod_c, _identity_table(tc), lw)
    pt = _proj(x_l, mod_l, _rope_table(t), lw)

    zc = jnp.zeros((b, 2, 256, 512), F32)
    zm = jnp.full((b, 2, 8, 256), NEG, F32)
    zs = jnp.zeros((b, 2, 256, 128), F32)
    hf_c, hb_c, c_fin, m_fin = _mlstm(pc, lw["w_conv"], lw["b_conv"], zc, zm)
    hf_l, hb_l, _, _ = _mlstm(pt, lw["w_conv"], lw["b_conv"], c_fin, m_fin)
    gf_c, gb_c, s_fin = _gla(pc, zs)
    gf_l, gb_l, _ = _gla(pt, s_fin)

    one_g = jnp.ones((N_HEADS, MLA_V, LANE), F32)
    fl = functools.partial(_flash, tk=FLASH_KEYS)
    ya_l = fl(pt["mqt"], pc["mk"], pc["mvt"], pt["mk"], pt["mvt"], lw["dlam"], one_g, nmap=1, finish=False, post=1.0,
              tq=min(FLASH_ROWS, t))
    yd_l = fl(pt["dqt"], pc["dk"], pc["dvt"], pt["dk"], pt["dvt"], lw["dlam"], lw["g_diff"], nmap=2, finish=True,
              post=1.0 - lam_init, tq=min(FLASH_ROWS // 2, t))
    xm, h2, aff, aff_t = _merge(x_l, mod_l, ya_l, (hf_l, hb_l), pt["lo"], (gf_l, gb_l), pt["gr"], yd_l, lw, tm=256)
    x_l = _ffn(xm, h2, aff, aff_t, mod_l, lw)

    if need_ctx:
        ya_c = fl(pc["mqt"], pc["mk"], pc["mvt"], None, None, lw["dlam"], one_g, nmap=1, finish=False, post=1.0, tq=tc)
        yd_c = fl(pc["dqt"], pc["dk"], pc["dvt"], None, None, lw["dlam"], lw["g_diff"], nmap=2, finish=True,
                  post=1.0 - lam_init, tq=tc)
        xm, h2, aff, aff_t = _merge(x_c, mod_c, ya_c, (hf_c, hb_c), pc["lo"], (gf_c, gb_c), pc["gr"], yd_c, lw, tm=tc)
        x_c = _ffn(xm, h2, aff, aff_t, mod_c, lw)
    return x_c, x_l


def kernel(x, c, ctx, c_ctx, w_ada, b_ada, g_mix_pre, g_mix_post, g_ffn_pre, g_ffn_post, w_in, g_q_lat, w_uq, g_kv_lat, w_ukv, w_conv, b_conv, b_igate, b_fgate, g_mlstm_out, w_alpha2, b_alpha, g_gla_out, diff_lambda, g_diff_out, w_branch, w_gate, b_gate, w_out, w_router, w_e_gate, w_e_up, w_e_down):
    p = dict(w_ada=w_ada, b_ada=b_ada, g_mix_pre=g_mix_pre, g_mix_post=g_mix_post, g_ffn_pre=g_ffn_pre,
             g_ffn_post=g_ffn_post, w_in=w_in, g_q_lat=g_q_lat, w_uq=w_uq, g_kv_lat=g_kv_lat, w_ukv=w_ukv,
             w_conv=w_conv, b_conv=b_conv, b_igate=b_igate, b_fgate=b_fgate, g_mlstm_out=g_mlstm_out,
             w_alpha2=w_alpha2, b_alpha=b_alpha, g_gla_out=g_gla_out, diff_lambda=diff_lambda, g_diff_out=g_diff_out,
             w_branch=w_branch, w_gate=w_gate, b_gate=b_gate, w_out=w_out, w_router=w_router,
             w_e_gate=w_e_gate.astype(BF16), w_e_up=w_e_up.astype(BF16), w_e_down=w_e_down.astype(BF16))
    b = x.shape[0]
    c8 = jnp.concatenate([c, c_ctx[None], jnp.zeros((8 - b - 1, D), F32)], axis=0)
    x_c, x_l = ctx, x
    for i in range(DEPTH):
        x_c, x_l = _hybrid_layer(i, x_c, x_l, c8, i < DEPTH - 1, p)
    return x_l
```

```python
import functools
import math

import numpy as np
import jax
import jax.numpy as jnp
from jax import lax
from jax.experimental import pallas as pl
from jax.experimental.pallas import tpu as pltpu

F32 = jnp.float32
BF16 = jnp.bfloat16
I32 = jnp.int32

D = 1024
DEPTH = 2
GRID_W = 64
N_HEADS = 4
MLA_NOPE, MLA_ROPE, MLA_V = 64, 32, 64
MLA_Q_LORA, MLA_KV_LORA = 256, 128
ML_DH = 64
GLA_DK, GLA_DV, GLA_RANK, GLA_TAU = 32, 64, 16, 16.0
DIFF_DQK, DIFF_DV = 32, 64
ROPE_DIM, ROPE_BASE = 32, 10000.0
N_EXPERTS, EC_CAPACITY, EXPERT_FF = 16, 2, 1408
NEG = -1e30
EPS = 1e-6
LOG2E = 1.4426950408889634

LANE = 128
HEAD_SLAB = 128
TOK_BLK = 256
ML_CHUNK = 256
ML_BLOCK = 256
GLA_CHUNK = 64
GLA_BLOCK = 512
GATHER_WIN = TOK_BLK + 16
GATHER_WIN_SMALL = 64
MOE_TOK = 4096
SLOT_BLK = 128
VMEM_LIMIT = 56 * 1024 * 1024

ZQ, ZKV, ZKRA, ZKRB, ZMLQK, ZMLV, ZMLO, ZGATE, ZGA = 0, 256, 384, 512, 640, 1152, 1408, 1664, 1792
ZGQ, ZGK, ZGV, ZGR, ZDQ, ZDQS, ZDK, ZDKS, NZ = 1920, 2048, 2176, 2432, 2688, 2944, 3200, 3456, 3712
KV_CHUNK = 256
VT_ROWS = 80
FLASH_ROWS = 512
FLASH_KEYS = 512
FLASH_UNROLL = 16


def _swap32(c):
    return (c // 32) * 32 + ((c % 32) ^ 8)


def _win_index():
    idx = -np.ones((NZ,), np.int64)
    idx[ZQ:ZQ + 256] = np.arange(0, 256)
    idx[ZKV:ZKV + 128] = np.arange(256, 384)
    r = np.arange(32)
    idx[ZKRA + 64:ZKRA + 96] = 384 + r
    idx[ZKRB + 64:ZKRB + 96] = 384 + (r ^ 8)
    ml = 416
    idx[ZMLQK:ZMLQK + 512] = ml + np.arange(512)
    idx[ZMLV:ZMLV + 256] = ml + 512 + np.arange(256)
    idx[ZMLO:ZMLO + 256] = ml + 768 + np.arange(256)
    idx[ZGATE:ZGATE + 16] = ml + 1024 + np.arange(16)
    gl = 1456
    idx[ZGQ:ZGQ + 128] = gl + np.arange(128)
    idx[ZGK:ZGK + 128] = gl + 128 + np.arange(128)
    idx[ZGV:ZGV + 256] = gl + 256 + np.arange(256)
    idx[ZGR:ZGR + 256] = gl + 512 + np.arange(256)
    idx[ZGA:ZGA + 32] = gl + 768 + np.arange(32)
    df = 2256
    c = np.arange(256)
    idx[ZDQ:ZDQ + 256] = df + c
    idx[ZDQS:ZDQS + 256] = df + _swap32(c)
    idx[ZDK:ZDK + 256] = df + 256 + c
    idx[ZDKS:ZDKS + 256] = df + 256 + _swap32(c)
    return idx


def _vt_rows(w_cols):
    n = w_cols.shape[0]
    w4 = w_cols.T.reshape(N_HEADS, 64, n)
    return jnp.pad(w4, ((0, 0), (0, VT_ROWS - 64), (0, 0))).reshape(N_HEADS * VT_ROWS, n)


_WIN_IDX = _win_index()


def _gather_cols(w, idx):
    safe = np.maximum(idx, 0)
    return jnp.where(jnp.asarray(idx >= 0)[None, :], w[:, safe], 0.0)


def _pad_heads_rows(w, width):
    n = w.shape[1]
    w4 = w.reshape(N_HEADS, width, n)
    return jnp.pad(w4, ((0, 0), (0, HEAD_SLAB - width), (0, 0))).reshape(N_HEADS * HEAD_SLAB, n)


def _cparams(sem):
    return pltpu.CompilerParams(dimension_semantics=sem, vmem_limit_bytes=VMEM_LIMIT)


def _rms(x):
    return x * lax.rsqrt(jnp.mean(x * x, axis=-1, keepdims=True) + EPS)


def _sigmoid(x):
    return 0.5 * jnp.tanh(0.5 * x) + 0.5


def _log_sigmoid(x):
    return jnp.minimum(x, 0.0) - jnp.log1p(jnp.exp(-jnp.abs(x)))


def _dot(a, b, precision=None):
    return jnp.dot(a, b, preferred_element_type=F32, precision=precision)


def _dot_nt(a, b, precision=None):
    return lax.dot_general(a, b, (((1,), (1,)), ((), ())), preferred_element_type=F32, precision=precision)


def _dot_tn(a, b, precision=None):
    return lax.dot_general(a, b, (((0,), (0,)), ((), ())), preferred_element_type=F32, precision=precision)


def _split_bf16(x, parts):
    out, r = [], x
    for _ in range(parts):
        t = r.astype(BF16)
        out.append(t)
        r = r - t.astype(F32)
    return out


def _dot_sel(sel, x, parts=3):
    return sum(_dot(sel, t) for t in _split_bf16(x, parts))


def _const_spec(shape):
    nd = len(shape)
    return pl.BlockSpec(shape, lambda *_: (0,) * nd)


def _ada_kernel(c_ref, w_ref, b_ref, o_ref):
    cv = c_ref[...]
    s = (cv * _sigmoid(cv)).astype(BF16)
    o_ref[...] = _dot(s, w_ref[...].astype(BF16)) + b_ref[...]


def _ada(c8, w_ada, b_ada):
    n, tn = 6 * D, 1024
    return pl.pallas_call(
        _ada_kernel, name="ada", grid=(n // tn,),
        in_specs=[pl.BlockSpec((8, D), lambda j: (0, 0)), pl.BlockSpec((D, tn), lambda j: (0, j)),
                  pl.BlockSpec((1, tn), lambda j: (0, j))],
        out_specs=pl.BlockSpec((8, tn), lambda j: (0, j)),
        out_shape=jax.ShapeDtypeStruct((8, n), F32), compiler_params=_cparams(("arbitrary",)),
    )(c8, w_ada, b_ada.reshape(1, n))


_PROJ_OUT = (
    ("mk", 512, BF16),
    ("lqk", 512, F32), ("lv", 256, BF16), ("lo", 256, BF16), ("gc", 128, F32),
    ("gq", 128, BF16), ("gk", 128, BF16), ("gv", 256, BF16), ("gr", 256, BF16), ("glg", 256, F32),
    ("dk", 512, BF16),
)


def _proj_kernel(x_ref, mod_ref, g_ref, w_ref, tab_ref, gq_ref, wq_ref, wqs_ref, gkv_ref, wk_ref, wvt_ref,
                 wgt_ref, gbr_ref, gbc_ref, wal_ref, bal_ref, wdvt_ref,
                 mk_ref, lqk_ref, lv_ref, lo_ref, gc_ref, gq_o, gk_o, gv_o, gr_o, glg_o,
                 dk_ref, grow_ref, mvt_ref, dvt_ref, mqt_ref, dqt_ref):
    x = x_ref[0]
    tm = x.shape[0]
    mod = mod_ref[0]
    h = _rms(x) * g_ref[...] * (1.0 + mod[1:2]) + mod[0:1]
    hb = h.astype(BF16)
    z = _dot(hb, w_ref[...])
    tab = tab_ref[...]
    ct, st, cd, sd = tab[:, 0:128], tab[:, 128:256], tab[:, 256:384], tab[:, 384:512]
    lane = lax.broadcasted_iota(I32, (tm, LANE), 1)

    qn = (_rms(z[:, ZQ:ZQ + 256]) * gq_ref[...]).astype(BF16)
    qa = _dot(qn, wq_ref[...])
    qb = _dot(qn, wqs_ref[...])
    qscale = (MLA_NOPE + MLA_ROPE) ** -0.5 * LOG2E
    for hh in range(N_HEADS):
        sl = slice(HEAD_SLAB * hh, HEAD_SLAB * (hh + 1))
        mqt_ref[0, hh] = ((qa[:, sl] * ct + qb[:, sl] * st) * qscale).T.astype(BF16)
    kvn = (_rms(z[:, ZKV:ZKV + 128]) * gkv_ref[...]).astype(BF16)
    kk = _dot(kvn, wk_ref[...])
    kr = z[:, ZKRA:ZKRA + 128] * ct + z[:, ZKRB:ZKRB + 128] * st
    for hh in range(N_HEADS):
        sl = slice(HEAD_SLAB * hh, HEAD_SLAB * (hh + 1))
        mk_ref[0, :, sl] = (kk[:, sl] + kr).astype(BF16)
    ones_row = lax.broadcasted_iota(I32, (N_HEADS * VT_ROWS, tm), 0) % VT_ROWS == MLA_V
    mvt_ref[0, 0] = jnp.where(ones_row, 1.0, _dot_nt(wvt_ref[...], kvn)).astype(BF16)
    dvt_ref[0, 0] = jnp.where(ones_row, 1.0, _dot_nt(wdvt_ref[...], hb)).astype(BF16)

    lqk_ref[0] = z[:, ZMLQK:ZMLQK + 512]
    lv_ref[0] = z[:, ZMLV:ZMLV + 256].astype(BF16)
    lo_ref[0] = z[:, ZMLO:ZMLO + 256].astype(BF16)
    gcol = z[:, ZGATE:ZGATE + 128] + gbr_ref[...]
    gc_ref[0] = jnp.where(lane < 8, gcol, jnp.where(lane < 16, _log_sigmoid(gcol), 0.0))
    zr = _dot_nt(wgt_ref[...], hb) + gbc_ref[...]
    rowi = lax.broadcasted_iota(I32, zr.shape, 0)
    grow_ref[0] = jnp.where(rowi < 8, zr, _log_sigmoid(zr))

    gq_o[0] = (z[:, ZGQ:ZGQ + 128] * GLA_DK ** -0.5).astype(BF16)
    gk_o[0] = z[:, ZGK:ZGK + 128].astype(BF16)
    gv_o[0] = z[:, ZGV:ZGV + 256].astype(BF16)
    gr_o[0] = z[:, ZGR:ZGR + 256].astype(BF16)
    zg = _dot(z[:, ZGA:ZGA + 128].astype(BF16), wal_ref[...]) + bal_ref[...]
    glg_o[0] = _log_sigmoid(zg) * (1.0 / GLA_TAU)

    dscale = DIFF_DQK ** -0.5 * LOG2E
    for g in range(2):
        gs = slice(128 * g, 128 * (g + 1))
        qg = (z[:, ZDQ:ZDQ + 256][:, gs] * cd + z[:, ZDQS:ZDQS + 256][:, gs] * sd) * dscale
        kg = z[:, ZDK:ZDK + 256][:, gs] * cd + z[:, ZDKS:ZDKS + 256][:, gs] * sd
        for hl in range(2):
            hh = 2 * g + hl
            for m in range(2):
                lo = 64 * hl + 32 * m
                dqt_ref[0, 2 * hh + m] = jnp.where((lane >= lo) & (lane < lo + 32), qg, 0.0).T.astype(BF16)
            dk_ref[0, :, HEAD_SLAB * hh:HEAD_SLAB * (hh + 1)] = jnp.where(
                (lane >= 64 * hl) & (lane < 64 * hl + 64), kg, 0.0).astype(BF16)


def _proj(x, mod, tab, lw):
    b, t, _ = x.shape
    tm = KV_CHUNK
    consts = [lw["g_mix_pre"], lw["w_ext"], None, lw["g_q_lat"], lw["wq"], lw["wqs"], lw["g_kv_lat"], lw["wk"], lw["wvt"],
              lw["wgt"], lw["gate_bias_row"], lw["gate_bias_col"], lw["walpha"], lw["balpha"], lw["wdvt"]]
    in_specs = [pl.BlockSpec((1, tm, D), lambda bi, i: (bi, i, 0)), pl.BlockSpec((1, 8, D), lambda bi, i: (bi, 0, 0))]
    args = [x, mod]
    for cst in consts:
        if cst is None:
            in_specs.append(pl.BlockSpec((tm, 512), lambda bi, i: (i, 0)))
            args.append(tab)
        else:
            in_specs.append(_const_spec(cst.shape))
            args.append(cst)
    out_specs = [pl.BlockSpec((1, tm, w), lambda bi, i: (bi, i, 0)) for _, w, _ in _PROJ_OUT]
    out_shape = [jax.ShapeDtypeStruct((b, t, w), dt) for _, w, dt in _PROJ_OUT]
    out_specs.append(pl.BlockSpec((1, 16, tm), lambda bi, i: (bi, 0, i)))
    out_shape.append(jax.ShapeDtypeStruct((b, 16, t), F32))
    for _ in range(2):
        out_specs.append(pl.BlockSpec((1, 1, N_HEADS * VT_ROWS, tm), lambda bi, i: (bi, i, 0, 0)))
        out_shape.append(jax.ShapeDtypeStruct((b, t // tm, N_HEADS * VT_ROWS, tm), BF16))
    for nslab in (N_HEADS, 2 * N_HEADS):
        out_specs.append(pl.BlockSpec((1, nslab, HEAD_SLAB, tm), lambda bi, i: (bi, 0, 0, i)))
        out_shape.append(jax.ShapeDtypeStruct((b, nslab, HEAD_SLAB, t), BF16))
    outs = pl.pallas_call(
        _proj_kernel, name="proj", grid=(b, t // tm), in_specs=in_specs, out_specs=out_specs, out_shape=out_shape,
        compiler_params=_cparams(("parallel", "arbitrary")),
    )(*args)
    res = {name: o for (name, _, _), o in zip(_PROJ_OUT, outs[:-5])}
    res["grow"], res["mvt"], res["dvt"], res["mqt"], res["dqt"] = outs[-5:]
    return res


def _flash_kernel(*refs, nmap, has_lat, tk, finish, post):
    if has_lat:
        q_ref, kc_ref, vc_ref, kl_ref, vl_ref, dl_ref, g_ref, o_ref, s_ref, acc_ref = refs
    else:
        q_ref, kc_ref, vc_ref, dl_ref, g_ref, o_ref = refs
    tq = q_ref.shape[3]
    qt = q_ref[0, 0] if nmap == 1 else jnp.concatenate([q_ref[0, mm] for mm in range(nmap)], axis=1)
    rows = nmap * tq
    sub = tk // KV_CHUNK

    def softmax(s, smax, m):
        m_new = jnp.maximum(m, smax)
        return m_new, jnp.exp2(m - m_new), jnp.exp2(s - m_new).astype(BF16)

    def pv(p, vts):
        return _dot(vts[0] if len(vts) == 1 else jnp.concatenate(vts, axis=1), p)

    s_ctx = _dot(kc_ref[0], qt)
    if has_lat:
        n = kl_ref.shape[1] // tk
        unroll = min(FLASH_UNROLL, n)

        def scores(j):
            if isinstance(j, int):
                return _dot(kl_ref[0, j * tk:(j + 1) * tk, :], qt)
            off = pl.multiple_of(j * tk, tk)
            return _dot(kl_ref[0, pl.ds(off, tk), :], qt)

        def values(j):
            return [vl_ref[0, j * sub + c] for c in range(sub)]

        def produce(slot, j):
            s = scores(j)
            s_ref[slot] = s
            return jnp.max(s, axis=0, keepdims=True)

        smax0 = produce(0, 0)

    m, _, p = softmax(s_ctx, jnp.max(s_ctx, axis=0, keepdims=True), jnp.full((1, rows), NEG, F32))
    acc = pv(p, [vc_ref[0, 0]])
    if has_lat:
        acc_ref[...] = acc

        def body(jj, carry):
            m, smax = carry
            j = unroll * jj
            for u in range(unroll):
                smax_next = smax
                if not isinstance(j, int):
                    smax_next = produce((u + 1) % 2, jnp.minimum(j + u + 1, n - 1))
                elif j + u + 1 < n:
                    smax_next = produce((u + 1) % 2, j + u + 1)
                m, alpha, p = softmax(s_ref[u % 2], smax, m)
                acc_ref[...] = alpha * acc_ref[...] + pv(p, values(j + u))
                smax = smax_next
            return m, smax

        if unroll == n:
            body(0, (m, smax0))
        else:
            lax.fori_loop(0, n // unroll, body, (m, smax0))
        acc = acc_ref[...]

    o = acc[0:MLA_V, :] / acc[MLA_V:MLA_V + 1, :]
    if nmap == 2:
        lv = dl_ref[...]
        lam = (jnp.exp(jnp.sum(lv[0:1] * lv[1:2], axis=-1, keepdims=True))
               - jnp.exp(jnp.sum(lv[2:3] * lv[3:4], axis=-1, keepdims=True)) + (1.0 - post))
        o = o[:, :tq] - lam * o[:, tq:]
    if finish:
        ms = jnp.mean(o * o, axis=0, keepdims=True)
        o = o * lax.rsqrt(ms + EPS) * jnp.concatenate([g_ref[0]] * (tq // LANE), axis=1) * post
    o_pad = jnp.concatenate([o, jnp.zeros((HEAD_SLAB - MLA_V, tq), F32)], axis=0)
    o_ref[0] = o_pad.T.astype(BF16)


def _flash(q, kc, vct, kl, vlt, dlam, g_out, *, nmap, finish, post, tq, tk):
    b, _, _, t = q.shape
    has_lat = kl is not None
    assert kc.shape[1] == KV_CHUNK and tk % KV_CHUNK == 0
    kspec = lambda n: pl.BlockSpec((1, n, HEAD_SLAB), lambda bi, h, i: (bi, 0, h))
    vspec = lambda n: pl.BlockSpec((1, n // KV_CHUNK, VT_ROWS, KV_CHUNK), lambda bi, h, i: (bi, 0, h, 0))
    in_specs = [pl.BlockSpec((1, nmap, HEAD_SLAB, tq), lambda bi, h, i: (bi, h, 0, i)), kspec(KV_CHUNK), vspec(KV_CHUNK)]
    args = [q, kc, vct]
    scratch = []
    if has_lat:
        tl = kl.shape[1]
        assert (tl // tk) % min(FLASH_UNROLL, tl // tk) == 0
        in_specs += [kspec(tl), vspec(tl)]
        args += [kl, vlt]
        scratch = [pltpu.VMEM((2, tk, nmap * tq), F32), pltpu.VMEM((VT_ROWS, nmap * tq), F32)]
    in_specs += [_const_spec(dlam.shape), pl.BlockSpec((1, MLA_V, LANE), lambda bi, h, i: (h, 0, 0))]
    args += [dlam, g_out]
    return pl.pallas_call(
        functools.partial(_flash_kernel, nmap=nmap, has_lat=has_lat, tk=tk, finish=finish, post=post),
        name="flash_diff" if nmap == 2 else "flash_mla",
        grid=(b, N_HEADS, t // tq), in_specs=in_specs,
        out_specs=pl.BlockSpec((1, tq, HEAD_SLAB), lambda bi, h, i: (bi, i, h)),
        out_shape=jax.ShapeDtypeStruct((b, t, N_HEADS * HEAD_SLAB), BF16),
        scratch_shapes=scratch,
        compiler_params=_cparams(("parallel", "parallel", "arbitrary")),
    )(*args)


def _head_of(shape, axis, width):
    return (lax.broadcasted_iota(I32, shape, axis) % (N_HEADS * width)) // width


def _mlstm_conv(first, last, x, xprev, xnext, wc, bcv):
    n = x.shape[0]
    row = lax.broadcasted_iota(I32, x.shape, 0)
    pr = jnp.where(first, 0.0, xprev[7:8, :])
    nx = jnp.where(last, 0.0, xnext[0:1, :])
    xm = jnp.where(row == 0, pr, pltpu.roll(x, 1, 0))
    xp = jnp.where(row == n - 1, nx, pltpu.roll(x, n - 1, 0))
    y = xm * wc[0:1] + x * wc[1:2] + xp * wc[2:3] + bcv
    qk = y * _sigmoid(y)
    return qk[:, :256], qk[:, 256:] * ML_DH ** -0.5


def _mlstm_dir(d, q, k, v, gcol, grow, cb, m0e):
    L = q.shape[0]
    li = lax.broadcasted_iota(I32, (L, L), 0)
    si = lax.broadcasted_iota(I32, (L, L), 1)
    tin = (si <= li) if d == 0 else (si >= li)
    tinb = tin.astype(BF16)
    bcol = _dot_sel(tinb, gcol)
    brow = sum(_dot_nt(t, tinb) for t in _split_bf16(grow, 3))
    hm256 = _head_of((L, 256), 1, ML_DH)
    hm512 = _head_of((L, 512), 1, ML_DH)
    e_idx = L - 1 if d == 0 else 0

    d_blk, inter_blk = [], []
    for hh in range(N_HEADS):
        c = 4 * d + hh
        bc = bcol[:, 8 + c:9 + c]
        d_blk.append(jnp.where(tin, bc - brow[8 + c:9 + c, :] + grow[c:c + 1, :], NEG))
        inter_blk.append(bc + m0e[0:1, 64 * hh:64 * hh + 1])
    d_st = jnp.concatenate(d_blk, axis=0)
    inter_st = jnp.concatenate(inter_blk, axis=0)
    mt = jnp.maximum(inter_st, jnp.max(d_st, axis=-1, keepdims=True))
    q_st = jnp.concatenate([jnp.where(hm256 == hh, q, 0.0) for hh in range(N_HEADS)], axis=0).astype(BF16)
    s_st = (jnp.exp(d_st - mt) * _dot_nt(q_st, k.astype(BF16))).astype(BF16)
    vext = jnp.concatenate([v, jnp.ones((L, 256), BF16)], axis=1)
    r = _dot(s_st, vext)
    aint = jnp.exp(inter_st - mt)
    p = _dot(q.astype(BF16), cb.astype(BF16))
    tot = jnp.zeros((L, 512), F32)
    mte = jnp.zeros((L, 256), F32)
    for hh in range(N_HEADS):
        rs = slice(hh * L, (hh + 1) * L)
        tot = jnp.where(hm512 == hh, r[rs] + aint[rs] * p, tot)
        mte = jnp.where(hm256 == hh, mt[rs], mte)
    hout = tot[:, :256] / jnp.maximum(jnp.abs(tot[:, 256:]), jnp.exp(-mte))

    wexp = jnp.zeros((L, 256), F32)
    arow = jnp.zeros((1, 512), F32)
    grw = jnp.zeros((1, 512), F32)
    mnew = jnp.zeros((1, 256), F32)
    hr512 = _head_of((1, 512), 1, ML_DH)
    hr256 = _head_of((1, 256), 1, ML_DH)
    for hh in range(N_HEADS):
        c = 4 * d + hh
        bc = bcol[:, 8 + c:9 + c]
        be = bc[e_idx:e_idx + 1, :]
        wl = be - bc + gcol[:, c:c + 1]
        mloc = jnp.max(wl, axis=0, keepdims=True)
        m0h = m0e[0:1, 64 * hh:64 * hh + 1]
        mn = jnp.maximum(be + m0h, mloc)
        wexp = jnp.where(hm256 == hh, jnp.exp(wl - mloc), wexp)
        arow = jnp.where(hr512 == hh, jnp.exp(be + m0h - mn), arow)
        grw = jnp.where(hr512 == hh, jnp.exp(mloc - mn), grw)
        mnew = jnp.where(hr256 == hh, mn, mnew)
    cl = _dot_tn((k * wexp).astype(BF16), vext)
    bd = lax.broadcasted_iota(I32, (256, 512), 0) // ML_DH == _head_of((256, 512), 1, ML_DH)
    return hout, arow * cb + jnp.where(bd, grw * cl, 0.0), jnp.broadcast_to(mnew, (8, 256))


def _mlstm_kernel(xf, xfp, xfn, xb, xbp, xbn, vf, vb, gcf, gcb, grf, grb, wc_ref, bc_ref, c0_ref, m0_ref,
                  hf_ref, hb_ref, c_ref, m_ref):
    i = pl.program_id(1)
    n = pl.num_programs(1)

    @pl.when(i == 0)
    def _():
        c_ref[...] = c0_ref[...]
        m_ref[...] = m0_ref[...]

    wc = wc_ref[...]
    bcv = bc_ref[...]
    L = ML_CHUNK
    nsub = xf.shape[1] // L
    streams = ((0, i == 0, i == n - 1, xf, xfp, xfn, vf, gcf, grf, hf_ref),
               (1, i == n - 1, i == 0, xb, xbp, xbn, vb, gcb, grb, hb_ref))
    for d, first, last, x, xp, xn, v, gc, gr, h_ref in streams:
        q, k = _mlstm_conv(first, last, x[0], xp[0], xn[0], wc, bcv)
        cb, m0e = c_ref[0, d], m_ref[0, d]
        for c in (range(nsub) if d == 0 else reversed(range(nsub))):
            sl = slice(c * L, (c + 1) * L)
            h_ref[0, sl, :], cb, m0e = _mlstm_dir(d, q[sl], k[sl], v[0, sl, :], gc[0, sl, :], gr[0, :, sl], cb, m0e)
        c_ref[0, d], m_ref[0, d] = cb, m0e


def _mlstm(pr, w_conv, b_conv, c0, m0):
    x, v, gc, gr = pr["lqk"], pr["lv"], pr["gc"], pr["grow"]
    b, t, _ = x.shape
    L = min(ML_BLOCK, t)
    n = t // L
    r8 = L // 8
    last8 = t // 8 - 1

    def fw(bi, i):
        return (bi, i, 0)

    def bw(bi, i):
        return (bi, n - 1 - i, 0)

    def halo(ix, shift):
        def f(bi, i):
            blk = ix(bi, i)[1]
            return (bi, jnp.clip(blk * r8 + shift, 0, last8), 0)
        return f

    main = lambda w, ix: pl.BlockSpec((1, L, w), ix)
    in_specs = [main(512, fw), pl.BlockSpec((1, 8, 512), halo(fw, -1)), pl.BlockSpec((1, 8, 512), halo(fw, r8)),
                main(512, bw), pl.BlockSpec((1, 8, 512), halo(bw, -1)), pl.BlockSpec((1, 8, 512), halo(bw, r8)),
                main(256, fw), main(256, bw), main(128, fw), main(128, bw),
                pl.BlockSpec((1, 16, L), lambda bi, i: (bi, 0, i)), pl.BlockSpec((1, 16, L), lambda bi, i: (bi, 0, n - 1 - i)),
                _const_spec(w_conv.shape), _const_spec(b_conv.shape),
                pl.BlockSpec((1, 2, 256, 512), lambda bi, i: (bi, 0, 0, 0)), pl.BlockSpec((1, 2, 8, 256), lambda bi, i: (bi, 0, 0, 0))]
    out_specs = [main(256, fw), main(256, bw),
                 pl.BlockSpec((1, 2, 256, 512), lambda bi, i: (bi, 0, 0, 0)), pl.BlockSpec((1, 2, 8, 256), lambda bi, i: (bi, 0, 0, 0))]
    out_shape = [jax.ShapeDtypeStruct((b, t, 256), F32), jax.ShapeDtypeStruct((b, t, 256), F32),
                 jax.ShapeDtypeStruct(c0.shape, F32), jax.ShapeDtypeStruct(m0.shape, F32)]
    return pl.pallas_call(
        _mlstm_kernel, name="mlstm", grid=(b, n), in_specs=in_specs, out_specs=out_specs, out_shape=out_shape,
        compiler_params=_cparams(("parallel", "arbitrary")),
    )(x, x, x, x, x, x, v, v, gc, gc, gr, gr, w_conv, b_conv, c0, m0)


def _gla_chunk(d, q, k, v, lg, sb):
    L = q.shape[0]
    li = lax.broadcasted_iota(I32, (L, L), 0)
    si = lax.broadcasted_iota(I32, (L, L), 1)
    tin = (si <= li) if d == 0 else (si >= li)
    lgd = lg[:, 128 * d:128 * (d + 1)]
    gcum = _dot_sel(tin.astype(BF16), lgd)
    e_idx = L - 1 if d == 0 else 0
    gend = gcum[e_idx:e_idx + 1, :]
    qf, kf = q.astype(F32), k.astype(F32)
    q_dec = qf * jnp.exp(gcum)
    k_dec = (kf * jnp.exp(-gcum)).astype(BF16)
    k_end = (kf * jnp.exp(gend - gcum)).astype(BF16)
    hm128 = _head_of((L, 128), 1, GLA_DK)
    hm256 = _head_of((L, 256), 1, GLA_DV)
    q_st = jnp.concatenate([jnp.where(hm128 == hh, q_dec, 0.0) for hh in range(N_HEADS)], axis=0).astype(BF16)
    att = _dot_nt(q_st, k_dec)
    tin4 = jnp.concatenate([tin] * N_HEADS, axis=0)
    o_st = _dot(jnp.where(tin4, att, 0.0).astype(BF16), v)
    o = _dot_nt(q_dec.astype(BF16), sb.astype(BF16))
    for hh in range(N_HEADS):
        o = o + jnp.where(hm256 == hh, o_st[hh * L:(hh + 1) * L], 0.0)
    bd = lax.broadcasted_iota(I32, (256, 128), 0) // GLA_DV == _head_of((256, 128), 1, GLA_DK)
    return o, jnp.exp(gend) * sb + jnp.where(bd, _dot_tn(v, k_end), 0.0)


def _gla_kernel(qf, kf, vf, lf, qb, kb, vb, lb, s0_ref, of_ref, ob_ref, s_ref):
    i = pl.program_id(1)

    @pl.when(i == 0)
    def _():
        s_ref[...] = s0_ref[...]

    L = GLA_CHUNK
    nsub = qf.shape[1] // L
    for d, (q, k, v, lg, o_ref) in enumerate(((qf, kf, vf, lf, of_ref), (qb, kb, vb, lb, ob_ref))):
        sb = s_ref[0, d]
        for c in (range(nsub) if d == 0 else reversed(range(nsub))):
            sl = slice(c * L, (c + 1) * L)
            o_ref[0, sl, :], sb = _gla_chunk(d, q[0, sl, :], k[0, sl, :], v[0, sl, :], lg[0, sl, :], sb)
        s_ref[0, d] = sb


def _gla(pr, s0):
    q, k, v, lg = pr["gq"], pr["gk"], pr["gv"], pr["glg"]
    b, t, _ = q.shape
    L = min(GLA_BLOCK, t)
    n = t // L
    fw = lambda bi, i: (bi, i, 0)
    bw = lambda bi, i: (bi, n - 1 - i, 0)
    blk = lambda w, ix: pl.BlockSpec((1, L, w), ix)
    st_spec = pl.BlockSpec((1, 2, 256, 128), lambda bi, i: (bi, 0, 0, 0))
    return pl.pallas_call(
        _gla_kernel, name="gla", grid=(b, n),
        in_specs=[blk(128, fw), blk(128, fw), blk(256, fw), blk(256, fw),
                  blk(128, bw), blk(128, bw), blk(256, bw), blk(256, bw), st_spec],
        out_specs=[blk(256, fw), blk(256, bw), st_spec],
        out_shape=[jax.ShapeDtypeStruct((b, t, 256), F32), jax.ShapeDtypeStruct((b, t, 256), F32),
                   jax.ShapeDtypeStruct(s0.shape, F32)],
        compiler_params=_cparams(("parallel", "arbitrary")),
    )(q, k, v, lg, q, k, v, lg, s0)


def _head_rms_expanded(x, width):
    n = x.shape[1]
    bd = (lax.broadcasted_iota(I32, (n, n), 0) // width == lax.broadcasted_iota(I32, (n, n), 1) // width).astype(BF16)
    return sum(_dot(t, bd) for t in _split_bf16(x * x, 2)) * (1.0 / width)


def _merge_kernel(x_ref, mod_ref, ya_ref, hf_ref, hb_ref, lo_ref, gf_ref, gb_ref, gr_ref, yd_ref,
                  gpre_ref, wg_ref, bg_ref, wbr_ref, wo_ref, gpost_ref, gffn_ref, wr_ref, wrt_ref, gml_ref, ggla_ref,
                  xm_ref, h2_ref, aff_ref, afft_ref):
    x = x_ref[0]
    tm = x.shape[0]
    mod = mod_ref[0]
    hb = (_rms(x) * gpre_ref[...] * (1.0 + mod[1:2]) + mod[0:1]).astype(BF16)

    hs = hf_ref[0] + hb_ref[0]
    y_ml = _sigmoid(lo_ref[0].astype(F32)) * (hs * lax.rsqrt(_head_rms_expanded(hs, ML_DH) + EPS) * gml_ref[...])
    gs = gf_ref[0] + gb_ref[0]
    rr = gr_ref[0].astype(F32)
    y_gla = rr * _sigmoid(rr) * (gs * lax.rsqrt(_head_rms_expanded(gs, GLA_DV) + EPS) * ggla_ref[...])

    branches = ((ya_ref[0], 0, 512), (y_ml.astype(BF16), 512, 256), (y_gla.astype(BF16), 768, 256), (yd_ref[0], 1024, 512))
    mix = jnp.zeros((tm, D), F32)
    for nb, (yb, r0, rw) in enumerate(branches):
        gate = _sigmoid(_dot(hb, wg_ref[:, nb * D:(nb + 1) * D]) + bg_ref[:, nb * D:(nb + 1) * D])
        mix = mix + gate * _dot(yb, wbr_ref[r0:r0 + rw, :])
    y = _dot(mix.astype(BF16), wo_ref[...])
    xm = x + mod[2:3] * (_rms(y) * gpost_ref[...])
    xm_ref[0] = xm

    h2 = (_rms(xm) * gffn_ref[...] * (1.0 + mod[4:5]) + mod[3:4]).astype(BF16)
    h2_ref[0] = h2
    lane = lax.broadcasted_iota(I32, (tm, LANE), 1)
    lg = jnp.where(lane < N_EXPERTS, _dot(h2, wr_ref[...]), NEG)
    e = jnp.exp(lg - jnp.max(lg, axis=-1, keepdims=True))
    aff_ref[0] = (e / jnp.sum(e, axis=-1, keepdims=True))[:, :N_EXPERTS]
    lt = _dot_nt(wrt_ref[...], h2)
    et = jnp.exp(lt - jnp.max(lt, axis=0, keepdims=True))
    afft_ref[0] = et / jnp.sum(et, axis=0, keepdims=True)


def _merge(x, mod, ya, ml, lo, gl, gr, yd, lw, tm):
    b, t, _ = x.shape
    tok = lambda w: pl.BlockSpec((1, tm, w), lambda bi, i: (bi, i, 0))
    consts = [lw["g_mix_pre"], lw["w_gate"], lw["b_gate"], lw["wbr"], lw["w_out"], lw["g_mix_post"], lw["g_ffn_pre"],
              lw["w_router"], lw["w_router_t"], lw["g_mlstm_out"], lw["g_gla_out"]]
    in_specs = [tok(D), pl.BlockSpec((1, 8, D), lambda bi, i: (bi, 0, 0)), tok(512), tok(256), tok(256), tok(256),
                tok(256), tok(256), tok(256), tok(512)] + [_const_spec(c.shape) for c in consts]
    out_specs = [tok(D), tok(D), tok(N_EXPERTS), pl.BlockSpec((1, N_EXPERTS, tm), lambda bi, i: (bi, 0, i))]
    out_shape = [jax.ShapeDtypeStruct((b, t, D), F32), jax.ShapeDtypeStruct((b, t, D), BF16),
                 jax.ShapeDtypeStruct((b, t, N_EXPERTS), F32), jax.ShapeDtypeStruct((b, N_EXPERTS, t), F32)]
    return pl.pallas_call(
        _merge_kernel, name="merge", grid=(b, t // tm), in_specs=in_specs, out_specs=out_specs, out_shape=out_shape,
        compiler_params=_cparams(("parallel", "arbitrary")),
    )(x, mod, ya, ml[0], ml[1], lo, gl[0], gl[1], gr, yd, *consts)


def _topk_kernel(a_ref, pos_ref, s0_ref, *, cap):
    nblk = a_ref.shape[1]
    bits = pltpu.bitcast(a_ref[0], I32)

    def bisect(i, thr):
        cand = thr | (1 << (30 - i))
        cnt = jnp.sum((bits >= cand).astype(I32), axis=(0, 2), keepdims=True)
        return jnp.where(cnt >= cap, cand, thr)

    thr3 = lax.fori_loop(0, 31, bisect, jnp.zeros((1, N_EXPERTS, 1), I32))
    need3 = cap - jnp.sum((bits > thr3).astype(I32), axis=(0, 2), keepdims=True)
    thr, need = thr3[0], need3[0].astype(F32)
    upper = (lax.broadcasted_iota(I32, (TOK_BLK, TOK_BLK), 0) <= lax.broadcasted_iota(I32, (TOK_BLK, TOK_BLK), 1)).astype(BF16)

    def blk(j, carry):
        c_eq, c_sel = carry
        bj = pltpu.bitcast(a_ref[0, j], I32)
        gt, eq = bj > thr, bj == thr
        cum_eq = _dot(eq.astype(BF16), upper) + c_eq
        sel = gt | (eq & (cum_eq <= need))
        cum_sel = _dot(sel.astype(BF16), upper) + c_sel
        pos_ref[0, j] = jnp.where(sel, cum_sel - 1.0, -1.0).astype(I32)
        s0_ref[0, j] = jnp.broadcast_to(c_sel, (N_EXPERTS, LANE)).astype(I32)
        return cum_eq[:, TOK_BLK - 1:TOK_BLK], cum_sel[:, TOK_BLK - 1:TOK_BLK]

    zero = jnp.zeros((N_EXPERTS, 1), F32)
    lax.fori_loop(0, nblk, blk, (zero, zero))


def _topk(aff_t, cap):
    b, _, t = aff_t.shape
    nblk = t // TOK_BLK
    a4 = aff_t.reshape(b, N_EXPERTS, nblk, TOK_BLK).transpose(0, 2, 1, 3)
    spec = lambda w: pl.BlockSpec((1, nblk, N_EXPERTS, w), lambda bi: (bi, 0, 0, 0))
    return pl.pallas_call(
        functools.partial(_topk_kernel, cap=cap), name="topk", grid=(b,),
        in_specs=[spec(TOK_BLK)], out_specs=[spec(TOK_BLK), spec(LANE)],
        out_shape=[jax.ShapeDtypeStruct((b, nblk, N_EXPERTS, TOK_BLK), I32),
                   jax.ShapeDtypeStruct((b, nblk, N_EXPERTS, LANE), I32)],
        compiler_params=_cparams(("parallel",)),
    )(a4)


def _moe_kernel(s0_ref, pos_ref, h_ref, wg_ref, wu_ref, wd_ref, ys_ref, xs_ref, *, nbatch, nblk, nsub, rows_step):
    q, tb = pl.program_id(0), pl.program_id(1)
    cur = q % 2

    @pl.when(q < N_EXPERTS * nbatch)
    def _():
        e, bi = q // nbatch, q % nbatch

        @pl.when(tb == 0)
        def _():
            xs_ref[cur] = jnp.zeros(xs_ref.shape[1:], BF16)

        base = (bi * N_EXPERTS + e) * (nblk + 1) + tb * nsub
        s0s = [s0_ref[base + sb] for sb in range(nsub + 1)]
        a0s = [pl.multiple_of((s0 // 16) * 16, 16) for s0 in s0s[:-1]]
        spans = [s0s[sb + 1] - a0s[sb] for sb in range(nsub)]

        def gather(win, sb):
            a0 = a0s[sb]
            prow = pos_ref[0, sb, pl.ds(e, 1), :]
            slot = lax.broadcasted_iota(I32, (win, TOK_BLK), 0) + a0
            rows = _dot((slot == prow).astype(BF16), h_ref[0, sb * TOK_BLK:(sb + 1) * TOK_BLK, :])
            xs_ref[cur, pl.ds(a0, win), :] = xs_ref[cur, pl.ds(a0, win), :] + rows.astype(BF16)

        all_small = functools.reduce(jnp.logical_and, [sp <= GATHER_WIN_SMALL for sp in spans])

        @pl.when(all_small)
        def _():
            for sb in range(nsub):
                gather(GATHER_WIN_SMALL, sb)

        @pl.when(jnp.logical_not(all_small))
        def _():
            for sb in range(nsub):
                nonempty = s0s[sb + 1] > s0s[sb]
                pl.when(nonempty & (spans[sb] <= GATHER_WIN_SMALL))(functools.partial(gather, GATHER_WIN_SMALL, sb))
                pl.when(spans[sb] > GATHER_WIN_SMALL)(functools.partial(gather, GATHER_WIN, sb))

    @pl.when(q > 0)
    def _():
        rows = min(TOK_BLK, rows_step)
        for c in range(rows_step // rows):
            off = pl.multiple_of(tb * rows_step + c * rows, rows)
            xc = xs_ref[1 - cur, pl.ds(off, rows), :]
            hg = _dot(xc, wg_ref[0, 0])
            hid = (hg * _sigmoid(hg) * _dot(xc, wu_ref[0, 0])).astype(BF16)
            ys_ref[0, 0, pl.ds(off, rows), :] = _dot(hid, wd_ref[0, 0]).astype(BF16)


def _moe(s0_flat, pos4, h2, lw, capp):
    b, t, _ = h2.shape
    layer = lw["layer"]
    nblk = t // TOK_BLK
    tok = min(MOE_TOK, t)
    nsub = tok // TOK_BLK
    nstep = t // tok
    rows_step = capp // nstep
    assert rows_step % min(TOK_BLK, rows_step) == 0 and rows_step % 16 == 0
    npair = N_EXPERTS * b
    gat = lambda q: jnp.minimum(q, npair - 1)
    ffn = lambda q: jnp.maximum(q - 1, 0)
    grid_spec = pltpu.PrefetchScalarGridSpec(
        num_scalar_prefetch=1, grid=(npair + 1, nstep),
        in_specs=[pl.BlockSpec((1, nsub, N_EXPERTS, TOK_BLK), lambda q, tb, s: (gat(q) % b, tb, 0, 0)),
                  pl.BlockSpec((1, tok, D), lambda q, tb, s: (gat(q) % b, tb, 0)),
                  pl.BlockSpec((1, 1, D, EXPERT_FF), lambda q, tb, s: (layer, ffn(q) // b, 0, 0)),
                  pl.BlockSpec((1, 1, D, EXPERT_FF), lambda q, tb, s: (layer, ffn(q) // b, 0, 0)),
                  pl.BlockSpec((1, 1, EXPERT_FF, D), lambda q, tb, s: (layer, ffn(q) // b, 0, 0))],
        out_specs=pl.BlockSpec((1, 1, capp, D), lambda q, tb, s: (ffn(q) % b, ffn(q) // b, 0, 0)),
        scratch_shapes=[pltpu.VMEM((2, capp + GATHER_WIN, D), BF16)])
    return pl.pallas_call(
        functools.partial(_moe_kernel, nbatch=b, nblk=nblk, nsub=nsub, rows_step=rows_step), name="moe",
        grid_spec=grid_spec, out_shape=jax.ShapeDtypeStruct((b, N_EXPERTS, capp, D), BF16),
        compiler_params=_cparams(("arbitrary", "arbitrary")),
    )(s0_flat, pos4, h2, lw["w_e_gate"], lw["w_e_up"], lw["w_e_down"])


def _combine_kernel(s0_ref, *refs, nblk, nb, sblk):
    ys_refs = refs[:2 * N_EXPERTS]
    pos_ref, aff_ref, xm_ref, mod_ref, g_ref, o_ref = refs[2 * N_EXPERTS:]
    bi, tb = pl.program_id(0), pl.program_id(1)
    pos = pos_ref[0]
    aff = aff_ref[0]
    lane = lax.broadcasted_iota(I32, (TOK_BLK, 2 * sblk), 1)
    acc = jnp.zeros((TOK_BLK, D), F32)
    for e in range(N_EXPERTS):
        s0 = s0_ref[(bi * N_EXPERTS + e) * (nblk + 1) + tb]
        blk0 = jnp.minimum(s0 // sblk, nb - 1)
        rel = pos[:, e:e + 1] - blk0 * sblk
        ysw = jnp.concatenate([ys_refs[2 * e][0, 0], ys_refs[2 * e + 1][0, 0]], axis=0)
        acc = acc + aff[:, e:e + 1] * _dot((lane == rel).astype(BF16), ysw)
    mod = mod_ref[0]
    o_ref[0] = xm_ref[0] + mod[5:6] * (_rms(acc) * g_ref[...])


def _combine(s0_flat, ys, pos_t, aff, xm, mod, g_post, sblk):
    b, t, _ = xm.shape
    nblk = t // TOK_BLK
    nb = ys.shape[2] // sblk

    def ys_spec(e, k):
        def ix(bi, tb, s):
            blk0 = jnp.minimum(s[(bi * N_EXPERTS + e) * (nblk + 1) + tb] // sblk, nb - 1)
            return (bi, e, jnp.minimum(blk0 + k, nb - 1), 0)
        return pl.BlockSpec((1, 1, sblk, D), ix)

    tok = lambda w: pl.BlockSpec((1, TOK_BLK, w), lambda bi, tb, s: (bi, tb, 0))
    in_specs = [ys_spec(e, k) for e in range(N_EXPERTS) for k in range(2)]
    in_specs += [tok(N_EXPERTS), tok(N_EXPERTS), tok(D), pl.BlockSpec((1, 8, D), lambda bi, tb, s: (bi, 0, 0)),
                 pl.BlockSpec((1, D), lambda bi, tb, s: (0, 0))]
    grid_spec = pltpu.PrefetchScalarGridSpec(num_scalar_prefetch=1, grid=(b, nblk), in_specs=in_specs, out_specs=tok(D))
    return pl.pallas_call(
        functools.partial(_combine_kernel, nblk=nblk, nb=nb, sblk=sblk), name="combine", grid_spec=grid_spec,
        out_shape=jax.ShapeDtypeStruct((b, t, D), F32),
        compiler_params=_cparams(("arbitrary", "arbitrary")),
    )(s0_flat, *([ys] * (2 * N_EXPERTS)), pos_t, aff, xm, mod, g_post)


def _rope_table(t):
    nf = ROPE_DIM // 4
    pos = jnp.arange(t)
    inv = ROPE_BASE ** (-jnp.arange(nf, dtype=F32) / nf)
    ang = jnp.stack([pos // GRID_W, pos % GRID_W], axis=-1).astype(F32)[..., None] * inv
    cos, sin = jnp.cos(ang), jnp.sin(ang)
    c32 = jnp.stack([cos, cos], axis=2).reshape(t, ROPE_DIM)
    s32 = jnp.stack([-sin, sin], axis=2).reshape(t, ROPE_DIM)
    one, zero = jnp.ones((t, 64), F32), jnp.zeros((t, 32), F32)
    ct = jnp.concatenate([one, c32, zero], axis=1)
    st = jnp.concatenate([0.0 * one, s32, zero], axis=1)
    return jnp.concatenate([ct, st, jnp.tile(c32, (1, 4)), jnp.tile(s32, (1, 4))], axis=1)


def _identity_table(t):
    one, zero = jnp.ones((t, 128), F32), jnp.zeros((t, 128), F32)
    ct = jnp.concatenate([jnp.ones((t, 96), F32), jnp.zeros((t, 32), F32)], axis=1)
    return jnp.concatenate([ct, zero, one, zero], axis=1)


def _layer_weights(i, p):
    lw = {}
    row = lambda a: a.reshape(1, -1)
    for name in ("g_mix_pre", "g_mix_post", "g_ffn_pre", "g_ffn_post", "g_q_lat", "g_kv_lat", "g_mlstm_out", "g_gla_out"):
        lw[name] = row(p[name][i])
    lw["w_ext"] = _gather_cols(p["w_in"][i], _WIN_IDX).astype(BF16)
    lw["wgt"] = p["w_in"][i][:, 416 + 1024:416 + 1040].T.astype(BF16)
    gb = jnp.concatenate([p["b_igate"][i].reshape(-1), p["b_fgate"][i].reshape(-1)])
    lw["gate_bias_row"] = jnp.pad(gb, (0, LANE - 16)).reshape(1, LANE)
    lw["gate_bias_col"] = gb.reshape(16, 1)
    wuq = p["w_uq"][i]
    qi = -np.ones((512,), np.int64)
    qsi = -np.ones((512,), np.int64)
    for h in range(N_HEADS):
        qi[128 * h:128 * h + 96] = 96 * h + np.arange(96)
        qsi[128 * h + 64:128 * h + 96] = 96 * h + 64 + (np.arange(32) ^ 8)
    lw["wq"] = _gather_cols(wuq, qi).astype(BF16)
    lw["wqs"] = _gather_cols(wuq, qsi).astype(BF16)
    ki = -np.ones((512,), np.int64)
    for h in range(N_HEADS):
        ki[128 * h:128 * h + 64] = 128 * h + np.arange(64)
    lw["wk"] = _gather_cols(p["w_ukv"][i], ki).astype(BF16)
    lw["wvt"] = _vt_rows(p["w_ukv"][i].reshape(MLA_KV_LORA, N_HEADS, 128)[:, :, 64:].reshape(MLA_KV_LORA, 256)).astype(BF16)
    lw["wdvt"] = _vt_rows(p["w_in"][i][:, 2768:3024]).astype(BF16)
    wa = p["w_alpha2"][i]
    wal = jnp.zeros((LANE, 256), F32).at[0:16, 0:128].set(wa[0]).at[16:32, 128:256].set(wa[1])
    lw["walpha"] = wal.astype(BF16)
    lw["balpha"] = p["b_alpha"][i].reshape(1, 256)
    lw["w_conv"] = p["w_conv"][i]
    lw["b_conv"] = row(p["b_conv"][i])
    lw["dlam"] = p["diff_lambda"][i]
    lw["g_diff"] = jnp.broadcast_to(p["g_diff_out"][i].reshape(N_HEADS, DIFF_DV, 1), (N_HEADS, DIFF_DV, LANE))
    wb = p["w_branch"][i]
    lw["wbr"] = jnp.concatenate([_pad_heads_rows(wb[0], 64), wb[1], wb[2], _pad_heads_rows(wb[3], 64)], axis=0).astype(BF16)
    lw["w_gate"] = p["w_gate"][i].astype(BF16)
    lw["b_gate"] = row(p["b_gate"][i])
    lw["w_out"] = p["w_out"][i].astype(BF16)
    lw["w_router"] = jnp.pad(p["w_router"][i], ((0, 0), (0, LANE - N_EXPERTS))).astype(BF16)
    lw["w_router_t"] = p["w_router"][i].T.astype(BF16)
    lw["layer"] = i
    lw["w_e_gate"], lw["w_e_up"], lw["w_e_down"] = p["w_e_gate"], p["w_e_up"], p["w_e_down"]
    return lw


def _ffn(xm, h2, aff, aff_t, mod, lw):
    b, t, _ = xm.shape
    nblk = t // TOK_BLK
    cap = EC_CAPACITY * t // N_EXPERTS
    capp = -(-cap // TOK_BLK) * TOK_BLK
    pos4, s04 = _topk(aff_t, cap)
    s0_be = jnp.concatenate([s04[..., 0].transpose(0, 2, 1), jnp.full((b, N_EXPERTS, 1), cap, I32)], axis=-1)
    s0_flat = s0_be.reshape(-1)
    ys = _moe(s0_flat, pos4, h2, lw, capp)
    pos_t = pos4.transpose(0, 1, 3, 2).reshape(b, t, N_EXPERTS)
    first = jnp.minimum(s0_be[..., :-1] // SLOT_BLK, capp // SLOT_BLK - 1)
    fits = jnp.all(s0_be[..., 1:] <= (first + 2) * SLOT_BLK)
    args = (s0_flat, ys, pos_t, aff, xm, mod, lw["g_ffn_post"])
    return lax.cond(fits, functools.partial(_combine, sblk=SLOT_BLK), functools.partial(_combine, sblk=TOK_BLK), *args)


def _hybrid_layer(i, x_c, x_l, c8, need_ctx, p):
    lw = _layer_weights(i, p)
    b, t, _ = x_l.shape
    tc = x_c.shape[1]
    lam_init = 0.8 - 0.6 * math.exp(-0.3 * i)
    mod8 = _ada(c8, p["w_ada"][i], p["b_ada"][i])
    pad = lambda m: jnp.pad(m.reshape(b, 6, D), ((0, 0), (0, 2), (0, 0)))
    mod_l = pad(mod8[:b])
    mod_c = pad(jnp.broadcast_to(mod8[b:b + 1], (b, 6 * D)))

    pc = _proj(x_c, mod_c, _identity_table(tc), lw)
    pt = _proj(x_l, mod_l, _rope_table(t), lw)

    zc = jnp.zeros((b, 2, 256, 512), F32)
    zm = jnp.full((b, 2, 8, 256), NEG, F32)
    zs = jnp.zeros((b, 2, 256, 128), F32)
    hf_c, hb_c, c_fin, m_fin = _mlstm(pc, lw["w_conv"], lw["b_conv"], zc, zm)
    hf_l, hb_l, _, _ = _mlstm(pt, lw["w_conv"], lw["b_conv"], c_fin, m_fin)
    gf_c, gb_c, s_fin = _gla(pc, zs)
    gf_l, gb_l, _ = _gla(pt, s_fin)

    one_g = jnp.ones((N_HEADS, MLA_V, LANE), F32)
    fl = functools.partial(_flash, tk=FLASH_KEYS)
    ya_l = fl(pt["mqt"], pc["mk"], pc["mvt"], pt["mk"], pt["mvt"], lw["dlam"], one_g, nmap=1, finish=False, post=1.0,
              tq=min(FLASH_ROWS, t))
    yd_l = fl(pt["dqt"], pc["dk"], pc["dvt"], pt["dk"], pt["dvt"], lw["dlam"], lw["g_diff"], nmap=2, finish=True,
              post=1.0 - lam_init, tq=min(FLASH_ROWS // 2, t))
    xm, h2, aff, aff_t = _merge(x_l, mod_l, ya_l, (hf_l, hb_l), pt["lo"], (gf_l, gb_l), pt["gr"], yd_l, lw, tm=min(512, t))
    x_l = _ffn(xm, h2, aff, aff_t, mod_l, lw)

    if need_ctx:
        ya_c = fl(pc["mqt"], pc["mk"], pc["mvt"], None, None, lw["dlam"], one_g, nmap=1, finish=False, post=1.0, tq=tc)
        yd_c = fl(pc["dqt"], pc["dk"], pc["dvt"], None, None, lw["dlam"], lw["g_diff"], nmap=2, finish=True,
                  post=1.0 - lam_init, tq=tc)
        xm, h2, aff, aff_t = _merge(x_c, mod_c, ya_c, (hf_c, hb_c), pc["lo"], (gf_c, gb_c), pc["gr"], yd_c, lw, tm=tc)
        x_c = _ffn(xm, h2, aff, aff_t, mod_c, lw)
    return x_c, x_l


def kernel(x, c, ctx, c_ctx, w_ada, b_ada, g_mix_pre, g_mix_post, g_ffn_pre, g_ffn_post, w_in, g_q_lat, w_uq, g_kv_lat, w_ukv, w_conv, b_conv, b_igate, b_fgate, g_mlstm_out, w_alpha2, b_alpha, g_gla_out, diff_lambda, g_diff_out, w_branch, w_gate, b_gate, w_out, w_router, w_e_gate, w_e_up, w_e_down):
    p = dict(w_ada=w_ada, b_ada=b_ada, g_mix_pre=g_mix_pre, g_mix_post=g_mix_post, g_ffn_pre=g_ffn_pre,
             g_ffn_post=g_ffn_post, w_in=w_in, g_q_lat=g_q_lat, w_uq=w_uq, g_kv_lat=g_kv_lat, w_ukv=w_ukv,
             w_conv=w_conv, b_conv=b_conv, b_igate=b_igate, b_fgate=b_fgate, g_mlstm_out=g_mlstm_out,
             w_alpha2=w_alpha2, b_alpha=b_alpha, g_gla_out=g_gla_out, diff_lambda=diff_lambda, g_diff_out=g_diff_out,
             w_branch=w_branch, w_gate=w_gate, b_gate=b_gate, w_out=w_out, w_router=w_router,
             w_e_gate=w_e_gate.astype(BF16), w_e_up=w_e_up.astype(BF16), w_e_down=w_e_down.astype(BF16))
    b = x.shape[0]
    c8 = jnp.concatenate([c, c_ctx[None], jnp.zeros((8 - b - 1, D), F32)], axis=0)
    x_c, x_l = ctx, x
    for i in range(DEPTH):
        x_c, x_l = _hybrid_layer(i, x_c, x_l, c8, i < DEPTH - 1, p)
    return x_l
```

```python
import functools
import math

import numpy as np
import jax
import jax.numpy as jnp
from jax import lax
from jax.experimental import pallas as pl
from jax.experimental.pallas import tpu as pltpu

F32 = jnp.float32
BF16 = jnp.bfloat16
I32 = jnp.int32

D = 1024
DEPTH = 2
GRID_W = 64
N_HEADS = 4
MLA_NOPE, MLA_ROPE, MLA_V = 64, 32, 64
MLA_Q_LORA, MLA_KV_LORA = 256, 128
ML_DH = 64
GLA_DK, GLA_DV, GLA_RANK, GLA_TAU = 32, 64, 16, 16.0
DIFF_DQK, DIFF_DV = 32, 64
ROPE_DIM, ROPE_BASE = 32, 10000.0
N_EXPERTS, EC_CAPACITY, EXPERT_FF = 16, 2, 1408
NEG = -1e30
EPS = 1e-6
LOG2E = 1.4426950408889634

LANE = 128
HEAD_SLAB = 128
TOK_BLK = 256
ML_CHUNK = 256
ML_BLOCK = 256
GLA_CHUNK = 64
GLA_BLOCK = 512
GATHER_WIN = TOK_BLK + 16
GATHER_WIN_SMALL = 64
MOE_TOK = 4096
SLOT_BLK = 128
VMEM_LIMIT = 56 * 1024 * 1024

ZQ, ZKV, ZKRA, ZKRB, ZMLQK, ZMLV, ZMLO, ZGATE, ZGA = 0, 256, 384, 512, 640, 1152, 1408, 1664, 1792
ZGQ, ZGK, ZGV, ZGR, ZDQ, ZDQS, ZDK, ZDKS, NZ = 1920, 2048, 2176, 2432, 2688, 2944, 3200, 3456, 3712
KV_CHUNK = 256
VT_ROWS = 80
FLASH_ROWS = 512
FLASH_KEYS = 512
FLASH_UNROLL = 16


def _swap32(c):
    return (c // 32) * 32 + ((c % 32) ^ 8)


def _win_index():
    idx = -np.ones((NZ,), np.int64)
    idx[ZQ:ZQ + 256] = np.arange(0, 256)
    idx[ZKV:ZKV + 128] = np.arange(256, 384)
    r = np.arange(32)
    idx[ZKRA + 64:ZKRA + 96] = 384 + r
    idx[ZKRB + 64:ZKRB + 96] = 384 + (r ^ 8)
    ml = 416
    idx[ZMLQK:ZMLQK + 512] = ml + np.arange(512)
    idx[ZMLV:ZMLV + 256] = ml + 512 + np.arange(256)
    idx[ZMLO:ZMLO + 256] = ml + 768 + np.arange(256)
    idx[ZGATE:ZGATE + 16] = ml + 1024 + np.arange(16)
    gl = 1456
    idx[ZGQ:ZGQ + 128] = gl + np.arange(128)
    idx[ZGK:ZGK + 128] = gl + 128 + np.arange(128)
    idx[ZGV:ZGV + 256] = gl + 256 + np.arange(256)
    idx[ZGR:ZGR + 256] = gl + 512 + np.arange(256)
    idx[ZGA:ZGA + 32] = gl + 768 + np.arange(32)
    df = 2256
    c = np.arange(256)
    idx[ZDQ:ZDQ + 256] = df + c
    idx[ZDQS:ZDQS + 256] = df + _swap32(c)
    idx[ZDK:ZDK + 256] = df + 256 + c
    idx[ZDKS:ZDKS + 256] = df + 256 + _swap32(c)
    return idx


def _vt_rows(w_cols):
    n = w_cols.shape[0]
    w4 = w_cols.T.reshape(N_HEADS, 64, n)
    return jnp.pad(w4, ((0, 0), (0, VT_ROWS - 64), (0, 0))).reshape(N_HEADS * VT_ROWS, n)


_WIN_IDX = _win_index()


def _gather_cols(w, idx):
    safe = np.maximum(idx, 0)
    return jnp.where(jnp.asarray(idx >= 0)[None, :], w[:, safe], 0.0)


def _pad_heads_rows(w, width):
    n = w.shape[1]
    w4 = w.reshape(N_HEADS, width, n)
    return jnp.pad(w4, ((0, 0), (0, HEAD_SLAB - width), (0, 0))).reshape(N_HEADS * HEAD_SLAB, n)


def _cparams(sem):
    return pltpu.CompilerParams(dimension_semantics=sem, vmem_limit_bytes=VMEM_LIMIT)


def _rms(x):
    return x * lax.rsqrt(jnp.mean(x * x, axis=-1, keepdims=True) + EPS)


def _sigmoid(x):
    return 0.5 * jnp.tanh(0.5 * x) + 0.5


def _log_sigmoid(x):
    return jnp.minimum(x, 0.0) - jnp.log1p(jnp.exp(-jnp.abs(x)))


def _dot(a, b, precision=None):
    return jnp.dot(a, b, preferred_element_type=F32, precision=precision)


def _dot_nt(a, b, precision=None):
    return lax.dot_general(a, b, (((1,), (1,)), ((), ())), preferred_element_type=F32, precision=precision)


def _dot_tn(a, b, precision=None):
    return lax.dot_general(a, b, (((0,), (0,)), ((), ())), preferred_element_type=F32, precision=precision)


def _split_bf16(x, parts):
    out, r = [], x
    for _ in range(parts):
        t = r.astype(BF16)
        out.append(t)
        r = r - t.astype(F32)
    return out


def _dot_sel(sel, x, parts=3):
    return sum(_dot(sel, t) for t in _split_bf16(x, parts))


def _const_spec(shape):
    nd = len(shape)
    return pl.BlockSpec(shape, lambda *_: (0,) * nd)


def _ada_kernel(c_ref, w_ref, b_ref, o_ref):
    cv = c_ref[...]
    s = (cv * _sigmoid(cv)).astype(BF16)
    o_ref[...] = _dot(s, w_ref[...].astype(BF16)) + b_ref[...]


def _ada(c8, w_ada, b_ada):
    n, tn = 6 * D, 1024
    return pl.pallas_call(
        _ada_kernel, name="ada", grid=(n // tn,),
        in_specs=[pl.BlockSpec((8, D), lambda j: (0, 0)), pl.BlockSpec((D, tn), lambda j: (0, j)),
                  pl.BlockSpec((1, tn), lambda j: (0, j))],
        out_specs=pl.BlockSpec((8, tn), lambda j: (0, j)),
        out_shape=jax.ShapeDtypeStruct((8, n), F32), compiler_params=_cparams(("arbitrary",)),
    )(c8, w_ada, b_ada.reshape(1, n))


_PROJ_OUT = (
    ("mk", 512, BF16),
    ("lqk", 512, F32), ("lv", 256, BF16), ("lo", 256, BF16), ("gc", 128, F32),
    ("gq", 128, BF16), ("gk", 128, BF16), ("gv", 256, BF16), ("gr", 256, BF16), ("glg", 256, F32),
    ("dk", 512, BF16),
)


def _proj_kernel(x_ref, mod_ref, g_ref, w_ref, tab_ref, gq_ref, wq_ref, wqs_ref, gkv_ref, wk_ref, wvt_ref,
                 wgt_ref, gbr_ref, gbc_ref, wal_ref, bal_ref, wdvt_ref,
                 mk_ref, lqk_ref, lv_ref, lo_ref, gc_ref, gq_o, gk_o, gv_o, gr_o, glg_o,
                 dk_ref, grow_ref, mvt_ref, dvt_ref, mqt_ref, dqt_ref):
    x = x_ref[0]
    tm = x.shape[0]
    mod = mod_ref[0]
    h = _rms(x) * g_ref[...] * (1.0 + mod[1:2]) + mod[0:1]
    hb = h.astype(BF16)
    z = _dot(hb, w_ref[...])
    tab = tab_ref[...]
    ct, st, cd, sd = tab[:, 0:128], tab[:, 128:256], tab[:, 256:384], tab[:, 384:512]
    lane = lax.broadcasted_iota(I32, (tm, LANE), 1)

    qn = (_rms(z[:, ZQ:ZQ + 256]) * gq_ref[...]).astype(BF16)
    qa = _dot(qn, wq_ref[...])
    qb = _dot(qn, wqs_ref[...])
    qscale = (MLA_NOPE + MLA_ROPE) ** -0.5 * LOG2E
    for hh in range(N_HEADS):
        sl = slice(HEAD_SLAB * hh, HEAD_SLAB * (hh + 1))
        mqt_ref[0, hh] = ((qa[:, sl] * ct + qb[:, sl] * st) * qscale).T.astype(BF16)
    kvn = (_rms(z[:, ZKV:ZKV + 128]) * gkv_ref[...]).astype(BF16)
    kk = _dot(kvn, wk_ref[...])
    kr = z[:, ZKRA:ZKRA + 128] * ct + z[:, ZKRB:ZKRB + 128] * st
    for hh in range(N_HEADS):
        sl = slice(HEAD_SLAB * hh, HEAD_SLAB * (hh + 1))
        mk_ref[0, :, sl] = (kk[:, sl] + kr).astype(BF16)
    ones_row = lax.broadcasted_iota(I32, (N_HEADS * VT_ROWS, tm), 0) % VT_ROWS == MLA_V
    mvt_ref[0, 0] = jnp.where(ones_row, 1.0, _dot_nt(wvt_ref[...], kvn)).astype(BF16)
    dvt_ref[0, 0] = jnp.where(ones_row, 1.0, _dot_nt(wdvt_ref[...], hb)).astype(BF16)

    lqk_ref[0] = z[:, ZMLQK:ZMLQK + 512]
    lv_ref[0] = z[:, ZMLV:ZMLV + 256].astype(BF16)
    lo_ref[0] = z[:, ZMLO:ZMLO + 256].astype(BF16)
    gcol = z[:, ZGATE:ZGATE + 128] + gbr_ref[...]
    gc_ref[0] = jnp.where(lane < 8, gcol, jnp.where(lane < 16, _log_sigmoid(gcol), 0.0))
    zr = _dot_nt(wgt_ref[...], hb) + gbc_ref[...]
    rowi = lax.broadcasted_iota(I32, zr.shape, 0)
    grow_ref[0] = jnp.where(rowi < 8, zr, _log_sigmoid(zr))

    gq_o[0] = (z[:, ZGQ:ZGQ + 128] * GLA_DK ** -0.5).astype(BF16)
    gk_o[0] = z[:, ZGK:ZGK + 128].astype(BF16)
    gv_o[0] = z[:, ZGV:ZGV + 256].astype(BF16)
    gr_o[0] = z[:, ZGR:ZGR + 256].astype(BF16)
    zg = _dot(z[:, ZGA:ZGA + 128].astype(BF16), wal_ref[...]) + bal_ref[...]
    glg_o[0] = _log_sigmoid(zg) * (1.0 / GLA_TAU)

    dscale = DIFF_DQK ** -0.5 * LOG2E
    for g in range(2):
        gs = slice(128 * g, 128 * (g + 1))
        qg = (z[:, ZDQ:ZDQ + 256][:, gs] * cd + z[:, ZDQS:ZDQS + 256][:, gs] * sd) * dscale
        kg = z[:, ZDK:ZDK + 256][:, gs] * cd + z[:, ZDKS:ZDKS + 256][:, gs] * sd
        for hl in range(2):
            hh = 2 * g + hl
            for m in range(2):
                lo = 64 * hl + 32 * m
                dqt_ref[0, 2 * hh + m] = jnp.where((lane >= lo) & (lane < lo + 32), qg, 0.0).T.astype(BF16)
            dk_ref[0, :, HEAD_SLAB * hh:HEAD_SLAB * (hh + 1)] = jnp.where(
                (lane >= 64 * hl) & (lane < 64 * hl + 64), kg, 0.0).astype(BF16)


def _proj(x, mod, tab, lw):
    b, t, _ = x.shape
    tm = KV_CHUNK
    consts = [lw["g_mix_pre"], lw["w_ext"], None, lw["g_q_lat"], lw["wq"], lw["wqs"], lw["g_kv_lat"], lw["wk"], lw["wvt"],
              lw["wgt"], lw["gate_bias_row"], lw["gate_bias_col"], lw["walpha"], lw["balpha"], lw["wdvt"]]
    in_specs = [pl.BlockSpec((1, tm, D), lambda bi, i: (bi, i, 0)), pl.BlockSpec((1, 8, D), lambda bi, i: (bi, 0, 0))]
    args = [x, mod]
    for cst in consts:
        if cst is None:
            in_specs.append(pl.BlockSpec((tm, 512), lambda bi, i: (i, 0)))
            args.append(tab)
        else:
            in_specs.append(_const_spec(cst.shape))
            args.append(cst)
    out_specs = [pl.BlockSpec((1, tm, w), lambda bi, i: (bi, i, 0)) for _, w, _ in _PROJ_OUT]
    out_shape = [jax.ShapeDtypeStruct((b, t, w), dt) for _, w, dt in _PROJ_OUT]
    out_specs.append(pl.BlockSpec((1, 16, tm), lambda bi, i: (bi, 0, i)))
    out_shape.append(jax.ShapeDtypeStruct((b, 16, t), F32))
    for _ in range(2):
        out_specs.append(pl.BlockSpec((1, 1, N_HEADS * VT_ROWS, tm), lambda bi, i: (bi, i, 0, 0)))
        out_shape.append(jax.ShapeDtypeStruct((b, t // tm, N_HEADS * VT_ROWS, tm), BF16))
    for nslab in (N_HEADS, 2 * N_HEADS):
        out_specs.append(pl.BlockSpec((1, nslab, HEAD_SLAB, tm), lambda bi, i: (bi, 0, 0, i)))
        out_shape.append(jax.ShapeDtypeStruct((b, nslab, HEAD_SLAB, t), BF16))
    outs = pl.pallas_call(
        _proj_kernel, name="proj", grid=(b, t // tm), in_specs=in_specs, out_specs=out_specs, out_shape=out_shape,
        compiler_params=_cparams(("parallel", "arbitrary")),
    )(*args)
    res = {name: o for (name, _, _), o in zip(_PROJ_OUT, outs[:-5])}
    res["grow"], res["mvt"], res["dvt"], res["mqt"], res["dqt"] = outs[-5:]
    return res


def _flash_kernel(*refs, nmap, has_lat, tk, finish, post):
    if has_lat:
        q_ref, kc_ref, vc_ref, kl_ref, vl_ref, dl_ref, g_ref, o_ref, s_ref, acc_ref = refs
    else:
        q_ref, kc_ref, vc_ref, dl_ref, g_ref, o_ref = refs
    tq = q_ref.shape[3]
    qt = q_ref[0, 0] if nmap == 1 else jnp.concatenate([q_ref[0, mm] for mm in range(nmap)], axis=1)
    rows = nmap * tq
    sub = tk // KV_CHUNK

    def softmax(s, smax, m):
        m_new = jnp.maximum(m, smax)
        return m_new, jnp.exp2(m - m_new), jnp.exp2(s - m_new).astype(BF16)

    def pv(p, vts):
        return _dot(vts[0] if len(vts) == 1 else jnp.concatenate(vts, axis=1), p)

    s_ctx = _dot(kc_ref[0], qt)
    if has_lat:
        n = kl_ref.shape[1] // tk
        unroll = min(FLASH_UNROLL, n)

        def scores(j):
            if isinstance(j, int):
                return _dot(kl_ref[0, j * tk:(j + 1) * tk, :], qt)
            off = pl.multiple_of(j * tk, tk)
            return _dot(kl_ref[0, pl.ds(off, tk), :], qt)

        def values(j):
            return [vl_ref[0, j * sub + c] for c in range(sub)]

        def produce(slot, j):
            s = scores(j)
            s_ref[slot] = s
            return jnp.max(s, axis=0, keepdims=True)

        smax0 = produce(0, 0)

    m, _, p = softmax(s_ctx, jnp.max(s_ctx, axis=0, keepdims=True), jnp.full((1, rows), NEG, F32))
    acc = pv(p, [vc_ref[0, 0]])
    if has_lat:
        acc_ref[...] = acc

        def body(jj, carry):
            m, smax = carry
            j = unroll * jj
            for u in range(unroll):
                smax_next = smax
                if not isinstance(j, int):
                    smax_next = produce((u + 1) % 2, jnp.minimum(j + u + 1, n - 1))
                elif j + u + 1 < n:
                    smax_next = produce((u + 1) % 2, j + u + 1)
                m, alpha, p = softmax(s_ref[u % 2], smax, m)
                acc_ref[...] = alpha * acc_ref[...] + pv(p, values(j + u))
                smax = smax_next
            return m, smax

        if unroll == n:
            body(0, (m, smax0))
        else:
            lax.fori_loop(0, n // unroll, body, (m, smax0))
        acc = acc_ref[...]

    o = acc[0:MLA_V, :] / acc[MLA_V:MLA_V + 1, :]
    if nmap == 2:
        lv = dl_ref[...]
        lam = (jnp.exp(jnp.sum(lv[0:1] * lv[1:2], axis=-1, keepdims=True))
               - jnp.exp(jnp.sum(lv[2:3] * lv[3:4], axis=-1, keepdims=True)) + (1.0 - post))
        o = o[:, :tq] - lam * o[:, tq:]
    if finish:
        ms = jnp.mean(o * o, axis=0, keepdims=True)
        o = o * lax.rsqrt(ms + EPS) * jnp.concatenate([g_ref[0]] * (tq // LANE), axis=1) * post
    o_pad = jnp.concatenate([o, jnp.zeros((HEAD_SLAB - MLA_V, tq), F32)], axis=0)
    o_ref[0] = o_pad.T.astype(BF16)


def _flash(q, kc, vct, kl, vlt, dlam, g_out, *, nmap, finish, post, tq, tk):
    b, _, _, t = q.shape
    has_lat = kl is not None
    assert kc.shape[1] == KV_CHUNK and tk % KV_CHUNK == 0
    kspec = lambda n: pl.BlockSpec((1, n, HEAD_SLAB), lambda bi, h, i: (bi, 0, h))
    vspec = lambda n: pl.BlockSpec((1, n // KV_CHUNK, VT_ROWS, KV_CHUNK), lambda bi, h, i: (bi, 0, h, 0))
    in_specs = [pl.BlockSpec((1, nmap, HEAD_SLAB, tq), lambda bi, h, i: (bi, h, 0, i)), kspec(KV_CHUNK), vspec(KV_CHUNK)]
    args = [q, kc, vct]
    scratch = []
    if has_lat:
        tl = kl.shape[1]
        assert (tl // tk) % min(FLASH_UNROLL, tl // tk) == 0
        in_specs += [kspec(tl), vspec(tl)]
        args += [kl, vlt]
        scratch = [pltpu.VMEM((2, tk, nmap * tq), F32), pltpu.VMEM((VT_ROWS, nmap * tq), F32)]
    in_specs += [_const_spec(dlam.shape), pl.BlockSpec((1, MLA_V, LANE), lambda bi, h, i: (h, 0, 0))]
    args += [dlam, g_out]
    return pl.pallas_call(
        functools.partial(_flash_kernel, nmap=nmap, has_lat=has_lat, tk=tk, finish=finish, post=post),
        name="flash_diff" if nmap == 2 else "flash_mla",
        grid=(b, N_HEADS, t // tq), in_specs=in_specs,
        out_specs=pl.BlockSpec((1, tq, HEAD_SLAB), lambda bi, h, i: (bi, i, h)),
        out_shape=jax.ShapeDtypeStruct((b, t, N_HEADS * HEAD_SLAB), BF16),
        scratch_shapes=scratch,
        compiler_params=_cparams(("parallel", "parallel", "arbitrary")),
    )(*args)


def _head_of(shape, axis, width):
    return (lax.broadcasted_iota(I32, shape, axis) % (N_HEADS * width)) // width


def _mlstm_conv(first, last, x, xprev, xnext, wc, bcv):
    n = x.shape[0]
    row = lax.broadcasted_iota(I32, x.shape, 0)
    pr = jnp.where(first, 0.0, xprev[7:8, :])
    nx = jnp.where(last, 0.0, xnext[0:1, :])
    xm = jnp.where(row == 0, pr, pltpu.roll(x, 1, 0))
    xp = jnp.where(row == n - 1, nx, pltpu.roll(x, n - 1, 0))
    y = xm * wc[0:1] + x * wc[1:2] + xp * wc[2:3] + bcv
    qk = y * _sigmoid(y)
    return qk[:, :256], qk[:, 256:] * ML_DH ** -0.5


def _mlstm_dir(d, q, k, v, gcol, grow, cb, m0e):
    L = q.shape[0]
    li = lax.broadcasted_iota(I32, (L, L), 0)
    si = lax.broadcasted_iota(I32, (L, L), 1)
    tin = (si <= li) if d == 0 else (si >= li)
    tinb = tin.astype(BF16)
    bcol = _dot_sel(tinb, gcol)
    brow = sum(_dot_nt(t, tinb) for t in _split_bf16(grow, 3))
    hm256 = _head_of((L, 256), 1, ML_DH)
    hm512 = _head_of((L, 512), 1, ML_DH)
    e_idx = L - 1 if d == 0 else 0

    d_blk, inter_blk = [], []
    for hh in range(N_HEADS):
        c = 4 * d + hh
        bc = bcol[:, 8 + c:9 + c]
        d_blk.append(jnp.where(tin, bc - brow[8 + c:9 + c, :] + grow[c:c + 1, :], NEG))
        inter_blk.append(bc + m0e[0:1, 64 * hh:64 * hh + 1])
    d_st = jnp.concatenate(d_blk, axis=0)
    inter_st = jnp.concatenate(inter_blk, axis=0)
    mt = jnp.maximum(inter_st, jnp.max(d_st, axis=-1, keepdims=True))
    q_st = jnp.concatenate([jnp.where(hm256 == hh, q, 0.0) for hh in range(N_HEADS)], axis=0).astype(BF16)
    s_st = (jnp.exp(d_st - mt) * _dot_nt(q_st, k.astype(BF16))).astype(BF16)
    vext = jnp.concatenate([v, jnp.ones((L, 256), BF16)], axis=1)
    r = _dot(s_st, vext)
    aint = jnp.exp(inter_st - mt)
    p = _dot(q.astype(BF16), cb.astype(BF16))
    tot = jnp.zeros((L, 512), F32)
    mte = jnp.zeros((L, 256), F32)
    for hh in range(N_HEADS):
        rs = slice(hh * L, (hh + 1) * L)
        tot = jnp.where(hm512 == hh, r[rs] + aint[rs] * p, tot)
        mte = jnp.where(hm256 == hh, mt[rs], mte)
    hout = tot[:, :256] / jnp.maximum(jnp.abs(tot[:, 256:]), jnp.exp(-mte))

    wexp = jnp.zeros((L, 256), F32)
    arow = jnp.zeros((1, 512), F32)
    grw = jnp.zeros((1, 512), F32)
    mnew = jnp.zeros((1, 256), F32)
    hr512 = _head_of((1, 512), 1, ML_DH)
    hr256 = _head_of((1, 256), 1, ML_DH)
    for hh in range(N_HEADS):
        c = 4 * d + hh
        bc = bcol[:, 8 + c:9 + c]
        be = bc[e_idx:e_idx + 1, :]
        wl = be - bc + gcol[:, c:c + 1]
        mloc = jnp.max(wl, axis=0, keepdims=True)
        m0h = m0e[0:1, 64 * hh:64 * hh + 1]
        mn = jnp.maximum(be + m0h, mloc)
        wexp = jnp.where(hm256 == hh, jnp.exp(wl - mloc), wexp)
        arow = jnp.where(hr512 == hh, jnp.exp(be + m0h - mn), arow)
        grw = jnp.where(hr512 == hh, jnp.exp(mloc - mn), grw)
        mnew = jnp.where(hr256 == hh, mn, mnew)
    cl = _dot_tn((k * wexp).astype(BF16), vext)
    bd = lax.broadcasted_iota(I32, (256, 512), 0) // ML_DH == _head_of((256, 512), 1, ML_DH)
    return hout, arow * cb + jnp.where(bd, grw * cl, 0.0), jnp.broadcast_to(mnew, (8, 256))


def _mlstm_kernel(xf, xfp, xfn, xb, xbp, xbn, vf, vb, gcf, gcb, grf, grb, wc_ref, bc_ref, c0_ref, m0_ref,
                  hf_ref, hb_ref, c_ref, m_ref):
    i = pl.program_id(1)
    n = pl.num_programs(1)

    @pl.when(i == 0)
    def _():
        c_ref[...] = c0_ref[...]
        m_ref[...] = m0_ref[...]

    wc = wc_ref[...]
    bcv = bc_ref[...]
    L = ML_CHUNK
    nsub = xf.shape[1] // L
    streams = ((0, i == 0, i == n - 1, xf, xfp, xfn, vf, gcf, grf, hf_ref),
               (1, i == n - 1, i == 0, xb, xbp, xbn, vb, gcb, grb, hb_ref))
    for d, first, last, x, xp, xn, v, gc, gr, h_ref in streams:
        q, k = _mlstm_conv(first, last, x[0], xp[0], xn[0], wc, bcv)
        cb, m0e = c_ref[0, d], m_ref[0, d]
        for c in (range(nsub) if d == 0 else reversed(range(nsub))):
            sl = slice(c * L, (c + 1) * L)
            h_ref[0, sl, :], cb, m0e = _mlstm_dir(d, q[sl], k[sl], v[0, sl, :], gc[0, sl, :], gr[0, :, sl], cb, m0e)
        c_ref[0, d], m_ref[0, d] = cb, m0e


def _mlstm(pr, w_conv, b_conv, c0, m0):
    x, v, gc, gr = pr["lqk"], pr["lv"], pr["gc"], pr["grow"]
    b, t, _ = x.shape
    L = min(ML_BLOCK, t)
    n = t // L
    r8 = L // 8
    last8 = t // 8 - 1

    def fw(bi, i):
        return (bi, i, 0)

    def bw(bi, i):
        return (bi, n - 1 - i, 0)

    def halo(ix, shift):
        def f(bi, i):
            blk = ix(bi, i)[1]
            return (bi, jnp.clip(blk * r8 + shift, 0, last8), 0)
        return f

    main = lambda w, ix: pl.BlockSpec((1, L, w), ix)
    in_specs = [main(512, fw), pl.BlockSpec((1, 8, 512), halo(fw, -1)), pl.BlockSpec((1, 8, 512), halo(fw, r8)),
                main(512, bw), pl.BlockSpec((1, 8, 512), halo(bw, -1)), pl.BlockSpec((1, 8, 512), halo(bw, r8)),
                main(256, fw), main(256, bw), main(128, fw), main(128, bw),
                pl.BlockSpec((1, 16, L), lambda bi, i: (bi, 0, i)), pl.BlockSpec((1, 16, L), lambda bi, i: (bi, 0, n - 1 - i)),
                _const_spec(w_conv.shape), _const_spec(b_conv.shape),
                pl.BlockSpec((1, 2, 256, 512), lambda bi, i: (bi, 0, 0, 0)), pl.BlockSpec((1, 2, 8, 256), lambda bi, i: (bi, 0, 0, 0))]
    out_specs = [main(256, fw), main(256, bw),
                 pl.BlockSpec((1, 2, 256, 512), lambda bi, i: (bi, 0, 0, 0)), pl.BlockSpec((1, 2, 8, 256), lambda bi, i: (bi, 0, 0, 0))]
    out_shape = [jax.ShapeDtypeStruct((b, t, 256), F32), jax.ShapeDtypeStruct((b, t, 256), F32),
                 jax.ShapeDtypeStruct(c0.shape, F32), jax.ShapeDtypeStruct(m0.shape, F32)]
    return pl.pallas_call(
        _mlstm_kernel, name="mlstm", grid=(b, n), in_specs=in_specs, out_specs=out_specs, out_shape=out_shape,
        compiler_params=_cparams(("parallel", "arbitrary")),
    )(x, x, x, x, x, x, v, v, gc, gc, gr, gr, w_conv, b_conv, c0, m0)


def _gla_chunk(d, q, k, v, lg, sb):
    L = q.shape[0]
    li = lax.broadcasted_iota(I32, (L, L), 0)
    si = lax.broadcasted_iota(I32, (L, L), 1)
    tin = (si <= li) if d == 0 else (si >= li)
    lgd = lg[:, 128 * d:128 * (d + 1)]
    gcum = _dot_sel(tin.astype(BF16), lgd)
    e_idx = L - 1 if d == 0 else 0
    gend = gcum[e_idx:e_idx + 1, :]
    qf, kf = q.astype(F32), k.astype(F32)
    q_dec = qf * jnp.exp(gcum)
    k_dec = (kf * jnp.exp(-gcum)).astype(BF16)
    k_end = (kf * jnp.exp(gend - gcum)).astype(BF16)
    hm128 = _head_of((L, 128), 1, GLA_DK)
    hm256 = _head_of((L, 256), 1, GLA_DV)
    q_st = jnp.concatenate([jnp.where(hm128 == hh, q_dec, 0.0) for hh in range(N_HEADS)], axis=0).astype(BF16)
    att = _dot_nt(q_st, k_dec)
    tin4 = jnp.concatenate([tin] * N_HEADS, axis=0)
    o_st = _dot(jnp.where(tin4, att, 0.0).astype(BF16), v)
    o = _dot_nt(q_dec.astype(BF16), sb.astype(BF16))
    for hh in range(N_HEADS):
        o = o + jnp.where(hm256 == hh, o_st[hh * L:(hh + 1) * L], 0.0)
    bd = lax.broadcasted_iota(I32, (256, 128), 0) // GLA_DV == _head_of((256, 128), 1, GLA_DK)
    return o, jnp.exp(gend) * sb + jnp.where(bd, _dot_tn(v, k_end), 0.0)


def _gla_kernel(qf, kf, vf, lf, qb, kb, vb, lb, s0_ref, of_ref, ob_ref, s_ref):
    i = pl.program_id(1)

    @pl.when(i == 0)
    def _():
        s_ref[...] = s0_ref[...]

    L = GLA_CHUNK
    nsub = qf.shape[1] // L
    for d, (q, k, v, lg, o_ref) in enumerate(((qf, kf, vf, lf, of_ref), (qb, kb, vb, lb, ob_ref))):
        sb = s_ref[0, d]
        for c in (range(nsub) if d == 0 else reversed(range(nsub))):
            sl = slice(c * L, (c + 1) * L)
            o_ref[0, sl, :], sb = _gla_chunk(d, q[0, sl, :], k[0, sl, :], v[0, sl, :], lg[0, sl, :], sb)
        s_ref[0, d] = sb


def _gla(pr, s0):
    q, k, v, lg = pr["gq"], pr["gk"], pr["gv"], pr["glg"]
    b, t, _ = q.shape
    L = min(GLA_BLOCK, t)
    n = t // L
    fw = lambda bi, i: (bi, i, 0)
    bw = lambda bi, i: (bi, n - 1 - i, 0)
    blk = lambda w, ix: pl.BlockSpec((1, L, w), ix)
    st_spec = pl.BlockSpec((1, 2, 256, 128), lambda bi, i: (bi, 0, 0, 0))
    return pl.pallas_call(
        _gla_kernel, name="gla", grid=(b, n),
        in_specs=[blk(128, fw), blk(128, fw), blk(256, fw), blk(256, fw),
                  blk(128, bw), blk(128, bw), blk(256, bw), blk(256, bw), st_spec],
        out_specs=[blk(256, fw), blk(256, bw), st_spec],
        out_shape=[jax.ShapeDtypeStruct((b, t, 256), F32), jax.ShapeDtypeStruct((b, t, 256), F32),
                   jax.ShapeDtypeStruct(s0.shape, F32)],
        compiler_params=_cparams(("parallel", "arbitrary")),
    )(q, k, v, lg, q, k, v, lg, s0)


def _head_rms_expanded(x, width):
    n = x.shape[1]
    bd = (lax.broadcasted_iota(I32, (n, n), 0) // width == lax.broadcasted_iota(I32, (n, n), 1) // width).astype(BF16)
    return sum(_dot(t, bd) for t in _split_bf16(x * x, 2)) * (1.0 / width)


def _merge_kernel(x_ref, mod_ref, ya_ref, hf_ref, hb_ref, lo_ref, gf_ref, gb_ref, gr_ref, yd_ref,
                  gpre_ref, wg_ref, bg_ref, wbr_ref, wo_ref, gpost_ref, gffn_ref, wr_ref, wrt_ref, gml_ref, ggla_ref,
                  xm_ref, h2_ref, aff_ref, afft_ref):
    x = x_ref[0]
    tm = x.shape[0]
    mod = mod_ref[0]
    hb = (_rms(x) * gpre_ref[...] * (1.0 + mod[1:2]) + mod[0:1]).astype(BF16)

    hs = hf_ref[0] + hb_ref[0]
    y_ml = _sigmoid(lo_ref[0].astype(F32)) * (hs * lax.rsqrt(_head_rms_expanded(hs, ML_DH) + EPS) * gml_ref[...])
    gs = gf_ref[0] + gb_ref[0]
    rr = gr_ref[0].astype(F32)
    y_gla = rr * _sigmoid(rr) * (gs * lax.rsqrt(_head_rms_expanded(gs, GLA_DV) + EPS) * ggla_ref[...])

    branches = ((ya_ref[0], 0, 512), (y_ml.astype(BF16), 512, 256), (y_gla.astype(BF16), 768, 256), (yd_ref[0], 1024, 512))
    mix = jnp.zeros((tm, D), F32)
    for nb, (yb, r0, rw) in enumerate(branches):
        gate = _sigmoid(_dot(hb, wg_ref[:, nb * D:(nb + 1) * D]) + bg_ref[:, nb * D:(nb + 1) * D])
        mix = mix + gate * _dot(yb, wbr_ref[r0:r0 + rw, :])
    y = _dot(mix.astype(BF16), wo_ref[...])
    xm = x + mod[2:3] * (_rms(y) * gpost_ref[...])
    xm_ref[0] = xm

    h2 = (_rms(xm) * gffn_ref[...] * (1.0 + mod[4:5]) + mod[3:4]).astype(BF16)
    h2_ref[0] = h2
    lane = lax.broadcasted_iota(I32, (tm, LANE), 1)
    lg = jnp.where(lane < N_EXPERTS, _dot(h2, wr_ref[...]), NEG)
    e = jnp.exp(lg - jnp.max(lg, axis=-1, keepdims=True))
    aff_ref[0] = (e / jnp.sum(e, axis=-1, keepdims=True))[:, :N_EXPERTS]
    lt = _dot_nt(wrt_ref[...], h2)
    et = jnp.exp(lt - jnp.max(lt, axis=0, keepdims=True))
    afft_ref[0] = et / jnp.sum(et, axis=0, keepdims=True)


def _merge(x, mod, ya, ml, lo, gl, gr, yd, lw, tm):
    b, t, _ = x.shape
    tok = lambda w: pl.BlockSpec((1, tm, w), lambda bi, i: (bi, i, 0))
    consts = [lw["g_mix_pre"], lw["w_gate"], lw["b_gate"], lw["wbr"], lw["w_out"], lw["g_mix_post"], lw["g_ffn_pre"],
              lw["w_router"], lw["w_router_t"], lw["g_mlstm_out"], lw["g_gla_out"]]
    in_specs = [tok(D), pl.BlockSpec((1, 8, D), lambda bi, i: (bi, 0, 0)), tok(512), tok(256), tok(256), tok(256),
                tok(256), tok(256), tok(256), tok(512)] + [_const_spec(c.shape) for c in consts]
    out_specs = [tok(D), tok(D), tok(N_EXPERTS), pl.BlockSpec((1, N_EXPERTS, tm), lambda bi, i: (bi, 0, i))]
    out_shape = [jax.ShapeDtypeStruct((b, t, D), F32), jax.ShapeDtypeStruct((b, t, D), BF16),
                 jax.ShapeDtypeStruct((b, t, N_EXPERTS), F32), jax.ShapeDtypeStruct((b, N_EXPERTS, t), F32)]
    return pl.pallas_call(
        _merge_kernel, name="merge", grid=(b, t // tm), in_specs=in_specs, out_specs=out_specs, out_shape=out_shape,
        compiler_params=_cparams(("parallel", "arbitrary")),
    )(x, mod, ya, ml[0], ml[1], lo, gl[0], gl[1], gr, yd, *consts)


def _topk_kernel(a_ref, pos_ref, s0_ref, *, cap):
    nblk = a_ref.shape[1]
    bits = pltpu.bitcast(a_ref[0], I32)

    def bisect(i, thr):
        cand = thr | (1 << (30 - i))
        cnt = jnp.sum((bits >= cand).astype(I32), axis=(0, 2), keepdims=True)
        return jnp.where(cnt >= cap, cand, thr)

    thr3 = lax.fori_loop(0, 31, bisect, jnp.zeros((1, N_EXPERTS, 1), I32))
    need3 = cap - jnp.sum((bits > thr3).astype(I32), axis=(0, 2), keepdims=True)
    thr, need = thr3[0], need3[0].astype(F32)
    upper = (lax.broadcasted_iota(I32, (TOK_BLK, TOK_BLK), 0) <= lax.broadcasted_iota(I32, (TOK_BLK, TOK_BLK), 1)).astype(BF16)

    def blk(j, carry):
        c_eq, c_sel = carry
        bj = pltpu.bitcast(a_ref[0, j], I32)
        gt, eq = bj > thr, bj == thr
        cum_eq = _dot(eq.astype(BF16), upper) + c_eq
        sel = gt | (eq & (cum_eq <= need))
        cum_sel = _dot(sel.astype(BF16), upper) + c_sel
        pos_ref[0, j] = jnp.where(sel, cum_sel - 1.0, -1.0).astype(I32)
        s0_ref[0, j] = jnp.broadcast_to(c_sel, (N_EXPERTS, LANE)).astype(I32)
        return cum_eq[:, TOK_BLK - 1:TOK_BLK], cum_sel[:, TOK_BLK - 1:TOK_BLK]

    zero = jnp.zeros((N_EXPERTS, 1), F32)
    lax.fori_loop(0, nblk, blk, (zero, zero))


def _topk(aff_t, cap):
    b, _, t = aff_t.shape
    nblk = t // TOK_BLK
    a4 = aff_t.reshape(b, N_EXPERTS, nblk, TOK_BLK).transpose(0, 2, 1, 3)
    spec = lambda w: pl.BlockSpec((1, nblk, N_EXPERTS, w), lambda bi: (bi, 0, 0, 0))
    return pl.pallas_call(
        functools.partial(_topk_kernel, cap=cap), name="topk", grid=(b,),
        in_specs=[spec(TOK_BLK)], out_specs=[spec(TOK_BLK), spec(LANE)],
        out_shape=[jax.ShapeDtypeStruct((b, nblk, N_EXPERTS, TOK_BLK), I32),
                   jax.ShapeDtypeStruct((b, nblk, N_EXPERTS, LANE), I32)],
        compiler_params=_cparams(("parallel",)),
    )(a4)


def _moe_kernel(s0_ref, pos_ref, h_ref, wg_ref, wu_ref, wd_ref, ys_ref, xs_ref, *, nbatch, nblk, nsub, rows_step):
    q, tb = pl.program_id(0), pl.program_id(1)
    cur = q % 2

    @pl.when(q < N_EXPERTS * nbatch)
    def _():
        e, bi = q // nbatch, q % nbatch

        @pl.when(tb == 0)
        def _():
            xs_ref[cur] = jnp.zeros(xs_ref.shape[1:], BF16)

        base = (bi * N_EXPERTS + e) * (nblk + 1) + tb * nsub
        s0s = [s0_ref[base + sb] for sb in range(nsub + 1)]
        a0s = [pl.multiple_of((s0 // 16) * 16, 16) for s0 in s0s[:-1]]
        spans = [s0s[sb + 1] - a0s[sb] for sb in range(nsub)]

        def gather(win, sb):
            a0 = a0s[sb]
            prow = pos_ref[0, sb, pl.ds(e, 1), :]
            slot = lax.broadcasted_iota(I32, (win, TOK_BLK), 0) + a0
            rows = _dot((slot == prow).astype(BF16), h_ref[0, sb * TOK_BLK:(sb + 1) * TOK_BLK, :])
            xs_ref[cur, pl.ds(a0, win), :] = xs_ref[cur, pl.ds(a0, win), :] + rows.astype(BF16)

        all_small = functools.reduce(jnp.logical_and, [sp <= GATHER_WIN_SMALL for sp in spans])

        @pl.when(all_small)
        def _():
            for sb in range(nsub):
                gather(GATHER_WIN_SMALL, sb)

        @pl.when(jnp.logical_not(all_small))
        def _():
            for sb in range(nsub):
                nonempty = s0s[sb + 1] > s0s[sb]
                pl.when(nonempty & (spans[sb] <= GATHER_WIN_SMALL))(functools.partial(gather, GATHER_WIN_SMALL, sb))
                pl.when(spans[sb] > GATHER_WIN_SMALL)(functools.partial(gather, GATHER_WIN, sb))

    @pl.when(q > 0)
    def _():
        rows = min(TOK_BLK, rows_step)
        for c in range(rows_step // rows):
            off = pl.multiple_of(tb * rows_step + c * rows, rows)
            xc = xs_ref[1 - cur, pl.ds(off, rows), :]
            hg = _dot(xc, wg_ref[0, 0])
            hid = (hg * _sigmoid(hg) * _dot(xc, wu_ref[0, 0])).astype(BF16)
            ys_ref[0, 0, pl.ds(off, rows), :] = _dot(hid, wd_ref[0, 0]).astype(BF16)


def _moe(s0_flat, pos4, h2, lw, capp):
    b, t, _ = h2.shape
    layer = lw["layer"]
    nblk = t // TOK_BLK
    tok = min(MOE_TOK, t)
    nsub = tok // TOK_BLK
    nstep = t // tok
    rows_step = capp // nstep
    assert rows_step % min(TOK_BLK, rows_step) == 0 and rows_step % 16 == 0
    npair = N_EXPERTS * b
    gat = lambda q: jnp.minimum(q, npair - 1)
    ffn = lambda q: jnp.maximum(q - 1, 0)
    grid_spec = pltpu.PrefetchScalarGridSpec(
        num_scalar_prefetch=1, grid=(npair + 1, nstep),
        in_specs=[pl.BlockSpec((1, nsub, N_EXPERTS, TOK_BLK), lambda q, tb, s: (gat(q) % b, tb, 0, 0)),
                  pl.BlockSpec((1, tok, D), lambda q, tb, s: (gat(q) % b, tb, 0)),
                  pl.BlockSpec((1, 1, D, EXPERT_FF), lambda q, tb, s: (layer, ffn(q) // b, 0, 0)),
                  pl.BlockSpec((1, 1, D, EXPERT_FF), lambda q, tb, s: (layer, ffn(q) // b, 0, 0)),
                  pl.BlockSpec((1, 1, EXPERT_FF, D), lambda q, tb, s: (layer, ffn(q) // b, 0, 0))],
        out_specs=pl.BlockSpec((1, 1, capp, D), lambda q, tb, s: (ffn(q) % b, ffn(q) // b, 0, 0)),
        scratch_shapes=[pltpu.VMEM((2, capp + GATHER_WIN, D), BF16)])
    return pl.pallas_call(
        functools.partial(_moe_kernel, nbatch=b, nblk=nblk, nsub=nsub, rows_step=rows_step), name="moe",
        grid_spec=grid_spec, out_shape=jax.ShapeDtypeStruct((b, N_EXPERTS, capp, D), BF16),
        compiler_params=pltpu.CompilerParams(dimension_semantics=("arbitrary", "arbitrary"), vmem_limit_bytes=VMEM_LIMIT,
                                             allow_input_fusion=[False, False, False, True, True, True]),
    )(s0_flat, pos4, h2, lw["w_e_gate"], lw["w_e_up"], lw["w_e_down"])


def _combine_kernel(s0_ref, *refs, nblk, nb, sblk):
    ys_refs = refs[:2 * N_EXPERTS]
    pos_ref, aff_ref, xm_ref, mod_ref, g_ref, o_ref = refs[2 * N_EXPERTS:]
    bi, tb = pl.program_id(0), pl.program_id(1)
    pos = pos_ref[0]
    aff = aff_ref[0]
    lane = lax.broadcasted_iota(I32, (TOK_BLK, 2 * sblk), 1)
    acc = jnp.zeros((TOK_BLK, D), F32)
    for e in range(N_EXPERTS):
        s0 = s0_ref[(bi * N_EXPERTS + e) * (nblk + 1) + tb]
        blk0 = jnp.minimum(s0 // sblk, nb - 1)
        rel = pos[:, e:e + 1] - blk0 * sblk
        ysw = jnp.concatenate([ys_refs[2 * e][0, 0], ys_refs[2 * e + 1][0, 0]], axis=0)
        acc = acc + aff[:, e:e + 1] * _dot((lane == rel).astype(BF16), ysw)
    mod = mod_ref[0]
    o_ref[0] = xm_ref[0] + mod[5:6] * (_rms(acc) * g_ref[...])


def _combine(s0_flat, ys, pos_t, aff, xm, mod, g_post, sblk):
    b, t, _ = xm.shape
    nblk = t // TOK_BLK
    nb = ys.shape[2] // sblk

    def ys_spec(e, k):
        def ix(bi, tb, s):
            blk0 = jnp.minimum(s[(bi * N_EXPERTS + e) * (nblk + 1) + tb] // sblk, nb - 1)
            return (bi, e, jnp.minimum(blk0 + k, nb - 1), 0)
        return pl.BlockSpec((1, 1, sblk, D), ix)

    tok = lambda w: pl.BlockSpec((1, TOK_BLK, w), lambda bi, tb, s: (bi, tb, 0))
    in_specs = [ys_spec(e, k) for e in range(N_EXPERTS) for k in range(2)]
    in_specs += [tok(N_EXPERTS), tok(N_EXPERTS), tok(D), pl.BlockSpec((1, 8, D), lambda bi, tb, s: (bi, 0, 0)),
                 pl.BlockSpec((1, D), lambda bi, tb, s: (0, 0))]
    grid_spec = pltpu.PrefetchScalarGridSpec(num_scalar_prefetch=1, grid=(b, nblk), in_specs=in_specs, out_specs=tok(D))
    return pl.pallas_call(
        functools.partial(_combine_kernel, nblk=nblk, nb=nb, sblk=sblk), name="combine", grid_spec=grid_spec,
        out_shape=jax.ShapeDtypeStruct((b, t, D), F32),
        compiler_params=_cparams(("arbitrary", "arbitrary")),
    )(s0_flat, *([ys] * (2 * N_EXPERTS)), pos_t, aff, xm, mod, g_post)


def _rope_table(t):
    nf = ROPE_DIM // 4
    pos = jnp.arange(t)
    inv = ROPE_BASE ** (-jnp.arange(nf, dtype=F32) / nf)
    ang = jnp.stack([pos // GRID_W, pos % GRID_W], axis=-1).astype(F32)[..., None] * inv
    cos, sin = jnp.cos(ang), jnp.sin(ang)
    c32 = jnp.stack([cos, cos], axis=2).reshape(t, ROPE_DIM)
    s32 = jnp.stack([-sin, sin], axis=2).reshape(t, ROPE_DIM)
    one, zero = jnp.ones((t, 64), F32), jnp.zeros((t, 32), F32)
    ct = jnp.concatenate([one, c32, zero], axis=1)
    st = jnp.concatenate([0.0 * one, s32, zero], axis=1)
    return jnp.concatenate([ct, st, jnp.tile(c32, (1, 4)), jnp.tile(s32, (1, 4))], axis=1)


def _identity_table(t):
    one, zero = jnp.ones((t, 128), F32), jnp.zeros((t, 128), F32)
    ct = jnp.concatenate([jnp.ones((t, 96), F32), jnp.zeros((t, 32), F32)], axis=1)
    return jnp.concatenate([ct, zero, one, zero], axis=1)


def _layer_weights(i, p):
    lw = {}
    row = lambda a: a.reshape(1, -1)
    for name in ("g_mix_pre", "g_mix_post", "g_ffn_pre", "g_ffn_post", "g_q_lat", "g_kv_lat", "g_mlstm_out", "g_gla_out"):
        lw[name] = row(p[name][i])
    lw["w_ext"] = _gather_cols(p["w_in"][i], _WIN_IDX).astype(BF16)
    lw["wgt"] = p["w_in"][i][:, 416 + 1024:416 + 1040].T.astype(BF16)
    gb = jnp.concatenate([p["b_igate"][i].reshape(-1), p["b_fgate"][i].reshape(-1)])
    lw["gate_bias_row"] = jnp.pad(gb, (0, LANE - 16)).reshape(1, LANE)
    lw["gate_bias_col"] = gb.reshape(16, 1)
    wuq = p["w_uq"][i]
    qi = -np.ones((512,), np.int64)
    qsi = -np.ones((512,), np.int64)
    for h in range(N_HEADS):
        qi[128 * h:128 * h + 96] = 96 * h + np.arange(96)
        qsi[128 * h + 64:128 * h + 96] = 96 * h + 64 + (np.arange(32) ^ 8)
    lw["wq"] = _gather_cols(wuq, qi).astype(BF16)
    lw["wqs"] = _gather_cols(wuq, qsi).astype(BF16)
    ki = -np.ones((512,), np.int64)
    for h in range(N_HEADS):
        ki[128 * h:128 * h + 64] = 128 * h + np.arange(64)
    lw["wk"] = _gather_cols(p["w_ukv"][i], ki).astype(BF16)
    lw["wvt"] = _vt_rows(p["w_ukv"][i].reshape(MLA_KV_LORA, N_HEADS, 128)[:, :, 64:].reshape(MLA_KV_LORA, 256)).astype(BF16)
    lw["wdvt"] = _vt_rows(p["w_in"][i][:, 2768:3024]).astype(BF16)
    wa = p["w_alpha2"][i]
    wal = jnp.zeros((LANE, 256), F32).at[0:16, 0:128].set(wa[0]).at[16:32, 128:256].set(wa[1])
    lw["walpha"] = wal.astype(BF16)
    lw["balpha"] = p["b_alpha"][i].reshape(1, 256)
    lw["w_conv"] = p["w_conv"][i]
    lw["b_conv"] = row(p["b_conv"][i])
    lw["dlam"] = p["diff_lambda"][i]
    lw["g_diff"] = jnp.broadcast_to(p["g_diff_out"][i].reshape(N_HEADS, DIFF_DV, 1), (N_HEADS, DIFF_DV, LANE))
    wb = p["w_branch"][i]
    lw["wbr"] = jnp.concatenate([_pad_heads_rows(wb[0], 64), wb[1], wb[2], _pad_heads_rows(wb[3], 64)], axis=0).astype(BF16)
    lw["w_gate"] = p["w_gate"][i].astype(BF16)
    lw["b_gate"] = row(p["b_gate"][i])
    lw["w_out"] = p["w_out"][i].astype(BF16)
    lw["w_router"] = jnp.pad(p["w_router"][i], ((0, 0), (0, LANE - N_EXPERTS))).astype(BF16)
    lw["w_router_t"] = p["w_router"][i].T.astype(BF16)
    lw["layer"] = i
    lw["w_e_gate"], lw["w_e_up"], lw["w_e_down"] = p["w_e_gate"], p["w_e_up"], p["w_e_down"]
    return lw


def _ffn(xm, h2, aff, aff_t, mod, lw):
    b, t, _ = xm.shape
    nblk = t // TOK_BLK
    cap = EC_CAPACITY * t // N_EXPERTS
    capp = -(-cap // TOK_BLK) * TOK_BLK
    pos4, s04 = _topk(aff_t, cap)
    s0_be = jnp.concatenate([s04[..., 0].transpose(0, 2, 1), jnp.full((b, N_EXPERTS, 1), cap, I32)], axis=-1)
    s0_flat = s0_be.reshape(-1)
    ys = _moe(s0_flat, pos4, h2, lw, capp)
    pos_t = pos4.transpose(0, 1, 3, 2).reshape(b, t, N_EXPERTS)
    first = jnp.minimum(s0_be[..., :-1] // SLOT_BLK, capp // SLOT_BLK - 1)
    fits = jnp.all(s0_be[..., 1:] <= (first + 2) * SLOT_BLK)
    args = (s0_flat, ys, pos_t, aff, xm, mod, lw["g_ffn_post"])
    return lax.cond(fits, functools.partial(_combine, sblk=SLOT_BLK), functools.partial(_combine, sblk=TOK_BLK), *args)


def _hybrid_layer(i, x_c, x_l, c8, need_ctx, p):
    lw = _layer_weights(i, p)
    b, t, _ = x_l.shape
    tc = x_c.shape[1]
    lam_init = 0.8 - 0.6 * math.exp(-0.3 * i)
    mod8 = _ada(c8, p["w_ada"][i], p["b_ada"][i])
    pad = lambda m: jnp.pad(m.reshape(b, 6, D), ((0, 0), (0, 2), (0, 0)))
    mod_l = pad(mod8[:b])
    mod_c = pad(jnp.broadcast_to(mod8[b:b + 1], (b, 6 * D)))

    pc = _proj(x_c, mod_c, _identity_table(tc), lw)
    pt = _proj(x_l, mod_l, _rope_table(t), lw)

    zc = jnp.zeros((b, 2, 256, 512), F32)
    zm = jnp.full((b, 2, 8, 256), NEG, F32)
    zs = jnp.zeros((b, 2, 256, 128), F32)
    hf_c, hb_c, c_fin, m_fin = _mlstm(pc, lw["w_conv"], lw["b_conv"], zc, zm)
    hf_l, hb_l, _, _ = _mlstm(pt, lw["w_conv"], lw["b_conv"], c_fin, m_fin)
    gf_c, gb_c, s_fin = _gla(pc, zs)
    gf_l, gb_l, _ = _gla(pt, s_fin)

    one_g = jnp.ones((N_HEADS, MLA_V, LANE), F32)
    fl = functools.partial(_flash, tk=FLASH_KEYS)
    ya_l = fl(pt["mqt"], pc["mk"], pc["mvt"], pt["mk"], pt["mvt"], lw["dlam"], one_g, nmap=1, finish=False, post=1.0,
              tq=min(FLASH_ROWS, t))
    yd_l = fl(pt["dqt"], pc["dk"], pc["dvt"], pt["dk"], pt["dvt"], lw["dlam"], lw["g_diff"], nmap=2, finish=True,
              post=1.0 - lam_init, tq=min(FLASH_ROWS // 2, t))
    xm, h2, aff, aff_t = _merge(x_l, mod_l, ya_l, (hf_l, hb_l), pt["lo"], (gf_l, gb_l), pt["gr"], yd_l, lw, tm=min(512, t))
    x_l = _ffn(xm, h2, aff, aff_t, mod_l, lw)

    if need_ctx:
        ya_c = fl(pc["mqt"], pc["mk"], pc["mvt"], None, None, lw["dlam"], one_g, nmap=1, finish=False, post=1.0, tq=tc)
        yd_c = fl(pc["dqt"], pc["dk"], pc["dvt"], None, None, lw["dlam"], lw["g_diff"], nmap=2, finish=True,
                  post=1.0 - lam_init, tq=tc)
        xm, h2, aff, aff_t = _merge(x_c, mod_c, ya_c, (hf_c, hb_c), pc["lo"], (gf_c, gb_c), pc["gr"], yd_c, lw, tm=tc)
        x_c = _ffn(xm, h2, aff, aff_t, mod_c, lw)
    return x_c, x_l


def kernel(x, c, ctx, c_ctx, w_ada, b_ada, g_mix_pre, g_mix_post, g_ffn_pre, g_ffn_post, w_in, g_q_lat, w_uq, g_kv_lat, w_ukv, w_conv, b_conv, b_igate, b_fgate, g_mlstm_out, w_alpha2, b_alpha, g_gla_out, diff_lambda, g_diff_out, w_branch, w_gate, b_gate, w_out, w_router, w_e_gate, w_e_up, w_e_down):
    p = dict(w_ada=w_ada, b_ada=b_ada, g_mix_pre=g_mix_pre, g_mix_post=g_mix_post, g_ffn_pre=g_ffn_pre,
             g_ffn_post=g_ffn_post, w_in=w_in, g_q_lat=g_q_lat, w_uq=w_uq, g_kv_lat=g_kv_lat, w_ukv=w_ukv,
             w_conv=w_conv, b_conv=b_conv, b_igate=b_igate, b_fgate=b_fgate, g_mlstm_out=g_mlstm_out,
             w_alpha2=w_alpha2, b_alpha=b_alpha, g_gla_out=g_gla_out, diff_lambda=diff_lambda, g_diff_out=g_diff_out,
             w_branch=w_branch, w_gate=w_gate, b_gate=b_gate, w_out=w_out, w_router=w_router,
             w_e_gate=w_e_gate.astype(BF16), w_e_up=w_e_up.astype(BF16), w_e_down=w_e_down.astype(BF16))
    b = x.shape[0]
    c8 = jnp.concatenate([c, c_ctx[None], jnp.zeros((8 - b - 1, D), F32)], axis=0)
    x_c, x_l = ctx, x
    for i in range(DEPTH):
        x_c, x_l = _hybrid_layer(i, x_c, x_l, c8, i < DEPTH - 1, p)
    return x_l
```
